```python
import jax, jax.numpy as jnp
from jax import lax
import numpy as np

D_MODEL = 2048
BATCH = 8
SEQ = 2048
DEPTH = 2

HEAD_DIM = 64
N_Q_HEADS = 16
N_KV_HEADS = 4
Q_PER_KV = N_Q_HEADS // N_KV_HEADS
ATTN_WIDTH = N_Q_HEADS * HEAD_DIM
KV_WIDTH = N_KV_HEADS * HEAD_DIM
WINDOW = 128
BLOCK = 128
ROPE_THETA = 10000.0
SGU_GROUPS = 8
SGU_GROUP_DIM = 128
SGU_WIDTH = SGU_GROUPS * SGU_GROUP_DIM
CHUNK = 128
N_BRANCHES = 2
D_FF = ((-(-8 * D_MODEL // 3) + 255) // 256) * 256
IN_WIDTH = ATTN_WIDTH + 2 * KV_WIDTH + 2 * SGU_WIDTH + N_BRANCHES * D_MODEL
EPS = 1e-6

kernel_name = "hybrid_gated_swa_sgu_block"


def rms_norm(x, g):
    xf = x.astype(jnp.float32)
    y = xf * lax.rsqrt(jnp.mean(xf * xf, axis=-1, keepdims=True) + EPS)
    return (y * g.astype(jnp.float32)).astype(x.dtype)


def rope_tables(seq):
    pos = jnp.arange(seq, dtype=jnp.float32)
    inv_freq = jnp.power(ROPE_THETA, -jnp.arange(0, HEAD_DIM, 2, dtype=jnp.float32) / HEAD_DIM)
    ang = pos[:, None] * inv_freq[None, :]
    return jnp.cos(ang), jnp.sin(ang)


def apply_rope(x, cos, sin):
    xf = x.astype(jnp.float32)
    half = HEAD_DIM // 2
    x1, x2 = xf[..., :half], xf[..., half:]
    c, s = cos[None, :, None, :], sin[None, :, None, :]
    return jnp.concatenate([x1 * c - x2 * s, x2 * c + x1 * s], axis=-1).astype(x.dtype)


def sliding_window_attention(q, k, v, sinks):
    B, S = q.shape[0], q.shape[1]
    nb = S // BLOCK
    qb = q.reshape(B, nb, BLOCK, N_KV_HEADS, Q_PER_KV, HEAD_DIM)
    kb = k.reshape(B, nb, BLOCK, N_KV_HEADS, HEAD_DIM)
    vb = v.reshape(B, nb, BLOCK, N_KV_HEADS, HEAD_DIM)

    def with_prev(t):
        prev = jnp.pad(t[:, :-1], ((0, 0), (1, 0), (0, 0), (0, 0), (0, 0)))
        return jnp.concatenate([prev, t], axis=2)

    kw, vw = with_prev(kb), with_prev(vb)
    scale = HEAD_DIM ** -0.5
    scores = jnp.einsum('bnqhgd,bnkhd->bnhgqk', qb, kw).astype(jnp.float32) * scale
    q_pos = jnp.arange(BLOCK)[:, None] + BLOCK
    k_pos = jnp.arange(2 * BLOCK)[None, :]
    diff = q_pos - k_pos
    band = (diff >= 0) & (diff < WINDOW)
    valid = (jnp.arange(nb)[:, None, None] > 0) | (k_pos >= BLOCK)[None]
    mask = (band[None] & valid)[None, :, None, None]
    scores = jnp.where(mask, scores, -1e30)
    sink = jnp.broadcast_to(
        sinks.astype(jnp.float32).reshape(N_KV_HEADS, Q_PER_KV)[None, None, :, :, None, None],
        scores.shape[:-1] + (1,))
    probs = jax.nn.softmax(jnp.concatenate([scores, sink], axis=-1), axis=-1)[..., :-1]
    out = jnp.einsum('bnhgqk,bnkhd->bnqhgd', probs.astype(v.dtype), vw)
    return out.reshape(B, S, ATTN_WIDTH)


def chunked_sgu(uv, w_s, b_s, ln_g, ln_b):
    B, S = uv.shape[0], uv.shape[1]
    nc = S // CHUNK
    u, v = uv[..., :SGU_WIDTH], uv[..., SGU_WIDTH:]
    vf = v.astype(jnp.float32).reshape(B, S, SGU_GROUPS, SGU_GROUP_DIM)
    mu = jnp.mean(vf, axis=-1, keepdims=True)
    var = jnp.mean(jnp.square(vf - mu), axis=-1, keepdims=True)
    vn = ((vf - mu) * lax.rsqrt(var + EPS) * ln_g.reshape(SGU_GROUPS, SGU_GROUP_DIM)
          + ln_b.reshape(SGU_GROUPS, SGU_GROUP_DIM)).astype(v.dtype)
    vc = vn.reshape(B, nc, CHUNK, SGU_GROUPS, SGU_GROUP_DIM)
    tri = jnp.tril(jnp.ones((CHUNK, CHUNK), dtype=bool))
    w = jnp.where(tri[None], w_s, jnp.zeros_like(w_s))
    s = jnp.einsum('gij,bnjgd->bnigd', w, vc) + jnp.transpose(b_s)[None, None, :, :, None]
    return u * s.reshape(B, S, SGU_WIDTH)


def _fwd_setup_inputs(seed: int = 0) -> dict:
    key = jax.random.key(seed)
    ks = jax.random.split(key, 17)
    D = D_MODEL

    def nrm(k, shape, scale):
        return jax.random.normal(k, shape, jnp.float32) * scale

    return {
        "x": nrm(ks[0], (BATCH, SEQ, D), 1.0),
        "mix_norm": 1.0 + nrm(ks[1], (DEPTH, D), 0.02),
        "w_in": nrm(ks[2], (DEPTH, D, IN_WIDTH), D ** -0.5),
        "q_norm": 1.0 + nrm(ks[3], (DEPTH, HEAD_DIM), 0.02),
        "k_norm": 1.0 + nrm(ks[4], (DEPTH, HEAD_DIM), 0.02),
        "sinks": nrm(ks[5], (DEPTH, N_Q_HEADS), 0.5),
        "sgu_ln_g": 1.0 + nrm(ks[6], (DEPTH, SGU_WIDTH), 0.02),
        "sgu_ln_b": nrm(ks[7], (DEPTH, SGU_WIDTH), 0.02),
        "w_spatial": nrm(ks[8], (DEPTH, SGU_GROUPS, CHUNK, CHUNK), 0.5 * CHUNK ** -0.5),
        "b_spatial": 1.0 + nrm(ks[9], (DEPTH, SGU_GROUPS, CHUNK), 0.02),
        "w_attn_branch": nrm(ks[10], (DEPTH, ATTN_WIDTH, D), ATTN_WIDTH ** -0.5),
        "w_sgu_branch": nrm(ks[11], (DEPTH, SGU_WIDTH, D), SGU_WIDTH ** -0.5),
        "w_out": nrm(ks[12], (DEPTH, D, D), D ** -0.5),
        "ffn_norm": 1.0 + nrm(ks[13], (DEPTH, D), 0.02),
        "w_gate": nrm(ks[14], (DEPTH, D, D_FF), D ** -0.5),
        "w_up": nrm(ks[15], (DEPTH, D, D_FF), D ** -0.5),
        "w_down": nrm(ks[16], (DEPTH, D_FF, D), D_FF ** -0.5),
    }


def _fwd_reference(x, mix_norm, w_in, q_norm, k_norm, sinks, sgu_ln_g, sgu_ln_b, w_spatial,
              b_spatial, w_attn_branch, w_sgu_branch, w_out, ffn_norm, w_gate, w_up, w_down):
    B, S = x.shape[0], x.shape[1]
    cos, sin = rope_tables(S)
    cuts = [ATTN_WIDTH, ATTN_WIDTH + KV_WIDTH, ATTN_WIDTH + 2 * KV_WIDTH,
            ATTN_WIDTH + 2 * KV_WIDTH + 2 * SGU_WIDTH]
    for l in range(DEPTH):
        h = rms_norm(x, mix_norm[l])
        proj = h @ w_in[l]
        q, k, v, uv, gate_logits = jnp.split(proj, cuts, axis=-1)
        q = apply_rope(rms_norm(q.reshape(B, S, N_Q_HEADS, HEAD_DIM), q_norm[l]), cos, sin)
        k = apply_rope(rms_norm(k.reshape(B, S, N_KV_HEADS, HEAD_DIM), k_norm[l]), cos, sin)
        v = v.reshape(B, S, N_KV_HEADS, HEAD_DIM)
        branch_a = sliding_window_attention(q, k, v, sinks[l]) @ w_attn_branch[l]
        branch_b = chunked_sgu(jax.nn.gelu(uv), w_spatial[l], b_spatial[l],
                               sgu_ln_g[l], sgu_ln_b[l]) @ w_sgu_branch[l]
        gates = jax.nn.sigmoid(gate_logits)
        merged = gates[..., :D_MODEL] * branch_a + gates[..., D_MODEL:] * branch_b
        x = x + merged @ w_out[l]
        h2 = rms_norm(x, ffn_norm[l])
        x = x + (jax.nn.silu(h2 @ w_gate[l]) * (h2 @ w_up[l])) @ w_down[l]
    return x


import jax as _jax
import jax.numpy as _jnp

TWIN_FORMAT = 'train_step'
FWD_PARAMS = ['x', 'mix_norm', 'w_in', 'q_norm', 'k_norm', 'sinks', 'sgu_ln_g', 'sgu_ln_b', 'w_spatial', 'b_spatial', 'w_attn_branch', 'w_sgu_branch', 'w_out', 'ffn_norm', 'w_gate', 'w_up', 'w_down']
TWIN_WEIGHTS = ['mix_norm', 'w_in', 'q_norm', 'k_norm', 'sinks', 'sgu_ln_g', 'sgu_ln_b', 'w_spatial', 'b_spatial', 'w_attn_branch', 'w_sgu_branch', 'w_out', 'ffn_norm', 'w_gate', 'w_up', 'w_down']
TWIN_DIFF_INPUT = 'x'
TWIN_INPUTS = ['x', 'mix_norm', 'w_in', 'q_norm', 'k_norm', 'sinks', 'sgu_ln_g', 'sgu_ln_b', 'w_spatial', 'b_spatial', 'w_attn_branch', 'w_sgu_branch', 'w_out', 'ffn_norm', 'w_gate', 'w_up', 'w_down', 'loss_target', 'm_mix_norm', 'm_w_in', 'm_q_norm', 'm_k_norm', 'm_sinks', 'm_sgu_ln_g', 'm_sgu_ln_b', 'm_w_spatial', 'm_b_spatial', 'm_w_attn_branch', 'm_w_sgu_branch', 'm_w_out', 'm_ffn_norm', 'm_w_gate', 'm_w_up', 'm_w_down', 'v_mix_norm', 'v_w_in', 'v_q_norm', 'v_k_norm', 'v_sinks', 'v_sgu_ln_g', 'v_sgu_ln_b', 'v_w_spatial', 'v_b_spatial', 'v_w_attn_branch', 'v_w_sgu_branch', 'v_w_out', 'v_ffn_norm', 'v_w_gate', 'v_w_up', 'v_w_down']
TWIN_OUTPUTS = ['loss', 'grad_x', 'grad_mix_norm', 'grad_w_in', 'grad_q_norm', 'grad_k_norm', 'grad_sinks', 'grad_sgu_ln_g', 'grad_sgu_ln_b', 'grad_w_spatial', 'grad_b_spatial', 'grad_w_attn_branch', 'grad_w_sgu_branch', 'grad_w_out', 'grad_ffn_norm', 'grad_w_gate', 'grad_w_up', 'grad_w_down', 'delta_mix_norm', 'delta_w_in', 'delta_q_norm', 'delta_k_norm', 'delta_sinks', 'delta_sgu_ln_g', 'delta_sgu_ln_b', 'delta_w_spatial', 'delta_b_spatial', 'delta_w_attn_branch', 'delta_w_sgu_branch', 'delta_w_out', 'delta_ffn_norm', 'delta_w_gate', 'delta_w_up', 'delta_w_down', 'new_m_mix_norm', 'new_m_w_in', 'new_m_q_norm', 'new_m_k_norm', 'new_m_sinks', 'new_m_sgu_ln_g', 'new_m_sgu_ln_b', 'new_m_w_spatial', 'new_m_b_spatial', 'new_m_w_attn_branch', 'new_m_w_sgu_branch', 'new_m_w_out', 'new_m_ffn_norm', 'new_m_w_gate', 'new_m_w_up', 'new_m_w_down', 'new_v_mix_norm', 'new_v_w_in', 'new_v_q_norm', 'new_v_k_norm', 'new_v_sinks', 'new_v_sgu_ln_g', 'new_v_sgu_ln_b', 'new_v_w_spatial', 'new_v_b_spatial', 'new_v_w_attn_branch', 'new_v_w_sgu_branch', 'new_v_w_out', 'new_v_ffn_norm', 'new_v_w_gate', 'new_v_w_up', 'new_v_w_down']
TWIN_LEAF_KINDS = {'loss': 'loss', 'grad_x': 'grad_x', 'grad_mix_norm': 'grad_w', 'grad_w_in': 'grad_w', 'grad_q_norm': 'grad_w', 'grad_k_norm': 'grad_w', 'grad_sinks': 'grad_w', 'grad_sgu_ln_g': 'grad_w', 'grad_sgu_ln_b': 'grad_w', 'grad_w_spatial': 'grad_w', 'grad_b_spatial': 'grad_w', 'grad_w_attn_branch': 'grad_w', 'grad_w_sgu_branch': 'grad_w', 'grad_w_out': 'grad_w', 'grad_ffn_norm': 'grad_w', 'grad_w_gate': 'grad_w', 'grad_w_up': 'grad_w', 'grad_w_down': 'grad_w', 'delta_mix_norm': 'delta_w', 'delta_w_in': 'delta_w', 'delta_q_norm': 'delta_w', 'delta_k_norm': 'delta_w', 'delta_sinks': 'delta_w', 'delta_sgu_ln_g': 'delta_w', 'delta_sgu_ln_b': 'delta_w', 'delta_w_spatial': 'delta_w', 'delta_b_spatial': 'delta_w', 'delta_w_attn_branch': 'delta_w', 'delta_w_sgu_branch': 'delta_w', 'delta_w_out': 'delta_w', 'delta_ffn_norm': 'delta_w', 'delta_w_gate': 'delta_w', 'delta_w_up': 'delta_w', 'delta_w_down': 'delta_w', 'new_m_mix_norm': 'new_m', 'new_m_w_in': 'new_m', 'new_m_q_norm': 'new_m', 'new_m_k_norm': 'new_m', 'new_m_sinks': 'new_m', 'new_m_sgu_ln_g': 'new_m', 'new_m_sgu_ln_b': 'new_m', 'new_m_w_spatial': 'new_m', 'new_m_b_spatial': 'new_m', 'new_m_w_attn_branch': 'new_m', 'new_m_w_sgu_branch': 'new_m', 'new_m_w_out': 'new_m', 'new_m_ffn_norm': 'new_m', 'new_m_w_gate': 'new_m', 'new_m_w_up': 'new_m', 'new_m_w_down': 'new_m', 'new_v_mix_norm': 'new_v', 'new_v_w_in': 'new_v', 'new_v_q_norm': 'new_v', 'new_v_k_norm': 'new_v', 'new_v_sinks': 'new_v', 'new_v_sgu_ln_g': 'new_v', 'new_v_sgu_ln_b': 'new_v', 'new_v_w_spatial': 'new_v', 'new_v_b_spatial': 'new_v', 'new_v_w_attn_branch': 'new_v', 'new_v_w_sgu_branch': 'new_v', 'new_v_w_out': 'new_v', 'new_v_ffn_norm': 'new_v', 'new_v_w_gate': 'new_v', 'new_v_w_up': 'new_v', 'new_v_w_down': 'new_v'}


def _forward(args):
    return _fwd_reference(*[args[k] for k in FWD_PARAMS])


def _output_shape():
    out = _jax.eval_shape(lambda: _forward(_fwd_setup_inputs(0)))
    return out.shape, out.dtype

N_MICROBATCH = 1
ADAM_LR = 0.001
ADAM_B1 = 0.9
ADAM_B2 = 0.999
ADAM_EPS = 1e-08
ADAM_WD = 0.01
ADAM_STEP = 10
PER_EXAMPLE_BATCH_AXIS = {'x': 0, 'loss_target': 0}
SHARED_INPUTS = []
_WEIGHT_DTYPES = {'mix_norm': _jnp.float32, 'w_in': _jnp.float32, 'q_norm': _jnp.float32, 'k_norm': _jnp.float32, 'sinks': _jnp.float32, 'sgu_ln_g': _jnp.float32, 'sgu_ln_b': _jnp.float32, 'w_spatial': _jnp.float32, 'b_spatial': _jnp.float32, 'w_attn_branch': _jnp.float32, 'w_sgu_branch': _jnp.float32, 'w_out': _jnp.float32, 'ffn_norm': _jnp.float32, 'w_gate': _jnp.float32, 'w_up': _jnp.float32, 'w_down': _jnp.float32}
MOMENT_SCALE = {'mix_norm': 1.491189e+00, 'w_in': 8.830164e-02, 'q_norm': 9.797635e-01, 'k_norm': 9.782523e-01, 'sinks': 1.765900e-01, 'sgu_ln_g': 2.405087e-01, 'sgu_ln_b': 5.225274e-02, 'w_spatial': 8.093389e-02, 'b_spatial': 2.037758e+00, 'w_attn_branch': 9.319103e-02, 'w_sgu_branch': 4.267795e-01, 'w_out': 3.821591e-01, 'ffn_norm': 6.151637e+00, 'w_gate': 1.161707e-01, 'w_up': 7.140250e-02, 'w_down': 1.077564e-01}


def _to_microbatches(a, axis):
    t = _jnp.moveaxis(a, axis, 0)
    t = t.reshape((N_MICROBATCH, t.shape[0] // N_MICROBATCH) + t.shape[1:])
    return _jnp.moveaxis(t, 1, axis + 1)


def setup_inputs(seed: int = 0) -> dict:
    inp = _fwd_setup_inputs(seed)
    key = _jax.random.fold_in(_jax.random.key(seed), 7919)
    shape, _ = _output_shape()
    out = dict(inp)
    out["loss_target"] = _jax.random.normal(_jax.random.fold_in(key, 0), shape, _jnp.float32)
    for i, name in enumerate(TWIN_WEIGHTS):
        w = inp[name].astype(_jnp.float32)
        if MOMENT_SCALE is None:
            s = _jnp.sqrt(_jnp.mean(_jnp.square(w)) + 1e-30)
        else:
            s = MOMENT_SCALE[name]
        km, kv = _jax.random.split(_jax.random.fold_in(key, i + 1))
        out[name] = w
        out["m_" + name] = s * _jax.random.normal(km, w.shape, _jnp.float32)
        out["v_" + name] = (s * s) * _jax.random.uniform(kv, w.shape, _jnp.float32, 0.5, 1.5)
    if N_MICROBATCH > 1:
        for name, axis in PER_EXAMPLE_BATCH_AXIS.items():
            out[name] = _to_microbatches(out[name], axis)
    return {'x': out['x'], 'mix_norm': out['mix_norm'], 'w_in': out['w_in'], 'q_norm': out['q_norm'], 'k_norm': out['k_norm'], 'sinks': out['sinks'], 'sgu_ln_g': out['sgu_ln_g'], 'sgu_ln_b': out['sgu_ln_b'], 'w_spatial': out['w_spatial'], 'b_spatial': out['b_spatial'], 'w_attn_branch': out['w_attn_branch'], 'w_sgu_branch': out['w_sgu_branch'], 'w_out': out['w_out'], 'ffn_norm': out['ffn_norm'], 'w_gate': out['w_gate'], 'w_up': out['w_up'], 'w_down': out['w_down'], 'loss_target': out['loss_target'], 'm_mix_norm': out['m_mix_norm'], 'm_w_in': out['m_w_in'], 'm_q_norm': out['m_q_norm'], 'm_k_norm': out['m_k_norm'], 'm_sinks': out['m_sinks'], 'm_sgu_ln_g': out['m_sgu_ln_g'], 'm_sgu_ln_b': out['m_sgu_ln_b'], 'm_w_spatial': out['m_w_spatial'], 'm_b_spatial': out['m_b_spatial'], 'm_w_attn_branch': out['m_w_attn_branch'], 'm_w_sgu_branch': out['m_w_sgu_branch'], 'm_w_out': out['m_w_out'], 'm_ffn_norm': out['m_ffn_norm'], 'm_w_gate': out['m_w_gate'], 'm_w_up': out['m_w_up'], 'm_w_down': out['m_w_down'], 'v_mix_norm': out['v_mix_norm'], 'v_w_in': out['v_w_in'], 'v_q_norm': out['v_q_norm'], 'v_k_norm': out['v_k_norm'], 'v_sinks': out['v_sinks'], 'v_sgu_ln_g': out['v_sgu_ln_g'], 'v_sgu_ln_b': out['v_sgu_ln_b'], 'v_w_spatial': out['v_w_spatial'], 'v_b_spatial': out['v_b_spatial'], 'v_w_attn_branch': out['v_w_attn_branch'], 'v_w_sgu_branch': out['v_w_sgu_branch'], 'v_w_out': out['v_w_out'], 'v_ffn_norm': out['v_ffn_norm'], 'v_w_gate': out['v_w_gate'], 'v_w_up': out['v_w_up'], 'v_w_down': out['v_w_down']}


def _loss(weights, diff, rest, loss_target):
    with _jax.named_scope("forward"):
        args = {**rest, TWIN_DIFF_INPUT: diff, **{k: w.astype(_WEIGHT_DTYPES[k]) for k, w in weights.items()}}
        y = _forward(args)
    with _jax.named_scope("loss_head"):
        err = _jnp.square(y.astype(_jnp.float32) - loss_target)
        return 0.5 * _jnp.sum(_jnp.mean(err, axis=-1)) if err.ndim else 0.5 * err


def _adamw(w, g, m, v):
    m = ADAM_B1 * m + (1.0 - ADAM_B1) * g
    v = ADAM_B2 * v + (1.0 - ADAM_B2) * _jnp.square(g)
    m_hat = m / (1.0 - ADAM_B1 ** ADAM_STEP)
    v_hat = v / (1.0 - ADAM_B2 ** ADAM_STEP)
    delta = -ADAM_LR * (m_hat / (_jnp.sqrt(v_hat) + ADAM_EPS) + ADAM_WD * w)
    return delta, m, v


def reference(x, mix_norm, w_in, q_norm, k_norm, sinks, sgu_ln_g, sgu_ln_b, w_spatial, b_spatial, w_attn_branch, w_sgu_branch, w_out, ffn_norm, w_gate, w_up, w_down, loss_target, m_mix_norm, m_w_in, m_q_norm, m_k_norm, m_sinks, m_sgu_ln_g, m_sgu_ln_b, m_w_spatial, m_b_spatial, m_w_attn_branch, m_w_sgu_branch, m_w_out, m_ffn_norm, m_w_gate, m_w_up, m_w_down, v_mix_norm, v_w_in, v_q_norm, v_k_norm, v_sinks, v_sgu_ln_g, v_sgu_ln_b, v_w_spatial, v_b_spatial, v_w_attn_branch, v_w_sgu_branch, v_w_out, v_ffn_norm, v_w_gate, v_w_up, v_w_down):
    given = dict(x=x, mix_norm=mix_norm, w_in=w_in, q_norm=q_norm, k_norm=k_norm, sinks=sinks, sgu_ln_g=sgu_ln_g, sgu_ln_b=sgu_ln_b, w_spatial=w_spatial, b_spatial=b_spatial, w_attn_branch=w_attn_branch, w_sgu_branch=w_sgu_branch, w_out=w_out, ffn_norm=ffn_norm, w_gate=w_gate, w_up=w_up, w_down=w_down, loss_target=loss_target, m_mix_norm=m_mix_norm, m_w_in=m_w_in, m_q_norm=m_q_norm, m_k_norm=m_k_norm, m_sinks=m_sinks, m_sgu_ln_g=m_sgu_ln_g, m_sgu_ln_b=m_sgu_ln_b, m_w_spatial=m_w_spatial, m_b_spatial=m_b_spatial, m_w_attn_branch=m_w_attn_branch, m_w_sgu_branch=m_w_sgu_branch, m_w_out=m_w_out, m_ffn_norm=m_ffn_norm, m_w_gate=m_w_gate, m_w_up=m_w_up, m_w_down=m_w_down, v_mix_norm=v_mix_norm, v_w_in=v_w_in, v_q_norm=v_q_norm, v_k_norm=v_k_norm, v_sinks=v_sinks, v_sgu_ln_g=v_sgu_ln_g, v_sgu_ln_b=v_sgu_ln_b, v_w_spatial=v_w_spatial, v_b_spatial=v_b_spatial, v_w_attn_branch=v_w_attn_branch, v_w_sgu_branch=v_w_sgu_branch, v_w_out=v_w_out, v_ffn_norm=v_ffn_norm, v_w_gate=v_w_gate, v_w_up=v_w_up, v_w_down=v_w_down)
    weights = {n: given[n] for n in TWIN_WEIGHTS}
    shared = {n: given[n] for n in SHARED_INPUTS}
    per_example = {n: given[n] for n in ['x']}
    grad_fn = _jax.value_and_grad(_loss, argnums=(0, 1))

    def one_microbatch(ex, loss_target):
        ex = dict(ex)
        diff = ex.pop(TWIN_DIFF_INPUT)
        return grad_fn(weights, diff, {**shared, **ex}, loss_target)

    if N_MICROBATCH == 1:
        loss, (grad_w, grad_x) = one_microbatch(per_example, given["loss_target"])
    else:
        def body(carry, xs):
            loss_sum, grad_sum = carry
            l_k, (gw_k, gx_k) = one_microbatch(xs[0], xs[1])
            with _jax.named_scope("update"):
                return (loss_sum + l_k, _jax.tree.map(_jnp.add, grad_sum, gw_k)), gx_k

        init = (_jnp.zeros((), _jnp.float32), _jax.tree.map(_jnp.zeros_like, weights))
        (loss, grad_w), grad_x = _jax.lax.scan(body, init, (per_example, given["loss_target"]))
    with _jax.named_scope("update"):
        delta_w, new_m, new_v = {}, {}, {}
        for n in TWIN_WEIGHTS:
            delta_w[n], new_m[n], new_v[n] = _adamw(weights[n], grad_w[n], given["m_" + n], given["v_" + n])
    return (loss, grad_x, *[grad_w[n] for n in TWIN_WEIGHTS], *[delta_w[n] for n in TWIN_WEIGHTS],
            *[new_m[n] for n in TWIN_WEIGHTS], *[new_v[n] for n in TWIN_WEIGHTS])
```

```python
import functools

import jax
import jax.numpy as jnp
from jax import lax
from jax.experimental import pallas as pl
from jax.experimental.pallas import tpu as pltpu

HEAD_DIM = 64
N_Q_HEADS = 16
N_KV_HEADS = 4
SGU_GROUPS = 8
BLOCK = 128
EPS = 1e-6
ADAM_LR = 0.001
ADAM_B1 = 0.9
ADAM_B2 = 0.999
ADAM_EPS = 1e-08
ADAM_WD = 0.01
ADAM_STEP = 10
N_CHIPS = 4
VMEM_LIMIT = 52 * 1024 * 1024

F32 = jnp.float32
MXU = jnp.bfloat16
NN = (((1,), (0,)), ((), ()))
NT = (((1,), (1,)), ((), ()))
TN = (((0,), (0,)), ((), ()))
MESH = pl.DeviceIdType.MESH
ANY = pl.BlockSpec(memory_space=pl.ANY)


def _tile(n, pref):
    if n <= pref:
        return n
    best = None
    for t in range(BLOCK, pref + 1, BLOCK):
        if n % t == 0:
            best = t
    assert best is not None, (n, pref)
    return best


def _params():
    return pltpu.CompilerParams(vmem_limit_bytes=VMEM_LIMIT)


def _mm(name, grid, n_red, operands, specs, pairs, dims, n_extra, out_shapes, out_specs,
        acc_shapes, epilogue):
    n_op = len(operands) - n_extra
    n_out = len(out_shapes)
    n_acc = len(acc_shapes)

    def body(*refs):
        ops = refs[:n_op]
        extra = refs[n_op:n_op + n_extra]
        outs = refs[n_op + n_extra:n_op + n_extra + n_out]
        accs = refs[n_op + n_extra + n_out:]

        def prod(a, b):
            return lax.dot_general(ops[a][...], ops[b][...], dims, preferred_element_type=F32)

        if n_red == 0:
            vals = [None] * n_acc
            for a, b, k in pairs:
                d = prod(a, b)
                vals[k] = d if vals[k] is None else vals[k] + d
            epilogue(vals, extra, outs)
        else:
            axes = list(range(len(grid) - n_red, len(grid)))
            first = pl.program_id(axes[0]) == 0
            last = pl.program_id(axes[0]) == grid[axes[0]] - 1
            for ax in axes[1:]:
                first = jnp.logical_and(first, pl.program_id(ax) == 0)
                last = jnp.logical_and(last, pl.program_id(ax) == grid[ax] - 1)

            @pl.when(first)
            def _():
                for acc in accs:
                    acc[...] = jnp.zeros(acc.shape, F32)

            for a, b, k in pairs:
                accs[k][...] += prod(a, b)

            @pl.when(last)
            def _():
                epilogue([acc[...] for acc in accs], extra, outs)

    scratch = [pltpu.VMEM(s, F32) for s in acc_shapes] if n_red else []
    return pl.pallas_call(
        body, name=name, grid=grid, in_specs=specs, out_specs=out_specs, out_shape=out_shapes,
        scratch_shapes=scratch, compiler_params=_params())(*operands)


def _sigmoid(x):
    return 1.0 / (1.0 + jnp.exp(-x))


_GELU_C = 0.7978845608028654
_GELU_A = 0.044715


def _gelu(x):
    return 0.5 * x * (1.0 + jnp.tanh(_GELU_C * (x + _GELU_A * x * x * x)))


def _gelu_grad(x):
    t = jnp.tanh(_GELU_C * (x + _GELU_A * x * x * x))
    return 0.5 * (1.0 + t) + 0.5 * x * (1.0 - t * t) * _GELU_C * (1.0 + 3.0 * _GELU_A * x * x)


def _rms_fwd(name, x, g):
    S, D = x.shape
    tr = _tile(S, 256)

    def body(x_ref, g_ref, o_ref):
        xv = x_ref[...]
        r = lax.rsqrt(jnp.mean(xv * xv, axis=-1, keepdims=True) + EPS)
        o_ref[...] = (xv * r * g_ref[...]).astype(MXU)

    return pl.pallas_call(
        body, name=name, grid=(S // tr,),
        in_specs=[pl.BlockSpec((tr, D), lambda i: (i, 0)), pl.BlockSpec((1, D), lambda i: (0, 0))],
        out_specs=pl.BlockSpec((tr, D), lambda i: (i, 0)),
        out_shape=jax.ShapeDtypeStruct((S, D), MXU), compiler_params=_params())(x, g)


def _rms_bwd(name, dh, x, g, dres):
    S, D = x.shape
    tr = _tile(S, 256)

    def body(dh_ref, x_ref, g_ref, dres_ref, dx_ref, dxb_ref, dg_ref):
        xv = x_ref[...]
        r = lax.rsqrt(jnp.mean(xv * xv, axis=-1, keepdims=True) + EPS)
        xh = xv * r
        dhv = dh_ref[...]
        dy = dhv * g_ref[...]
        dx = dres_ref[...] + r * (dy - xh * jnp.mean(dy * xh, axis=-1, keepdims=True))
        dx_ref[...] = dx
        dxb_ref[...] = dx.astype(MXU)

        @pl.when(pl.program_id(0) == 0)
        def _():
            dg_ref[...] = jnp.zeros(dg_ref.shape, F32)

        dg_ref[...] += jnp.sum(dhv * xh, axis=0, keepdims=True)

    row = pl.BlockSpec((tr, D), lambda i: (i, 0))
    vec = pl.BlockSpec((1, D), lambda i: (0, 0))
    return pl.pallas_call(
        body, name=name, grid=(S // tr,), in_specs=[row, row, vec, row], out_specs=[row, row, vec],
        out_shape=[jax.ShapeDtypeStruct((S, D), F32), jax.ShapeDtypeStruct((S, D), MXU),
                   jax.ShapeDtypeStruct((1, D), F32)],
        compiler_params=_params())(dh, x, g, dres)


def _loss_head(y, target):
    S, D = y.shape
    tr = _tile(S, 256)

    def body(y_ref, t_ref, loss_ref, dy_ref, dyb_ref):
        d = y_ref[...] - t_ref[...]
        dy = d * (1.0 / D)
        dy_ref[...] = dy
        dyb_ref[...] = dy.astype(MXU)

        @pl.when(pl.program_id(0) == 0)
        def _():
            loss_ref[...] = jnp.zeros(loss_ref.shape, F32)

        loss_ref[...] += (0.5 / D) * jnp.sum(jnp.sum(d * d, axis=-1, keepdims=True), axis=0, keepdims=True)

    row = pl.BlockSpec((tr, D), lambda i: (i, 0))
    return pl.pallas_call(
        body, name="loss_head", grid=(S // tr,), in_specs=[row, row],
        out_specs=[pl.BlockSpec((1, 1), lambda i: (0, 0)), row, row],
        out_shape=[jax.ShapeDtypeStruct((1, 1), F32), jax.ShapeDtypeStruct((S, D), F32),
                   jax.ShapeDtypeStruct((S, D), MXU)],
        compiler_params=_params())(y, target)


def _head_sum(v):
    r = lax.broadcasted_iota(jnp.int32, (BLOCK, BLOCK), 0) // HEAD_DIM
    c = lax.broadcasted_iota(jnp.int32, (BLOCK, BLOCK), 1) // HEAD_DIM
    ones = jnp.where(r == c, 1.0, 0.0).astype(jnp.bfloat16)
    hi = v.astype(jnp.bfloat16)
    lo = (v - hi.astype(F32)).astype(jnp.bfloat16)
    parts = []
    for t in range(v.shape[1] // BLOCK):
        sl = slice(t * BLOCK, (t + 1) * BLOCK)
        parts.append(jnp.dot(hi[:, sl], ones, preferred_element_type=F32)
                     + jnp.dot(lo[:, sl], ones, preferred_element_type=F32))
    return parts[0] if len(parts) == 1 else jnp.concatenate(parts, axis=-1)


def _swap_halves(v):
    w = v.shape[1]
    half = HEAD_DIM // 2
    lane = lax.broadcasted_iota(jnp.int32, v.shape, 1) % HEAD_DIM
    return jnp.where(lane < half, pltpu.roll(v, w - half, 1), pltpu.roll(v, half, 1))


def _norm_rope(xv, gain, cos, sin):
    r = lax.rsqrt(_head_sum(xv * xv) * (1.0 / HEAD_DIM) + EPS)
    xn = xv * r * gain
    return xn * cos + _swap_halves(xn) * sin


def _norm_rope_bwd(dy, xv, gain, cos, sin):
    r = lax.rsqrt(_head_sum(xv * xv) * (1.0 / HEAD_DIM) + EPS)
    xh = xv * r
    dxn = dy * cos + _swap_halves(dy * sin)
    dgain = jnp.sum(dxn * xh, axis=0, keepdims=True)
    dxh = dxn * gain
    dx = r * (dxh - xh * (_head_sum(dxh * xh) * (1.0 / HEAD_DIM)))
    return dx, dgain


def _fold_heads(v):
    acc = v[:, 0:BLOCK]
    for t in range(1, v.shape[1] // BLOCK):
        acc = acc + v[:, t * BLOCK:(t + 1) * BLOCK]
    return acc + pltpu.roll(acc, HEAD_DIM, 1)


def _tile_lanes(v, width):
    return v if width == BLOCK else jnp.tile(v, (1, width // BLOCK))


def _qk_prep(proj, qg, kg, cos, sin, AW, KW):
    S = proj.shape[0]
    tr = _tile(S, 256)
    scale = HEAD_DIM ** -0.5

    def body(q_ref, k_ref, v_ref, qg_ref, kg_ref, cos_ref, sin_ref, qo_ref, ko_ref, vo_ref):
        c, s = cos_ref[...], sin_ref[...]
        q = _norm_rope(q_ref[...], _tile_lanes(qg_ref[...], AW), _tile_lanes(c, AW), _tile_lanes(s, AW))
        k = _norm_rope(k_ref[...], _tile_lanes(kg_ref[...], KW), _tile_lanes(c, KW), _tile_lanes(s, KW))
        qo_ref[...] = (q * scale).astype(MXU)
        ko_ref[...] = k.astype(MXU)
        vo_ref[...] = v_ref[...].astype(MXU)

    assert AW % KW == 0
    vec = pl.BlockSpec((1, BLOCK), lambda i: (0, 0))
    tab = pl.BlockSpec((tr, BLOCK), lambda i: (i, 0))
    return pl.pallas_call(
        body, name="qk_prep", grid=(S // tr,),
        in_specs=[pl.BlockSpec((tr, AW), lambda i: (i, 0)),
                  pl.BlockSpec((tr, KW), lambda i: (i, AW // KW)),
                  pl.BlockSpec((tr, KW), lambda i: (i, AW // KW + 1)), vec, vec, tab, tab],
        out_specs=[pl.BlockSpec((tr, AW), lambda i: (i, 0)), pl.BlockSpec((tr, KW), lambda i: (i, 0)),
                   pl.BlockSpec((tr, KW), lambda i: (i, 0))],
        out_shape=[jax.ShapeDtypeStruct((S, AW), MXU), jax.ShapeDtypeStruct((S, KW), MXU),
                   jax.ShapeDtypeStruct((S, KW), MXU)],
        compiler_params=_params())(proj, proj, proj, qg, kg, cos, sin)


def _attn_probs(n, q, kp, kc, g, sink_ref, qpk):
    hd = HEAD_DIM
    kcat = jnp.concatenate([kp[:, g * hd:(g + 1) * hd], kc[:, g * hd:(g + 1) * hd]], axis=0)
    qs = jnp.concatenate([q[:, (g * qpk + j) * hd:(g * qpk + j + 1) * hd] for j in range(qpk)], axis=0)
    s = lax.dot_general(qs, kcat, NT, preferred_element_type=F32)
    row = lax.broadcasted_iota(jnp.int32, (BLOCK, 2 * BLOCK), 0)
    col = lax.broadcasted_iota(jnp.int32, (BLOCK, 2 * BLOCK), 1)
    ok = (col > row) & (col <= row + BLOCK) & ((col >= BLOCK) | (n > 0))
    s = jnp.where(jnp.concatenate([ok] * qpk, axis=0), s, -1e30)
    sk = jnp.concatenate([jnp.full((BLOCK, 1), sink_ref[0, g * qpk + j], F32) for j in range(qpk)], axis=0)
    m = jnp.maximum(jnp.max(s, axis=-1, keepdims=True), sk)
    e = jnp.exp(s - m)
    es = jnp.exp(sk - m)
    z = jnp.sum(e, axis=-1, keepdims=True) + es
    return e / z, es / z, qs, kcat


def _attn_fwd(qr, kr, vb, sinks):
    S, AW = qr.shape
    KW = kr.shape[1]
    nb = S // BLOCK
    nkv = KW // HEAD_DIM
    qpk = AW // KW
    hd = HEAD_DIM

    def body(sink_ref, q_ref, kp_ref, kc_ref, vp_ref, vc_ref, o_ref):
        n = pl.program_id(0)
        q, kp, kc, vp, vc = q_ref[...], kp_ref[...], kc_ref[...], vp_ref[...], vc_ref[...]
        outs = [None] * (nkv * qpk)
        for g in range(nkv):
            p, _, _, _ = _attn_probs(n, q, kp, kc, g, sink_ref, qpk)
            vcat = jnp.concatenate([vp[:, g * hd:(g + 1) * hd], vc[:, g * hd:(g + 1) * hd]], axis=0)
            o = jnp.dot(p.astype(MXU), vcat, preferred_element_type=F32)
            for j in range(qpk):
                outs[g * qpk + j] = o[j * BLOCK:(j + 1) * BLOCK]
        o_ref[...] = jnp.concatenate(outs, axis=-1).astype(MXU)

    cur = lambda n: (n, 0)
    prev = lambda n: (jnp.maximum(n - 1, 0), 0)
    return pl.pallas_call(
        body, name="attn_fwd", grid=(nb,),
        in_specs=[pl.BlockSpec(memory_space=pltpu.SMEM), pl.BlockSpec((BLOCK, AW), cur),
                  pl.BlockSpec((BLOCK, KW), prev), pl.BlockSpec((BLOCK, KW), cur),
                  pl.BlockSpec((BLOCK, KW), prev), pl.BlockSpec((BLOCK, KW), cur)],
        out_specs=pl.BlockSpec((BLOCK, AW), cur),
        out_shape=jax.ShapeDtypeStruct((S, AW), MXU), compiler_params=_params())(sinks, qr, kr, kr, vb, vb)


def _attn_bwd(qr, kr, vb, sinks, dattn):
    S, AW = qr.shape
    KW = kr.shape[1]
    nb = S // BLOCK
    nkv = KW // HEAD_DIM
    qpk = AW // KW
    hd = HEAD_DIM
    scale = HEAD_DIM ** -0.5

    def body(sink_ref, q_ref, kp_ref, kc_ref, vp_ref, vc_ref, do_ref,
             dq_ref, dkp_ref, dkc_ref, dvp_ref, dvc_ref, dsink_ref):
        n = pl.program_id(0)
        q, kp, kc, vp, vc = q_ref[...], kp_ref[...], kc_ref[...], vp_ref[...], vc_ref[...]
        do = do_ref[...].astype(MXU)
        lane = lax.broadcasted_iota(jnp.int32, (1, BLOCK), 1)
        dsink = jnp.zeros((1, BLOCK), F32)
        dqs = [None] * (nkv * qpk)
        dkps, dkcs, dvps, dvcs = [], [], [], []
        for g in range(nkv):
            p, psink, qs, kcat = _attn_probs(n, q, kp, kc, g, sink_ref, qpk)
            vcat = jnp.concatenate([vp[:, g * hd:(g + 1) * hd], vc[:, g * hd:(g + 1) * hd]], axis=0)
            dos = jnp.concatenate([do[:, (g * qpk + j) * hd:(g * qpk + j + 1) * hd] for j in range(qpk)], axis=0)
            dp = lax.dot_general(dos, vcat, NT, preferred_element_type=F32)
            dv = lax.dot_general(p.astype(MXU), dos, TN, preferred_element_type=F32)
            delta = jnp.sum(p * dp, axis=-1, keepdims=True)
            ds = (p * (dp - delta)).astype(MXU)
            dsk = -psink * delta
            dq = jnp.dot(ds, kcat, preferred_element_type=F32) * scale
            dk = lax.dot_general(ds, qs, TN, preferred_element_type=F32)
            for j in range(qpk):
                dqs[g * qpk + j] = dq[j * BLOCK:(j + 1) * BLOCK]
                tot = jnp.sum(dsk[j * BLOCK:(j + 1) * BLOCK], axis=0, keepdims=True)
                dsink = dsink + jnp.where(lane == g * qpk + j, tot, 0.0)
            dkps.append(dk[:BLOCK])
            dkcs.append(dk[BLOCK:])
            dvps.append(dv[:BLOCK])
            dvcs.append(dv[BLOCK:])
        dq_ref[...] = jnp.concatenate(dqs, axis=-1)
        dkp_ref[...] = jnp.concatenate(dkps, axis=-1)
        dkc_ref[...] = jnp.concatenate(dkcs, axis=-1)
        dvp_ref[...] = jnp.concatenate(dvps, axis=-1)
        dvc_ref[...] = jnp.concatenate(dvcs, axis=-1)

        @pl.when(n == 0)
        def _():
            dsink_ref[...] = jnp.zeros(dsink_ref.shape, F32)

        dsink_ref[...] += dsink

    cur = lambda n: (n, 0)
    prev = lambda n: (jnp.maximum(n - 1, 0), 0)
    kv = jax.ShapeDtypeStruct((S, KW), F32)
    kvspec = pl.BlockSpec((BLOCK, KW), cur)
    return pl.pallas_call(
        body, name="attn_bwd", grid=(nb,),
        in_specs=[pl.BlockSpec(memory_space=pltpu.SMEM), pl.BlockSpec((BLOCK, AW), cur),
                  pl.BlockSpec((BLOCK, KW), prev), kvspec, pl.BlockSpec((BLOCK, KW), prev), kvspec,
                  pl.BlockSpec((BLOCK, AW), cur)],
        out_specs=[pl.BlockSpec((BLOCK, AW), cur), kvspec, kvspec, kvspec, kvspec,
                   pl.BlockSpec((1, BLOCK), lambda n: (0, 0))],
        out_shape=[jax.ShapeDtypeStruct((S, AW), F32), kv, kv, kv, kv, jax.ShapeDtypeStruct((1, BLOCK), F32)],
        compiler_params=_params())(sinks, qr, kr, kr, vb, vb, dattn)


def _qk_prep_bwd(proj, qg, kg, cos, sin, dq, dkp, dkc, dvp, dvc, AW, KW):
    S = proj.shape[0]
    nb = S // BLOCK

    def body(q_ref, k_ref, qg_ref, kg_ref, cos_ref, sin_ref, dq_ref, dkp_ref, dkc_ref, dvp_ref, dvc_ref,
             o_ref, dqg_ref, dkg_ref):
        n = pl.program_id(0)
        c, s = cos_ref[...], sin_ref[...]
        has_next = jnp.where(n < nb - 1, 1.0, 0.0)
        dk = dkc_ref[...] + has_next * dkp_ref[...]
        dv = dvc_ref[...] + has_next * dvp_ref[...]
        dxq, dqg = _norm_rope_bwd(dq_ref[...], q_ref[...], _tile_lanes(qg_ref[...], AW),
                                  _tile_lanes(c, AW), _tile_lanes(s, AW))
        dxk, dkg = _norm_rope_bwd(dk, k_ref[...], _tile_lanes(kg_ref[...], KW),
                                  _tile_lanes(c, KW), _tile_lanes(s, KW))
        o_ref[...] = jnp.concatenate([dxq, dxk, dv], axis=-1).astype(MXU)

        @pl.when(n == 0)
        def _():
            dqg_ref[...] = jnp.zeros(dqg_ref.shape, F32)
            dkg_ref[...] = jnp.zeros(dkg_ref.shape, F32)

        dqg_ref[...] += _fold_heads(dqg)
        dkg_ref[...] += _fold_heads(dkg)

    cur = lambda n: (n, 0)
    nxt = lambda n: (jnp.minimum(n + 1, nb - 1), 0)
    vec = pl.BlockSpec((1, BLOCK), lambda n: (0, 0))
    tab = pl.BlockSpec((BLOCK, BLOCK), cur)
    return pl.pallas_call(
        body, name="qk_prep_bwd", grid=(nb,),
        in_specs=[pl.BlockSpec((BLOCK, AW), cur), pl.BlockSpec((BLOCK, KW), lambda n: (n, AW // KW)),
                  vec, vec, tab, tab, pl.BlockSpec((BLOCK, AW), cur),
                  pl.BlockSpec((BLOCK, KW), nxt), pl.BlockSpec((BLOCK, KW), cur),
                  pl.BlockSpec((BLOCK, KW), nxt), pl.BlockSpec((BLOCK, KW), cur)],
        out_specs=[pl.BlockSpec((BLOCK, AW + 2 * KW), cur), vec, vec],
        out_shape=[jax.ShapeDtypeStruct((S, AW + 2 * KW), MXU), jax.ShapeDtypeStruct((1, BLOCK), F32),
                   jax.ShapeDtypeStruct((1, BLOCK), F32)],
        compiler_params=_params())(proj, proj, qg, kg, cos, sin, dq, dkp, dkc, dvp, dvc)


def _sgu_tile(pu_ref, pv_ref, lng_ref, lnb_ref, w_ref, b_ref):
    u = _gelu(pu_ref[...])
    v = _gelu(pv_ref[...])
    mu = jnp.mean(v, axis=-1, keepdims=True)
    vc = v - mu
    r = lax.rsqrt(jnp.mean(vc * vc, axis=-1, keepdims=True) + EPS)
    xh = vc * r
    vn = xh * lng_ref[...] + lnb_ref[...]
    row = lax.broadcasted_iota(jnp.int32, (BLOCK, BLOCK), 0)
    col = lax.broadcasted_iota(jnp.int32, (BLOCK, BLOCK), 1)
    tri = row >= col
    w = jnp.where(tri, w_ref[...], 0.0).astype(MXU)
    s = jnp.dot(w, vn.astype(MXU), preferred_element_type=F32) + b_ref[...]
    return u, xh, r, vn, w, s, tri


def _sgu_specs(u0):
    G = SGU_GROUPS
    return [pl.BlockSpec((BLOCK, BLOCK), lambda g, n: (n, u0 + g)),
            pl.BlockSpec((BLOCK, BLOCK), lambda g, n: (n, u0 + G + g)),
            pl.BlockSpec((1, BLOCK), lambda g, n: (0, g)), pl.BlockSpec((1, BLOCK), lambda g, n: (0, g)),
            pl.BlockSpec((None, BLOCK, BLOCK), lambda g, n: (g, 0, 0)),
            pl.BlockSpec((None, BLOCK, 1), lambda g, n: (g, 0, 0))]


def _sgu_fwd(proj, lng, lnb, ws, bs, u0):
    S = proj.shape[0]
    G = SGU_GROUPS

    def body(pu_ref, pv_ref, lng_ref, lnb_ref, w_ref, b_ref, o_ref):
        u, _, _, _, _, s, _ = _sgu_tile(pu_ref, pv_ref, lng_ref, lnb_ref, w_ref, b_ref)
        o_ref[...] = (u * s).astype(MXU)

    return pl.pallas_call(
        body, name="sgu_fwd", grid=(G, S // BLOCK), in_specs=_sgu_specs(u0),
        out_specs=pl.BlockSpec((BLOCK, BLOCK), lambda g, n: (n, g)),
        out_shape=jax.ShapeDtypeStruct((S, G * BLOCK), MXU), compiler_params=_params())(proj, proj, lng, lnb, ws, bs)


def _sgu_bwd(proj, lng, lnb, ws, bs, dsgu, u0):
    S = proj.shape[0]
    G = SGU_GROUPS

    def body(pu_ref, pv_ref, lng_ref, lnb_ref, w_ref, b_ref, do_ref,
             dpu_ref, dpv_ref, dw_ref, db_ref, dlng_ref, dlnb_ref):
        u, xh, r, vn, w, s, tri = _sgu_tile(pu_ref, pv_ref, lng_ref, lnb_ref, w_ref, b_ref)
        do = do_ref[...]
        dpu_ref[...] = (do * s * _gelu_grad(pu_ref[...])).astype(MXU)
        ds = do * u
        dsb = ds.astype(MXU)
        dw = jnp.where(tri, lax.dot_general(dsb, vn.astype(MXU), NT, preferred_element_type=F32), 0.0)
        dvn = lax.dot_general(w, dsb, TN, preferred_element_type=F32)
        dxh = dvn * lng_ref[...]
        dv = r * (dxh - jnp.mean(dxh, axis=-1, keepdims=True) - xh * jnp.mean(dxh * xh, axis=-1, keepdims=True))
        dpv_ref[...] = (dv * _gelu_grad(pv_ref[...])).astype(MXU)

        @pl.when(pl.program_id(1) == 0)
        def _():
            dw_ref[...] = jnp.zeros(dw_ref.shape, F32)
            db_ref[...] = jnp.zeros(db_ref.shape, F32)
            dlng_ref[...] = jnp.zeros(dlng_ref.shape, F32)
            dlnb_ref[...] = jnp.zeros(dlnb_ref.shape, F32)

        dw_ref[...] += dw
        db_ref[...] += jnp.sum(ds, axis=-1, keepdims=True)
        dlng_ref[...] += jnp.sum(dvn * xh, axis=0, keepdims=True)
        dlnb_ref[...] += jnp.sum(dvn, axis=0, keepdims=True)

    tile = lambda off: pl.BlockSpec((BLOCK, BLOCK), lambda g, n: (n, off + g))
    vec = pl.BlockSpec((1, BLOCK), lambda g, n: (0, g))
    half = jax.ShapeDtypeStruct((S, G * BLOCK), MXU)
    return pl.pallas_call(
        body, name="sgu_bwd", grid=(G, S // BLOCK), in_specs=_sgu_specs(u0) + [tile(0)],
        out_specs=[tile(0), tile(0), pl.BlockSpec((None, BLOCK, BLOCK), lambda g, n: (g, 0, 0)),
                   pl.BlockSpec((None, BLOCK, 1), lambda g, n: (g, 0, 0)), vec, vec],
        out_shape=[half, half, jax.ShapeDtypeStruct((G, BLOCK, BLOCK), F32),
                   jax.ShapeDtypeStruct((G, BLOCK, 1), F32),
                   jax.ShapeDtypeStruct((1, G * BLOCK), F32), jax.ShapeDtypeStruct((1, G * BLOCK), F32)],
        compiler_params=_params())(proj, proj, lng, lnb, ws, bs, dsgu)


def _store_f32(vals, extra, outs):
    for v, o in zip(vals, outs):
        o[...] = v


def _store_mxu(vals, extra, outs):
    for v, o in zip(vals, outs):
        o[...] = v.astype(MXU)


def _proj_in(h, w):
    S, D = h.shape
    Ns = w.shape[2]
    tm, tn = _tile(S, 1024), _tile(Ns, 1024)
    npb = Ns // tn
    return _mm("proj_in", (S // tm, N_CHIPS, npb), 0, [h, w],
               [pl.BlockSpec((tm, D), lambda i, s, j: (i, 0)), pl.BlockSpec((None, D, tn), lambda i, s, j: (s, 0, j))],
               [(0, 1, 0)], NN, 0, [jax.ShapeDtypeStruct((S, N_CHIPS * Ns), F32)],
               [pl.BlockSpec((tm, tn), lambda i, s, j: (i, s * npb + j))], [None], _store_f32)[0]


def _branches(attn, sgu, wa, ws, proj, gate0):
    S, AW = attn.shape
    SW = sgu.shape[1]
    Nb = wa.shape[2]
    D = N_CHIPS * Nb
    tm = _tile(S, 512)
    assert gate0 % Nb == 0
    ga, gb = gate0 // Nb, (gate0 + D) // Nb

    def epilogue(vals, extra, outs):
        a, b = vals
        outs[0][...] = (_sigmoid(extra[0][...]) * a + _sigmoid(extra[1][...]) * b).astype(MXU)
        outs[1][...] = a
        outs[2][...] = b

    tile = pl.BlockSpec((tm, Nb), lambda i, s: (i, s))
    wspec = lambda k: pl.BlockSpec((None, k, Nb), lambda i, s: (s, 0, 0))
    f = jax.ShapeDtypeStruct((S, D), F32)
    return _mm("branches", (S // tm, N_CHIPS), 0, [attn, sgu, wa, ws, proj, proj],
               [pl.BlockSpec((tm, AW), lambda i, s: (i, 0)), pl.BlockSpec((tm, SW), lambda i, s: (i, 0)),
                wspec(AW), wspec(SW), pl.BlockSpec((tm, Nb), lambda i, s: (i, ga + s)),
                pl.BlockSpec((tm, Nb), lambda i, s: (i, gb + s))],
               [(0, 2, 0), (1, 3, 1)], NN, 2, [jax.ShapeDtypeStruct((S, D), MXU), f, f], [tile] * 3,
               [None, None], epilogue)


def _rows_mm(name, a, w, res):
    S = a.shape[0]
    _, K, N = w.shape
    tm, tn = _tile(S, 1024), _tile(N, 1024)

    def epilogue(vals, extra, outs):
        outs[0][...] = extra[0][...] + vals[0]

    out = pl.BlockSpec((tm, tn), lambda i, j, s: (i, j))
    return _mm(name, (S // tm, N // tn, N_CHIPS), 1, [a, w, res],
               [pl.BlockSpec((tm, K), lambda i, j, s: (i, s)), pl.BlockSpec((None, K, tn), lambda i, j, s: (s, 0, j)), out],
               [(0, 1, 0)], NN, 1, [jax.ShapeDtypeStruct((S, N), F32)], [out], [(tm, tn)], epilogue)[0]


def _gate_up(h2, wg, wu):
    S, D = h2.shape
    Nf = wg.shape[2]
    tm, tk = _tile(S, 512), _tile(D, 512)

    def epilogue(vals, extra, outs):
        g, u = vals
        outs[0][...] = g
        outs[1][...] = u
        outs[2][...] = (g * _sigmoid(g) * u).astype(MXU)

    w = pl.BlockSpec((None, tk, Nf), lambda i, s, k: (s, k, 0))
    o = pl.BlockSpec((tm, Nf), lambda i, s, k: (i, s))
    f = jax.ShapeDtypeStruct((S, N_CHIPS * Nf), F32)
    return _mm("gate_up", (S // tm, N_CHIPS, D // tk), 1, [h2, wg, wu],
               [pl.BlockSpec((tm, tk), lambda i, s, k: (i, k)), w, w], [(0, 1, 0), (0, 2, 1)], NN, 0,
               [f, f, jax.ShapeDtypeStruct((S, N_CHIPS * Nf), MXU)], [o, o, o], [(tm, Nf)] * 2, epilogue)


def _down_bwd(dyb, wd, g, u):
    S, D = dyb.shape
    Kf = wd.shape[1]
    tm = _tile(S, 512)

    def epilogue(vals, extra, outs):
        da, gv, uv = vals[0], extra[0][...], extra[1][...]
        sg = _sigmoid(gv)
        outs[0][...] = (da * uv * sg * (1.0 + gv * (1.0 - sg))).astype(MXU)
        outs[1][...] = (da * gv * sg).astype(MXU)

    t = pl.BlockSpec((tm, Kf), lambda i, s: (i, s))
    o = jax.ShapeDtypeStruct((S, N_CHIPS * Kf), MXU)
    return _mm("down_bwd", (S // tm, N_CHIPS), 0, [dyb, wd, g, u],
               [pl.BlockSpec((tm, D), lambda i, s: (i, 0)), pl.BlockSpec((None, Kf, D), lambda i, s: (s, 0, 0)), t, t],
               [(0, 1, 0)], NT, 2, [o, o], [t, t], [None], epilogue)


def _out_bwd(dxb, wo, proj, ba, bb, gate0):
    S, D = dxb.shape
    Ko = wo.shape[1]
    tm = _tile(S, 512)
    assert gate0 % Ko == 0
    ga, gb = gate0 // Ko, (gate0 + D) // Ko

    def epilogue(vals, extra, outs):
        dm = vals[0]
        sa, sb = _sigmoid(extra[0][...]), _sigmoid(extra[1][...])
        outs[0][...] = (dm * sa).astype(MXU)
        outs[1][...] = (dm * sb).astype(MXU)
        outs[2][...] = (dm * extra[2][...] * sa * (1.0 - sa)).astype(MXU)
        outs[3][...] = (dm * extra[3][...] * sb * (1.0 - sb)).astype(MXU)

    t = pl.BlockSpec((tm, Ko), lambda i, s: (i, s))
    o = jax.ShapeDtypeStruct((S, D), MXU)
    return _mm("out_bwd", (S // tm, N_CHIPS), 0, [dxb, wo, proj, proj, ba, bb],
               [pl.BlockSpec((tm, D), lambda i, s: (i, 0)), pl.BlockSpec((None, Ko, D), lambda i, s: (s, 0, 0)),
                pl.BlockSpec((tm, Ko), lambda i, s: (i, ga + s)), pl.BlockSpec((tm, Ko), lambda i, s: (i, gb + s)), t, t],
               [(0, 1, 0)], NT, 4, [o] * 4, [t] * 4, [None], epilogue)


def _dx_cols(name, terms, n_out):
    S = terms[0][0].shape[0]
    _, K, Ns = terms[0][1].shape
    tm, tko, tn = _tile(S, 1024), _tile(K, 1024), _tile(Ns, 1408)
    npb = Ns // tn
    operands, specs, pairs = [], [], []
    for t, (dy, w, k) in enumerate(terms):
        assert w.shape == (N_CHIPS, K, Ns)
        operands += [dy, w]
        specs += [pl.BlockSpec((tm, tn), lambda i, jk, s, jn: (i, s * npb + jn)),
                  pl.BlockSpec((None, tko, tn), lambda i, jk, s, jn: (s, jk, jn))]
        pairs.append((2 * t, 2 * t + 1, k))
    out = pl.BlockSpec((tm, tko), lambda i, jk, s, jn: (i, jk))
    return _mm(name, (S // tm, K // tko, N_CHIPS, npb), 2, operands, specs, pairs, NT, 0,
               [jax.ShapeDtypeStruct((S, K), F32)] * n_out, [out] * n_out, [(tm, tko)] * n_out, _store_f32)


def _dw_cols(name, a, dy):
    S, K = a.shape
    Ns = dy.shape[1] // N_CHIPS
    tk, tn = _tile(K, 512), _tile(Ns, 1408)
    npb = Ns // tn
    return _mm(name, (K // tk, N_CHIPS, npb), 0, [a, dy],
               [pl.BlockSpec((S, tk), lambda jk, s, jn: (0, jk)), pl.BlockSpec((S, tn), lambda jk, s, jn: (0, s * npb + jn))],
               [(0, 1, 0)], TN, 0, [jax.ShapeDtypeStruct((N_CHIPS, K, Ns), MXU)],
               [pl.BlockSpec((None, tk, tn), lambda jk, s, jn: (s, jk, jn))], [None], _store_mxu)[0]


def _dw_rows(name, a, dy):
    S = a.shape[0]
    K = a.shape[1] // N_CHIPS
    N = dy.shape[1]
    tk, tn = _tile(K, 1408), _tile(N, 1024)
    nkb = K // tk
    return _mm(name, (N_CHIPS, nkb, N // tn), 0, [a, dy],
               [pl.BlockSpec((S, tk), lambda s, jk, jn: (0, s * nkb + jk)), pl.BlockSpec((S, tn), lambda s, jk, jn: (0, jn))],
               [(0, 1, 0)], TN, 0, [jax.ShapeDtypeStruct((N_CHIPS, K, N), MXU)],
               [pl.BlockSpec((None, tk, tn), lambda s, jk, jn: (s, jk, jn))], [None], _store_mxu)[0]


def _layer_fwd(x, w, sp, cos, sin, dims):
    AW, KW, gate0, u0 = dims
    h = _rms_fwd("mix_norm", x, sp["mix_norm"])
    proj = _proj_in(h, w["w_in"])
    qr, kr, vb = _qk_prep(proj, sp["q_norm"], sp["k_norm"], cos, sin, AW, KW)
    attn = _attn_fwd(qr, kr, vb, sp["sinks"])
    sgu = _sgu_fwd(proj, sp["sgu_ln_g"], sp["sgu_ln_b"], sp["w_spatial"], sp["b_spatial"], u0)
    merged, ba, bb = _branches(attn, sgu, w["w_attn_branch"], w["w_sgu_branch"], proj, gate0)
    x1 = _rows_mm("out_proj", merged, w["w_out"], x)
    h2 = _rms_fwd("ffn_norm", x1, sp["ffn_norm"])
    g, u, act = _gate_up(h2, w["w_gate"], w["w_up"])
    x2 = _rows_mm("down_proj", act, w["w_down"], x1)
    saved = dict(x=x, h=h, proj=proj, qr=qr, kr=kr, vb=vb, attn=attn, sgu=sgu, merged=merged, ba=ba, bb=bb,
                 x1=x1, h2=h2, g=g, u=u, act=act)
    return x2, saved


def _layer_bwd(dy, dyb, w, sp, sv, cos, sin, dims):
    AW, KW, gate0, u0 = dims
    big, small = {}, {}
    dg, du = _down_bwd(dyb, w["w_down"], sv["g"], sv["u"])
    big["w_down"] = _dw_rows("dw_down", sv["act"], dyb)
    big["w_gate"] = _dw_cols("dw_gate", sv["h2"], dg)
    big["w_up"] = _dw_cols("dw_up", sv["h2"], du)
    dh2 = _dx_cols("dh2", [(dg, w["w_gate"], 0), (du, w["w_up"], 0)], 1)[0]
    dx1, dx1b, small["ffn_norm"] = _rms_bwd("ffn_norm_bwd", dh2, sv["x1"], sp["ffn_norm"], dy)
    dba, dbb, dgla, dglb = _out_bwd(dx1b, w["w_out"], sv["proj"], sv["ba"], sv["bb"], gate0)
    big["w_out"] = _dw_rows("dw_out", sv["merged"], dx1b)
    dattn, dsgu = _dx_cols("dbranch_in", [(dba, w["w_attn_branch"], 0), (dbb, w["w_sgu_branch"], 1)], 2)
    big["w_attn_branch"] = _dw_cols("dw_attn_branch", sv["attn"], dba)
    big["w_sgu_branch"] = _dw_cols("dw_sgu_branch", sv["sgu"], dbb)
    dpu, dpv, small["w_spatial"], db, small["sgu_ln_g"], small["sgu_ln_b"] = _sgu_bwd(
        sv["proj"], sp["sgu_ln_g"], sp["sgu_ln_b"], sp["w_spatial"], sp["b_spatial"], dsgu, u0)
    small["b_spatial"] = db[:, :, 0]
    dq, dkp, dkc, dvp, dvc, dsink = _attn_bwd(sv["qr"], sv["kr"], sv["vb"], sp["sinks"], dattn)
    small["sinks"] = dsink[:, :sp["sinks"].shape[1]]
    dqkv, dqg, dkg = _qk_prep_bwd(sv["proj"], sp["q_norm"], sp["k_norm"], cos, sin, dq, dkp, dkc, dvp, dvc, AW, KW)
    small["q_norm"] = dqg[:, :HEAD_DIM]
    small["k_norm"] = dkg[:, :HEAD_DIM]
    dproj = jnp.concatenate([dqkv, dpu, dpv, dgla, dglb], axis=1)
    big["w_in"] = _dw_cols("dw_in", sv["h"], dproj)
    dh = _dx_cols("dh", [(dproj, w["w_in"], 0)], 1)[0]
    dx, dxb, small["mix_norm"] = _rms_bwd("mix_norm_bwd", dh, sv["x"], sp["mix_norm"], dx1)
    return dx, dxb, big, small


def _place():
    x, y, c = lax.axis_index("x"), lax.axis_index("y"), lax.axis_index("c")
    chips = [(1 - x, y), (x, 1 - y), (1 - x, 1 - y)]
    return x, y, c, chips


def _half_rows(c, rows):
    h = rows // 2
    assert h % 16 == 0
    return pl.ds(pl.multiple_of(c * h, 16), h)


def _all_gather(shards):
    n = len(shards)

    def body(*refs):
        src, dst = refs[:n], refs[n:2 * n]
        ici_send, ici_recv, d2d_send, d2d_recv, loc_sem = refs[2 * n:]
        x, y, c, chips = _place()
        me = 2 * x + y
        sibling = (x, y, 1 - c)
        local = [pltpu.make_async_copy(src[a], dst[a].at[me], loc_sem.at[a]) for a in range(n)]
        for cp in local:
            cp.start()

        def ici(a, j, block, rows):
            px, py = chips[j]
            return pltpu.make_async_remote_copy(
                src_ref=src[a].at[rows], dst_ref=dst[a].at[block, rows], send_sem=ici_send.at[a, j],
                recv_sem=ici_recv.at[a, j], device_id=(px, py, c), device_id_type=MESH)

        def d2d(a, j, rows):
            px, py = chips[j]
            blk = dst[a].at[2 * px + py, rows]
            return pltpu.make_async_remote_copy(
                src_ref=blk, dst_ref=blk, send_sem=d2d_send.at[a, j], recv_sem=d2d_recv.at[a, j],
                device_id=sibling, device_id_type=MESH)

        sends = []
        for a in range(n):
            mine = _half_rows(c, src[a].shape[0])
            for j in range(3):
                sends.append(ici(a, j, me, mine))
                sends[-1].start()
        for a in range(n):
            mine = _half_rows(c, src[a].shape[0])
            for j in range(3):
                px, py = chips[j]
                ici(a, j, 2 * px + py, mine).wait_recv()
                sends.append(d2d(a, j, mine))
                sends[-1].start()
        for a in range(n):
            other = _half_rows(1 - c, src[a].shape[0])
            for j in range(3):
                d2d(a, j, other).wait_recv()
        for cp in sends:
            cp.wait_send()
        for cp in local:
            cp.wait()

    out_shape = [jax.ShapeDtypeStruct((N_CHIPS,) + s.shape, s.dtype) for s in shards]
    return pl.pallas_call(
        body, name="gather_weights", in_specs=[ANY] * n, out_specs=[ANY] * n, out_shape=out_shape,
        scratch_shapes=[pltpu.SemaphoreType.DMA((n, 3))] * 4 + [pltpu.SemaphoreType.DMA((n,))])(*shards)


def _pair_exchange(grads):
    n = len(grads)

    def body(*refs):
        src, dst = refs[:n], refs[n:2 * n]
        send_sem, recv_sem = refs[2 * n:]
        x, y, c, _ = _place()
        copies = []
        for a in range(n):
            theirs = _half_rows(1 - c, src[a].shape[1])
            copies.append(pltpu.make_async_remote_copy(
                src_ref=src[a].at[:, theirs], dst_ref=dst[a], send_sem=send_sem.at[a], recv_sem=recv_sem.at[a],
                device_id=(x, y, 1 - c), device_id_type=MESH))
            copies[-1].start()
        for cp in copies:
            cp.wait()

    out_shape = [jax.ShapeDtypeStruct((g.shape[0], g.shape[1] // 2, g.shape[2]), g.dtype) for g in grads]
    return pl.pallas_call(
        body, name="grad_pair_exchange", in_specs=[ANY] * n, out_specs=[ANY] * n, out_shape=out_shape,
        scratch_shapes=[pltpu.SemaphoreType.DMA((n,))] * 2)(*grads)


def _row_tile(rows, pref):
    best = None
    for t in range(16, min(rows, pref) + 1, 16):
        if rows % t == 0:
            best = t
    assert best is not None, rows
    return best


def _pair_sum(name, core, g, p):
    _, h, C = p.shape
    tr = _row_tile(h, 256)
    nrb = h // tr

    def body(core_ref, g_ref, p_ref, o_ref):
        o_ref[...] = (g_ref[...].astype(F32) + p_ref[...].astype(F32)).astype(o_ref.dtype)

    spec = pl.BlockSpec((None, tr, C), lambda s, i, core_ref: (s, i, 0))
    return pl.pallas_call(
        body, name=name, out_shape=jax.ShapeDtypeStruct(p.shape, p.dtype),
        grid_spec=pltpu.PrefetchScalarGridSpec(
            num_scalar_prefetch=1, grid=(N_CHIPS, nrb),
            in_specs=[pl.BlockSpec((None, tr, C), lambda s, i, core_ref: (s, core_ref[0] * nrb + i, 0)), spec],
            out_specs=spec),
        compiler_params=_params())(core, g, p)


def _chip_scatter(sums):
    n = len(sums)

    def body(*refs):
        src, dst = refs[:n], refs[n:2 * n]
        send_sem, recv_sem, loc_sem = refs[2 * n:]
        x, y, c, chips = _place()
        me = 2 * x + y
        local = [pltpu.make_async_copy(src[a].at[me], dst[a].at[3], loc_sem.at[a]) for a in range(n)]
        for cp in local:
            cp.start()
        copies = []
        for a in range(n):
            for j, (px, py) in enumerate(chips):
                copies.append(pltpu.make_async_remote_copy(
                    src_ref=src[a].at[2 * px + py], dst_ref=dst[a].at[j], send_sem=send_sem.at[a, j],
                    recv_sem=recv_sem.at[a, j], device_id=(px, py, c), device_id_type=MESH))
                copies[-1].start()
        for cp in copies:
            cp.wait()
        for cp in local:
            cp.wait()

    return pl.pallas_call(
        body, name="grad_chip_scatter", in_specs=[ANY] * n, out_specs=[ANY] * n,
        out_shape=[jax.ShapeDtypeStruct(s.shape, s.dtype) for s in sums],
        scratch_shapes=[pltpu.SemaphoreType.DMA((n, 3))] * 2 + [pltpu.SemaphoreType.DMA((n,))])(*sums)


def _slot_sum(name, r):
    _, h, C = r.shape
    tr = _row_tile(h, 256)

    def body(r0, r1, r2, r3, o_ref):
        o_ref[...] = ((r0[...].astype(F32) + r1[...].astype(F32)) + r2[...].astype(F32)) + r3[...].astype(F32)

    slot = lambda k: pl.BlockSpec((None, tr, C), lambda i: (k, i, 0))
    return pl.pallas_call(
        body, name=name, grid=(h // tr,), in_specs=[slot(0), slot(1), slot(2), slot(3)],
        out_specs=pl.BlockSpec((tr, C), lambda i: (i, 0)), out_shape=jax.ShapeDtypeStruct((h, C), F32),
        compiler_params=_params())(r, r, r, r)


def _half_exchange(halves):
    n = len(halves)

    def body(*refs):
        src, dst = refs[:n], refs[n:2 * n]
        send_sem, recv_sem, loc_sem = refs[2 * n:]
        x, y, c, _ = _place()
        local, copies = [], []
        for a in range(n):
            mine = _half_rows(c, dst[a].shape[0])
            local.append(pltpu.make_async_copy(src[a], dst[a].at[mine], loc_sem.at[a]))
            local[-1].start()
            copies.append(pltpu.make_async_remote_copy(
                src_ref=src[a], dst_ref=dst[a].at[mine], send_sem=send_sem.at[a], recv_sem=recv_sem.at[a],
                device_id=(x, y, 1 - c), device_id_type=MESH))
            copies[-1].start()
        for cp in copies:
            cp.wait()
        for cp in local:
            cp.wait()

    return pl.pallas_call(
        body, name="grad_half_exchange", in_specs=[ANY] * n, out_specs=[ANY] * n,
        out_shape=[jax.ShapeDtypeStruct((2 * s.shape[0], s.shape[1]), s.dtype) for s in halves],
        scratch_shapes=[pltpu.SemaphoreType.DMA((n,))] * 3)(*halves)


def _reduce_scatter(grads, tags):
    core = lax.axis_index("c").astype(jnp.int32).reshape(1)
    theirs = _pair_exchange(grads)
    sums = [_pair_sum("pair_sum_" + t, core, g, p) for t, g, p in zip(tags, grads, theirs)]
    slots = _chip_scatter(sums)
    halves = [_slot_sum("slot_sum_" + t, r) for t, r in zip(tags, slots)]
    return _half_exchange(halves)


def _all_reduce_small(v):
    rows = v.shape[0]
    n_dev = 2 * N_CHIPS

    def body(x_ref, out_ref, gat_ref, send_sems, recv_sems, local_sem):
        x, y, c, chips = _place()
        me, sibling = (x, y, c), (x, y, 1 - c)

        def slot(px, py, pc):
            return gat_ref.at[4 * px + 2 * py + pc]

        def copy(k, block, to, src=None):
            return pltpu.make_async_remote_copy(
                src_ref=slot(*block) if src is None else src, dst_ref=slot(*block), send_sem=send_sems.at[k],
                recv_sem=recv_sems.at[k], device_id=to, device_id_type=MESH)

        mine = pltpu.make_async_copy(x_ref, slot(*me), local_sem)
        mine.start()
        first = [copy(0, me, sibling, src=x_ref)]
        first += [copy(1 + j, me, (*chip, c), src=x_ref) for j, chip in enumerate(chips)]
        for cp in first:
            cp.start()
        passed = [copy(4 + j, (*chip, c), sibling) for j, chip in enumerate(chips)]
        for j, chip in enumerate(chips):
            copy(1 + j, (*chip, c), me).wait_recv()
            passed[j].start()
        copy(0, sibling, me).wait_recv()
        for j, chip in enumerate(chips):
            copy(4 + j, (*chip, 1 - c), me).wait_recv()
        for cp in first + passed:
            cp.wait_send()
        mine.wait()
        acc = gat_ref[0]
        for d in range(1, n_dev):
            acc = acc + gat_ref[d]
        out_ref[...] = acc

    vm = pl.BlockSpec(memory_space=pltpu.VMEM)
    return pl.pallas_call(
        body, name="small_grad_all_reduce", in_specs=[vm], out_specs=vm,
        out_shape=jax.ShapeDtypeStruct(v.shape, F32),
        scratch_shapes=[pltpu.VMEM((n_dev, rows, BLOCK), F32), pltpu.SemaphoreType.DMA((7,)),
                        pltpu.SemaphoreType.DMA((7,)), pltpu.SemaphoreType.DMA],
        compiler_params=_params())(v)


def _adamw_math(w, g, m, v):
    m2 = ADAM_B1 * m + (1.0 - ADAM_B1) * g
    v2 = ADAM_B2 * v + (1.0 - ADAM_B2) * (g * g)
    m_hat = m2 / (1.0 - ADAM_B1 ** ADAM_STEP)
    v_hat = v2 / (1.0 - ADAM_B2 ** ADAM_STEP)
    delta = -ADAM_LR * (m_hat / (jnp.sqrt(v_hat) + ADAM_EPS) + ADAM_WD * w)
    return delta, m2, v2


def _adamw_big(name, grads, w, m, v):
    L, R, C = w.shape
    tr = _row_tile(R, 128)
    nrb = R // tr

    def body(*refs):
        g_refs = refs[:L]
        w_ref, m_ref, v_ref, go_ref, d_ref, mo_ref, vo_ref = refs[L:]
        layer = pl.program_id(0)
        g = g_refs[0][...]
        for k in range(1, L):
            g = jnp.where(layer == k, g_refs[k][...], g)
        delta, m2, v2 = _adamw_math(w_ref[...], g, m_ref[...], v_ref[...])
        go_ref[...] = g
        d_ref[...] = delta
        mo_ref[...] = m2
        vo_ref[...] = v2

    def gspec(k):
        return pl.BlockSpec((tr, C), lambda l, i: (jnp.where(l == k, i, (nrb - 1) * (k < l)), 0))

    blk = pl.BlockSpec((None, tr, C), lambda l, i: (l, i, 0))
    shp = jax.ShapeDtypeStruct(w.shape, F32)
    return pl.pallas_call(
        body, name=name, grid=(L, nrb), in_specs=[gspec(k) for k in range(L)] + [blk] * 3, out_specs=[blk] * 4,
        out_shape=[shp] * 4, compiler_params=_params())(*grads, w, m, v)


def _adamw_small(g, w, m, v):
    rows = g.shape[0]
    tr = _row_tile(rows, 512)

    def body(g_ref, w_ref, m_ref, v_ref, d_ref, mo_ref, vo_ref):
        delta, m2, v2 = _adamw_math(w_ref[...], g_ref[...], m_ref[...], v_ref[...])
        d_ref[...] = delta
        mo_ref[...] = m2
        vo_ref[...] = v2

    blk = pl.BlockSpec((tr, BLOCK), lambda i: (i, 0))
    shp = jax.ShapeDtypeStruct(g.shape, F32)
    return pl.pallas_call(
        body, name="adamw_small", grid=(rows // tr,), in_specs=[blk] * 4, out_specs=[blk] * 3, out_shape=[shp] * 3,
        compiler_params=_params())(g, w, m, v)


def _pack(arrays):
    flat = jnp.concatenate([a.reshape(-1) for a in arrays])
    pad = (-flat.shape[0]) % (16 * BLOCK)
    return jnp.pad(flat, (0, pad)).reshape(-1, BLOCK)


def _unpack(packed, like):
    flat = packed.reshape(-1)
    out, off = [], 0
    for a in like:
        out.append(flat[off:off + a.size].reshape(a.shape))
        off += a.size
    return out


BIG = ("w_in", "w_attn_branch", "w_sgu_branch", "w_out", "w_gate", "w_up", "w_down")
SMALL = ("mix_norm", "q_norm", "k_norm", "sinks", "sgu_ln_g", "sgu_ln_b", "w_spatial", "b_spatial", "ffn_norm")
ORDER = ("mix_norm", "w_in", "q_norm", "k_norm", "sinks", "sgu_ln_g", "sgu_ln_b", "w_spatial", "b_spatial",
         "w_attn_branch", "w_sgu_branch", "w_out", "ffn_norm", "w_gate", "w_up", "w_down")


def _rope_tables(seq):
    pos = jnp.arange(seq, dtype=F32)
    inv_freq = jnp.power(10000.0, -jnp.arange(0, HEAD_DIM, 2, dtype=F32) / HEAD_DIM)
    ang = pos[:, None] * inv_freq[None, :]
    cos, sin = jnp.cos(ang), jnp.sin(ang)
    reps = BLOCK // HEAD_DIM
    return (jnp.tile(jnp.concatenate([cos, cos], axis=1), (1, reps)),
            jnp.tile(jnp.concatenate([-sin, sin], axis=1), (1, reps)))


def kernel(x, mix_norm, w_in, q_norm, k_norm, sinks, sgu_ln_g, sgu_ln_b, w_spatial, b_spatial, w_attn_branch, w_sgu_branch, w_out, ffn_norm, w_gate, w_up, w_down, loss_target, m_mix_norm, m_w_in, m_q_norm, m_k_norm, m_sinks, m_sgu_ln_g, m_sgu_ln_b, m_w_spatial, m_b_spatial, m_w_attn_branch, m_w_sgu_branch, m_w_out, m_ffn_norm, m_w_gate, m_w_up, m_w_down, v_mix_norm, v_w_in, v_q_norm, v_k_norm, v_sinks, v_sgu_ln_g, v_sgu_ln_b, v_w_spatial, v_b_spatial, v_w_attn_branch, v_w_sgu_branch, v_w_out, v_ffn_norm, v_w_gate, v_w_up, v_w_down):
    weights = dict(mix_norm=mix_norm, w_in=w_in, q_norm=q_norm, k_norm=k_norm, sinks=sinks, sgu_ln_g=sgu_ln_g,
                   sgu_ln_b=sgu_ln_b, w_spatial=w_spatial, b_spatial=b_spatial, w_attn_branch=w_attn_branch,
                   w_sgu_branch=w_sgu_branch, w_out=w_out, ffn_norm=ffn_norm, w_gate=w_gate, w_up=w_up, w_down=w_down)
    mom1 = dict(mix_norm=m_mix_norm, w_in=m_w_in, q_norm=m_q_norm, k_norm=m_k_norm, sinks=m_sinks,
                sgu_ln_g=m_sgu_ln_g, sgu_ln_b=m_sgu_ln_b, w_spatial=m_w_spatial, b_spatial=m_b_spatial,
                w_attn_branch=m_w_attn_branch, w_sgu_branch=m_w_sgu_branch, w_out=m_w_out, ffn_norm=m_ffn_norm,
                w_gate=m_w_gate, w_up=m_w_up, w_down=m_w_down)
    mom2 = dict(mix_norm=v_mix_norm, w_in=v_w_in, q_norm=v_q_norm, k_norm=v_k_norm, sinks=v_sinks,
                sgu_ln_g=v_sgu_ln_g, sgu_ln_b=v_sgu_ln_b, w_spatial=v_w_spatial, b_spatial=v_b_spatial,
                w_attn_branch=v_w_attn_branch, w_sgu_branch=v_w_sgu_branch, w_out=v_w_out, ffn_norm=v_ffn_norm,
                w_gate=v_w_gate, w_up=v_w_up, w_down=v_w_down)
    xs, target = x[0], loss_target[0]
    S, D = xs.shape
    L = w_in.shape[0]
    AW, KW, SW = N_Q_HEADS * HEAD_DIM, N_KV_HEADS * HEAD_DIM, SGU_GROUPS * BLOCK
    dims = (AW, KW, AW + 2 * KW + 2 * SW, (AW + 2 * KW) // BLOCK)
    cos, sin = _rope_tables(S)
    reps = BLOCK // HEAD_DIM

    gathered = _all_gather([weights[n][l].astype(MXU) for l in range(L) for n in BIG])
    wl = [{n: gathered[l * len(BIG) + i] for i, n in enumerate(BIG)} for l in range(L)]
    sp = [dict(mix_norm=mix_norm[l][None], ffn_norm=ffn_norm[l][None], q_norm=jnp.tile(q_norm[l][None], (1, reps)),
               k_norm=jnp.tile(k_norm[l][None], (1, reps)), sinks=sinks[l][None], sgu_ln_g=sgu_ln_g[l][None],
               sgu_ln_b=sgu_ln_b[l][None], w_spatial=w_spatial[l], b_spatial=b_spatial[l][:, :, None])
          for l in range(L)]

    act, saved = xs, []
    for l in range(L):
        act, sv = _layer_fwd(act, wl[l], sp[l], cos, sin, dims)
        saved.append(sv)
    loss_part, dy, dyb = _loss_head(act, target)
    loss = lax.psum(loss_part[0, 0], ("x", "y", "c"))

    big_g, small_g = [None] * L, [None] * L
    for l in reversed(range(L)):
        dy, dyb, big_g[l], small_g[l] = _layer_bwd(dy, dyb, wl[l], sp[l], saved[l], cos, sin, dims)
    grad_x = dy[None]

    tags = ["%s_%d" % (n, l) for l in range(L) for n in BIG]
    reduced = _reduce_scatter([big_g[l][n] for l in range(L) for n in BIG], tags)
    grads, deltas, new_m, new_v = {}, {}, {}, {}
    for i, n in enumerate(BIG):
        per_layer = [reduced[l * len(BIG) + i] for l in range(L)]
        grads[n], deltas[n], new_m[n], new_v[n] = _adamw_big("adamw_" + n, per_layer, weights[n], mom1[n], mom2[n])

    small_like = [weights[n] for n in SMALL]
    local = [jnp.stack([small_g[l][n].reshape(weights[n].shape[1:]) for l in range(L)]) for n in SMALL]
    g_small = _all_reduce_small(_pack(local))
    d_small, m_small, v_small = _adamw_small(g_small, _pack(small_like), _pack([mom1[n] for n in SMALL]),
                                             _pack([mom2[n] for n in SMALL]))
    for n, g, d, m2, v2 in zip(SMALL, _unpack(g_small, small_like), _unpack(d_small, small_like),
                               _unpack(m_small, small_like), _unpack(v_small, small_like)):
        grads[n], deltas[n], new_m[n], new_v[n] = g, d, m2, v2

    return (loss, grad_x, *[grads[n] for n in ORDER], *[deltas[n] for n in ORDER],
            *[new_m[n] for n in ORDER], *[new_v[n] for n in ORDER])
```

```python
import functools

import jax
import jax.numpy as jnp
from jax import lax
from jax.experimental import pallas as pl
from jax.experimental.pallas import tpu as pltpu

HEAD_DIM = 64
N_Q_HEADS = 16
N_KV_HEADS = 4
SGU_GROUPS = 8
BLOCK = 128
EPS = 1e-6
ADAM_LR = 0.001
ADAM_B1 = 0.9
ADAM_B2 = 0.999
ADAM_EPS = 1e-08
ADAM_WD = 0.01
ADAM_STEP = 10
N_CHIPS = 4
VMEM_LIMIT = 52 * 1024 * 1024

F32 = jnp.float32
MXU = jnp.bfloat16
NN = (((1,), (0,)), ((), ()))
NT = (((1,), (1,)), ((), ()))
TN = (((0,), (0,)), ((), ()))
MESH = pl.DeviceIdType.MESH
ANY = pl.BlockSpec(memory_space=pl.ANY)


def _tile(n, pref):
    if n <= pref:
        return n
    best = None
    for t in range(BLOCK, pref + 1, BLOCK):
        if n % t == 0:
            best = t
    assert best is not None, (n, pref)
    return best


def _params():
    return pltpu.CompilerParams(vmem_limit_bytes=VMEM_LIMIT)


def _mm(name, grid, n_red, operands, specs, pairs, dims, n_extra, out_shapes, out_specs,
        acc_shapes, epilogue):
    n_op = len(operands) - n_extra
    n_out = len(out_shapes)
    n_acc = len(acc_shapes)

    def body(*refs):
        ops = refs[:n_op]
        extra = refs[n_op:n_op + n_extra]
        outs = refs[n_op + n_extra:n_op + n_extra + n_out]
        accs = refs[n_op + n_extra + n_out:]

        def prod(a, b):
            return lax.dot_general(ops[a][...], ops[b][...], dims, preferred_element_type=F32)

        if n_red == 0:
            vals = [None] * n_acc
            for a, b, k in pairs:
                d = prod(a, b)
                vals[k] = d if vals[k] is None else vals[k] + d
            epilogue(vals, extra, outs)
        else:
            axes = list(range(len(grid) - n_red, len(grid)))
            first = pl.program_id(axes[0]) == 0
            last = pl.program_id(axes[0]) == grid[axes[0]] - 1
            for ax in axes[1:]:
                first = jnp.logical_and(first, pl.program_id(ax) == 0)
                last = jnp.logical_and(last, pl.program_id(ax) == grid[ax] - 1)

            @pl.when(first)
            def _():
                for acc in accs:
                    acc[...] = jnp.zeros(acc.shape, F32)

            for a, b, k in pairs:
                accs[k][...] += prod(a, b)

            @pl.when(last)
            def _():
                epilogue([acc[...] for acc in accs], extra, outs)

    scratch = [pltpu.VMEM(s, F32) for s in acc_shapes] if n_red else []
    return pl.pallas_call(
        body, name=name, grid=grid, in_specs=specs, out_specs=out_specs, out_shape=out_shapes,
        scratch_shapes=scratch, compiler_params=_params())(*operands)


def _sigmoid(x):
    return 1.0 / (1.0 + jnp.exp(-x))


_GELU_C = 0.7978845608028654
_GELU_A = 0.044715


def _gelu(x):
    return 0.5 * x * (1.0 + jnp.tanh(_GELU_C * (x + _GELU_A * x * x * x)))


def _gelu_grad(x):
    t = jnp.tanh(_GELU_C * (x + _GELU_A * x * x * x))
    return 0.5 * (1.0 + t) + 0.5 * x * (1.0 - t * t) * _GELU_C * (1.0 + 3.0 * _GELU_A * x * x)


def _rms_fwd(name, x, g):
    S, D = x.shape
    tr = _tile(S, 256)

    def body(x_ref, g_ref, o_ref):
        xv = x_ref[...]
        r = lax.rsqrt(jnp.mean(xv * xv, axis=-1, keepdims=True) + EPS)
        o_ref[...] = (xv * r * g_ref[...]).astype(MXU)

    return pl.pallas_call(
        body, name=name, grid=(S // tr,),
        in_specs=[pl.BlockSpec((tr, D), lambda i: (i, 0)), pl.BlockSpec((1, D), lambda i: (0, 0))],
        out_specs=pl.BlockSpec((tr, D), lambda i: (i, 0)),
        out_shape=jax.ShapeDtypeStruct((S, D), MXU), compiler_params=_params())(x, g)


def _rms_bwd(name, dh, x, g, dres):
    S, D = x.shape
    tr = _tile(S, 256)

    def body(dh_ref, x_ref, g_ref, dres_ref, dx_ref, dxb_ref, dg_ref):
        xv = x_ref[...]
        r = lax.rsqrt(jnp.mean(xv * xv, axis=-1, keepdims=True) + EPS)
        xh = xv * r
        dhv = dh_ref[...]
        dy = dhv * g_ref[...]
        dx = dres_ref[...] + r * (dy - xh * jnp.mean(dy * xh, axis=-1, keepdims=True))
        dx_ref[...] = dx
        dxb_ref[...] = dx.astype(MXU)

        @pl.when(pl.program_id(0) == 0)
        def _():
            dg_ref[...] = jnp.zeros(dg_ref.shape, F32)

        dg_ref[...] += jnp.sum(dhv * xh, axis=0, keepdims=True)

    row = pl.BlockSpec((tr, D), lambda i: (i, 0))
    vec = pl.BlockSpec((1, D), lambda i: (0, 0))
    return pl.pallas_call(
        body, name=name, grid=(S // tr,), in_specs=[row, row, vec, row], out_specs=[row, row, vec],
        out_shape=[jax.ShapeDtypeStruct((S, D), F32), jax.ShapeDtypeStruct((S, D), MXU),
                   jax.ShapeDtypeStruct((1, D), F32)],
        compiler_params=_params())(dh, x, g, dres)


def _loss_head(y, target):
    S, D = y.shape
    tr = _tile(S, 256)

    def body(y_ref, t_ref, loss_ref, dy_ref, dyb_ref):
        d = y_ref[...] - t_ref[...]
        dy = d * (1.0 / D)
        dy_ref[...] = dy
        dyb_ref[...] = dy.astype(MXU)

        @pl.when(pl.program_id(0) == 0)
        def _():
            loss_ref[...] = jnp.zeros(loss_ref.shape, F32)

        loss_ref[...] += (0.5 / D) * jnp.sum(jnp.sum(d * d, axis=-1, keepdims=True), axis=0, keepdims=True)

    row = pl.BlockSpec((tr, D), lambda i: (i, 0))
    return pl.pallas_call(
        body, name="loss_head", grid=(S // tr,), in_specs=[row, row],
        out_specs=[pl.BlockSpec((1, 1), lambda i: (0, 0)), row, row],
        out_shape=[jax.ShapeDtypeStruct((1, 1), F32), jax.ShapeDtypeStruct((S, D), F32),
                   jax.ShapeDtypeStruct((S, D), MXU)],
        compiler_params=_params())(y, target)


def _head_sum(v):
    r = lax.broadcasted_iota(jnp.int32, (BLOCK, BLOCK), 0) // HEAD_DIM
    c = lax.broadcasted_iota(jnp.int32, (BLOCK, BLOCK), 1) // HEAD_DIM
    ones = jnp.where(r == c, 1.0, 0.0).astype(jnp.bfloat16)
    hi = v.astype(jnp.bfloat16)
    lo = (v - hi.astype(F32)).astype(jnp.bfloat16)
    parts = []
    for t in range(v.shape[1] // BLOCK):
        sl = slice(t * BLOCK, (t + 1) * BLOCK)
        parts.append(jnp.dot(hi[:, sl], ones, preferred_element_type=F32)
                     + jnp.dot(lo[:, sl], ones, preferred_element_type=F32))
    return parts[0] if len(parts) == 1 else jnp.concatenate(parts, axis=-1)


def _swap_halves(v):
    w = v.shape[1]
    half = HEAD_DIM // 2
    lane = lax.broadcasted_iota(jnp.int32, v.shape, 1) % HEAD_DIM
    return jnp.where(lane < half, pltpu.roll(v, w - half, 1), pltpu.roll(v, half, 1))


def _norm_rope(xv, gain, cos, sin):
    r = lax.rsqrt(_head_sum(xv * xv) * (1.0 / HEAD_DIM) + EPS)
    xn = xv * r * gain
    return xn * cos + _swap_halves(xn) * sin


def _norm_rope_bwd(dy, xv, gain, cos, sin):
    r = lax.rsqrt(_head_sum(xv * xv) * (1.0 / HEAD_DIM) + EPS)
    xh = xv * r
    dxn = dy * cos + _swap_halves(dy * sin)
    dgain = jnp.sum(dxn * xh, axis=0, keepdims=True)
    dxh = dxn * gain
    dx = r * (dxh - xh * (_head_sum(dxh * xh) * (1.0 / HEAD_DIM)))
    return dx, dgain


def _fold_heads(v):
    acc = v[:, 0:BLOCK]
    for t in range(1, v.shape[1] // BLOCK):
        acc = acc + v[:, t * BLOCK:(t + 1) * BLOCK]
    return acc + pltpu.roll(acc, HEAD_DIM, 1)


def _tile_lanes(v, width):
    return v if width == BLOCK else jnp.tile(v, (1, width // BLOCK))


def _qk_prep(proj, qg, kg, cos, sin, AW, KW):
    S = proj.shape[0]
    tr = _tile(S, 256)
    scale = HEAD_DIM ** -0.5

    def body(q_ref, k_ref, v_ref, qg_ref, kg_ref, cos_ref, sin_ref, qo_ref, ko_ref, vo_ref):
        c, s = cos_ref[...], sin_ref[...]
        q = _norm_rope(q_ref[...], _tile_lanes(qg_ref[...], AW), _tile_lanes(c, AW), _tile_lanes(s, AW))
        k = _norm_rope(k_ref[...], _tile_lanes(kg_ref[...], KW), _tile_lanes(c, KW), _tile_lanes(s, KW))
        qo_ref[...] = (q * scale).astype(MXU)
        ko_ref[...] = k.astype(MXU)
        vo_ref[...] = v_ref[...].astype(MXU)

    assert AW % KW == 0
    vec = pl.BlockSpec((1, BLOCK), lambda i: (0, 0))
    tab = pl.BlockSpec((tr, BLOCK), lambda i: (i, 0))
    return pl.pallas_call(
        body, name="qk_prep", grid=(S // tr,),
        in_specs=[pl.BlockSpec((tr, AW), lambda i: (i, 0)),
                  pl.BlockSpec((tr, KW), lambda i: (i, AW // KW)),
                  pl.BlockSpec((tr, KW), lambda i: (i, AW // KW + 1)), vec, vec, tab, tab],
        out_specs=[pl.BlockSpec((tr, AW), lambda i: (i, 0)), pl.BlockSpec((tr, KW), lambda i: (i, 0)),
                   pl.BlockSpec((tr, KW), lambda i: (i, 0))],
        out_shape=[jax.ShapeDtypeStruct((S, AW), MXU), jax.ShapeDtypeStruct((S, KW), MXU),
                   jax.ShapeDtypeStruct((S, KW), MXU)],
        compiler_params=_params())(proj, proj, proj, qg, kg, cos, sin)


def _attn_probs(n, q, kp, kc, g, sink_ref, qpk):
    hd = HEAD_DIM
    kcat = jnp.concatenate([kp[:, g * hd:(g + 1) * hd], kc[:, g * hd:(g + 1) * hd]], axis=0)
    qs = jnp.concatenate([q[:, (g * qpk + j) * hd:(g * qpk + j + 1) * hd] for j in range(qpk)], axis=0)
    s = lax.dot_general(qs, kcat, NT, preferred_element_type=F32)
    row = lax.broadcasted_iota(jnp.int32, (BLOCK, 2 * BLOCK), 0)
    col = lax.broadcasted_iota(jnp.int32, (BLOCK, 2 * BLOCK), 1)
    ok = (col > row) & (col <= row + BLOCK) & ((col >= BLOCK) | (n > 0))
    s = jnp.where(jnp.concatenate([ok] * qpk, axis=0), s, -1e30)
    sk = jnp.concatenate([jnp.full((BLOCK, 1), sink_ref[0, g * qpk + j], F32) for j in range(qpk)], axis=0)
    m = jnp.maximum(jnp.max(s, axis=-1, keepdims=True), sk)
    e = jnp.exp(s - m)
    es = jnp.exp(sk - m)
    z = jnp.sum(e, axis=-1, keepdims=True) + es
    return e / z, es / z, qs, kcat


def _attn_fwd(qr, kr, vb, sinks):
    S, AW = qr.shape
    KW = kr.shape[1]
    nb = S // BLOCK
    nkv = KW // HEAD_DIM
    qpk = AW // KW
    hd = HEAD_DIM

    def body(sink_ref, q_ref, kp_ref, kc_ref, vp_ref, vc_ref, o_ref):
        n = pl.program_id(0)
        q, kp, kc, vp, vc = q_ref[...], kp_ref[...], kc_ref[...], vp_ref[...], vc_ref[...]
        outs = [None] * (nkv * qpk)
        for g in range(nkv):
            p, _, _, _ = _attn_probs(n, q, kp, kc, g, sink_ref, qpk)
            vcat = jnp.concatenate([vp[:, g * hd:(g + 1) * hd], vc[:, g * hd:(g + 1) * hd]], axis=0)
            o = jnp.dot(p.astype(MXU), vcat, preferred_element_type=F32)
            for j in range(qpk):
                outs[g * qpk + j] = o[j * BLOCK:(j + 1) * BLOCK]
        o_ref[...] = jnp.concatenate(outs, axis=-1).astype(MXU)

    cur = lambda n: (n, 0)
    prev = lambda n: (jnp.maximum(n - 1, 0), 0)
    return pl.pallas_call(
        body, name="attn_fwd", grid=(nb,),
        in_specs=[pl.BlockSpec(memory_space=pltpu.SMEM), pl.BlockSpec((BLOCK, AW), cur),
                  pl.BlockSpec((BLOCK, KW), prev), pl.BlockSpec((BLOCK, KW), cur),
                  pl.BlockSpec((BLOCK, KW), prev), pl.BlockSpec((BLOCK, KW), cur)],
        out_specs=pl.BlockSpec((BLOCK, AW), cur),
        out_shape=jax.ShapeDtypeStruct((S, AW), MXU), compiler_params=_params())(sinks, qr, kr, kr, vb, vb)


def _attn_bwd(qr, kr, vb, sinks, dattn):
    S, AW = qr.shape
    KW = kr.shape[1]
    nb = S // BLOCK
    nkv = KW // HEAD_DIM
    qpk = AW // KW
    hd = HEAD_DIM
    scale = HEAD_DIM ** -0.5

    def body(sink_ref, q_ref, kp_ref, kc_ref, vp_ref, vc_ref, do_ref,
             dq_ref, dkp_ref, dkc_ref, dvp_ref, dvc_ref, dsink_ref):
        n = pl.program_id(0)
        q, kp, kc, vp, vc = q_ref[...], kp_ref[...], kc_ref[...], vp_ref[...], vc_ref[...]
        do = do_ref[...].astype(MXU)
        lane = lax.broadcasted_iota(jnp.int32, (1, BLOCK), 1)
        dsink = jnp.zeros((1, BLOCK), F32)
        dqs = [None] * (nkv * qpk)
        dkps, dkcs, dvps, dvcs = [], [], [], []
        for g in range(nkv):
            p, psink, qs, kcat = _attn_probs(n, q, kp, kc, g, sink_ref, qpk)
            vcat = jnp.concatenate([vp[:, g * hd:(g + 1) * hd], vc[:, g * hd:(g + 1) * hd]], axis=0)
            dos = jnp.concatenate([do[:, (g * qpk + j) * hd:(g * qpk + j + 1) * hd] for j in range(qpk)], axis=0)
            dp = lax.dot_general(dos, vcat, NT, preferred_element_type=F32)
            dv = lax.dot_general(p.astype(MXU), dos, TN, preferred_element_type=F32)
            delta = jnp.sum(p * dp, axis=-1, keepdims=True)
            ds = (p * (dp - delta)).astype(MXU)
            dsk = -psink * delta
            dq = jnp.dot(ds, kcat, preferred_element_type=F32) * scale
            dk = lax.dot_general(ds, qs, TN, preferred_element_type=F32)
            for j in range(qpk):
                dqs[g * qpk + j] = dq[j * BLOCK:(j + 1) * BLOCK]
                tot = jnp.sum(dsk[j * BLOCK:(j + 1) * BLOCK], axis=0, keepdims=True)
                dsink = dsink + jnp.where(lane == g * qpk + j, tot, 0.0)
            dkps.append(dk[:BLOCK])
            dkcs.append(dk[BLOCK:])
            dvps.append(dv[:BLOCK])
            dvcs.append(dv[BLOCK:])
        dq_ref[...] = jnp.concatenate(dqs, axis=-1)
        dkp_ref[...] = jnp.concatenate(dkps, axis=-1)
        dkc_ref[...] = jnp.concatenate(dkcs, axis=-1)
        dvp_ref[...] = jnp.concatenate(dvps, axis=-1)
        dvc_ref[...] = jnp.concatenate(dvcs, axis=-1)

        @pl.when(n == 0)
        def _():
            dsink_ref[...] = jnp.zeros(dsink_ref.shape, F32)

        dsink_ref[...] += dsink

    cur = lambda n: (n, 0)
    prev = lambda n: (jnp.maximum(n - 1, 0), 0)
    kv = jax.ShapeDtypeStruct((S, KW), F32)
    kvspec = pl.BlockSpec((BLOCK, KW), cur)
    return pl.pallas_call(
        body, name="attn_bwd", grid=(nb,),
        in_specs=[pl.BlockSpec(memory_space=pltpu.SMEM), pl.BlockSpec((BLOCK, AW), cur),
                  pl.BlockSpec((BLOCK, KW), prev), kvspec, pl.BlockSpec((BLOCK, KW), prev), kvspec,
                  pl.BlockSpec((BLOCK, AW), cur)],
        out_specs=[pl.BlockSpec((BLOCK, AW), cur), kvspec, kvspec, kvspec, kvspec,
                   pl.BlockSpec((1, BLOCK), lambda n: (0, 0))],
        out_shape=[jax.ShapeDtypeStruct((S, AW), F32), kv, kv, kv, kv, jax.ShapeDtypeStruct((1, BLOCK), F32)],
        compiler_params=_params())(sinks, qr, kr, kr, vb, vb, dattn)


def _qk_prep_bwd(proj, qg, kg, cos, sin, dq, dkp, dkc, dvp, dvc, AW, KW):
    S = proj.shape[0]
    nb = S // BLOCK

    def body(q_ref, k_ref, qg_ref, kg_ref, cos_ref, sin_ref, dq_ref, dkp_ref, dkc_ref, dvp_ref, dvc_ref,
             o_ref, dqg_ref, dkg_ref):
        n = pl.program_id(0)
        c, s = cos_ref[...], sin_ref[...]
        has_next = jnp.where(n < nb - 1, 1.0, 0.0)
        dk = dkc_ref[...] + has_next * dkp_ref[...]
        dv = dvc_ref[...] + has_next * dvp_ref[...]
        dxq, dqg = _norm_rope_bwd(dq_ref[...], q_ref[...], _tile_lanes(qg_ref[...], AW),
                                  _tile_lanes(c, AW), _tile_lanes(s, AW))
        dxk, dkg = _norm_rope_bwd(dk, k_ref[...], _tile_lanes(kg_ref[...], KW),
                                  _tile_lanes(c, KW), _tile_lanes(s, KW))
        o_ref[...] = jnp.concatenate([dxq, dxk, dv], axis=-1).astype(MXU)

        @pl.when(n == 0)
        def _():
            dqg_ref[...] = jnp.zeros(dqg_ref.shape, F32)
            dkg_ref[...] = jnp.zeros(dkg_ref.shape, F32)

        dqg_ref[...] += _fold_heads(dqg)
        dkg_ref[...] += _fold_heads(dkg)

    cur = lambda n: (n, 0)
    nxt = lambda n: (jnp.minimum(n + 1, nb - 1), 0)
    vec = pl.BlockSpec((1, BLOCK), lambda n: (0, 0))
    tab = pl.BlockSpec((BLOCK, BLOCK), cur)
    return pl.pallas_call(
        body, name="qk_prep_bwd", grid=(nb,),
        in_specs=[pl.BlockSpec((BLOCK, AW), cur), pl.BlockSpec((BLOCK, KW), lambda n: (n, AW // KW)),
                  vec, vec, tab, tab, pl.BlockSpec((BLOCK, AW), cur),
                  pl.BlockSpec((BLOCK, KW), nxt), pl.BlockSpec((BLOCK, KW), cur),
                  pl.BlockSpec((BLOCK, KW), nxt), pl.BlockSpec((BLOCK, KW), cur)],
        out_specs=[pl.BlockSpec((BLOCK, AW + 2 * KW), cur), vec, vec],
        out_shape=[jax.ShapeDtypeStruct((S, AW + 2 * KW), MXU), jax.ShapeDtypeStruct((1, BLOCK), F32),
                   jax.ShapeDtypeStruct((1, BLOCK), F32)],
        compiler_params=_params())(proj, proj, qg, kg, cos, sin, dq, dkp, dkc, dvp, dvc)


def _sgu_tile(pu_ref, pv_ref, lng_ref, lnb_ref, w_ref, b_ref):
    u = _gelu(pu_ref[...])
    v = _gelu(pv_ref[...])
    mu = jnp.mean(v, axis=-1, keepdims=True)
    vc = v - mu
    r = lax.rsqrt(jnp.mean(vc * vc, axis=-1, keepdims=True) + EPS)
    xh = vc * r
    vn = xh * lng_ref[...] + lnb_ref[...]
    row = lax.broadcasted_iota(jnp.int32, (BLOCK, BLOCK), 0)
    col = lax.broadcasted_iota(jnp.int32, (BLOCK, BLOCK), 1)
    tri = row >= col
    w = jnp.where(tri, w_ref[...], 0.0).astype(MXU)
    s = jnp.dot(w, vn.astype(MXU), preferred_element_type=F32) + b_ref[...]
    return u, xh, r, vn, w, s, tri


def _sgu_specs(u0):
    G = SGU_GROUPS
    return [pl.BlockSpec((BLOCK, BLOCK), lambda g, n: (n, u0 + g)),
            pl.BlockSpec((BLOCK, BLOCK), lambda g, n: (n, u0 + G + g)),
            pl.BlockSpec((1, BLOCK), lambda g, n: (0, g)), pl.BlockSpec((1, BLOCK), lambda g, n: (0, g)),
            pl.BlockSpec((None, BLOCK, BLOCK), lambda g, n: (g, 0, 0)),
            pl.BlockSpec((None, BLOCK, 1), lambda g, n: (g, 0, 0))]


def _sgu_fwd(proj, lng, lnb, ws, bs, u0):
    S = proj.shape[0]
    G = SGU_GROUPS

    def body(pu_ref, pv_ref, lng_ref, lnb_ref, w_ref, b_ref, o_ref):
        u, _, _, _, _, s, _ = _sgu_tile(pu_ref, pv_ref, lng_ref, lnb_ref, w_ref, b_ref)
        o_ref[...] = (u * s).astype(MXU)

    return pl.pallas_call(
        body, name="sgu_fwd", grid=(G, S // BLOCK), in_specs=_sgu_specs(u0),
        out_specs=pl.BlockSpec((BLOCK, BLOCK), lambda g, n: (n, g)),
        out_shape=jax.ShapeDtypeStruct((S, G * BLOCK), MXU), compiler_params=_params())(proj, proj, lng, lnb, ws, bs)


def _sgu_bwd(proj, lng, lnb, ws, bs, dsgu, u0):
    S = proj.shape[0]
    G = SGU_GROUPS

    def body(pu_ref, pv_ref, lng_ref, lnb_ref, w_ref, b_ref, do_ref,
             dpu_ref, dpv_ref, dw_ref, db_ref, dlng_ref, dlnb_ref):
        u, xh, r, vn, w, s, tri = _sgu_tile(pu_ref, pv_ref, lng_ref, lnb_ref, w_ref, b_ref)
        do = do_ref[...]
        dpu_ref[...] = (do * s * _gelu_grad(pu_ref[...])).astype(MXU)
        ds = do * u
        dsb = ds.astype(MXU)
        dw = jnp.where(tri, lax.dot_general(dsb, vn.astype(MXU), NT, preferred_element_type=F32), 0.0)
        dvn = lax.dot_general(w, dsb, TN, preferred_element_type=F32)
        dxh = dvn * lng_ref[...]
        dv = r * (dxh - jnp.mean(dxh, axis=-1, keepdims=True) - xh * jnp.mean(dxh * xh, axis=-1, keepdims=True))
        dpv_ref[...] = (dv * _gelu_grad(pv_ref[...])).astype(MXU)

        @pl.when(pl.program_id(1) == 0)
        def _():
            dw_ref[...] = jnp.zeros(dw_ref.shape, F32)
            db_ref[...] = jnp.zeros(db_ref.shape, F32)
            dlng_ref[...] = jnp.zeros(dlng_ref.shape, F32)
            dlnb_ref[...] = jnp.zeros(dlnb_ref.shape, F32)

        dw_ref[...] += dw
        db_ref[...] += jnp.sum(ds, axis=-1, keepdims=True)
        dlng_ref[...] += jnp.sum(dvn * xh, axis=0, keepdims=True)
        dlnb_ref[...] += jnp.sum(dvn, axis=0, keepdims=True)

    tile = lambda off: pl.BlockSpec((BLOCK, BLOCK), lambda g, n: (n, off + g))
    vec = pl.BlockSpec((1, BLOCK), lambda g, n: (0, g))
    half = jax.ShapeDtypeStruct((S, G * BLOCK), MXU)
    return pl.pallas_call(
        body, name="sgu_bwd", grid=(G, S // BLOCK), in_specs=_sgu_specs(u0) + [tile(0)],
        out_specs=[tile(0), tile(0), pl.BlockSpec((None, BLOCK, BLOCK), lambda g, n: (g, 0, 0)),
                   pl.BlockSpec((None, BLOCK, 1), lambda g, n: (g, 0, 0)), vec, vec],
        out_shape=[half, half, jax.ShapeDtypeStruct((G, BLOCK, BLOCK), F32),
                   jax.ShapeDtypeStruct((G, BLOCK, 1), F32),
                   jax.ShapeDtypeStruct((1, G * BLOCK), F32), jax.ShapeDtypeStruct((1, G * BLOCK), F32)],
        compiler_params=_params())(proj, proj, lng, lnb, ws, bs, dsgu)


def _store_f32(vals, extra, outs):
    for v, o in zip(vals, outs):
        o[...] = v


def _store_mxu(vals, extra, outs):
    for v, o in zip(vals, outs):
        o[...] = v.astype(MXU)


def _proj_in(h, w):
    S, D = h.shape
    Ns = w.shape[2]
    tm, tn = _tile(S, 1024), _tile(Ns, 1024)
    npb = Ns // tn
    return _mm("proj_in", (S // tm, N_CHIPS, npb), 0, [h, w],
               [pl.BlockSpec((tm, D), lambda i, s, j: (i, 0)), pl.BlockSpec((None, D, tn), lambda i, s, j: (s, 0, j))],
               [(0, 1, 0)], NN, 0, [jax.ShapeDtypeStruct((S, N_CHIPS * Ns), F32)],
               [pl.BlockSpec((tm, tn), lambda i, s, j: (i, s * npb + j))], [None], _store_f32)[0]


def _branches(attn, sgu, wa, ws, proj, gate0):
    S, AW = attn.shape
    SW = sgu.shape[1]
    Nb = wa.shape[2]
    D = N_CHIPS * Nb
    tm = _tile(S, 512)
    assert gate0 % Nb == 0
    ga, gb = gate0 // Nb, (gate0 + D) // Nb

    def epilogue(vals, extra, outs):
        a, b = vals
        outs[0][...] = (_sigmoid(extra[0][...]) * a + _sigmoid(extra[1][...]) * b).astype(MXU)
        outs[1][...] = a
        outs[2][...] = b

    tile = pl.BlockSpec((tm, Nb), lambda i, s: (i, s))
    wspec = lambda k: pl.BlockSpec((None, k, Nb), lambda i, s: (s, 0, 0))
    f = jax.ShapeDtypeStruct((S, D), F32)
    return _mm("branches", (S // tm, N_CHIPS), 0, [attn, sgu, wa, ws, proj, proj],
               [pl.BlockSpec((tm, AW), lambda i, s: (i, 0)), pl.BlockSpec((tm, SW), lambda i, s: (i, 0)),
                wspec(AW), wspec(SW), pl.BlockSpec((tm, Nb), lambda i, s: (i, ga + s)),
                pl.BlockSpec((tm, Nb), lambda i, s: (i, gb + s))],
               [(0, 2, 0), (1, 3, 1)], NN, 2, [jax.ShapeDtypeStruct((S, D), MXU), f, f], [tile] * 3,
               [None, None], epilogue)


def _rows_mm(name, a, w, res):
    S = a.shape[0]
    _, K, N = w.shape
    tm, tn = _tile(S, 1024), _tile(N, 1024)

    def epilogue(vals, extra, outs):
        outs[0][...] = extra[0][...] + vals[0]

    out = pl.BlockSpec((tm, tn), lambda i, j, s: (i, j))
    return _mm(name, (S // tm, N // tn, N_CHIPS), 1, [a, w, res],
               [pl.BlockSpec((tm, K), lambda i, j, s: (i, s)), pl.BlockSpec((None, K, tn), lambda i, j, s: (s, 0, j)), out],
               [(0, 1, 0)], NN, 1, [jax.ShapeDtypeStruct((S, N), F32)], [out], [(tm, tn)], epilogue)[0]


def _gate_up(h2, wg, wu):
    S, D = h2.shape
    Nf = wg.shape[2]
    tm, tk = _tile(S, 512), _tile(D, 512)

    def epilogue(vals, extra, outs):
        g, u = vals
        outs[0][...] = g
        outs[1][...] = u
        outs[2][...] = (g * _sigmoid(g) * u).astype(MXU)

    w = pl.BlockSpec((None, tk, Nf), lambda i, s, k: (s, k, 0))
    o = pl.BlockSpec((tm, Nf), lambda i, s, k: (i, s))
    f = jax.ShapeDtypeStruct((S, N_CHIPS * Nf), F32)
    return _mm("gate_up", (S // tm, N_CHIPS, D // tk), 1, [h2, wg, wu],
               [pl.BlockSpec((tm, tk), lambda i, s, k: (i, k)), w, w], [(0, 1, 0), (0, 2, 1)], NN, 0,
               [f, f, jax.ShapeDtypeStruct((S, N_CHIPS * Nf), MXU)], [o, o, o], [(tm, Nf)] * 2, epilogue)


def _down_bwd(dyb, wd, g, u):
    S, D = dyb.shape
    Kf = wd.shape[1]
    tm = _tile(S, 512)

    def epilogue(vals, extra, outs):
        da, gv, uv = vals[0], extra[0][...], extra[1][...]
        sg = _sigmoid(gv)
        outs[0][...] = (da * uv * sg * (1.0 + gv * (1.0 - sg))).astype(MXU)
        outs[1][...] = (da * gv * sg).astype(MXU)

    t = pl.BlockSpec((tm, Kf), lambda i, s: (i, s))
    o = jax.ShapeDtypeStruct((S, N_CHIPS * Kf), MXU)
    return _mm("down_bwd", (S // tm, N_CHIPS), 0, [dyb, wd, g, u],
               [pl.BlockSpec((tm, D), lambda i, s: (i, 0)), pl.BlockSpec((None, Kf, D), lambda i, s: (s, 0, 0)), t, t],
               [(0, 1, 0)], NT, 2, [o, o], [t, t], [None], epilogue)


def _out_bwd(dxb, wo, proj, ba, bb, gate0):
    S, D = dxb.shape
    Ko = wo.shape[1]
    tm = _tile(S, 512)
    assert gate0 % Ko == 0
    ga, gb = gate0 // Ko, (gate0 + D) // Ko

    def epilogue(vals, extra, outs):
        dm = vals[0]
        sa, sb = _sigmoid(extra[0][...]), _sigmoid(extra[1][...])
        outs[0][...] = (dm * sa).astype(MXU)
        outs[1][...] = (dm * sb).astype(MXU)
        outs[2][...] = (dm * extra[2][...] * sa * (1.0 - sa)).astype(MXU)
        outs[3][...] = (dm * extra[3][...] * sb * (1.0 - sb)).astype(MXU)

    t = pl.BlockSpec((tm, Ko), lambda i, s: (i, s))
    o = jax.ShapeDtypeStruct((S, D), MXU)
    return _mm("out_bwd", (S // tm, N_CHIPS), 0, [dxb, wo, proj, proj, ba, bb],
               [pl.BlockSpec((tm, D), lambda i, s: (i, 0)), pl.BlockSpec((None, Ko, D), lambda i, s: (s, 0, 0)),
                pl.BlockSpec((tm, Ko), lambda i, s: (i, ga + s)), pl.BlockSpec((tm, Ko), lambda i, s: (i, gb + s)), t, t],
               [(0, 1, 0)], NT, 4, [o] * 4, [t] * 4, [None], epilogue)


def _dx_cols(name, terms, n_out):
    S = terms[0][0].shape[0]
    _, K, Ns = terms[0][1].shape
    tm, tko, tn = _tile(S, 1024), _tile(K, 1024), _tile(Ns, 1408)
    npb = Ns // tn
    operands, specs, pairs = [], [], []
    for t, (dy, w, k) in enumerate(terms):
        assert w.shape == (N_CHIPS, K, Ns)
        operands += [dy, w]
        specs += [pl.BlockSpec((tm, tn), lambda i, jk, s, jn: (i, s * npb + jn)),
                  pl.BlockSpec((None, tko, tn), lambda i, jk, s, jn: (s, jk, jn))]
        pairs.append((2 * t, 2 * t + 1, k))
    out = pl.BlockSpec((tm, tko), lambda i, jk, s, jn: (i, jk))
    return _mm(name, (S // tm, K // tko, N_CHIPS, npb), 2, operands, specs, pairs, NT, 0,
               [jax.ShapeDtypeStruct((S, K), F32)] * n_out, [out] * n_out, [(tm, tko)] * n_out, _store_f32)


def _dw_cols(name, a, dy):
    S, K = a.shape
    Ns = dy.shape[1] // N_CHIPS
    tk, tn = _tile(K, 512), _tile(Ns, 1408)
    npb = Ns // tn
    return _mm(name, (K // tk, N_CHIPS, npb), 0, [a, dy],
               [pl.BlockSpec((S, tk), lambda jk, s, jn: (0, jk)), pl.BlockSpec((S, tn), lambda jk, s, jn: (0, s * npb + jn))],
               [(0, 1, 0)], TN, 0, [jax.ShapeDtypeStruct((N_CHIPS, K, Ns), MXU)],
               [pl.BlockSpec((None, tk, tn), lambda jk, s, jn: (s, jk, jn))], [None], _store_mxu)[0]


def _dw_rows(name, a, dy):
    S = a.shape[0]
    K = a.shape[1] // N_CHIPS
    N = dy.shape[1]
    tk, tn = _tile(K, 1408), _tile(N, 1024)
    nkb = K // tk
    return _mm(name, (N_CHIPS, nkb, N // tn), 0, [a, dy],
               [pl.BlockSpec((S, tk), lambda s, jk, jn: (0, s * nkb + jk)), pl.BlockSpec((S, tn), lambda s, jk, jn: (0, jn))],
               [(0, 1, 0)], TN, 0, [jax.ShapeDtypeStruct((N_CHIPS, K, N), MXU)],
               [pl.BlockSpec((None, tk, tn), lambda s, jk, jn: (s, jk, jn))], [None], _store_mxu)[0]


def _layer_fwd(x, w, sp, cos, sin, dims):
    AW, KW, gate0, u0 = dims
    h = _rms_fwd("mix_norm", x, sp["mix_norm"])
    proj = _proj_in(h, w["w_in"])
    qr, kr, vb = _qk_prep(proj, sp["q_norm"], sp["k_norm"], cos, sin, AW, KW)
    attn = _attn_fwd(qr, kr, vb, sp["sinks"])
    sgu = _sgu_fwd(proj, sp["sgu_ln_g"], sp["sgu_ln_b"], sp["w_spatial"], sp["b_spatial"], u0)
    merged, ba, bb = _branches(attn, sgu, w["w_attn_branch"], w["w_sgu_branch"], proj, gate0)
    x1 = _rows_mm("out_proj", merged, w["w_out"], x)
    h2 = _rms_fwd("ffn_norm", x1, sp["ffn_norm"])
    g, u, act = _gate_up(h2, w["w_gate"], w["w_up"])
    x2 = _rows_mm("down_proj", act, w["w_down"], x1)
    saved = dict(x=x, h=h, proj=proj, qr=qr, kr=kr, vb=vb, attn=attn, sgu=sgu, merged=merged, ba=ba, bb=bb,
                 x1=x1, h2=h2, g=g, u=u, act=act)
    return x2, saved


def _layer_bwd(dy, dyb, w, sp, sv, cos, sin, dims):
    AW, KW, gate0, u0 = dims
    big, small = {}, {}
    dg, du = _down_bwd(dyb, w["w_down"], sv["g"], sv["u"])
    big["w_down"] = _dw_rows("dw_down", sv["act"], dyb)
    big["w_gate"] = _dw_cols("dw_gate", sv["h2"], dg)
    big["w_up"] = _dw_cols("dw_up", sv["h2"], du)
    dh2 = _dx_cols("dh2", [(dg, w["w_gate"], 0), (du, w["w_up"], 0)], 1)[0]
    dx1, dx1b, small["ffn_norm"] = _rms_bwd("ffn_norm_bwd", dh2, sv["x1"], sp["ffn_norm"], dy)
    dba, dbb, dgla, dglb = _out_bwd(dx1b, w["w_out"], sv["proj"], sv["ba"], sv["bb"], gate0)
    big["w_out"] = _dw_rows("dw_out", sv["merged"], dx1b)
    dattn, dsgu = _dx_cols("dbranch_in", [(dba, w["w_attn_branch"], 0), (dbb, w["w_sgu_branch"], 1)], 2)
    big["w_attn_branch"] = _dw_cols("dw_attn_branch", sv["attn"], dba)
    big["w_sgu_branch"] = _dw_cols("dw_sgu_branch", sv["sgu"], dbb)
    dpu, dpv, small["w_spatial"], db, small["sgu_ln_g"], small["sgu_ln_b"] = _sgu_bwd(
        sv["proj"], sp["sgu_ln_g"], sp["sgu_ln_b"], sp["w_spatial"], sp["b_spatial"], dsgu, u0)
    small["b_spatial"] = db[:, :, 0]
    dq, dkp, dkc, dvp, dvc, dsink = _attn_bwd(sv["qr"], sv["kr"], sv["vb"], sp["sinks"], dattn)
    small["sinks"] = dsink[:, :sp["sinks"].shape[1]]
    dqkv, dqg, dkg = _qk_prep_bwd(sv["proj"], sp["q_norm"], sp["k_norm"], cos, sin, dq, dkp, dkc, dvp, dvc, AW, KW)
    small["q_norm"] = dqg[:, :HEAD_DIM]
    small["k_norm"] = dkg[:, :HEAD_DIM]
    dproj = jnp.concatenate([dqkv, dpu, dpv, dgla, dglb], axis=1)
    big["w_in"] = _dw_cols("dw_in", sv["h"], dproj)
    dh = _dx_cols("dh", [(dproj, w["w_in"], 0)], 1)[0]
    dx, dxb, small["mix_norm"] = _rms_bwd("mix_norm_bwd", dh, sv["x"], sp["mix_norm"], dx1)
    return dx, dxb, big, small


def _place():
    x, y, c = lax.axis_index("x"), lax.axis_index("y"), lax.axis_index("c")
    chips = [(1 - x, y), (x, 1 - y), (1 - x, 1 - y)]
    return x, y, c, chips


def _half_rows(c, rows):
    h = rows // 2
    assert h % 16 == 0
    return pl.ds(pl.multiple_of(c * h, 16), h)


def _row_tile(rows, pref):
    best = None
    for t in range(16, min(rows, pref) + 1, 16):
        if rows % t == 0:
            best = t
    assert best is not None, rows
    return best


def _cast_own(name, chip, w, layer):
    _, R, C = w.shape
    tr = _row_tile(R, 256)

    def body(chip_ref, w_ref, o_ref):
        o_ref[...] = w_ref[...].astype(MXU)

    return pl.pallas_call(
        body, name=name, out_shape=jax.ShapeDtypeStruct((N_CHIPS, R, C), MXU),
        grid_spec=pltpu.PrefetchScalarGridSpec(
            num_scalar_prefetch=1, grid=(R // tr,),
            in_specs=[pl.BlockSpec((None, tr, C), lambda i, chip_ref: (layer, i, 0))],
            out_specs=pl.BlockSpec((None, tr, C), lambda i, chip_ref: (chip_ref[0], i, 0))),
        compiler_params=_params())(chip, w)


def _all_gather(bufs):
    n = len(bufs)

    def body(*refs):
        dst = refs[n:2 * n]
        ici_send, ici_recv, d2d_send, d2d_recv = refs[2 * n:]
        x, y, c, chips = _place()
        me = 2 * x + y
        sibling = (x, y, 1 - c)

        def ici(a, j, block, rows):
            px, py = chips[j]
            return pltpu.make_async_remote_copy(
                src_ref=dst[a].at[block, rows], dst_ref=dst[a].at[block, rows], send_sem=ici_send.at[a, j],
                recv_sem=ici_recv.at[a, j], device_id=(px, py, c), device_id_type=MESH)

        def d2d(a, j, rows):
            px, py = chips[j]
            blk = dst[a].at[2 * px + py, rows]
            return pltpu.make_async_remote_copy(
                src_ref=blk, dst_ref=blk, send_sem=d2d_send.at[a, j], recv_sem=d2d_recv.at[a, j],
                device_id=sibling, device_id_type=MESH)

        sends = []
        for a in range(n):
            mine = _half_rows(c, dst[a].shape[1])
            for j in range(3):
                sends.append(ici(a, j, me, mine))
                sends[-1].start()
        for a in range(n):
            mine = _half_rows(c, dst[a].shape[1])
            for j in range(3):
                px, py = chips[j]
                ici(a, j, 2 * px + py, mine).wait_recv()
                sends.append(d2d(a, j, mine))
                sends[-1].start()
        for a in range(n):
            other = _half_rows(1 - c, dst[a].shape[1])
            for j in range(3):
                d2d(a, j, other).wait_recv()
        for cp in sends:
            cp.wait_send()

    return pl.pallas_call(
        body, name="gather_weights", in_specs=[ANY] * n, out_specs=[ANY] * n,
        out_shape=[jax.ShapeDtypeStruct(b.shape, b.dtype) for b in bufs],
        input_output_aliases={a: a for a in range(n)},
        scratch_shapes=[pltpu.SemaphoreType.DMA((n, 3))] * 4)(*bufs)


def _pair_exchange(grads):
    n = len(grads)

    def body(*refs):
        src, dst = refs[:n], refs[n:2 * n]
        send_sem, recv_sem = refs[2 * n:]
        x, y, c, _ = _place()
        copies = []
        for a in range(n):
            theirs = _half_rows(1 - c, src[a].shape[1])
            copies.append(pltpu.make_async_remote_copy(
                src_ref=src[a].at[:, theirs], dst_ref=dst[a], send_sem=send_sem.at[a], recv_sem=recv_sem.at[a],
                device_id=(x, y, 1 - c), device_id_type=MESH))
            copies[-1].start()
        for cp in copies:
            cp.wait()

    out_shape = [jax.ShapeDtypeStruct((g.shape[0], g.shape[1] // 2, g.shape[2]), g.dtype) for g in grads]
    return pl.pallas_call(
        body, name="grad_pair_exchange", in_specs=[ANY] * n, out_specs=[ANY] * n, out_shape=out_shape,
        scratch_shapes=[pltpu.SemaphoreType.DMA((n,))] * 2)(*grads)


def _pair_sum(name, core, g, p):
    _, h, C = p.shape
    tr = _row_tile(h, 256)
    nrb = h // tr

    def body(core_ref, g_ref, p_ref, o_ref):
        o_ref[...] = (g_ref[...].astype(F32) + p_ref[...].astype(F32)).astype(o_ref.dtype)

    spec = pl.BlockSpec((None, tr, C), lambda s, i, core_ref: (s, i, 0))
    return pl.pallas_call(
        body, name=name, out_shape=jax.ShapeDtypeStruct(p.shape, p.dtype),
        grid_spec=pltpu.PrefetchScalarGridSpec(
            num_scalar_prefetch=1, grid=(N_CHIPS, nrb),
            in_specs=[pl.BlockSpec((None, tr, C), lambda s, i, core_ref: (s, core_ref[0] * nrb + i, 0)), spec],
            out_specs=spec),
        compiler_params=_params())(core, g, p)


def _chip_scatter(sums):
    n = len(sums)

    def body(*refs):
        src, dst = refs[:n], refs[n:2 * n]
        send_sem, recv_sem = refs[2 * n:]
        x, y, c, chips = _place()
        copies = []
        for a in range(n):
            for j, (px, py) in enumerate(chips):
                copies.append(pltpu.make_async_remote_copy(
                    src_ref=src[a].at[2 * px + py], dst_ref=dst[a].at[j], send_sem=send_sem.at[a, j],
                    recv_sem=recv_sem.at[a, j], device_id=(px, py, c), device_id_type=MESH))
                copies[-1].start()
        for cp in copies:
            cp.wait()

    return pl.pallas_call(
        body, name="grad_chip_scatter", in_specs=[ANY] * n, out_specs=[ANY] * n,
        out_shape=[jax.ShapeDtypeStruct((3,) + s.shape[1:], s.dtype) for s in sums],
        scratch_shapes=[pltpu.SemaphoreType.DMA((n, 3))] * 2)(*sums)


def _slot_sum(name, place, slots, sums):
    _, h, C = slots.shape
    tr = _row_tile(h, 256)
    nrb = h // tr

    def body(place_ref, r0, r1, r2, own, o_ref):
        o_ref[...] = ((r0[...].astype(F32) + r1[...].astype(F32)) + r2[...].astype(F32)) + own[...].astype(F32)

    slot = lambda k: pl.BlockSpec((None, tr, C), lambda i, place_ref: (k, i, 0))
    return pl.pallas_call(
        body, name=name, out_shape=jax.ShapeDtypeStruct((2 * h, C), F32),
        grid_spec=pltpu.PrefetchScalarGridSpec(
            num_scalar_prefetch=1, grid=(nrb,),
            in_specs=[slot(0), slot(1), slot(2),
                      pl.BlockSpec((None, tr, C), lambda i, place_ref: (place_ref[0], i, 0))],
            out_specs=pl.BlockSpec((tr, C), lambda i, place_ref: (place_ref[1] * nrb + i, 0))),
        compiler_params=_params())(place, slots, slots, slots, sums)


def _half_exchange(bufs):
    n = len(bufs)

    def body(*refs):
        dst = refs[n:2 * n]
        send_sem, recv_sem = refs[2 * n:]
        x, y, c, _ = _place()
        copies = []
        for a in range(n):
            mine = dst[a].at[_half_rows(c, dst[a].shape[0])]
            copies.append(pltpu.make_async_remote_copy(
                src_ref=mine, dst_ref=mine, send_sem=send_sem.at[a], recv_sem=recv_sem.at[a],
                device_id=(x, y, 1 - c), device_id_type=MESH))
            copies[-1].start()
        for cp in copies:
            cp.wait()

    return pl.pallas_call(
        body, name="grad_half_exchange", in_specs=[ANY] * n, out_specs=[ANY] * n,
        out_shape=[jax.ShapeDtypeStruct(b.shape, b.dtype) for b in bufs],
        input_output_aliases={a: a for a in range(n)},
        scratch_shapes=[pltpu.SemaphoreType.DMA((n,))] * 2)(*bufs)


def _reduce_scatter(grads, tags, chip, core):
    place = jnp.concatenate([chip, core])
    theirs = _pair_exchange(grads)
    sums = [_pair_sum("pair_sum_" + t, core, g, p) for t, g, p in zip(tags, grads, theirs)]
    slots = _chip_scatter(sums)
    halves = [_slot_sum("slot_sum_" + t, place, r, s) for t, r, s in zip(tags, slots, sums)]
    return _half_exchange(halves)


def _all_reduce_small(v):
    rows = v.shape[0]
    n_dev = 2 * N_CHIPS

    def body(x_ref, out_ref, gat_ref, send_sems, recv_sems, local_sem):
        x, y, c, chips = _place()
        me, sibling = (x, y, c), (x, y, 1 - c)

        def slot(px, py, pc):
            return gat_ref.at[4 * px + 2 * py + pc]

        def copy(k, block, to, src=None):
            return pltpu.make_async_remote_copy(
                src_ref=slot(*block) if src is None else src, dst_ref=slot(*block), send_sem=send_sems.at[k],
                recv_sem=recv_sems.at[k], device_id=to, device_id_type=MESH)

        mine = pltpu.make_async_copy(x_ref, slot(*me), local_sem)
        mine.start()
        first = [copy(0, me, sibling, src=x_ref)]
        first += [copy(1 + j, me, (*chip, c), src=x_ref) for j, chip in enumerate(chips)]
        for cp in first:
            cp.start()
        passed = [copy(4 + j, (*chip, c), sibling) for j, chip in enumerate(chips)]
        for j, chip in enumerate(chips):
            copy(1 + j, (*chip, c), me).wait_recv()
            passed[j].start()
        copy(0, sibling, me).wait_recv()
        for j, chip in enumerate(chips):
            copy(4 + j, (*chip, 1 - c), me).wait_recv()
        for cp in first + passed:
            cp.wait_send()
        mine.wait()
        acc = gat_ref[0]
        for d in range(1, n_dev):
            acc = acc + gat_ref[d]
        out_ref[...] = acc

    vm = pl.BlockSpec(memory_space=pltpu.VMEM)
    return pl.pallas_call(
        body, name="small_grad_all_reduce", in_specs=[vm], out_specs=vm,
        out_shape=jax.ShapeDtypeStruct(v.shape, F32),
        scratch_shapes=[pltpu.VMEM((n_dev, rows, BLOCK), F32), pltpu.SemaphoreType.DMA((7,)),
                        pltpu.SemaphoreType.DMA((7,)), pltpu.SemaphoreType.DMA],
        compiler_params=_params())(v)


def _adamw_math(w, g, m, v):
    m2 = ADAM_B1 * m + (1.0 - ADAM_B1) * g
    v2 = ADAM_B2 * v + (1.0 - ADAM_B2) * (g * g)
    m_hat = m2 / (1.0 - ADAM_B1 ** ADAM_STEP)
    v_hat = v2 / (1.0 - ADAM_B2 ** ADAM_STEP)
    delta = -ADAM_LR * (m_hat / (jnp.sqrt(v_hat) + ADAM_EPS) + ADAM_WD * w)
    return delta, m2, v2


def _adamw_big(name, grads, w, m, v):
    L, R, C = w.shape
    tr = _row_tile(R, 128)
    nrb = R // tr

    def body(*refs):
        g_refs = refs[:L]
        w_ref, m_ref, v_ref, go_ref, d_ref, mo_ref, vo_ref = refs[L:]
        layer = pl.program_id(0)
        g = g_refs[0][...]
        for k in range(1, L):
            g = jnp.where(layer == k, g_refs[k][...], g)
        delta, m2, v2 = _adamw_math(w_ref[...], g, m_ref[...], v_ref[...])
        go_ref[...] = g
        d_ref[...] = delta
        mo_ref[...] = m2
        vo_ref[...] = v2

    def gspec(k):
        return pl.BlockSpec((tr, C), lambda l, i: (jnp.where(l == k, i, (nrb - 1) * (k < l)), 0))

    blk = pl.BlockSpec((None, tr, C), lambda l, i: (l, i, 0))
    shp = jax.ShapeDtypeStruct(w.shape, F32)
    return pl.pallas_call(
        body, name=name, grid=(L, nrb), in_specs=[gspec(k) for k in range(L)] + [blk] * 3, out_specs=[blk] * 4,
        out_shape=[shp] * 4, compiler_params=_params())(*grads, w, m, v)


def _adamw_small(g, w, m, v):
    rows = g.shape[0]
    tr = _row_tile(rows, 512)

    def body(g_ref, w_ref, m_ref, v_ref, d_ref, mo_ref, vo_ref):
        delta, m2, v2 = _adamw_math(w_ref[...], g_ref[...], m_ref[...], v_ref[...])
        d_ref[...] = delta
        mo_ref[...] = m2
        vo_ref[...] = v2

    blk = pl.BlockSpec((tr, BLOCK), lambda i: (i, 0))
    shp = jax.ShapeDtypeStruct(g.shape, F32)
    return pl.pallas_call(
        body, name="adamw_small", grid=(rows // tr,), in_specs=[blk] * 4, out_specs=[blk] * 3, out_shape=[shp] * 3,
        compiler_params=_params())(g, w, m, v)


def _pack(arrays):
    flat = jnp.concatenate([a.reshape(-1) for a in arrays])
    pad = (-flat.shape[0]) % (16 * BLOCK)
    return jnp.pad(flat, (0, pad)).reshape(-1, BLOCK)


def _unpack(packed, like):
    flat = packed.reshape(-1)
    out, off = [], 0
    for a in like:
        out.append(flat[off:off + a.size].reshape(a.shape))
        off += a.size
    return out


BIG = ("w_in", "w_attn_branch", "w_sgu_branch", "w_out", "w_gate", "w_up", "w_down")
SMALL = ("mix_norm", "q_norm", "k_norm", "sinks", "sgu_ln_g", "sgu_ln_b", "w_spatial", "b_spatial", "ffn_norm")
ORDER = ("mix_norm", "w_in", "q_norm", "k_norm", "sinks", "sgu_ln_g", "sgu_ln_b", "w_spatial", "b_spatial",
         "w_attn_branch", "w_sgu_branch", "w_out", "ffn_norm", "w_gate", "w_up", "w_down")


def _rope_tables(seq):
    pos = jnp.arange(seq, dtype=F32)
    inv_freq = jnp.power(10000.0, -jnp.arange(0, HEAD_DIM, 2, dtype=F32) / HEAD_DIM)
    ang = pos[:, None] * inv_freq[None, :]
    cos, sin = jnp.cos(ang), jnp.sin(ang)
    reps = BLOCK // HEAD_DIM
    return (jnp.tile(jnp.concatenate([cos, cos], axis=1), (1, reps)),
            jnp.tile(jnp.concatenate([-sin, sin], axis=1), (1, reps)))


def kernel(x, mix_norm, w_in, q_norm, k_norm, sinks, sgu_ln_g, sgu_ln_b, w_spatial, b_spatial, w_attn_branch, w_sgu_branch, w_out, ffn_norm, w_gate, w_up, w_down, loss_target, m_mix_norm, m_w_in, m_q_norm, m_k_norm, m_sinks, m_sgu_ln_g, m_sgu_ln_b, m_w_spatial, m_b_spatial, m_w_attn_branch, m_w_sgu_branch, m_w_out, m_ffn_norm, m_w_gate, m_w_up, m_w_down, v_mix_norm, v_w_in, v_q_norm, v_k_norm, v_sinks, v_sgu_ln_g, v_sgu_ln_b, v_w_spatial, v_b_spatial, v_w_attn_branch, v_w_sgu_branch, v_w_out, v_ffn_norm, v_w_gate, v_w_up, v_w_down):
    weights = dict(mix_norm=mix_norm, w_in=w_in, q_norm=q_norm, k_norm=k_norm, sinks=sinks, sgu_ln_g=sgu_ln_g,
                   sgu_ln_b=sgu_ln_b, w_spatial=w_spatial, b_spatial=b_spatial, w_attn_branch=w_attn_branch,
                   w_sgu_branch=w_sgu_branch, w_out=w_out, ffn_norm=ffn_norm, w_gate=w_gate, w_up=w_up, w_down=w_down)
    mom1 = dict(mix_norm=m_mix_norm, w_in=m_w_in, q_norm=m_q_norm, k_norm=m_k_norm, sinks=m_sinks,
                sgu_ln_g=m_sgu_ln_g, sgu_ln_b=m_sgu_ln_b, w_spatial=m_w_spatial, b_spatial=m_b_spatial,
                w_attn_branch=m_w_attn_branch, w_sgu_branch=m_w_sgu_branch, w_out=m_w_out, ffn_norm=m_ffn_norm,
                w_gate=m_w_gate, w_up=m_w_up, w_down=m_w_down)
    mom2 = dict(mix_norm=v_mix_norm, w_in=v_w_in, q_norm=v_q_norm, k_norm=v_k_norm, sinks=v_sinks,
                sgu_ln_g=v_sgu_ln_g, sgu_ln_b=v_sgu_ln_b, w_spatial=v_w_spatial, b_spatial=v_b_spatial,
                w_attn_branch=v_w_attn_branch, w_sgu_branch=v_w_sgu_branch, w_out=v_w_out, ffn_norm=v_ffn_norm,
                w_gate=v_w_gate, w_up=v_w_up, w_down=v_w_down)
    xs, target = x[0], loss_target[0]
    S, D = xs.shape
    L = w_in.shape[0]
    AW, KW, SW = N_Q_HEADS * HEAD_DIM, N_KV_HEADS * HEAD_DIM, SGU_GROUPS * BLOCK
    dims = (AW, KW, AW + 2 * KW + 2 * SW, (AW + 2 * KW) // BLOCK)
    cos, sin = _rope_tables(S)
    reps = BLOCK // HEAD_DIM

    chip = (2 * lax.axis_index("x") + lax.axis_index("y")).astype(jnp.int32).reshape(1)
    core = lax.axis_index("c").astype(jnp.int32).reshape(1)
    gathered = _all_gather([_cast_own("cast_%s_%d" % (n, l), chip, weights[n], l) for l in range(L) for n in BIG])
    wl = [{n: gathered[l * len(BIG) + i] for i, n in enumerate(BIG)} for l in range(L)]
    sp = [dict(mix_norm=mix_norm[l][None], ffn_norm=ffn_norm[l][None], q_norm=jnp.tile(q_norm[l][None], (1, reps)),
               k_norm=jnp.tile(k_norm[l][None], (1, reps)), sinks=sinks[l][None], sgu_ln_g=sgu_ln_g[l][None],
               sgu_ln_b=sgu_ln_b[l][None], w_spatial=w_spatial[l], b_spatial=b_spatial[l][:, :, None])
          for l in range(L)]

    act, saved = xs, []
    for l in range(L):
        act, sv = _layer_fwd(act, wl[l], sp[l], cos, sin, dims)
        saved.append(sv)
    loss_part, dy, dyb = _loss_head(act, target)
    loss = lax.psum(loss_part[0, 0], ("x", "y", "c"))

    big_g, small_g = [None] * L, [None] * L
    for l in reversed(range(L)):
        dy, dyb, big_g[l], small_g[l] = _layer_bwd(dy, dyb, wl[l], sp[l], saved[l], cos, sin, dims)
    grad_x = dy[None]

    tags = ["%s_%d" % (n, l) for l in range(L) for n in BIG]
    reduced = _reduce_scatter([big_g[l][n] for l in range(L) for n in BIG], tags, chip, core)
    grads, deltas, new_m, new_v = {}, {}, {}, {}
    for i, n in enumerate(BIG):
        per_layer = [reduced[l * len(BIG) + i] for l in range(L)]
        grads[n], deltas[n], new_m[n], new_v[n] = _adamw_big("adamw_" + n, per_layer, weights[n], mom1[n], mom2[n])

    small_like = [weights[n] for n in SMALL]
    local = [jnp.stack([small_g[l][n].reshape(weights[n].shape[1:]) for l in range(L)]) for n in SMALL]
    g_small = _all_reduce_small(_pack(local))
    d_small, m_small, v_small = _adamw_small(g_small, _pack(small_like), _pack([mom1[n] for n in SMALL]),
                                             _pack([mom2[n] for n in SMALL]))
    for n, g, d, m2, v2 in zip(SMALL, _unpack(g_small, small_like), _unpack(d_small, small_like),
                               _unpack(m_small, small_like), _unpack(v_small, small_like)):
        grads[n], deltas[n], new_m[n], new_v[n] = g, d, m2, v2

    return (loss, grad_x, *[grads[n] for n in ORDER], *[deltas[n] for n in ORDER],
            *[new_m[n] for n in ORDER], *[new_v[n] for n in ORDER])
```

```python
import functools

import jax
import jax.numpy as jnp
from jax import lax
from jax.experimental import pallas as pl
from jax.experimental.pallas import tpu as pltpu

HEAD_DIM = 64
N_Q_HEADS = 16
N_KV_HEADS = 4
SGU_GROUPS = 8
BLOCK = 128
EPS = 1e-6
ADAM_LR = 0.001
ADAM_B1 = 0.9
ADAM_B2 = 0.999
ADAM_EPS = 1e-08
ADAM_WD = 0.01
ADAM_STEP = 10
N_CHIPS = 4
VMEM_LIMIT = 52 * 1024 * 1024

F32 = jnp.float32
MXU = jnp.bfloat16
NN = (((1,), (0,)), ((), ()))
NT = (((1,), (1,)), ((), ()))
TN = (((0,), (0,)), ((), ()))
MESH = pl.DeviceIdType.MESH
ANY = pl.BlockSpec(memory_space=pl.ANY)


def _tile(n, pref):
    if n <= pref:
        return n
    best = None
    for t in range(BLOCK, pref + 1, BLOCK):
        if n % t == 0:
            best = t
    assert best is not None, (n, pref)
    return best


def _params():
    return pltpu.CompilerParams(vmem_limit_bytes=VMEM_LIMIT)


def _mm(name, grid, n_red, operands, specs, pairs, dims, n_extra, out_shapes, out_specs,
        acc_shapes, epilogue):
    n_op = len(operands) - n_extra
    n_out = len(out_shapes)
    n_acc = len(acc_shapes)

    def body(*refs):
        ops = refs[:n_op]
        extra = refs[n_op:n_op + n_extra]
        outs = refs[n_op + n_extra:n_op + n_extra + n_out]
        accs = refs[n_op + n_extra + n_out:]

        def prod(a, b):
            return lax.dot_general(ops[a][...], ops[b][...], dims, preferred_element_type=F32)

        if n_red == 0:
            vals = [None] * n_acc
            for a, b, k in pairs:
                d = prod(a, b)
                vals[k] = d if vals[k] is None else vals[k] + d
            epilogue(vals, extra, outs)
        else:
            axes = list(range(len(grid) - n_red, len(grid)))
            first = pl.program_id(axes[0]) == 0
            last = pl.program_id(axes[0]) == grid[axes[0]] - 1
            for ax in axes[1:]:
                first = jnp.logical_and(first, pl.program_id(ax) == 0)
                last = jnp.logical_and(last, pl.program_id(ax) == grid[ax] - 1)

            @pl.when(first)
            def _():
                for acc in accs:
                    acc[...] = jnp.zeros(acc.shape, F32)

            for a, b, k in pairs:
                accs[k][...] += prod(a, b)

            @pl.when(last)
            def _():
                epilogue([acc[...] for acc in accs], extra, outs)

    scratch = [pltpu.VMEM(s, F32) for s in acc_shapes] if n_red else []
    return pl.pallas_call(
        body, name=name, grid=grid, in_specs=specs, out_specs=out_specs, out_shape=out_shapes,
        scratch_shapes=scratch, compiler_params=_params())(*operands)


def _sigmoid(x):
    return 1.0 / (1.0 + jnp.exp(-x))


_GELU_C = 0.7978845608028654
_GELU_A = 0.044715


def _gelu(x):
    return 0.5 * x * (1.0 + jnp.tanh(_GELU_C * (x + _GELU_A * x * x * x)))


def _gelu_grad(x):
    t = jnp.tanh(_GELU_C * (x + _GELU_A * x * x * x))
    return 0.5 * (1.0 + t) + 0.5 * x * (1.0 - t * t) * _GELU_C * (1.0 + 3.0 * _GELU_A * x * x)


def _rms_fwd(name, x, g):
    S, D = x.shape
    tr = _tile(S, 256)

    def body(x_ref, g_ref, o_ref):
        xv = x_ref[...]
        r = lax.rsqrt(jnp.mean(xv * xv, axis=-1, keepdims=True) + EPS)
        o_ref[...] = (xv * r * g_ref[...]).astype(MXU)

    return pl.pallas_call(
        body, name=name, grid=(S // tr,),
        in_specs=[pl.BlockSpec((tr, D), lambda i: (i, 0)), pl.BlockSpec((1, D), lambda i: (0, 0))],
        out_specs=pl.BlockSpec((tr, D), lambda i: (i, 0)),
        out_shape=jax.ShapeDtypeStruct((S, D), MXU), compiler_params=_params())(x, g)


def _rms_bwd(name, dh, x, g, dres):
    S, D = x.shape
    tr = _tile(S, 256)

    def body(dh_ref, x_ref, g_ref, dres_ref, dx_ref, dxb_ref, dg_ref):
        xv = x_ref[...]
        r = lax.rsqrt(jnp.mean(xv * xv, axis=-1, keepdims=True) + EPS)
        xh = xv * r
        dhv = dh_ref[...]
        dy = dhv * g_ref[...]
        dx = dres_ref[...] + r * (dy - xh * jnp.mean(dy * xh, axis=-1, keepdims=True))
        dx_ref[...] = dx
        dxb_ref[...] = dx.astype(MXU)

        @pl.when(pl.program_id(0) == 0)
        def _():
            dg_ref[...] = jnp.zeros(dg_ref.shape, F32)

        dg_ref[...] += jnp.sum(dhv * xh, axis=0, keepdims=True)

    row = pl.BlockSpec((tr, D), lambda i: (i, 0))
    vec = pl.BlockSpec((1, D), lambda i: (0, 0))
    return pl.pallas_call(
        body, name=name, grid=(S // tr,), in_specs=[row, row, vec, row], out_specs=[row, row, vec],
        out_shape=[jax.ShapeDtypeStruct((S, D), F32), jax.ShapeDtypeStruct((S, D), MXU),
                   jax.ShapeDtypeStruct((1, D), F32)],
        compiler_params=_params())(dh, x, g, dres)


def _loss_head(y, target):
    S, D = y.shape
    tr = _tile(S, 256)

    def body(y_ref, t_ref, loss_ref, dy_ref, dyb_ref):
        d = y_ref[...] - t_ref[...]
        dy = d * (1.0 / D)
        dy_ref[...] = dy
        dyb_ref[...] = dy.astype(MXU)

        @pl.when(pl.program_id(0) == 0)
        def _():
            loss_ref[...] = jnp.zeros(loss_ref.shape, F32)

        loss_ref[...] += (0.5 / D) * jnp.sum(jnp.sum(d * d, axis=-1, keepdims=True), axis=0, keepdims=True)

    row = pl.BlockSpec((tr, D), lambda i: (i, 0))
    return pl.pallas_call(
        body, name="loss_head", grid=(S // tr,), in_specs=[row, row],
        out_specs=[pl.BlockSpec((1, 1), lambda i: (0, 0)), row, row],
        out_shape=[jax.ShapeDtypeStruct((1, 1), F32), jax.ShapeDtypeStruct((S, D), F32),
                   jax.ShapeDtypeStruct((S, D), MXU)],
        compiler_params=_params())(y, target)


def _head_sum(v):
    r = lax.broadcasted_iota(jnp.int32, (BLOCK, BLOCK), 0) // HEAD_DIM
    c = lax.broadcasted_iota(jnp.int32, (BLOCK, BLOCK), 1) // HEAD_DIM
    ones = jnp.where(r == c, 1.0, 0.0).astype(jnp.bfloat16)
    hi = v.astype(jnp.bfloat16)
    lo = (v - hi.astype(F32)).astype(jnp.bfloat16)
    parts = []
    for t in range(v.shape[1] // BLOCK):
        sl = slice(t * BLOCK, (t + 1) * BLOCK)
        parts.append(jnp.dot(hi[:, sl], ones, preferred_element_type=F32)
                     + jnp.dot(lo[:, sl], ones, preferred_element_type=F32))
    return parts[0] if len(parts) == 1 else jnp.concatenate(parts, axis=-1)


def _swap_halves(v):
    w = v.shape[1]
    half = HEAD_DIM // 2
    lane = lax.broadcasted_iota(jnp.int32, v.shape, 1) % HEAD_DIM
    return jnp.where(lane < half, pltpu.roll(v, w - half, 1), pltpu.roll(v, half, 1))


def _norm_rope(xv, gain, cos, sin):
    r = lax.rsqrt(_head_sum(xv * xv) * (1.0 / HEAD_DIM) + EPS)
    xn = xv * r * gain
    return xn * cos + _swap_halves(xn) * sin


def _norm_rope_bwd(dy, xv, gain, cos, sin):
    r = lax.rsqrt(_head_sum(xv * xv) * (1.0 / HEAD_DIM) + EPS)
    xh = xv * r
    dxn = dy * cos + _swap_halves(dy * sin)
    dgain = jnp.sum(dxn * xh, axis=0, keepdims=True)
    dxh = dxn * gain
    dx = r * (dxh - xh * (_head_sum(dxh * xh) * (1.0 / HEAD_DIM)))
    return dx, dgain


def _fold_heads(v):
    acc = v[:, 0:BLOCK]
    for t in range(1, v.shape[1] // BLOCK):
        acc = acc + v[:, t * BLOCK:(t + 1) * BLOCK]
    return acc + pltpu.roll(acc, HEAD_DIM, 1)


def _tile_lanes(v, width):
    return v if width == BLOCK else jnp.tile(v, (1, width // BLOCK))


def _qk_prep(proj, qg, kg, cos, sin, AW, KW):
    S = proj.shape[0]
    tr = _tile(S, 256)
    scale = HEAD_DIM ** -0.5

    def body(q_ref, k_ref, v_ref, qg_ref, kg_ref, cos_ref, sin_ref, qo_ref, ko_ref, vo_ref):
        c, s = cos_ref[...], sin_ref[...]
        q = _norm_rope(q_ref[...], _tile_lanes(qg_ref[...], AW), _tile_lanes(c, AW), _tile_lanes(s, AW))
        k = _norm_rope(k_ref[...], _tile_lanes(kg_ref[...], KW), _tile_lanes(c, KW), _tile_lanes(s, KW))
        qo_ref[...] = (q * scale).astype(MXU)
        ko_ref[...] = k.astype(MXU)
        vo_ref[...] = v_ref[...].astype(MXU)

    assert AW % KW == 0
    vec = pl.BlockSpec((1, BLOCK), lambda i: (0, 0))
    tab = pl.BlockSpec((tr, BLOCK), lambda i: (i, 0))
    return pl.pallas_call(
        body, name="qk_prep", grid=(S // tr,),
        in_specs=[pl.BlockSpec((tr, AW), lambda i: (i, 0)),
                  pl.BlockSpec((tr, KW), lambda i: (i, AW // KW)),
                  pl.BlockSpec((tr, KW), lambda i: (i, AW // KW + 1)), vec, vec, tab, tab],
        out_specs=[pl.BlockSpec((tr, AW), lambda i: (i, 0)), pl.BlockSpec((tr, KW), lambda i: (i, 0)),
                   pl.BlockSpec((tr, KW), lambda i: (i, 0))],
        out_shape=[jax.ShapeDtypeStruct((S, AW), MXU), jax.ShapeDtypeStruct((S, KW), MXU),
                   jax.ShapeDtypeStruct((S, KW), MXU)],
        compiler_params=_params())(proj, proj, proj, qg, kg, cos, sin)


def _attn_probs(n, q, kp, kc, g, sink_ref, qpk):
    hd = HEAD_DIM
    kcat = jnp.concatenate([kp[:, g * hd:(g + 1) * hd], kc[:, g * hd:(g + 1) * hd]], axis=0)
    qs = jnp.concatenate([q[:, (g * qpk + j) * hd:(g * qpk + j + 1) * hd] for j in range(qpk)], axis=0)
    s = lax.dot_general(qs, kcat, NT, preferred_element_type=F32)
    row = lax.broadcasted_iota(jnp.int32, (BLOCK, 2 * BLOCK), 0)
    col = lax.broadcasted_iota(jnp.int32, (BLOCK, 2 * BLOCK), 1)
    ok = (col > row) & (col <= row + BLOCK) & ((col >= BLOCK) | (n > 0))
    s = jnp.where(jnp.concatenate([ok] * qpk, axis=0), s, -1e30)
    sk = jnp.concatenate([jnp.full((BLOCK, 1), sink_ref[0, g * qpk + j], F32) for j in range(qpk)], axis=0)
    m = jnp.maximum(jnp.max(s, axis=-1, keepdims=True), sk)
    e = jnp.exp(s - m)
    es = jnp.exp(sk - m)
    z = jnp.sum(e, axis=-1, keepdims=True) + es
    return e / z, es / z, qs, kcat


def _attn_fwd(qr, kr, vb, sinks):
    S, AW = qr.shape
    KW = kr.shape[1]
    nb = S // BLOCK
    nkv = KW // HEAD_DIM
    qpk = AW // KW
    hd = HEAD_DIM

    def body(sink_ref, q_ref, kp_ref, kc_ref, vp_ref, vc_ref, o_ref):
        n = pl.program_id(0)
        q, kp, kc, vp, vc = q_ref[...], kp_ref[...], kc_ref[...], vp_ref[...], vc_ref[...]
        outs = [None] * (nkv * qpk)
        for g in range(nkv):
            p, _, _, _ = _attn_probs(n, q, kp, kc, g, sink_ref, qpk)
            vcat = jnp.concatenate([vp[:, g * hd:(g + 1) * hd], vc[:, g * hd:(g + 1) * hd]], axis=0)
            o = jnp.dot(p.astype(MXU), vcat, preferred_element_type=F32)
            for j in range(qpk):
                outs[g * qpk + j] = o[j * BLOCK:(j + 1) * BLOCK]
        o_ref[...] = jnp.concatenate(outs, axis=-1).astype(MXU)

    cur = lambda n: (n, 0)
    prev = lambda n: (jnp.maximum(n - 1, 0), 0)
    return pl.pallas_call(
        body, name="attn_fwd", grid=(nb,),
        in_specs=[pl.BlockSpec(memory_space=pltpu.SMEM), pl.BlockSpec((BLOCK, AW), cur),
                  pl.BlockSpec((BLOCK, KW), prev), pl.BlockSpec((BLOCK, KW), cur),
                  pl.BlockSpec((BLOCK, KW), prev), pl.BlockSpec((BLOCK, KW), cur)],
        out_specs=pl.BlockSpec((BLOCK, AW), cur),
        out_shape=jax.ShapeDtypeStruct((S, AW), MXU), compiler_params=_params())(sinks, qr, kr, kr, vb, vb)


def _attn_bwd(qr, kr, vb, sinks, dattn):
    S, AW = qr.shape
    KW = kr.shape[1]
    nb = S // BLOCK
    nkv = KW // HEAD_DIM
    qpk = AW // KW
    hd = HEAD_DIM
    scale = HEAD_DIM ** -0.5

    def body(sink_ref, q_ref, kp_ref, kc_ref, vp_ref, vc_ref, do_ref,
             dq_ref, dkp_ref, dkc_ref, dvp_ref, dvc_ref, dsink_ref):
        n = pl.program_id(0)
        q, kp, kc, vp, vc = q_ref[...], kp_ref[...], kc_ref[...], vp_ref[...], vc_ref[...]
        do = do_ref[...].astype(MXU)
        lane = lax.broadcasted_iota(jnp.int32, (1, BLOCK), 1)
        dsink = jnp.zeros((1, BLOCK), F32)
        dqs = [None] * (nkv * qpk)
        dkps, dkcs, dvps, dvcs = [], [], [], []
        for g in range(nkv):
            p, psink, qs, kcat = _attn_probs(n, q, kp, kc, g, sink_ref, qpk)
            vcat = jnp.concatenate([vp[:, g * hd:(g + 1) * hd], vc[:, g * hd:(g + 1) * hd]], axis=0)
            dos = jnp.concatenate([do[:, (g * qpk + j) * hd:(g * qpk + j + 1) * hd] for j in range(qpk)], axis=0)
            dp = lax.dot_general(dos, vcat, NT, preferred_element_type=F32)
            dv = lax.dot_general(p.astype(MXU), dos, TN, preferred_element_type=F32)
            delta = jnp.sum(p * dp, axis=-1, keepdims=True)
            ds = (p * (dp - delta)).astype(MXU)
            dsk = -psink * delta
            dq = jnp.dot(ds, kcat, preferred_element_type=F32) * scale
            dk = lax.dot_general(ds, qs, TN, preferred_element_type=F32)
            for j in range(qpk):
                dqs[g * qpk + j] = dq[j * BLOCK:(j + 1) * BLOCK]
                tot = jnp.sum(dsk[j * BLOCK:(j + 1) * BLOCK], axis=0, keepdims=True)
                dsink = dsink + jnp.where(lane == g * qpk + j, tot, 0.0)
            dkps.append(dk[:BLOCK])
            dkcs.append(dk[BLOCK:])
            dvps.append(dv[:BLOCK])
            dvcs.append(dv[BLOCK:])
        dq_ref[...] = jnp.concatenate(dqs, axis=-1)
        dkp_ref[...] = jnp.concatenate(dkps, axis=-1)
        dkc_ref[...] = jnp.concatenate(dkcs, axis=-1)
        dvp_ref[...] = jnp.concatenate(dvps, axis=-1)
        dvc_ref[...] = jnp.concatenate(dvcs, axis=-1)

        @pl.when(n == 0)
        def _():
            dsink_ref[...] = jnp.zeros(dsink_ref.shape, F32)

        dsink_ref[...] += dsink

    cur = lambda n: (n, 0)
    prev = lambda n: (jnp.maximum(n - 1, 0), 0)
    kv = jax.ShapeDtypeStruct((S, KW), F32)
    kvspec = pl.BlockSpec((BLOCK, KW), cur)
    return pl.pallas_call(
        body, name="attn_bwd", grid=(nb,),
        in_specs=[pl.BlockSpec(memory_space=pltpu.SMEM), pl.BlockSpec((BLOCK, AW), cur),
                  pl.BlockSpec((BLOCK, KW), prev), kvspec, pl.BlockSpec((BLOCK, KW), prev), kvspec,
                  pl.BlockSpec((BLOCK, AW), cur)],
        out_specs=[pl.BlockSpec((BLOCK, AW), cur), kvspec, kvspec, kvspec, kvspec,
                   pl.BlockSpec((1, BLOCK), lambda n: (0, 0))],
        out_shape=[jax.ShapeDtypeStruct((S, AW), F32), kv, kv, kv, kv, jax.ShapeDtypeStruct((1, BLOCK), F32)],
        compiler_params=_params())(sinks, qr, kr, kr, vb, vb, dattn)


def _qk_prep_bwd(proj, qg, kg, cos, sin, dq, dkp, dkc, dvp, dvc, AW, KW):
    S = proj.shape[0]
    nb = S // BLOCK

    def body(q_ref, k_ref, qg_ref, kg_ref, cos_ref, sin_ref, dq_ref, dkp_ref, dkc_ref, dvp_ref, dvc_ref,
             o_ref, dqg_ref, dkg_ref):
        n = pl.program_id(0)
        c, s = cos_ref[...], sin_ref[...]
        has_next = jnp.where(n < nb - 1, 1.0, 0.0)
        dk = dkc_ref[...] + has_next * dkp_ref[...]
        dv = dvc_ref[...] + has_next * dvp_ref[...]
        dxq, dqg = _norm_rope_bwd(dq_ref[...], q_ref[...], _tile_lanes(qg_ref[...], AW),
                                  _tile_lanes(c, AW), _tile_lanes(s, AW))
        dxk, dkg = _norm_rope_bwd(dk, k_ref[...], _tile_lanes(kg_ref[...], KW),
                                  _tile_lanes(c, KW), _tile_lanes(s, KW))
        o_ref[...] = jnp.concatenate([dxq, dxk, dv], axis=-1).astype(MXU)

        @pl.when(n == 0)
        def _():
            dqg_ref[...] = jnp.zeros(dqg_ref.shape, F32)
            dkg_ref[...] = jnp.zeros(dkg_ref.shape, F32)

        dqg_ref[...] += _fold_heads(dqg)
        dkg_ref[...] += _fold_heads(dkg)

    cur = lambda n: (n, 0)
    nxt = lambda n: (jnp.minimum(n + 1, nb - 1), 0)
    vec = pl.BlockSpec((1, BLOCK), lambda n: (0, 0))
    tab = pl.BlockSpec((BLOCK, BLOCK), cur)
    return pl.pallas_call(
        body, name="qk_prep_bwd", grid=(nb,),
        in_specs=[pl.BlockSpec((BLOCK, AW), cur), pl.BlockSpec((BLOCK, KW), lambda n: (n, AW // KW)),
                  vec, vec, tab, tab, pl.BlockSpec((BLOCK, AW), cur),
                  pl.BlockSpec((BLOCK, KW), nxt), pl.BlockSpec((BLOCK, KW), cur),
                  pl.BlockSpec((BLOCK, KW), nxt), pl.BlockSpec((BLOCK, KW), cur)],
        out_specs=[pl.BlockSpec((BLOCK, AW + 2 * KW), cur), vec, vec],
        out_shape=[jax.ShapeDtypeStruct((S, AW + 2 * KW), MXU), jax.ShapeDtypeStruct((1, BLOCK), F32),
                   jax.ShapeDtypeStruct((1, BLOCK), F32)],
        compiler_params=_params())(proj, proj, qg, kg, cos, sin, dq, dkp, dkc, dvp, dvc)


def _sgu_tile(pu_ref, pv_ref, lng_ref, lnb_ref, w_ref, b_ref):
    u = _gelu(pu_ref[...])
    v = _gelu(pv_ref[...])
    mu = jnp.mean(v, axis=-1, keepdims=True)
    vc = v - mu
    r = lax.rsqrt(jnp.mean(vc * vc, axis=-1, keepdims=True) + EPS)
    xh = vc * r
    vn = xh * lng_ref[...] + lnb_ref[...]
    row = lax.broadcasted_iota(jnp.int32, (BLOCK, BLOCK), 0)
    col = lax.broadcasted_iota(jnp.int32, (BLOCK, BLOCK), 1)
    tri = row >= col
    w = jnp.where(tri, w_ref[...], 0.0).astype(MXU)
    s = jnp.dot(w, vn.astype(MXU), preferred_element_type=F32) + b_ref[...]
    return u, xh, r, vn, w, s, tri


def _sgu_specs(u0):
    G = SGU_GROUPS
    return [pl.BlockSpec((BLOCK, BLOCK), lambda g, n: (n, u0 + g)),
            pl.BlockSpec((BLOCK, BLOCK), lambda g, n: (n, u0 + G + g)),
            pl.BlockSpec((1, BLOCK), lambda g, n: (0, g)), pl.BlockSpec((1, BLOCK), lambda g, n: (0, g)),
            pl.BlockSpec((None, BLOCK, BLOCK), lambda g, n: (g, 0, 0)),
            pl.BlockSpec((None, BLOCK, 1), lambda g, n: (g, 0, 0))]


def _sgu_fwd(proj, lng, lnb, ws, bs, u0):
    S = proj.shape[0]
    G = SGU_GROUPS

    def body(pu_ref, pv_ref, lng_ref, lnb_ref, w_ref, b_ref, o_ref):
        u, _, _, _, _, s, _ = _sgu_tile(pu_ref, pv_ref, lng_ref, lnb_ref, w_ref, b_ref)
        o_ref[...] = (u * s).astype(MXU)

    return pl.pallas_call(
        body, name="sgu_fwd", grid=(G, S // BLOCK), in_specs=_sgu_specs(u0),
        out_specs=pl.BlockSpec((BLOCK, BLOCK), lambda g, n: (n, g)),
        out_shape=jax.ShapeDtypeStruct((S, G * BLOCK), MXU), compiler_params=_params())(proj, proj, lng, lnb, ws, bs)


def _sgu_bwd(proj, lng, lnb, ws, bs, dsgu, u0):
    S = proj.shape[0]
    G = SGU_GROUPS

    def body(pu_ref, pv_ref, lng_ref, lnb_ref, w_ref, b_ref, do_ref,
             dpu_ref, dpv_ref, dw_ref, db_ref, dlng_ref, dlnb_ref):
        u, xh, r, vn, w, s, tri = _sgu_tile(pu_ref, pv_ref, lng_ref, lnb_ref, w_ref, b_ref)
        do = do_ref[...]
        dpu_ref[...] = (do * s * _gelu_grad(pu_ref[...])).astype(MXU)
        ds = do * u
        dsb = ds.astype(MXU)
        dw = jnp.where(tri, lax.dot_general(dsb, vn.astype(MXU), NT, preferred_element_type=F32), 0.0)
        dvn = lax.dot_general(w, dsb, TN, preferred_element_type=F32)
        dxh = dvn * lng_ref[...]
        dv = r * (dxh - jnp.mean(dxh, axis=-1, keepdims=True) - xh * jnp.mean(dxh * xh, axis=-1, keepdims=True))
        dpv_ref[...] = (dv * _gelu_grad(pv_ref[...])).astype(MXU)

        @pl.when(pl.program_id(1) == 0)
        def _():
            dw_ref[...] = jnp.zeros(dw_ref.shape, F32)
            db_ref[...] = jnp.zeros(db_ref.shape, F32)
            dlng_ref[...] = jnp.zeros(dlng_ref.shape, F32)
            dlnb_ref[...] = jnp.zeros(dlnb_ref.shape, F32)

        dw_ref[...] += dw
        db_ref[...] += jnp.sum(ds, axis=-1, keepdims=True)
        dlng_ref[...] += jnp.sum(dvn * xh, axis=0, keepdims=True)
        dlnb_ref[...] += jnp.sum(dvn, axis=0, keepdims=True)

    tile = lambda off: pl.BlockSpec((BLOCK, BLOCK), lambda g, n: (n, off + g))
    vec = pl.BlockSpec((1, BLOCK), lambda g, n: (0, g))
    half = jax.ShapeDtypeStruct((S, G * BLOCK), MXU)
    return pl.pallas_call(
        body, name="sgu_bwd", grid=(G, S // BLOCK), in_specs=_sgu_specs(u0) + [tile(0)],
        out_specs=[tile(0), tile(0), pl.BlockSpec((None, BLOCK, BLOCK), lambda g, n: (g, 0, 0)),
                   pl.BlockSpec((None, BLOCK, 1), lambda g, n: (g, 0, 0)), vec, vec],
        out_shape=[half, half, jax.ShapeDtypeStruct((G, BLOCK, BLOCK), F32),
                   jax.ShapeDtypeStruct((G, BLOCK, 1), F32),
                   jax.ShapeDtypeStruct((1, G * BLOCK), F32), jax.ShapeDtypeStruct((1, G * BLOCK), F32)],
        compiler_params=_params())(proj, proj, lng, lnb, ws, bs, dsgu)


def _store_f32(vals, extra, outs):
    for v, o in zip(vals, outs):
        o[...] = v


def _store_mxu(vals, extra, outs):
    for v, o in zip(vals, outs):
        o[...] = v.astype(MXU)


def _proj_in(h, w):
    S, D = h.shape
    Ns = w.shape[2]
    tm, tn = _tile(S, 1024), _tile(Ns, 1024)
    npb = Ns // tn
    return _mm("proj_in", (S // tm, N_CHIPS, npb), 0, [h, w],
               [pl.BlockSpec((tm, D), lambda i, s, j: (i, 0)), pl.BlockSpec((None, D, tn), lambda i, s, j: (s, 0, j))],
               [(0, 1, 0)], NN, 0, [jax.ShapeDtypeStruct((S, N_CHIPS * Ns), F32)],
               [pl.BlockSpec((tm, tn), lambda i, s, j: (i, s * npb + j))], [None], _store_f32)[0]


def _branches(attn, sgu, wa, ws, proj, gate0):
    S, AW = attn.shape
    SW = sgu.shape[1]
    Nb = wa.shape[2]
    D = N_CHIPS * Nb
    tm = _tile(S, 512)
    assert gate0 % Nb == 0
    ga, gb = gate0 // Nb, (gate0 + D) // Nb

    def epilogue(vals, extra, outs):
        a, b = vals
        outs[0][...] = (_sigmoid(extra[0][...]) * a + _sigmoid(extra[1][...]) * b).astype(MXU)
        outs[1][...] = a
        outs[2][...] = b

    tile = pl.BlockSpec((tm, Nb), lambda i, s: (i, s))
    wspec = lambda k: pl.BlockSpec((None, k, Nb), lambda i, s: (s, 0, 0))
    f = jax.ShapeDtypeStruct((S, D), F32)
    return _mm("branches", (S // tm, N_CHIPS), 0, [attn, sgu, wa, ws, proj, proj],
               [pl.BlockSpec((tm, AW), lambda i, s: (i, 0)), pl.BlockSpec((tm, SW), lambda i, s: (i, 0)),
                wspec(AW), wspec(SW), pl.BlockSpec((tm, Nb), lambda i, s: (i, ga + s)),
                pl.BlockSpec((tm, Nb), lambda i, s: (i, gb + s))],
               [(0, 2, 0), (1, 3, 1)], NN, 2, [jax.ShapeDtypeStruct((S, D), MXU), f, f], [tile] * 3,
               [None, None], epilogue)


def _rows_mm(name, a, w, res):
    S = a.shape[0]
    _, K, N = w.shape
    tm, tn = _tile(S, 1024), _tile(N, 1024)

    def epilogue(vals, extra, outs):
        outs[0][...] = extra[0][...] + vals[0]

    out = pl.BlockSpec((tm, tn), lambda i, j, s: (i, j))
    return _mm(name, (S // tm, N // tn, N_CHIPS), 1, [a, w, res],
               [pl.BlockSpec((tm, K), lambda i, j, s: (i, s)), pl.BlockSpec((None, K, tn), lambda i, j, s: (s, 0, j)), out],
               [(0, 1, 0)], NN, 1, [jax.ShapeDtypeStruct((S, N), F32)], [out], [(tm, tn)], epilogue)[0]


def _gate_up(h2, wg, wu):
    S, D = h2.shape
    Nf = wg.shape[2]
    tm, tk = _tile(S, 512), _tile(D, 512)

    def epilogue(vals, extra, outs):
        g, u = vals
        outs[0][...] = g
        outs[1][...] = u
        outs[2][...] = (g * _sigmoid(g) * u).astype(MXU)

    w = pl.BlockSpec((None, tk, Nf), lambda i, s, k: (s, k, 0))
    o = pl.BlockSpec((tm, Nf), lambda i, s, k: (i, s))
    f = jax.ShapeDtypeStruct((S, N_CHIPS * Nf), F32)
    return _mm("gate_up", (S // tm, N_CHIPS, D // tk), 1, [h2, wg, wu],
               [pl.BlockSpec((tm, tk), lambda i, s, k: (i, k)), w, w], [(0, 1, 0), (0, 2, 1)], NN, 0,
               [f, f, jax.ShapeDtypeStruct((S, N_CHIPS * Nf), MXU)], [o, o, o], [(tm, Nf)] * 2, epilogue)


def _down_bwd(dyb, wd, g, u):
    S, D = dyb.shape
    Kf = wd.shape[1]
    tm = _tile(S, 512)

    def epilogue(vals, extra, outs):
        da, gv, uv = vals[0], extra[0][...], extra[1][...]
        sg = _sigmoid(gv)
        outs[0][...] = (da * uv * sg * (1.0 + gv * (1.0 - sg))).astype(MXU)
        outs[1][...] = (da * gv * sg).astype(MXU)

    t = pl.BlockSpec((tm, Kf), lambda i, s: (i, s))
    o = jax.ShapeDtypeStruct((S, N_CHIPS * Kf), MXU)
    return _mm("down_bwd", (S // tm, N_CHIPS), 0, [dyb, wd, g, u],
               [pl.BlockSpec((tm, D), lambda i, s: (i, 0)), pl.BlockSpec((None, Kf, D), lambda i, s: (s, 0, 0)), t, t],
               [(0, 1, 0)], NT, 2, [o, o], [t, t], [None], epilogue)


def _out_bwd(dxb, wo, proj, ba, bb, gate0):
    S, D = dxb.shape
    Ko = wo.shape[1]
    tm = _tile(S, 512)
    assert gate0 % Ko == 0
    ga, gb = gate0 // Ko, (gate0 + D) // Ko

    def epilogue(vals, extra, outs):
        dm = vals[0]
        sa, sb = _sigmoid(extra[0][...]), _sigmoid(extra[1][...])
        outs[0][...] = (dm * sa).astype(MXU)
        outs[1][...] = (dm * sb).astype(MXU)
        outs[2][...] = (dm * extra[2][...] * sa * (1.0 - sa)).astype(MXU)
        outs[3][...] = (dm * extra[3][...] * sb * (1.0 - sb)).astype(MXU)

    t = pl.BlockSpec((tm, Ko), lambda i, s: (i, s))
    o = jax.ShapeDtypeStruct((S, D), MXU)
    return _mm("out_bwd", (S // tm, N_CHIPS), 0, [dxb, wo, proj, proj, ba, bb],
               [pl.BlockSpec((tm, D), lambda i, s: (i, 0)), pl.BlockSpec((None, Ko, D), lambda i, s: (s, 0, 0)),
                pl.BlockSpec((tm, Ko), lambda i, s: (i, ga + s)), pl.BlockSpec((tm, Ko), lambda i, s: (i, gb + s)), t, t],
               [(0, 1, 0)], NT, 4, [o] * 4, [t] * 4, [None], epilogue)


def _dx_cols(name, terms, n_out):
    S = terms[0][0].shape[0]
    _, K, Ns = terms[0][1].shape
    tm, tko, tn = _tile(S, 1024), _tile(K, 1024), _tile(Ns, 1408)
    npb = Ns // tn
    operands, specs, pairs = [], [], []
    for t, (dy, w, k) in enumerate(terms):
        assert w.shape == (N_CHIPS, K, Ns)
        operands += [dy, w]
        specs += [pl.BlockSpec((tm, tn), lambda i, jk, s, jn: (i, s * npb + jn)),
                  pl.BlockSpec((None, tko, tn), lambda i, jk, s, jn: (s, jk, jn))]
        pairs.append((2 * t, 2 * t + 1, k))
    out = pl.BlockSpec((tm, tko), lambda i, jk, s, jn: (i, jk))
    return _mm(name, (S // tm, K // tko, N_CHIPS, npb), 2, operands, specs, pairs, NT, 0,
               [jax.ShapeDtypeStruct((S, K), F32)] * n_out, [out] * n_out, [(tm, tko)] * n_out, _store_f32)


def _dw_cols(name, a, dy):
    S, K = a.shape
    Ns = dy.shape[1] // N_CHIPS
    tk, tn = _tile(K, 512), _tile(Ns, 1408)
    npb = Ns // tn
    return _mm(name, (K // tk, N_CHIPS, npb), 0, [a, dy],
               [pl.BlockSpec((S, tk), lambda jk, s, jn: (0, jk)), pl.BlockSpec((S, tn), lambda jk, s, jn: (0, s * npb + jn))],
               [(0, 1, 0)], TN, 0, [jax.ShapeDtypeStruct((N_CHIPS, K, Ns), MXU)],
               [pl.BlockSpec((None, tk, tn), lambda jk, s, jn: (s, jk, jn))], [None], _store_mxu)[0]


def _dw_rows(name, a, dy):
    S = a.shape[0]
    K = a.shape[1] // N_CHIPS
    N = dy.shape[1]
    tk, tn = _tile(K, 1408), _tile(N, 1024)
    nkb = K // tk
    return _mm(name, (N_CHIPS, nkb, N // tn), 0, [a, dy],
               [pl.BlockSpec((S, tk), lambda s, jk, jn: (0, s * nkb + jk)), pl.BlockSpec((S, tn), lambda s, jk, jn: (0, jn))],
               [(0, 1, 0)], TN, 0, [jax.ShapeDtypeStruct((N_CHIPS, K, N), MXU)],
               [pl.BlockSpec((None, tk, tn), lambda s, jk, jn: (s, jk, jn))], [None], _store_mxu)[0]


def _layer_fwd(x, w, stream, layer, last, sp, cos, sin, dims):
    AW, KW, gate0, u0 = dims
    w = dict(w)
    h = _rms_fwd("mix_norm", x, sp["mix_norm"])
    proj = _proj_in(h, w["w_in"])
    qr, kr, vb = _qk_prep(proj, sp["q_norm"], sp["k_norm"], cos, sin, AW, KW)
    stream.forward(layer, 1, qr)
    attn = _attn_fwd(qr, kr, vb, sp["sinks"])
    w.update(stream.finish(layer, 1, attn))
    sgu = _sgu_fwd(proj, sp["sgu_ln_g"], sp["sgu_ln_b"], sp["w_spatial"], sp["b_spatial"], u0)
    merged, ba, bb = _branches(attn, sgu, w["w_attn_branch"], w["w_sgu_branch"], proj, gate0)
    stream.forward(layer, 2, merged)
    x1 = _rows_mm("out_proj", merged, w["w_out"], x)
    w.update(stream.finish(layer, 2, x1))
    h2 = _rms_fwd("ffn_norm", x1, sp["ffn_norm"])
    g, u, act = _gate_up(h2, w["w_gate"], w["w_up"])
    if not last:
        stream.forward(layer + 1, 0, g)
    x2 = _rows_mm("down_proj", act, w["w_down"], x1)
    w_next = None if last else stream.finish(layer + 1, 0, x2)
    saved = dict(x=x, h=h, proj=proj, qr=qr, kr=kr, vb=vb, attn=attn, sgu=sgu, merged=merged, ba=ba, bb=bb,
                 x1=x1, h2=h2, g=g, u=u, act=act)
    return x2, saved, w, w_next


def _layer_bwd(dy, dyb, w, sp, sv, cos, sin, dims):
    AW, KW, gate0, u0 = dims
    big, small = {}, {}
    dg, du = _down_bwd(dyb, w["w_down"], sv["g"], sv["u"])
    big["w_down"] = _dw_rows("dw_down", sv["act"], dyb)
    big["w_gate"] = _dw_cols("dw_gate", sv["h2"], dg)
    big["w_up"] = _dw_cols("dw_up", sv["h2"], du)
    dh2 = _dx_cols("dh2", [(dg, w["w_gate"], 0), (du, w["w_up"], 0)], 1)[0]
    dx1, dx1b, small["ffn_norm"] = _rms_bwd("ffn_norm_bwd", dh2, sv["x1"], sp["ffn_norm"], dy)
    dba, dbb, dgla, dglb = _out_bwd(dx1b, w["w_out"], sv["proj"], sv["ba"], sv["bb"], gate0)
    big["w_out"] = _dw_rows("dw_out", sv["merged"], dx1b)
    dattn, dsgu = _dx_cols("dbranch_in", [(dba, w["w_attn_branch"], 0), (dbb, w["w_sgu_branch"], 1)], 2)
    big["w_attn_branch"] = _dw_cols("dw_attn_branch", sv["attn"], dba)
    big["w_sgu_branch"] = _dw_cols("dw_sgu_branch", sv["sgu"], dbb)
    dpu, dpv, small["w_spatial"], db, small["sgu_ln_g"], small["sgu_ln_b"] = _sgu_bwd(
        sv["proj"], sp["sgu_ln_g"], sp["sgu_ln_b"], sp["w_spatial"], sp["b_spatial"], dsgu, u0)
    small["b_spatial"] = db[:, :, 0]
    dq, dkp, dkc, dvp, dvc, dsink = _attn_bwd(sv["qr"], sv["kr"], sv["vb"], sp["sinks"], dattn)
    small["sinks"] = dsink[:, :sp["sinks"].shape[1]]
    dqkv, dqg, dkg = _qk_prep_bwd(sv["proj"], sp["q_norm"], sp["k_norm"], cos, sin, dq, dkp, dkc, dvp, dvc, AW, KW)
    small["q_norm"] = dqg[:, :HEAD_DIM]
    small["k_norm"] = dkg[:, :HEAD_DIM]
    dproj = jnp.concatenate([dqkv, dpu, dpv, dgla, dglb], axis=1)
    big["w_in"] = _dw_cols("dw_in", sv["h"], dproj)
    dh = _dx_cols("dh", [(dproj, w["w_in"], 0)], 1)[0]
    dx, dxb, small["mix_norm"] = _rms_bwd("mix_norm_bwd", dh, sv["x"], sp["mix_norm"], dx1)
    return dx, dxb, big, small


def _place():
    x, y, c = lax.axis_index("x"), lax.axis_index("y"), lax.axis_index("c")
    chips = [(1 - x, y), (x, 1 - y), (1 - x, 1 - y)]
    return x, y, c, chips


def _half_rows(c, rows):
    h = rows // 2
    assert h % 16 == 0
    return pl.ds(pl.multiple_of(c * h, 16), h)


def _row_tile(rows, pref):
    best = None
    for t in range(16, min(rows, pref) + 1, 16):
        if rows % t == 0:
            best = t
    assert best is not None, rows
    return best


def _cast_own(name, chip, w, layer):
    _, R, C = w.shape
    tr = _row_tile(R, 256)

    def body(chip_ref, w_ref, o_ref):
        o_ref[...] = w_ref[...].astype(MXU)

    return pl.pallas_call(
        body, name=name, out_shape=jax.ShapeDtypeStruct((N_CHIPS, R, C), MXU),
        grid_spec=pltpu.PrefetchScalarGridSpec(
            num_scalar_prefetch=1, grid=(R // tr,),
            in_specs=[pl.BlockSpec((None, tr, C), lambda i, chip_ref: (layer, i, 0))],
            out_specs=pl.BlockSpec((None, tr, C), lambda i, chip_ref: (chip_ref[0], i, 0))),
        compiler_params=_params())(chip, w)


HBM = pl.BlockSpec(memory_space=pltpu.HBM)
SEM = pl.BlockSpec(memory_space=pltpu.SEMAPHORE)
DATAFLOW = pltpu.SideEffectType.DATAFLOW_SIDE_EFFECTING


def _gather_copies(bufs, send_sem, recv_sem):
    x, y, c, chips = _place()

    def ici(a, j, block):
        px, py = chips[j]
        blk = bufs[a].at[block, _half_rows(c, bufs[a].shape[1])]
        return pltpu.make_async_remote_copy(
            src_ref=blk, dst_ref=blk, send_sem=send_sem.at[3 * a + j], recv_sem=recv_sem.at[3 * a + j],
            device_id=(px, py, c), device_id_type=MESH)

    def d2d(a, j, core):
        px, py = chips[j]
        blk = bufs[a].at[2 * px + py, _half_rows(core, bufs[a].shape[1])]
        return pltpu.make_async_remote_copy(
            src_ref=blk, dst_ref=blk, send_sem=send_sem.at[3 * a + j], recv_sem=recv_sem.at[3 * a + j],
            device_id=(x, y, 1 - c), device_id_type=MESH)

    return ici, d2d


def _in_hbm(bufs):
    return [pltpu.with_memory_space_constraint(b, pltpu.HBM) for b in bufs]


def _gather_start(name, bufs, after):
    n = len(bufs)

    def body(*refs):
        dst = refs[n + 1:2 * n + 1]
        send_sem, recv_sem, token = refs[2 * n + 1:]
        x, y, c, chips = _place()
        ici, _ = _gather_copies(dst, send_sem, recv_sem)
        for a in range(n):
            for j in range(3):
                ici(a, j, 2 * x + y).start()
        token[...] = jnp.zeros(token.shape, token.dtype)

    sems = pltpu.SemaphoreType.DMA((3 * n,))
    outs = pl.pallas_call(
        body, name=name, in_specs=[HBM] * n + [ANY],
        out_specs=[HBM] * n + [SEM, SEM, pl.BlockSpec(memory_space=pltpu.VMEM)],
        out_shape=[pltpu.HBM(b.shape, b.dtype) for b in bufs] + [sems, sems, jax.ShapeDtypeStruct((8, BLOCK), F32)],
        input_output_aliases={a: a for a in range(n)},
        compiler_params=pltpu.CompilerParams(has_side_effects=DATAFLOW))(*_in_hbm(bufs), after)
    return outs[:n], outs[n], outs[n + 1], outs[n + 2]


def _gather_forward(name, bufs, ici_send, ici_recv, after):
    n = len(bufs)

    def body(*refs):
        ici_send_ref, ici_recv_ref = refs[n], refs[n + 1]
        dst = refs[n + 3:2 * n + 3]
        d2d_send, d2d_recv = refs[2 * n + 3:]
        x, y, c, chips = _place()
        ici, _ = _gather_copies(dst, ici_send_ref, ici_recv_ref)
        _, d2d = _gather_copies(dst, d2d_send, d2d_recv)
        for a in range(n):
            for j, (px, py) in enumerate(chips):
                ici(a, j, 2 * px + py).wait_recv()
                d2d(a, j, c).start()
        for a in range(n):
            for j in range(3):
                ici(a, j, 2 * x + y).wait_send()

    sems = pltpu.SemaphoreType.DMA((3 * n,))
    outs = pl.pallas_call(
        body, name=name, in_specs=[HBM] * n + [SEM, SEM, ANY], out_specs=[HBM] * n + [SEM, SEM],
        out_shape=[pltpu.HBM(b.shape, b.dtype) for b in bufs] + [sems, sems],
        input_output_aliases={a: a for a in range(n)},
        compiler_params=pltpu.CompilerParams(has_side_effects=DATAFLOW))(*bufs, ici_send, ici_recv, after)
    return outs[:n], outs[n], outs[n + 1]


def _gather_finish(name, bufs, d2d_send, d2d_recv, after):
    n = len(bufs)

    def body(*refs):
        send_ref, recv_ref = refs[n], refs[n + 1]
        dst = refs[n + 3:]
        x, y, c, chips = _place()
        _, d2d = _gather_copies(dst, send_ref, recv_ref)
        for a in range(n):
            for j in range(3):
                d2d(a, j, 1 - c).wait_recv()
                d2d(a, j, c).wait_send()

    return pl.pallas_call(
        body, name=name, in_specs=[HBM] * n + [SEM, SEM, ANY], out_specs=[HBM] * n,
        out_shape=[pltpu.HBM(b.shape, b.dtype) for b in bufs],
        input_output_aliases={a: a for a in range(n)},
        compiler_params=pltpu.CompilerParams(has_side_effects=DATAFLOW))(*bufs, d2d_send, d2d_recv, after)


GROUPS = (("w_in",), ("w_attn_branch", "w_sgu_branch", "w_out"), ("w_gate", "w_up", "w_down"))


class _WeightStream:
    def __init__(self, started):
        self.started, self.passed = started, {}

    def forward(self, layer, group, after):
        bufs, send, recv = self.started[(layer, group)]
        self.passed[(layer, group)] = _gather_forward("gather_forward_%d_%d" % (layer, group), bufs, send, recv, after)

    def finish(self, layer, group, after):
        bufs, send, recv = self.passed[(layer, group)]
        done = _gather_finish("gather_finish_%d_%d" % (layer, group), bufs, send, recv, after)
        return dict(zip(GROUPS[group], done))


def _pair_exchange(grads):
    n = len(grads)

    def body(*refs):
        src, dst = refs[:n], refs[n:2 * n]
        send_sem, recv_sem = refs[2 * n:]
        x, y, c, _ = _place()
        copies = []
        for a in range(n):
            theirs = _half_rows(1 - c, src[a].shape[1])
            copies.append(pltpu.make_async_remote_copy(
                src_ref=src[a].at[:, theirs], dst_ref=dst[a], send_sem=send_sem.at[a], recv_sem=recv_sem.at[a],
                device_id=(x, y, 1 - c), device_id_type=MESH))
            copies[-1].start()
        for cp in copies:
            cp.wait()

    out_shape = [jax.ShapeDtypeStruct((g.shape[0], g.shape[1] // 2, g.shape[2]), g.dtype) for g in grads]
    return pl.pallas_call(
        body, name="grad_pair_exchange", in_specs=[ANY] * n, out_specs=[ANY] * n, out_shape=out_shape,
        scratch_shapes=[pltpu.SemaphoreType.DMA((n,))] * 2)(*grads)


def _pair_sum(name, core, g, p):
    _, h, C = p.shape
    tr = _row_tile(h, 256)
    nrb = h // tr

    def body(core_ref, g_ref, p_ref, o_ref):
        o_ref[...] = (g_ref[...].astype(F32) + p_ref[...].astype(F32)).astype(o_ref.dtype)

    spec = pl.BlockSpec((None, tr, C), lambda s, i, core_ref: (s, i, 0))
    return pl.pallas_call(
        body, name=name, out_shape=jax.ShapeDtypeStruct(p.shape, p.dtype),
        grid_spec=pltpu.PrefetchScalarGridSpec(
            num_scalar_prefetch=1, grid=(N_CHIPS, nrb),
            in_specs=[pl.BlockSpec((None, tr, C), lambda s, i, core_ref: (s, core_ref[0] * nrb + i, 0)), spec],
            out_specs=spec),
        compiler_params=_params())(core, g, p)


def _chip_scatter(sums):
    n = len(sums)

    def body(*refs):
        src, dst = refs[:n], refs[n:2 * n]
        send_sem, recv_sem = refs[2 * n:]
        x, y, c, chips = _place()
        copies = []
        for a in range(n):
            for j, (px, py) in enumerate(chips):
                copies.append(pltpu.make_async_remote_copy(
                    src_ref=src[a].at[2 * px + py], dst_ref=dst[a].at[j], send_sem=send_sem.at[a, j],
                    recv_sem=recv_sem.at[a, j], device_id=(px, py, c), device_id_type=MESH))
                copies[-1].start()
        for cp in copies:
            cp.wait()

    return pl.pallas_call(
        body, name="grad_chip_scatter", in_specs=[ANY] * n, out_specs=[ANY] * n,
        out_shape=[jax.ShapeDtypeStruct((3,) + s.shape[1:], s.dtype) for s in sums],
        scratch_shapes=[pltpu.SemaphoreType.DMA((n, 3))] * 2)(*sums)


def _slot_sum(name, place, slots, sums):
    _, h, C = slots.shape
    tr = _row_tile(h, 256)
    nrb = h // tr

    def body(place_ref, r0, r1, r2, own, o_ref):
        o_ref[...] = ((r0[...].astype(F32) + r1[...].astype(F32)) + r2[...].astype(F32)) + own[...].astype(F32)

    slot = lambda k: pl.BlockSpec((None, tr, C), lambda i, place_ref: (k, i, 0))
    return pl.pallas_call(
        body, name=name, out_shape=jax.ShapeDtypeStruct((2 * h, C), F32),
        grid_spec=pltpu.PrefetchScalarGridSpec(
            num_scalar_prefetch=1, grid=(nrb,),
            in_specs=[slot(0), slot(1), slot(2),
                      pl.BlockSpec((None, tr, C), lambda i, place_ref: (place_ref[0], i, 0))],
            out_specs=pl.BlockSpec((tr, C), lambda i, place_ref: (place_ref[1] * nrb + i, 0))),
        compiler_params=_params())(place, slots, slots, slots, sums)


def _half_exchange(bufs):
    n = len(bufs)

    def body(*refs):
        dst = refs[n:2 * n]
        send_sem, recv_sem = refs[2 * n:]
        x, y, c, _ = _place()
        copies = []
        for a in range(n):
            mine = dst[a].at[_half_rows(c, dst[a].shape[0])]
            copies.append(pltpu.make_async_remote_copy(
                src_ref=mine, dst_ref=mine, send_sem=send_sem.at[a], recv_sem=recv_sem.at[a],
                device_id=(x, y, 1 - c), device_id_type=MESH))
            copies[-1].start()
        for cp in copies:
            cp.wait()

    return pl.pallas_call(
        body, name="grad_half_exchange", in_specs=[ANY] * n, out_specs=[ANY] * n,
        out_shape=[jax.ShapeDtypeStruct(b.shape, b.dtype) for b in bufs],
        input_output_aliases={a: a for a in range(n)},
        scratch_shapes=[pltpu.SemaphoreType.DMA((n,))] * 2)(*bufs)


def _reduce_scatter(grads, tags, chip, core):
    place = jnp.concatenate([chip, core])
    theirs = _pair_exchange(grads)
    sums = [_pair_sum("pair_sum_" + t, core, g, p) for t, g, p in zip(tags, grads, theirs)]
    slots = _chip_scatter(sums)
    halves = [_slot_sum("slot_sum_" + t, place, r, s) for t, r, s in zip(tags, slots, sums)]
    return _half_exchange(halves)


def _all_reduce_small(v):
    rows = v.shape[0]
    n_dev = 2 * N_CHIPS

    def body(x_ref, out_ref, gat_ref, send_sems, recv_sems, local_sem):
        x, y, c, chips = _place()
        me, sibling = (x, y, c), (x, y, 1 - c)

        def slot(px, py, pc):
            return gat_ref.at[4 * px + 2 * py + pc]

        def copy(k, block, to, src=None):
            return pltpu.make_async_remote_copy(
                src_ref=slot(*block) if src is None else src, dst_ref=slot(*block), send_sem=send_sems.at[k],
                recv_sem=recv_sems.at[k], device_id=to, device_id_type=MESH)

        mine = pltpu.make_async_copy(x_ref, slot(*me), local_sem)
        mine.start()
        first = [copy(0, me, sibling, src=x_ref)]
        first += [copy(1 + j, me, (*chip, c), src=x_ref) for j, chip in enumerate(chips)]
        for cp in first:
            cp.start()
        passed = [copy(4 + j, (*chip, c), sibling) for j, chip in enumerate(chips)]
        for j, chip in enumerate(chips):
            copy(1 + j, (*chip, c), me).wait_recv()
            passed[j].start()
        copy(0, sibling, me).wait_recv()
        for j, chip in enumerate(chips):
            copy(4 + j, (*chip, 1 - c), me).wait_recv()
        for cp in first + passed:
            cp.wait_send()
        mine.wait()
        acc = gat_ref[0]
        for d in range(1, n_dev):
            acc = acc + gat_ref[d]
        out_ref[...] = acc

    vm = pl.BlockSpec(memory_space=pltpu.VMEM)
    return pl.pallas_call(
        body, name="small_grad_all_reduce", in_specs=[vm], out_specs=vm,
        out_shape=jax.ShapeDtypeStruct(v.shape, F32),
        scratch_shapes=[pltpu.VMEM((n_dev, rows, BLOCK), F32), pltpu.SemaphoreType.DMA((7,)),
                        pltpu.SemaphoreType.DMA((7,)), pltpu.SemaphoreType.DMA],
        compiler_params=_params())(v)


def _adamw_math(w, g, m, v):
    m2 = ADAM_B1 * m + (1.0 - ADAM_B1) * g
    v2 = ADAM_B2 * v + (1.0 - ADAM_B2) * (g * g)
    m_hat = m2 / (1.0 - ADAM_B1 ** ADAM_STEP)
    v_hat = v2 / (1.0 - ADAM_B2 ** ADAM_STEP)
    delta = -ADAM_LR * (m_hat / (jnp.sqrt(v_hat) + ADAM_EPS) + ADAM_WD * w)
    return delta, m2, v2


def _adamw_big(name, grads, w, m, v):
    L, R, C = w.shape
    tr = _row_tile(R, 128)
    nrb = R // tr

    def body(*refs):
        g_refs = refs[:L]
        w_ref, m_ref, v_ref, go_ref, d_ref, mo_ref, vo_ref = refs[L:]
        layer = pl.program_id(0)
        g = g_refs[0][...]
        for k in range(1, L):
            g = jnp.where(layer == k, g_refs[k][...], g)
        delta, m2, v2 = _adamw_math(w_ref[...], g, m_ref[...], v_ref[...])
        go_ref[...] = g
        d_ref[...] = delta
        mo_ref[...] = m2
        vo_ref[...] = v2

    def gspec(k):
        return pl.BlockSpec((tr, C), lambda l, i: (jnp.where(l == k, i, (nrb - 1) * (k < l)), 0))

    blk = pl.BlockSpec((None, tr, C), lambda l, i: (l, i, 0))
    shp = jax.ShapeDtypeStruct(w.shape, F32)
    return pl.pallas_call(
        body, name=name, grid=(L, nrb), in_specs=[gspec(k) for k in range(L)] + [blk] * 3, out_specs=[blk] * 4,
        out_shape=[shp] * 4, compiler_params=_params())(*grads, w, m, v)


def _adamw_small(g, w, m, v):
    rows = g.shape[0]
    tr = _row_tile(rows, 512)

    def body(g_ref, w_ref, m_ref, v_ref, d_ref, mo_ref, vo_ref):
        delta, m2, v2 = _adamw_math(w_ref[...], g_ref[...], m_ref[...], v_ref[...])
        d_ref[...] = delta
        mo_ref[...] = m2
        vo_ref[...] = v2

    blk = pl.BlockSpec((tr, BLOCK), lambda i: (i, 0))
    shp = jax.ShapeDtypeStruct(g.shape, F32)
    return pl.pallas_call(
        body, name="adamw_small", grid=(rows // tr,), in_specs=[blk] * 4, out_specs=[blk] * 3, out_shape=[shp] * 3,
        compiler_params=_params())(g, w, m, v)


def _pack(arrays):
    flat = jnp.concatenate([a.reshape(-1) for a in arrays])
    pad = (-flat.shape[0]) % (16 * BLOCK)
    return jnp.pad(flat, (0, pad)).reshape(-1, BLOCK)


def _unpack(packed, like):
    flat = packed.reshape(-1)
    out, off = [], 0
    for a in like:
        out.append(flat[off:off + a.size].reshape(a.shape))
        off += a.size
    return out


BIG = ("w_in", "w_attn_branch", "w_sgu_branch", "w_out", "w_gate", "w_up", "w_down")
SMALL = ("mix_norm", "q_norm", "k_norm", "sinks", "sgu_ln_g", "sgu_ln_b", "w_spatial", "b_spatial", "ffn_norm")
ORDER = ("mix_norm", "w_in", "q_norm", "k_norm", "sinks", "sgu_ln_g", "sgu_ln_b", "w_spatial", "b_spatial",
         "w_attn_branch", "w_sgu_branch", "w_out", "ffn_norm", "w_gate", "w_up", "w_down")


def _rope_tables(seq):
    pos = jnp.arange(seq, dtype=F32)
    inv_freq = jnp.power(10000.0, -jnp.arange(0, HEAD_DIM, 2, dtype=F32) / HEAD_DIM)
    ang = pos[:, None] * inv_freq[None, :]
    cos, sin = jnp.cos(ang), jnp.sin(ang)
    reps = BLOCK // HEAD_DIM
    return (jnp.tile(jnp.concatenate([cos, cos], axis=1), (1, reps)),
            jnp.tile(jnp.concatenate([-sin, sin], axis=1), (1, reps)))


def kernel(x, mix_norm, w_in, q_norm, k_norm, sinks, sgu_ln_g, sgu_ln_b, w_spatial, b_spatial, w_attn_branch, w_sgu_branch, w_out, ffn_norm, w_gate, w_up, w_down, loss_target, m_mix_norm, m_w_in, m_q_norm, m_k_norm, m_sinks, m_sgu_ln_g, m_sgu_ln_b, m_w_spatial, m_b_spatial, m_w_attn_branch, m_w_sgu_branch, m_w_out, m_ffn_norm, m_w_gate, m_w_up, m_w_down, v_mix_norm, v_w_in, v_q_norm, v_k_norm, v_sinks, v_sgu_ln_g, v_sgu_ln_b, v_w_spatial, v_b_spatial, v_w_attn_branch, v_w_sgu_branch, v_w_out, v_ffn_norm, v_w_gate, v_w_up, v_w_down):
    weights = dict(mix_norm=mix_norm, w_in=w_in, q_norm=q_norm, k_norm=k_norm, sinks=sinks, sgu_ln_g=sgu_ln_g,
                   sgu_ln_b=sgu_ln_b, w_spatial=w_spatial, b_spatial=b_spatial, w_attn_branch=w_attn_branch,
                   w_sgu_branch=w_sgu_branch, w_out=w_out, ffn_norm=ffn_norm, w_gate=w_gate, w_up=w_up, w_down=w_down)
    mom1 = dict(mix_norm=m_mix_norm, w_in=m_w_in, q_norm=m_q_norm, k_norm=m_k_norm, sinks=m_sinks,
                sgu_ln_g=m_sgu_ln_g, sgu_ln_b=m_sgu_ln_b, w_spatial=m_w_spatial, b_spatial=m_b_spatial,
                w_attn_branch=m_w_attn_branch, w_sgu_branch=m_w_sgu_branch, w_out=m_w_out, ffn_norm=m_ffn_norm,
                w_gate=m_w_gate, w_up=m_w_up, w_down=m_w_down)
    mom2 = dict(mix_norm=v_mix_norm, w_in=v_w_in, q_norm=v_q_norm, k_norm=v_k_norm, sinks=v_sinks,
                sgu_ln_g=v_sgu_ln_g, sgu_ln_b=v_sgu_ln_b, w_spatial=v_w_spatial, b_spatial=v_b_spatial,
                w_attn_branch=v_w_attn_branch, w_sgu_branch=v_w_sgu_branch, w_out=v_w_out, ffn_norm=v_ffn_norm,
                w_gate=v_w_gate, w_up=v_w_up, w_down=v_w_down)
    xs, target = x[0], loss_target[0]
    S, D = xs.shape
    L = w_in.shape[0]
    AW, KW, SW = N_Q_HEADS * HEAD_DIM, N_KV_HEADS * HEAD_DIM, SGU_GROUPS * BLOCK
    dims = (AW, KW, AW + 2 * KW + 2 * SW, (AW + 2 * KW) // BLOCK)
    cos, sin = _rope_tables(S)
    reps = BLOCK // HEAD_DIM

    chip = (2 * lax.axis_index("x") + lax.axis_index("y")).astype(jnp.int32).reshape(1)
    core = lax.axis_index("c").astype(jnp.int32).reshape(1)
    started, token = {}, chip
    for l in range(L):
        for gi, names in enumerate(GROUPS):
            bufs = [_cast_own("cast_%s_%d" % (n, l), chip, weights[n], l) for n in names]
            bufs, send, recv, token = _gather_start("gather_start_%d_%d" % (l, gi), bufs, token)
            started[(l, gi)] = (bufs, send, recv)
    stream = _WeightStream(started)
    stream.forward(0, 0, token)
    w_first = stream.finish(0, 0, token)
    sp = [dict(mix_norm=mix_norm[l][None], ffn_norm=ffn_norm[l][None], q_norm=jnp.tile(q_norm[l][None], (1, reps)),
               k_norm=jnp.tile(k_norm[l][None], (1, reps)), sinks=sinks[l][None], sgu_ln_g=sgu_ln_g[l][None],
               sgu_ln_b=sgu_ln_b[l][None], w_spatial=w_spatial[l], b_spatial=b_spatial[l][:, :, None])
          for l in range(L)]

    act, saved, wl = xs, [], []
    for l in range(L):
        act, sv, w_all, w_first = _layer_fwd(act, w_first, stream, l, l == L - 1, sp[l], cos, sin, dims)
        saved.append(sv)
        wl.append(w_all)
    loss_part, dy, dyb = _loss_head(act, target)
    loss = lax.psum(loss_part[0, 0], ("x", "y", "c"))

    big_g, small_g = [None] * L, [None] * L
    for l in reversed(range(L)):
        dy, dyb, big_g[l], small_g[l] = _layer_bwd(dy, dyb, wl[l], sp[l], saved[l], cos, sin, dims)
    grad_x = dy[None]

    tags = ["%s_%d" % (n, l) for l in range(L) for n in BIG]
    reduced = _reduce_scatter([big_g[l][n] for l in range(L) for n in BIG], tags, chip, core)
    grads, deltas, new_m, new_v = {}, {}, {}, {}
    for i, n in enumerate(BIG):
        per_layer = [reduced[l * len(BIG) + i] for l in range(L)]
        grads[n], deltas[n], new_m[n], new_v[n] = _adamw_big("adamw_" + n, per_layer, weights[n], mom1[n], mom2[n])

    small_like = [weights[n] for n in SMALL]
    local = [jnp.stack([small_g[l][n].reshape(weights[n].shape[1:]) for l in range(L)]) for n in SMALL]
    g_small = _all_reduce_small(_pack(local))
    d_small, m_small, v_small = _adamw_small(g_small, _pack(small_like), _pack([mom1[n] for n in SMALL]),
                                             _pack([mom2[n] for n in SMALL]))
    for n, g, d, m2, v2 in zip(SMALL, _unpack(g_small, small_like), _unpack(d_small, small_like),
                               _unpack(m_small, small_like), _unpack(v_small, small_like)):
        grads[n], deltas[n], new_m[n], new_v[n] = g, d, m2, v2

    return (loss, grad_x, *[grads[n] for n in ORDER], *[deltas[n] for n in ORDER],
            *[new_m[n] for n in ORDER], *[new_v[n] for n in ORDER])
```

```python
import functools

import jax
import jax.numpy as jnp
from jax import lax
from jax.experimental import pallas as pl
from jax.experimental.pallas import tpu as pltpu

HEAD_DIM = 64
N_Q_HEADS = 16
N_KV_HEADS = 4
SGU_GROUPS = 8
BLOCK = 128
EPS = 1e-6
ADAM_LR = 0.001
ADAM_B1 = 0.9
ADAM_B2 = 0.999
ADAM_EPS = 1e-08
ADAM_WD = 0.01
ADAM_STEP = 10
N_CHIPS = 4
VMEM_LIMIT = 52 * 1024 * 1024

F32 = jnp.float32
MXU = jnp.bfloat16
NN = (((1,), (0,)), ((), ()))
NT = (((1,), (1,)), ((), ()))
TN = (((0,), (0,)), ((), ()))
MESH = pl.DeviceIdType.MESH
ANY = pl.BlockSpec(memory_space=pl.ANY)


def _tile(n, pref):
    if n <= pref:
        return n
    best = None
    for t in range(BLOCK, pref + 1, BLOCK):
        if n % t == 0:
            best = t
    assert best is not None, (n, pref)
    return best


def _params():
    return pltpu.CompilerParams(vmem_limit_bytes=VMEM_LIMIT)


def _mm(name, grid, n_red, operands, specs, pairs, dims, n_extra, out_shapes, out_specs,
        acc_shapes, epilogue, after=None):
    n_op = len(operands) - n_extra
    n_out = len(out_shapes)
    n_acc = len(acc_shapes)
    if after is not None:
        operands, specs = list(operands) + [after], list(specs) + [ANY]
    n_in = len(operands)

    def body(*refs):
        ops = refs[:n_op]
        extra = refs[n_op:n_op + n_extra]
        outs = refs[n_in:n_in + n_out]
        accs = refs[n_in + n_out:]

        def prod(a, b):
            return lax.dot_general(ops[a][...], ops[b][...], dims, preferred_element_type=F32)

        if n_red == 0:
            vals = [None] * n_acc
            for a, b, k in pairs:
                d = prod(a, b)
                vals[k] = d if vals[k] is None else vals[k] + d
            epilogue(vals, extra, outs)
        else:
            axes = list(range(len(grid) - n_red, len(grid)))
            first = pl.program_id(axes[0]) == 0
            last = pl.program_id(axes[0]) == grid[axes[0]] - 1
            for ax in axes[1:]:
                first = jnp.logical_and(first, pl.program_id(ax) == 0)
                last = jnp.logical_and(last, pl.program_id(ax) == grid[ax] - 1)

            @pl.when(first)
            def _():
                for acc in accs:
                    acc[...] = jnp.zeros(acc.shape, F32)

            for a, b, k in pairs:
                accs[k][...] += prod(a, b)

            @pl.when(last)
            def _():
                epilogue([acc[...] for acc in accs], extra, outs)

    scratch = [pltpu.VMEM(s, F32) for s in acc_shapes] if n_red else []
    return pl.pallas_call(
        body, name=name, grid=grid, in_specs=specs, out_specs=out_specs, out_shape=out_shapes,
        scratch_shapes=scratch, compiler_params=_params())(*operands)


def _sigmoid(x):
    return 1.0 / (1.0 + jnp.exp(-x))


_GELU_C = 0.7978845608028654
_GELU_A = 0.044715


def _gelu(x):
    return 0.5 * x * (1.0 + jnp.tanh(_GELU_C * (x + _GELU_A * x * x * x)))


def _gelu_grad(x):
    t = jnp.tanh(_GELU_C * (x + _GELU_A * x * x * x))
    return 0.5 * (1.0 + t) + 0.5 * x * (1.0 - t * t) * _GELU_C * (1.0 + 3.0 * _GELU_A * x * x)


def _rms_fwd(name, x, g):
    S, D = x.shape
    tr = _tile(S, 256)

    def body(x_ref, g_ref, o_ref):
        xv = x_ref[...]
        r = lax.rsqrt(jnp.mean(xv * xv, axis=-1, keepdims=True) + EPS)
        o_ref[...] = (xv * r * g_ref[...]).astype(MXU)

    return pl.pallas_call(
        body, name=name, grid=(S // tr,),
        in_specs=[pl.BlockSpec((tr, D), lambda i: (i, 0)), pl.BlockSpec((1, D), lambda i: (0, 0))],
        out_specs=pl.BlockSpec((tr, D), lambda i: (i, 0)),
        out_shape=jax.ShapeDtypeStruct((S, D), MXU), compiler_params=_params())(x, g)


def _rms_bwd(name, dh, x, g, dres):
    S, D = x.shape
    tr = _tile(S, 256)

    def body(dh_ref, x_ref, g_ref, dres_ref, dx_ref, dxb_ref, dg_ref):
        xv = x_ref[...]
        r = lax.rsqrt(jnp.mean(xv * xv, axis=-1, keepdims=True) + EPS)
        xh = xv * r
        dhv = dh_ref[...]
        dy = dhv * g_ref[...]
        dx = dres_ref[...] + r * (dy - xh * jnp.mean(dy * xh, axis=-1, keepdims=True))
        dx_ref[...] = dx
        dxb_ref[...] = dx.astype(MXU)

        @pl.when(pl.program_id(0) == 0)
        def _():
            dg_ref[...] = jnp.zeros(dg_ref.shape, F32)

        dg_ref[...] += jnp.sum(dhv * xh, axis=0, keepdims=True)

    row = pl.BlockSpec((tr, D), lambda i: (i, 0))
    vec = pl.BlockSpec((1, D), lambda i: (0, 0))
    return pl.pallas_call(
        body, name=name, grid=(S // tr,), in_specs=[row, row, vec, row], out_specs=[row, row, vec],
        out_shape=[jax.ShapeDtypeStruct((S, D), F32), jax.ShapeDtypeStruct((S, D), MXU),
                   jax.ShapeDtypeStruct((1, D), F32)],
        compiler_params=_params())(dh, x, g, dres)


def _loss_head(y, target):
    S, D = y.shape
    tr = _tile(S, 256)

    def body(y_ref, t_ref, loss_ref, dy_ref, dyb_ref):
        d = y_ref[...] - t_ref[...]
        dy = d * (1.0 / D)
        dy_ref[...] = dy
        dyb_ref[...] = dy.astype(MXU)

        @pl.when(pl.program_id(0) == 0)
        def _():
            loss_ref[...] = jnp.zeros(loss_ref.shape, F32)

        loss_ref[...] += (0.5 / D) * jnp.sum(jnp.sum(d * d, axis=-1, keepdims=True), axis=0, keepdims=True)

    row = pl.BlockSpec((tr, D), lambda i: (i, 0))
    return pl.pallas_call(
        body, name="loss_head", grid=(S // tr,), in_specs=[row, row],
        out_specs=[pl.BlockSpec((1, 1), lambda i: (0, 0)), row, row],
        out_shape=[jax.ShapeDtypeStruct((1, 1), F32), jax.ShapeDtypeStruct((S, D), F32),
                   jax.ShapeDtypeStruct((S, D), MXU)],
        compiler_params=_params())(y, target)


def _head_sum(v):
    r = lax.broadcasted_iota(jnp.int32, (BLOCK, BLOCK), 0) // HEAD_DIM
    c = lax.broadcasted_iota(jnp.int32, (BLOCK, BLOCK), 1) // HEAD_DIM
    ones = jnp.where(r == c, 1.0, 0.0).astype(jnp.bfloat16)
    hi = v.astype(jnp.bfloat16)
    lo = (v - hi.astype(F32)).astype(jnp.bfloat16)
    parts = []
    for t in range(v.shape[1] // BLOCK):
        sl = slice(t * BLOCK, (t + 1) * BLOCK)
        parts.append(jnp.dot(hi[:, sl], ones, preferred_element_type=F32)
                     + jnp.dot(lo[:, sl], ones, preferred_element_type=F32))
    return parts[0] if len(parts) == 1 else jnp.concatenate(parts, axis=-1)


def _swap_halves(v):
    w = v.shape[1]
    half = HEAD_DIM // 2
    lane = lax.broadcasted_iota(jnp.int32, v.shape, 1) % HEAD_DIM
    return jnp.where(lane < half, pltpu.roll(v, w - half, 1), pltpu.roll(v, half, 1))


def _norm_rope(xv, gain, cos, sin):
    r = lax.rsqrt(_head_sum(xv * xv) * (1.0 / HEAD_DIM) + EPS)
    xn = xv * r * gain
    return xn * cos + _swap_halves(xn) * sin


def _norm_rope_bwd(dy, xv, gain, cos, sin):
    r = lax.rsqrt(_head_sum(xv * xv) * (1.0 / HEAD_DIM) + EPS)
    xh = xv * r
    dxn = dy * cos + _swap_halves(dy * sin)
    dgain = jnp.sum(dxn * xh, axis=0, keepdims=True)
    dxh = dxn * gain
    dx = r * (dxh - xh * (_head_sum(dxh * xh) * (1.0 / HEAD_DIM)))
    return dx, dgain


def _fold_heads(v):
    acc = v[:, 0:BLOCK]
    for t in range(1, v.shape[1] // BLOCK):
        acc = acc + v[:, t * BLOCK:(t + 1) * BLOCK]
    return acc + pltpu.roll(acc, HEAD_DIM, 1)


def _tile_lanes(v, width):
    return v if width == BLOCK else jnp.tile(v, (1, width // BLOCK))


def _qk_prep(proj, qg, kg, cos, sin, AW, KW):
    S = proj.shape[0]
    tr = _tile(S, 256)
    scale = HEAD_DIM ** -0.5

    def body(q_ref, k_ref, v_ref, qg_ref, kg_ref, cos_ref, sin_ref, qo_ref, ko_ref, vo_ref):
        c, s = cos_ref[...], sin_ref[...]
        q = _norm_rope(q_ref[...], _tile_lanes(qg_ref[...], AW), _tile_lanes(c, AW), _tile_lanes(s, AW))
        k = _norm_rope(k_ref[...], _tile_lanes(kg_ref[...], KW), _tile_lanes(c, KW), _tile_lanes(s, KW))
        qo_ref[...] = (q * scale).astype(MXU)
        ko_ref[...] = k.astype(MXU)
        vo_ref[...] = v_ref[...].astype(MXU)

    assert AW % KW == 0
    vec = pl.BlockSpec((1, BLOCK), lambda i: (0, 0))
    tab = pl.BlockSpec((tr, BLOCK), lambda i: (i, 0))
    return pl.pallas_call(
        body, name="qk_prep", grid=(S // tr,),
        in_specs=[pl.BlockSpec((tr, AW), lambda i: (i, 0)),
                  pl.BlockSpec((tr, KW), lambda i: (i, AW // KW)),
                  pl.BlockSpec((tr, KW), lambda i: (i, AW // KW + 1)), vec, vec, tab, tab],
        out_specs=[pl.BlockSpec((tr, AW), lambda i: (i, 0)), pl.BlockSpec((tr, KW), lambda i: (i, 0)),
                   pl.BlockSpec((tr, KW), lambda i: (i, 0))],
        out_shape=[jax.ShapeDtypeStruct((S, AW), MXU), jax.ShapeDtypeStruct((S, KW), MXU),
                   jax.ShapeDtypeStruct((S, KW), MXU)],
        compiler_params=_params())(proj, proj, proj, qg, kg, cos, sin)


def _attn_probs(n, q, kp, kc, g, sink_ref, qpk):
    hd = HEAD_DIM
    kcat = jnp.concatenate([kp[:, g * hd:(g + 1) * hd], kc[:, g * hd:(g + 1) * hd]], axis=0)
    qs = jnp.concatenate([q[:, (g * qpk + j) * hd:(g * qpk + j + 1) * hd] for j in range(qpk)], axis=0)
    s = lax.dot_general(qs, kcat, NT, preferred_element_type=F32)
    row = lax.broadcasted_iota(jnp.int32, (BLOCK, 2 * BLOCK), 0)
    col = lax.broadcasted_iota(jnp.int32, (BLOCK, 2 * BLOCK), 1)
    ok = (col > row) & (col <= row + BLOCK) & ((col >= BLOCK) | (n > 0))
    s = jnp.where(jnp.concatenate([ok] * qpk, axis=0), s, -1e30)
    sk = jnp.concatenate([jnp.full((BLOCK, 1), sink_ref[0, g * qpk + j], F32) for j in range(qpk)], axis=0)
    m = jnp.maximum(jnp.max(s, axis=-1, keepdims=True), sk)
    e = jnp.exp(s - m)
    es = jnp.exp(sk - m)
    z = jnp.sum(e, axis=-1, keepdims=True) + es
    return e / z, es / z, qs, kcat


def _attn_fwd(qr, kr, vb, sinks):
    S, AW = qr.shape
    KW = kr.shape[1]
    nb = S // BLOCK
    nkv = KW // HEAD_DIM
    qpk = AW // KW
    hd = HEAD_DIM

    def body(sink_ref, q_ref, kp_ref, kc_ref, vp_ref, vc_ref, o_ref):
        n = pl.program_id(0)
        q, kp, kc, vp, vc = q_ref[...], kp_ref[...], kc_ref[...], vp_ref[...], vc_ref[...]
        outs = [None] * (nkv * qpk)
        for g in range(nkv):
            p, _, _, _ = _attn_probs(n, q, kp, kc, g, sink_ref, qpk)
            vcat = jnp.concatenate([vp[:, g * hd:(g + 1) * hd], vc[:, g * hd:(g + 1) * hd]], axis=0)
            o = jnp.dot(p.astype(MXU), vcat, preferred_element_type=F32)
            for j in range(qpk):
                outs[g * qpk + j] = o[j * BLOCK:(j + 1) * BLOCK]
        o_ref[...] = jnp.concatenate(outs, axis=-1).astype(MXU)

    cur = lambda n: (n, 0)
    prev = lambda n: (jnp.maximum(n - 1, 0), 0)
    return pl.pallas_call(
        body, name="attn_fwd", grid=(nb,),
        in_specs=[pl.BlockSpec(memory_space=pltpu.SMEM), pl.BlockSpec((BLOCK, AW), cur),
                  pl.BlockSpec((BLOCK, KW), prev), pl.BlockSpec((BLOCK, KW), cur),
                  pl.BlockSpec((BLOCK, KW), prev), pl.BlockSpec((BLOCK, KW), cur)],
        out_specs=pl.BlockSpec((BLOCK, AW), cur),
        out_shape=jax.ShapeDtypeStruct((S, AW), MXU), compiler_params=_params())(sinks, qr, kr, kr, vb, vb)


def _attn_bwd(qr, kr, vb, sinks, dattn):
    S, AW = qr.shape
    KW = kr.shape[1]
    nb = S // BLOCK
    nkv = KW // HEAD_DIM
    qpk = AW // KW
    hd = HEAD_DIM
    scale = HEAD_DIM ** -0.5

    def body(sink_ref, q_ref, kp_ref, kc_ref, vp_ref, vc_ref, do_ref,
             dq_ref, dkp_ref, dkc_ref, dvp_ref, dvc_ref, dsink_ref):
        n = pl.program_id(0)
        q, kp, kc, vp, vc = q_ref[...], kp_ref[...], kc_ref[...], vp_ref[...], vc_ref[...]
        do = do_ref[...].astype(MXU)
        lane = lax.broadcasted_iota(jnp.int32, (1, BLOCK), 1)
        dsink = jnp.zeros((1, BLOCK), F32)
        dqs = [None] * (nkv * qpk)
        dkps, dkcs, dvps, dvcs = [], [], [], []
        for g in range(nkv):
            p, psink, qs, kcat = _attn_probs(n, q, kp, kc, g, sink_ref, qpk)
            vcat = jnp.concatenate([vp[:, g * hd:(g + 1) * hd], vc[:, g * hd:(g + 1) * hd]], axis=0)
            dos = jnp.concatenate([do[:, (g * qpk + j) * hd:(g * qpk + j + 1) * hd] for j in range(qpk)], axis=0)
            dp = lax.dot_general(dos, vcat, NT, preferred_element_type=F32)
            dv = lax.dot_general(p.astype(MXU), dos, TN, preferred_element_type=F32)
            delta = jnp.sum(p * dp, axis=-1, keepdims=True)
            ds = (p * (dp - delta)).astype(MXU)
            dsk = -psink * delta
            dq = jnp.dot(ds, kcat, preferred_element_type=F32) * scale
            dk = lax.dot_general(ds, qs, TN, preferred_element_type=F32)
            for j in range(qpk):
                dqs[g * qpk + j] = dq[j * BLOCK:(j + 1) * BLOCK]
                tot = jnp.sum(dsk[j * BLOCK:(j + 1) * BLOCK], axis=0, keepdims=True)
                dsink = dsink + jnp.where(lane == g * qpk + j, tot, 0.0)
            dkps.append(dk[:BLOCK])
            dkcs.append(dk[BLOCK:])
            dvps.append(dv[:BLOCK])
            dvcs.append(dv[BLOCK:])
        dq_ref[...] = jnp.concatenate(dqs, axis=-1)
        dkp_ref[...] = jnp.concatenate(dkps, axis=-1)
        dkc_ref[...] = jnp.concatenate(dkcs, axis=-1)
        dvp_ref[...] = jnp.concatenate(dvps, axis=-1)
        dvc_ref[...] = jnp.concatenate(dvcs, axis=-1)

        @pl.when(n == 0)
        def _():
            dsink_ref[...] = jnp.zeros(dsink_ref.shape, F32)

        dsink_ref[...] += dsink

    cur = lambda n: (n, 0)
    prev = lambda n: (jnp.maximum(n - 1, 0), 0)
    kv = jax.ShapeDtypeStruct((S, KW), F32)
    kvspec = pl.BlockSpec((BLOCK, KW), cur)
    return pl.pallas_call(
        body, name="attn_bwd", grid=(nb,),
        in_specs=[pl.BlockSpec(memory_space=pltpu.SMEM), pl.BlockSpec((BLOCK, AW), cur),
                  pl.BlockSpec((BLOCK, KW), prev), kvspec, pl.BlockSpec((BLOCK, KW), prev), kvspec,
                  pl.BlockSpec((BLOCK, AW), cur)],
        out_specs=[pl.BlockSpec((BLOCK, AW), cur), kvspec, kvspec, kvspec, kvspec,
                   pl.BlockSpec((1, BLOCK), lambda n: (0, 0))],
        out_shape=[jax.ShapeDtypeStruct((S, AW), F32), kv, kv, kv, kv, jax.ShapeDtypeStruct((1, BLOCK), F32)],
        compiler_params=_params())(sinks, qr, kr, kr, vb, vb, dattn)


def _qk_prep_bwd(proj, qg, kg, cos, sin, dq, dkp, dkc, dvp, dvc, AW, KW):
    S = proj.shape[0]
    nb = S // BLOCK

    def body(q_ref, k_ref, qg_ref, kg_ref, cos_ref, sin_ref, dq_ref, dkp_ref, dkc_ref, dvp_ref, dvc_ref,
             o_ref, dqg_ref, dkg_ref):
        n = pl.program_id(0)
        c, s = cos_ref[...], sin_ref[...]
        has_next = jnp.where(n < nb - 1, 1.0, 0.0)
        dk = dkc_ref[...] + has_next * dkp_ref[...]
        dv = dvc_ref[...] + has_next * dvp_ref[...]
        dxq, dqg = _norm_rope_bwd(dq_ref[...], q_ref[...], _tile_lanes(qg_ref[...], AW),
                                  _tile_lanes(c, AW), _tile_lanes(s, AW))
        dxk, dkg = _norm_rope_bwd(dk, k_ref[...], _tile_lanes(kg_ref[...], KW),
                                  _tile_lanes(c, KW), _tile_lanes(s, KW))
        o_ref[...] = jnp.concatenate([dxq, dxk, dv], axis=-1).astype(MXU)

        @pl.when(n == 0)
        def _():
            dqg_ref[...] = jnp.zeros(dqg_ref.shape, F32)
            dkg_ref[...] = jnp.zeros(dkg_ref.shape, F32)

        dqg_ref[...] += _fold_heads(dqg)
        dkg_ref[...] += _fold_heads(dkg)

    cur = lambda n: (n, 0)
    nxt = lambda n: (jnp.minimum(n + 1, nb - 1), 0)
    vec = pl.BlockSpec((1, BLOCK), lambda n: (0, 0))
    tab = pl.BlockSpec((BLOCK, BLOCK), cur)
    return pl.pallas_call(
        body, name="qk_prep_bwd", grid=(nb,),
        in_specs=[pl.BlockSpec((BLOCK, AW), cur), pl.BlockSpec((BLOCK, KW), lambda n: (n, AW // KW)),
                  vec, vec, tab, tab, pl.BlockSpec((BLOCK, AW), cur),
                  pl.BlockSpec((BLOCK, KW), nxt), pl.BlockSpec((BLOCK, KW), cur),
                  pl.BlockSpec((BLOCK, KW), nxt), pl.BlockSpec((BLOCK, KW), cur)],
        out_specs=[pl.BlockSpec((BLOCK, AW + 2 * KW), cur), vec, vec],
        out_shape=[jax.ShapeDtypeStruct((S, AW + 2 * KW), MXU), jax.ShapeDtypeStruct((1, BLOCK), F32),
                   jax.ShapeDtypeStruct((1, BLOCK), F32)],
        compiler_params=_params())(proj, proj, qg, kg, cos, sin, dq, dkp, dkc, dvp, dvc)


def _sgu_tile(pu_ref, pv_ref, lng_ref, lnb_ref, w_ref, b_ref):
    u = _gelu(pu_ref[...])
    v = _gelu(pv_ref[...])
    mu = jnp.mean(v, axis=-1, keepdims=True)
    vc = v - mu
    r = lax.rsqrt(jnp.mean(vc * vc, axis=-1, keepdims=True) + EPS)
    xh = vc * r
    vn = xh * lng_ref[...] + lnb_ref[...]
    row = lax.broadcasted_iota(jnp.int32, (BLOCK, BLOCK), 0)
    col = lax.broadcasted_iota(jnp.int32, (BLOCK, BLOCK), 1)
    tri = row >= col
    w = jnp.where(tri, w_ref[...], 0.0).astype(MXU)
    s = jnp.dot(w, vn.astype(MXU), preferred_element_type=F32) + b_ref[...]
    return u, xh, r, vn, w, s, tri


def _sgu_specs(u0):
    G = SGU_GROUPS
    return [pl.BlockSpec((BLOCK, BLOCK), lambda g, n: (n, u0 + g)),
            pl.BlockSpec((BLOCK, BLOCK), lambda g, n: (n, u0 + G + g)),
            pl.BlockSpec((1, BLOCK), lambda g, n: (0, g)), pl.BlockSpec((1, BLOCK), lambda g, n: (0, g)),
            pl.BlockSpec((None, BLOCK, BLOCK), lambda g, n: (g, 0, 0)),
            pl.BlockSpec((None, BLOCK, 1), lambda g, n: (g, 0, 0))]


def _sgu_fwd(proj, lng, lnb, ws, bs, u0):
    S = proj.shape[0]
    G = SGU_GROUPS

    def body(pu_ref, pv_ref, lng_ref, lnb_ref, w_ref, b_ref, o_ref):
        u, _, _, _, _, s, _ = _sgu_tile(pu_ref, pv_ref, lng_ref, lnb_ref, w_ref, b_ref)
        o_ref[...] = (u * s).astype(MXU)

    return pl.pallas_call(
        body, name="sgu_fwd", grid=(G, S // BLOCK), in_specs=_sgu_specs(u0),
        out_specs=pl.BlockSpec((BLOCK, BLOCK), lambda g, n: (n, g)),
        out_shape=jax.ShapeDtypeStruct((S, G * BLOCK), MXU), compiler_params=_params())(proj, proj, lng, lnb, ws, bs)


def _sgu_bwd(proj, lng, lnb, ws, bs, dsgu, u0):
    S = proj.shape[0]
    G = SGU_GROUPS

    def body(pu_ref, pv_ref, lng_ref, lnb_ref, w_ref, b_ref, do_ref,
             dpu_ref, dpv_ref, dw_ref, db_ref, dlng_ref, dlnb_ref):
        u, xh, r, vn, w, s, tri = _sgu_tile(pu_ref, pv_ref, lng_ref, lnb_ref, w_ref, b_ref)
        do = do_ref[...]
        dpu_ref[...] = (do * s * _gelu_grad(pu_ref[...])).astype(MXU)
        ds = do * u
        dsb = ds.astype(MXU)
        dw = jnp.where(tri, lax.dot_general(dsb, vn.astype(MXU), NT, preferred_element_type=F32), 0.0)
        dvn = lax.dot_general(w, dsb, TN, preferred_element_type=F32)
        dxh = dvn * lng_ref[...]
        dv = r * (dxh - jnp.mean(dxh, axis=-1, keepdims=True) - xh * jnp.mean(dxh * xh, axis=-1, keepdims=True))
        dpv_ref[...] = (dv * _gelu_grad(pv_ref[...])).astype(MXU)

        @pl.when(pl.program_id(1) == 0)
        def _():
            dw_ref[...] = jnp.zeros(dw_ref.shape, F32)
            db_ref[...] = jnp.zeros(db_ref.shape, F32)
            dlng_ref[...] = jnp.zeros(dlng_ref.shape, F32)
            dlnb_ref[...] = jnp.zeros(dlnb_ref.shape, F32)

        dw_ref[...] += dw
        db_ref[...] += jnp.sum(ds, axis=-1, keepdims=True)
        dlng_ref[...] += jnp.sum(dvn * xh, axis=0, keepdims=True)
        dlnb_ref[...] += jnp.sum(dvn, axis=0, keepdims=True)

    tile = lambda off: pl.BlockSpec((BLOCK, BLOCK), lambda g, n: (n, off + g))
    vec = pl.BlockSpec((1, BLOCK), lambda g, n: (0, g))
    half = jax.ShapeDtypeStruct((S, G * BLOCK), MXU)
    return pl.pallas_call(
        body, name="sgu_bwd", grid=(G, S // BLOCK), in_specs=_sgu_specs(u0) + [tile(0)],
        out_specs=[tile(0), tile(0), pl.BlockSpec((None, BLOCK, BLOCK), lambda g, n: (g, 0, 0)),
                   pl.BlockSpec((None, BLOCK, 1), lambda g, n: (g, 0, 0)), vec, vec],
        out_shape=[half, half, jax.ShapeDtypeStruct((G, BLOCK, BLOCK), F32),
                   jax.ShapeDtypeStruct((G, BLOCK, 1), F32),
                   jax.ShapeDtypeStruct((1, G * BLOCK), F32), jax.ShapeDtypeStruct((1, G * BLOCK), F32)],
        compiler_params=_params())(proj, proj, lng, lnb, ws, bs, dsgu)


def _store_f32(vals, extra, outs):
    for v, o in zip(vals, outs):
        o[...] = v


def _store_mxu(vals, extra, outs):
    for v, o in zip(vals, outs):
        o[...] = v.astype(MXU)


def _proj_in(h, w):
    S, D = h.shape
    Ns = w.shape[2]
    tm, tn = _tile(S, 1024), _tile(Ns, 1024)
    npb = Ns // tn
    return _mm("proj_in", (S // tm, N_CHIPS, npb), 0, [h, w],
               [pl.BlockSpec((tm, D), lambda i, s, j: (i, 0)), pl.BlockSpec((None, D, tn), lambda i, s, j: (s, 0, j))],
               [(0, 1, 0)], NN, 0, [jax.ShapeDtypeStruct((S, N_CHIPS * Ns), F32)],
               [pl.BlockSpec((tm, tn), lambda i, s, j: (i, s * npb + j))], [None], _store_f32)[0]


def _branches(attn, sgu, wa, ws, proj, gate0):
    S, AW = attn.shape
    SW = sgu.shape[1]
    Nb = wa.shape[2]
    D = N_CHIPS * Nb
    tm = _tile(S, 512)
    assert gate0 % Nb == 0
    ga, gb = gate0 // Nb, (gate0 + D) // Nb

    def epilogue(vals, extra, outs):
        a, b = vals
        outs[0][...] = (_sigmoid(extra[0][...]) * a + _sigmoid(extra[1][...]) * b).astype(MXU)
        outs[1][...] = a
        outs[2][...] = b

    tile = pl.BlockSpec((tm, Nb), lambda i, s: (i, s))
    wspec = lambda k: pl.BlockSpec((None, k, Nb), lambda i, s: (s, 0, 0))
    f = jax.ShapeDtypeStruct((S, D), F32)
    return _mm("branches", (S // tm, N_CHIPS), 0, [attn, sgu, wa, ws, proj, proj],
               [pl.BlockSpec((tm, AW), lambda i, s: (i, 0)), pl.BlockSpec((tm, SW), lambda i, s: (i, 0)),
                wspec(AW), wspec(SW), pl.BlockSpec((tm, Nb), lambda i, s: (i, ga + s)),
                pl.BlockSpec((tm, Nb), lambda i, s: (i, gb + s))],
               [(0, 2, 0), (1, 3, 1)], NN, 2, [jax.ShapeDtypeStruct((S, D), MXU), f, f], [tile] * 3,
               [None, None], epilogue)


def _rows_mm(name, a, w, res):
    S = a.shape[0]
    _, K, N = w.shape
    tm, tn = _tile(S, 1024), _tile(N, 1024)

    def epilogue(vals, extra, outs):
        outs[0][...] = extra[0][...] + vals[0]

    out = pl.BlockSpec((tm, tn), lambda i, j, s: (i, j))
    return _mm(name, (S // tm, N // tn, N_CHIPS), 1, [a, w, res],
               [pl.BlockSpec((tm, K), lambda i, j, s: (i, s)), pl.BlockSpec((None, K, tn), lambda i, j, s: (s, 0, j)), out],
               [(0, 1, 0)], NN, 1, [jax.ShapeDtypeStruct((S, N), F32)], [out], [(tm, tn)], epilogue)[0]


def _gate_up(h2, wg, wu):
    S, D = h2.shape
    Nf = wg.shape[2]
    tm, tk = _tile(S, 512), _tile(D, 512)

    def epilogue(vals, extra, outs):
        g, u = vals
        outs[0][...] = g
        outs[1][...] = u
        outs[2][...] = (g * _sigmoid(g) * u).astype(MXU)

    w = pl.BlockSpec((None, tk, Nf), lambda i, s, k: (s, k, 0))
    o = pl.BlockSpec((tm, Nf), lambda i, s, k: (i, s))
    f = jax.ShapeDtypeStruct((S, N_CHIPS * Nf), F32)
    return _mm("gate_up", (S // tm, N_CHIPS, D // tk), 1, [h2, wg, wu],
               [pl.BlockSpec((tm, tk), lambda i, s, k: (i, k)), w, w], [(0, 1, 0), (0, 2, 1)], NN, 0,
               [f, f, jax.ShapeDtypeStruct((S, N_CHIPS * Nf), MXU)], [o, o, o], [(tm, Nf)] * 2, epilogue)


def _down_bwd(dyb, wd, g, u):
    S, D = dyb.shape
    Kf = wd.shape[1]
    tm = _tile(S, 512)

    def epilogue(vals, extra, outs):
        da, gv, uv = vals[0], extra[0][...], extra[1][...]
        sg = _sigmoid(gv)
        outs[0][...] = (da * uv * sg * (1.0 + gv * (1.0 - sg))).astype(MXU)
        outs[1][...] = (da * gv * sg).astype(MXU)

    t = pl.BlockSpec((tm, Kf), lambda i, s: (i, s))
    o = jax.ShapeDtypeStruct((S, N_CHIPS * Kf), MXU)
    return _mm("down_bwd", (S // tm, N_CHIPS), 0, [dyb, wd, g, u],
               [pl.BlockSpec((tm, D), lambda i, s: (i, 0)), pl.BlockSpec((None, Kf, D), lambda i, s: (s, 0, 0)), t, t],
               [(0, 1, 0)], NT, 2, [o, o], [t, t], [None], epilogue)


def _out_bwd(dxb, wo, proj, ba, bb, gate0):
    S, D = dxb.shape
    Ko = wo.shape[1]
    tm = _tile(S, 512)
    assert gate0 % Ko == 0
    ga, gb = gate0 // Ko, (gate0 + D) // Ko

    def epilogue(vals, extra, outs):
        dm = vals[0]
        sa, sb = _sigmoid(extra[0][...]), _sigmoid(extra[1][...])
        outs[0][...] = (dm * sa).astype(MXU)
        outs[1][...] = (dm * sb).astype(MXU)
        outs[2][...] = (dm * extra[2][...] * sa * (1.0 - sa)).astype(MXU)
        outs[3][...] = (dm * extra[3][...] * sb * (1.0 - sb)).astype(MXU)

    t = pl.BlockSpec((tm, Ko), lambda i, s: (i, s))
    o = jax.ShapeDtypeStruct((S, D), MXU)
    return _mm("out_bwd", (S // tm, N_CHIPS), 0, [dxb, wo, proj, proj, ba, bb],
               [pl.BlockSpec((tm, D), lambda i, s: (i, 0)), pl.BlockSpec((None, Ko, D), lambda i, s: (s, 0, 0)),
                pl.BlockSpec((tm, Ko), lambda i, s: (i, ga + s)), pl.BlockSpec((tm, Ko), lambda i, s: (i, gb + s)), t, t],
               [(0, 1, 0)], NT, 4, [o] * 4, [t] * 4, [None], epilogue)


def _dx_cols(name, terms, n_out, after=None):
    S = terms[0][0].shape[0]
    _, K, Ns = terms[0][1].shape
    tm, tko, tn = _tile(S, 1024), _tile(K, 1024), _tile(Ns, 1408)
    npb = Ns // tn
    operands, specs, pairs = [], [], []
    for t, (dy, w, k) in enumerate(terms):
        assert w.shape == (N_CHIPS, K, Ns)
        operands += [dy, w]
        specs += [pl.BlockSpec((tm, tn), lambda i, jk, s, jn: (i, s * npb + jn)),
                  pl.BlockSpec((None, tko, tn), lambda i, jk, s, jn: (s, jk, jn))]
        pairs.append((2 * t, 2 * t + 1, k))
    out = pl.BlockSpec((tm, tko), lambda i, jk, s, jn: (i, jk))
    return _mm(name, (S // tm, K // tko, N_CHIPS, npb), 2, operands, specs, pairs, NT, 0,
               [jax.ShapeDtypeStruct((S, K), F32)] * n_out, [out] * n_out, [(tm, tko)] * n_out, _store_f32, after)


def _dw_cols(name, a, dy):
    S, K = a.shape
    Ns = dy.shape[1] // N_CHIPS
    tk, tn = _tile(K, 512), _tile(Ns, 1408)
    npb = Ns // tn
    return _mm(name, (K // tk, N_CHIPS, npb), 0, [a, dy],
               [pl.BlockSpec((S, tk), lambda jk, s, jn: (0, jk)), pl.BlockSpec((S, tn), lambda jk, s, jn: (0, s * npb + jn))],
               [(0, 1, 0)], TN, 0, [jax.ShapeDtypeStruct((N_CHIPS, K, Ns), MXU)],
               [pl.BlockSpec((None, tk, tn), lambda jk, s, jn: (s, jk, jn))], [None], _store_mxu)[0]


def _dw_rows(name, a, dy):
    S = a.shape[0]
    K = a.shape[1] // N_CHIPS
    N = dy.shape[1]
    tk, tn = _tile(K, 1408), _tile(N, 1024)
    nkb = K // tk
    return _mm(name, (N_CHIPS, nkb, N // tn), 0, [a, dy],
               [pl.BlockSpec((S, tk), lambda s, jk, jn: (0, s * nkb + jk)), pl.BlockSpec((S, tn), lambda s, jk, jn: (0, jn))],
               [(0, 1, 0)], TN, 0, [jax.ShapeDtypeStruct((N_CHIPS, K, N), MXU)],
               [pl.BlockSpec((None, tk, tn), lambda s, jk, jn: (s, jk, jn))], [None], _store_mxu)[0]


def _layer_fwd(x, w, stream, layer, last, sp, cos, sin, dims):
    AW, KW, gate0, u0 = dims
    w = dict(w)
    h = _rms_fwd("mix_norm", x, sp["mix_norm"])
    proj = _proj_in(h, w["w_in"])
    qr, kr, vb = _qk_prep(proj, sp["q_norm"], sp["k_norm"], cos, sin, AW, KW)
    stream.forward(layer, 1, qr)
    attn = _attn_fwd(qr, kr, vb, sp["sinks"])
    w.update(stream.finish(layer, 1, attn))
    sgu = _sgu_fwd(proj, sp["sgu_ln_g"], sp["sgu_ln_b"], sp["w_spatial"], sp["b_spatial"], u0)
    merged, ba, bb = _branches(attn, sgu, w["w_attn_branch"], w["w_sgu_branch"], proj, gate0)
    stream.forward(layer, 2, merged)
    x1 = _rows_mm("out_proj", merged, w["w_out"], x)
    w.update(stream.finish(layer, 2, x1))
    h2 = _rms_fwd("ffn_norm", x1, sp["ffn_norm"])
    g, u, act = _gate_up(h2, w["w_gate"], w["w_up"])
    if not last:
        stream.forward(layer + 1, 0, g)
    x2 = _rows_mm("down_proj", act, w["w_down"], x1)
    w_next = None if last else stream.finish(layer + 1, 0, x2)
    saved = dict(x=x, h=h, proj=proj, qr=qr, kr=kr, vb=vb, attn=attn, sgu=sgu, merged=merged, ba=ba, bb=bb,
                 x1=x1, h2=h2, g=g, u=u, act=act)
    return x2, saved, w, w_next


def _layer_bwd(dy, dyb, w, sp, sv, cos, sin, dims, reduce_group):
    AW, KW, gate0, u0 = dims
    big, small = {}, {}
    dg, du = _down_bwd(dyb, w["w_down"], sv["g"], sv["u"])
    big["w_down"] = _dw_rows("dw_down", sv["act"], dyb)
    big["w_gate"] = _dw_cols("dw_gate", sv["h2"], dg)
    big["w_up"] = _dw_cols("dw_up", sv["h2"], du)
    token = reduce_group(2, big)
    dh2 = _dx_cols("dh2", [(dg, w["w_gate"], 0), (du, w["w_up"], 0)], 1, token)[0]
    dx1, dx1b, small["ffn_norm"] = _rms_bwd("ffn_norm_bwd", dh2, sv["x1"], sp["ffn_norm"], dy)
    dba, dbb, dgla, dglb = _out_bwd(dx1b, w["w_out"], sv["proj"], sv["ba"], sv["bb"], gate0)
    big["w_out"] = _dw_rows("dw_out", sv["merged"], dx1b)
    big["w_attn_branch"] = _dw_cols("dw_attn_branch", sv["attn"], dba)
    big["w_sgu_branch"] = _dw_cols("dw_sgu_branch", sv["sgu"], dbb)
    token = reduce_group(1, big)
    dattn, dsgu = _dx_cols("dbranch_in", [(dba, w["w_attn_branch"], 0), (dbb, w["w_sgu_branch"], 1)], 2, token)
    dpu, dpv, small["w_spatial"], db, small["sgu_ln_g"], small["sgu_ln_b"] = _sgu_bwd(
        sv["proj"], sp["sgu_ln_g"], sp["sgu_ln_b"], sp["w_spatial"], sp["b_spatial"], dsgu, u0)
    small["b_spatial"] = db[:, :, 0]
    dq, dkp, dkc, dvp, dvc, dsink = _attn_bwd(sv["qr"], sv["kr"], sv["vb"], sp["sinks"], dattn)
    small["sinks"] = dsink[:, :sp["sinks"].shape[1]]
    dqkv, dqg, dkg = _qk_prep_bwd(sv["proj"], sp["q_norm"], sp["k_norm"], cos, sin, dq, dkp, dkc, dvp, dvc, AW, KW)
    small["q_norm"] = dqg[:, :HEAD_DIM]
    small["k_norm"] = dkg[:, :HEAD_DIM]
    dproj = jnp.concatenate([dqkv, dpu, dpv, dgla, dglb], axis=1)
    big["w_in"] = _dw_cols("dw_in", sv["h"], dproj)
    token = reduce_group(0, big)
    dh = _dx_cols("dh", [(dproj, w["w_in"], 0)], 1, token)[0]
    dx, dxb, small["mix_norm"] = _rms_bwd("mix_norm_bwd", dh, sv["x"], sp["mix_norm"], dx1)
    return dx, dxb, small


def _place():
    x, y, c = lax.axis_index("x"), lax.axis_index("y"), lax.axis_index("c")
    chips = [(1 - x, y), (x, 1 - y), (1 - x, 1 - y)]
    return x, y, c, chips


def _half_rows(c, rows):
    h = rows // 2
    assert h % 16 == 0
    return pl.ds(pl.multiple_of(c * h, 16), h)


def _row_tile(rows, pref):
    best = None
    for t in range(16, min(rows, pref) + 1, 16):
        if rows % t == 0:
            best = t
    assert best is not None, rows
    return best


def _cast_own(name, chip, w, layer):
    _, R, C = w.shape
    tr = _row_tile(R, 256)

    def body(chip_ref, w_ref, o_ref):
        o_ref[...] = w_ref[...].astype(MXU)

    return pl.pallas_call(
        body, name=name, out_shape=jax.ShapeDtypeStruct((N_CHIPS, R, C), MXU),
        grid_spec=pltpu.PrefetchScalarGridSpec(
            num_scalar_prefetch=1, grid=(R // tr,),
            in_specs=[pl.BlockSpec((None, tr, C), lambda i, chip_ref: (layer, i, 0))],
            out_specs=pl.BlockSpec((None, tr, C), lambda i, chip_ref: (chip_ref[0], i, 0))),
        compiler_params=_params())(chip, w)


HBM = pl.BlockSpec(memory_space=pltpu.HBM)
SEM = pl.BlockSpec(memory_space=pltpu.SEMAPHORE)
DATAFLOW = pltpu.SideEffectType.DATAFLOW_SIDE_EFFECTING


def _gather_copies(bufs, send_sem, recv_sem):
    x, y, c, chips = _place()

    def ici(a, j, block):
        px, py = chips[j]
        blk = bufs[a].at[block, _half_rows(c, bufs[a].shape[1])]
        return pltpu.make_async_remote_copy(
            src_ref=blk, dst_ref=blk, send_sem=send_sem.at[3 * a + j], recv_sem=recv_sem.at[3 * a + j],
            device_id=(px, py, c), device_id_type=MESH)

    def d2d(a, j, core):
        px, py = chips[j]
        blk = bufs[a].at[2 * px + py, _half_rows(core, bufs[a].shape[1])]
        return pltpu.make_async_remote_copy(
            src_ref=blk, dst_ref=blk, send_sem=send_sem.at[3 * a + j], recv_sem=recv_sem.at[3 * a + j],
            device_id=(x, y, 1 - c), device_id_type=MESH)

    return ici, d2d


def _in_hbm(bufs):
    return [pltpu.with_memory_space_constraint(b, pltpu.HBM) for b in bufs]


def _gather_start(name, bufs, after):
    n = len(bufs)

    def body(*refs):
        dst = refs[n + 1:2 * n + 1]
        send_sem, recv_sem, token = refs[2 * n + 1:]
        x, y, c, chips = _place()
        ici, _ = _gather_copies(dst, send_sem, recv_sem)
        for a in range(n):
            for j in range(3):
                ici(a, j, 2 * x + y).start()
        token[...] = jnp.zeros(token.shape, token.dtype)

    sems = pltpu.SemaphoreType.DMA((3 * n,))
    outs = pl.pallas_call(
        body, name=name, in_specs=[HBM] * n + [ANY],
        out_specs=[HBM] * n + [SEM, SEM, pl.BlockSpec(memory_space=pltpu.VMEM)],
        out_shape=[pltpu.HBM(b.shape, b.dtype) for b in bufs] + [sems, sems, jax.ShapeDtypeStruct((8, BLOCK), F32)],
        input_output_aliases={a: a for a in range(n)},
        compiler_params=pltpu.CompilerParams(has_side_effects=DATAFLOW))(*_in_hbm(bufs), after)
    return outs[:n], outs[n], outs[n + 1], outs[n + 2]


def _gather_forward(name, bufs, ici_send, ici_recv, after):
    n = len(bufs)

    def body(*refs):
        ici_send_ref, ici_recv_ref = refs[n], refs[n + 1]
        dst = refs[n + 3:2 * n + 3]
        d2d_send, d2d_recv = refs[2 * n + 3:]
        x, y, c, chips = _place()
        ici, _ = _gather_copies(dst, ici_send_ref, ici_recv_ref)
        _, d2d = _gather_copies(dst, d2d_send, d2d_recv)
        for a in range(n):
            for j, (px, py) in enumerate(chips):
                ici(a, j, 2 * px + py).wait_recv()
                d2d(a, j, c).start()
        for a in range(n):
            for j in range(3):
                ici(a, j, 2 * x + y).wait_send()

    sems = pltpu.SemaphoreType.DMA((3 * n,))
    outs = pl.pallas_call(
        body, name=name, in_specs=[HBM] * n + [SEM, SEM, ANY], out_specs=[HBM] * n + [SEM, SEM],
        out_shape=[pltpu.HBM(b.shape, b.dtype) for b in bufs] + [sems, sems],
        input_output_aliases={a: a for a in range(n)},
        compiler_params=pltpu.CompilerParams(has_side_effects=DATAFLOW))(*bufs, ici_send, ici_recv, after)
    return outs[:n], outs[n], outs[n + 1]


def _gather_finish(name, bufs, d2d_send, d2d_recv, after):
    n = len(bufs)

    def body(*refs):
        send_ref, recv_ref = refs[n], refs[n + 1]
        dst = refs[n + 3:]
        x, y, c, chips = _place()
        _, d2d = _gather_copies(dst, send_ref, recv_ref)
        for a in range(n):
            for j in range(3):
                d2d(a, j, 1 - c).wait_recv()
                d2d(a, j, c).wait_send()

    return pl.pallas_call(
        body, name=name, in_specs=[HBM] * n + [SEM, SEM, ANY], out_specs=[HBM] * n,
        out_shape=[pltpu.HBM(b.shape, b.dtype) for b in bufs],
        input_output_aliases={a: a for a in range(n)},
        compiler_params=pltpu.CompilerParams(has_side_effects=DATAFLOW))(*bufs, d2d_send, d2d_recv, after)


GROUPS = (("w_in",), ("w_attn_branch", "w_sgu_branch", "w_out"), ("w_gate", "w_up", "w_down"))


class _WeightStream:
    def __init__(self, started):
        self.started, self.passed = started, {}

    def forward(self, layer, group, after):
        bufs, send, recv = self.started[(layer, group)]
        self.passed[(layer, group)] = _gather_forward("gather_forward_%d_%d" % (layer, group), bufs, send, recv, after)

    def finish(self, layer, group, after):
        bufs, send, recv = self.passed[(layer, group)]
        done = _gather_finish("gather_finish_%d_%d" % (layer, group), bufs, send, recv, after)
        return dict(zip(GROUPS[group], done))


def _pair_exchange(name, grads):
    n = len(grads)

    def body(*refs):
        src, dst = refs[:n], refs[n:2 * n]
        send_sem, recv_sem = refs[2 * n:]
        x, y, c, _ = _place()
        copies = []
        for a in range(n):
            theirs = _half_rows(1 - c, src[a].shape[1])
            copies.append(pltpu.make_async_remote_copy(
                src_ref=src[a].at[:, theirs], dst_ref=dst[a], send_sem=send_sem.at[a], recv_sem=recv_sem.at[a],
                device_id=(x, y, 1 - c), device_id_type=MESH))
            copies[-1].start()
        for cp in copies:
            cp.wait()

    out_shape = [jax.ShapeDtypeStruct((g.shape[0], g.shape[1] // 2, g.shape[2]), g.dtype) for g in grads]
    return pl.pallas_call(
        body, name=name, in_specs=[ANY] * n, out_specs=[ANY] * n, out_shape=out_shape,
        scratch_shapes=[pltpu.SemaphoreType.DMA((n,))] * 2)(*grads)


def _pair_sum(name, core, g, p):
    _, h, C = p.shape
    tr = _row_tile(h, 256)
    nrb = h // tr

    def body(core_ref, g_ref, p_ref, o_ref):
        o_ref[...] = (g_ref[...].astype(F32) + p_ref[...].astype(F32)).astype(o_ref.dtype)

    spec = pl.BlockSpec((None, tr, C), lambda s, i, core_ref: (s, i, 0))
    return pl.pallas_call(
        body, name=name, out_shape=jax.ShapeDtypeStruct(p.shape, p.dtype),
        grid_spec=pltpu.PrefetchScalarGridSpec(
            num_scalar_prefetch=1, grid=(N_CHIPS, nrb),
            in_specs=[pl.BlockSpec((None, tr, C), lambda s, i, core_ref: (s, core_ref[0] * nrb + i, 0)), spec],
            out_specs=spec),
        compiler_params=_params())(core, g, p)


def _scatter_copies(sums, slots, send_sem, recv_sem):
    x, y, c, chips = _place()

    def make(a, j):
        px, py = chips[j]
        return pltpu.make_async_remote_copy(
            src_ref=sums[a].at[2 * px + py], dst_ref=slots[a].at[j], send_sem=send_sem.at[3 * a + j],
            recv_sem=recv_sem.at[3 * a + j], device_id=(px, py, c), device_id_type=MESH)

    return make


def _scatter_start(name, sums, after):
    n = len(sums)
    slots = [lax.empty((3,) + s.shape[1:], s.dtype) for s in sums]

    def body(*refs):
        src, dst = refs[2 * n + 1:3 * n + 1], refs[3 * n + 1:4 * n + 1]
        send_sem, recv_sem, token = refs[4 * n + 1:]
        copy = _scatter_copies(src, dst, send_sem, recv_sem)
        for a in range(n):
            for j in range(3):
                copy(a, j).start()
        token[...] = jnp.zeros(token.shape, token.dtype)

    sems = pltpu.SemaphoreType.DMA((3 * n,))
    outs = pl.pallas_call(
        body, name=name, in_specs=[HBM] * (2 * n) + [ANY],
        out_specs=[HBM] * (2 * n) + [SEM, SEM, pl.BlockSpec(memory_space=pltpu.VMEM)],
        out_shape=[pltpu.HBM(b.shape, b.dtype) for b in sums + slots] + [sems, sems, jax.ShapeDtypeStruct((8, BLOCK), F32)],
        input_output_aliases={a: a for a in range(2 * n)},
        compiler_params=pltpu.CompilerParams(has_side_effects=DATAFLOW))(*_in_hbm(sums + slots), after)
    return outs[:n], outs[n:2 * n], outs[2 * n], outs[2 * n + 1], outs[2 * n + 2]


def _scatter_finish(name, sums, slots, send_sem, recv_sem, after):
    n = len(sums)

    def body(*refs):
        send_ref, recv_ref = refs[2 * n], refs[2 * n + 1]
        src, dst = refs[2 * n + 3:3 * n + 3], refs[3 * n + 3:]
        copy = _scatter_copies(src, dst, send_ref, recv_ref)
        for a in range(n):
            for j in range(3):
                copy(a, j).wait_send()
                copy(a, j).wait_recv()

    outs = pl.pallas_call(
        body, name=name, in_specs=[HBM] * (2 * n) + [SEM, SEM, ANY], out_specs=[HBM] * (2 * n),
        out_shape=[pltpu.HBM(b.shape, b.dtype) for b in sums + slots],
        input_output_aliases={a: a for a in range(2 * n)},
        compiler_params=pltpu.CompilerParams(has_side_effects=DATAFLOW))(*sums, *slots, send_sem, recv_sem, after)
    return outs[:n], outs[n:]


def _slot_sum(name, place, slots, sums):
    _, h, C = slots.shape
    tr = _row_tile(h, 256)
    nrb = h // tr

    def body(place_ref, r0, r1, r2, own, o_ref):
        o_ref[...] = ((r0[...].astype(F32) + r1[...].astype(F32)) + r2[...].astype(F32)) + own[...].astype(F32)

    slot = lambda k: pl.BlockSpec((None, tr, C), lambda i, place_ref: (k, i, 0))
    return pl.pallas_call(
        body, name=name, out_shape=jax.ShapeDtypeStruct((2 * h, C), F32),
        grid_spec=pltpu.PrefetchScalarGridSpec(
            num_scalar_prefetch=1, grid=(nrb,),
            in_specs=[slot(0), slot(1), slot(2),
                      pl.BlockSpec((None, tr, C), lambda i, place_ref: (place_ref[0], i, 0))],
            out_specs=pl.BlockSpec((tr, C), lambda i, place_ref: (place_ref[1] * nrb + i, 0))),
        compiler_params=_params())(place, slots, slots, slots, sums)


def _half_exchange(bufs):
    n = len(bufs)

    def body(*refs):
        dst = refs[n:2 * n]
        send_sem, recv_sem = refs[2 * n:]
        x, y, c, _ = _place()
        copies = []
        for a in range(n):
            mine = dst[a].at[_half_rows(c, dst[a].shape[0])]
            copies.append(pltpu.make_async_remote_copy(
                src_ref=mine, dst_ref=mine, send_sem=send_sem.at[a], recv_sem=recv_sem.at[a],
                device_id=(x, y, 1 - c), device_id_type=MESH))
            copies[-1].start()
        for cp in copies:
            cp.wait()

    return pl.pallas_call(
        body, name="grad_half_exchange", in_specs=[ANY] * n, out_specs=[ANY] * n,
        out_shape=[jax.ShapeDtypeStruct(b.shape, b.dtype) for b in bufs],
        input_output_aliases={a: a for a in range(n)},
        scratch_shapes=[pltpu.SemaphoreType.DMA((n,))] * 2)(*bufs)


class _GradReducer:
    def __init__(self, chip, core):
        self.core, self.place, self.started = core, jnp.concatenate([chip, core]), []

    def start(self, layer, group, grads):
        tag = "%d_%d" % (layer, group)
        names = GROUPS[group]
        mine = [grads[n] for n in names]
        theirs = _pair_exchange("grad_pair_exchange_" + tag, mine)
        sums = [_pair_sum("pair_sum_%s_%d" % (n, layer), self.core, g, p) for n, g, p in zip(names, mine, theirs)]
        sums, slots, send, recv, token = _scatter_start("grad_scatter_start_" + tag, sums, sums[0])
        self.started.append((layer, names, sums, slots, send, recv))
        return token

    def finish(self, after):
        halves, keys = [], []
        for layer, names, sums, slots, send, recv in self.started:
            sums, slots = _scatter_finish("grad_scatter_finish_%s_%d" % (names[0], layer), sums, slots, send, recv, after)
            for n, r, s in zip(names, slots, sums):
                halves.append(_slot_sum("slot_sum_%s_%d" % (n, layer), self.place, r, s))
                keys.append((n, layer))
            after = halves[-1]
        return dict(zip(keys, _half_exchange(halves)))


def _all_reduce_small(v):
    rows = v.shape[0]
    n_dev = 2 * N_CHIPS

    def body(x_ref, out_ref, gat_ref, send_sems, recv_sems, local_sem):
        x, y, c, chips = _place()
        me, sibling = (x, y, c), (x, y, 1 - c)

        def slot(px, py, pc):
            return gat_ref.at[4 * px + 2 * py + pc]

        def copy(k, block, to, src=None):
            return pltpu.make_async_remote_copy(
                src_ref=slot(*block) if src is None else src, dst_ref=slot(*block), send_sem=send_sems.at[k],
                recv_sem=recv_sems.at[k], device_id=to, device_id_type=MESH)

        mine = pltpu.make_async_copy(x_ref, slot(*me), local_sem)
        mine.start()
        first = [copy(0, me, sibling, src=x_ref)]
        first += [copy(1 + j, me, (*chip, c), src=x_ref) for j, chip in enumerate(chips)]
        for cp in first:
            cp.start()
        passed = [copy(4 + j, (*chip, c), sibling) for j, chip in enumerate(chips)]
        for j, chip in enumerate(chips):
            copy(1 + j, (*chip, c), me).wait_recv()
            passed[j].start()
        copy(0, sibling, me).wait_recv()
        for j, chip in enumerate(chips):
            copy(4 + j, (*chip, 1 - c), me).wait_recv()
        for cp in first + passed:
            cp.wait_send()
        mine.wait()
        acc = gat_ref[0]
        for d in range(1, n_dev):
            acc = acc + gat_ref[d]
        out_ref[...] = acc

    vm = pl.BlockSpec(memory_space=pltpu.VMEM)
    return pl.pallas_call(
        body, name="small_grad_all_reduce", in_specs=[vm], out_specs=vm,
        out_shape=jax.ShapeDtypeStruct(v.shape, F32),
        scratch_shapes=[pltpu.VMEM((n_dev, rows, BLOCK), F32), pltpu.SemaphoreType.DMA((7,)),
                        pltpu.SemaphoreType.DMA((7,)), pltpu.SemaphoreType.DMA],
        compiler_params=_params())(v)


def _adamw_math(w, g, m, v):
    m2 = ADAM_B1 * m + (1.0 - ADAM_B1) * g
    v2 = ADAM_B2 * v + (1.0 - ADAM_B2) * (g * g)
    m_hat = m2 / (1.0 - ADAM_B1 ** ADAM_STEP)
    v_hat = v2 / (1.0 - ADAM_B2 ** ADAM_STEP)
    delta = -ADAM_LR * (m_hat / (jnp.sqrt(v_hat) + ADAM_EPS) + ADAM_WD * w)
    return delta, m2, v2


def _adamw_big(name, grads, w, m, v):
    L, R, C = w.shape
    tr = _row_tile(R, 128)
    nrb = R // tr

    def body(*refs):
        g_refs = refs[:L]
        w_ref, m_ref, v_ref, go_ref, d_ref, mo_ref, vo_ref = refs[L:]
        layer = pl.program_id(0)
        g = g_refs[0][...]
        for k in range(1, L):
            g = jnp.where(layer == k, g_refs[k][...], g)
        delta, m2, v2 = _adamw_math(w_ref[...], g, m_ref[...], v_ref[...])
        go_ref[...] = g
        d_ref[...] = delta
        mo_ref[...] = m2
        vo_ref[...] = v2

    def gspec(k):
        return pl.BlockSpec((tr, C), lambda l, i: (jnp.where(l == k, i, (nrb - 1) * (k < l)), 0))

    blk = pl.BlockSpec((None, tr, C), lambda l, i: (l, i, 0))
    shp = jax.ShapeDtypeStruct(w.shape, F32)
    return pl.pallas_call(
        body, name=name, grid=(L, nrb), in_specs=[gspec(k) for k in range(L)] + [blk] * 3, out_specs=[blk] * 4,
        out_shape=[shp] * 4, compiler_params=_params())(*grads, w, m, v)


def _adamw_small(g, w, m, v):
    rows = g.shape[0]
    tr = _row_tile(rows, 512)

    def body(g_ref, w_ref, m_ref, v_ref, d_ref, mo_ref, vo_ref):
        delta, m2, v2 = _adamw_math(w_ref[...], g_ref[...], m_ref[...], v_ref[...])
        d_ref[...] = delta
        mo_ref[...] = m2
        vo_ref[...] = v2

    blk = pl.BlockSpec((tr, BLOCK), lambda i: (i, 0))
    shp = jax.ShapeDtypeStruct(g.shape, F32)
    return pl.pallas_call(
        body, name="adamw_small", grid=(rows // tr,), in_specs=[blk] * 4, out_specs=[blk] * 3, out_shape=[shp] * 3,
        compiler_params=_params())(g, w, m, v)


def _pack(arrays):
    flat = jnp.concatenate([a.reshape(-1) for a in arrays])
    pad = (-flat.shape[0]) % (16 * BLOCK)
    return jnp.pad(flat, (0, pad)).reshape(-1, BLOCK)


def _unpack(packed, like):
    flat = packed.reshape(-1)
    out, off = [], 0
    for a in like:
        out.append(flat[off:off + a.size].reshape(a.shape))
        off += a.size
    return out


BIG = ("w_in", "w_attn_branch", "w_sgu_branch", "w_out", "w_gate", "w_up", "w_down")
SMALL = ("mix_norm", "q_norm", "k_norm", "sinks", "sgu_ln_g", "sgu_ln_b", "w_spatial", "b_spatial", "ffn_norm")
ORDER = ("mix_norm", "w_in", "q_norm", "k_norm", "sinks", "sgu_ln_g", "sgu_ln_b", "w_spatial", "b_spatial",
         "w_attn_branch", "w_sgu_branch", "w_out", "ffn_norm", "w_gate", "w_up", "w_down")


def _rope_tables(seq):
    pos = jnp.arange(seq, dtype=F32)
    inv_freq = jnp.power(10000.0, -jnp.arange(0, HEAD_DIM, 2, dtype=F32) / HEAD_DIM)
    ang = pos[:, None] * inv_freq[None, :]
    cos, sin = jnp.cos(ang), jnp.sin(ang)
    reps = BLOCK // HEAD_DIM
    return (jnp.tile(jnp.concatenate([cos, cos], axis=1), (1, reps)),
            jnp.tile(jnp.concatenate([-sin, sin], axis=1), (1, reps)))


def kernel(x, mix_norm, w_in, q_norm, k_norm, sinks, sgu_ln_g, sgu_ln_b, w_spatial, b_spatial, w_attn_branch, w_sgu_branch, w_out, ffn_norm, w_gate, w_up, w_down, loss_target, m_mix_norm, m_w_in, m_q_norm, m_k_norm, m_sinks, m_sgu_ln_g, m_sgu_ln_b, m_w_spatial, m_b_spatial, m_w_attn_branch, m_w_sgu_branch, m_w_out, m_ffn_norm, m_w_gate, m_w_up, m_w_down, v_mix_norm, v_w_in, v_q_norm, v_k_norm, v_sinks, v_sgu_ln_g, v_sgu_ln_b, v_w_spatial, v_b_spatial, v_w_attn_branch, v_w_sgu_branch, v_w_out, v_ffn_norm, v_w_gate, v_w_up, v_w_down):
    weights = dict(mix_norm=mix_norm, w_in=w_in, q_norm=q_norm, k_norm=k_norm, sinks=sinks, sgu_ln_g=sgu_ln_g,
                   sgu_ln_b=sgu_ln_b, w_spatial=w_spatial, b_spatial=b_spatial, w_attn_branch=w_attn_branch,
                   w_sgu_branch=w_sgu_branch, w_out=w_out, ffn_norm=ffn_norm, w_gate=w_gate, w_up=w_up, w_down=w_down)
    mom1 = dict(mix_norm=m_mix_norm, w_in=m_w_in, q_norm=m_q_norm, k_norm=m_k_norm, sinks=m_sinks,
                sgu_ln_g=m_sgu_ln_g, sgu_ln_b=m_sgu_ln_b, w_spatial=m_w_spatial, b_spatial=m_b_spatial,
                w_attn_branch=m_w_attn_branch, w_sgu_branch=m_w_sgu_branch, w_out=m_w_out, ffn_norm=m_ffn_norm,
                w_gate=m_w_gate, w_up=m_w_up, w_down=m_w_down)
    mom2 = dict(mix_norm=v_mix_norm, w_in=v_w_in, q_norm=v_q_norm, k_norm=v_k_norm, sinks=v_sinks,
                sgu_ln_g=v_sgu_ln_g, sgu_ln_b=v_sgu_ln_b, w_spatial=v_w_spatial, b_spatial=v_b_spatial,
                w_attn_branch=v_w_attn_branch, w_sgu_branch=v_w_sgu_branch, w_out=v_w_out, ffn_norm=v_ffn_norm,
                w_gate=v_w_gate, w_up=v_w_up, w_down=v_w_down)
    xs, target = x[0], loss_target[0]
    S, D = xs.shape
    L = w_in.shape[0]
    AW, KW, SW = N_Q_HEADS * HEAD_DIM, N_KV_HEADS * HEAD_DIM, SGU_GROUPS * BLOCK
    dims = (AW, KW, AW + 2 * KW + 2 * SW, (AW + 2 * KW) // BLOCK)
    cos, sin = _rope_tables(S)
    reps = BLOCK // HEAD_DIM

    chip = (2 * lax.axis_index("x") + lax.axis_index("y")).astype(jnp.int32).reshape(1)
    core = lax.axis_index("c").astype(jnp.int32).reshape(1)
    started, token = {}, chip
    for l in range(L):
        for gi, names in enumerate(GROUPS):
            bufs = [_cast_own("cast_%s_%d" % (n, l), chip, weights[n], l) for n in names]
            bufs, send, recv, token = _gather_start("gather_start_%d_%d" % (l, gi), bufs, token)
            started[(l, gi)] = (bufs, send, recv)
    stream = _WeightStream(started)
    stream.forward(0, 0, token)
    w_first = stream.finish(0, 0, token)
    sp = [dict(mix_norm=mix_norm[l][None], ffn_norm=ffn_norm[l][None], q_norm=jnp.tile(q_norm[l][None], (1, reps)),
               k_norm=jnp.tile(k_norm[l][None], (1, reps)), sinks=sinks[l][None], sgu_ln_g=sgu_ln_g[l][None],
               sgu_ln_b=sgu_ln_b[l][None], w_spatial=w_spatial[l], b_spatial=b_spatial[l][:, :, None])
          for l in range(L)]

    act, saved, wl = xs, [], []
    for l in range(L):
        act, sv, w_all, w_first = _layer_fwd(act, w_first, stream, l, l == L - 1, sp[l], cos, sin, dims)
        saved.append(sv)
        wl.append(w_all)
    loss_part, dy, dyb = _loss_head(act, target)
    loss = lax.psum(loss_part[0, 0], ("x", "y", "c"))

    reducer = _GradReducer(chip, core)
    small_g = [None] * L
    for l in reversed(range(L)):
        dy, dyb, small_g[l] = _layer_bwd(dy, dyb, wl[l], sp[l], saved[l], cos, sin, dims,
                                         functools.partial(reducer.start, l))
    grad_x = dy[None]

    reduced = reducer.finish(dy)
    grads, deltas, new_m, new_v = {}, {}, {}, {}
    for n in BIG:
        per_layer = [reduced[(n, l)] for l in range(L)]
        grads[n], deltas[n], new_m[n], new_v[n] = _adamw_big("adamw_" + n, per_layer, weights[n], mom1[n], mom2[n])

    small_like = [weights[n] for n in SMALL]
    local = [jnp.stack([small_g[l][n].reshape(weights[n].shape[1:]) for l in range(L)]) for n in SMALL]
    g_small = _all_reduce_small(_pack(local))
    d_small, m_small, v_small = _adamw_small(g_small, _pack(small_like), _pack([mom1[n] for n in SMALL]),
                                             _pack([mom2[n] for n in SMALL]))
    for n, g, d, m2, v2 in zip(SMALL, _unpack(g_small, small_like), _unpack(d_small, small_like),
                               _unpack(m_small, small_like), _unpack(v_small, small_like)):
        grads[n], deltas[n], new_m[n], new_v[n] = g, d, m2, v2

    return (loss, grad_x, *[grads[n] for n in ORDER], *[deltas[n] for n in ORDER],
            *[new_m[n] for n in ORDER], *[new_v[n] for n in ORDER])
```

```python
import functools

import jax
import jax.numpy as jnp
from jax import lax
from jax.experimental import pallas as pl
from jax.experimental.pallas import tpu as pltpu

HEAD_DIM = 64
N_Q_HEADS = 16
N_KV_HEADS = 4
SGU_GROUPS = 8
BLOCK = 128
EPS = 1e-6
ADAM_LR = 0.001
ADAM_B1 = 0.9
ADAM_B2 = 0.999
ADAM_EPS = 1e-08
ADAM_WD = 0.01
ADAM_STEP = 10
N_CHIPS = 4
VMEM_LIMIT = 52 * 1024 * 1024

F32 = jnp.float32
MXU = jnp.bfloat16
NN = (((1,), (0,)), ((), ()))
NT = (((1,), (1,)), ((), ()))
TN = (((0,), (0,)), ((), ()))
MESH = pl.DeviceIdType.MESH
ANY = pl.BlockSpec(memory_space=pl.ANY)


def _tile(n, pref):
    if n <= pref:
        return n
    best = None
    for t in range(BLOCK, pref + 1, BLOCK):
        if n % t == 0:
            best = t
    assert best is not None, (n, pref)
    return best


def _params():
    return pltpu.CompilerParams(vmem_limit_bytes=VMEM_LIMIT)


def _mm(name, grid, n_red, operands, specs, pairs, dims, n_extra, out_shapes, out_specs,
        acc_shapes, epilogue, after=None):
    n_op = len(operands) - n_extra
    n_out = len(out_shapes)
    n_acc = len(acc_shapes)
    if after is not None:
        operands, specs = list(operands) + [after], list(specs) + [ANY]
    n_in = len(operands)
    axes = [ax for ax in range(len(grid) - n_red, len(grid)) if grid[ax] > 1]

    def body(*refs):
        ops = refs[:n_op]
        extra = refs[n_op:n_op + n_extra]
        outs = refs[n_in:n_in + n_out]
        accs = refs[n_in + n_out:]

        def prod(a, b):
            return lax.dot_general(ops[a][...], ops[b][...], dims, preferred_element_type=F32)

        vals = [None] * n_acc
        for a, b, k in pairs:
            d = prod(a, b)
            vals[k] = d if vals[k] is None else vals[k] + d
        if not axes:
            epilogue(vals, extra, outs)
        else:
            first = pl.program_id(axes[0]) == 0
            last = pl.program_id(axes[0]) == grid[axes[0]] - 1
            for ax in axes[1:]:
                first = jnp.logical_and(first, pl.program_id(ax) == 0)
                last = jnp.logical_and(last, pl.program_id(ax) == grid[ax] - 1)

            @pl.when(first)
            def _():
                for acc, v in zip(accs, vals):
                    acc[...] = v

            @pl.when(jnp.logical_not(jnp.logical_or(first, last)))
            def _():
                for acc, v in zip(accs, vals):
                    acc[...] += v

            @pl.when(last)
            def _():
                epilogue([acc[...] + v for acc, v in zip(accs, vals)], extra, outs)

    scratch = [pltpu.VMEM(s, F32) for s in acc_shapes] if axes else []
    return pl.pallas_call(
        body, name=name, grid=grid, in_specs=specs, out_specs=out_specs, out_shape=out_shapes,
        scratch_shapes=scratch, compiler_params=_params())(*operands)


def _sigmoid(x):
    return 1.0 / (1.0 + jnp.exp(-x))


_GELU_C = 0.7978845608028654
_GELU_A = 0.044715


def _gelu(x):
    return 0.5 * x * (1.0 + jnp.tanh(_GELU_C * (x + _GELU_A * x * x * x)))


def _gelu_grad(x):
    t = jnp.tanh(_GELU_C * (x + _GELU_A * x * x * x))
    return 0.5 * (1.0 + t) + 0.5 * x * (1.0 - t * t) * _GELU_C * (1.0 + 3.0 * _GELU_A * x * x)


def _rms_fwd(name, x, g):
    S, D = x.shape
    tr = _tile(S, 256)

    def body(x_ref, g_ref, o_ref):
        xv = x_ref[...]
        r = lax.rsqrt(jnp.mean(xv * xv, axis=-1, keepdims=True) + EPS)
        o_ref[...] = (xv * r * g_ref[...]).astype(MXU)

    return pl.pallas_call(
        body, name=name, grid=(S // tr,),
        in_specs=[pl.BlockSpec((tr, D), lambda i: (i, 0)), pl.BlockSpec((1, D), lambda i: (0, 0))],
        out_specs=pl.BlockSpec((tr, D), lambda i: (i, 0)),
        out_shape=jax.ShapeDtypeStruct((S, D), MXU), compiler_params=_params())(x, g)


def _rms_bwd(name, dh, x, g, dres, after):
    S, D = x.shape
    tr = _tile(S, 256)

    def body(dh_ref, x_ref, g_ref, dres_ref, after_ref, dx_ref, dxb_ref, dg_ref):
        xv = x_ref[...]
        r = lax.rsqrt(jnp.mean(xv * xv, axis=-1, keepdims=True) + EPS)
        xh = xv * r
        dhv = dh_ref[...]
        dy = dhv * g_ref[...]
        dx = dres_ref[...] + r * (dy - xh * jnp.mean(dy * xh, axis=-1, keepdims=True))
        dx_ref[...] = dx
        dxb_ref[...] = dx.astype(MXU)

        @pl.when(pl.program_id(0) == 0)
        def _():
            dg_ref[...] = jnp.zeros(dg_ref.shape, F32)

        dg_ref[...] += jnp.sum(dhv * xh, axis=0, keepdims=True)

    row = pl.BlockSpec((tr, D), lambda i: (i, 0))
    vec = pl.BlockSpec((1, D), lambda i: (0, 0))
    return pl.pallas_call(
        body, name=name, grid=(S // tr,), in_specs=[row, row, vec, row, ANY], out_specs=[row, row, vec],
        out_shape=[jax.ShapeDtypeStruct((S, D), F32), jax.ShapeDtypeStruct((S, D), MXU),
                   jax.ShapeDtypeStruct((1, D), F32)],
        compiler_params=_params())(dh, x, g, dres, after)


def _loss_head(y, target):
    S, D = y.shape
    tr = _tile(S, 256)

    def body(y_ref, t_ref, loss_ref, dy_ref, dyb_ref):
        d = y_ref[...] - t_ref[...]
        dy = d * (1.0 / D)
        dy_ref[...] = dy
        dyb_ref[...] = dy.astype(MXU)

        @pl.when(pl.program_id(0) == 0)
        def _():
            loss_ref[...] = jnp.zeros(loss_ref.shape, F32)

        loss_ref[...] += (0.5 / D) * jnp.sum(jnp.sum(d * d, axis=-1, keepdims=True), axis=0, keepdims=True)

    row = pl.BlockSpec((tr, D), lambda i: (i, 0))
    return pl.pallas_call(
        body, name="loss_head", grid=(S // tr,), in_specs=[row, row],
        out_specs=[pl.BlockSpec((1, 1), lambda i: (0, 0)), row, row],
        out_shape=[jax.ShapeDtypeStruct((1, 1), F32), jax.ShapeDtypeStruct((S, D), F32),
                   jax.ShapeDtypeStruct((S, D), MXU)],
        compiler_params=_params())(y, target)


def _head_sum(v):
    r = lax.broadcasted_iota(jnp.int32, (BLOCK, BLOCK), 0) // HEAD_DIM
    c = lax.broadcasted_iota(jnp.int32, (BLOCK, BLOCK), 1) // HEAD_DIM
    ones = jnp.where(r == c, 1.0, 0.0).astype(jnp.bfloat16)
    hi = v.astype(jnp.bfloat16)
    lo = (v - hi.astype(F32)).astype(jnp.bfloat16)
    parts = []
    for t in range(v.shape[1] // BLOCK):
        sl = slice(t * BLOCK, (t + 1) * BLOCK)
        parts.append(jnp.dot(hi[:, sl], ones, preferred_element_type=F32)
                     + jnp.dot(lo[:, sl], ones, preferred_element_type=F32))
    return parts[0] if len(parts) == 1 else jnp.concatenate(parts, axis=-1)


def _swap_halves(v):
    w = v.shape[1]
    half = HEAD_DIM // 2
    lane = lax.broadcasted_iota(jnp.int32, v.shape, 1) % HEAD_DIM
    return jnp.where(lane < half, pltpu.roll(v, w - half, 1), pltpu.roll(v, half, 1))


def _norm_rope(xv, gain, cos, sin):
    r = lax.rsqrt(_head_sum(xv * xv) * (1.0 / HEAD_DIM) + EPS)
    xn = xv * r * gain
    return xn * cos + _swap_halves(xn) * sin


def _norm_rope_bwd(dy, xv, gain, cos, sin):
    r = lax.rsqrt(_head_sum(xv * xv) * (1.0 / HEAD_DIM) + EPS)
    xh = xv * r
    dxn = dy * cos + _swap_halves(dy * sin)
    dgain = jnp.sum(dxn * xh, axis=0, keepdims=True)
    dxh = dxn * gain
    dx = r * (dxh - xh * (_head_sum(dxh * xh) * (1.0 / HEAD_DIM)))
    return dx, dgain


def _fold_heads(v):
    acc = v[:, 0:BLOCK]
    for t in range(1, v.shape[1] // BLOCK):
        acc = acc + v[:, t * BLOCK:(t + 1) * BLOCK]
    return acc + pltpu.roll(acc, HEAD_DIM, 1)


def _tile_lanes(v, width):
    return v if width == BLOCK else jnp.tile(v, (1, width // BLOCK))


def _qk_prep(proj, qg, kg, cos, sin, AW, KW):
    S = proj.shape[0]
    tr = _tile(S, 256)
    scale = HEAD_DIM ** -0.5

    def body(q_ref, k_ref, v_ref, qg_ref, kg_ref, cos_ref, sin_ref, qo_ref, ko_ref, vo_ref):
        c, s = cos_ref[...], sin_ref[...]
        q = _norm_rope(q_ref[...], _tile_lanes(qg_ref[...], AW), _tile_lanes(c, AW), _tile_lanes(s, AW))
        k = _norm_rope(k_ref[...], _tile_lanes(kg_ref[...], KW), _tile_lanes(c, KW), _tile_lanes(s, KW))
        qo_ref[...] = (q * scale).astype(MXU)
        ko_ref[...] = k.astype(MXU)
        vo_ref[...] = v_ref[...].astype(MXU)

    assert AW % KW == 0
    vec = pl.BlockSpec((1, BLOCK), lambda i: (0, 0))
    tab = pl.BlockSpec((tr, BLOCK), lambda i: (i, 0))
    return pl.pallas_call(
        body, name="qk_prep", grid=(S // tr,),
        in_specs=[pl.BlockSpec((tr, AW), lambda i: (i, 0)),
                  pl.BlockSpec((tr, KW), lambda i: (i, AW // KW)),
                  pl.BlockSpec((tr, KW), lambda i: (i, AW // KW + 1)), vec, vec, tab, tab],
        out_specs=[pl.BlockSpec((tr, AW), lambda i: (i, 0)), pl.BlockSpec((tr, KW), lambda i: (i, 0)),
                   pl.BlockSpec((tr, KW), lambda i: (i, 0))],
        out_shape=[jax.ShapeDtypeStruct((S, AW), MXU), jax.ShapeDtypeStruct((S, KW), MXU),
                   jax.ShapeDtypeStruct((S, KW), MXU)],
        compiler_params=_params())(proj, proj, proj, qg, kg, cos, sin)


def _attn_probs(n, q, kp, kc, g, sink_ref, qpk):
    hd = HEAD_DIM
    kcat = jnp.concatenate([kp[:, g * hd:(g + 1) * hd], kc[:, g * hd:(g + 1) * hd]], axis=0)
    qs = jnp.concatenate([q[:, (g * qpk + j) * hd:(g * qpk + j + 1) * hd] for j in range(qpk)], axis=0)
    s = lax.dot_general(qs, kcat, NT, preferred_element_type=F32)
    row = lax.broadcasted_iota(jnp.int32, (BLOCK, 2 * BLOCK), 0)
    col = lax.broadcasted_iota(jnp.int32, (BLOCK, 2 * BLOCK), 1)
    ok = (col > row) & (col <= row + BLOCK) & ((col >= BLOCK) | (n > 0))
    s = jnp.where(jnp.concatenate([ok] * qpk, axis=0), s, -1e30)
    sk = jnp.concatenate([jnp.full((BLOCK, 1), sink_ref[0, g * qpk + j], F32) for j in range(qpk)], axis=0)
    m = jnp.maximum(jnp.max(s, axis=-1, keepdims=True), sk)
    e = jnp.exp(s - m)
    es = jnp.exp(sk - m)
    z = jnp.sum(e, axis=-1, keepdims=True) + es
    return e / z, es / z, qs, kcat


def _attn_fwd(qr, kr, vb, sinks):
    S, AW = qr.shape
    KW = kr.shape[1]
    nb = S // BLOCK
    nkv = KW // HEAD_DIM
    qpk = AW // KW
    hd = HEAD_DIM

    def body(sink_ref, q_ref, kp_ref, kc_ref, vp_ref, vc_ref, o_ref):
        n = pl.program_id(0)
        q, kp, kc, vp, vc = q_ref[...], kp_ref[...], kc_ref[...], vp_ref[...], vc_ref[...]
        outs = [None] * (nkv * qpk)
        for g in range(nkv):
            p, _, _, _ = _attn_probs(n, q, kp, kc, g, sink_ref, qpk)
            vcat = jnp.concatenate([vp[:, g * hd:(g + 1) * hd], vc[:, g * hd:(g + 1) * hd]], axis=0)
            o = jnp.dot(p.astype(MXU), vcat, preferred_element_type=F32)
            for j in range(qpk):
                outs[g * qpk + j] = o[j * BLOCK:(j + 1) * BLOCK]
        o_ref[...] = jnp.concatenate(outs, axis=-1).astype(MXU)

    cur = lambda n: (n, 0)
    prev = lambda n: (jnp.maximum(n - 1, 0), 0)
    return pl.pallas_call(
        body, name="attn_fwd", grid=(nb,),
        in_specs=[pl.BlockSpec(memory_space=pltpu.SMEM), pl.BlockSpec((BLOCK, AW), cur),
                  pl.BlockSpec((BLOCK, KW), prev), pl.BlockSpec((BLOCK, KW), cur),
                  pl.BlockSpec((BLOCK, KW), prev), pl.BlockSpec((BLOCK, KW), cur)],
        out_specs=pl.BlockSpec((BLOCK, AW), cur),
        out_shape=jax.ShapeDtypeStruct((S, AW), MXU), compiler_params=_params())(sinks, qr, kr, kr, vb, vb)


def _attn_bwd(qr, kr, vb, sinks, dattn):
    S, AW = qr.shape
    KW = kr.shape[1]
    nb = S // BLOCK
    nkv = KW // HEAD_DIM
    qpk = AW // KW
    hd = HEAD_DIM
    scale = HEAD_DIM ** -0.5

    def body(sink_ref, q_ref, kp_ref, kc_ref, vp_ref, vc_ref, do_ref,
             dq_ref, dkp_ref, dkc_ref, dvp_ref, dvc_ref, dsink_ref):
        n = pl.program_id(0)
        q, kp, kc, vp, vc = q_ref[...], kp_ref[...], kc_ref[...], vp_ref[...], vc_ref[...]
        do = do_ref[...].astype(MXU)
        lane = lax.broadcasted_iota(jnp.int32, (1, BLOCK), 1)
        dsink = jnp.zeros((1, BLOCK), F32)
        dqs = [None] * (nkv * qpk)
        dkps, dkcs, dvps, dvcs = [], [], [], []
        for g in range(nkv):
            p, psink, qs, kcat = _attn_probs(n, q, kp, kc, g, sink_ref, qpk)
            vcat = jnp.concatenate([vp[:, g * hd:(g + 1) * hd], vc[:, g * hd:(g + 1) * hd]], axis=0)
            dos = jnp.concatenate([do[:, (g * qpk + j) * hd:(g * qpk + j + 1) * hd] for j in range(qpk)], axis=0)
            dp = lax.dot_general(dos, vcat, NT, preferred_element_type=F32)
            dv = lax.dot_general(p.astype(MXU), dos, TN, preferred_element_type=F32)
            delta = jnp.sum(p * dp, axis=-1, keepdims=True)
            ds = (p * (dp - delta)).astype(MXU)
            dsk = -psink * delta
            dq = jnp.dot(ds, kcat, preferred_element_type=F32) * scale
            dk = lax.dot_general(ds, qs, TN, preferred_element_type=F32)
            for j in range(qpk):
                dqs[g * qpk + j] = dq[j * BLOCK:(j + 1) * BLOCK]
                tot = jnp.sum(dsk[j * BLOCK:(j + 1) * BLOCK], axis=0, keepdims=True)
                dsink = dsink + jnp.where(lane == g * qpk + j, tot, 0.0)
            dkps.append(dk[:BLOCK])
            dkcs.append(dk[BLOCK:])
            dvps.append(dv[:BLOCK])
            dvcs.append(dv[BLOCK:])
        dq_ref[...] = jnp.concatenate(dqs, axis=-1)
        dkp_ref[...] = jnp.concatenate(dkps, axis=-1)
        dkc_ref[...] = jnp.concatenate(dkcs, axis=-1)
        dvp_ref[...] = jnp.concatenate(dvps, axis=-1)
        dvc_ref[...] = jnp.concatenate(dvcs, axis=-1)

        @pl.when(n == 0)
        def _():
            dsink_ref[...] = jnp.zeros(dsink_ref.shape, F32)

        dsink_ref[...] += dsink

    cur = lambda n: (n, 0)
    prev = lambda n: (jnp.maximum(n - 1, 0), 0)
    kv = jax.ShapeDtypeStruct((S, KW), F32)
    kvspec = pl.BlockSpec((BLOCK, KW), cur)
    return pl.pallas_call(
        body, name="attn_bwd", grid=(nb,),
        in_specs=[pl.BlockSpec(memory_space=pltpu.SMEM), pl.BlockSpec((BLOCK, AW), cur),
                  pl.BlockSpec((BLOCK, KW), prev), kvspec, pl.BlockSpec((BLOCK, KW), prev), kvspec,
                  pl.BlockSpec((BLOCK, AW), cur)],
        out_specs=[pl.BlockSpec((BLOCK, AW), cur), kvspec, kvspec, kvspec, kvspec,
                   pl.BlockSpec((1, BLOCK), lambda n: (0, 0))],
        out_shape=[jax.ShapeDtypeStruct((S, AW), F32), kv, kv, kv, kv, jax.ShapeDtypeStruct((1, BLOCK), F32)],
        compiler_params=_params())(sinks, qr, kr, kr, vb, vb, dattn)


def _qk_prep_bwd(proj, qg, kg, cos, sin, dq, dkp, dkc, dvp, dvc, AW, KW):
    S = proj.shape[0]
    nb = S // BLOCK

    def body(q_ref, k_ref, qg_ref, kg_ref, cos_ref, sin_ref, dq_ref, dkp_ref, dkc_ref, dvp_ref, dvc_ref,
             o_ref, dqg_ref, dkg_ref):
        n = pl.program_id(0)
        c, s = cos_ref[...], sin_ref[...]
        has_next = jnp.where(n < nb - 1, 1.0, 0.0)
        dk = dkc_ref[...] + has_next * dkp_ref[...]
        dv = dvc_ref[...] + has_next * dvp_ref[...]
        dxq, dqg = _norm_rope_bwd(dq_ref[...], q_ref[...], _tile_lanes(qg_ref[...], AW),
                                  _tile_lanes(c, AW), _tile_lanes(s, AW))
        dxk, dkg = _norm_rope_bwd(dk, k_ref[...], _tile_lanes(kg_ref[...], KW),
                                  _tile_lanes(c, KW), _tile_lanes(s, KW))
        o_ref[...] = jnp.concatenate([dxq, dxk, dv], axis=-1).astype(MXU)

        @pl.when(n == 0)
        def _():
            dqg_ref[...] = jnp.zeros(dqg_ref.shape, F32)
            dkg_ref[...] = jnp.zeros(dkg_ref.shape, F32)

        dqg_ref[...] += _fold_heads(dqg)
        dkg_ref[...] += _fold_heads(dkg)

    cur = lambda n: (n, 0)
    nxt = lambda n: (jnp.minimum(n + 1, nb - 1), 0)
    vec = pl.BlockSpec((1, BLOCK), lambda n: (0, 0))
    tab = pl.BlockSpec((BLOCK, BLOCK), cur)
    return pl.pallas_call(
        body, name="qk_prep_bwd", grid=(nb,),
        in_specs=[pl.BlockSpec((BLOCK, AW), cur), pl.BlockSpec((BLOCK, KW), lambda n: (n, AW // KW)),
                  vec, vec, tab, tab, pl.BlockSpec((BLOCK, AW), cur),
                  pl.BlockSpec((BLOCK, KW), nxt), pl.BlockSpec((BLOCK, KW), cur),
                  pl.BlockSpec((BLOCK, KW), nxt), pl.BlockSpec((BLOCK, KW), cur)],
        out_specs=[pl.BlockSpec((BLOCK, AW + 2 * KW), cur), vec, vec],
        out_shape=[jax.ShapeDtypeStruct((S, AW + 2 * KW), MXU), jax.ShapeDtypeStruct((1, BLOCK), F32),
                   jax.ShapeDtypeStruct((1, BLOCK), F32)],
        compiler_params=_params())(proj, proj, qg, kg, cos, sin, dq, dkp, dkc, dvp, dvc)


SGU_LANES = 512
SGU_ROWS = 256


def _sgu_group(v, lng, lnb, w_f32, b):
    rows = v.shape[0]
    mu = jnp.mean(v, axis=-1, keepdims=True)
    vc = v - mu
    r = lax.rsqrt(jnp.mean(vc * vc, axis=-1, keepdims=True) + EPS)
    xh = vc * r
    vn = (xh * lng + lnb).astype(MXU)
    row = lax.broadcasted_iota(jnp.int32, (BLOCK, BLOCK), 0)
    col = lax.broadcasted_iota(jnp.int32, (BLOCK, BLOCK), 1)
    tri = row >= col
    w = jnp.where(tri, w_f32, 0.0).astype(MXU)
    chunks = [jnp.dot(w, vn[k * BLOCK:(k + 1) * BLOCK], preferred_element_type=F32) + b for k in range(rows // BLOCK)]
    s = chunks[0] if len(chunks) == 1 else jnp.concatenate(chunks, axis=0)
    return xh, r, vn, w, s, tri


def _sgu_layout(S, u_col):
    SW = SGU_GROUPS * BLOCK
    lb, tr = min(SGU_LANES, SW), min(SGU_ROWS, S)
    assert u_col % lb == 0 and SW % lb == 0 and S % tr == 0
    ub, nlb, gpb = u_col // lb, SW // lb, lb // BLOCK
    specs = [pl.BlockSpec((tr, lb), lambda j, i: (i, ub + j)), pl.BlockSpec((tr, lb), lambda j, i: (i, ub + nlb + j)),
             pl.BlockSpec((1, lb), lambda j, i: (0, j)), pl.BlockSpec((1, lb), lambda j, i: (0, j)),
             pl.BlockSpec((gpb, BLOCK, BLOCK), lambda j, i: (j, 0, 0)),
             pl.BlockSpec((gpb, BLOCK, 1), lambda j, i: (j, 0, 0))]
    return lb, tr, gpb, nlb, specs


def _sgu_fwd(proj, lng, lnb, ws, bs, u_col):
    S = proj.shape[0]
    lb, tr, gpb, nlb, specs = _sgu_layout(S, u_col)

    def body(pu_ref, pv_ref, lng_ref, lnb_ref, w_ref, b_ref, o_ref):
        u = _gelu(pu_ref[...])
        v = _gelu(pv_ref[...])
        outs = []
        for g in range(gpb):
            sl = slice(g * BLOCK, (g + 1) * BLOCK)
            s = _sgu_group(v[:, sl], lng_ref[:, sl], lnb_ref[:, sl], w_ref[g], b_ref[g])[4]
            outs.append(u[:, sl] * s)
        o_ref[...] = (outs[0] if gpb == 1 else jnp.concatenate(outs, axis=-1)).astype(MXU)

    return pl.pallas_call(
        body, name="sgu_fwd", grid=(nlb, S // tr), in_specs=specs,
        out_specs=pl.BlockSpec((tr, lb), lambda j, i: (i, j)),
        out_shape=jax.ShapeDtypeStruct((S, nlb * lb), MXU), compiler_params=_params())(proj, proj, lng, lnb, ws, bs)


def _sgu_bwd(proj, lng, lnb, ws, bs, dsgu, u_col, after):
    S = proj.shape[0]
    G = SGU_GROUPS
    lb, tr, gpb, nlb, specs = _sgu_layout(S, u_col)
    nch = tr // BLOCK

    def body(pu_ref, pv_ref, lng_ref, lnb_ref, w_ref, b_ref, do_ref, after_ref,
             dpu_ref, dpv_ref, dw_ref, db_ref, dlng_ref, dlnb_ref):
        pu, pv, do = pu_ref[...], pv_ref[...], do_ref[...]
        u = _gelu(pu)
        v = _gelu(pv)

        @pl.when(pl.program_id(1) == 0)
        def _():
            dw_ref[...] = jnp.zeros(dw_ref.shape, F32)
            db_ref[...] = jnp.zeros(db_ref.shape, F32)
            dlng_ref[...] = jnp.zeros(dlng_ref.shape, F32)
            dlnb_ref[...] = jnp.zeros(dlnb_ref.shape, F32)

        ss, dvs, dlng, dlnb = [], [], [], []
        for g in range(gpb):
            sl = slice(g * BLOCK, (g + 1) * BLOCK)
            xh, r, vn, w, s, tri = _sgu_group(v[:, sl], lng_ref[:, sl], lnb_ref[:, sl], w_ref[g], b_ref[g])
            ds = do[:, sl] * u[:, sl]
            dsb = ds.astype(MXU)
            dw, db, dvn = None, None, []
            for k in range(nch):
                rows = slice(k * BLOCK, (k + 1) * BLOCK)
                part = lax.dot_general(dsb[rows], vn[rows], NT, preferred_element_type=F32)
                dw = part if dw is None else dw + part
                rowsum = jnp.sum(ds[rows], axis=-1, keepdims=True)
                db = rowsum if db is None else db + rowsum
                dvn.append(lax.dot_general(w, dsb[rows], TN, preferred_element_type=F32))
            dvn = dvn[0] if nch == 1 else jnp.concatenate(dvn, axis=0)
            dw_ref[g] += jnp.where(tri, dw, 0.0)
            db_ref[g] += db
            dxh = dvn * lng_ref[:, sl]
            dvs.append(r * (dxh - jnp.mean(dxh, axis=-1, keepdims=True)
                            - xh * jnp.mean(dxh * xh, axis=-1, keepdims=True)))
            dlng.append(jnp.sum(dvn * xh, axis=0, keepdims=True))
            dlnb.append(jnp.sum(dvn, axis=0, keepdims=True))
            ss.append(s)
        cat = lambda parts: parts[0] if gpb == 1 else jnp.concatenate(parts, axis=-1)
        dpu_ref[...] = (do * cat(ss) * _gelu_grad(pu)).astype(MXU)
        dpv_ref[...] = (cat(dvs) * _gelu_grad(pv)).astype(MXU)
        dlng_ref[...] += cat(dlng)
        dlnb_ref[...] += cat(dlnb)

    tile = pl.BlockSpec((tr, lb), lambda j, i: (i, j))
    vec = pl.BlockSpec((1, lb), lambda j, i: (0, j))
    half = jax.ShapeDtypeStruct((S, G * BLOCK), MXU)
    return pl.pallas_call(
        body, name="sgu_bwd", grid=(nlb, S // tr), in_specs=specs + [tile, ANY],
        out_specs=[tile, tile, pl.BlockSpec((gpb, BLOCK, BLOCK), lambda j, i: (j, 0, 0)),
                   pl.BlockSpec((gpb, BLOCK, 1), lambda j, i: (j, 0, 0)), vec, vec],
        out_shape=[half, half, jax.ShapeDtypeStruct((G, BLOCK, BLOCK), F32),
                   jax.ShapeDtypeStruct((G, BLOCK, 1), F32),
                   jax.ShapeDtypeStruct((1, G * BLOCK), F32), jax.ShapeDtypeStruct((1, G * BLOCK), F32)],
        compiler_params=_params())(proj, proj, lng, lnb, ws, bs, dsgu, after)


def _store_f32(vals, extra, outs):
    for v, o in zip(vals, outs):
        o[...] = v


def _store_mxu(vals, extra, outs):
    for v, o in zip(vals, outs):
        o[...] = v.astype(MXU)


def _proj_in(h, w):
    S, D = h.shape
    Ns = w.shape[2]
    tm, tn = _tile(S, 1024), _tile(Ns, 1024)
    npb = Ns // tn
    return _mm("proj_in", (S // tm, N_CHIPS, npb), 0, [h, w],
               [pl.BlockSpec((tm, D), lambda i, s, j: (i, 0)), pl.BlockSpec((None, D, tn), lambda i, s, j: (s, 0, j))],
               [(0, 1, 0)], NN, 0, [jax.ShapeDtypeStruct((S, N_CHIPS * Ns), F32)],
               [pl.BlockSpec((tm, tn), lambda i, s, j: (i, s * npb + j))], [None], _store_f32)[0]


def _branches(attn, sgu, wa, ws, proj, gate0):
    S, AW = attn.shape
    SW = sgu.shape[1]
    Nb = wa.shape[2]
    D = N_CHIPS * Nb
    tm = _tile(S, 512)
    assert gate0 % Nb == 0
    ga, gb = gate0 // Nb, (gate0 + D) // Nb

    def epilogue(vals, extra, outs):
        a, b = vals
        outs[0][...] = (_sigmoid(extra[0][...]) * a + _sigmoid(extra[1][...]) * b).astype(MXU)
        outs[1][...] = a
        outs[2][...] = b

    tile = pl.BlockSpec((tm, Nb), lambda i, s: (i, s))
    wspec = lambda k: pl.BlockSpec((None, k, Nb), lambda i, s: (s, 0, 0))
    f = jax.ShapeDtypeStruct((S, D), F32)
    return _mm("branches", (S // tm, N_CHIPS), 0, [attn, sgu, wa, ws, proj, proj],
               [pl.BlockSpec((tm, AW), lambda i, s: (i, 0)), pl.BlockSpec((tm, SW), lambda i, s: (i, 0)),
                wspec(AW), wspec(SW), pl.BlockSpec((tm, Nb), lambda i, s: (i, ga + s)),
                pl.BlockSpec((tm, Nb), lambda i, s: (i, gb + s))],
               [(0, 2, 0), (1, 3, 1)], NN, 2, [jax.ShapeDtypeStruct((S, D), MXU), f, f], [tile] * 3,
               [None, None], epilogue)


def _rows_mm(name, a, w, res):
    S = a.shape[0]
    _, K, N = w.shape
    tm, tn = _tile(S, 1024), _tile(N, 1024)

    def epilogue(vals, extra, outs):
        outs[0][...] = extra[0][...] + vals[0]

    out = pl.BlockSpec((tm, tn), lambda i, j, s: (i, j))
    return _mm(name, (S // tm, N // tn, N_CHIPS), 1, [a, w, res],
               [pl.BlockSpec((tm, K), lambda i, j, s: (i, s)), pl.BlockSpec((None, K, tn), lambda i, j, s: (s, 0, j)), out],
               [(0, 1, 0)], NN, 1, [jax.ShapeDtypeStruct((S, N), F32)], [out], [(tm, tn)], epilogue)[0]


def _gate_up(h2, wg, wu):
    S, D = h2.shape
    Nf = wg.shape[2]
    tm, tk = _tile(S, 512), _tile(D, 512)

    def epilogue(vals, extra, outs):
        g, u = vals
        outs[0][...] = g
        outs[1][...] = u
        outs[2][...] = (g * _sigmoid(g) * u).astype(MXU)

    w = pl.BlockSpec((None, tk, Nf), lambda i, s, k: (s, k, 0))
    o = pl.BlockSpec((tm, Nf), lambda i, s, k: (i, s))
    f = jax.ShapeDtypeStruct((S, N_CHIPS * Nf), F32)
    return _mm("gate_up", (S // tm, N_CHIPS, D // tk), 1, [h2, wg, wu],
               [pl.BlockSpec((tm, tk), lambda i, s, k: (i, k)), w, w], [(0, 1, 0), (0, 2, 1)], NN, 0,
               [f, f, jax.ShapeDtypeStruct((S, N_CHIPS * Nf), MXU)], [o, o, o], [(tm, Nf)] * 2, epilogue)


def _down_bwd(dyb, wd, g, u):
    S, D = dyb.shape
    Kf = wd.shape[1]
    tm = _tile(S, 512)

    def epilogue(vals, extra, outs):
        da, gv, uv = vals[0], extra[0][...], extra[1][...]
        sg = _sigmoid(gv)
        outs[0][...] = (da * uv * sg * (1.0 + gv * (1.0 - sg))).astype(MXU)
        outs[1][...] = (da * gv * sg).astype(MXU)

    t = pl.BlockSpec((tm, Kf), lambda i, s: (i, s))
    o = jax.ShapeDtypeStruct((S, N_CHIPS * Kf), MXU)
    return _mm("down_bwd", (S // tm, N_CHIPS), 0, [dyb, wd, g, u],
               [pl.BlockSpec((tm, D), lambda i, s: (i, 0)), pl.BlockSpec((None, Kf, D), lambda i, s: (s, 0, 0)), t, t],
               [(0, 1, 0)], NT, 2, [o, o], [t, t], [None], epilogue)


def _out_bwd(dxb, wo, proj, ba, bb, gate0):
    S, D = dxb.shape
    Ko = wo.shape[1]
    tm = _tile(S, 512)
    assert gate0 % Ko == 0
    ga, gb = gate0 // Ko, (gate0 + D) // Ko

    def epilogue(vals, extra, outs):
        dm = vals[0]
        sa, sb = _sigmoid(extra[0][...]), _sigmoid(extra[1][...])
        outs[0][...] = (dm * sa).astype(MXU)
        outs[1][...] = (dm * sb).astype(MXU)
        outs[2][...] = (dm * extra[2][...] * sa * (1.0 - sa)).astype(MXU)
        outs[3][...] = (dm * extra[3][...] * sb * (1.0 - sb)).astype(MXU)

    t = pl.BlockSpec((tm, Ko), lambda i, s: (i, s))
    o = jax.ShapeDtypeStruct((S, D), MXU)
    return _mm("out_bwd", (S // tm, N_CHIPS), 0, [dxb, wo, proj, proj, ba, bb],
               [pl.BlockSpec((tm, D), lambda i, s: (i, 0)), pl.BlockSpec((None, Ko, D), lambda i, s: (s, 0, 0)),
                pl.BlockSpec((tm, Ko), lambda i, s: (i, ga + s)), pl.BlockSpec((tm, Ko), lambda i, s: (i, gb + s)), t, t],
               [(0, 1, 0)], NT, 4, [o] * 4, [t] * 4, [None], epilogue)


def _dx_cols(name, terms, n_out, after=None):
    S = terms[0][0].shape[0]
    _, K, Ns = terms[0][1].shape
    tm, tko, tn = _tile(S, 1024), _tile(K, 1024), _tile(Ns, 1408)
    npb = Ns // tn
    operands, specs, pairs = [], [], []
    for t, (dy, w, k) in enumerate(terms):
        assert w.shape == (N_CHIPS, K, Ns)
        operands += [dy, w]
        specs += [pl.BlockSpec((tm, tn), lambda i, jk, s, jn: (i, s * npb + jn)),
                  pl.BlockSpec((None, tko, tn), lambda i, jk, s, jn: (s, jk, jn))]
        pairs.append((2 * t, 2 * t + 1, k))
    out = pl.BlockSpec((tm, tko), lambda i, jk, s, jn: (i, jk))
    return _mm(name, (S // tm, K // tko, N_CHIPS, npb), 2, operands, specs, pairs, NT, 0,
               [jax.ShapeDtypeStruct((S, K), F32)] * n_out, [out] * n_out, [(tm, tko)] * n_out, _store_f32, after)


def _dw_cols(name, a, dy):
    S, K = a.shape
    Ns = dy.shape[1] // N_CHIPS
    tk, tn = _tile(K, 512), _tile(Ns, 1408)
    npb = Ns // tn
    return _mm(name, (K // tk, N_CHIPS, npb), 0, [a, dy],
               [pl.BlockSpec((S, tk), lambda jk, s, jn: (0, jk)), pl.BlockSpec((S, tn), lambda jk, s, jn: (0, s * npb + jn))],
               [(0, 1, 0)], TN, 0, [jax.ShapeDtypeStruct((N_CHIPS, K, Ns), MXU)],
               [pl.BlockSpec((None, tk, tn), lambda jk, s, jn: (s, jk, jn))], [None], _store_mxu)[0]


def _dw_rows(name, a, dy):
    S = a.shape[0]
    K = a.shape[1] // N_CHIPS
    N = dy.shape[1]
    tk, tn = _tile(K, 1408), _tile(N, 1024)
    nkb = K // tk
    return _mm(name, (N_CHIPS, nkb, N // tn), 0, [a, dy],
               [pl.BlockSpec((S, tk), lambda s, jk, jn: (0, s * nkb + jk)), pl.BlockSpec((S, tn), lambda s, jk, jn: (0, jn))],
               [(0, 1, 0)], TN, 0, [jax.ShapeDtypeStruct((N_CHIPS, K, N), MXU)],
               [pl.BlockSpec((None, tk, tn), lambda s, jk, jn: (s, jk, jn))], [None], _store_mxu)[0]


def _layer_fwd(x, w, stream, layer, last, sp, cos, sin, dims):
    AW, KW, gate0, u_col = dims
    w = dict(w)
    h = _rms_fwd("mix_norm", x, sp["mix_norm"])
    proj = _proj_in(h, w["w_in"])
    qr, kr, vb = _qk_prep(proj, sp["q_norm"], sp["k_norm"], cos, sin, AW, KW)
    stream.forward(layer, 1, qr)
    attn = _attn_fwd(qr, kr, vb, sp["sinks"])
    w.update(stream.finish(layer, 1, attn))
    sgu = _sgu_fwd(proj, sp["sgu_ln_g"], sp["sgu_ln_b"], sp["w_spatial"], sp["b_spatial"], u_col)
    merged, ba, bb = _branches(attn, sgu, w["w_attn_branch"], w["w_sgu_branch"], proj, gate0)
    stream.forward(layer, 2, merged)
    x1 = _rows_mm("out_proj", merged, w["w_out"], x)
    w.update(stream.finish(layer, 2, x1))
    h2 = _rms_fwd("ffn_norm", x1, sp["ffn_norm"])
    g, u, act = _gate_up(h2, w["w_gate"], w["w_up"])
    if not last:
        stream.forward(layer + 1, 0, g)
    x2 = _rows_mm("down_proj", act, w["w_down"], x1)
    w_next = None if last else stream.finish(layer + 1, 0, x2)
    saved = dict(x=x, h=h, proj=proj, qr=qr, kr=kr, vb=vb, attn=attn, sgu=sgu, merged=merged, ba=ba, bb=bb,
                 x1=x1, h2=h2, g=g, u=u, act=act)
    return x2, saved, w, w_next


def _layer_bwd(dy, dyb, w, sp, sv, cos, sin, dims, reducer, layer):
    AW, KW, gate0, u_col = dims
    big, small = {}, {}
    dg, du = _down_bwd(dyb, w["w_down"], sv["g"], sv["u"])
    big["w_down"] = _dw_rows("dw_down", sv["act"], dyb)
    big["w_gate"] = _dw_cols("dw_gate", sv["h2"], dg)
    big["w_up"] = _dw_cols("dw_up", sv["h2"], du)
    token = reducer.start(layer, 2, big)
    dh2 = _dx_cols("dh2", [(dg, w["w_gate"], 0), (du, w["w_up"], 0)], 1, token)[0]
    token = reducer.scatter(layer, 2, dh2)
    dx1, dx1b, small["ffn_norm"] = _rms_bwd("ffn_norm_bwd", dh2, sv["x1"], sp["ffn_norm"], dy, token)
    dba, dbb, dgla, dglb = _out_bwd(dx1b, w["w_out"], sv["proj"], sv["ba"], sv["bb"], gate0)
    big["w_out"] = _dw_rows("dw_out", sv["merged"], dx1b)
    big["w_attn_branch"] = _dw_cols("dw_attn_branch", sv["attn"], dba)
    big["w_sgu_branch"] = _dw_cols("dw_sgu_branch", sv["sgu"], dbb)
    token = reducer.start(layer, 1, big)
    dattn, dsgu = _dx_cols("dbranch_in", [(dba, w["w_attn_branch"], 0), (dbb, w["w_sgu_branch"], 1)], 2, token)
    token = reducer.scatter(layer, 1, dsgu)
    dpu, dpv, small["w_spatial"], db, small["sgu_ln_g"], small["sgu_ln_b"] = _sgu_bwd(
        sv["proj"], sp["sgu_ln_g"], sp["sgu_ln_b"], sp["w_spatial"], sp["b_spatial"], dsgu, u_col, token)
    small["b_spatial"] = db[:, :, 0]
    dq, dkp, dkc, dvp, dvc, dsink = _attn_bwd(sv["qr"], sv["kr"], sv["vb"], sp["sinks"], dattn)
    small["sinks"] = dsink[:, :sp["sinks"].shape[1]]
    dqkv, dqg, dkg = _qk_prep_bwd(sv["proj"], sp["q_norm"], sp["k_norm"], cos, sin, dq, dkp, dkc, dvp, dvc, AW, KW)
    small["q_norm"] = dqg[:, :HEAD_DIM]
    small["k_norm"] = dkg[:, :HEAD_DIM]
    dproj = jnp.concatenate([dqkv, dpu, dpv, dgla, dglb], axis=1)
    big["w_in"] = _dw_cols("dw_in", sv["h"], dproj)
    token = reducer.start(layer, 0, big)
    dh = _dx_cols("dh", [(dproj, w["w_in"], 0)], 1, token)[0]
    token = reducer.scatter(layer, 0, dh)
    dx, dxb, small["mix_norm"] = _rms_bwd("mix_norm_bwd", dh, sv["x"], sp["mix_norm"], dx1, token)
    return dx, dxb, small


def _place():
    x, y, c = lax.axis_index("x"), lax.axis_index("y"), lax.axis_index("c")
    chips = [(1 - x, y), (x, 1 - y), (1 - x, 1 - y)]
    return x, y, c, chips


def _half_rows(c, rows):
    h = rows // 2
    assert h % 16 == 0
    return pl.ds(pl.multiple_of(c * h, 16), h)


def _row_tile(rows, pref):
    best = None
    for t in range(16, min(rows, pref) + 1, 16):
        if rows % t == 0:
            best = t
    assert best is not None, rows
    return best


def _cast_own(name, chip, w, layer):
    _, R, C = w.shape
    tr = _row_tile(R, 256)

    def body(chip_ref, w_ref, o_ref):
        o_ref[...] = w_ref[...].astype(MXU)

    return pl.pallas_call(
        body, name=name, out_shape=jax.ShapeDtypeStruct((N_CHIPS, R, C), MXU),
        grid_spec=pltpu.PrefetchScalarGridSpec(
            num_scalar_prefetch=1, grid=(R // tr,),
            in_specs=[pl.BlockSpec((None, tr, C), lambda i, chip_ref: (layer, i, 0))],
            out_specs=pl.BlockSpec((None, tr, C), lambda i, chip_ref: (chip_ref[0], i, 0))),
        compiler_params=_params())(chip, w)


HBM = pl.BlockSpec(memory_space=pltpu.HBM)
SEM = pl.BlockSpec(memory_space=pltpu.SEMAPHORE)
DATAFLOW = pltpu.SideEffectType.DATAFLOW_SIDE_EFFECTING


def _gather_copies(bufs, send_sem, recv_sem):
    x, y, c, chips = _place()

    def ici(a, j, block):
        px, py = chips[j]
        blk = bufs[a].at[block, _half_rows(c, bufs[a].shape[1])]
        return pltpu.make_async_remote_copy(
            src_ref=blk, dst_ref=blk, send_sem=send_sem.at[3 * a + j], recv_sem=recv_sem.at[3 * a + j],
            device_id=(px, py, c), device_id_type=MESH)

    def d2d(a, j, core):
        px, py = chips[j]
        blk = bufs[a].at[2 * px + py, _half_rows(core, bufs[a].shape[1])]
        return pltpu.make_async_remote_copy(
            src_ref=blk, dst_ref=blk, send_sem=send_sem.at[3 * a + j], recv_sem=recv_sem.at[3 * a + j],
            device_id=(x, y, 1 - c), device_id_type=MESH)

    return ici, d2d


def _in_hbm(bufs):
    return [pltpu.with_memory_space_constraint(b, pltpu.HBM) for b in bufs]


def _gather_start(name, bufs, after):
    n = len(bufs)

    def body(*refs):
        dst = refs[n + 1:2 * n + 1]
        send_sem, recv_sem, token = refs[2 * n + 1:]
        x, y, c, chips = _place()
        ici, _ = _gather_copies(dst, send_sem, recv_sem)
        for a in range(n):
            for j in range(3):
                ici(a, j, 2 * x + y).start()
        token[...] = jnp.zeros(token.shape, token.dtype)

    sems = pltpu.SemaphoreType.DMA((3 * n,))
    outs = pl.pallas_call(
        body, name=name, in_specs=[HBM] * n + [ANY],
        out_specs=[HBM] * n + [SEM, SEM, pl.BlockSpec(memory_space=pltpu.VMEM)],
        out_shape=[pltpu.HBM(b.shape, b.dtype) for b in bufs] + [sems, sems, jax.ShapeDtypeStruct((8, BLOCK), F32)],
        input_output_aliases={a: a for a in range(n)},
        compiler_params=pltpu.CompilerParams(has_side_effects=DATAFLOW))(*_in_hbm(bufs), after)
    return outs[:n], outs[n], outs[n + 1], outs[n + 2]


def _gather_forward(name, bufs, ici_send, ici_recv, after):
    n = len(bufs)

    def body(*refs):
        ici_send_ref, ici_recv_ref = refs[n], refs[n + 1]
        dst = refs[n + 3:2 * n + 3]
        d2d_send, d2d_recv = refs[2 * n + 3:]
        x, y, c, chips = _place()
        ici, _ = _gather_copies(dst, ici_send_ref, ici_recv_ref)
        _, d2d = _gather_copies(dst, d2d_send, d2d_recv)
        for a in range(n):
            for j, (px, py) in enumerate(chips):
                ici(a, j, 2 * px + py).wait_recv()
                d2d(a, j, c).start()
        for a in range(n):
            for j in range(3):
                ici(a, j, 2 * x + y).wait_send()

    sems = pltpu.SemaphoreType.DMA((3 * n,))
    outs = pl.pallas_call(
        body, name=name, in_specs=[HBM] * n + [SEM, SEM, ANY], out_specs=[HBM] * n + [SEM, SEM],
        out_shape=[pltpu.HBM(b.shape, b.dtype) for b in bufs] + [sems, sems],
        input_output_aliases={a: a for a in range(n)},
        compiler_params=pltpu.CompilerParams(has_side_effects=DATAFLOW))(*bufs, ici_send, ici_recv, after)
    return outs[:n], outs[n], outs[n + 1]


def _gather_finish(name, bufs, d2d_send, d2d_recv, after):
    n = len(bufs)

    def body(*refs):
        send_ref, recv_ref = refs[n], refs[n + 1]
        dst = refs[n + 3:]
        x, y, c, chips = _place()
        _, d2d = _gather_copies(dst, send_ref, recv_ref)
        for a in range(n):
            for j in range(3):
                d2d(a, j, 1 - c).wait_recv()
                d2d(a, j, c).wait_send()

    return pl.pallas_call(
        body, name=name, in_specs=[HBM] * n + [SEM, SEM, ANY], out_specs=[HBM] * n,
        out_shape=[pltpu.HBM(b.shape, b.dtype) for b in bufs],
        input_output_aliases={a: a for a in range(n)},
        compiler_params=pltpu.CompilerParams(has_side_effects=DATAFLOW))(*bufs, d2d_send, d2d_recv, after)


GROUPS = (("w_in",), ("w_attn_branch", "w_sgu_branch", "w_out"), ("w_gate", "w_up", "w_down"))


class _WeightStream:
    def __init__(self, started):
        self.started, self.passed = started, {}

    def forward(self, layer, group, after):
        bufs, send, recv = self.started[(layer, group)]
        self.passed[(layer, group)] = _gather_forward("gather_forward_%d_%d" % (layer, group), bufs, send, recv, after)

    def finish(self, layer, group, after):
        bufs, send, recv = self.passed[(layer, group)]
        done = _gather_finish("gather_finish_%d_%d" % (layer, group), bufs, send, recv, after)
        return dict(zip(GROUPS[group], done))


def _pair_copies(grads, lands, send_sem, recv_sem):
    x, y, c, _ = _place()

    def make(a):
        theirs = _half_rows(1 - c, grads[a].shape[1])
        return pltpu.make_async_remote_copy(
            src_ref=grads[a].at[:, theirs], dst_ref=lands[a], send_sem=send_sem.at[a], recv_sem=recv_sem.at[a],
            device_id=(x, y, 1 - c), device_id_type=MESH)

    return make


def _pair_start(name, grads, after):
    n = len(grads)
    lands = [lax.empty((g.shape[0], g.shape[1] // 2, g.shape[2]), g.dtype) for g in grads]

    def body(*refs):
        src, dst = refs[2 * n + 1:3 * n + 1], refs[3 * n + 1:4 * n + 1]
        send_sem, recv_sem, token = refs[4 * n + 1:]
        copy = _pair_copies(src, dst, send_sem, recv_sem)
        for a in range(n):
            copy(a).start()
        token[...] = jnp.zeros(token.shape, token.dtype)

    sems = pltpu.SemaphoreType.DMA((n,))
    outs = pl.pallas_call(
        body, name=name, in_specs=[HBM] * (2 * n) + [ANY],
        out_specs=[HBM] * (2 * n) + [SEM, SEM, pl.BlockSpec(memory_space=pltpu.VMEM)],
        out_shape=[pltpu.HBM(b.shape, b.dtype) for b in grads + lands] + [sems, sems, jax.ShapeDtypeStruct((8, BLOCK), F32)],
        input_output_aliases={a: a for a in range(2 * n)},
        compiler_params=pltpu.CompilerParams(has_side_effects=DATAFLOW))(*_in_hbm(grads + lands), after)
    return outs[:n], outs[n:2 * n], outs[2 * n], outs[2 * n + 1], outs[2 * n + 2]


def _pair_finish(name, grads, lands, send_sem, recv_sem, after):
    n = len(grads)

    def body(*refs):
        send_ref, recv_ref = refs[2 * n], refs[2 * n + 1]
        src, dst = refs[2 * n + 3:3 * n + 3], refs[3 * n + 3:]
        copy = _pair_copies(src, dst, send_ref, recv_ref)
        for a in range(n):
            copy(a).wait_send()
            copy(a).wait_recv()

    outs = pl.pallas_call(
        body, name=name, in_specs=[HBM] * (2 * n) + [SEM, SEM, ANY], out_specs=[HBM] * (2 * n),
        out_shape=[pltpu.HBM(b.shape, b.dtype) for b in grads + lands],
        input_output_aliases={a: a for a in range(2 * n)},
        compiler_params=pltpu.CompilerParams(has_side_effects=DATAFLOW))(*grads, *lands, send_sem, recv_sem, after)
    return outs[:n], outs[n:]


def _pair_sum(name, core, g, p):
    _, h, C = p.shape
    tr = _row_tile(h, 256)
    nrb = h // tr

    def body(core_ref, g_ref, p_ref, o_ref):
        o_ref[...] = (g_ref[...].astype(F32) + p_ref[...].astype(F32)).astype(o_ref.dtype)

    spec = pl.BlockSpec((None, tr, C), lambda s, i, core_ref: (s, i, 0))
    return pl.pallas_call(
        body, name=name, out_shape=jax.ShapeDtypeStruct(p.shape, p.dtype),
        grid_spec=pltpu.PrefetchScalarGridSpec(
            num_scalar_prefetch=1, grid=(N_CHIPS, nrb),
            in_specs=[pl.BlockSpec((None, tr, C), lambda s, i, core_ref: (s, core_ref[0] * nrb + i, 0)), spec],
            out_specs=spec),
        compiler_params=_params())(core, g, p)


def _scatter_copies(sums, slots, send_sem, recv_sem):
    x, y, c, chips = _place()

    def make(a, j):
        px, py = chips[j]
        return pltpu.make_async_remote_copy(
            src_ref=sums[a].at[2 * px + py], dst_ref=slots[a].at[j], send_sem=send_sem.at[3 * a + j],
            recv_sem=recv_sem.at[3 * a + j], device_id=(px, py, c), device_id_type=MESH)

    return make


def _scatter_start(name, sums, after):
    n = len(sums)
    slots = [lax.empty((3,) + s.shape[1:], s.dtype) for s in sums]

    def body(*refs):
        src, dst = refs[2 * n + 1:3 * n + 1], refs[3 * n + 1:4 * n + 1]
        send_sem, recv_sem, token = refs[4 * n + 1:]
        copy = _scatter_copies(src, dst, send_sem, recv_sem)
        for a in range(n):
            for j in range(3):
                copy(a, j).start()
        token[...] = jnp.zeros(token.shape, token.dtype)

    sems = pltpu.SemaphoreType.DMA((3 * n,))
    outs = pl.pallas_call(
        body, name=name, in_specs=[HBM] * (2 * n) + [ANY],
        out_specs=[HBM] * (2 * n) + [SEM, SEM, pl.BlockSpec(memory_space=pltpu.VMEM)],
        out_shape=[pltpu.HBM(b.shape, b.dtype) for b in sums + slots] + [sems, sems, jax.ShapeDtypeStruct((8, BLOCK), F32)],
        input_output_aliases={a: a for a in range(2 * n)},
        compiler_params=pltpu.CompilerParams(has_side_effects=DATAFLOW))(*_in_hbm(sums + slots), after)
    return outs[:n], outs[n:2 * n], outs[2 * n], outs[2 * n + 1], outs[2 * n + 2]


def _scatter_finish(name, sums, slots, send_sem, recv_sem, after):
    n = len(sums)

    def body(*refs):
        send_ref, recv_ref = refs[2 * n], refs[2 * n + 1]
        src, dst = refs[2 * n + 3:3 * n + 3], refs[3 * n + 3:]
        copy = _scatter_copies(src, dst, send_ref, recv_ref)
        for a in range(n):
            for j in range(3):
                copy(a, j).wait_send()
                copy(a, j).wait_recv()

    outs = pl.pallas_call(
        body, name=name, in_specs=[HBM] * (2 * n) + [SEM, SEM, ANY], out_specs=[HBM] * (2 * n),
        out_shape=[pltpu.HBM(b.shape, b.dtype) for b in sums + slots],
        input_output_aliases={a: a for a in range(2 * n)},
        compiler_params=pltpu.CompilerParams(has_side_effects=DATAFLOW))(*sums, *slots, send_sem, recv_sem, after)
    return outs[:n], outs[n:]


def _slot_sum(name, place, slots, sums):
    _, h, C = slots.shape
    tr = _row_tile(h, 256)
    nrb = h // tr

    def body(place_ref, r0, r1, r2, own, o_ref):
        o_ref[...] = ((r0[...].astype(F32) + r1[...].astype(F32)) + r2[...].astype(F32)) + own[...].astype(F32)

    slot = lambda k: pl.BlockSpec((None, tr, C), lambda i, place_ref: (k, i, 0))
    return pl.pallas_call(
        body, name=name, out_shape=jax.ShapeDtypeStruct((2 * h, C), F32),
        grid_spec=pltpu.PrefetchScalarGridSpec(
            num_scalar_prefetch=1, grid=(nrb,),
            in_specs=[slot(0), slot(1), slot(2),
                      pl.BlockSpec((None, tr, C), lambda i, place_ref: (place_ref[0], i, 0))],
            out_specs=pl.BlockSpec((tr, C), lambda i, place_ref: (place_ref[1] * nrb + i, 0))),
        compiler_params=_params())(place, slots, slots, slots, sums)


def _half_exchange(bufs):
    n = len(bufs)

    def body(*refs):
        dst = refs[n:2 * n]
        send_sem, recv_sem = refs[2 * n:]
        x, y, c, _ = _place()
        copies = []
        for a in range(n):
            mine = dst[a].at[_half_rows(c, dst[a].shape[0])]
            copies.append(pltpu.make_async_remote_copy(
                src_ref=mine, dst_ref=mine, send_sem=send_sem.at[a], recv_sem=recv_sem.at[a],
                device_id=(x, y, 1 - c), device_id_type=MESH))
            copies[-1].start()
        for cp in copies:
            cp.wait()

    return pl.pallas_call(
        body, name="grad_half_exchange", in_specs=[ANY] * n, out_specs=[ANY] * n,
        out_shape=[jax.ShapeDtypeStruct(b.shape, b.dtype) for b in bufs],
        input_output_aliases={a: a for a in range(n)},
        scratch_shapes=[pltpu.SemaphoreType.DMA((n,))] * 2)(*bufs)


class _GradReducer:
    def __init__(self, chip, core):
        self.core, self.place, self.pairs, self.started = core, jnp.concatenate([chip, core]), {}, []

    def start(self, layer, group, grads):
        mine = [grads[n] for n in GROUPS[group]]
        mine, lands, send, recv, token = _pair_start("grad_pair_start_%d_%d" % (layer, group), mine, mine[0])
        self.pairs[(layer, group)] = (mine, lands, send, recv)
        return token

    def scatter(self, layer, group, after):
        tag = "%d_%d" % (layer, group)
        names = GROUPS[group]
        mine, lands, send, recv = self.pairs.pop((layer, group))
        mine, theirs = _pair_finish("grad_pair_finish_" + tag, mine, lands, send, recv, after)
        sums = [_pair_sum("pair_sum_%s_%d" % (n, layer), self.core, g, p) for n, g, p in zip(names, mine, theirs)]
        sums, slots, send, recv, token = _scatter_start("grad_scatter_start_" + tag, sums, sums[0])
        self.started.append((layer, names, sums, slots, send, recv))
        return token

    def finish(self, after):
        halves, keys = [], []
        for layer, names, sums, slots, send, recv in self.started:
            sums, slots = _scatter_finish("grad_scatter_finish_%s_%d" % (names[0], layer), sums, slots, send, recv, after)
            for n, r, s in zip(names, slots, sums):
                halves.append(_slot_sum("slot_sum_%s_%d" % (n, layer), self.place, r, s))
                keys.append((n, layer))
            after = halves[-1]
        return dict(zip(keys, _half_exchange(halves)))


def _all_reduce_small(v):
    rows = v.shape[0]
    n_dev = 2 * N_CHIPS

    def body(x_ref, out_ref, gat_ref, send_sems, recv_sems, local_sem):
        x, y, c, chips = _place()
        me, sibling = (x, y, c), (x, y, 1 - c)

        def slot(px, py, pc):
            return gat_ref.at[4 * px + 2 * py + pc]

        def copy(k, block, to, src=None):
            return pltpu.make_async_remote_copy(
                src_ref=slot(*block) if src is None else src, dst_ref=slot(*block), send_sem=send_sems.at[k],
                recv_sem=recv_sems.at[k], device_id=to, device_id_type=MESH)

        mine = pltpu.make_async_copy(x_ref, slot(*me), local_sem)
        mine.start()
        first = [copy(0, me, sibling, src=x_ref)]
        first += [copy(1 + j, me, (*chip, c), src=x_ref) for j, chip in enumerate(chips)]
        for cp in first:
            cp.start()
        passed = [copy(4 + j, (*chip, c), sibling) for j, chip in enumerate(chips)]
        for j, chip in enumerate(chips):
            copy(1 + j, (*chip, c), me).wait_recv()
            passed[j].start()
        copy(0, sibling, me).wait_recv()
        for j, chip in enumerate(chips):
            copy(4 + j, (*chip, 1 - c), me).wait_recv()
        for cp in first + passed:
            cp.wait_send()
        mine.wait()
        acc = gat_ref[0]
        for d in range(1, n_dev):
            acc = acc + gat_ref[d]
        out_ref[...] = acc

    vm = pl.BlockSpec(memory_space=pltpu.VMEM)
    return pl.pallas_call(
        body, name="small_grad_all_reduce", in_specs=[vm], out_specs=vm,
        out_shape=jax.ShapeDtypeStruct(v.shape, F32),
        scratch_shapes=[pltpu.VMEM((n_dev, rows, BLOCK), F32), pltpu.SemaphoreType.DMA((7,)),
                        pltpu.SemaphoreType.DMA((7,)), pltpu.SemaphoreType.DMA],
        compiler_params=_params())(v)


def _adamw_math(w, g, m, v):
    m2 = ADAM_B1 * m + (1.0 - ADAM_B1) * g
    v2 = ADAM_B2 * v + (1.0 - ADAM_B2) * (g * g)
    m_hat = m2 / (1.0 - ADAM_B1 ** ADAM_STEP)
    v_hat = v2 / (1.0 - ADAM_B2 ** ADAM_STEP)
    delta = -ADAM_LR * (m_hat / (jnp.sqrt(v_hat) + ADAM_EPS) + ADAM_WD * w)
    return delta, m2, v2


def _adamw_big(name, grads, w, m, v):
    L, R, C = w.shape
    tr = _row_tile(R, 128)
    nrb = R // tr

    def body(*refs):
        g_refs = refs[:L]
        w_ref, m_ref, v_ref, go_ref, d_ref, mo_ref, vo_ref = refs[L:]
        layer = pl.program_id(0)
        g = g_refs[0][...]
        for k in range(1, L):
            g = jnp.where(layer == k, g_refs[k][...], g)
        delta, m2, v2 = _adamw_math(w_ref[...], g, m_ref[...], v_ref[...])
        go_ref[...] = g
        d_ref[...] = delta
        mo_ref[...] = m2
        vo_ref[...] = v2

    def gspec(k):
        return pl.BlockSpec((tr, C), lambda l, i: (jnp.where(l == k, i, (nrb - 1) * (k < l)), 0))

    blk = pl.BlockSpec((None, tr, C), lambda l, i: (l, i, 0))
    shp = jax.ShapeDtypeStruct(w.shape, F32)
    return pl.pallas_call(
        body, name=name, grid=(L, nrb), in_specs=[gspec(k) for k in range(L)] + [blk] * 3, out_specs=[blk] * 4,
        out_shape=[shp] * 4, compiler_params=_params())(*grads, w, m, v)


def _adamw_small(g, w, m, v):
    rows = g.shape[0]
    tr = _row_tile(rows, 512)

    def body(g_ref, w_ref, m_ref, v_ref, d_ref, mo_ref, vo_ref):
        delta, m2, v2 = _adamw_math(w_ref[...], g_ref[...], m_ref[...], v_ref[...])
        d_ref[...] = delta
        mo_ref[...] = m2
        vo_ref[...] = v2

    blk = pl.BlockSpec((tr, BLOCK), lambda i: (i, 0))
    shp = jax.ShapeDtypeStruct(g.shape, F32)
    return pl.pallas_call(
        body, name="adamw_small", grid=(rows // tr,), in_specs=[blk] * 4, out_specs=[blk] * 3, out_shape=[shp] * 3,
        compiler_params=_params())(g, w, m, v)


def _pack(arrays):
    flat = jnp.concatenate([a.reshape(-1) for a in arrays])
    pad = (-flat.shape[0]) % (16 * BLOCK)
    return jnp.pad(flat, (0, pad)).reshape(-1, BLOCK)


def _unpack(packed, like):
    flat = packed.reshape(-1)
    out, off = [], 0
    for a in like:
        out.append(flat[off:off + a.size].reshape(a.shape))
        off += a.size
    return out


BIG = ("w_in", "w_attn_branch", "w_sgu_branch", "w_out", "w_gate", "w_up", "w_down")
SMALL = ("mix_norm", "q_norm", "k_norm", "sinks", "sgu_ln_g", "sgu_ln_b", "w_spatial", "b_spatial", "ffn_norm")
ORDER = ("mix_norm", "w_in", "q_norm", "k_norm", "sinks", "sgu_ln_g", "sgu_ln_b", "w_spatial", "b_spatial",
         "w_attn_branch", "w_sgu_branch", "w_out", "ffn_norm", "w_gate", "w_up", "w_down")


def _rope_tables(seq):
    pos = jnp.arange(seq, dtype=F32)
    inv_freq = jnp.power(10000.0, -jnp.arange(0, HEAD_DIM, 2, dtype=F32) / HEAD_DIM)
    ang = pos[:, None] * inv_freq[None, :]
    cos, sin = jnp.cos(ang), jnp.sin(ang)
    reps = BLOCK // HEAD_DIM
    return (jnp.tile(jnp.concatenate([cos, cos], axis=1), (1, reps)),
            jnp.tile(jnp.concatenate([-sin, sin], axis=1), (1, reps)))


def kernel(x, mix_norm, w_in, q_norm, k_norm, sinks, sgu_ln_g, sgu_ln_b, w_spatial, b_spatial, w_attn_branch, w_sgu_branch, w_out, ffn_norm, w_gate, w_up, w_down, loss_target, m_mix_norm, m_w_in, m_q_norm, m_k_norm, m_sinks, m_sgu_ln_g, m_sgu_ln_b, m_w_spatial, m_b_spatial, m_w_attn_branch, m_w_sgu_branch, m_w_out, m_ffn_norm, m_w_gate, m_w_up, m_w_down, v_mix_norm, v_w_in, v_q_norm, v_k_norm, v_sinks, v_sgu_ln_g, v_sgu_ln_b, v_w_spatial, v_b_spatial, v_w_attn_branch, v_w_sgu_branch, v_w_out, v_ffn_norm, v_w_gate, v_w_up, v_w_down):
    weights = dict(mix_norm=mix_norm, w_in=w_in, q_norm=q_norm, k_norm=k_norm, sinks=sinks, sgu_ln_g=sgu_ln_g,
                   sgu_ln_b=sgu_ln_b, w_spatial=w_spatial, b_spatial=b_spatial, w_attn_branch=w_attn_branch,
                   w_sgu_branch=w_sgu_branch, w_out=w_out, ffn_norm=ffn_norm, w_gate=w_gate, w_up=w_up, w_down=w_down)
    mom1 = dict(mix_norm=m_mix_norm, w_in=m_w_in, q_norm=m_q_norm, k_norm=m_k_norm, sinks=m_sinks,
                sgu_ln_g=m_sgu_ln_g, sgu_ln_b=m_sgu_ln_b, w_spatial=m_w_spatial, b_spatial=m_b_spatial,
                w_attn_branch=m_w_attn_branch, w_sgu_branch=m_w_sgu_branch, w_out=m_w_out, ffn_norm=m_ffn_norm,
                w_gate=m_w_gate, w_up=m_w_up, w_down=m_w_down)
    mom2 = dict(mix_norm=v_mix_norm, w_in=v_w_in, q_norm=v_q_norm, k_norm=v_k_norm, sinks=v_sinks,
                sgu_ln_g=v_sgu_ln_g, sgu_ln_b=v_sgu_ln_b, w_spatial=v_w_spatial, b_spatial=v_b_spatial,
                w_attn_branch=v_w_attn_branch, w_sgu_branch=v_w_sgu_branch, w_out=v_w_out, ffn_norm=v_ffn_norm,
                w_gate=v_w_gate, w_up=v_w_up, w_down=v_w_down)
    xs, target = x[0], loss_target[0]
    S, D = xs.shape
    L = w_in.shape[0]
    AW, KW, SW = N_Q_HEADS * HEAD_DIM, N_KV_HEADS * HEAD_DIM, SGU_GROUPS * BLOCK
    dims = (AW, KW, AW + 2 * KW + 2 * SW, AW + 2 * KW)
    cos, sin = _rope_tables(S)
    reps = BLOCK // HEAD_DIM

    chip = (2 * lax.axis_index("x") + lax.axis_index("y")).astype(jnp.int32).reshape(1)
    core = lax.axis_index("c").astype(jnp.int32).reshape(1)
    started, token = {}, chip
    for l in range(L):
        for gi, names in enumerate(GROUPS):
            bufs = [_cast_own("cast_%s_%d" % (n, l), chip, weights[n], l) for n in names]
            bufs, send, recv, token = _gather_start("gather_start_%d_%d" % (l, gi), bufs, token)
            started[(l, gi)] = (bufs, send, recv)
    stream = _WeightStream(started)
    stream.forward(0, 0, token)
    w_first = stream.finish(0, 0, token)
    sp = [dict(mix_norm=mix_norm[l][None], ffn_norm=ffn_norm[l][None], q_norm=jnp.tile(q_norm[l][None], (1, reps)),
               k_norm=jnp.tile(k_norm[l][None], (1, reps)), sinks=sinks[l][None], sgu_ln_g=sgu_ln_g[l][None],
               sgu_ln_b=sgu_ln_b[l][None], w_spatial=w_spatial[l], b_spatial=b_spatial[l][:, :, None])
          for l in range(L)]

    act, saved, wl = xs, [], []
    for l in range(L):
        act, sv, w_all, w_first = _layer_fwd(act, w_first, stream, l, l == L - 1, sp[l], cos, sin, dims)
        saved.append(sv)
        wl.append(w_all)
    loss_part, dy, dyb = _loss_head(act, target)
    loss = lax.psum(loss_part[0, 0], ("x", "y", "c"))

    reducer = _GradReducer(chip, core)
    small_g = [None] * L
    for l in reversed(range(L)):
        dy, dyb, small_g[l] = _layer_bwd(dy, dyb, wl[l], sp[l], saved[l], cos, sin, dims, reducer, l)
    grad_x = dy[None]

    reduced = reducer.finish(dy)
    grads, deltas, new_m, new_v = {}, {}, {}, {}
    for n in BIG:
        per_layer = [reduced[(n, l)] for l in range(L)]
        grads[n], deltas[n], new_m[n], new_v[n] = _adamw_big("adamw_" + n, per_layer, weights[n], mom1[n], mom2[n])

    small_like = [weights[n] for n in SMALL]
    local = [jnp.stack([small_g[l][n].reshape(weights[n].shape[1:]) for l in range(L)]) for n in SMALL]
    g_small = _all_reduce_small(_pack(local))
    d_small, m_small, v_small = _adamw_small(g_small, _pack(small_like), _pack([mom1[n] for n in SMALL]),
                                             _pack([mom2[n] for n in SMALL]))
    for n, g, d, m2, v2 in zip(SMALL, _unpack(g_small, small_like), _unpack(d_small, small_like),
                               _unpack(m_small, small_like), _unpack(v_small, small_like)):
        grads[n], deltas[n], new_m[n], new_v[n] = g, d, m2, v2

    return (loss, grad_x, *[grads[n] for n in ORDER], *[deltas[n] for n in ORDER],
            *[new_m[n] for n in ORDER], *[new_v[n] for n in ORDER])
```

```python
import functools

import jax
import jax.numpy as jnp
from jax import lax
from jax.experimental import pallas as pl
from jax.experimental.pallas import tpu as pltpu

HEAD_DIM = 64
N_Q_HEADS = 16
N_KV_HEADS = 4
SGU_GROUPS = 8
BLOCK = 128
EPS = 1e-6
ADAM_LR = 0.001
ADAM_B1 = 0.9
ADAM_B2 = 0.999
ADAM_EPS = 1e-08
ADAM_WD = 0.01
ADAM_STEP = 10
N_CHIPS = 4
VMEM_LIMIT = 52 * 1024 * 1024

F32 = jnp.float32
MXU = jnp.bfloat16
NN = (((1,), (0,)), ((), ()))
NT = (((1,), (1,)), ((), ()))
TN = (((0,), (0,)), ((), ()))
MESH = pl.DeviceIdType.MESH
ANY = pl.BlockSpec(memory_space=pl.ANY)


def _tile(n, pref):
    if n <= pref:
        return n
    best = None
    for t in range(BLOCK, pref + 1, BLOCK):
        if n % t == 0:
            best = t
    assert best is not None, (n, pref)
    return best


def _params():
    return pltpu.CompilerParams(vmem_limit_bytes=VMEM_LIMIT)


def _mm(name, grid, n_red, operands, specs, pairs, dims, n_extra, out_shapes, out_specs,
        acc_shapes, epilogue, after=None):
    n_op = len(operands) - n_extra
    n_out = len(out_shapes)
    n_acc = len(acc_shapes)
    if after is not None:
        operands, specs = list(operands) + [after], list(specs) + [ANY]
    n_in = len(operands)
    axes = [ax for ax in range(len(grid) - n_red, len(grid)) if grid[ax] > 1]

    def body(*refs):
        ops = refs[:n_op]
        extra = refs[n_op:n_op + n_extra]
        outs = refs[n_in:n_in + n_out]
        accs = refs[n_in + n_out:]

        def prod(a, b):
            return lax.dot_general(ops[a][...], ops[b][...], dims, preferred_element_type=F32)

        if not axes:
            vals = [None] * n_acc
            for a, b, k in pairs:
                d = prod(a, b)
                vals[k] = d if vals[k] is None else vals[k] + d
            epilogue(vals, extra, outs)
        else:
            first = pl.program_id(axes[0]) == 0
            last = pl.program_id(axes[0]) == grid[axes[0]] - 1
            for ax in axes[1:]:
                first = jnp.logical_and(first, pl.program_id(ax) == 0)
                last = jnp.logical_and(last, pl.program_id(ax) == grid[ax] - 1)

            @pl.when(first)
            def _():
                for acc in accs:
                    acc[...] = jnp.zeros(acc.shape, F32)

            for a, b, k in pairs:
                accs[k][...] += prod(a, b)

            @pl.when(last)
            def _():
                epilogue([acc[...] for acc in accs], extra, outs)

    scratch = [pltpu.VMEM(s, F32) for s in acc_shapes] if axes else []
    return pl.pallas_call(
        body, name=name, grid=grid, in_specs=specs, out_specs=out_specs, out_shape=out_shapes,
        scratch_shapes=scratch, compiler_params=_params())(*operands)


def _sigmoid(x):
    return 1.0 / (1.0 + jnp.exp(-x))


_GELU_C = 0.7978845608028654
_GELU_A = 0.044715


def _gelu(x):
    return 0.5 * x * (1.0 + jnp.tanh(_GELU_C * (x + _GELU_A * x * x * x)))


def _gelu_grad(x):
    t = jnp.tanh(_GELU_C * (x + _GELU_A * x * x * x))
    return 0.5 * (1.0 + t) + 0.5 * x * (1.0 - t * t) * _GELU_C * (1.0 + 3.0 * _GELU_A * x * x)


def _rms_fwd(name, x, g):
    S, D = x.shape
    tr = _tile(S, 256)

    def body(x_ref, g_ref, o_ref):
        xv = x_ref[...]
        r = lax.rsqrt(jnp.mean(xv * xv, axis=-1, keepdims=True) + EPS)
        o_ref[...] = (xv * r * g_ref[...]).astype(MXU)

    return pl.pallas_call(
        body, name=name, grid=(S // tr,),
        in_specs=[pl.BlockSpec((tr, D), lambda i: (i, 0)), pl.BlockSpec((1, D), lambda i: (0, 0))],
        out_specs=pl.BlockSpec((tr, D), lambda i: (i, 0)),
        out_shape=jax.ShapeDtypeStruct((S, D), MXU), compiler_params=_params())(x, g)


def _rms_bwd(name, dh, x, g, dres, after):
    S, D = x.shape
    tr = _tile(S, 256)

    def body(dh_ref, x_ref, g_ref, dres_ref, after_ref, dx_ref, dxb_ref, dg_ref):
        xv = x_ref[...]
        r = lax.rsqrt(jnp.mean(xv * xv, axis=-1, keepdims=True) + EPS)
        xh = xv * r
        dhv = dh_ref[...]
        dy = dhv * g_ref[...]
        dx = dres_ref[...] + r * (dy - xh * jnp.mean(dy * xh, axis=-1, keepdims=True))
        dx_ref[...] = dx
        dxb_ref[...] = dx.astype(MXU)

        @pl.when(pl.program_id(0) == 0)
        def _():
            dg_ref[...] = jnp.zeros(dg_ref.shape, F32)

        dg_ref[...] += jnp.sum(dhv * xh, axis=0, keepdims=True)

    row = pl.BlockSpec((tr, D), lambda i: (i, 0))
    vec = pl.BlockSpec((1, D), lambda i: (0, 0))
    return pl.pallas_call(
        body, name=name, grid=(S // tr,), in_specs=[row, row, vec, row, ANY], out_specs=[row, row, vec],
        out_shape=[jax.ShapeDtypeStruct((S, D), F32), jax.ShapeDtypeStruct((S, D), MXU),
                   jax.ShapeDtypeStruct((1, D), F32)],
        compiler_params=_params())(dh, x, g, dres, after)


def _loss_head(y, target):
    S, D = y.shape
    tr = _tile(S, 256)

    def body(y_ref, t_ref, loss_ref, dy_ref, dyb_ref):
        d = y_ref[...] - t_ref[...]
        dy = d * (1.0 / D)
        dy_ref[...] = dy
        dyb_ref[...] = dy.astype(MXU)

        @pl.when(pl.program_id(0) == 0)
        def _():
            loss_ref[...] = jnp.zeros(loss_ref.shape, F32)

        loss_ref[...] += (0.5 / D) * jnp.sum(jnp.sum(d * d, axis=-1, keepdims=True), axis=0, keepdims=True)

    row = pl.BlockSpec((tr, D), lambda i: (i, 0))
    return pl.pallas_call(
        body, name="loss_head", grid=(S // tr,), in_specs=[row, row],
        out_specs=[pl.BlockSpec((1, 1), lambda i: (0, 0)), row, row],
        out_shape=[jax.ShapeDtypeStruct((1, 1), F32), jax.ShapeDtypeStruct((S, D), F32),
                   jax.ShapeDtypeStruct((S, D), MXU)],
        compiler_params=_params())(y, target)


def _head_sum(v):
    r = lax.broadcasted_iota(jnp.int32, (BLOCK, BLOCK), 0) // HEAD_DIM
    c = lax.broadcasted_iota(jnp.int32, (BLOCK, BLOCK), 1) // HEAD_DIM
    ones = jnp.where(r == c, 1.0, 0.0).astype(jnp.bfloat16)
    hi = v.astype(jnp.bfloat16)
    lo = (v - hi.astype(F32)).astype(jnp.bfloat16)
    parts = []
    for t in range(v.shape[1] // BLOCK):
        sl = slice(t * BLOCK, (t + 1) * BLOCK)
        parts.append(jnp.dot(hi[:, sl], ones, preferred_element_type=F32)
                     + jnp.dot(lo[:, sl], ones, preferred_element_type=F32))
    return parts[0] if len(parts) == 1 else jnp.concatenate(parts, axis=-1)


def _swap_halves(v):
    w = v.shape[1]
    half = HEAD_DIM // 2
    lane = lax.broadcasted_iota(jnp.int32, v.shape, 1) % HEAD_DIM
    return jnp.where(lane < half, pltpu.roll(v, w - half, 1), pltpu.roll(v, half, 1))


def _norm_rope(xv, gain, cos, sin):
    r = lax.rsqrt(_head_sum(xv * xv) * (1.0 / HEAD_DIM) + EPS)
    xn = xv * r * gain
    return xn * cos + _swap_halves(xn) * sin


def _norm_rope_bwd(dy, xv, gain, cos, sin):
    r = lax.rsqrt(_head_sum(xv * xv) * (1.0 / HEAD_DIM) + EPS)
    xh = xv * r
    dxn = dy * cos + _swap_halves(dy * sin)
    dgain = jnp.sum(dxn * xh, axis=0, keepdims=True)
    dxh = dxn * gain
    dx = r * (dxh - xh * (_head_sum(dxh * xh) * (1.0 / HEAD_DIM)))
    return dx, dgain


def _fold_heads(v):
    acc = v[:, 0:BLOCK]
    for t in range(1, v.shape[1] // BLOCK):
        acc = acc + v[:, t * BLOCK:(t + 1) * BLOCK]
    return acc + pltpu.roll(acc, HEAD_DIM, 1)


def _tile_lanes(v, width):
    return v if width == BLOCK else jnp.tile(v, (1, width // BLOCK))


def _qk_prep(proj, qg, kg, cos, sin, AW, KW):
    S = proj.shape[0]
    tr = _tile(S, 256)
    scale = HEAD_DIM ** -0.5

    def body(q_ref, k_ref, v_ref, qg_ref, kg_ref, cos_ref, sin_ref, qo_ref, ko_ref, vo_ref):
        c, s = cos_ref[...], sin_ref[...]
        q = _norm_rope(q_ref[...], _tile_lanes(qg_ref[...], AW), _tile_lanes(c, AW), _tile_lanes(s, AW))
        k = _norm_rope(k_ref[...], _tile_lanes(kg_ref[...], KW), _tile_lanes(c, KW), _tile_lanes(s, KW))
        qo_ref[...] = (q * scale).astype(MXU)
        ko_ref[...] = k.astype(MXU)
        vo_ref[...] = v_ref[...].astype(MXU)

    assert AW % KW == 0
    vec = pl.BlockSpec((1, BLOCK), lambda i: (0, 0))
    tab = pl.BlockSpec((tr, BLOCK), lambda i: (i, 0))
    return pl.pallas_call(
        body, name="qk_prep", grid=(S // tr,),
        in_specs=[pl.BlockSpec((tr, AW), lambda i: (i, 0)),
                  pl.BlockSpec((tr, KW), lambda i: (i, AW // KW)),
                  pl.BlockSpec((tr, KW), lambda i: (i, AW // KW + 1)), vec, vec, tab, tab],
        out_specs=[pl.BlockSpec((tr, AW), lambda i: (i, 0)), pl.BlockSpec((tr, KW), lambda i: (i, 0)),
                   pl.BlockSpec((tr, KW), lambda i: (i, 0))],
        out_shape=[jax.ShapeDtypeStruct((S, AW), MXU), jax.ShapeDtypeStruct((S, KW), MXU),
                   jax.ShapeDtypeStruct((S, KW), MXU)],
        compiler_params=_params())(proj, proj, proj, qg, kg, cos, sin)


def _attn_probs(n, q, kp, kc, g, sink_ref, qpk):
    hd = HEAD_DIM
    kcat = jnp.concatenate([kp[:, g * hd:(g + 1) * hd], kc[:, g * hd:(g + 1) * hd]], axis=0)
    qs = jnp.concatenate([q[:, (g * qpk + j) * hd:(g * qpk + j + 1) * hd] for j in range(qpk)], axis=0)
    s = lax.dot_general(qs, kcat, NT, preferred_element_type=F32)
    row = lax.broadcasted_iota(jnp.int32, (BLOCK, 2 * BLOCK), 0)
    col = lax.broadcasted_iota(jnp.int32, (BLOCK, 2 * BLOCK), 1)
    ok = (col > row) & (col <= row + BLOCK) & ((col >= BLOCK) | (n > 0))
    s = jnp.where(jnp.concatenate([ok] * qpk, axis=0), s, -1e30)
    sk = jnp.concatenate([jnp.full((BLOCK, 1), sink_ref[0, g * qpk + j], F32) for j in range(qpk)], axis=0)
    m = jnp.maximum(jnp.max(s, axis=-1, keepdims=True), sk)
    e = jnp.exp(s - m)
    es = jnp.exp(sk - m)
    z = jnp.sum(e, axis=-1, keepdims=True) + es
    return e / z, es / z, qs, kcat


def _attn_fwd(qr, kr, vb, sinks):
    S, AW = qr.shape
    KW = kr.shape[1]
    nb = S // BLOCK
    nkv = KW // HEAD_DIM
    qpk = AW // KW
    hd = HEAD_DIM

    def body(sink_ref, q_ref, kp_ref, kc_ref, vp_ref, vc_ref, o_ref):
        n = pl.program_id(0)
        q, kp, kc, vp, vc = q_ref[...], kp_ref[...], kc_ref[...], vp_ref[...], vc_ref[...]
        outs = [None] * (nkv * qpk)
        for g in range(nkv):
            p, _, _, _ = _attn_probs(n, q, kp, kc, g, sink_ref, qpk)
            vcat = jnp.concatenate([vp[:, g * hd:(g + 1) * hd], vc[:, g * hd:(g + 1) * hd]], axis=0)
            o = jnp.dot(p.astype(MXU), vcat, preferred_element_type=F32)
            for j in range(qpk):
                outs[g * qpk + j] = o[j * BLOCK:(j + 1) * BLOCK]
        o_ref[...] = jnp.concatenate(outs, axis=-1).astype(MXU)

    cur = lambda n: (n, 0)
    prev = lambda n: (jnp.maximum(n - 1, 0), 0)
    return pl.pallas_call(
        body, name="attn_fwd", grid=(nb,),
        in_specs=[pl.BlockSpec(memory_space=pltpu.SMEM), pl.BlockSpec((BLOCK, AW), cur),
                  pl.BlockSpec((BLOCK, KW), prev), pl.BlockSpec((BLOCK, KW), cur),
                  pl.BlockSpec((BLOCK, KW), prev), pl.BlockSpec((BLOCK, KW), cur)],
        out_specs=pl.BlockSpec((BLOCK, AW), cur),
        out_shape=jax.ShapeDtypeStruct((S, AW), MXU), compiler_params=_params())(sinks, qr, kr, kr, vb, vb)


def _attn_bwd(qr, kr, vb, sinks, dattn):
    S, AW = qr.shape
    KW = kr.shape[1]
    nb = S // BLOCK
    nkv = KW // HEAD_DIM
    qpk = AW // KW
    hd = HEAD_DIM
    scale = HEAD_DIM ** -0.5

    def body(sink_ref, q_ref, kp_ref, kc_ref, vp_ref, vc_ref, do_ref,
             dq_ref, dkp_ref, dkc_ref, dvp_ref, dvc_ref, dsink_ref):
        n = pl.program_id(0)
        q, kp, kc, vp, vc = q_ref[...], kp_ref[...], kc_ref[...], vp_ref[...], vc_ref[...]
        do = do_ref[...].astype(MXU)
        lane = lax.broadcasted_iota(jnp.int32, (1, BLOCK), 1)
        dsink = jnp.zeros((1, BLOCK), F32)
        dqs = [None] * (nkv * qpk)
        dkps, dkcs, dvps, dvcs = [], [], [], []
        for g in range(nkv):
            p, psink, qs, kcat = _attn_probs(n, q, kp, kc, g, sink_ref, qpk)
            vcat = jnp.concatenate([vp[:, g * hd:(g + 1) * hd], vc[:, g * hd:(g + 1) * hd]], axis=0)
            dos = jnp.concatenate([do[:, (g * qpk + j) * hd:(g * qpk + j + 1) * hd] for j in range(qpk)], axis=0)
            dp = lax.dot_general(dos, vcat, NT, preferred_element_type=F32)
            dv = lax.dot_general(p.astype(MXU), dos, TN, preferred_element_type=F32)
            delta = jnp.sum(p * dp, axis=-1, keepdims=True)
            ds = (p * (dp - delta)).astype(MXU)
            dsk = -psink * delta
            dq = jnp.dot(ds, kcat, preferred_element_type=F32) * scale
            dk = lax.dot_general(ds, qs, TN, preferred_element_type=F32)
            for j in range(qpk):
                dqs[g * qpk + j] = dq[j * BLOCK:(j + 1) * BLOCK]
                tot = jnp.sum(dsk[j * BLOCK:(j + 1) * BLOCK], axis=0, keepdims=True)
                dsink = dsink + jnp.where(lane == g * qpk + j, tot, 0.0)
            dkps.append(dk[:BLOCK])
            dkcs.append(dk[BLOCK:])
            dvps.append(dv[:BLOCK])
            dvcs.append(dv[BLOCK:])
        dq_ref[...] = jnp.concatenate(dqs, axis=-1)
        dkp_ref[...] = jnp.concatenate(dkps, axis=-1)
        dkc_ref[...] = jnp.concatenate(dkcs, axis=-1)
        dvp_ref[...] = jnp.concatenate(dvps, axis=-1)
        dvc_ref[...] = jnp.concatenate(dvcs, axis=-1)

        @pl.when(n == 0)
        def _():
            dsink_ref[...] = jnp.zeros(dsink_ref.shape, F32)

        dsink_ref[...] += dsink

    cur = lambda n: (n, 0)
    prev = lambda n: (jnp.maximum(n - 1, 0), 0)
    kv = jax.ShapeDtypeStruct((S, KW), F32)
    kvspec = pl.BlockSpec((BLOCK, KW), cur)
    return pl.pallas_call(
        body, name="attn_bwd", grid=(nb,),
        in_specs=[pl.BlockSpec(memory_space=pltpu.SMEM), pl.BlockSpec((BLOCK, AW), cur),
                  pl.BlockSpec((BLOCK, KW), prev), kvspec, pl.BlockSpec((BLOCK, KW), prev), kvspec,
                  pl.BlockSpec((BLOCK, AW), cur)],
        out_specs=[pl.BlockSpec((BLOCK, AW), cur), kvspec, kvspec, kvspec, kvspec,
                   pl.BlockSpec((1, BLOCK), lambda n: (0, 0))],
        out_shape=[jax.ShapeDtypeStruct((S, AW), F32), kv, kv, kv, kv, jax.ShapeDtypeStruct((1, BLOCK), F32)],
        compiler_params=_params())(sinks, qr, kr, kr, vb, vb, dattn)


def _qk_prep_bwd(proj, qg, kg, cos, sin, dq, dkp, dkc, dvp, dvc, AW, KW):
    S = proj.shape[0]
    nb = S // BLOCK

    def body(q_ref, k_ref, qg_ref, kg_ref, cos_ref, sin_ref, dq_ref, dkp_ref, dkc_ref, dvp_ref, dvc_ref,
             o_ref, dqg_ref, dkg_ref):
        n = pl.program_id(0)
        c, s = cos_ref[...], sin_ref[...]
        has_next = jnp.where(n < nb - 1, 1.0, 0.0)
        dk = dkc_ref[...] + has_next * dkp_ref[...]
        dv = dvc_ref[...] + has_next * dvp_ref[...]
        dxq, dqg = _norm_rope_bwd(dq_ref[...], q_ref[...], _tile_lanes(qg_ref[...], AW),
                                  _tile_lanes(c, AW), _tile_lanes(s, AW))
        dxk, dkg = _norm_rope_bwd(dk, k_ref[...], _tile_lanes(kg_ref[...], KW),
                                  _tile_lanes(c, KW), _tile_lanes(s, KW))
        o_ref[...] = jnp.concatenate([dxq, dxk, dv], axis=-1).astype(MXU)

        @pl.when(n == 0)
        def _():
            dqg_ref[...] = jnp.zeros(dqg_ref.shape, F32)
            dkg_ref[...] = jnp.zeros(dkg_ref.shape, F32)

        dqg_ref[...] += _fold_heads(dqg)
        dkg_ref[...] += _fold_heads(dkg)

    cur = lambda n: (n, 0)
    nxt = lambda n: (jnp.minimum(n + 1, nb - 1), 0)
    vec = pl.BlockSpec((1, BLOCK), lambda n: (0, 0))
    tab = pl.BlockSpec((BLOCK, BLOCK), cur)
    return pl.pallas_call(
        body, name="qk_prep_bwd", grid=(nb,),
        in_specs=[pl.BlockSpec((BLOCK, AW), cur), pl.BlockSpec((BLOCK, KW), lambda n: (n, AW // KW)),
                  vec, vec, tab, tab, pl.BlockSpec((BLOCK, AW), cur),
                  pl.BlockSpec((BLOCK, KW), nxt), pl.BlockSpec((BLOCK, KW), cur),
                  pl.BlockSpec((BLOCK, KW), nxt), pl.BlockSpec((BLOCK, KW), cur)],
        out_specs=[pl.BlockSpec((BLOCK, AW + 2 * KW), cur), vec, vec],
        out_shape=[jax.ShapeDtypeStruct((S, AW + 2 * KW), MXU), jax.ShapeDtypeStruct((1, BLOCK), F32),
                   jax.ShapeDtypeStruct((1, BLOCK), F32)],
        compiler_params=_params())(proj, proj, qg, kg, cos, sin, dq, dkp, dkc, dvp, dvc)


SGU_LANES = 512
SGU_ROWS = 256


def _sgu_group(v, lng, lnb, w_f32, b):
    rows = v.shape[0]
    mu = jnp.mean(v, axis=-1, keepdims=True)
    vc = v - mu
    r = lax.rsqrt(jnp.mean(vc * vc, axis=-1, keepdims=True) + EPS)
    xh = vc * r
    vn = (xh * lng + lnb).astype(MXU)
    row = lax.broadcasted_iota(jnp.int32, (BLOCK, BLOCK), 0)
    col = lax.broadcasted_iota(jnp.int32, (BLOCK, BLOCK), 1)
    tri = row >= col
    w = jnp.where(tri, w_f32, 0.0).astype(MXU)
    chunks = [jnp.dot(w, vn[k * BLOCK:(k + 1) * BLOCK], preferred_element_type=F32) + b for k in range(rows // BLOCK)]
    s = chunks[0] if len(chunks) == 1 else jnp.concatenate(chunks, axis=0)
    return xh, r, vn, w, s, tri


def _sgu_layout(S, u_col):
    SW = SGU_GROUPS * BLOCK
    lb, tr = min(SGU_LANES, SW), min(SGU_ROWS, S)
    assert u_col % lb == 0 and SW % lb == 0 and S % tr == 0
    ub, nlb, gpb = u_col // lb, SW // lb, lb // BLOCK
    specs = [pl.BlockSpec((tr, lb), lambda j, i: (i, ub + j)), pl.BlockSpec((tr, lb), lambda j, i: (i, ub + nlb + j)),
             pl.BlockSpec((1, lb), lambda j, i: (0, j)), pl.BlockSpec((1, lb), lambda j, i: (0, j)),
             pl.BlockSpec((gpb, BLOCK, BLOCK), lambda j, i: (j, 0, 0)),
             pl.BlockSpec((gpb, BLOCK, 1), lambda j, i: (j, 0, 0))]
    return lb, tr, gpb, nlb, specs


def _sgu_fwd(proj, lng, lnb, ws, bs, u_col):
    S = proj.shape[0]
    lb, tr, gpb, nlb, specs = _sgu_layout(S, u_col)

    def body(pu_ref, pv_ref, lng_ref, lnb_ref, w_ref, b_ref, o_ref):
        u = _gelu(pu_ref[...])
        v = _gelu(pv_ref[...])
        outs = []
        for g in range(gpb):
            sl = slice(g * BLOCK, (g + 1) * BLOCK)
            s = _sgu_group(v[:, sl], lng_ref[:, sl], lnb_ref[:, sl], w_ref[g], b_ref[g])[4]
            outs.append(u[:, sl] * s)
        o_ref[...] = (outs[0] if gpb == 1 else jnp.concatenate(outs, axis=-1)).astype(MXU)

    return pl.pallas_call(
        body, name="sgu_fwd", grid=(nlb, S // tr), in_specs=specs,
        out_specs=pl.BlockSpec((tr, lb), lambda j, i: (i, j)),
        out_shape=jax.ShapeDtypeStruct((S, nlb * lb), MXU), compiler_params=_params())(proj, proj, lng, lnb, ws, bs)


def _sgu_bwd(proj, lng, lnb, ws, bs, dsgu, u_col, after):
    S = proj.shape[0]
    G = SGU_GROUPS
    lb, tr, gpb, nlb, specs = _sgu_layout(S, u_col)
    nch = tr // BLOCK

    def body(pu_ref, pv_ref, lng_ref, lnb_ref, w_ref, b_ref, do_ref, after_ref,
             dpu_ref, dpv_ref, dw_ref, db_ref, dlng_ref, dlnb_ref):
        pu, pv, do = pu_ref[...], pv_ref[...], do_ref[...]
        u = _gelu(pu)
        v = _gelu(pv)

        @pl.when(pl.program_id(1) == 0)
        def _():
            dw_ref[...] = jnp.zeros(dw_ref.shape, F32)
            db_ref[...] = jnp.zeros(db_ref.shape, F32)
            dlng_ref[...] = jnp.zeros(dlng_ref.shape, F32)
            dlnb_ref[...] = jnp.zeros(dlnb_ref.shape, F32)

        ss, dvs, dlng, dlnb = [], [], [], []
        for g in range(gpb):
            sl = slice(g * BLOCK, (g + 1) * BLOCK)
            xh, r, vn, w, s, tri = _sgu_group(v[:, sl], lng_ref[:, sl], lnb_ref[:, sl], w_ref[g], b_ref[g])
            ds = do[:, sl] * u[:, sl]
            dsb = ds.astype(MXU)
            dw, db, dvn = None, None, []
            for k in range(nch):
                rows = slice(k * BLOCK, (k + 1) * BLOCK)
                part = lax.dot_general(dsb[rows], vn[rows], NT, preferred_element_type=F32)
                dw = part if dw is None else dw + part
                rowsum = jnp.sum(ds[rows], axis=-1, keepdims=True)
                db = rowsum if db is None else db + rowsum
                dvn.append(lax.dot_general(w, dsb[rows], TN, preferred_element_type=F32))
            dvn = dvn[0] if nch == 1 else jnp.concatenate(dvn, axis=0)
            dw_ref[g] += jnp.where(tri, dw, 0.0)
            db_ref[g] += db
            dxh = dvn * lng_ref[:, sl]
            dvs.append(r * (dxh - jnp.mean(dxh, axis=-1, keepdims=True)
                            - xh * jnp.mean(dxh * xh, axis=-1, keepdims=True)))
            dlng.append(jnp.sum(dvn * xh, axis=0, keepdims=True))
            dlnb.append(jnp.sum(dvn, axis=0, keepdims=True))
            ss.append(s)
        cat = lambda parts: parts[0] if gpb == 1 else jnp.concatenate(parts, axis=-1)
        dpu_ref[...] = (do * cat(ss) * _gelu_grad(pu)).astype(MXU)
        dpv_ref[...] = (cat(dvs) * _gelu_grad(pv)).astype(MXU)
        dlng_ref[...] += cat(dlng)
        dlnb_ref[...] += cat(dlnb)

    tile = pl.BlockSpec((tr, lb), lambda j, i: (i, j))
    vec = pl.BlockSpec((1, lb), lambda j, i: (0, j))
    half = jax.ShapeDtypeStruct((S, G * BLOCK), MXU)
    return pl.pallas_call(
        body, name="sgu_bwd", grid=(nlb, S // tr), in_specs=specs + [tile, ANY],
        out_specs=[tile, tile, pl.BlockSpec((gpb, BLOCK, BLOCK), lambda j, i: (j, 0, 0)),
                   pl.BlockSpec((gpb, BLOCK, 1), lambda j, i: (j, 0, 0)), vec, vec],
        out_shape=[half, half, jax.ShapeDtypeStruct((G, BLOCK, BLOCK), F32),
                   jax.ShapeDtypeStruct((G, BLOCK, 1), F32),
                   jax.ShapeDtypeStruct((1, G * BLOCK), F32), jax.ShapeDtypeStruct((1, G * BLOCK), F32)],
        compiler_params=_params())(proj, proj, lng, lnb, ws, bs, dsgu, after)


def _store_f32(vals, extra, outs):
    for v, o in zip(vals, outs):
        o[...] = v


def _store_mxu(vals, extra, outs):
    for v, o in zip(vals, outs):
        o[...] = v.astype(MXU)


def _proj_in(h, w):
    S, D = h.shape
    Ns = w.shape[2]
    tm, tn = _tile(S, 1024), _tile(Ns, 1024)
    npb = Ns // tn
    return _mm("proj_in", (S // tm, N_CHIPS, npb), 0, [h, w],
               [pl.BlockSpec((tm, D), lambda i, s, j: (i, 0)), pl.BlockSpec((None, D, tn), lambda i, s, j: (s, 0, j))],
               [(0, 1, 0)], NN, 0, [jax.ShapeDtypeStruct((S, N_CHIPS * Ns), F32)],
               [pl.BlockSpec((tm, tn), lambda i, s, j: (i, s * npb + j))], [None], _store_f32)[0]


def _branches(attn, sgu, wa, ws, proj, gate0):
    S, AW = attn.shape
    SW = sgu.shape[1]
    Nb = wa.shape[2]
    D = N_CHIPS * Nb
    tm = _tile(S, 512)
    assert gate0 % Nb == 0
    ga, gb = gate0 // Nb, (gate0 + D) // Nb

    def epilogue(vals, extra, outs):
        a, b = vals
        outs[0][...] = (_sigmoid(extra[0][...]) * a + _sigmoid(extra[1][...]) * b).astype(MXU)
        outs[1][...] = a
        outs[2][...] = b

    tile = pl.BlockSpec((tm, Nb), lambda i, s: (i, s))
    wspec = lambda k: pl.BlockSpec((None, k, Nb), lambda i, s: (s, 0, 0))
    f = jax.ShapeDtypeStruct((S, D), F32)
    return _mm("branches", (S // tm, N_CHIPS), 0, [attn, sgu, wa, ws, proj, proj],
               [pl.BlockSpec((tm, AW), lambda i, s: (i, 0)), pl.BlockSpec((tm, SW), lambda i, s: (i, 0)),
                wspec(AW), wspec(SW), pl.BlockSpec((tm, Nb), lambda i, s: (i, ga + s)),
                pl.BlockSpec((tm, Nb), lambda i, s: (i, gb + s))],
               [(0, 2, 0), (1, 3, 1)], NN, 2, [jax.ShapeDtypeStruct((S, D), MXU), f, f], [tile] * 3,
               [None, None], epilogue)


def _rows_mm(name, a, w, res):
    S = a.shape[0]
    _, K, N = w.shape
    tm, tn = _tile(S, 1024), _tile(N, 1024)

    def epilogue(vals, extra, outs):
        outs[0][...] = extra[0][...] + vals[0]

    out = pl.BlockSpec((tm, tn), lambda i, j, s: (i, j))
    return _mm(name, (S // tm, N // tn, N_CHIPS), 1, [a, w, res],
               [pl.BlockSpec((tm, K), lambda i, j, s: (i, s)), pl.BlockSpec((None, K, tn), lambda i, j, s: (s, 0, j)), out],
               [(0, 1, 0)], NN, 1, [jax.ShapeDtypeStruct((S, N), F32)], [out], [(tm, tn)], epilogue)[0]


def _gate_up(h2, wg, wu):
    S, D = h2.shape
    Nf = wg.shape[2]
    tm, tk = _tile(S, 512), _tile(D, 512)

    def epilogue(vals, extra, outs):
        g, u = vals
        outs[0][...] = g
        outs[1][...] = u
        outs[2][...] = (g * _sigmoid(g) * u).astype(MXU)

    w = pl.BlockSpec((None, tk, Nf), lambda i, s, k: (s, k, 0))
    o = pl.BlockSpec((tm, Nf), lambda i, s, k: (i, s))
    f = jax.ShapeDtypeStruct((S, N_CHIPS * Nf), F32)
    return _mm("gate_up", (S // tm, N_CHIPS, D // tk), 1, [h2, wg, wu],
               [pl.BlockSpec((tm, tk), lambda i, s, k: (i, k)), w, w], [(0, 1, 0), (0, 2, 1)], NN, 0,
               [f, f, jax.ShapeDtypeStruct((S, N_CHIPS * Nf), MXU)], [o, o, o], [(tm, Nf)] * 2, epilogue)


def _down_bwd(dyb, wd, g, u):
    S, D = dyb.shape
    Kf = wd.shape[1]
    tm = _tile(S, 512)

    def epilogue(vals, extra, outs):
        da, gv, uv = vals[0], extra[0][...], extra[1][...]
        sg = _sigmoid(gv)
        outs[0][...] = (da * uv * sg * (1.0 + gv * (1.0 - sg))).astype(MXU)
        outs[1][...] = (da * gv * sg).astype(MXU)

    t = pl.BlockSpec((tm, Kf), lambda i, s: (i, s))
    o = jax.ShapeDtypeStruct((S, N_CHIPS * Kf), MXU)
    return _mm("down_bwd", (S // tm, N_CHIPS), 0, [dyb, wd, g, u],
               [pl.BlockSpec((tm, D), lambda i, s: (i, 0)), pl.BlockSpec((None, Kf, D), lambda i, s: (s, 0, 0)), t, t],
               [(0, 1, 0)], NT, 2, [o, o], [t, t], [None], epilogue)


def _out_bwd(dxb, wo, proj, ba, bb, gate0):
    S, D = dxb.shape
    Ko = wo.shape[1]
    tm = _tile(S, 512)
    assert gate0 % Ko == 0
    ga, gb = gate0 // Ko, (gate0 + D) // Ko

    def epilogue(vals, extra, outs):
        dm = vals[0]
        sa, sb = _sigmoid(extra[0][...]), _sigmoid(extra[1][...])
        outs[0][...] = (dm * sa).astype(MXU)
        outs[1][...] = (dm * sb).astype(MXU)
        outs[2][...] = (dm * extra[2][...] * sa * (1.0 - sa)).astype(MXU)
        outs[3][...] = (dm * extra[3][...] * sb * (1.0 - sb)).astype(MXU)

    t = pl.BlockSpec((tm, Ko), lambda i, s: (i, s))
    o = jax.ShapeDtypeStruct((S, D), MXU)
    return _mm("out_bwd", (S // tm, N_CHIPS), 0, [dxb, wo, proj, proj, ba, bb],
               [pl.BlockSpec((tm, D), lambda i, s: (i, 0)), pl.BlockSpec((None, Ko, D), lambda i, s: (s, 0, 0)),
                pl.BlockSpec((tm, Ko), lambda i, s: (i, ga + s)), pl.BlockSpec((tm, Ko), lambda i, s: (i, gb + s)), t, t],
               [(0, 1, 0)], NT, 4, [o] * 4, [t] * 4, [None], epilogue)


def _dx_cols(name, terms, n_out, after=None):
    S = terms[0][0].shape[0]
    _, K, Ns = terms[0][1].shape
    tm, tko, tn = _tile(S, 1024), _tile(K, 1024), _tile(Ns, 1408)
    npb = Ns // tn
    operands, specs, pairs = [], [], []
    for t, (dy, w, k) in enumerate(terms):
        assert w.shape == (N_CHIPS, K, Ns)
        operands += [dy, w]
        specs += [pl.BlockSpec((tm, tn), lambda i, jk, s, jn: (i, s * npb + jn)),
                  pl.BlockSpec((None, tko, tn), lambda i, jk, s, jn: (s, jk, jn))]
        pairs.append((2 * t, 2 * t + 1, k))
    out = pl.BlockSpec((tm, tko), lambda i, jk, s, jn: (i, jk))
    return _mm(name, (S // tm, K // tko, N_CHIPS, npb), 2, operands, specs, pairs, NT, 0,
               [jax.ShapeDtypeStruct((S, K), F32)] * n_out, [out] * n_out, [(tm, tko)] * n_out, _store_f32, after)


def _dw_cols(name, a, dy):
    S, K = a.shape
    Ns = dy.shape[1] // N_CHIPS
    tk, tn = _tile(K, 512), _tile(Ns, 1408)
    npb = Ns // tn
    return _mm(name, (K // tk, N_CHIPS, npb), 0, [a, dy],
               [pl.BlockSpec((S, tk), lambda jk, s, jn: (0, jk)), pl.BlockSpec((S, tn), lambda jk, s, jn: (0, s * npb + jn))],
               [(0, 1, 0)], TN, 0, [jax.ShapeDtypeStruct((N_CHIPS, K, Ns), MXU)],
               [pl.BlockSpec((None, tk, tn), lambda jk, s, jn: (s, jk, jn))], [None], _store_mxu)[0]


def _dw_rows(name, a, dy):
    S = a.shape[0]
    K = a.shape[1] // N_CHIPS
    N = dy.shape[1]
    tk, tn = _tile(K, 1408), _tile(N, 1024)
    nkb = K // tk
    return _mm(name, (N_CHIPS, nkb, N // tn), 0, [a, dy],
               [pl.BlockSpec((S, tk), lambda s, jk, jn: (0, s * nkb + jk)), pl.BlockSpec((S, tn), lambda s, jk, jn: (0, jn))],
               [(0, 1, 0)], TN, 0, [jax.ShapeDtypeStruct((N_CHIPS, K, N), MXU)],
               [pl.BlockSpec((None, tk, tn), lambda s, jk, jn: (s, jk, jn))], [None], _store_mxu)[0]


def _layer_fwd(x, stream, layer, last, sp, cos, sin, dims):
    AW, KW, gate0, u_col = dims
    h = _rms_fwd("mix_norm", x, sp["mix_norm"])
    w = stream.finish(layer, 0, h)
    proj = _proj_in(h, w["w_in"])
    qr, kr, vb = _qk_prep(proj, sp["q_norm"], sp["k_norm"], cos, sin, AW, KW)
    stream.forward(layer, 1, qr)
    attn = _attn_fwd(qr, kr, vb, sp["sinks"])
    w.update(stream.finish(layer, 1, attn))
    sgu = _sgu_fwd(proj, sp["sgu_ln_g"], sp["sgu_ln_b"], sp["w_spatial"], sp["b_spatial"], u_col)
    merged, ba, bb = _branches(attn, sgu, w["w_attn_branch"], w["w_sgu_branch"], proj, gate0)
    stream.forward(layer, 2, merged)
    x1 = _rows_mm("out_proj", merged, w["w_out"], x)
    w.update(stream.finish(layer, 2, x1))
    h2 = _rms_fwd("ffn_norm", x1, sp["ffn_norm"])
    stream.forward(layer, 3, h2)
    g, u, act = _gate_up(h2, w["w_gate"], w["w_up"])
    w.update(stream.finish(layer, 3, g))
    x2 = _rows_mm("down_proj", act, w["w_down"], x1)
    if not last:
        stream.forward(layer + 1, 0, x2)
    saved = dict(x=x, h=h, proj=proj, qr=qr, kr=kr, vb=vb, attn=attn, sgu=sgu, merged=merged, ba=ba, bb=bb,
                 x1=x1, h2=h2, g=g, u=u, act=act)
    return x2, saved, w


def _layer_bwd(dy, dyb, w, sp, sv, cos, sin, dims, reducer, layer):
    AW, KW, gate0, u_col = dims
    big, small = {}, {}
    dg, du = _down_bwd(dyb, w["w_down"], sv["g"], sv["u"])
    big["w_down"] = _dw_rows("dw_down", sv["act"], dyb)
    big["w_gate"] = _dw_cols("dw_gate", sv["h2"], dg)
    big["w_up"] = _dw_cols("dw_up", sv["h2"], du)
    token = reducer.start(layer, 2, big)
    dh2 = _dx_cols("dh2", [(dg, w["w_gate"], 0), (du, w["w_up"], 0)], 1, token)[0]
    token = reducer.scatter(layer, 2, dh2)
    dx1, dx1b, small["ffn_norm"] = _rms_bwd("ffn_norm_bwd", dh2, sv["x1"], sp["ffn_norm"], dy, token)
    dba, dbb, dgla, dglb = _out_bwd(dx1b, w["w_out"], sv["proj"], sv["ba"], sv["bb"], gate0)
    big["w_out"] = _dw_rows("dw_out", sv["merged"], dx1b)
    big["w_attn_branch"] = _dw_cols("dw_attn_branch", sv["attn"], dba)
    big["w_sgu_branch"] = _dw_cols("dw_sgu_branch", sv["sgu"], dbb)
    token = reducer.start(layer, 1, big)
    dattn, dsgu = _dx_cols("dbranch_in", [(dba, w["w_attn_branch"], 0), (dbb, w["w_sgu_branch"], 1)], 2, token)
    token = reducer.scatter(layer, 1, dsgu)
    dpu, dpv, small["w_spatial"], db, small["sgu_ln_g"], small["sgu_ln_b"] = _sgu_bwd(
        sv["proj"], sp["sgu_ln_g"], sp["sgu_ln_b"], sp["w_spatial"], sp["b_spatial"], dsgu, u_col, token)
    small["b_spatial"] = db[:, :, 0]
    dq, dkp, dkc, dvp, dvc, dsink = _attn_bwd(sv["qr"], sv["kr"], sv["vb"], sp["sinks"], dattn)
    small["sinks"] = dsink[:, :sp["sinks"].shape[1]]
    dqkv, dqg, dkg = _qk_prep_bwd(sv["proj"], sp["q_norm"], sp["k_norm"], cos, sin, dq, dkp, dkc, dvp, dvc, AW, KW)
    small["q_norm"] = dqg[:, :HEAD_DIM]
    small["k_norm"] = dkg[:, :HEAD_DIM]
    dproj = jnp.concatenate([dqkv, dpu, dpv, dgla, dglb], axis=1)
    big["w_in"] = _dw_cols("dw_in", sv["h"], dproj)
    token = reducer.start(layer, 0, big)
    dh = _dx_cols("dh", [(dproj, w["w_in"], 0)], 1, token)[0]
    token = reducer.scatter(layer, 0, dh)
    dx, dxb, small["mix_norm"] = _rms_bwd("mix_norm_bwd", dh, sv["x"], sp["mix_norm"], dx1, token)
    return dx, dxb, small


def _place():
    x, y, c = lax.axis_index("x"), lax.axis_index("y"), lax.axis_index("c")
    chips = [(1 - x, y), (x, 1 - y), (1 - x, 1 - y)]
    return x, y, c, chips


def _half_rows(c, rows):
    h = rows // 2
    assert h % 16 == 0
    return pl.ds(pl.multiple_of(c * h, 16), h)


def _row_tile(rows, pref):
    best = None
    for t in range(16, min(rows, pref) + 1, 16):
        if rows % t == 0:
            best = t
    assert best is not None, rows
    return best


def _cast_own(name, chip, w, layer):
    _, R, C = w.shape
    tr = _row_tile(R, 256)

    def body(chip_ref, w_ref, o_ref):
        o_ref[...] = w_ref[...].astype(MXU)

    return pl.pallas_call(
        body, name=name, out_shape=jax.ShapeDtypeStruct((N_CHIPS, R, C), MXU),
        grid_spec=pltpu.PrefetchScalarGridSpec(
            num_scalar_prefetch=1, grid=(R // tr,),
            in_specs=[pl.BlockSpec((None, tr, C), lambda i, chip_ref: (layer, i, 0))],
            out_specs=pl.BlockSpec((None, tr, C), lambda i, chip_ref: (chip_ref[0], i, 0))),
        compiler_params=_params())(chip, w)


HBM = pl.BlockSpec(memory_space=pltpu.HBM)
SEM = pl.BlockSpec(memory_space=pltpu.SEMAPHORE)
DATAFLOW = pltpu.SideEffectType.DATAFLOW_SIDE_EFFECTING


def _gather_copies(bufs, send_sem, recv_sem):
    x, y, c, chips = _place()

    def ici(a, j, block):
        px, py = chips[j]
        blk = bufs[a].at[block, _half_rows(c, bufs[a].shape[1])]
        return pltpu.make_async_remote_copy(
            src_ref=blk, dst_ref=blk, send_sem=send_sem.at[3 * a + j], recv_sem=recv_sem.at[3 * a + j],
            device_id=(px, py, c), device_id_type=MESH)

    def d2d(a, j, core):
        px, py = chips[j]
        blk = bufs[a].at[2 * px + py, _half_rows(core, bufs[a].shape[1])]
        return pltpu.make_async_remote_copy(
            src_ref=blk, dst_ref=blk, send_sem=send_sem.at[3 * a + j], recv_sem=recv_sem.at[3 * a + j],
            device_id=(x, y, 1 - c), device_id_type=MESH)

    return ici, d2d


def _in_hbm(bufs):
    return [pltpu.with_memory_space_constraint(b, pltpu.HBM) for b in bufs]


def _gather_start(name, bufs, after):
    n = len(bufs)

    def body(*refs):
        dst = refs[n + 1:2 * n + 1]
        send_sem, recv_sem, token = refs[2 * n + 1:]
        x, y, c, chips = _place()
        ici, _ = _gather_copies(dst, send_sem, recv_sem)
        for a in range(n):
            for j in range(3):
                ici(a, j, 2 * x + y).start()
        token[...] = jnp.zeros(token.shape, token.dtype)

    sems = pltpu.SemaphoreType.DMA((3 * n,))
    outs = pl.pallas_call(
        body, name=name, in_specs=[HBM] * n + [ANY],
        out_specs=[HBM] * n + [SEM, SEM, pl.BlockSpec(memory_space=pltpu.VMEM)],
        out_shape=[pltpu.HBM(b.shape, b.dtype) for b in bufs] + [sems, sems, jax.ShapeDtypeStruct((8, BLOCK), F32)],
        input_output_aliases={a: a for a in range(n)},
        compiler_params=pltpu.CompilerParams(has_side_effects=DATAFLOW))(*_in_hbm(bufs), after)
    return outs[:n], outs[n], outs[n + 1], outs[n + 2]


def _gather_forward(name, bufs, ici_send, ici_recv, after):
    n = len(bufs)

    def body(*refs):
        ici_send_ref, ici_recv_ref = refs[n], refs[n + 1]
        dst = refs[n + 3:2 * n + 3]
        d2d_send, d2d_recv = refs[2 * n + 3:]
        x, y, c, chips = _place()
        ici, _ = _gather_copies(dst, ici_send_ref, ici_recv_ref)
        _, d2d = _gather_copies(dst, d2d_send, d2d_recv)
        for a in range(n):
            for j, (px, py) in enumerate(chips):
                ici(a, j, 2 * px + py).wait_recv()
                d2d(a, j, c).start()
        for a in range(n):
            for j in range(3):
                ici(a, j, 2 * x + y).wait_send()

    sems = pltpu.SemaphoreType.DMA((3 * n,))
    outs = pl.pallas_call(
        body, name=name, in_specs=[HBM] * n + [SEM, SEM, ANY], out_specs=[HBM] * n + [SEM, SEM],
        out_shape=[pltpu.HBM(b.shape, b.dtype) for b in bufs] + [sems, sems],
        input_output_aliases={a: a for a in range(n)},
        compiler_params=pltpu.CompilerParams(has_side_effects=DATAFLOW))(*bufs, ici_send, ici_recv, after)
    return outs[:n], outs[n], outs[n + 1]


def _gather_finish(name, bufs, d2d_send, d2d_recv, after):
    n = len(bufs)

    def body(*refs):
        send_ref, recv_ref = refs[n], refs[n + 1]
        dst = refs[n + 3:]
        x, y, c, chips = _place()
        _, d2d = _gather_copies(dst, send_ref, recv_ref)
        for a in range(n):
            for j in range(3):
                d2d(a, j, 1 - c).wait_recv()
                d2d(a, j, c).wait_send()

    return pl.pallas_call(
        body, name=name, in_specs=[HBM] * n + [SEM, SEM, ANY], out_specs=[HBM] * n,
        out_shape=[pltpu.HBM(b.shape, b.dtype) for b in bufs],
        input_output_aliases={a: a for a in range(n)},
        compiler_params=pltpu.CompilerParams(has_side_effects=DATAFLOW))(*bufs, d2d_send, d2d_recv, after)


GATHER = (("w_in",), ("w_attn_branch", "w_sgu_branch", "w_out"), ("w_gate", "w_up"), ("w_down",))
REDUCE = (("w_in",), ("w_attn_branch", "w_sgu_branch", "w_out"), ("w_gate", "w_up", "w_down"))


class _WeightStream:
    def __init__(self, started):
        self.started, self.passed = started, {}

    def forward(self, layer, group, after):
        bufs, send, recv = self.started[(layer, group)]
        self.passed[(layer, group)] = _gather_forward("gather_forward_%d_%d" % (layer, group), bufs, send, recv, after)

    def finish(self, layer, group, after):
        bufs, send, recv = self.passed[(layer, group)]
        done = _gather_finish("gather_finish_%d_%d" % (layer, group), bufs, send, recv, after)
        return dict(zip(GATHER[group], done))


def _pair_copies(grads, lands, send_sem, recv_sem):
    x, y, c, _ = _place()

    def make(a):
        theirs = _half_rows(1 - c, grads[a].shape[1])
        return pltpu.make_async_remote_copy(
            src_ref=grads[a].at[:, theirs], dst_ref=lands[a], send_sem=send_sem.at[a], recv_sem=recv_sem.at[a],
            device_id=(x, y, 1 - c), device_id_type=MESH)

    return make


def _pair_start(name, grads, after):
    n = len(grads)
    lands = [lax.empty((g.shape[0], g.shape[1] // 2, g.shape[2]), g.dtype) for g in grads]

    def body(*refs):
        src, dst = refs[2 * n + 1:3 * n + 1], refs[3 * n + 1:4 * n + 1]
        send_sem, recv_sem, token = refs[4 * n + 1:]
        copy = _pair_copies(src, dst, send_sem, recv_sem)
        for a in range(n):
            copy(a).start()
        token[...] = jnp.zeros(token.shape, token.dtype)

    sems = pltpu.SemaphoreType.DMA((n,))
    outs = pl.pallas_call(
        body, name=name, in_specs=[HBM] * (2 * n) + [ANY],
        out_specs=[HBM] * (2 * n) + [SEM, SEM, pl.BlockSpec(memory_space=pltpu.VMEM)],
        out_shape=[pltpu.HBM(b.shape, b.dtype) for b in grads + lands] + [sems, sems, jax.ShapeDtypeStruct((8, BLOCK), F32)],
        input_output_aliases={a: a for a in range(2 * n)},
        compiler_params=pltpu.CompilerParams(has_side_effects=DATAFLOW))(*_in_hbm(grads + lands), after)
    return outs[:n], outs[n:2 * n], outs[2 * n], outs[2 * n + 1], outs[2 * n + 2]


def _pair_finish(name, grads, lands, send_sem, recv_sem, after):
    n = len(grads)

    def body(*refs):
        send_ref, recv_ref = refs[2 * n], refs[2 * n + 1]
        src, dst = refs[2 * n + 3:3 * n + 3], refs[3 * n + 3:]
        copy = _pair_copies(src, dst, send_ref, recv_ref)
        for a in range(n):
            copy(a).wait_send()
            copy(a).wait_recv()

    outs = pl.pallas_call(
        body, name=name, in_specs=[HBM] * (2 * n) + [SEM, SEM, ANY], out_specs=[HBM] * (2 * n),
        out_shape=[pltpu.HBM(b.shape, b.dtype) for b in grads + lands],
        input_output_aliases={a: a for a in range(2 * n)},
        compiler_params=pltpu.CompilerParams(has_side_effects=DATAFLOW))(*grads, *lands, send_sem, recv_sem, after)
    return outs[:n], outs[n:]


def _pair_sum(name, core, g, p):
    _, h, C = p.shape
    tr = _row_tile(h, 256)
    nrb = h // tr

    def body(core_ref, g_ref, p_ref, o_ref):
        o_ref[...] = (g_ref[...].astype(F32) + p_ref[...].astype(F32)).astype(o_ref.dtype)

    spec = pl.BlockSpec((None, tr, C), lambda s, i, core_ref: (s, i, 0))
    return pl.pallas_call(
        body, name=name, out_shape=jax.ShapeDtypeStruct(p.shape, p.dtype),
        grid_spec=pltpu.PrefetchScalarGridSpec(
            num_scalar_prefetch=1, grid=(N_CHIPS, nrb),
            in_specs=[pl.BlockSpec((None, tr, C), lambda s, i, core_ref: (s, core_ref[0] * nrb + i, 0)), spec],
            out_specs=spec),
        compiler_params=_params())(core, g, p)


def _scatter_copies(sums, slots, send_sem, recv_sem):
    x, y, c, chips = _place()

    def make(a, j):
        px, py = chips[j]
        return pltpu.make_async_remote_copy(
            src_ref=sums[a].at[2 * px + py], dst_ref=slots[a].at[j], send_sem=send_sem.at[3 * a + j],
            recv_sem=recv_sem.at[3 * a + j], device_id=(px, py, c), device_id_type=MESH)

    return make


def _scatter_start(name, sums, after):
    n = len(sums)
    slots = [lax.empty((3,) + s.shape[1:], s.dtype) for s in sums]

    def body(*refs):
        src, dst = refs[2 * n + 1:3 * n + 1], refs[3 * n + 1:4 * n + 1]
        send_sem, recv_sem, token = refs[4 * n + 1:]
        copy = _scatter_copies(src, dst, send_sem, recv_sem)
        for a in range(n):
            for j in range(3):
                copy(a, j).start()
        token[...] = jnp.zeros(token.shape, token.dtype)

    sems = pltpu.SemaphoreType.DMA((3 * n,))
    outs = pl.pallas_call(
        body, name=name, in_specs=[HBM] * (2 * n) + [ANY],
        out_specs=[HBM] * (2 * n) + [SEM, SEM, pl.BlockSpec(memory_space=pltpu.VMEM)],
        out_shape=[pltpu.HBM(b.shape, b.dtype) for b in sums + slots] + [sems, sems, jax.ShapeDtypeStruct((8, BLOCK), F32)],
        input_output_aliases={a: a for a in range(2 * n)},
        compiler_params=pltpu.CompilerParams(has_side_effects=DATAFLOW))(*_in_hbm(sums + slots), after)
    return outs[:n], outs[n:2 * n], outs[2 * n], outs[2 * n + 1], outs[2 * n + 2]


def _scatter_finish(name, sums, slots, send_sem, recv_sem, after):
    n = len(sums)

    def body(*refs):
        send_ref, recv_ref = refs[2 * n], refs[2 * n + 1]
        src, dst = refs[2 * n + 3:3 * n + 3], refs[3 * n + 3:]
        copy = _scatter_copies(src, dst, send_ref, recv_ref)
        for a in range(n):
            for j in range(3):
                copy(a, j).wait_send()
                copy(a, j).wait_recv()

    outs = pl.pallas_call(
        body, name=name, in_specs=[HBM] * (2 * n) + [SEM, SEM, ANY], out_specs=[HBM] * (2 * n),
        out_shape=[pltpu.HBM(b.shape, b.dtype) for b in sums + slots],
        input_output_aliases={a: a for a in range(2 * n)},
        compiler_params=pltpu.CompilerParams(has_side_effects=DATAFLOW))(*sums, *slots, send_sem, recv_sem, after)
    return outs[:n], outs[n:]


def _slot_sum(name, place, slots, sums):
    _, h, C = slots.shape
    tr = _row_tile(h, 256)
    nrb = h // tr

    def body(place_ref, r0, r1, r2, own, o_ref):
        o_ref[...] = ((r0[...].astype(F32) + r1[...].astype(F32)) + r2[...].astype(F32)) + own[...].astype(F32)

    slot = lambda k: pl.BlockSpec((None, tr, C), lambda i, place_ref: (k, i, 0))
    return pl.pallas_call(
        body, name=name, out_shape=jax.ShapeDtypeStruct((2 * h, C), F32),
        grid_spec=pltpu.PrefetchScalarGridSpec(
            num_scalar_prefetch=1, grid=(nrb,),
            in_specs=[slot(0), slot(1), slot(2),
                      pl.BlockSpec((None, tr, C), lambda i, place_ref: (place_ref[0], i, 0))],
            out_specs=pl.BlockSpec((tr, C), lambda i, place_ref: (place_ref[1] * nrb + i, 0))),
        compiler_params=_params())(place, slots, slots, slots, sums)


def _half_exchange(name, bufs):
    n = len(bufs)

    def body(*refs):
        dst = refs[n:2 * n]
        send_sem, recv_sem = refs[2 * n:]
        x, y, c, _ = _place()
        copies = []
        for a in range(n):
            mine = dst[a].at[_half_rows(c, dst[a].shape[0])]
            copies.append(pltpu.make_async_remote_copy(
                src_ref=mine, dst_ref=mine, send_sem=send_sem.at[a], recv_sem=recv_sem.at[a],
                device_id=(x, y, 1 - c), device_id_type=MESH))
            copies[-1].start()
        for cp in copies:
            cp.wait()

    return pl.pallas_call(
        body, name=name, in_specs=[ANY] * n, out_specs=[ANY] * n,
        out_shape=[jax.ShapeDtypeStruct(b.shape, b.dtype) for b in bufs],
        input_output_aliases={a: a for a in range(n)},
        scratch_shapes=[pltpu.SemaphoreType.DMA((n,))] * 2)(*bufs)


class _GradReducer:
    def __init__(self, chip, core):
        self.core, self.place, self.pairs, self.started = core, jnp.concatenate([chip, core]), {}, []

    def start(self, layer, group, grads):
        mine = [grads[n] for n in REDUCE[group]]
        mine, lands, send, recv, token = _pair_start("grad_pair_start_%d_%d" % (layer, group), mine, self.place)
        self.pairs[(layer, group)] = (mine, lands, send, recv)
        return token

    def scatter(self, layer, group, after):
        tag = "%d_%d" % (layer, group)
        names = REDUCE[group]
        mine, lands, send, recv = self.pairs.pop((layer, group))
        mine, theirs = _pair_finish("grad_pair_finish_" + tag, mine, lands, send, recv, after)
        sums = [_pair_sum("pair_sum_%s_%d" % (n, layer), self.core, g, p) for n, g, p in zip(names, mine, theirs)]
        sums, slots, send, recv, token = _scatter_start("grad_scatter_start_" + tag, sums, self.place)
        self.started.append((layer, names, sums, slots, send, recv))
        return token

    def finish(self, after, update):
        for layer in sorted({entry[0] for entry in self.started}, reverse=True):
            halves, keys = [], []
            for lyr, names, sums, slots, send, recv in self.started:
                if lyr != layer:
                    continue
                sums, slots = _scatter_finish("grad_scatter_finish_%s_%d" % (names[0], layer), sums, slots, send, recv,
                                              after)
                for n, r, s in zip(names, slots, sums):
                    halves.append(_slot_sum("slot_sum_%s_%d" % (n, layer), self.place, r, s))
                    keys.append(n)
                after = halves[-1]
            whole = _half_exchange("grad_half_exchange_%d" % layer, halves)
            after = update(layer, dict(zip(keys, whole)))


def _all_reduce_small(v):
    rows = v.shape[0]
    n_dev = 2 * N_CHIPS

    def body(x_ref, out_ref, gat_ref, send_sems, recv_sems, local_sem):
        x, y, c, chips = _place()
        me, sibling = (x, y, c), (x, y, 1 - c)

        def slot(px, py, pc):
            return gat_ref.at[4 * px + 2 * py + pc]

        def copy(k, block, to, src=None):
            return pltpu.make_async_remote_copy(
                src_ref=slot(*block) if src is None else src, dst_ref=slot(*block), send_sem=send_sems.at[k],
                recv_sem=recv_sems.at[k], device_id=to, device_id_type=MESH)

        mine = pltpu.make_async_copy(x_ref, slot(*me), local_sem)
        mine.start()
        first = [copy(0, me, sibling, src=x_ref)]
        first += [copy(1 + j, me, (*chip, c), src=x_ref) for j, chip in enumerate(chips)]
        for cp in first:
            cp.start()
        passed = [copy(4 + j, (*chip, c), sibling) for j, chip in enumerate(chips)]
        for j, chip in enumerate(chips):
            copy(1 + j, (*chip, c), me).wait_recv()
            passed[j].start()
        copy(0, sibling, me).wait_recv()
        for j, chip in enumerate(chips):
            copy(4 + j, (*chip, 1 - c), me).wait_recv()
        for cp in first + passed:
            cp.wait_send()
        mine.wait()
        acc = gat_ref[0]
        for d in range(1, n_dev):
            acc = acc + gat_ref[d]
        out_ref[...] = acc

    vm = pl.BlockSpec(memory_space=pltpu.VMEM)
    return pl.pallas_call(
        body, name="small_grad_all_reduce", in_specs=[vm], out_specs=vm,
        out_shape=jax.ShapeDtypeStruct(v.shape, F32),
        scratch_shapes=[pltpu.VMEM((n_dev, rows, BLOCK), F32), pltpu.SemaphoreType.DMA((7,)),
                        pltpu.SemaphoreType.DMA((7,)), pltpu.SemaphoreType.DMA],
        compiler_params=_params())(v)


def _adamw_math(w, g, m, v):
    m2 = ADAM_B1 * m + (1.0 - ADAM_B1) * g
    v2 = ADAM_B2 * v + (1.0 - ADAM_B2) * (g * g)
    m_hat = m2 / (1.0 - ADAM_B1 ** ADAM_STEP)
    v_hat = v2 / (1.0 - ADAM_B2 ** ADAM_STEP)
    delta = -ADAM_LR * (m_hat / (jnp.sqrt(v_hat) + ADAM_EPS) + ADAM_WD * w)
    return delta, m2, v2


def _adamw_big(name, layer, grad, w, m, v, others):
    L, R, C = w.shape
    tr = _row_tile(R, 128)

    def body(g_ref, w_ref, m_ref, v_ref, *rest):
        go_ref, d_ref, mo_ref, vo_ref = rest[-4:]
        g = g_ref[...]
        delta, m2, v2 = _adamw_math(w_ref[...], g, m_ref[...], v_ref[...])
        go_ref[...] = g
        d_ref[...] = delta
        mo_ref[...] = m2
        vo_ref[...] = v2

    blk = pl.BlockSpec((None, tr, C), lambda i: (layer, i, 0))
    shp = jax.ShapeDtypeStruct(w.shape, F32)
    others = [] if others is None else list(others)
    return pl.pallas_call(
        body, name=name, grid=(R // tr,),
        in_specs=[pl.BlockSpec((tr, C), lambda i: (i, 0))] + [blk] * 3 + [ANY] * len(others), out_specs=[blk] * 4,
        out_shape=[shp] * 4, input_output_aliases={4 + k: k for k in range(len(others))},
        compiler_params=_params())(grad, w, m, v, *others)


def _adamw_small(g, w, m, v):
    rows = g.shape[0]
    tr = _row_tile(rows, 512)

    def body(g_ref, w_ref, m_ref, v_ref, d_ref, mo_ref, vo_ref):
        delta, m2, v2 = _adamw_math(w_ref[...], g_ref[...], m_ref[...], v_ref[...])
        d_ref[...] = delta
        mo_ref[...] = m2
        vo_ref[...] = v2

    blk = pl.BlockSpec((tr, BLOCK), lambda i: (i, 0))
    shp = jax.ShapeDtypeStruct(g.shape, F32)
    return pl.pallas_call(
        body, name="adamw_small", grid=(rows // tr,), in_specs=[blk] * 4, out_specs=[blk] * 3, out_shape=[shp] * 3,
        compiler_params=_params())(g, w, m, v)


def _pack(arrays):
    flat = jnp.concatenate([a.reshape(-1) for a in arrays])
    pad = (-flat.shape[0]) % (16 * BLOCK)
    return jnp.pad(flat, (0, pad)).reshape(-1, BLOCK)


def _unpack(packed, like):
    flat = packed.reshape(-1)
    out, off = [], 0
    for a in like:
        out.append(flat[off:off + a.size].reshape(a.shape))
        off += a.size
    return out


BIG = ("w_in", "w_attn_branch", "w_sgu_branch", "w_out", "w_gate", "w_up", "w_down")
SMALL = ("mix_norm", "q_norm", "k_norm", "sinks", "sgu_ln_g", "sgu_ln_b", "w_spatial", "b_spatial", "ffn_norm")
ORDER = ("mix_norm", "w_in", "q_norm", "k_norm", "sinks", "sgu_ln_g", "sgu_ln_b", "w_spatial", "b_spatial",
         "w_attn_branch", "w_sgu_branch", "w_out", "ffn_norm", "w_gate", "w_up", "w_down")


def _rope_tables(seq):
    pos = jnp.arange(seq, dtype=F32)
    inv_freq = jnp.power(10000.0, -jnp.arange(0, HEAD_DIM, 2, dtype=F32) / HEAD_DIM)
    ang = pos[:, None] * inv_freq[None, :]
    cos, sin = jnp.cos(ang), jnp.sin(ang)
    reps = BLOCK // HEAD_DIM
    return (jnp.tile(jnp.concatenate([cos, cos], axis=1), (1, reps)),
            jnp.tile(jnp.concatenate([-sin, sin], axis=1), (1, reps)))


def kernel(x, mix_norm, w_in, q_norm, k_norm, sinks, sgu_ln_g, sgu_ln_b, w_spatial, b_spatial, w_attn_branch, w_sgu_branch, w_out, ffn_norm, w_gate, w_up, w_down, loss_target, m_mix_norm, m_w_in, m_q_norm, m_k_norm, m_sinks, m_sgu_ln_g, m_sgu_ln_b, m_w_spatial, m_b_spatial, m_w_attn_branch, m_w_sgu_branch, m_w_out, m_ffn_norm, m_w_gate, m_w_up, m_w_down, v_mix_norm, v_w_in, v_q_norm, v_k_norm, v_sinks, v_sgu_ln_g, v_sgu_ln_b, v_w_spatial, v_b_spatial, v_w_attn_branch, v_w_sgu_branch, v_w_out, v_ffn_norm, v_w_gate, v_w_up, v_w_down):
    weights = dict(mix_norm=mix_norm, w_in=w_in, q_norm=q_norm, k_norm=k_norm, sinks=sinks, sgu_ln_g=sgu_ln_g,
                   sgu_ln_b=sgu_ln_b, w_spatial=w_spatial, b_spatial=b_spatial, w_attn_branch=w_attn_branch,
                   w_sgu_branch=w_sgu_branch, w_out=w_out, ffn_norm=ffn_norm, w_gate=w_gate, w_up=w_up, w_down=w_down)
    mom1 = dict(mix_norm=m_mix_norm, w_in=m_w_in, q_norm=m_q_norm, k_norm=m_k_norm, sinks=m_sinks,
                sgu_ln_g=m_sgu_ln_g, sgu_ln_b=m_sgu_ln_b, w_spatial=m_w_spatial, b_spatial=m_b_spatial,
                w_attn_branch=m_w_attn_branch, w_sgu_branch=m_w_sgu_branch, w_out=m_w_out, ffn_norm=m_ffn_norm,
                w_gate=m_w_gate, w_up=m_w_up, w_down=m_w_down)
    mom2 = dict(mix_norm=v_mix_norm, w_in=v_w_in, q_norm=v_q_norm, k_norm=v_k_norm, sinks=v_sinks,
                sgu_ln_g=v_sgu_ln_g, sgu_ln_b=v_sgu_ln_b, w_spatial=v_w_spatial, b_spatial=v_b_spatial,
                w_attn_branch=v_w_attn_branch, w_sgu_branch=v_w_sgu_branch, w_out=v_w_out, ffn_norm=v_ffn_norm,
                w_gate=v_w_gate, w_up=v_w_up, w_down=v_w_down)
    xs, target = x[0], loss_target[0]
    S, D = xs.shape
    L = w_in.shape[0]
    AW, KW, SW = N_Q_HEADS * HEAD_DIM, N_KV_HEADS * HEAD_DIM, SGU_GROUPS * BLOCK
    dims = (AW, KW, AW + 2 * KW + 2 * SW, AW + 2 * KW)
    cos, sin = _rope_tables(S)
    reps = BLOCK // HEAD_DIM

    chip = (2 * lax.axis_index("x") + lax.axis_index("y")).astype(jnp.int32).reshape(1)
    core = lax.axis_index("c").astype(jnp.int32).reshape(1)
    started, token = {}, chip
    for l in range(L):
        for gi, names in enumerate(GATHER):
            bufs = [_cast_own("cast_%s_%d" % (n, l), chip, weights[n], l) for n in names]
            bufs, send, recv, token = _gather_start("gather_start_%d_%d" % (l, gi), bufs, token)
            started[(l, gi)] = (bufs, send, recv)
    stream = _WeightStream(started)
    stream.forward(0, 0, token)
    sp = [dict(mix_norm=mix_norm[l][None], ffn_norm=ffn_norm[l][None], q_norm=jnp.tile(q_norm[l][None], (1, reps)),
               k_norm=jnp.tile(k_norm[l][None], (1, reps)), sinks=sinks[l][None], sgu_ln_g=sgu_ln_g[l][None],
               sgu_ln_b=sgu_ln_b[l][None], w_spatial=w_spatial[l], b_spatial=b_spatial[l][:, :, None])
          for l in range(L)]

    act, saved, wl = xs, [], []
    for l in range(L):
        act, sv, w_all = _layer_fwd(act, stream, l, l == L - 1, sp[l], cos, sin, dims)
        saved.append(sv)
        wl.append(w_all)
    loss_part, dy, dyb = _loss_head(act, target)
    loss = lax.psum(loss_part[0, 0], ("x", "y", "c"))

    reducer = _GradReducer(chip, core)
    small_g = [None] * L
    for l in reversed(range(L)):
        dy, dyb, small_g[l] = _layer_bwd(dy, dyb, wl[l], sp[l], saved[l], cos, sin, dims, reducer, l)
    grad_x = dy[None]

    updated = {}

    def update(layer, reduced):
        for n in BIG:
            updated[n] = _adamw_big("adamw_%s_%d" % (n, layer), layer, reduced[n], weights[n], mom1[n], mom2[n],
                                    updated.get(n))
        return updated[BIG[-1]][0]

    reducer.finish(dy, update)
    grads, deltas, new_m, new_v = {}, {}, {}, {}
    for n in BIG:
        grads[n], deltas[n], new_m[n], new_v[n] = updated[n]

    small_like = [weights[n] for n in SMALL]
    local = [jnp.stack([small_g[l][n].reshape(weights[n].shape[1:]) for l in range(L)]) for n in SMALL]
    g_small = _all_reduce_small(_pack(local))
    d_small, m_small, v_small = _adamw_small(g_small, _pack(small_like), _pack([mom1[n] for n in SMALL]),
                                             _pack([mom2[n] for n in SMALL]))
    for n, g, d, m2, v2 in zip(SMALL, _unpack(g_small, small_like), _unpack(d_small, small_like),
                               _unpack(m_small, small_like), _unpack(v_small, small_like)):
        grads[n], deltas[n], new_m[n], new_v[n] = g, d, m2, v2

    return (loss, grad_x, *[grads[n] for n in ORDER], *[deltas[n] for n in ORDER],
            *[new_m[n] for n in ORDER], *[new_v[n] for n in ORDER])
```

```python
import functools

import jax
import jax.numpy as jnp
from jax import lax
from jax.experimental import pallas as pl
from jax.experimental.pallas import tpu as pltpu

HEAD_DIM = 64
N_Q_HEADS = 16
N_KV_HEADS = 4
SGU_GROUPS = 8
BLOCK = 128
EPS = 1e-6
ADAM_LR = 0.001
ADAM_B1 = 0.9
ADAM_B2 = 0.999
ADAM_EPS = 1e-08
ADAM_WD = 0.01
ADAM_STEP = 10
N_CHIPS = 4
VMEM_LIMIT = 52 * 1024 * 1024

F32 = jnp.float32
MXU = jnp.bfloat16
NN = (((1,), (0,)), ((), ()))
NT = (((1,), (1,)), ((), ()))
TN = (((0,), (0,)), ((), ()))
MESH = pl.DeviceIdType.MESH
ANY = pl.BlockSpec(memory_space=pl.ANY)


def _tile(n, pref):
    if n <= pref:
        return n
    best = None
    for t in range(BLOCK, pref + 1, BLOCK):
        if n % t == 0:
            best = t
    assert best is not None, (n, pref)
    return best


def _params():
    return pltpu.CompilerParams(vmem_limit_bytes=VMEM_LIMIT)


def _mm(name, grid, n_red, operands, specs, pairs, dims, n_extra, out_shapes, out_specs,
        acc_shapes, epilogue, after=None):
    n_op = len(operands) - n_extra
    n_out = len(out_shapes)
    n_acc = len(acc_shapes)
    if after is not None:
        operands, specs = list(operands) + [after], list(specs) + [ANY]
    n_in = len(operands)
    axes = [ax for ax in range(len(grid) - n_red, len(grid)) if grid[ax] > 1]

    def body(*refs):
        ops = refs[:n_op]
        extra = refs[n_op:n_op + n_extra]
        outs = refs[n_in:n_in + n_out]
        accs = refs[n_in + n_out:]

        def prod(a, b):
            return lax.dot_general(ops[a][...], ops[b][...], dims, preferred_element_type=F32)

        if not axes:
            vals = [None] * n_acc
            for a, b, k in pairs:
                d = prod(a, b)
                vals[k] = d if vals[k] is None else vals[k] + d
            epilogue(vals, extra, outs)
        else:
            first = pl.program_id(axes[0]) == 0
            last = pl.program_id(axes[0]) == grid[axes[0]] - 1
            for ax in axes[1:]:
                first = jnp.logical_and(first, pl.program_id(ax) == 0)
                last = jnp.logical_and(last, pl.program_id(ax) == grid[ax] - 1)

            @pl.when(first)
            def _():
                for acc in accs:
                    acc[...] = jnp.zeros(acc.shape, F32)

            for a, b, k in pairs:
                accs[k][...] += prod(a, b)

            @pl.when(last)
            def _():
                epilogue([acc[...] for acc in accs], extra, outs)

    scratch = [pltpu.VMEM(s, F32) for s in acc_shapes] if axes else []
    return pl.pallas_call(
        body, name=name, grid=grid, in_specs=specs, out_specs=out_specs, out_shape=out_shapes,
        scratch_shapes=scratch, compiler_params=_params())(*operands)


def _sigmoid(x):
    return 1.0 / (1.0 + jnp.exp(-x))


_GELU_C = 0.7978845608028654
_GELU_A = 0.044715


def _gelu(x):
    return 0.5 * x * (1.0 + jnp.tanh(_GELU_C * (x + _GELU_A * x * x * x)))


def _gelu_grad(x):
    t = jnp.tanh(_GELU_C * (x + _GELU_A * x * x * x))
    return 0.5 * (1.0 + t) + 0.5 * x * (1.0 - t * t) * _GELU_C * (1.0 + 3.0 * _GELU_A * x * x)


def _rms_fwd(name, x, g):
    S, D = x.shape
    tr = _tile(S, 256)

    def body(x_ref, g_ref, o_ref):
        xv = x_ref[...]
        r = lax.rsqrt(jnp.mean(xv * xv, axis=-1, keepdims=True) + EPS)
        o_ref[...] = (xv * r * g_ref[...]).astype(MXU)

    return pl.pallas_call(
        body, name=name, grid=(S // tr,),
        in_specs=[pl.BlockSpec((tr, D), lambda i: (i, 0)), pl.BlockSpec((1, D), lambda i: (0, 0))],
        out_specs=pl.BlockSpec((tr, D), lambda i: (i, 0)),
        out_shape=jax.ShapeDtypeStruct((S, D), MXU), compiler_params=_params())(x, g)


def _rms_bwd(name, dh, x, g, dres, after):
    S, D = x.shape
    tr = _tile(S, 256)

    def body(dh_ref, x_ref, g_ref, dres_ref, after_ref, dx_ref, dxb_ref, dg_ref):
        xv = x_ref[...]
        r = lax.rsqrt(jnp.mean(xv * xv, axis=-1, keepdims=True) + EPS)
        xh = xv * r
        dhv = dh_ref[...]
        dy = dhv * g_ref[...]
        dx = dres_ref[...] + r * (dy - xh * jnp.mean(dy * xh, axis=-1, keepdims=True))
        dx_ref[...] = dx
        dxb_ref[...] = dx.astype(MXU)

        @pl.when(pl.program_id(0) == 0)
        def _():
            dg_ref[...] = jnp.zeros(dg_ref.shape, F32)

        dg_ref[...] += jnp.sum(dhv * xh, axis=0, keepdims=True)

    row = pl.BlockSpec((tr, D), lambda i: (i, 0))
    vec = pl.BlockSpec((1, D), lambda i: (0, 0))
    return pl.pallas_call(
        body, name=name, grid=(S // tr,), in_specs=[row, row, vec, row, ANY], out_specs=[row, row, vec],
        out_shape=[jax.ShapeDtypeStruct((S, D), F32), jax.ShapeDtypeStruct((S, D), MXU),
                   jax.ShapeDtypeStruct((1, D), F32)],
        compiler_params=_params())(dh, x, g, dres, after)


def _loss_head(y, target):
    S, D = y.shape
    tr = _tile(S, 256)

    def body(y_ref, t_ref, loss_ref, dy_ref, dyb_ref):
        d = y_ref[...] - t_ref[...]
        dy = d * (1.0 / D)
        dy_ref[...] = dy
        dyb_ref[...] = dy.astype(MXU)

        @pl.when(pl.program_id(0) == 0)
        def _():
            loss_ref[...] = jnp.zeros(loss_ref.shape, F32)

        loss_ref[...] += (0.5 / D) * jnp.sum(jnp.sum(d * d, axis=-1, keepdims=True), axis=0, keepdims=True)

    row = pl.BlockSpec((tr, D), lambda i: (i, 0))
    return pl.pallas_call(
        body, name="loss_head", grid=(S // tr,), in_specs=[row, row],
        out_specs=[pl.BlockSpec((1, 1), lambda i: (0, 0)), row, row],
        out_shape=[jax.ShapeDtypeStruct((1, 1), F32), jax.ShapeDtypeStruct((S, D), F32),
                   jax.ShapeDtypeStruct((S, D), MXU)],
        compiler_params=_params())(y, target)


def _head_sum(v):
    r = lax.broadcasted_iota(jnp.int32, (BLOCK, BLOCK), 0) // HEAD_DIM
    c = lax.broadcasted_iota(jnp.int32, (BLOCK, BLOCK), 1) // HEAD_DIM
    ones = jnp.where(r == c, 1.0, 0.0).astype(jnp.bfloat16)
    hi = v.astype(jnp.bfloat16)
    lo = (v - hi.astype(F32)).astype(jnp.bfloat16)
    parts = []
    for t in range(v.shape[1] // BLOCK):
        sl = slice(t * BLOCK, (t + 1) * BLOCK)
        parts.append(jnp.dot(hi[:, sl], ones, preferred_element_type=F32)
                     + jnp.dot(lo[:, sl], ones, preferred_element_type=F32))
    return parts[0] if len(parts) == 1 else jnp.concatenate(parts, axis=-1)


def _swap_halves(v):
    w = v.shape[1]
    half = HEAD_DIM // 2
    lane = lax.broadcasted_iota(jnp.int32, v.shape, 1) % HEAD_DIM
    return jnp.where(lane < half, pltpu.roll(v, w - half, 1), pltpu.roll(v, half, 1))


def _norm_rope(xv, gain, cos, sin):
    r = lax.rsqrt(_head_sum(xv * xv) * (1.0 / HEAD_DIM) + EPS)
    xn = xv * r * gain
    return xn * cos + _swap_halves(xn) * sin


def _norm_rope_bwd(dy, xv, gain, cos, sin):
    r = lax.rsqrt(_head_sum(xv * xv) * (1.0 / HEAD_DIM) + EPS)
    xh = xv * r
    dxn = dy * cos + _swap_halves(dy * sin)
    dgain = jnp.sum(dxn * xh, axis=0, keepdims=True)
    dxh = dxn * gain
    dx = r * (dxh - xh * (_head_sum(dxh * xh) * (1.0 / HEAD_DIM)))
    return dx, dgain


def _fold_heads(v):
    acc = v[:, 0:BLOCK]
    for t in range(1, v.shape[1] // BLOCK):
        acc = acc + v[:, t * BLOCK:(t + 1) * BLOCK]
    return acc + pltpu.roll(acc, HEAD_DIM, 1)


def _tile_lanes(v, width):
    return v if width == BLOCK else jnp.tile(v, (1, width // BLOCK))


def _low_half(rows):
    assert BLOCK == 2 * HEAD_DIM
    return lax.broadcasted_iota(jnp.int32, (rows, BLOCK), 1) < HEAD_DIM


def _spread_heads(v):
    low = _low_half(v.shape[0])
    out = []
    for t in range(v.shape[1] // BLOCK):
        tile = v[:, t * BLOCK:(t + 1) * BLOCK]
        swapped = pltpu.roll(tile, HEAD_DIM, 1)
        out += [jnp.where(low, tile, swapped), jnp.where(low, swapped, tile)]
    return jnp.concatenate(out, axis=-1)


def _gather_heads(v):
    low = _low_half(v.shape[0])
    out = []
    for t in range(v.shape[1] // (2 * BLOCK)):
        a, b = v[:, 2 * t * BLOCK:(2 * t + 1) * BLOCK], v[:, (2 * t + 1) * BLOCK:(2 * t + 2) * BLOCK]
        out.append(jnp.where(low, a + pltpu.roll(a, HEAD_DIM, 1), b + pltpu.roll(b, HEAD_DIM, 1)))
    return out[0] if len(out) == 1 else jnp.concatenate(out, axis=-1)


def _qk_prep(proj, qg, kg, cos, sin, AW, KW):
    S = proj.shape[0]
    tr = _tile(S, 256)
    scale = HEAD_DIM ** -0.5

    def body(q_ref, k_ref, v_ref, qg_ref, kg_ref, cos_ref, sin_ref, qo_ref, ko_ref, vo_ref):
        c, s = cos_ref[...], sin_ref[...]
        q = _norm_rope(q_ref[...], _tile_lanes(qg_ref[...], AW), _tile_lanes(c, AW), _tile_lanes(s, AW))
        k = _norm_rope(k_ref[...], _tile_lanes(kg_ref[...], KW), _tile_lanes(c, KW), _tile_lanes(s, KW))
        qo_ref[...] = (q * scale).astype(MXU)
        ko_ref[...] = _spread_heads(k).astype(MXU)
        vo_ref[...] = _spread_heads(v_ref[...]).astype(MXU)

    assert AW % KW == 0
    vec = pl.BlockSpec((1, BLOCK), lambda i: (0, 0))
    tab = pl.BlockSpec((tr, BLOCK), lambda i: (i, 0))
    wide = pl.BlockSpec((tr, 2 * KW), lambda i: (i, 0))
    return pl.pallas_call(
        body, name="qk_prep", grid=(S // tr,),
        in_specs=[pl.BlockSpec((tr, AW), lambda i: (i, 0)),
                  pl.BlockSpec((tr, KW), lambda i: (i, AW // KW)),
                  pl.BlockSpec((tr, KW), lambda i: (i, AW // KW + 1)), vec, vec, tab, tab],
        out_specs=[pl.BlockSpec((tr, AW), lambda i: (i, 0)), wide, wide],
        out_shape=[jax.ShapeDtypeStruct((S, AW), MXU), jax.ShapeDtypeStruct((S, 2 * KW), MXU),
                   jax.ShapeDtypeStruct((S, 2 * KW), MXU)],
        compiler_params=_params())(proj, proj, proj, qg, kg, cos, sin)


def _stack_heads(x, g, qpk):
    low = _low_half(BLOCK)
    parts = []
    for j in range(qpk):
        h = g * qpk + j
        tile = x[:, (h // 2) * BLOCK:(h // 2 + 1) * BLOCK]
        parts.append(jnp.where(low if h % 2 == 0 else jnp.logical_not(low), tile, jnp.zeros_like(tile)))
    return jnp.concatenate(parts, axis=0)


def _unstack_heads(y, qpk):
    low = _low_half(BLOCK)
    tiles = [jnp.where(low, y[2 * t * BLOCK:(2 * t + 1) * BLOCK], y[(2 * t + 1) * BLOCK:(2 * t + 2) * BLOCK])
             for t in range(qpk // 2)]
    return tiles[0] if len(tiles) == 1 else jnp.concatenate(tiles, axis=-1)


def _attn_probs(n, q, kp, kc, g, sink_ref, qpk):
    kcat = jnp.concatenate([kp[:, g * BLOCK:(g + 1) * BLOCK], kc[:, g * BLOCK:(g + 1) * BLOCK]], axis=0)
    qs = _stack_heads(q, g, qpk)
    s = lax.dot_general(qs, kcat, NT, preferred_element_type=F32)
    row = lax.broadcasted_iota(jnp.int32, (BLOCK, 2 * BLOCK), 0)
    col = lax.broadcasted_iota(jnp.int32, (BLOCK, 2 * BLOCK), 1)
    ok = (col > row) & (col <= row + BLOCK) & ((col >= BLOCK) | (n > 0))
    s = jnp.where(jnp.concatenate([ok] * qpk, axis=0), s, -1e30)
    sk = jnp.concatenate([jnp.full((BLOCK, 1), sink_ref[0, g * qpk + j], F32) for j in range(qpk)], axis=0)
    m = jnp.maximum(jnp.max(s, axis=-1, keepdims=True), sk)
    e = jnp.exp(s - m)
    es = jnp.exp(sk - m)
    z = jnp.sum(e, axis=-1, keepdims=True) + es
    return e / z, es / z, qs, kcat


def _attn_fwd(qr, kr, vb, sinks):
    S, AW = qr.shape
    KW = kr.shape[1]
    nb = S // BLOCK
    nkv = KW // BLOCK
    qpk = AW // (nkv * HEAD_DIM)
    assert qpk % 2 == 0

    def body(sink_ref, q_ref, kp_ref, kc_ref, vp_ref, vc_ref, o_ref):
        n = pl.program_id(0)
        q, kp, kc, vp, vc = q_ref[...], kp_ref[...], kc_ref[...], vp_ref[...], vc_ref[...]
        outs = []
        for g in range(nkv):
            p, _, _, _ = _attn_probs(n, q, kp, kc, g, sink_ref, qpk)
            vcat = jnp.concatenate([vp[:, g * BLOCK:(g + 1) * BLOCK], vc[:, g * BLOCK:(g + 1) * BLOCK]], axis=0)
            outs.append(_unstack_heads(jnp.dot(p.astype(MXU), vcat, preferred_element_type=F32), qpk))
        o_ref[...] = jnp.concatenate(outs, axis=-1).astype(MXU)

    cur = lambda n: (n, 0)
    prev = lambda n: (jnp.maximum(n - 1, 0), 0)
    return pl.pallas_call(
        body, name="attn_fwd", grid=(nb,),
        in_specs=[pl.BlockSpec(memory_space=pltpu.SMEM), pl.BlockSpec((BLOCK, AW), cur),
                  pl.BlockSpec((BLOCK, KW), prev), pl.BlockSpec((BLOCK, KW), cur),
                  pl.BlockSpec((BLOCK, KW), prev), pl.BlockSpec((BLOCK, KW), cur)],
        out_specs=pl.BlockSpec((BLOCK, AW), cur),
        out_shape=jax.ShapeDtypeStruct((S, AW), MXU), compiler_params=_params())(sinks, qr, kr, kr, vb, vb)


def _attn_bwd(qr, kr, vb, sinks, dattn):
    S, AW = qr.shape
    KW = kr.shape[1]
    nb = S // BLOCK
    nkv = KW // BLOCK
    qpk = AW // (nkv * HEAD_DIM)
    scale = HEAD_DIM ** -0.5

    def body(sink_ref, q_ref, kp_ref, kc_ref, vp_ref, vc_ref, do_ref,
             dq_ref, dkp_ref, dkc_ref, dvp_ref, dvc_ref, dsink_ref):
        n = pl.program_id(0)
        q, kp, kc, vp, vc = q_ref[...], kp_ref[...], kc_ref[...], vp_ref[...], vc_ref[...]
        do = do_ref[...].astype(MXU)
        lane = lax.broadcasted_iota(jnp.int32, (1, BLOCK), 1)
        dsink = jnp.zeros((1, BLOCK), F32)
        dqs, dkps, dkcs, dvps, dvcs = [], [], [], [], []
        for g in range(nkv):
            p, psink, qs, kcat = _attn_probs(n, q, kp, kc, g, sink_ref, qpk)
            vcat = jnp.concatenate([vp[:, g * BLOCK:(g + 1) * BLOCK], vc[:, g * BLOCK:(g + 1) * BLOCK]], axis=0)
            dos = _stack_heads(do, g, qpk)
            dp = lax.dot_general(dos, vcat, NT, preferred_element_type=F32)
            dv = lax.dot_general(p.astype(MXU), dos, TN, preferred_element_type=F32)
            delta = jnp.sum(p * dp, axis=-1, keepdims=True)
            ds = (p * (dp - delta)).astype(MXU)
            dsk = -psink * delta
            dqs.append(_unstack_heads(jnp.dot(ds, kcat, preferred_element_type=F32) * scale, qpk))
            dk = lax.dot_general(ds, qs, TN, preferred_element_type=F32)
            for j in range(qpk):
                tot = jnp.sum(dsk[j * BLOCK:(j + 1) * BLOCK], axis=0, keepdims=True)
                dsink = dsink + jnp.where(lane == g * qpk + j, tot, 0.0)
            dkps.append(dk[:BLOCK])
            dkcs.append(dk[BLOCK:])
            dvps.append(dv[:BLOCK])
            dvcs.append(dv[BLOCK:])
        dq_ref[...] = jnp.concatenate(dqs, axis=-1)
        dkp_ref[...] = jnp.concatenate(dkps, axis=-1)
        dkc_ref[...] = jnp.concatenate(dkcs, axis=-1)
        dvp_ref[...] = jnp.concatenate(dvps, axis=-1)
        dvc_ref[...] = jnp.concatenate(dvcs, axis=-1)

        @pl.when(n == 0)
        def _():
            dsink_ref[...] = jnp.zeros(dsink_ref.shape, F32)

        dsink_ref[...] += dsink

    cur = lambda n: (n, 0)
    prev = lambda n: (jnp.maximum(n - 1, 0), 0)
    kv = jax.ShapeDtypeStruct((S, KW), F32)
    kvspec = pl.BlockSpec((BLOCK, KW), cur)
    return pl.pallas_call(
        body, name="attn_bwd", grid=(nb,),
        in_specs=[pl.BlockSpec(memory_space=pltpu.SMEM), pl.BlockSpec((BLOCK, AW), cur),
                  pl.BlockSpec((BLOCK, KW), prev), kvspec, pl.BlockSpec((BLOCK, KW), prev), kvspec,
                  pl.BlockSpec((BLOCK, AW), cur)],
        out_specs=[pl.BlockSpec((BLOCK, AW), cur), kvspec, kvspec, kvspec, kvspec,
                   pl.BlockSpec((1, BLOCK), lambda n: (0, 0))],
        out_shape=[jax.ShapeDtypeStruct((S, AW), F32), kv, kv, kv, kv, jax.ShapeDtypeStruct((1, BLOCK), F32)],
        compiler_params=_params())(sinks, qr, kr, kr, vb, vb, dattn)


def _qk_prep_bwd(proj, qg, kg, cos, sin, dq, dkp, dkc, dvp, dvc, AW, KW):
    S = proj.shape[0]
    nb = S // BLOCK

    def body(q_ref, k_ref, qg_ref, kg_ref, cos_ref, sin_ref, dq_ref, dkp_ref, dkc_ref, dvp_ref, dvc_ref,
             o_ref, dqg_ref, dkg_ref):
        n = pl.program_id(0)
        c, s = cos_ref[...], sin_ref[...]
        has_next = jnp.where(n < nb - 1, 1.0, 0.0)
        dk = _gather_heads(dkc_ref[...] + has_next * dkp_ref[...])
        dv = _gather_heads(dvc_ref[...] + has_next * dvp_ref[...])
        dxq, dqg = _norm_rope_bwd(dq_ref[...], q_ref[...], _tile_lanes(qg_ref[...], AW),
                                  _tile_lanes(c, AW), _tile_lanes(s, AW))
        dxk, dkg = _norm_rope_bwd(dk, k_ref[...], _tile_lanes(kg_ref[...], KW),
                                  _tile_lanes(c, KW), _tile_lanes(s, KW))
        o_ref[...] = jnp.concatenate([dxq, dxk, dv], axis=-1).astype(MXU)

        @pl.when(n == 0)
        def _():
            dqg_ref[...] = jnp.zeros(dqg_ref.shape, F32)
            dkg_ref[...] = jnp.zeros(dkg_ref.shape, F32)

        dqg_ref[...] += _fold_heads(dqg)
        dkg_ref[...] += _fold_heads(dkg)

    cur = lambda n: (n, 0)
    nxt = lambda n: (jnp.minimum(n + 1, nb - 1), 0)
    vec = pl.BlockSpec((1, BLOCK), lambda n: (0, 0))
    tab = pl.BlockSpec((BLOCK, BLOCK), cur)
    return pl.pallas_call(
        body, name="qk_prep_bwd", grid=(nb,),
        in_specs=[pl.BlockSpec((BLOCK, AW), cur), pl.BlockSpec((BLOCK, KW), lambda n: (n, AW // KW)),
                  vec, vec, tab, tab, pl.BlockSpec((BLOCK, AW), cur),
                  pl.BlockSpec((BLOCK, 2 * KW), nxt), pl.BlockSpec((BLOCK, 2 * KW), cur),
                  pl.BlockSpec((BLOCK, 2 * KW), nxt), pl.BlockSpec((BLOCK, 2 * KW), cur)],
        out_specs=[pl.BlockSpec((BLOCK, AW + 2 * KW), cur), vec, vec],
        out_shape=[jax.ShapeDtypeStruct((S, AW + 2 * KW), MXU), jax.ShapeDtypeStruct((1, BLOCK), F32),
                   jax.ShapeDtypeStruct((1, BLOCK), F32)],
        compiler_params=_params())(proj, proj, qg, kg, cos, sin, dq, dkp, dkc, dvp, dvc)


SGU_LANES = 512
SGU_ROWS = 256


def _sgu_group(v, lng, lnb, w_f32, b):
    rows = v.shape[0]
    mu = jnp.mean(v, axis=-1, keepdims=True)
    vc = v - mu
    r = lax.rsqrt(jnp.mean(vc * vc, axis=-1, keepdims=True) + EPS)
    xh = vc * r
    vn = (xh * lng + lnb).astype(MXU)
    row = lax.broadcasted_iota(jnp.int32, (BLOCK, BLOCK), 0)
    col = lax.broadcasted_iota(jnp.int32, (BLOCK, BLOCK), 1)
    tri = row >= col
    w = jnp.where(tri, w_f32, 0.0).astype(MXU)
    chunks = [jnp.dot(w, vn[k * BLOCK:(k + 1) * BLOCK], preferred_element_type=F32) + b for k in range(rows // BLOCK)]
    s = chunks[0] if len(chunks) == 1 else jnp.concatenate(chunks, axis=0)
    return xh, r, vn, w, s, tri


def _sgu_layout(S, u_col):
    SW = SGU_GROUPS * BLOCK
    lb, tr = min(SGU_LANES, SW), min(SGU_ROWS, S)
    assert u_col % lb == 0 and SW % lb == 0 and S % tr == 0
    ub, nlb, gpb = u_col // lb, SW // lb, lb // BLOCK
    specs = [pl.BlockSpec((tr, lb), lambda j, i: (i, ub + j)), pl.BlockSpec((tr, lb), lambda j, i: (i, ub + nlb + j)),
             pl.BlockSpec((1, lb), lambda j, i: (0, j)), pl.BlockSpec((1, lb), lambda j, i: (0, j)),
             pl.BlockSpec((gpb, BLOCK, BLOCK), lambda j, i: (j, 0, 0)),
             pl.BlockSpec((gpb, BLOCK, 1), lambda j, i: (j, 0, 0))]
    return lb, tr, gpb, nlb, specs


def _sgu_fwd(proj, lng, lnb, ws, bs, u_col):
    S = proj.shape[0]
    lb, tr, gpb, nlb, specs = _sgu_layout(S, u_col)

    def body(pu_ref, pv_ref, lng_ref, lnb_ref, w_ref, b_ref, o_ref):
        u = _gelu(pu_ref[...])
        v = _gelu(pv_ref[...])
        outs = []
        for g in range(gpb):
            sl = slice(g * BLOCK, (g + 1) * BLOCK)
            s = _sgu_group(v[:, sl], lng_ref[:, sl], lnb_ref[:, sl], w_ref[g], b_ref[g])[4]
            outs.append(u[:, sl] * s)
        o_ref[...] = (outs[0] if gpb == 1 else jnp.concatenate(outs, axis=-1)).astype(MXU)

    return pl.pallas_call(
        body, name="sgu_fwd", grid=(nlb, S // tr), in_specs=specs,
        out_specs=pl.BlockSpec((tr, lb), lambda j, i: (i, j)),
        out_shape=jax.ShapeDtypeStruct((S, nlb * lb), MXU), compiler_params=_params())(proj, proj, lng, lnb, ws, bs)


def _sgu_bwd(proj, lng, lnb, ws, bs, dsgu, u_col, after):
    S = proj.shape[0]
    G = SGU_GROUPS
    lb, tr, gpb, nlb, specs = _sgu_layout(S, u_col)
    nch = tr // BLOCK

    def body(pu_ref, pv_ref, lng_ref, lnb_ref, w_ref, b_ref, do_ref, after_ref,
             dpu_ref, dpv_ref, dw_ref, db_ref, dlng_ref, dlnb_ref):
        pu, pv, do = pu_ref[...], pv_ref[...], do_ref[...]
        u = _gelu(pu)
        v = _gelu(pv)

        @pl.when(pl.program_id(1) == 0)
        def _():
            dw_ref[...] = jnp.zeros(dw_ref.shape, F32)
            db_ref[...] = jnp.zeros(db_ref.shape, F32)
            dlng_ref[...] = jnp.zeros(dlng_ref.shape, F32)
            dlnb_ref[...] = jnp.zeros(dlnb_ref.shape, F32)

        ss, dvs, dlng, dlnb = [], [], [], []
        for g in range(gpb):
            sl = slice(g * BLOCK, (g + 1) * BLOCK)
            xh, r, vn, w, s, tri = _sgu_group(v[:, sl], lng_ref[:, sl], lnb_ref[:, sl], w_ref[g], b_ref[g])
            ds = do[:, sl] * u[:, sl]
            dsb = ds.astype(MXU)
            dw, db, dvn = None, None, []
            for k in range(nch):
                rows = slice(k * BLOCK, (k + 1) * BLOCK)
                part = lax.dot_general(dsb[rows], vn[rows], NT, preferred_element_type=F32)
                dw = part if dw is None else dw + part
                rowsum = jnp.sum(ds[rows], axis=-1, keepdims=True)
                db = rowsum if db is None else db + rowsum
                dvn.append(lax.dot_general(w, dsb[rows], TN, preferred_element_type=F32))
            dvn = dvn[0] if nch == 1 else jnp.concatenate(dvn, axis=0)
            dw_ref[g] += jnp.where(tri, dw, 0.0)
            db_ref[g] += db
            dxh = dvn * lng_ref[:, sl]
            dvs.append(r * (dxh - jnp.mean(dxh, axis=-1, keepdims=True)
                            - xh * jnp.mean(dxh * xh, axis=-1, keepdims=True)))
            dlng.append(jnp.sum(dvn * xh, axis=0, keepdims=True))
            dlnb.append(jnp.sum(dvn, axis=0, keepdims=True))
            ss.append(s)
        cat = lambda parts: parts[0] if gpb == 1 else jnp.concatenate(parts, axis=-1)
        dpu_ref[...] = (do * cat(ss) * _gelu_grad(pu)).astype(MXU)
        dpv_ref[...] = (cat(dvs) * _gelu_grad(pv)).astype(MXU)
        dlng_ref[...] += cat(dlng)
        dlnb_ref[...] += cat(dlnb)

    tile = pl.BlockSpec((tr, lb), lambda j, i: (i, j))
    vec = pl.BlockSpec((1, lb), lambda j, i: (0, j))
    half = jax.ShapeDtypeStruct((S, G * BLOCK), MXU)
    return pl.pallas_call(
        body, name="sgu_bwd", grid=(nlb, S // tr), in_specs=specs + [tile, ANY],
        out_specs=[tile, tile, pl.BlockSpec((gpb, BLOCK, BLOCK), lambda j, i: (j, 0, 0)),
                   pl.BlockSpec((gpb, BLOCK, 1), lambda j, i: (j, 0, 0)), vec, vec],
        out_shape=[half, half, jax.ShapeDtypeStruct((G, BLOCK, BLOCK), F32),
                   jax.ShapeDtypeStruct((G, BLOCK, 1), F32),
                   jax.ShapeDtypeStruct((1, G * BLOCK), F32), jax.ShapeDtypeStruct((1, G * BLOCK), F32)],
        compiler_params=_params())(proj, proj, lng, lnb, ws, bs, dsgu, after)


def _store_f32(vals, extra, outs):
    for v, o in zip(vals, outs):
        o[...] = v


def _store_mxu(vals, extra, outs):
    for v, o in zip(vals, outs):
        o[...] = v.astype(MXU)


def _proj_in(h, w):
    S, D = h.shape
    Ns = w.shape[2]
    tm, tn = _tile(S, 1024), _tile(Ns, 1024)
    npb = Ns // tn
    return _mm("proj_in", (S // tm, N_CHIPS, npb), 0, [h, w],
               [pl.BlockSpec((tm, D), lambda i, s, j: (i, 0)), pl.BlockSpec((None, D, tn), lambda i, s, j: (s, 0, j))],
               [(0, 1, 0)], NN, 0, [jax.ShapeDtypeStruct((S, N_CHIPS * Ns), F32)],
               [pl.BlockSpec((tm, tn), lambda i, s, j: (i, s * npb + j))], [None], _store_f32)[0]


def _branches(attn, sgu, wa, ws, proj, gate0):
    S, AW = attn.shape
    SW = sgu.shape[1]
    Nb = wa.shape[2]
    D = N_CHIPS * Nb
    tm = _tile(S, 512)
    assert gate0 % Nb == 0
    ga, gb = gate0 // Nb, (gate0 + D) // Nb

    def epilogue(vals, extra, outs):
        a, b = vals
        outs[0][...] = (_sigmoid(extra[0][...]) * a + _sigmoid(extra[1][...]) * b).astype(MXU)
        outs[1][...] = a
        outs[2][...] = b

    tile = pl.BlockSpec((tm, Nb), lambda i, s: (i, s))
    wspec = lambda k: pl.BlockSpec((None, k, Nb), lambda i, s: (s, 0, 0))
    f = jax.ShapeDtypeStruct((S, D), F32)
    return _mm("branches", (S // tm, N_CHIPS), 0, [attn, sgu, wa, ws, proj, proj],
               [pl.BlockSpec((tm, AW), lambda i, s: (i, 0)), pl.BlockSpec((tm, SW), lambda i, s: (i, 0)),
                wspec(AW), wspec(SW), pl.BlockSpec((tm, Nb), lambda i, s: (i, ga + s)),
                pl.BlockSpec((tm, Nb), lambda i, s: (i, gb + s))],
               [(0, 2, 0), (1, 3, 1)], NN, 2, [jax.ShapeDtypeStruct((S, D), MXU), f, f], [tile] * 3,
               [None, None], epilogue)


def _rows_mm(name, a, w, res):
    S = a.shape[0]
    _, K, N = w.shape
    tm, tn = _tile(S, 1024), _tile(N, 1024)

    def epilogue(vals, extra, outs):
        outs[0][...] = extra[0][...] + vals[0]

    out = pl.BlockSpec((tm, tn), lambda i, j, s: (i, j))
    return _mm(name, (S // tm, N // tn, N_CHIPS), 1, [a, w, res],
               [pl.BlockSpec((tm, K), lambda i, j, s: (i, s)), pl.BlockSpec((None, K, tn), lambda i, j, s: (s, 0, j)), out],
               [(0, 1, 0)], NN, 1, [jax.ShapeDtypeStruct((S, N), F32)], [out], [(tm, tn)], epilogue)[0]


def _gate_up(h2, wg, wu):
    S, D = h2.shape
    Nf = wg.shape[2]
    tm, tk = _tile(S, 512), _tile(D, 512)

    def epilogue(vals, extra, outs):
        g, u = vals
        outs[0][...] = g
        outs[1][...] = u
        outs[2][...] = (g * _sigmoid(g) * u).astype(MXU)

    w = pl.BlockSpec((None, tk, Nf), lambda i, s, k: (s, k, 0))
    o = pl.BlockSpec((tm, Nf), lambda i, s, k: (i, s))
    f = jax.ShapeDtypeStruct((S, N_CHIPS * Nf), F32)
    return _mm("gate_up", (S // tm, N_CHIPS, D // tk), 1, [h2, wg, wu],
               [pl.BlockSpec((tm, tk), lambda i, s, k: (i, k)), w, w], [(0, 1, 0), (0, 2, 1)], NN, 0,
               [f, f, jax.ShapeDtypeStruct((S, N_CHIPS * Nf), MXU)], [o, o, o], [(tm, Nf)] * 2, epilogue)


def _down_bwd(dyb, wd, g, u):
    S, D = dyb.shape
    Kf = wd.shape[1]
    tm = _tile(S, 512)

    def epilogue(vals, extra, outs):
        da, gv, uv = vals[0], extra[0][...], extra[1][...]
        sg = _sigmoid(gv)
        outs[0][...] = (da * uv * sg * (1.0 + gv * (1.0 - sg))).astype(MXU)
        outs[1][...] = (da * gv * sg).astype(MXU)

    t = pl.BlockSpec((tm, Kf), lambda i, s: (i, s))
    o = jax.ShapeDtypeStruct((S, N_CHIPS * Kf), MXU)
    return _mm("down_bwd", (S // tm, N_CHIPS), 0, [dyb, wd, g, u],
               [pl.BlockSpec((tm, D), lambda i, s: (i, 0)), pl.BlockSpec((None, Kf, D), lambda i, s: (s, 0, 0)), t, t],
               [(0, 1, 0)], NT, 2, [o, o], [t, t], [None], epilogue)


def _out_bwd(dxb, wo, proj, ba, bb, gate0):
    S, D = dxb.shape
    Ko = wo.shape[1]
    tm = _tile(S, 512)
    assert gate0 % Ko == 0
    ga, gb = gate0 // Ko, (gate0 + D) // Ko

    def epilogue(vals, extra, outs):
        dm = vals[0]
        sa, sb = _sigmoid(extra[0][...]), _sigmoid(extra[1][...])
        outs[0][...] = (dm * sa).astype(MXU)
        outs[1][...] = (dm * sb).astype(MXU)
        outs[2][...] = (dm * extra[2][...] * sa * (1.0 - sa)).astype(MXU)
        outs[3][...] = (dm * extra[3][...] * sb * (1.0 - sb)).astype(MXU)

    t = pl.BlockSpec((tm, Ko), lambda i, s: (i, s))
    o = jax.ShapeDtypeStruct((S, D), MXU)
    return _mm("out_bwd", (S // tm, N_CHIPS), 0, [dxb, wo, proj, proj, ba, bb],
               [pl.BlockSpec((tm, D), lambda i, s: (i, 0)), pl.BlockSpec((None, Ko, D), lambda i, s: (s, 0, 0)),
                pl.BlockSpec((tm, Ko), lambda i, s: (i, ga + s)), pl.BlockSpec((tm, Ko), lambda i, s: (i, gb + s)), t, t],
               [(0, 1, 0)], NT, 4, [o] * 4, [t] * 4, [None], epilogue)


def _dx_cols(name, terms, n_out, after=None):
    S = terms[0][0].shape[0]
    _, K, Ns = terms[0][1].shape
    tm, tko, tn = _tile(S, 1024), _tile(K, 1024), _tile(Ns, 1408)
    npb = Ns // tn
    operands, specs, pairs = [], [], []
    for t, (dy, w, k) in enumerate(terms):
        assert w.shape == (N_CHIPS, K, Ns)
        operands += [dy, w]
        specs += [pl.BlockSpec((tm, tn), lambda i, jk, s, jn: (i, s * npb + jn)),
                  pl.BlockSpec((None, tko, tn), lambda i, jk, s, jn: (s, jk, jn))]
        pairs.append((2 * t, 2 * t + 1, k))
    out = pl.BlockSpec((tm, tko), lambda i, jk, s, jn: (i, jk))
    return _mm(name, (S // tm, K // tko, N_CHIPS, npb), 2, operands, specs, pairs, NT, 0,
               [jax.ShapeDtypeStruct((S, K), F32)] * n_out, [out] * n_out, [(tm, tko)] * n_out, _store_f32, after)


def _dw_cols(name, a, dy):
    S, K = a.shape
    Ns = dy.shape[1] // N_CHIPS
    tk, tn = _tile(K, 512), _tile(Ns, 1408)
    npb = Ns // tn
    return _mm(name, (K // tk, N_CHIPS, npb), 0, [a, dy],
               [pl.BlockSpec((S, tk), lambda jk, s, jn: (0, jk)), pl.BlockSpec((S, tn), lambda jk, s, jn: (0, s * npb + jn))],
               [(0, 1, 0)], TN, 0, [jax.ShapeDtypeStruct((N_CHIPS, K, Ns), MXU)],
               [pl.BlockSpec((None, tk, tn), lambda jk, s, jn: (s, jk, jn))], [None], _store_mxu)[0]


def _dw_rows(name, a, dy):
    S = a.shape[0]
    K = a.shape[1] // N_CHIPS
    N = dy.shape[1]
    tk, tn = _tile(K, 1408), _tile(N, 1024)
    nkb = K // tk
    return _mm(name, (N_CHIPS, nkb, N // tn), 0, [a, dy],
               [pl.BlockSpec((S, tk), lambda s, jk, jn: (0, s * nkb + jk)), pl.BlockSpec((S, tn), lambda s, jk, jn: (0, jn))],
               [(0, 1, 0)], TN, 0, [jax.ShapeDtypeStruct((N_CHIPS, K, N), MXU)],
               [pl.BlockSpec((None, tk, tn), lambda s, jk, jn: (s, jk, jn))], [None], _store_mxu)[0]


def _layer_fwd(x, stream, layer, last, sp, cos, sin, dims):
    AW, KW, gate0, u_col = dims
    h = _rms_fwd("mix_norm", x, sp["mix_norm"])
    w = stream.finish(layer, 0, h)
    proj = _proj_in(h, w["w_in"])
    qr, kr, vb = _qk_prep(proj, sp["q_norm"], sp["k_norm"], cos, sin, AW, KW)
    stream.forward(layer, 1, qr)
    attn = _attn_fwd(qr, kr, vb, sp["sinks"])
    w.update(stream.finish(layer, 1, attn))
    sgu = _sgu_fwd(proj, sp["sgu_ln_g"], sp["sgu_ln_b"], sp["w_spatial"], sp["b_spatial"], u_col)
    merged, ba, bb = _branches(attn, sgu, w["w_attn_branch"], w["w_sgu_branch"], proj, gate0)
    stream.forward(layer, 2, merged)
    x1 = _rows_mm("out_proj", merged, w["w_out"], x)
    w.update(stream.finish(layer, 2, x1))
    h2 = _rms_fwd("ffn_norm", x1, sp["ffn_norm"])
    stream.forward(layer, 3, h2)
    g, u, act = _gate_up(h2, w["w_gate"], w["w_up"])
    w.update(stream.finish(layer, 3, g))
    x2 = _rows_mm("down_proj", act, w["w_down"], x1)
    if not last:
        stream.forward(layer + 1, 0, x2)
    saved = dict(x=x, h=h, proj=proj, qr=qr, kr=kr, vb=vb, attn=attn, sgu=sgu, merged=merged, ba=ba, bb=bb,
                 x1=x1, h2=h2, g=g, u=u, act=act)
    return x2, saved, w


def _layer_bwd(dy, dyb, w, sp, sv, cos, sin, dims, reducer, layer):
    AW, KW, gate0, u_col = dims
    big, small = {}, {}
    dg, du = _down_bwd(dyb, w["w_down"], sv["g"], sv["u"])
    big["w_down"] = _dw_rows("dw_down", sv["act"], dyb)
    big["w_gate"] = _dw_cols("dw_gate", sv["h2"], dg)
    big["w_up"] = _dw_cols("dw_up", sv["h2"], du)
    token = reducer.start(layer, 2, big)
    dh2 = _dx_cols("dh2", [(dg, w["w_gate"], 0), (du, w["w_up"], 0)], 1, token)[0]
    token = reducer.scatter(layer, 2, dh2)
    dx1, dx1b, small["ffn_norm"] = _rms_bwd("ffn_norm_bwd", dh2, sv["x1"], sp["ffn_norm"], dy, token)
    dba, dbb, dgla, dglb = _out_bwd(dx1b, w["w_out"], sv["proj"], sv["ba"], sv["bb"], gate0)
    big["w_out"] = _dw_rows("dw_out", sv["merged"], dx1b)
    big["w_attn_branch"] = _dw_cols("dw_attn_branch", sv["attn"], dba)
    big["w_sgu_branch"] = _dw_cols("dw_sgu_branch", sv["sgu"], dbb)
    token = reducer.start(layer, 1, big)
    dattn, dsgu = _dx_cols("dbranch_in", [(dba, w["w_attn_branch"], 0), (dbb, w["w_sgu_branch"], 1)], 2, token)
    token = reducer.scatter(layer, 1, dsgu)
    dpu, dpv, small["w_spatial"], db, small["sgu_ln_g"], small["sgu_ln_b"] = _sgu_bwd(
        sv["proj"], sp["sgu_ln_g"], sp["sgu_ln_b"], sp["w_spatial"], sp["b_spatial"], dsgu, u_col, token)
    small["b_spatial"] = db[:, :, 0]
    dq, dkp, dkc, dvp, dvc, dsink = _attn_bwd(sv["qr"], sv["kr"], sv["vb"], sp["sinks"], dattn)
    small["sinks"] = dsink[:, :sp["sinks"].shape[1]]
    dqkv, dqg, dkg = _qk_prep_bwd(sv["proj"], sp["q_norm"], sp["k_norm"], cos, sin, dq, dkp, dkc, dvp, dvc, AW, KW)
    small["q_norm"] = dqg[:, :HEAD_DIM]
    small["k_norm"] = dkg[:, :HEAD_DIM]
    dproj = jnp.concatenate([dqkv, dpu, dpv, dgla, dglb], axis=1)
    big["w_in"] = _dw_cols("dw_in", sv["h"], dproj)
    token = reducer.start(layer, 0, big)
    dh = _dx_cols("dh", [(dproj, w["w_in"], 0)], 1, token)[0]
    token = reducer.scatter(layer, 0, dh)
    dx, dxb, small["mix_norm"] = _rms_bwd("mix_norm_bwd", dh, sv["x"], sp["mix_norm"], dx1, token)
    return dx, dxb, small


def _place():
    x, y, c = lax.axis_index("x"), lax.axis_index("y"), lax.axis_index("c")
    chips = [(1 - x, y), (x, 1 - y), (1 - x, 1 - y)]
    return x, y, c, chips


def _half_rows(c, rows):
    h = rows // 2
    assert h % 16 == 0
    return pl.ds(pl.multiple_of(c * h, 16), h)


def _row_tile(rows, pref):
    best = None
    for t in range(16, min(rows, pref) + 1, 16):
        if rows % t == 0:
            best = t
    assert best is not None, rows
    return best


def _cast_own(name, chip, w, layer):
    _, R, C = w.shape
    tr = _row_tile(R, 256)

    def body(chip_ref, w_ref, o_ref):
        o_ref[...] = w_ref[...].astype(MXU)

    return pl.pallas_call(
        body, name=name, out_shape=jax.ShapeDtypeStruct((N_CHIPS, R, C), MXU),
        grid_spec=pltpu.PrefetchScalarGridSpec(
            num_scalar_prefetch=1, grid=(R // tr,),
            in_specs=[pl.BlockSpec((None, tr, C), lambda i, chip_ref: (layer, i, 0))],
            out_specs=pl.BlockSpec((None, tr, C), lambda i, chip_ref: (chip_ref[0], i, 0))),
        compiler_params=_params())(chip, w)


HBM = pl.BlockSpec(memory_space=pltpu.HBM)
SEM = pl.BlockSpec(memory_space=pltpu.SEMAPHORE)
DATAFLOW = pltpu.SideEffectType.DATAFLOW_SIDE_EFFECTING


def _gather_copies(bufs, send_sem, recv_sem):
    x, y, c, chips = _place()

    def ici(a, j, block):
        px, py = chips[j]
        blk = bufs[a].at[block, _half_rows(c, bufs[a].shape[1])]
        return pltpu.make_async_remote_copy(
            src_ref=blk, dst_ref=blk, send_sem=send_sem.at[3 * a + j], recv_sem=recv_sem.at[3 * a + j],
            device_id=(px, py, c), device_id_type=MESH)

    def d2d(a, j, core):
        px, py = chips[j]
        blk = bufs[a].at[2 * px + py, _half_rows(core, bufs[a].shape[1])]
        return pltpu.make_async_remote_copy(
            src_ref=blk, dst_ref=blk, send_sem=send_sem.at[3 * a + j], recv_sem=recv_sem.at[3 * a + j],
            device_id=(x, y, 1 - c), device_id_type=MESH)

    return ici, d2d


def _in_hbm(bufs):
    return [pltpu.with_memory_space_constraint(b, pltpu.HBM) for b in bufs]


def _gather_start(name, bufs, after):
    n = len(bufs)

    def body(*refs):
        dst = refs[n + 1:2 * n + 1]
        send_sem, recv_sem, token = refs[2 * n + 1:]
        x, y, c, chips = _place()
        ici, _ = _gather_copies(dst, send_sem, recv_sem)
        for a in range(n):
            for j in range(3):
                ici(a, j, 2 * x + y).start()
        token[...] = jnp.zeros(token.shape, token.dtype)

    sems = pltpu.SemaphoreType.DMA((3 * n,))
    outs = pl.pallas_call(
        body, name=name, in_specs=[HBM] * n + [ANY],
        out_specs=[HBM] * n + [SEM, SEM, pl.BlockSpec(memory_space=pltpu.VMEM)],
        out_shape=[pltpu.HBM(b.shape, b.dtype) for b in bufs] + [sems, sems, jax.ShapeDtypeStruct((8, BLOCK), F32)],
        input_output_aliases={a: a for a in range(n)},
        compiler_params=pltpu.CompilerParams(has_side_effects=DATAFLOW))(*_in_hbm(bufs), after)
    return outs[:n], outs[n], outs[n + 1], outs[n + 2]


def _gather_forward(name, bufs, ici_send, ici_recv, after):
    n = len(bufs)

    def body(*refs):
        ici_send_ref, ici_recv_ref = refs[n], refs[n + 1]
        dst = refs[n + 3:2 * n + 3]
        d2d_send, d2d_recv = refs[2 * n + 3:]
        x, y, c, chips = _place()
        ici, _ = _gather_copies(dst, ici_send_ref, ici_recv_ref)
        _, d2d = _gather_copies(dst, d2d_send, d2d_recv)
        for a in range(n):
            for j, (px, py) in enumerate(chips):
                ici(a, j, 2 * px + py).wait_recv()
                d2d(a, j, c).start()
        for a in range(n):
            for j in range(3):
                ici(a, j, 2 * x + y).wait_send()

    sems = pltpu.SemaphoreType.DMA((3 * n,))
    outs = pl.pallas_call(
        body, name=name, in_specs=[HBM] * n + [SEM, SEM, ANY], out_specs=[HBM] * n + [SEM, SEM],
        out_shape=[pltpu.HBM(b.shape, b.dtype) for b in bufs] + [sems, sems],
        input_output_aliases={a: a for a in range(n)},
        compiler_params=pltpu.CompilerParams(has_side_effects=DATAFLOW))(*bufs, ici_send, ici_recv, after)
    return outs[:n], outs[n], outs[n + 1]


def _gather_finish(name, bufs, d2d_send, d2d_recv, after):
    n = len(bufs)

    def body(*refs):
        send_ref, recv_ref = refs[n], refs[n + 1]
        dst = refs[n + 3:]
        x, y, c, chips = _place()
        _, d2d = _gather_copies(dst, send_ref, recv_ref)
        for a in range(n):
            for j in range(3):
                d2d(a, j, 1 - c).wait_recv()
                d2d(a, j, c).wait_send()

    return pl.pallas_call(
        body, name=name, in_specs=[HBM] * n + [SEM, SEM, ANY], out_specs=[HBM] * n,
        out_shape=[pltpu.HBM(b.shape, b.dtype) for b in bufs],
        input_output_aliases={a: a for a in range(n)},
        compiler_params=pltpu.CompilerParams(has_side_effects=DATAFLOW))(*bufs, d2d_send, d2d_recv, after)


GATHER = (("w_in",), ("w_attn_branch", "w_sgu_branch", "w_out"), ("w_gate", "w_up"), ("w_down",))
REDUCE = (("w_in",), ("w_attn_branch", "w_sgu_branch", "w_out"), ("w_gate", "w_up", "w_down"))


class _WeightStream:
    def __init__(self, started):
        self.started, self.passed = started, {}

    def forward(self, layer, group, after):
        bufs, send, recv = self.started[(layer, group)]
        self.passed[(layer, group)] = _gather_forward("gather_forward_%d_%d" % (layer, group), bufs, send, recv, after)

    def finish(self, layer, group, after):
        bufs, send, recv = self.passed[(layer, group)]
        done = _gather_finish("gather_finish_%d_%d" % (layer, group), bufs, send, recv, after)
        return dict(zip(GATHER[group], done))


def _pair_copies(grads, lands, send_sem, recv_sem):
    x, y, c, _ = _place()

    def make(a):
        theirs = _half_rows(1 - c, grads[a].shape[1])
        return pltpu.make_async_remote_copy(
            src_ref=grads[a].at[:, theirs], dst_ref=lands[a], send_sem=send_sem.at[a], recv_sem=recv_sem.at[a],
            device_id=(x, y, 1 - c), device_id_type=MESH)

    return make


def _pair_start(name, grads, after):
    n = len(grads)
    lands = [lax.empty((g.shape[0], g.shape[1] // 2, g.shape[2]), g.dtype) for g in grads]

    def body(*refs):
        src, dst = refs[2 * n + 1:3 * n + 1], refs[3 * n + 1:4 * n + 1]
        send_sem, recv_sem, token = refs[4 * n + 1:]
        copy = _pair_copies(src, dst, send_sem, recv_sem)
        for a in range(n):
            copy(a).start()
        token[...] = jnp.zeros(token.shape, token.dtype)

    sems = pltpu.SemaphoreType.DMA((n,))
    outs = pl.pallas_call(
        body, name=name, in_specs=[HBM] * (2 * n) + [ANY],
        out_specs=[HBM] * (2 * n) + [SEM, SEM, pl.BlockSpec(memory_space=pltpu.VMEM)],
        out_shape=[pltpu.HBM(b.shape, b.dtype) for b in grads + lands] + [sems, sems, jax.ShapeDtypeStruct((8, BLOCK), F32)],
        input_output_aliases={a: a for a in range(2 * n)},
        compiler_params=pltpu.CompilerParams(has_side_effects=DATAFLOW))(*_in_hbm(grads + lands), after)
    return outs[:n], outs[n:2 * n], outs[2 * n], outs[2 * n + 1], outs[2 * n + 2]


def _pair_finish(name, grads, lands, send_sem, recv_sem, after):
    n = len(grads)

    def body(*refs):
        send_ref, recv_ref = refs[2 * n], refs[2 * n + 1]
        src, dst = refs[2 * n + 3:3 * n + 3], refs[3 * n + 3:]
        copy = _pair_copies(src, dst, send_ref, recv_ref)
        for a in range(n):
            copy(a).wait_send()
            copy(a).wait_recv()

    outs = pl.pallas_call(
        body, name=name, in_specs=[HBM] * (2 * n) + [SEM, SEM, ANY], out_specs=[HBM] * (2 * n),
        out_shape=[pltpu.HBM(b.shape, b.dtype) for b in grads + lands],
        input_output_aliases={a: a for a in range(2 * n)},
        compiler_params=pltpu.CompilerParams(has_side_effects=DATAFLOW))(*grads, *lands, send_sem, recv_sem, after)
    return outs[:n], outs[n:]


def _pair_sum(name, core, g, p):
    _, h, C = p.shape
    tr = _row_tile(h, 256)
    nrb = h // tr

    def body(core_ref, g_ref, p_ref, o_ref):
        o_ref[...] = (g_ref[...].astype(F32) + p_ref[...].astype(F32)).astype(o_ref.dtype)

    spec = pl.BlockSpec((None, tr, C), lambda s, i, core_ref: (s, i, 0))
    return pl.pallas_call(
        body, name=name, out_shape=jax.ShapeDtypeStruct(p.shape, p.dtype),
        grid_spec=pltpu.PrefetchScalarGridSpec(
            num_scalar_prefetch=1, grid=(N_CHIPS, nrb),
            in_specs=[pl.BlockSpec((None, tr, C), lambda s, i, core_ref: (s, core_ref[0] * nrb + i, 0)), spec],
            out_specs=spec),
        compiler_params=_params())(core, g, p)


def _scatter_copies(sums, slots, send_sem, recv_sem):
    x, y, c, chips = _place()

    def make(a, j):
        px, py = chips[j]
        return pltpu.make_async_remote_copy(
            src_ref=sums[a].at[2 * px + py], dst_ref=slots[a].at[j], send_sem=send_sem.at[3 * a + j],
            recv_sem=recv_sem.at[3 * a + j], device_id=(px, py, c), device_id_type=MESH)

    return make


def _scatter_start(name, sums, after):
    n = len(sums)
    slots = [lax.empty((3,) + s.shape[1:], s.dtype) for s in sums]

    def body(*refs):
        src, dst = refs[2 * n + 1:3 * n + 1], refs[3 * n + 1:4 * n + 1]
        send_sem, recv_sem, token = refs[4 * n + 1:]
        copy = _scatter_copies(src, dst, send_sem, recv_sem)
        for a in range(n):
            for j in range(3):
                copy(a, j).start()
        token[...] = jnp.zeros(token.shape, token.dtype)

    sems = pltpu.SemaphoreType.DMA((3 * n,))
    outs = pl.pallas_call(
        body, name=name, in_specs=[HBM] * (2 * n) + [ANY],
        out_specs=[HBM] * (2 * n) + [SEM, SEM, pl.BlockSpec(memory_space=pltpu.VMEM)],
        out_shape=[pltpu.HBM(b.shape, b.dtype) for b in sums + slots] + [sems, sems, jax.ShapeDtypeStruct((8, BLOCK), F32)],
        input_output_aliases={a: a for a in range(2 * n)},
        compiler_params=pltpu.CompilerParams(has_side_effects=DATAFLOW))(*_in_hbm(sums + slots), after)
    return outs[:n], outs[n:2 * n], outs[2 * n], outs[2 * n + 1], outs[2 * n + 2]


def _scatter_finish(name, sums, slots, send_sem, recv_sem, after):
    n = len(sums)

    def body(*refs):
        send_ref, recv_ref = refs[2 * n], refs[2 * n + 1]
        src, dst = refs[2 * n + 3:3 * n + 3], refs[3 * n + 3:]
        copy = _scatter_copies(src, dst, send_ref, recv_ref)
        for a in range(n):
            for j in range(3):
                copy(a, j).wait_send()
                copy(a, j).wait_recv()

    outs = pl.pallas_call(
        body, name=name, in_specs=[HBM] * (2 * n) + [SEM, SEM, ANY], out_specs=[HBM] * (2 * n),
        out_shape=[pltpu.HBM(b.shape, b.dtype) for b in sums + slots],
        input_output_aliases={a: a for a in range(2 * n)},
        compiler_params=pltpu.CompilerParams(has_side_effects=DATAFLOW))(*sums, *slots, send_sem, recv_sem, after)
    return outs[:n], outs[n:]


def _slot_sum(name, place, slots, sums):
    _, h, C = slots.shape
    tr = _row_tile(h, 256)
    nrb = h // tr

    def body(place_ref, r0, r1, r2, own, o_ref):
        o_ref[...] = ((r0[...].astype(F32) + r1[...].astype(F32)) + r2[...].astype(F32)) + own[...].astype(F32)

    slot = lambda k: pl.BlockSpec((None, tr, C), lambda i, place_ref: (k, i, 0))
    return pl.pallas_call(
        body, name=name, out_shape=jax.ShapeDtypeStruct((2 * h, C), F32),
        grid_spec=pltpu.PrefetchScalarGridSpec(
            num_scalar_prefetch=1, grid=(nrb,),
            in_specs=[slot(0), slot(1), slot(2),
                      pl.BlockSpec((None, tr, C), lambda i, place_ref: (place_ref[0], i, 0))],
            out_specs=pl.BlockSpec((tr, C), lambda i, place_ref: (place_ref[1] * nrb + i, 0))),
        compiler_params=_params())(place, slots, slots, slots, sums)


def _half_exchange(name, bufs):
    n = len(bufs)

    def body(*refs):
        dst = refs[n:2 * n]
        send_sem, recv_sem = refs[2 * n:]
        x, y, c, _ = _place()
        copies = []
        for a in range(n):
            mine = dst[a].at[_half_rows(c, dst[a].shape[0])]
            copies.append(pltpu.make_async_remote_copy(
                src_ref=mine, dst_ref=mine, send_sem=send_sem.at[a], recv_sem=recv_sem.at[a],
                device_id=(x, y, 1 - c), device_id_type=MESH))
            copies[-1].start()
        for cp in copies:
            cp.wait()

    return pl.pallas_call(
        body, name=name, in_specs=[ANY] * n, out_specs=[ANY] * n,
        out_shape=[jax.ShapeDtypeStruct(b.shape, b.dtype) for b in bufs],
        input_output_aliases={a: a for a in range(n)},
        scratch_shapes=[pltpu.SemaphoreType.DMA((n,))] * 2)(*bufs)


class _GradReducer:
    def __init__(self, chip, core):
        self.core, self.place, self.pairs, self.started = core, jnp.concatenate([chip, core]), {}, []

    def start(self, layer, group, grads):
        mine = [grads[n] for n in REDUCE[group]]
        mine, lands, send, recv, token = _pair_start("grad_pair_start_%d_%d" % (layer, group), mine, self.place)
        self.pairs[(layer, group)] = (mine, lands, send, recv)
        return token

    def scatter(self, layer, group, after):
        tag = "%d_%d" % (layer, group)
        names = REDUCE[group]
        mine, lands, send, recv = self.pairs.pop((layer, group))
        mine, theirs = _pair_finish("grad_pair_finish_" + tag, mine, lands, send, recv, after)
        sums = [_pair_sum("pair_sum_%s_%d" % (n, layer), self.core, g, p) for n, g, p in zip(names, mine, theirs)]
        sums, slots, send, recv, token = _scatter_start("grad_scatter_start_" + tag, sums, self.place)
        self.started.append((layer, names, sums, slots, send, recv))
        return token

    def finish(self, after, update):
        for layer in sorted({entry[0] for entry in self.started}, reverse=True):
            halves, keys = [], []
            for lyr, names, sums, slots, send, recv in self.started:
                if lyr != layer:
                    continue
                sums, slots = _scatter_finish("grad_scatter_finish_%s_%d" % (names[0], layer), sums, slots, send, recv,
                                              after)
                for n, r, s in zip(names, slots, sums):
                    halves.append(_slot_sum("slot_sum_%s_%d" % (n, layer), self.place, r, s))
                    keys.append(n)
                after = halves[-1]
            whole = _half_exchange("grad_half_exchange_%d" % layer, halves)
            after = update(layer, dict(zip(keys, whole)))


def _all_reduce_small(v):
    rows = v.shape[0]
    n_dev = 2 * N_CHIPS

    def body(x_ref, out_ref, gat_ref, send_sems, recv_sems, local_sem):
        x, y, c, chips = _place()
        me, sibling = (x, y, c), (x, y, 1 - c)

        def slot(px, py, pc):
            return gat_ref.at[4 * px + 2 * py + pc]

        def copy(k, block, to, src=None):
            return pltpu.make_async_remote_copy(
                src_ref=slot(*block) if src is None else src, dst_ref=slot(*block), send_sem=send_sems.at[k],
                recv_sem=recv_sems.at[k], device_id=to, device_id_type=MESH)

        mine = pltpu.make_async_copy(x_ref, slot(*me), local_sem)
        mine.start()
        first = [copy(0, me, sibling, src=x_ref)]
        first += [copy(1 + j, me, (*chip, c), src=x_ref) for j, chip in enumerate(chips)]
        for cp in first:
            cp.start()
        passed = [copy(4 + j, (*chip, c), sibling) for j, chip in enumerate(chips)]
        for j, chip in enumerate(chips):
            copy(1 + j, (*chip, c), me).wait_recv()
            passed[j].start()
        copy(0, sibling, me).wait_recv()
        for j, chip in enumerate(chips):
            copy(4 + j, (*chip, 1 - c), me).wait_recv()
        for cp in first + passed:
            cp.wait_send()
        mine.wait()
        acc = gat_ref[0]
        for d in range(1, n_dev):
            acc = acc + gat_ref[d]
        out_ref[...] = acc

    vm = pl.BlockSpec(memory_space=pltpu.VMEM)
    return pl.pallas_call(
        body, name="small_grad_all_reduce", in_specs=[vm], out_specs=vm,
        out_shape=jax.ShapeDtypeStruct(v.shape, F32),
        scratch_shapes=[pltpu.VMEM((n_dev, rows, BLOCK), F32), pltpu.SemaphoreType.DMA((7,)),
                        pltpu.SemaphoreType.DMA((7,)), pltpu.SemaphoreType.DMA],
        compiler_params=_params())(v)


def _adamw_math(w, g, m, v):
    m2 = ADAM_B1 * m + (1.0 - ADAM_B1) * g
    v2 = ADAM_B2 * v + (1.0 - ADAM_B2) * (g * g)
    m_hat = m2 / (1.0 - ADAM_B1 ** ADAM_STEP)
    v_hat = v2 / (1.0 - ADAM_B2 ** ADAM_STEP)
    delta = -ADAM_LR * (m_hat / (jnp.sqrt(v_hat) + ADAM_EPS) + ADAM_WD * w)
    return delta, m2, v2


def _adamw_big(name, layer, grad, w, m, v, others):
    L, R, C = w.shape
    tr = _row_tile(R, 128)

    def body(g_ref, w_ref, m_ref, v_ref, *rest):
        go_ref, d_ref, mo_ref, vo_ref = rest[-4:]
        g = g_ref[...]
        delta, m2, v2 = _adamw_math(w_ref[...], g, m_ref[...], v_ref[...])
        go_ref[...] = g
        d_ref[...] = delta
        mo_ref[...] = m2
        vo_ref[...] = v2

    blk = pl.BlockSpec((None, tr, C), lambda i: (layer, i, 0))
    shp = jax.ShapeDtypeStruct(w.shape, F32)
    others = [] if others is None else list(others)
    return pl.pallas_call(
        body, name=name, grid=(R // tr,),
        in_specs=[pl.BlockSpec((tr, C), lambda i: (i, 0))] + [blk] * 3 + [ANY] * len(others), out_specs=[blk] * 4,
        out_shape=[shp] * 4, input_output_aliases={4 + k: k for k in range(len(others))},
        compiler_params=_params())(grad, w, m, v, *others)


def _adamw_small(g, w, m, v):
    rows = g.shape[0]
    tr = _row_tile(rows, 512)

    def body(g_ref, w_ref, m_ref, v_ref, d_ref, mo_ref, vo_ref):
        delta, m2, v2 = _adamw_math(w_ref[...], g_ref[...], m_ref[...], v_ref[...])
        d_ref[...] = delta
        mo_ref[...] = m2
        vo_ref[...] = v2

    blk = pl.BlockSpec((tr, BLOCK), lambda i: (i, 0))
    shp = jax.ShapeDtypeStruct(g.shape, F32)
    return pl.pallas_call(
        body, name="adamw_small", grid=(rows // tr,), in_specs=[blk] * 4, out_specs=[blk] * 3, out_shape=[shp] * 3,
        compiler_params=_params())(g, w, m, v)


def _pack(arrays):
    flat = jnp.concatenate([a.reshape(-1) for a in arrays])
    pad = (-flat.shape[0]) % (16 * BLOCK)
    return jnp.pad(flat, (0, pad)).reshape(-1, BLOCK)


def _unpack(packed, like):
    flat = packed.reshape(-1)
    out, off = [], 0
    for a in like:
        out.append(flat[off:off + a.size].reshape(a.shape))
        off += a.size
    return out


BIG = ("w_in", "w_attn_branch", "w_sgu_branch", "w_out", "w_gate", "w_up", "w_down")
SMALL = ("mix_norm", "q_norm", "k_norm", "sinks", "sgu_ln_g", "sgu_ln_b", "w_spatial", "b_spatial", "ffn_norm")
ORDER = ("mix_norm", "w_in", "q_norm", "k_norm", "sinks", "sgu_ln_g", "sgu_ln_b", "w_spatial", "b_spatial",
         "w_attn_branch", "w_sgu_branch", "w_out", "ffn_norm", "w_gate", "w_up", "w_down")


def _rope_tables(seq):
    pos = jnp.arange(seq, dtype=F32)
    inv_freq = jnp.power(10000.0, -jnp.arange(0, HEAD_DIM, 2, dtype=F32) / HEAD_DIM)
    ang = pos[:, None] * inv_freq[None, :]
    cos, sin = jnp.cos(ang), jnp.sin(ang)
    reps = BLOCK // HEAD_DIM
    return (jnp.tile(jnp.concatenate([cos, cos], axis=1), (1, reps)),
            jnp.tile(jnp.concatenate([-sin, sin], axis=1), (1, reps)))


def kernel(x, mix_norm, w_in, q_norm, k_norm, sinks, sgu_ln_g, sgu_ln_b, w_spatial, b_spatial, w_attn_branch, w_sgu_branch, w_out, ffn_norm, w_gate, w_up, w_down, loss_target, m_mix_norm, m_w_in, m_q_norm, m_k_norm, m_sinks, m_sgu_ln_g, m_sgu_ln_b, m_w_spatial, m_b_spatial, m_w_attn_branch, m_w_sgu_branch, m_w_out, m_ffn_norm, m_w_gate, m_w_up, m_w_down, v_mix_norm, v_w_in, v_q_norm, v_k_norm, v_sinks, v_sgu_ln_g, v_sgu_ln_b, v_w_spatial, v_b_spatial, v_w_attn_branch, v_w_sgu_branch, v_w_out, v_ffn_norm, v_w_gate, v_w_up, v_w_down):
    weights = dict(mix_norm=mix_norm, w_in=w_in, q_norm=q_norm, k_norm=k_norm, sinks=sinks, sgu_ln_g=sgu_ln_g,
                   sgu_ln_b=sgu_ln_b, w_spatial=w_spatial, b_spatial=b_spatial, w_attn_branch=w_attn_branch,
                   w_sgu_branch=w_sgu_branch, w_out=w_out, ffn_norm=ffn_norm, w_gate=w_gate, w_up=w_up, w_down=w_down)
    mom1 = dict(mix_norm=m_mix_norm, w_in=m_w_in, q_norm=m_q_norm, k_norm=m_k_norm, sinks=m_sinks,
                sgu_ln_g=m_sgu_ln_g, sgu_ln_b=m_sgu_ln_b, w_spatial=m_w_spatial, b_spatial=m_b_spatial,
                w_attn_branch=m_w_attn_branch, w_sgu_branch=m_w_sgu_branch, w_out=m_w_out, ffn_norm=m_ffn_norm,
                w_gate=m_w_gate, w_up=m_w_up, w_down=m_w_down)
    mom2 = dict(mix_norm=v_mix_norm, w_in=v_w_in, q_norm=v_q_norm, k_norm=v_k_norm, sinks=v_sinks,
                sgu_ln_g=v_sgu_ln_g, sgu_ln_b=v_sgu_ln_b, w_spatial=v_w_spatial, b_spatial=v_b_spatial,
                w_attn_branch=v_w_attn_branch, w_sgu_branch=v_w_sgu_branch, w_out=v_w_out, ffn_norm=v_ffn_norm,
                w_gate=v_w_gate, w_up=v_w_up, w_down=v_w_down)
    xs, target = x[0], loss_target[0]
    S, D = xs.shape
    L = w_in.shape[0]
    AW, KW, SW = N_Q_HEADS * HEAD_DIM, N_KV_HEADS * HEAD_DIM, SGU_GROUPS * BLOCK
    dims = (AW, KW, AW + 2 * KW + 2 * SW, AW + 2 * KW)
    cos, sin = _rope_tables(S)
    reps = BLOCK // HEAD_DIM

    chip = (2 * lax.axis_index("x") + lax.axis_index("y")).astype(jnp.int32).reshape(1)
    core = lax.axis_index("c").astype(jnp.int32).reshape(1)
    started, token = {}, chip
    for l in range(L):
        for gi, names in enumerate(GATHER):
            bufs = [_cast_own("cast_%s_%d" % (n, l), chip, weights[n], l) for n in names]
            bufs, send, recv, token = _gather_start("gather_start_%d_%d" % (l, gi), bufs, token)
            started[(l, gi)] = (bufs, send, recv)
    stream = _WeightStream(started)
    stream.forward(0, 0, token)
    sp = [dict(mix_norm=mix_norm[l][None], ffn_norm=ffn_norm[l][None], q_norm=jnp.tile(q_norm[l][None], (1, reps)),
               k_norm=jnp.tile(k_norm[l][None], (1, reps)), sinks=sinks[l][None], sgu_ln_g=sgu_ln_g[l][None],
               sgu_ln_b=sgu_ln_b[l][None], w_spatial=w_spatial[l], b_spatial=b_spatial[l][:, :, None])
          for l in range(L)]

    act, saved, wl = xs, [], []
    for l in range(L):
        act, sv, w_all = _layer_fwd(act, stream, l, l == L - 1, sp[l], cos, sin, dims)
        saved.append(sv)
        wl.append(w_all)
    loss_part, dy, dyb = _loss_head(act, target)
    loss = lax.psum(loss_part[0, 0], ("x", "y", "c"))

    reducer = _GradReducer(chip, core)
    small_g = [None] * L
    for l in reversed(range(L)):
        dy, dyb, small_g[l] = _layer_bwd(dy, dyb, wl[l], sp[l], saved[l], cos, sin, dims, reducer, l)
    grad_x = dy[None]

    updated = {}

    def update(layer, reduced):
        for n in BIG:
            updated[n] = _adamw_big("adamw_%s_%d" % (n, layer), layer, reduced[n], weights[n], mom1[n], mom2[n],
                                    updated.get(n))
        return updated[BIG[-1]][0]

    reducer.finish(dy, update)
    grads, deltas, new_m, new_v = {}, {}, {}, {}
    for n in BIG:
        grads[n], deltas[n], new_m[n], new_v[n] = updated[n]

    small_like = [weights[n] for n in SMALL]
    local = [jnp.stack([small_g[l][n].reshape(weights[n].shape[1:]) for l in range(L)]) for n in SMALL]
    g_small = _all_reduce_small(_pack(local))
    d_small, m_small, v_small = _adamw_small(g_small, _pack(small_like), _pack([mom1[n] for n in SMALL]),
                                             _pack([mom2[n] for n in SMALL]))
    for n, g, d, m2, v2 in zip(SMALL, _unpack(g_small, small_like), _unpack(d_small, small_like),
                               _unpack(m_small, small_like), _unpack(v_small, small_like)):
        grads[n], deltas[n], new_m[n], new_v[n] = g, d, m2, v2

    return (loss, grad_x, *[grads[n] for n in ORDER], *[deltas[n] for n in ORDER],
            *[new_m[n] for n in ORDER], *[new_v[n] for n in ORDER])
```

```python
import functools

import jax
import jax.numpy as jnp
from jax import lax
from jax.experimental import pallas as pl
from jax.experimental.pallas import tpu as pltpu

HEAD_DIM = 64
N_Q_HEADS = 16
N_KV_HEADS = 4
SGU_GROUPS = 8
BLOCK = 128
EPS = 1e-6
ADAM_LR = 0.001
ADAM_B1 = 0.9
ADAM_B2 = 0.999
ADAM_EPS = 1e-08
ADAM_WD = 0.01
ADAM_STEP = 10
N_CHIPS = 4
VMEM_LIMIT = 52 * 1024 * 1024
MXU_CHUNK = 256

F32 = jnp.float32
MXU = jnp.bfloat16
NN = (((1,), (0,)), ((), ()))
NT = (((1,), (1,)), ((), ()))
TN = (((0,), (0,)), ((), ()))
MESH = pl.DeviceIdType.MESH
ANY = pl.BlockSpec(memory_space=pl.ANY)


def _tile(n, pref):
    if n <= pref:
        return n
    best = None
    for t in range(BLOCK, pref + 1, BLOCK):
        if n % t == 0:
            best = t
    assert best is not None, (n, pref)
    return best


def _params():
    return pltpu.CompilerParams(vmem_limit_bytes=VMEM_LIMIT)


def _mm(name, grid, n_red, operands, specs, pairs, dims, n_extra, out_shapes, out_specs,
        acc_shapes, epilogue, after=None, chunk=None):
    n_op = len(operands) - n_extra
    n_out = len(out_shapes)
    n_acc = len(acc_shapes)
    if after is not None:
        operands, specs = list(operands) + [after], list(specs) + [ANY]
    n_in = len(operands)
    axes = [ax for ax in range(len(grid) - n_red, len(grid)) if grid[ax] > 1]

    def body(*refs):
        ops = refs[:n_op]
        extra = refs[n_op:n_op + n_extra]
        outs = refs[n_in:n_in + n_out]
        accs = refs[n_in + n_out:]

        def prod(a, b, cols=None):
            rhs = ops[b]
            if cols is not None:
                rhs = rhs.at[:, cols] if dims == NN else rhs.at[cols, :]
            return lax.dot_general(ops[a][...], rhs[...], dims, preferred_element_type=F32)

        def products(cols=None):
            vals = [None] * n_acc
            for a, b, k in pairs:
                d = prod(a, b, cols)
                vals[k] = d if vals[k] is None else vals[k] + d
            return vals

        if not axes and chunk is not None:
            width = outs[0].shape[-1]
            for c0 in range(0, width, chunk):
                cols = pl.ds(c0, min(chunk, width - c0))
                epilogue(products(cols), [e.at[:, cols] for e in extra], [o.at[:, cols] for o in outs])
        elif not axes:
            epilogue(products(), extra, outs)
        else:
            first = pl.program_id(axes[0]) == 0
            last = pl.program_id(axes[0]) == grid[axes[0]] - 1
            for ax in axes[1:]:
                first = jnp.logical_and(first, pl.program_id(ax) == 0)
                last = jnp.logical_and(last, pl.program_id(ax) == grid[ax] - 1)

            @pl.when(first)
            def _():
                for acc in accs:
                    acc[...] = jnp.zeros(acc.shape, F32)

            for a, b, k in pairs:
                accs[k][...] += prod(a, b)

            @pl.when(last)
            def _():
                epilogue([acc[...] for acc in accs], extra, outs)

    scratch = [pltpu.VMEM(s, F32) for s in acc_shapes] if axes else []
    return pl.pallas_call(
        body, name=name, grid=grid, in_specs=specs, out_specs=out_specs, out_shape=out_shapes,
        scratch_shapes=scratch, compiler_params=_params())(*operands)


def _sigmoid(x):
    return 1.0 / (1.0 + jnp.exp(-x))


_GELU_C = 0.7978845608028654
_GELU_A = 0.044715


def _gelu(x):
    return 0.5 * x * (1.0 + jnp.tanh(_GELU_C * (x + _GELU_A * x * x * x)))


def _gelu_grad(x):
    t = jnp.tanh(_GELU_C * (x + _GELU_A * x * x * x))
    return 0.5 * (1.0 + t) + 0.5 * x * (1.0 - t * t) * _GELU_C * (1.0 + 3.0 * _GELU_A * x * x)


def _rms_fwd(name, x, g):
    S, D = x.shape
    tr = _tile(S, 256)

    def body(x_ref, g_ref, o_ref):
        xv = x_ref[...]
        r = lax.rsqrt(jnp.mean(xv * xv, axis=-1, keepdims=True) + EPS)
        o_ref[...] = (xv * r * g_ref[...]).astype(MXU)

    return pl.pallas_call(
        body, name=name, grid=(S // tr,),
        in_specs=[pl.BlockSpec((tr, D), lambda i: (i, 0)), pl.BlockSpec((1, D), lambda i: (0, 0))],
        out_specs=pl.BlockSpec((tr, D), lambda i: (i, 0)),
        out_shape=jax.ShapeDtypeStruct((S, D), MXU), compiler_params=_params())(x, g)


def _rms_bwd(name, dh, x, g, dres, after):
    S, D = x.shape
    tr = _tile(S, 256)

    def body(dh_ref, x_ref, g_ref, dres_ref, after_ref, dx_ref, dxb_ref, dg_ref):
        xv = x_ref[...]
        r = lax.rsqrt(jnp.mean(xv * xv, axis=-1, keepdims=True) + EPS)
        xh = xv * r
        dhv = dh_ref[...]
        dy = dhv * g_ref[...]
        dx = dres_ref[...] + r * (dy - xh * jnp.mean(dy * xh, axis=-1, keepdims=True))
        dx_ref[...] = dx
        dxb_ref[...] = dx.astype(MXU)

        @pl.when(pl.program_id(0) == 0)
        def _():
            dg_ref[...] = jnp.zeros(dg_ref.shape, F32)

        dg_ref[...] += jnp.sum(dhv * xh, axis=0, keepdims=True)

    row = pl.BlockSpec((tr, D), lambda i: (i, 0))
    vec = pl.BlockSpec((1, D), lambda i: (0, 0))
    return pl.pallas_call(
        body, name=name, grid=(S // tr,), in_specs=[row, row, vec, row, ANY], out_specs=[row, row, vec],
        out_shape=[jax.ShapeDtypeStruct((S, D), F32), jax.ShapeDtypeStruct((S, D), MXU),
                   jax.ShapeDtypeStruct((1, D), F32)],
        compiler_params=_params())(dh, x, g, dres, after)


def _loss_head(y, target):
    S, D = y.shape
    tr = _tile(S, 256)

    def body(y_ref, t_ref, loss_ref, dy_ref, dyb_ref):
        d = y_ref[...] - t_ref[...]
        dy = d * (1.0 / D)
        dy_ref[...] = dy
        dyb_ref[...] = dy.astype(MXU)

        @pl.when(pl.program_id(0) == 0)
        def _():
            loss_ref[...] = jnp.zeros(loss_ref.shape, F32)

        loss_ref[...] += (0.5 / D) * jnp.sum(jnp.sum(d * d, axis=-1, keepdims=True), axis=0, keepdims=True)

    row = pl.BlockSpec((tr, D), lambda i: (i, 0))
    return pl.pallas_call(
        body, name="loss_head", grid=(S // tr,), in_specs=[row, row],
        out_specs=[pl.BlockSpec((1, 1), lambda i: (0, 0)), row, row],
        out_shape=[jax.ShapeDtypeStruct((1, 1), F32), jax.ShapeDtypeStruct((S, D), F32),
                   jax.ShapeDtypeStruct((S, D), MXU)],
        compiler_params=_params())(y, target)


def _head_sum(v):
    r = lax.broadcasted_iota(jnp.int32, (BLOCK, BLOCK), 0) // HEAD_DIM
    c = lax.broadcasted_iota(jnp.int32, (BLOCK, BLOCK), 1) // HEAD_DIM
    ones = jnp.where(r == c, 1.0, 0.0).astype(jnp.bfloat16)
    hi = v.astype(jnp.bfloat16)
    lo = (v - hi.astype(F32)).astype(jnp.bfloat16)
    parts = []
    for t in range(v.shape[1] // BLOCK):
        sl = slice(t * BLOCK, (t + 1) * BLOCK)
        parts.append(jnp.dot(hi[:, sl], ones, preferred_element_type=F32)
                     + jnp.dot(lo[:, sl], ones, preferred_element_type=F32))
    return parts[0] if len(parts) == 1 else jnp.concatenate(parts, axis=-1)


def _swap_halves(v):
    w = v.shape[1]
    half = HEAD_DIM // 2
    lane = lax.broadcasted_iota(jnp.int32, v.shape, 1) % HEAD_DIM
    return jnp.where(lane < half, pltpu.roll(v, w - half, 1), pltpu.roll(v, half, 1))


def _norm_rope(xv, gain, cos, sin):
    r = lax.rsqrt(_head_sum(xv * xv) * (1.0 / HEAD_DIM) + EPS)
    xn = xv * r * gain
    return xn * cos + _swap_halves(xn) * sin


def _norm_rope_bwd(dy, xv, gain, cos, sin):
    r = lax.rsqrt(_head_sum(xv * xv) * (1.0 / HEAD_DIM) + EPS)
    xh = xv * r
    dxn = dy * cos + _swap_halves(dy * sin)
    dgain = jnp.sum(dxn * xh, axis=0, keepdims=True)
    dxh = dxn * gain
    dx = r * (dxh - xh * (_head_sum(dxh * xh) * (1.0 / HEAD_DIM)))
    return dx, dgain


def _fold_heads(v):
    acc = v[:, 0:BLOCK]
    for t in range(1, v.shape[1] // BLOCK):
        acc = acc + v[:, t * BLOCK:(t + 1) * BLOCK]
    return acc + pltpu.roll(acc, HEAD_DIM, 1)


def _tile_lanes(v, width):
    return v if width == BLOCK else jnp.tile(v, (1, width // BLOCK))


def _low_half(rows):
    assert BLOCK == 2 * HEAD_DIM
    return lax.broadcasted_iota(jnp.int32, (rows, BLOCK), 1) < HEAD_DIM


def _spread_heads(v):
    low = _low_half(v.shape[0])
    out = []
    for t in range(v.shape[1] // BLOCK):
        tile = v[:, t * BLOCK:(t + 1) * BLOCK]
        swapped = pltpu.roll(tile, HEAD_DIM, 1)
        out += [jnp.where(low, tile, swapped), jnp.where(low, swapped, tile)]
    return jnp.concatenate(out, axis=-1)


def _gather_heads(v):
    low = _low_half(v.shape[0])
    out = []
    for t in range(v.shape[1] // (2 * BLOCK)):
        a, b = v[:, 2 * t * BLOCK:(2 * t + 1) * BLOCK], v[:, (2 * t + 1) * BLOCK:(2 * t + 2) * BLOCK]
        out.append(jnp.where(low, a + pltpu.roll(a, HEAD_DIM, 1), b + pltpu.roll(b, HEAD_DIM, 1)))
    return out[0] if len(out) == 1 else jnp.concatenate(out, axis=-1)


def _qk_prep(proj, qg, kg, cos, sin, AW, KW):
    S = proj.shape[0]
    tr = _tile(S, 256)
    scale = HEAD_DIM ** -0.5

    def body(q_ref, k_ref, v_ref, qg_ref, kg_ref, cos_ref, sin_ref, qo_ref, ko_ref, vo_ref):
        c, s = cos_ref[...], sin_ref[...]
        q = _norm_rope(q_ref[...], _tile_lanes(qg_ref[...], AW), _tile_lanes(c, AW), _tile_lanes(s, AW))
        k = _norm_rope(k_ref[...], _tile_lanes(kg_ref[...], KW), _tile_lanes(c, KW), _tile_lanes(s, KW))
        qo_ref[...] = (q * scale).astype(MXU)
        ko_ref[...] = _spread_heads(k).astype(MXU)
        vo_ref[...] = _spread_heads(v_ref[...]).astype(MXU)

    assert AW % KW == 0
    vec = pl.BlockSpec((1, BLOCK), lambda i: (0, 0))
    tab = pl.BlockSpec((tr, BLOCK), lambda i: (i, 0))
    wide = pl.BlockSpec((tr, 2 * KW), lambda i: (i, 0))
    return pl.pallas_call(
        body, name="qk_prep", grid=(S // tr,),
        in_specs=[pl.BlockSpec((tr, AW), lambda i: (i, 0)),
                  pl.BlockSpec((tr, KW), lambda i: (i, AW // KW)),
                  pl.BlockSpec((tr, KW), lambda i: (i, AW // KW + 1)), vec, vec, tab, tab],
        out_specs=[pl.BlockSpec((tr, AW), lambda i: (i, 0)), wide, wide],
        out_shape=[jax.ShapeDtypeStruct((S, AW), MXU), jax.ShapeDtypeStruct((S, 2 * KW), MXU),
                   jax.ShapeDtypeStruct((S, 2 * KW), MXU)],
        compiler_params=_params())(proj, proj, proj, qg, kg, cos, sin)


def _stack_heads(x, g, qpk):
    low = _low_half(BLOCK)
    parts = []
    for j in range(qpk):
        h = g * qpk + j
        tile = x[:, (h // 2) * BLOCK:(h // 2 + 1) * BLOCK]
        parts.append(jnp.where(low if h % 2 == 0 else jnp.logical_not(low), tile, jnp.zeros_like(tile)))
    return jnp.concatenate(parts, axis=0)


def _unstack_heads(y, qpk):
    low = _low_half(BLOCK)
    tiles = [jnp.where(low, y[2 * t * BLOCK:(2 * t + 1) * BLOCK], y[(2 * t + 1) * BLOCK:(2 * t + 2) * BLOCK])
             for t in range(qpk // 2)]
    return tiles[0] if len(tiles) == 1 else jnp.concatenate(tiles, axis=-1)


def _attn_probs(n, q, kp, kc, g, sink_ref, qpk):
    kcat = jnp.concatenate([kp[:, g * BLOCK:(g + 1) * BLOCK], kc[:, g * BLOCK:(g + 1) * BLOCK]], axis=0)
    qs = _stack_heads(q, g, qpk)
    s = lax.dot_general(qs, kcat, NT, preferred_element_type=F32)
    row = lax.broadcasted_iota(jnp.int32, (BLOCK, 2 * BLOCK), 0)
    col = lax.broadcasted_iota(jnp.int32, (BLOCK, 2 * BLOCK), 1)
    ok = (col > row) & (col <= row + BLOCK) & ((col >= BLOCK) | (n > 0))
    s = jnp.where(jnp.concatenate([ok] * qpk, axis=0), s, -1e30)
    sk = jnp.concatenate([jnp.full((BLOCK, 1), sink_ref[0, g * qpk + j], F32) for j in range(qpk)], axis=0)
    m = jnp.maximum(jnp.max(s, axis=-1, keepdims=True), sk)
    e = jnp.exp(s - m)
    es = jnp.exp(sk - m)
    z = jnp.sum(e, axis=-1, keepdims=True) + es
    return e / z, es / z, qs, kcat


def _attn_fwd(qr, kr, vb, sinks):
    S, AW = qr.shape
    KW = kr.shape[1]
    nb = S // BLOCK
    nkv = KW // BLOCK
    qpk = AW // (nkv * HEAD_DIM)
    assert qpk % 2 == 0

    def body(sink_ref, q_ref, kp_ref, kc_ref, vp_ref, vc_ref, o_ref):
        n = pl.program_id(0)
        q, kp, kc, vp, vc = q_ref[...], kp_ref[...], kc_ref[...], vp_ref[...], vc_ref[...]
        outs = []
        for g in range(nkv):
            p, _, _, _ = _attn_probs(n, q, kp, kc, g, sink_ref, qpk)
            vcat = jnp.concatenate([vp[:, g * BLOCK:(g + 1) * BLOCK], vc[:, g * BLOCK:(g + 1) * BLOCK]], axis=0)
            outs.append(_unstack_heads(jnp.dot(p.astype(MXU), vcat, preferred_element_type=F32), qpk))
        o_ref[...] = jnp.concatenate(outs, axis=-1).astype(MXU)

    cur = lambda n: (n, 0)
    prev = lambda n: (jnp.maximum(n - 1, 0), 0)
    return pl.pallas_call(
        body, name="attn_fwd", grid=(nb,),
        in_specs=[pl.BlockSpec(memory_space=pltpu.SMEM), pl.BlockSpec((BLOCK, AW), cur),
                  pl.BlockSpec((BLOCK, KW), prev), pl.BlockSpec((BLOCK, KW), cur),
                  pl.BlockSpec((BLOCK, KW), prev), pl.BlockSpec((BLOCK, KW), cur)],
        out_specs=pl.BlockSpec((BLOCK, AW), cur),
        out_shape=jax.ShapeDtypeStruct((S, AW), MXU), compiler_params=_params())(sinks, qr, kr, kr, vb, vb)


def _attn_bwd(qr, kr, vb, sinks, dattn):
    S, AW = qr.shape
    KW = kr.shape[1]
    nb = S // BLOCK
    nkv = KW // BLOCK
    qpk = AW // (nkv * HEAD_DIM)
    scale = HEAD_DIM ** -0.5

    def body(sink_ref, q_ref, kp_ref, kc_ref, vp_ref, vc_ref, do_ref,
             dq_ref, dkp_ref, dkc_ref, dvp_ref, dvc_ref, dsink_ref):
        n = pl.program_id(0)
        q, kp, kc, vp, vc = q_ref[...], kp_ref[...], kc_ref[...], vp_ref[...], vc_ref[...]
        do = do_ref[...].astype(MXU)
        lane = lax.broadcasted_iota(jnp.int32, (1, BLOCK), 1)
        dsink = jnp.zeros((1, BLOCK), F32)
        dqs, dkps, dkcs, dvps, dvcs = [], [], [], [], []
        for g in range(nkv):
            p, psink, qs, kcat = _attn_probs(n, q, kp, kc, g, sink_ref, qpk)
            vcat = jnp.concatenate([vp[:, g * BLOCK:(g + 1) * BLOCK], vc[:, g * BLOCK:(g + 1) * BLOCK]], axis=0)
            dos = _stack_heads(do, g, qpk)
            dp = lax.dot_general(dos, vcat, NT, preferred_element_type=F32)
            dv = lax.dot_general(p.astype(MXU), dos, TN, preferred_element_type=F32)
            delta = jnp.sum(p * dp, axis=-1, keepdims=True)
            ds = (p * (dp - delta)).astype(MXU)
            dsk = -psink * delta
            dqs.append(_unstack_heads(jnp.dot(ds, kcat, preferred_element_type=F32) * scale, qpk))
            dk = lax.dot_general(ds, qs, TN, preferred_element_type=F32)
            for j in range(qpk):
                tot = jnp.sum(dsk[j * BLOCK:(j + 1) * BLOCK], axis=0, keepdims=True)
                dsink = dsink + jnp.where(lane == g * qpk + j, tot, 0.0)
            dkps.append(dk[:BLOCK])
            dkcs.append(dk[BLOCK:])
            dvps.append(dv[:BLOCK])
            dvcs.append(dv[BLOCK:])
        dq_ref[...] = jnp.concatenate(dqs, axis=-1)
        dkp_ref[...] = jnp.concatenate(dkps, axis=-1)
        dkc_ref[...] = jnp.concatenate(dkcs, axis=-1)
        dvp_ref[...] = jnp.concatenate(dvps, axis=-1)
        dvc_ref[...] = jnp.concatenate(dvcs, axis=-1)

        @pl.when(n == 0)
        def _():
            dsink_ref[...] = jnp.zeros(dsink_ref.shape, F32)

        dsink_ref[...] += dsink

    cur = lambda n: (n, 0)
    prev = lambda n: (jnp.maximum(n - 1, 0), 0)
    kv = jax.ShapeDtypeStruct((S, KW), F32)
    kvspec = pl.BlockSpec((BLOCK, KW), cur)
    return pl.pallas_call(
        body, name="attn_bwd", grid=(nb,),
        in_specs=[pl.BlockSpec(memory_space=pltpu.SMEM), pl.BlockSpec((BLOCK, AW), cur),
                  pl.BlockSpec((BLOCK, KW), prev), kvspec, pl.BlockSpec((BLOCK, KW), prev), kvspec,
                  pl.BlockSpec((BLOCK, AW), cur)],
        out_specs=[pl.BlockSpec((BLOCK, AW), cur), kvspec, kvspec, kvspec, kvspec,
                   pl.BlockSpec((1, BLOCK), lambda n: (0, 0))],
        out_shape=[jax.ShapeDtypeStruct((S, AW), F32), kv, kv, kv, kv, jax.ShapeDtypeStruct((1, BLOCK), F32)],
        compiler_params=_params())(sinks, qr, kr, kr, vb, vb, dattn)


def _qk_prep_bwd(proj, qg, kg, cos, sin, dq, dkp, dkc, dvp, dvc, AW, KW):
    S = proj.shape[0]
    nb = S // BLOCK

    def body(q_ref, k_ref, qg_ref, kg_ref, cos_ref, sin_ref, dq_ref, dkp_ref, dkc_ref, dvp_ref, dvc_ref,
             o_ref, dqg_ref, dkg_ref):
        n = pl.program_id(0)
        c, s = cos_ref[...], sin_ref[...]
        has_next = jnp.where(n < nb - 1, 1.0, 0.0)
        dk = _gather_heads(dkc_ref[...] + has_next * dkp_ref[...])
        dv = _gather_heads(dvc_ref[...] + has_next * dvp_ref[...])
        dxq, dqg = _norm_rope_bwd(dq_ref[...], q_ref[...], _tile_lanes(qg_ref[...], AW),
                                  _tile_lanes(c, AW), _tile_lanes(s, AW))
        dxk, dkg = _norm_rope_bwd(dk, k_ref[...], _tile_lanes(kg_ref[...], KW),
                                  _tile_lanes(c, KW), _tile_lanes(s, KW))
        o_ref[...] = jnp.concatenate([dxq, dxk, dv], axis=-1).astype(MXU)

        @pl.when(n == 0)
        def _():
            dqg_ref[...] = jnp.zeros(dqg_ref.shape, F32)
            dkg_ref[...] = jnp.zeros(dkg_ref.shape, F32)

        dqg_ref[...] += _fold_heads(dqg)
        dkg_ref[...] += _fold_heads(dkg)

    cur = lambda n: (n, 0)
    nxt = lambda n: (jnp.minimum(n + 1, nb - 1), 0)
    vec = pl.BlockSpec((1, BLOCK), lambda n: (0, 0))
    tab = pl.BlockSpec((BLOCK, BLOCK), cur)
    return pl.pallas_call(
        body, name="qk_prep_bwd", grid=(nb,),
        in_specs=[pl.BlockSpec((BLOCK, AW), cur), pl.BlockSpec((BLOCK, KW), lambda n: (n, AW // KW)),
                  vec, vec, tab, tab, pl.BlockSpec((BLOCK, AW), cur),
                  pl.BlockSpec((BLOCK, 2 * KW), nxt), pl.BlockSpec((BLOCK, 2 * KW), cur),
                  pl.BlockSpec((BLOCK, 2 * KW), nxt), pl.BlockSpec((BLOCK, 2 * KW), cur)],
        out_specs=[pl.BlockSpec((BLOCK, AW + 2 * KW), cur), vec, vec],
        out_shape=[jax.ShapeDtypeStruct((S, AW + 2 * KW), MXU), jax.ShapeDtypeStruct((1, BLOCK), F32),
                   jax.ShapeDtypeStruct((1, BLOCK), F32)],
        compiler_params=_params())(proj, proj, qg, kg, cos, sin, dq, dkp, dkc, dvp, dvc)


SGU_LANES = 512
SGU_ROWS = 256


def _sgu_group(v, lng, lnb, w_f32, b):
    rows = v.shape[0]
    mu = jnp.mean(v, axis=-1, keepdims=True)
    vc = v - mu
    r = lax.rsqrt(jnp.mean(vc * vc, axis=-1, keepdims=True) + EPS)
    xh = vc * r
    vn = (xh * lng + lnb).astype(MXU)
    row = lax.broadcasted_iota(jnp.int32, (BLOCK, BLOCK), 0)
    col = lax.broadcasted_iota(jnp.int32, (BLOCK, BLOCK), 1)
    tri = row >= col
    w = jnp.where(tri, w_f32, 0.0).astype(MXU)
    chunks = [jnp.dot(w, vn[k * BLOCK:(k + 1) * BLOCK], preferred_element_type=F32) + b for k in range(rows // BLOCK)]
    s = chunks[0] if len(chunks) == 1 else jnp.concatenate(chunks, axis=0)
    return xh, r, vn, w, s, tri


def _sgu_layout(S, u_col):
    SW = SGU_GROUPS * BLOCK
    lb, tr = min(SGU_LANES, SW), min(SGU_ROWS, S)
    assert u_col % lb == 0 and SW % lb == 0 and S % tr == 0
    ub, nlb, gpb = u_col // lb, SW // lb, lb // BLOCK
    specs = [pl.BlockSpec((tr, lb), lambda j, i: (i, ub + j)), pl.BlockSpec((tr, lb), lambda j, i: (i, ub + nlb + j)),
             pl.BlockSpec((1, lb), lambda j, i: (0, j)), pl.BlockSpec((1, lb), lambda j, i: (0, j)),
             pl.BlockSpec((gpb, BLOCK, BLOCK), lambda j, i: (j, 0, 0)),
             pl.BlockSpec((gpb, BLOCK, 1), lambda j, i: (j, 0, 0))]
    return lb, tr, gpb, nlb, specs


def _sgu_fwd(proj, lng, lnb, ws, bs, u_col):
    S = proj.shape[0]
    lb, tr, gpb, nlb, specs = _sgu_layout(S, u_col)

    def body(pu_ref, pv_ref, lng_ref, lnb_ref, w_ref, b_ref, o_ref):
        u = _gelu(pu_ref[...])
        v = _gelu(pv_ref[...])
        outs = []
        for g in range(gpb):
            sl = slice(g * BLOCK, (g + 1) * BLOCK)
            s = _sgu_group(v[:, sl], lng_ref[:, sl], lnb_ref[:, sl], w_ref[g], b_ref[g])[4]
            outs.append(u[:, sl] * s)
        o_ref[...] = (outs[0] if gpb == 1 else jnp.concatenate(outs, axis=-1)).astype(MXU)

    return pl.pallas_call(
        body, name="sgu_fwd", grid=(nlb, S // tr), in_specs=specs,
        out_specs=pl.BlockSpec((tr, lb), lambda j, i: (i, j)),
        out_shape=jax.ShapeDtypeStruct((S, nlb * lb), MXU), compiler_params=_params())(proj, proj, lng, lnb, ws, bs)


def _sgu_bwd(proj, lng, lnb, ws, bs, dsgu, u_col, after):
    S = proj.shape[0]
    G = SGU_GROUPS
    lb, tr, gpb, nlb, specs = _sgu_layout(S, u_col)
    nch = tr // BLOCK

    def body(pu_ref, pv_ref, lng_ref, lnb_ref, w_ref, b_ref, do_ref, after_ref,
             dpu_ref, dpv_ref, dw_ref, db_ref, dlng_ref, dlnb_ref):
        pu, pv, do = pu_ref[...], pv_ref[...], do_ref[...]
        u = _gelu(pu)
        v = _gelu(pv)

        @pl.when(pl.program_id(1) == 0)
        def _():
            dw_ref[...] = jnp.zeros(dw_ref.shape, F32)
            db_ref[...] = jnp.zeros(db_ref.shape, F32)
            dlng_ref[...] = jnp.zeros(dlng_ref.shape, F32)
            dlnb_ref[...] = jnp.zeros(dlnb_ref.shape, F32)

        ss, dvs, dlng, dlnb = [], [], [], []
        for g in range(gpb):
            sl = slice(g * BLOCK, (g + 1) * BLOCK)
            xh, r, vn, w, s, tri = _sgu_group(v[:, sl], lng_ref[:, sl], lnb_ref[:, sl], w_ref[g], b_ref[g])
            ds = do[:, sl] * u[:, sl]
            dsb = ds.astype(MXU)
            dw, db, dvn = None, None, []
            for k in range(nch):
                rows = slice(k * BLOCK, (k + 1) * BLOCK)
                part = lax.dot_general(dsb[rows], vn[rows], NT, preferred_element_type=F32)
                dw = part if dw is None else dw + part
                rowsum = jnp.sum(ds[rows], axis=-1, keepdims=True)
                db = rowsum if db is None else db + rowsum
                dvn.append(lax.dot_general(w, dsb[rows], TN, preferred_element_type=F32))
            dvn = dvn[0] if nch == 1 else jnp.concatenate(dvn, axis=0)
            dw_ref[g] += jnp.where(tri, dw, 0.0)
            db_ref[g] += db
            dxh = dvn * lng_ref[:, sl]
            dvs.append(r * (dxh - jnp.mean(dxh, axis=-1, keepdims=True)
                            - xh * jnp.mean(dxh * xh, axis=-1, keepdims=True)))
            dlng.append(jnp.sum(dvn * xh, axis=0, keepdims=True))
            dlnb.append(jnp.sum(dvn, axis=0, keepdims=True))
            ss.append(s)
        cat = lambda parts: parts[0] if gpb == 1 else jnp.concatenate(parts, axis=-1)
        dpu_ref[...] = (do * cat(ss) * _gelu_grad(pu)).astype(MXU)
        dpv_ref[...] = (cat(dvs) * _gelu_grad(pv)).astype(MXU)
        dlng_ref[...] += cat(dlng)
        dlnb_ref[...] += cat(dlnb)

    tile = pl.BlockSpec((tr, lb), lambda j, i: (i, j))
    vec = pl.BlockSpec((1, lb), lambda j, i: (0, j))
    half = jax.ShapeDtypeStruct((S, G * BLOCK), MXU)
    return pl.pallas_call(
        body, name="sgu_bwd", grid=(nlb, S // tr), in_specs=specs + [tile, ANY],
        out_specs=[tile, tile, pl.BlockSpec((gpb, BLOCK, BLOCK), lambda j, i: (j, 0, 0)),
                   pl.BlockSpec((gpb, BLOCK, 1), lambda j, i: (j, 0, 0)), vec, vec],
        out_shape=[half, half, jax.ShapeDtypeStruct((G, BLOCK, BLOCK), F32),
                   jax.ShapeDtypeStruct((G, BLOCK, 1), F32),
                   jax.ShapeDtypeStruct((1, G * BLOCK), F32), jax.ShapeDtypeStruct((1, G * BLOCK), F32)],
        compiler_params=_params())(proj, proj, lng, lnb, ws, bs, dsgu, after)


def _store_f32(vals, extra, outs):
    for v, o in zip(vals, outs):
        o[...] = v


def _store_mxu(vals, extra, outs):
    for v, o in zip(vals, outs):
        o[...] = v.astype(MXU)


def _proj_in(h, w):
    S, D = h.shape
    Ns = w.shape[2]
    tm, tn = _tile(S, 1024), _tile(Ns, 1024)
    npb = Ns // tn
    return _mm("proj_in", (S // tm, N_CHIPS, npb), 0, [h, w],
               [pl.BlockSpec((tm, D), lambda i, s, j: (i, 0)), pl.BlockSpec((None, D, tn), lambda i, s, j: (s, 0, j))],
               [(0, 1, 0)], NN, 0, [jax.ShapeDtypeStruct((S, N_CHIPS * Ns), F32)],
               [pl.BlockSpec((tm, tn), lambda i, s, j: (i, s * npb + j))], [None], _store_f32)[0]


def _branches(attn, sgu, wa, ws, proj, gate0):
    S, AW = attn.shape
    SW = sgu.shape[1]
    Nb = wa.shape[2]
    D = N_CHIPS * Nb
    tm = _tile(S, 512)
    assert gate0 % Nb == 0
    ga, gb = gate0 // Nb, (gate0 + D) // Nb

    def epilogue(vals, extra, outs):
        a, b = vals
        outs[0][...] = (_sigmoid(extra[0][...]) * a + _sigmoid(extra[1][...]) * b).astype(MXU)
        outs[1][...] = a
        outs[2][...] = b

    tile = pl.BlockSpec((tm, Nb), lambda i, s: (i, s))
    wspec = lambda k: pl.BlockSpec((None, k, Nb), lambda i, s: (s, 0, 0))
    f = jax.ShapeDtypeStruct((S, D), F32)
    return _mm("branches", (S // tm, N_CHIPS), 0, [attn, sgu, wa, ws, proj, proj],
               [pl.BlockSpec((tm, AW), lambda i, s: (i, 0)), pl.BlockSpec((tm, SW), lambda i, s: (i, 0)),
                wspec(AW), wspec(SW), pl.BlockSpec((tm, Nb), lambda i, s: (i, ga + s)),
                pl.BlockSpec((tm, Nb), lambda i, s: (i, gb + s))],
               [(0, 2, 0), (1, 3, 1)], NN, 2, [jax.ShapeDtypeStruct((S, D), MXU), f, f], [tile] * 3,
               [None, None], epilogue, chunk=MXU_CHUNK)


def _rows_mm(name, a, w, res):
    S = a.shape[0]
    _, K, N = w.shape
    tm, tn = _tile(S, 1024), _tile(N, 1024)

    def epilogue(vals, extra, outs):
        outs[0][...] = extra[0][...] + vals[0]

    out = pl.BlockSpec((tm, tn), lambda i, j, s: (i, j))
    return _mm(name, (S // tm, N // tn, N_CHIPS), 1, [a, w, res],
               [pl.BlockSpec((tm, K), lambda i, j, s: (i, s)), pl.BlockSpec((None, K, tn), lambda i, j, s: (s, 0, j)), out],
               [(0, 1, 0)], NN, 1, [jax.ShapeDtypeStruct((S, N), F32)], [out], [(tm, tn)], epilogue)[0]


def _gate_up(h2, wg, wu):
    S, D = h2.shape
    Nf = wg.shape[2]
    tm = _tile(S, 256)

    def epilogue(vals, extra, outs):
        g, u = vals
        outs[0][...] = g
        outs[1][...] = u
        outs[2][...] = (g * _sigmoid(g) * u).astype(MXU)

    w = pl.BlockSpec((None, D, Nf), lambda s, i: (s, 0, 0))
    o = pl.BlockSpec((tm, Nf), lambda s, i: (i, s))
    f = jax.ShapeDtypeStruct((S, N_CHIPS * Nf), F32)
    return _mm("gate_up", (N_CHIPS, S // tm), 0, [h2, wg, wu],
               [pl.BlockSpec((tm, D), lambda s, i: (i, 0)), w, w], [(0, 1, 0), (0, 2, 1)], NN, 0,
               [f, f, jax.ShapeDtypeStruct((S, N_CHIPS * Nf), MXU)], [o, o, o], [None, None], epilogue,
               chunk=MXU_CHUNK)


def _down_bwd(dyb, wd, g, u):
    S, D = dyb.shape
    Kf = wd.shape[1]
    tm = _tile(S, 512)

    def epilogue(vals, extra, outs):
        da, gv, uv = vals[0], extra[0][...], extra[1][...]
        sg = _sigmoid(gv)
        outs[0][...] = (da * uv * sg * (1.0 + gv * (1.0 - sg))).astype(MXU)
        outs[1][...] = (da * gv * sg).astype(MXU)

    t = pl.BlockSpec((tm, Kf), lambda i, s: (i, s))
    o = jax.ShapeDtypeStruct((S, N_CHIPS * Kf), MXU)
    return _mm("down_bwd", (S // tm, N_CHIPS), 0, [dyb, wd, g, u],
               [pl.BlockSpec((tm, D), lambda i, s: (i, 0)), pl.BlockSpec((None, Kf, D), lambda i, s: (s, 0, 0)), t, t],
               [(0, 1, 0)], NT, 2, [o, o], [t, t], [None], epilogue, chunk=MXU_CHUNK)


def _out_bwd(dxb, wo, proj, ba, bb, gate0):
    S, D = dxb.shape
    Ko = wo.shape[1]
    tm = _tile(S, 512)
    assert gate0 % Ko == 0
    ga, gb = gate0 // Ko, (gate0 + D) // Ko

    def epilogue(vals, extra, outs):
        dm = vals[0]
        sa, sb = _sigmoid(extra[0][...]), _sigmoid(extra[1][...])
        outs[0][...] = (dm * sa).astype(MXU)
        outs[1][...] = (dm * sb).astype(MXU)
        outs[2][...] = (dm * extra[2][...] * sa * (1.0 - sa)).astype(MXU)
        outs[3][...] = (dm * extra[3][...] * sb * (1.0 - sb)).astype(MXU)

    t = pl.BlockSpec((tm, Ko), lambda i, s: (i, s))
    o = jax.ShapeDtypeStruct((S, D), MXU)
    return _mm("out_bwd", (S // tm, N_CHIPS), 0, [dxb, wo, proj, proj, ba, bb],
               [pl.BlockSpec((tm, D), lambda i, s: (i, 0)), pl.BlockSpec((None, Ko, D), lambda i, s: (s, 0, 0)),
                pl.BlockSpec((tm, Ko), lambda i, s: (i, ga + s)), pl.BlockSpec((tm, Ko), lambda i, s: (i, gb + s)), t, t],
               [(0, 1, 0)], NT, 4, [o] * 4, [t] * 4, [None], epilogue, chunk=MXU_CHUNK)


def _dx_cols(name, terms, n_out, after=None):
    S = terms[0][0].shape[0]
    _, K, Ns = terms[0][1].shape
    tm, tko, tn = _tile(S, 1024), _tile(K, 1024), _tile(Ns, 1408)
    npb = Ns // tn
    operands, specs, pairs = [], [], []
    for t, (dy, w, k) in enumerate(terms):
        assert w.shape == (N_CHIPS, K, Ns)
        operands += [dy, w]
        specs += [pl.BlockSpec((tm, tn), lambda i, jk, s, jn: (i, s * npb + jn)),
                  pl.BlockSpec((None, tko, tn), lambda i, jk, s, jn: (s, jk, jn))]
        pairs.append((2 * t, 2 * t + 1, k))
    out = pl.BlockSpec((tm, tko), lambda i, jk, s, jn: (i, jk))
    return _mm(name, (S // tm, K // tko, N_CHIPS, npb), 2, operands, specs, pairs, NT, 0,
               [jax.ShapeDtypeStruct((S, K), F32)] * n_out, [out] * n_out, [(tm, tko)] * n_out, _store_f32, after)


def _dw_cols(name, a, dy):
    S, K = a.shape
    Ns = dy.shape[1] // N_CHIPS
    tk, tn = _tile(K, 512), _tile(Ns, 1408)
    npb = Ns // tn
    return _mm(name, (K // tk, N_CHIPS, npb), 0, [a, dy],
               [pl.BlockSpec((S, tk), lambda jk, s, jn: (0, jk)), pl.BlockSpec((S, tn), lambda jk, s, jn: (0, s * npb + jn))],
               [(0, 1, 0)], TN, 0, [jax.ShapeDtypeStruct((N_CHIPS, K, Ns), MXU)],
               [pl.BlockSpec((None, tk, tn), lambda jk, s, jn: (s, jk, jn))], [None], _store_mxu)[0]


def _dw_rows(name, a, dy):
    S = a.shape[0]
    K = a.shape[1] // N_CHIPS
    N = dy.shape[1]
    tk, tn = _tile(K, 1408), _tile(N, 1024)
    nkb = K // tk
    return _mm(name, (N_CHIPS, nkb, N // tn), 0, [a, dy],
               [pl.BlockSpec((S, tk), lambda s, jk, jn: (0, s * nkb + jk)), pl.BlockSpec((S, tn), lambda s, jk, jn: (0, jn))],
               [(0, 1, 0)], TN, 0, [jax.ShapeDtypeStruct((N_CHIPS, K, N), MXU)],
               [pl.BlockSpec((None, tk, tn), lambda s, jk, jn: (s, jk, jn))], [None], _store_mxu)[0]


def _layer_fwd(x, stream, layer, last, sp, cos, sin, dims):
    AW, KW, gate0, u_col = dims
    h = _rms_fwd("mix_norm", x, sp["mix_norm"])
    w = stream.finish(layer, 0, h)
    proj = _proj_in(h, w["w_in"])
    qr, kr, vb = _qk_prep(proj, sp["q_norm"], sp["k_norm"], cos, sin, AW, KW)
    stream.forward(layer, 1, qr)
    attn = _attn_fwd(qr, kr, vb, sp["sinks"])
    w.update(stream.finish(layer, 1, attn))
    sgu = _sgu_fwd(proj, sp["sgu_ln_g"], sp["sgu_ln_b"], sp["w_spatial"], sp["b_spatial"], u_col)
    merged, ba, bb = _branches(attn, sgu, w["w_attn_branch"], w["w_sgu_branch"], proj, gate0)
    stream.forward(layer, 2, merged)
    x1 = _rows_mm("out_proj", merged, w["w_out"], x)
    w.update(stream.finish(layer, 2, x1))
    h2 = _rms_fwd("ffn_norm", x1, sp["ffn_norm"])
    stream.forward(layer, 3, h2)
    g, u, act = _gate_up(h2, w["w_gate"], w["w_up"])
    w.update(stream.finish(layer, 3, g))
    x2 = _rows_mm("down_proj", act, w["w_down"], x1)
    if not last:
        stream.forward(layer + 1, 0, x2)
    saved = dict(x=x, h=h, proj=proj, qr=qr, kr=kr, vb=vb, attn=attn, sgu=sgu, merged=merged, ba=ba, bb=bb,
                 x1=x1, h2=h2, g=g, u=u, act=act)
    return x2, saved, w


def _layer_bwd(dy, dyb, w, sp, sv, cos, sin, dims, reducer, layer):
    AW, KW, gate0, u_col = dims
    big, small = {}, {}
    dg, du = _down_bwd(dyb, w["w_down"], sv["g"], sv["u"])
    big["w_down"] = _dw_rows("dw_down", sv["act"], dyb)
    big["w_gate"] = _dw_cols("dw_gate", sv["h2"], dg)
    big["w_up"] = _dw_cols("dw_up", sv["h2"], du)
    token = reducer.start(layer, 2, big)
    dh2 = _dx_cols("dh2", [(dg, w["w_gate"], 0), (du, w["w_up"], 0)], 1, token)[0]
    token = reducer.scatter(layer, 2, dh2)
    dx1, dx1b, small["ffn_norm"] = _rms_bwd("ffn_norm_bwd", dh2, sv["x1"], sp["ffn_norm"], dy, token)
    dba, dbb, dgla, dglb = _out_bwd(dx1b, w["w_out"], sv["proj"], sv["ba"], sv["bb"], gate0)
    big["w_out"] = _dw_rows("dw_out", sv["merged"], dx1b)
    big["w_attn_branch"] = _dw_cols("dw_attn_branch", sv["attn"], dba)
    big["w_sgu_branch"] = _dw_cols("dw_sgu_branch", sv["sgu"], dbb)
    token = reducer.start(layer, 1, big)
    dattn, dsgu = _dx_cols("dbranch_in", [(dba, w["w_attn_branch"], 0), (dbb, w["w_sgu_branch"], 1)], 2, token)
    token = reducer.scatter(layer, 1, dsgu)
    dpu, dpv, small["w_spatial"], db, small["sgu_ln_g"], small["sgu_ln_b"] = _sgu_bwd(
        sv["proj"], sp["sgu_ln_g"], sp["sgu_ln_b"], sp["w_spatial"], sp["b_spatial"], dsgu, u_col, token)
    small["b_spatial"] = db[:, :, 0]
    dq, dkp, dkc, dvp, dvc, dsink = _attn_bwd(sv["qr"], sv["kr"], sv["vb"], sp["sinks"], dattn)
    small["sinks"] = dsink[:, :sp["sinks"].shape[1]]
    dqkv, dqg, dkg = _qk_prep_bwd(sv["proj"], sp["q_norm"], sp["k_norm"], cos, sin, dq, dkp, dkc, dvp, dvc, AW, KW)
    small["q_norm"] = dqg[:, :HEAD_DIM]
    small["k_norm"] = dkg[:, :HEAD_DIM]
    dproj = jnp.concatenate([dqkv, dpu, dpv, dgla, dglb], axis=1)
    big["w_in"] = _dw_cols("dw_in", sv["h"], dproj)
    token = reducer.start(layer, 0, big)
    dh = _dx_cols("dh", [(dproj, w["w_in"], 0)], 1, token)[0]
    token = reducer.scatter(layer, 0, dh)
    dx, dxb, small["mix_norm"] = _rms_bwd("mix_norm_bwd", dh, sv["x"], sp["mix_norm"], dx1, token)
    return dx, dxb, small


def _place():
    x, y, c = lax.axis_index("x"), lax.axis_index("y"), lax.axis_index("c")
    chips = [(1 - x, y), (x, 1 - y), (1 - x, 1 - y)]
    return x, y, c, chips


def _half_rows(c, rows):
    h = rows // 2
    assert h % 16 == 0
    return pl.ds(pl.multiple_of(c * h, 16), h)


def _row_tile(rows, pref):
    best = None
    for t in range(16, min(rows, pref) + 1, 16):
        if rows % t == 0:
            best = t
    assert best is not None, rows
    return best


def _cast_own(name, chip, w, layer):
    _, R, C = w.shape
    tr = _row_tile(R, 512)

    def body(chip_ref, w_ref, o_ref):
        o_ref[...] = w_ref[...].astype(MXU)

    return pl.pallas_call(
        body, name=name, out_shape=jax.ShapeDtypeStruct((N_CHIPS, R, C), MXU),
        grid_spec=pltpu.PrefetchScalarGridSpec(
            num_scalar_prefetch=1, grid=(R // tr,),
            in_specs=[pl.BlockSpec((None, tr, C), lambda i, chip_ref: (layer, i, 0))],
            out_specs=pl.BlockSpec((None, tr, C), lambda i, chip_ref: (chip_ref[0], i, 0))),
        compiler_params=_params())(chip, w)


HBM = pl.BlockSpec(memory_space=pltpu.HBM)
SEM = pl.BlockSpec(memory_space=pltpu.SEMAPHORE)
DATAFLOW = pltpu.SideEffectType.DATAFLOW_SIDE_EFFECTING


def _gather_copies(bufs, send_sem, recv_sem):
    x, y, c, chips = _place()

    def ici(a, j, block):
        px, py = chips[j]
        blk = bufs[a].at[block, _half_rows(c, bufs[a].shape[1])]
        return pltpu.make_async_remote_copy(
            src_ref=blk, dst_ref=blk, send_sem=send_sem.at[3 * a + j], recv_sem=recv_sem.at[3 * a + j],
            device_id=(px, py, c), device_id_type=MESH)

    def d2d(a, j, core):
        px, py = chips[j]
        blk = bufs[a].at[2 * px + py, _half_rows(core, bufs[a].shape[1])]
        return pltpu.make_async_remote_copy(
            src_ref=blk, dst_ref=blk, send_sem=send_sem.at[3 * a + j], recv_sem=recv_sem.at[3 * a + j],
            device_id=(x, y, 1 - c), device_id_type=MESH)

    return ici, d2d


def _in_hbm(bufs):
    return [pltpu.with_memory_space_constraint(b, pltpu.HBM) for b in bufs]


def _gather_start(name, bufs, after):
    n = len(bufs)

    def body(*refs):
        dst = refs[n + 1:2 * n + 1]
        send_sem, recv_sem, token = refs[2 * n + 1:]
        x, y, c, chips = _place()
        ici, _ = _gather_copies(dst, send_sem, recv_sem)
        for a in range(n):
            for j in range(3):
                ici(a, j, 2 * x + y).start()
        token[...] = jnp.zeros(token.shape, token.dtype)

    sems = pltpu.SemaphoreType.DMA((3 * n,))
    outs = pl.pallas_call(
        body, name=name, in_specs=[HBM] * n + [ANY],
        out_specs=[HBM] * n + [SEM, SEM, pl.BlockSpec(memory_space=pltpu.VMEM)],
        out_shape=[pltpu.HBM(b.shape, b.dtype) for b in bufs] + [sems, sems, jax.ShapeDtypeStruct((8, BLOCK), F32)],
        input_output_aliases={a: a for a in range(n)},
        compiler_params=pltpu.CompilerParams(has_side_effects=DATAFLOW))(*_in_hbm(bufs), after)
    return outs[:n], outs[n], outs[n + 1], outs[n + 2]


def _gather_forward(name, bufs, ici_send, ici_recv, after):
    n = len(bufs)

    def body(*refs):
        ici_send_ref, ici_recv_ref = refs[n], refs[n + 1]
        dst = refs[n + 3:2 * n + 3]
        d2d_send, d2d_recv = refs[2 * n + 3:]
        x, y, c, chips = _place()
        ici, _ = _gather_copies(dst, ici_send_ref, ici_recv_ref)
        _, d2d = _gather_copies(dst, d2d_send, d2d_recv)
        for a in range(n):
            for j, (px, py) in enumerate(chips):
                ici(a, j, 2 * px + py).wait_recv()
                d2d(a, j, c).start()
        for a in range(n):
            for j in range(3):
                ici(a, j, 2 * x + y).wait_send()

    sems = pltpu.SemaphoreType.DMA((3 * n,))
    outs = pl.pallas_call(
        body, name=name, in_specs=[HBM] * n + [SEM, SEM, ANY], out_specs=[HBM] * n + [SEM, SEM],
        out_shape=[pltpu.HBM(b.shape, b.dtype) for b in bufs] + [sems, sems],
        input_output_aliases={a: a for a in range(n)},
        compiler_params=pltpu.CompilerParams(has_side_effects=DATAFLOW))(*bufs, ici_send, ici_recv, after)
    return outs[:n], outs[n], outs[n + 1]


def _gather_finish(name, bufs, d2d_send, d2d_recv, after):
    n = len(bufs)

    def body(*refs):
        send_ref, recv_ref = refs[n], refs[n + 1]
        dst = refs[n + 3:]
        x, y, c, chips = _place()
        _, d2d = _gather_copies(dst, send_ref, recv_ref)
        for a in range(n):
            for j in range(3):
                d2d(a, j, 1 - c).wait_recv()
                d2d(a, j, c).wait_send()

    return pl.pallas_call(
        body, name=name, in_specs=[HBM] * n + [SEM, SEM, ANY], out_specs=[HBM] * n,
        out_shape=[pltpu.HBM(b.shape, b.dtype) for b in bufs],
        input_output_aliases={a: a for a in range(n)},
        compiler_params=pltpu.CompilerParams(has_side_effects=DATAFLOW))(*bufs, d2d_send, d2d_recv, after)


GATHER = (("w_in",), ("w_attn_branch", "w_sgu_branch", "w_out"), ("w_gate", "w_up"), ("w_down",))
REDUCE = (("w_in",), ("w_attn_branch", "w_sgu_branch", "w_out"), ("w_gate", "w_up", "w_down"))


class _WeightStream:
    def __init__(self, started):
        self.started, self.passed = started, {}

    def forward(self, layer, group, after):
        bufs, send, recv = self.started[(layer, group)]
        self.passed[(layer, group)] = _gather_forward("gather_forward_%d_%d" % (layer, group), bufs, send, recv, after)

    def finish(self, layer, group, after):
        bufs, send, recv = self.passed[(layer, group)]
        done = _gather_finish("gather_finish_%d_%d" % (layer, group), bufs, send, recv, after)
        return dict(zip(GATHER[group], done))


def _pair_copies(grads, lands, send_sem, recv_sem):
    x, y, c, _ = _place()

    def make(a):
        theirs = _half_rows(1 - c, grads[a].shape[1])
        return pltpu.make_async_remote_copy(
            src_ref=grads[a].at[:, theirs], dst_ref=lands[a], send_sem=send_sem.at[a], recv_sem=recv_sem.at[a],
            device_id=(x, y, 1 - c), device_id_type=MESH)

    return make


def _pair_start(name, grads, after):
    n = len(grads)
    lands = [lax.empty((g.shape[0], g.shape[1] // 2, g.shape[2]), g.dtype) for g in grads]

    def body(*refs):
        src, dst = refs[2 * n + 1:3 * n + 1], refs[3 * n + 1:4 * n + 1]
        send_sem, recv_sem, token = refs[4 * n + 1:]
        copy = _pair_copies(src, dst, send_sem, recv_sem)
        for a in range(n):
            copy(a).start()
        token[...] = jnp.zeros(token.shape, token.dtype)

    sems = pltpu.SemaphoreType.DMA((n,))
    outs = pl.pallas_call(
        body, name=name, in_specs=[HBM] * (2 * n) + [ANY],
        out_specs=[HBM] * (2 * n) + [SEM, SEM, pl.BlockSpec(memory_space=pltpu.VMEM)],
        out_shape=[pltpu.HBM(b.shape, b.dtype) for b in grads + lands] + [sems, sems, jax.ShapeDtypeStruct((8, BLOCK), F32)],
        input_output_aliases={a: a for a in range(2 * n)},
        compiler_params=pltpu.CompilerParams(has_side_effects=DATAFLOW))(*_in_hbm(grads + lands), after)
    return outs[:n], outs[n:2 * n], outs[2 * n], outs[2 * n + 1], outs[2 * n + 2]


def _pair_finish(name, grads, lands, send_sem, recv_sem, after):
    n = len(grads)

    def body(*refs):
        send_ref, recv_ref = refs[2 * n], refs[2 * n + 1]
        src, dst = refs[2 * n + 3:3 * n + 3], refs[3 * n + 3:]
        copy = _pair_copies(src, dst, send_ref, recv_ref)
        for a in range(n):
            copy(a).wait_send()
            copy(a).wait_recv()

    outs = pl.pallas_call(
        body, name=name, in_specs=[HBM] * (2 * n) + [SEM, SEM, ANY], out_specs=[HBM] * (2 * n),
        out_shape=[pltpu.HBM(b.shape, b.dtype) for b in grads + lands],
        input_output_aliases={a: a for a in range(2 * n)},
        compiler_params=pltpu.CompilerParams(has_side_effects=DATAFLOW))(*grads, *lands, send_sem, recv_sem, after)
    return outs[:n], outs[n:]


def _pair_sum(name, core, g, p):
    _, h, C = p.shape
    tr = _row_tile(h, 512)
    nrb = h // tr

    def body(core_ref, g_ref, p_ref, o_ref):
        o_ref[...] = (g_ref[...].astype(F32) + p_ref[...].astype(F32)).astype(o_ref.dtype)

    spec = pl.BlockSpec((None, tr, C), lambda s, i, core_ref: (s, i, 0))
    return pl.pallas_call(
        body, name=name, out_shape=jax.ShapeDtypeStruct(p.shape, p.dtype),
        grid_spec=pltpu.PrefetchScalarGridSpec(
            num_scalar_prefetch=1, grid=(N_CHIPS, nrb),
            in_specs=[pl.BlockSpec((None, tr, C), lambda s, i, core_ref: (s, core_ref[0] * nrb + i, 0)), spec],
            out_specs=spec),
        compiler_params=_params())(core, g, p)


def _scatter_copies(sums, slots, send_sem, recv_sem):
    x, y, c, chips = _place()

    def make(a, j):
        px, py = chips[j]
        return pltpu.make_async_remote_copy(
            src_ref=sums[a].at[2 * px + py], dst_ref=slots[a].at[j], send_sem=send_sem.at[3 * a + j],
            recv_sem=recv_sem.at[3 * a + j], device_id=(px, py, c), device_id_type=MESH)

    return make


def _scatter_start(name, sums, after):
    n = len(sums)
    slots = [lax.empty((3,) + s.shape[1:], s.dtype) for s in sums]

    def body(*refs):
        src, dst = refs[2 * n + 1:3 * n + 1], refs[3 * n + 1:4 * n + 1]
        send_sem, recv_sem, token = refs[4 * n + 1:]
        copy = _scatter_copies(src, dst, send_sem, recv_sem)
        for a in range(n):
            for j in range(3):
                copy(a, j).start()
        token[...] = jnp.zeros(token.shape, token.dtype)

    sems = pltpu.SemaphoreType.DMA((3 * n,))
    outs = pl.pallas_call(
        body, name=name, in_specs=[HBM] * (2 * n) + [ANY],
        out_specs=[HBM] * (2 * n) + [SEM, SEM, pl.BlockSpec(memory_space=pltpu.VMEM)],
        out_shape=[pltpu.HBM(b.shape, b.dtype) for b in sums + slots] + [sems, sems, jax.ShapeDtypeStruct((8, BLOCK), F32)],
        input_output_aliases={a: a for a in range(2 * n)},
        compiler_params=pltpu.CompilerParams(has_side_effects=DATAFLOW))(*_in_hbm(sums + slots), after)
    return outs[:n], outs[n:2 * n], outs[2 * n], outs[2 * n + 1], outs[2 * n + 2]


def _scatter_finish(name, sums, slots, send_sem, recv_sem, after):
    n = len(sums)

    def body(*refs):
        send_ref, recv_ref = refs[2 * n], refs[2 * n + 1]
        src, dst = refs[2 * n + 3:3 * n + 3], refs[3 * n + 3:]
        copy = _scatter_copies(src, dst, send_ref, recv_ref)
        for a in range(n):
            for j in range(3):
                copy(a, j).wait_send()
                copy(a, j).wait_recv()

    outs = pl.pallas_call(
        body, name=name, in_specs=[HBM] * (2 * n) + [SEM, SEM, ANY], out_specs=[HBM] * (2 * n),
        out_shape=[pltpu.HBM(b.shape, b.dtype) for b in sums + slots],
        input_output_aliases={a: a for a in range(2 * n)},
        compiler_params=pltpu.CompilerParams(has_side_effects=DATAFLOW))(*sums, *slots, send_sem, recv_sem, after)
    return outs[:n], outs[n:]


def _slot_sum(name, place, slots, sums):
    _, h, C = slots.shape
    tr = _row_tile(h, 512)
    nrb = h // tr

    def body(place_ref, r0, r1, r2, own, o_ref):
        o_ref[...] = ((r0[...].astype(F32) + r1[...].astype(F32)) + r2[...].astype(F32)) + own[...].astype(F32)

    slot = lambda k: pl.BlockSpec((None, tr, C), lambda i, place_ref: (k, i, 0))
    return pl.pallas_call(
        body, name=name, out_shape=jax.ShapeDtypeStruct((2 * h, C), F32),
        grid_spec=pltpu.PrefetchScalarGridSpec(
            num_scalar_prefetch=1, grid=(nrb,),
            in_specs=[slot(0), slot(1), slot(2),
                      pl.BlockSpec((None, tr, C), lambda i, place_ref: (place_ref[0], i, 0))],
            out_specs=pl.BlockSpec((tr, C), lambda i, place_ref: (place_ref[1] * nrb + i, 0))),
        compiler_params=_params())(place, slots, slots, slots, sums)


def _half_exchange(name, bufs):
    n = len(bufs)

    def body(*refs):
        dst = refs[n:2 * n]
        send_sem, recv_sem = refs[2 * n:]
        x, y, c, _ = _place()
        copies = []
        for a in range(n):
            mine = dst[a].at[_half_rows(c, dst[a].shape[0])]
            copies.append(pltpu.make_async_remote_copy(
                src_ref=mine, dst_ref=mine, send_sem=send_sem.at[a], recv_sem=recv_sem.at[a],
                device_id=(x, y, 1 - c), device_id_type=MESH))
            copies[-1].start()
        for cp in copies:
            cp.wait()

    return pl.pallas_call(
        body, name=name, in_specs=[ANY] * n, out_specs=[ANY] * n,
        out_shape=[jax.ShapeDtypeStruct(b.shape, b.dtype) for b in bufs],
        input_output_aliases={a: a for a in range(n)},
        scratch_shapes=[pltpu.SemaphoreType.DMA((n,))] * 2)(*bufs)


class _GradReducer:
    def __init__(self, chip, core):
        self.core, self.place, self.pairs, self.started = core, jnp.concatenate([chip, core]), {}, []

    def start(self, layer, group, grads):
        mine = [grads[n] for n in REDUCE[group]]
        mine, lands, send, recv, token = _pair_start("grad_pair_start_%d_%d" % (layer, group), mine, self.place)
        self.pairs[(layer, group)] = (mine, lands, send, recv)
        return token

    def scatter(self, layer, group, after):
        tag = "%d_%d" % (layer, group)
        names = REDUCE[group]
        mine, lands, send, recv = self.pairs.pop((layer, group))
        mine, theirs = _pair_finish("grad_pair_finish_" + tag, mine, lands, send, recv, after)
        sums = [_pair_sum("pair_sum_%s_%d" % (n, layer), self.core, g, p) for n, g, p in zip(names, mine, theirs)]
        sums, slots, send, recv, token = _scatter_start("grad_scatter_start_" + tag, sums, self.place)
        self.started.append((layer, names, sums, slots, send, recv))
        return token

    def finish(self, after, update):
        for layer in sorted({entry[0] for entry in self.started}, reverse=True):
            halves, keys = [], []
            for lyr, names, sums, slots, send, recv in self.started:
                if lyr != layer:
                    continue
                sums, slots = _scatter_finish("grad_scatter_finish_%s_%d" % (names[0], layer), sums, slots, send, recv,
                                              after)
                for n, r, s in zip(names, slots, sums):
                    halves.append(_slot_sum("slot_sum_%s_%d" % (n, layer), self.place, r, s))
                    keys.append(n)
                after = halves[-1]
            whole = _half_exchange("grad_half_exchange_%d" % layer, halves)
            after = update(layer, dict(zip(keys, whole)))


def _all_reduce_small(v):
    rows = v.shape[0]
    n_dev = 2 * N_CHIPS

    def body(x_ref, out_ref, gat_ref, send_sems, recv_sems, local_sem):
        x, y, c, chips = _place()
        me, sibling = (x, y, c), (x, y, 1 - c)

        def slot(px, py, pc):
            return gat_ref.at[4 * px + 2 * py + pc]

        def copy(k, block, to, src=None):
            return pltpu.make_async_remote_copy(
                src_ref=slot(*block) if src is None else src, dst_ref=slot(*block), send_sem=send_sems.at[k],
                recv_sem=recv_sems.at[k], device_id=to, device_id_type=MESH)

        mine = pltpu.make_async_copy(x_ref, slot(*me), local_sem)
        mine.start()
        first = [copy(0, me, sibling, src=x_ref)]
        first += [copy(1 + j, me, (*chip, c), src=x_ref) for j, chip in enumerate(chips)]
        for cp in first:
            cp.start()
        passed = [copy(4 + j, (*chip, c), sibling) for j, chip in enumerate(chips)]
        for j, chip in enumerate(chips):
            copy(1 + j, (*chip, c), me).wait_recv()
            passed[j].start()
        copy(0, sibling, me).wait_recv()
        for j, chip in enumerate(chips):
            copy(4 + j, (*chip, 1 - c), me).wait_recv()
        for cp in first + passed:
            cp.wait_send()
        mine.wait()
        acc = gat_ref[0]
        for d in range(1, n_dev):
            acc = acc + gat_ref[d]
        out_ref[...] = acc

    vm = pl.BlockSpec(memory_space=pltpu.VMEM)
    return pl.pallas_call(
        body, name="small_grad_all_reduce", in_specs=[vm], out_specs=vm,
        out_shape=jax.ShapeDtypeStruct(v.shape, F32),
        scratch_shapes=[pltpu.VMEM((n_dev, rows, BLOCK), F32), pltpu.SemaphoreType.DMA((7,)),
                        pltpu.SemaphoreType.DMA((7,)), pltpu.SemaphoreType.DMA],
        compiler_params=_params())(v)


def _adamw_math(w, g, m, v):
    m2 = ADAM_B1 * m + (1.0 - ADAM_B1) * g
    v2 = ADAM_B2 * v + (1.0 - ADAM_B2) * (g * g)
    m_hat = m2 / (1.0 - ADAM_B1 ** ADAM_STEP)
    v_hat = v2 / (1.0 - ADAM_B2 ** ADAM_STEP)
    delta = -ADAM_LR * (m_hat / (jnp.sqrt(v_hat) + ADAM_EPS) + ADAM_WD * w)
    return delta, m2, v2


def _adamw_big(name, layer, grad, w, m, v, others):
    L, R, C = w.shape
    tr = _row_tile(R, 256)

    def body(g_ref, w_ref, m_ref, v_ref, *rest):
        go_ref, d_ref, mo_ref, vo_ref = rest[-4:]
        g = g_ref[...]
        delta, m2, v2 = _adamw_math(w_ref[...], g, m_ref[...], v_ref[...])
        go_ref[...] = g
        d_ref[...] = delta
        mo_ref[...] = m2
        vo_ref[...] = v2

    blk = pl.BlockSpec((None, tr, C), lambda i: (layer, i, 0))
    shp = jax.ShapeDtypeStruct(w.shape, F32)
    others = [] if others is None else list(others)
    return pl.pallas_call(
        body, name=name, grid=(R // tr,),
        in_specs=[pl.BlockSpec((tr, C), lambda i: (i, 0))] + [blk] * 3 + [ANY] * len(others), out_specs=[blk] * 4,
        out_shape=[shp] * 4, input_output_aliases={4 + k: k for k in range(len(others))},
        compiler_params=_params())(grad, w, m, v, *others)


def _adamw_small(g, w, m, v):
    rows = g.shape[0]
    tr = _row_tile(rows, 512)

    def body(g_ref, w_ref, m_ref, v_ref, d_ref, mo_ref, vo_ref):
        delta, m2, v2 = _adamw_math(w_ref[...], g_ref[...], m_ref[...], v_ref[...])
        d_ref[...] = delta
        mo_ref[...] = m2
        vo_ref[...] = v2

    blk = pl.BlockSpec((tr, BLOCK), lambda i: (i, 0))
    shp = jax.ShapeDtypeStruct(g.shape, F32)
    return pl.pallas_call(
        body, name="adamw_small", grid=(rows // tr,), in_specs=[blk] * 4, out_specs=[blk] * 3, out_shape=[shp] * 3,
        compiler_params=_params())(g, w, m, v)


def _pack(arrays):
    flat = jnp.concatenate([a.reshape(-1) for a in arrays])
    pad = (-flat.shape[0]) % (16 * BLOCK)
    return jnp.pad(flat, (0, pad)).reshape(-1, BLOCK)


def _unpack(packed, like):
    flat = packed.reshape(-1)
    out, off = [], 0
    for a in like:
        out.append(flat[off:off + a.size].reshape(a.shape))
        off += a.size
    return out


BIG = ("w_in", "w_attn_branch", "w_sgu_branch", "w_out", "w_gate", "w_up", "w_down")
SMALL = ("mix_norm", "q_norm", "k_norm", "sinks", "sgu_ln_g", "sgu_ln_b", "w_spatial", "b_spatial", "ffn_norm")
ORDER = ("mix_norm", "w_in", "q_norm", "k_norm", "sinks", "sgu_ln_g", "sgu_ln_b", "w_spatial", "b_spatial",
         "w_attn_branch", "w_sgu_branch", "w_out", "ffn_norm", "w_gate", "w_up", "w_down")


def _rope_tables(seq):
    pos = jnp.arange(seq, dtype=F32)
    inv_freq = jnp.power(10000.0, -jnp.arange(0, HEAD_DIM, 2, dtype=F32) / HEAD_DIM)
    ang = pos[:, None] * inv_freq[None, :]
    cos, sin = jnp.cos(ang), jnp.sin(ang)
    reps = BLOCK // HEAD_DIM
    return (jnp.tile(jnp.concatenate([cos, cos], axis=1), (1, reps)),
            jnp.tile(jnp.concatenate([-sin, sin], axis=1), (1, reps)))


def kernel(x, mix_norm, w_in, q_norm, k_norm, sinks, sgu_ln_g, sgu_ln_b, w_spatial, b_spatial, w_attn_branch, w_sgu_branch, w_out, ffn_norm, w_gate, w_up, w_down, loss_target, m_mix_norm, m_w_in, m_q_norm, m_k_norm, m_sinks, m_sgu_ln_g, m_sgu_ln_b, m_w_spatial, m_b_spatial, m_w_attn_branch, m_w_sgu_branch, m_w_out, m_ffn_norm, m_w_gate, m_w_up, m_w_down, v_mix_norm, v_w_in, v_q_norm, v_k_norm, v_sinks, v_sgu_ln_g, v_sgu_ln_b, v_w_spatial, v_b_spatial, v_w_attn_branch, v_w_sgu_branch, v_w_out, v_ffn_norm, v_w_gate, v_w_up, v_w_down):
    weights = dict(mix_norm=mix_norm, w_in=w_in, q_norm=q_norm, k_norm=k_norm, sinks=sinks, sgu_ln_g=sgu_ln_g,
                   sgu_ln_b=sgu_ln_b, w_spatial=w_spatial, b_spatial=b_spatial, w_attn_branch=w_attn_branch,
                   w_sgu_branch=w_sgu_branch, w_out=w_out, ffn_norm=ffn_norm, w_gate=w_gate, w_up=w_up, w_down=w_down)
    mom1 = dict(mix_norm=m_mix_norm, w_in=m_w_in, q_norm=m_q_norm, k_norm=m_k_norm, sinks=m_sinks,
                sgu_ln_g=m_sgu_ln_g, sgu_ln_b=m_sgu_ln_b, w_spatial=m_w_spatial, b_spatial=m_b_spatial,
                w_attn_branch=m_w_attn_branch, w_sgu_branch=m_w_sgu_branch, w_out=m_w_out, ffn_norm=m_ffn_norm,
                w_gate=m_w_gate, w_up=m_w_up, w_down=m_w_down)
    mom2 = dict(mix_norm=v_mix_norm, w_in=v_w_in, q_norm=v_q_norm, k_norm=v_k_norm, sinks=v_sinks,
                sgu_ln_g=v_sgu_ln_g, sgu_ln_b=v_sgu_ln_b, w_spatial=v_w_spatial, b_spatial=v_b_spatial,
                w_attn_branch=v_w_attn_branch, w_sgu_branch=v_w_sgu_branch, w_out=v_w_out, ffn_norm=v_ffn_norm,
                w_gate=v_w_gate, w_up=v_w_up, w_down=v_w_down)
    xs, target = x[0], loss_target[0]
    S, D = xs.shape
    L = w_in.shape[0]
    AW, KW, SW = N_Q_HEADS * HEAD_DIM, N_KV_HEADS * HEAD_DIM, SGU_GROUPS * BLOCK
    dims = (AW, KW, AW + 2 * KW + 2 * SW, AW + 2 * KW)
    cos, sin = _rope_tables(S)
    reps = BLOCK // HEAD_DIM

    chip = (2 * lax.axis_index("x") + lax.axis_index("y")).astype(jnp.int32).reshape(1)
    core = lax.axis_index("c").astype(jnp.int32).reshape(1)
    started, token = {}, chip
    for l in range(L):
        for gi, names in enumerate(GATHER):
            bufs = [_cast_own("cast_%s_%d" % (n, l), chip, weights[n], l) for n in names]
            bufs, send, recv, token = _gather_start("gather_start_%d_%d" % (l, gi), bufs, token)
            started[(l, gi)] = (bufs, send, recv)
    stream = _WeightStream(started)
    stream.forward(0, 0, token)
    sp = [dict(mix_norm=mix_norm[l][None], ffn_norm=ffn_norm[l][None], q_norm=jnp.tile(q_norm[l][None], (1, reps)),
               k_norm=jnp.tile(k_norm[l][None], (1, reps)), sinks=sinks[l][None], sgu_ln_g=sgu_ln_g[l][None],
               sgu_ln_b=sgu_ln_b[l][None], w_spatial=w_spatial[l], b_spatial=b_spatial[l][:, :, None])
          for l in range(L)]

    act, saved, wl = xs, [], []
    for l in range(L):
        act, sv, w_all = _layer_fwd(act, stream, l, l == L - 1, sp[l], cos, sin, dims)
        saved.append(sv)
        wl.append(w_all)
    loss_part, dy, dyb = _loss_head(act, target)
    loss = lax.psum(loss_part[0, 0], ("x", "y", "c"))

    reducer = _GradReducer(chip, core)
    small_g = [None] * L
    for l in reversed(range(L)):
        dy, dyb, small_g[l] = _layer_bwd(dy, dyb, wl[l], sp[l], saved[l], cos, sin, dims, reducer, l)
    grad_x = dy[None]

    updated = {}

    def update(layer, reduced):
        for n in BIG:
            updated[n] = _adamw_big("adamw_%s_%d" % (n, layer), layer, reduced[n], weights[n], mom1[n], mom2[n],
                                    updated.get(n))
        return updated[BIG[-1]][0]

    reducer.finish(dy, update)
    grads, deltas, new_m, new_v = {}, {}, {}, {}
    for n in BIG:
        grads[n], deltas[n], new_m[n], new_v[n] = updated[n]

    small_like = [weights[n] for n in SMALL]
    local = [jnp.stack([small_g[l][n].reshape(weights[n].shape[1:]) for l in range(L)]) for n in SMALL]
    g_small = _all_reduce_small(_pack(local))
    d_small, m_small, v_small = _adamw_small(g_small, _pack(small_like), _pack([mom1[n] for n in SMALL]),
                                             _pack([mom2[n] for n in SMALL]))
    for n, g, d, m2, v2 in zip(SMALL, _unpack(g_small, small_like), _unpack(d_small, small_like),
                               _unpack(m_small, small_like), _unpack(v_small, small_like)):
        grads[n], deltas[n], new_m[n], new_v[n] = g, d, m2, v2

    return (loss, grad_x, *[grads[n] for n in ORDER], *[deltas[n] for n in ORDER],
            *[new_m[n] for n in ORDER], *[new_v[n] for n in ORDER])
```

```python
import functools

import jax
import jax.numpy as jnp
from jax import lax
from jax.experimental import pallas as pl
from jax.experimental.pallas import tpu as pltpu

HEAD_DIM = 64
N_Q_HEADS = 16
N_KV_HEADS = 4
SGU_GROUPS = 8
BLOCK = 128
EPS = 1e-6
ADAM_LR = 0.001
ADAM_B1 = 0.9
ADAM_B2 = 0.999
ADAM_EPS = 1e-08
ADAM_WD = 0.01
ADAM_STEP = 10
N_CHIPS = 4
VMEM_LIMIT = 52 * 1024 * 1024
MXU_CHUNK = 256

F32 = jnp.float32
MXU = jnp.bfloat16
NN = (((1,), (0,)), ((), ()))
NT = (((1,), (1,)), ((), ()))
TN = (((0,), (0,)), ((), ()))
MESH = pl.DeviceIdType.MESH
ANY = pl.BlockSpec(memory_space=pl.ANY)


def _tile(n, pref):
    if n <= pref:
        return n
    best = None
    for t in range(BLOCK, pref + 1, BLOCK):
        if n % t == 0:
            best = t
    assert best is not None, (n, pref)
    return best


def _params():
    return pltpu.CompilerParams(vmem_limit_bytes=VMEM_LIMIT)


def _mm(name, grid, n_red, operands, specs, pairs, dims, n_extra, out_shapes, out_specs,
        acc_shapes, epilogue, after=None, chunk=None):
    n_op = len(operands) - n_extra
    n_out = len(out_shapes)
    n_acc = len(acc_shapes)
    if after is not None:
        operands, specs = list(operands) + [after], list(specs) + [ANY]
    n_in = len(operands)
    axes = [ax for ax in range(len(grid) - n_red, len(grid)) if grid[ax] > 1]

    def body(*refs):
        ops = refs[:n_op]
        extra = refs[n_op:n_op + n_extra]
        outs = refs[n_in:n_in + n_out]
        accs = refs[n_in + n_out:]

        def prod(a, b, cols=None):
            rhs = ops[b]
            if cols is not None:
                rhs = rhs.at[:, cols] if dims == NN else rhs.at[cols, :]
            return lax.dot_general(ops[a][...], rhs[...], dims, preferred_element_type=F32)

        def products(cols=None):
            vals = [None] * n_acc
            for a, b, k in pairs:
                d = prod(a, b, cols)
                vals[k] = d if vals[k] is None else vals[k] + d
            return vals

        if not axes and chunk is not None:
            width = outs[0].shape[-1]
            for c0 in range(0, width, chunk):
                cols = pl.ds(c0, min(chunk, width - c0))
                epilogue(products(cols), [e.at[:, cols] for e in extra], [o.at[:, cols] for o in outs])
        elif not axes:
            epilogue(products(), extra, outs)
        else:
            first = pl.program_id(axes[0]) == 0
            last = pl.program_id(axes[0]) == grid[axes[0]] - 1
            for ax in axes[1:]:
                first = jnp.logical_and(first, pl.program_id(ax) == 0)
                last = jnp.logical_and(last, pl.program_id(ax) == grid[ax] - 1)

            @pl.when(first)
            def _():
                for acc in accs:
                    acc[...] = jnp.zeros(acc.shape, F32)

            for a, b, k in pairs:
                accs[k][...] += prod(a, b)

            @pl.when(last)
            def _():
                epilogue([acc[...] for acc in accs], extra, outs)

    scratch = [pltpu.VMEM(s, F32) for s in acc_shapes] if axes else []
    return pl.pallas_call(
        body, name=name, grid=grid, in_specs=specs, out_specs=out_specs, out_shape=out_shapes,
        scratch_shapes=scratch, compiler_params=_params())(*operands)


def _sigmoid(x):
    return 1.0 / (1.0 + jnp.exp(-x))


_GELU_C = 0.7978845608028654
_GELU_A = 0.044715


def _gelu(x):
    return 0.5 * x * (1.0 + jnp.tanh(_GELU_C * (x + _GELU_A * x * x * x)))


def _gelu_grad(x):
    t = jnp.tanh(_GELU_C * (x + _GELU_A * x * x * x))
    return 0.5 * (1.0 + t) + 0.5 * x * (1.0 - t * t) * _GELU_C * (1.0 + 3.0 * _GELU_A * x * x)


def _rms_fwd(name, x, g):
    S, D = x.shape
    tr = _tile(S, 256)

    def body(x_ref, g_ref, o_ref):
        xv = x_ref[...]
        r = lax.rsqrt(jnp.mean(xv * xv, axis=-1, keepdims=True) + EPS)
        o_ref[...] = (xv * r * g_ref[...]).astype(MXU)

    return pl.pallas_call(
        body, name=name, grid=(S // tr,),
        in_specs=[pl.BlockSpec((tr, D), lambda i: (i, 0)), pl.BlockSpec((1, D), lambda i: (0, 0))],
        out_specs=pl.BlockSpec((tr, D), lambda i: (i, 0)),
        out_shape=jax.ShapeDtypeStruct((S, D), MXU), compiler_params=_params())(x, g)


def _rms_bwd(name, dh, x, g, dres, after):
    S, D = x.shape
    tr = _tile(S, 256)

    def body(dh_ref, x_ref, g_ref, dres_ref, after_ref, dx_ref, dxb_ref, dg_ref):
        xv = x_ref[...]
        r = lax.rsqrt(jnp.mean(xv * xv, axis=-1, keepdims=True) + EPS)
        xh = xv * r
        dhv = dh_ref[...]
        dy = dhv * g_ref[...]
        dx = dres_ref[...] + r * (dy - xh * jnp.mean(dy * xh, axis=-1, keepdims=True))
        dx_ref[...] = dx
        dxb_ref[...] = dx.astype(MXU)

        @pl.when(pl.program_id(0) == 0)
        def _():
            dg_ref[...] = jnp.zeros(dg_ref.shape, F32)

        dg_ref[...] += jnp.sum(dhv * xh, axis=0, keepdims=True)

    row = pl.BlockSpec((tr, D), lambda i: (i, 0))
    vec = pl.BlockSpec((1, D), lambda i: (0, 0))
    return pl.pallas_call(
        body, name=name, grid=(S // tr,), in_specs=[row, row, vec, row, ANY], out_specs=[row, row, vec],
        out_shape=[jax.ShapeDtypeStruct((S, D), F32), jax.ShapeDtypeStruct((S, D), MXU),
                   jax.ShapeDtypeStruct((1, D), F32)],
        compiler_params=_params())(dh, x, g, dres, after)


def _loss_head(y, target):
    S, D = y.shape
    tr = _tile(S, 256)

    def body(y_ref, t_ref, loss_ref, dy_ref, dyb_ref):
        d = y_ref[...] - t_ref[...]
        dy = d * (1.0 / D)
        dy_ref[...] = dy
        dyb_ref[...] = dy.astype(MXU)

        @pl.when(pl.program_id(0) == 0)
        def _():
            loss_ref[...] = jnp.zeros(loss_ref.shape, F32)

        loss_ref[...] += (0.5 / D) * jnp.sum(jnp.sum(d * d, axis=-1, keepdims=True), axis=0, keepdims=True)

    row = pl.BlockSpec((tr, D), lambda i: (i, 0))
    return pl.pallas_call(
        body, name="loss_head", grid=(S // tr,), in_specs=[row, row],
        out_specs=[pl.BlockSpec((1, 1), lambda i: (0, 0)), row, row],
        out_shape=[jax.ShapeDtypeStruct((1, 1), F32), jax.ShapeDtypeStruct((S, D), F32),
                   jax.ShapeDtypeStruct((S, D), MXU)],
        compiler_params=_params())(y, target)


def _head_sum(v):
    r = lax.broadcasted_iota(jnp.int32, (BLOCK, BLOCK), 0) // HEAD_DIM
    c = lax.broadcasted_iota(jnp.int32, (BLOCK, BLOCK), 1) // HEAD_DIM
    ones = jnp.where(r == c, 1.0, 0.0).astype(jnp.bfloat16)
    hi = v.astype(jnp.bfloat16)
    lo = (v - hi.astype(F32)).astype(jnp.bfloat16)
    parts = []
    for t in range(v.shape[1] // BLOCK):
        sl = slice(t * BLOCK, (t + 1) * BLOCK)
        parts.append(jnp.dot(hi[:, sl], ones, preferred_element_type=F32)
                     + jnp.dot(lo[:, sl], ones, preferred_element_type=F32))
    return parts[0] if len(parts) == 1 else jnp.concatenate(parts, axis=-1)


def _swap_halves(v):
    w = v.shape[1]
    half = HEAD_DIM // 2
    lane = lax.broadcasted_iota(jnp.int32, v.shape, 1) % HEAD_DIM
    return jnp.where(lane < half, pltpu.roll(v, w - half, 1), pltpu.roll(v, half, 1))


def _norm_rope(xv, gain, cos, sin):
    r = lax.rsqrt(_head_sum(xv * xv) * (1.0 / HEAD_DIM) + EPS)
    xn = xv * r * gain
    return xn * cos + _swap_halves(xn) * sin


def _norm_rope_bwd(dy, xv, gain, cos, sin):
    r = lax.rsqrt(_head_sum(xv * xv) * (1.0 / HEAD_DIM) + EPS)
    xh = xv * r
    dxn = dy * cos + _swap_halves(dy * sin)
    dgain = jnp.sum(dxn * xh, axis=0, keepdims=True)
    dxh = dxn * gain
    dx = r * (dxh - xh * (_head_sum(dxh * xh) * (1.0 / HEAD_DIM)))
    return dx, dgain


def _fold_heads(v):
    acc = v[:, 0:BLOCK]
    for t in range(1, v.shape[1] // BLOCK):
        acc = acc + v[:, t * BLOCK:(t + 1) * BLOCK]
    return acc + pltpu.roll(acc, HEAD_DIM, 1)


def _tile_lanes(v, width):
    return v if width == BLOCK else jnp.tile(v, (1, width // BLOCK))


def _low_half(rows):
    assert BLOCK == 2 * HEAD_DIM
    return lax.broadcasted_iota(jnp.int32, (rows, BLOCK), 1) < HEAD_DIM


def _spread_heads(v):
    low = _low_half(v.shape[0])
    out = []
    for t in range(v.shape[1] // BLOCK):
        tile = v[:, t * BLOCK:(t + 1) * BLOCK]
        swapped = pltpu.roll(tile, HEAD_DIM, 1)
        out += [jnp.where(low, tile, swapped), jnp.where(low, swapped, tile)]
    return jnp.concatenate(out, axis=-1)


def _gather_heads(v):
    low = _low_half(v.shape[0])
    out = []
    for t in range(v.shape[1] // (2 * BLOCK)):
        a, b = v[:, 2 * t * BLOCK:(2 * t + 1) * BLOCK], v[:, (2 * t + 1) * BLOCK:(2 * t + 2) * BLOCK]
        out.append(jnp.where(low, a + pltpu.roll(a, HEAD_DIM, 1), b + pltpu.roll(b, HEAD_DIM, 1)))
    return out[0] if len(out) == 1 else jnp.concatenate(out, axis=-1)


def _qk_prep(proj, qg, kg, cos, sin, AW, KW):
    S = proj.shape[0]
    tr = _tile(S, 256)
    scale = HEAD_DIM ** -0.5

    def body(q_ref, k_ref, v_ref, qg_ref, kg_ref, cos_ref, sin_ref, qo_ref, ko_ref, vo_ref):
        c, s = cos_ref[...], sin_ref[...]
        q = _norm_rope(q_ref[...], _tile_lanes(qg_ref[...], AW), _tile_lanes(c, AW), _tile_lanes(s, AW))
        k = _norm_rope(k_ref[...], _tile_lanes(kg_ref[...], KW), _tile_lanes(c, KW), _tile_lanes(s, KW))
        qo_ref[...] = (q * scale).astype(MXU)
        ko_ref[...] = _spread_heads(k).astype(MXU)
        vo_ref[...] = _spread_heads(v_ref[...]).astype(MXU)

    assert AW % KW == 0
    vec = pl.BlockSpec((1, BLOCK), lambda i: (0, 0))
    tab = pl.BlockSpec((tr, BLOCK), lambda i: (i, 0))
    wide = pl.BlockSpec((tr, 2 * KW), lambda i: (i, 0))
    return pl.pallas_call(
        body, name="qk_prep", grid=(S // tr,),
        in_specs=[pl.BlockSpec((tr, AW), lambda i: (i, 0)),
                  pl.BlockSpec((tr, KW), lambda i: (i, AW // KW)),
                  pl.BlockSpec((tr, KW), lambda i: (i, AW // KW + 1)), vec, vec, tab, tab],
        out_specs=[pl.BlockSpec((tr, AW), lambda i: (i, 0)), wide, wide],
        out_shape=[jax.ShapeDtypeStruct((S, AW), MXU), jax.ShapeDtypeStruct((S, 2 * KW), MXU),
                   jax.ShapeDtypeStruct((S, 2 * KW), MXU)],
        compiler_params=_params())(proj, proj, proj, qg, kg, cos, sin)


def _stack_heads(x, g, qpk):
    low = _low_half(BLOCK)
    parts = []
    for j in range(qpk):
        h = g * qpk + j
        tile = x[:, (h // 2) * BLOCK:(h // 2 + 1) * BLOCK]
        parts.append(jnp.where(low if h % 2 == 0 else jnp.logical_not(low), tile, jnp.zeros_like(tile)))
    return jnp.concatenate(parts, axis=0)


def _unstack_heads(y, qpk):
    low = _low_half(BLOCK)
    tiles = [jnp.where(low, y[2 * t * BLOCK:(2 * t + 1) * BLOCK], y[(2 * t + 1) * BLOCK:(2 * t + 2) * BLOCK])
             for t in range(qpk // 2)]
    return tiles[0] if len(tiles) == 1 else jnp.concatenate(tiles, axis=-1)


def _attn_probs(n, q, kp, kc, g, sink_ref, qpk):
    kcat = jnp.concatenate([kp[:, g * BLOCK:(g + 1) * BLOCK], kc[:, g * BLOCK:(g + 1) * BLOCK]], axis=0)
    qs = _stack_heads(q, g, qpk)
    s = lax.dot_general(qs, kcat, NT, preferred_element_type=F32)
    row = lax.broadcasted_iota(jnp.int32, (BLOCK, 2 * BLOCK), 0)
    col = lax.broadcasted_iota(jnp.int32, (BLOCK, 2 * BLOCK), 1)
    ok = (col > row) & (col <= row + BLOCK) & ((col >= BLOCK) | (n > 0))
    s = jnp.where(jnp.concatenate([ok] * qpk, axis=0), s, -1e30)
    sk = jnp.concatenate([jnp.full((BLOCK, 1), sink_ref[0, g * qpk + j], F32) for j in range(qpk)], axis=0)
    m = jnp.maximum(jnp.max(s, axis=-1, keepdims=True), sk)
    e = jnp.exp(s - m)
    es = jnp.exp(sk - m)
    z = jnp.sum(e, axis=-1, keepdims=True) + es
    return e / z, es / z, qs, kcat


def _attn_fwd(qr, kr, vb, sinks):
    S, AW = qr.shape
    KW = kr.shape[1]
    nb = S // BLOCK
    nkv = KW // BLOCK
    qpk = AW // (nkv * HEAD_DIM)
    assert qpk % 2 == 0

    def body(sink_ref, q_ref, kp_ref, kc_ref, vp_ref, vc_ref, o_ref):
        n = pl.program_id(0)
        q, kp, kc, vp, vc = q_ref[...], kp_ref[...], kc_ref[...], vp_ref[...], vc_ref[...]
        outs = []
        for g in range(nkv):
            p, _, _, _ = _attn_probs(n, q, kp, kc, g, sink_ref, qpk)
            vcat = jnp.concatenate([vp[:, g * BLOCK:(g + 1) * BLOCK], vc[:, g * BLOCK:(g + 1) * BLOCK]], axis=0)
            outs.append(_unstack_heads(jnp.dot(p.astype(MXU), vcat, preferred_element_type=F32), qpk))
        o_ref[...] = jnp.concatenate(outs, axis=-1).astype(MXU)

    cur = lambda n: (n, 0)
    prev = lambda n: (jnp.maximum(n - 1, 0), 0)
    return pl.pallas_call(
        body, name="attn_fwd", grid=(nb,),
        in_specs=[pl.BlockSpec(memory_space=pltpu.SMEM), pl.BlockSpec((BLOCK, AW), cur),
                  pl.BlockSpec((BLOCK, KW), prev), pl.BlockSpec((BLOCK, KW), cur),
                  pl.BlockSpec((BLOCK, KW), prev), pl.BlockSpec((BLOCK, KW), cur)],
        out_specs=pl.BlockSpec((BLOCK, AW), cur),
        out_shape=jax.ShapeDtypeStruct((S, AW), MXU), compiler_params=_params())(sinks, qr, kr, kr, vb, vb)


def _attn_bwd(qr, kr, vb, sinks, dattn):
    S, AW = qr.shape
    KW = kr.shape[1]
    nb = S // BLOCK
    nkv = KW // BLOCK
    qpk = AW // (nkv * HEAD_DIM)
    scale = HEAD_DIM ** -0.5

    def body(sink_ref, q_ref, kp_ref, kc_ref, vp_ref, vc_ref, do_ref,
             dq_ref, dkp_ref, dkc_ref, dvp_ref, dvc_ref, dsink_ref):
        n = pl.program_id(0)
        q, kp, kc, vp, vc = q_ref[...], kp_ref[...], kc_ref[...], vp_ref[...], vc_ref[...]
        do = do_ref[...].astype(MXU)
        lane = lax.broadcasted_iota(jnp.int32, (1, BLOCK), 1)
        dsink = jnp.zeros((1, BLOCK), F32)
        dqs, dkps, dkcs, dvps, dvcs = [], [], [], [], []
        for g in range(nkv):
            p, psink, qs, kcat = _attn_probs(n, q, kp, kc, g, sink_ref, qpk)
            vcat = jnp.concatenate([vp[:, g * BLOCK:(g + 1) * BLOCK], vc[:, g * BLOCK:(g + 1) * BLOCK]], axis=0)
            dos = _stack_heads(do, g, qpk)
            dp = lax.dot_general(dos, vcat, NT, preferred_element_type=F32)
            dv = lax.dot_general(p.astype(MXU), dos, TN, preferred_element_type=F32)
            delta = jnp.sum(p * dp, axis=-1, keepdims=True)
            ds = (p * (dp - delta)).astype(MXU)
            dsk = -psink * delta
            dqs.append(_unstack_heads(jnp.dot(ds, kcat, preferred_element_type=F32) * scale, qpk))
            dk = lax.dot_general(ds, qs, TN, preferred_element_type=F32)
            for j in range(qpk):
                tot = jnp.sum(dsk[j * BLOCK:(j + 1) * BLOCK], axis=0, keepdims=True)
                dsink = dsink + jnp.where(lane == g * qpk + j, tot, 0.0)
            dkps.append(dk[:BLOCK])
            dkcs.append(dk[BLOCK:])
            dvps.append(dv[:BLOCK])
            dvcs.append(dv[BLOCK:])
        dq_ref[...] = jnp.concatenate(dqs, axis=-1)
        dkp_ref[...] = jnp.concatenate(dkps, axis=-1)
        dkc_ref[...] = jnp.concatenate(dkcs, axis=-1)
        dvp_ref[...] = jnp.concatenate(dvps, axis=-1)
        dvc_ref[...] = jnp.concatenate(dvcs, axis=-1)

        @pl.when(n == 0)
        def _():
            dsink_ref[...] = jnp.zeros(dsink_ref.shape, F32)

        dsink_ref[...] += dsink

    cur = lambda n: (n, 0)
    prev = lambda n: (jnp.maximum(n - 1, 0), 0)
    kv = jax.ShapeDtypeStruct((S, KW), F32)
    kvspec = pl.BlockSpec((BLOCK, KW), cur)
    return pl.pallas_call(
        body, name="attn_bwd", grid=(nb,),
        in_specs=[pl.BlockSpec(memory_space=pltpu.SMEM), pl.BlockSpec((BLOCK, AW), cur),
                  pl.BlockSpec((BLOCK, KW), prev), kvspec, pl.BlockSpec((BLOCK, KW), prev), kvspec,
                  pl.BlockSpec((BLOCK, AW), cur)],
        out_specs=[pl.BlockSpec((BLOCK, AW), cur), kvspec, kvspec, kvspec, kvspec,
                   pl.BlockSpec((1, BLOCK), lambda n: (0, 0))],
        out_shape=[jax.ShapeDtypeStruct((S, AW), F32), kv, kv, kv, kv, jax.ShapeDtypeStruct((1, BLOCK), F32)],
        compiler_params=_params())(sinks, qr, kr, kr, vb, vb, dattn)


def _qk_prep_bwd(proj, qg, kg, cos, sin, dq, dkp, dkc, dvp, dvc, AW, KW):
    S = proj.shape[0]
    nb = S // BLOCK

    def body(q_ref, k_ref, qg_ref, kg_ref, cos_ref, sin_ref, dq_ref, dkp_ref, dkc_ref, dvp_ref, dvc_ref,
             o_ref, dqg_ref, dkg_ref):
        n = pl.program_id(0)
        c, s = cos_ref[...], sin_ref[...]
        has_next = jnp.where(n < nb - 1, 1.0, 0.0)
        dk = _gather_heads(dkc_ref[...] + has_next * dkp_ref[...])
        dv = _gather_heads(dvc_ref[...] + has_next * dvp_ref[...])
        dxq, dqg = _norm_rope_bwd(dq_ref[...], q_ref[...], _tile_lanes(qg_ref[...], AW),
                                  _tile_lanes(c, AW), _tile_lanes(s, AW))
        dxk, dkg = _norm_rope_bwd(dk, k_ref[...], _tile_lanes(kg_ref[...], KW),
                                  _tile_lanes(c, KW), _tile_lanes(s, KW))
        o_ref[...] = jnp.concatenate([dxq, dxk, dv], axis=-1).astype(MXU)

        @pl.when(n == 0)
        def _():
            dqg_ref[...] = jnp.zeros(dqg_ref.shape, F32)
            dkg_ref[...] = jnp.zeros(dkg_ref.shape, F32)

        dqg_ref[...] += _fold_heads(dqg)
        dkg_ref[...] += _fold_heads(dkg)

    cur = lambda n: (n, 0)
    nxt = lambda n: (jnp.minimum(n + 1, nb - 1), 0)
    vec = pl.BlockSpec((1, BLOCK), lambda n: (0, 0))
    tab = pl.BlockSpec((BLOCK, BLOCK), cur)
    return pl.pallas_call(
        body, name="qk_prep_bwd", grid=(nb,),
        in_specs=[pl.BlockSpec((BLOCK, AW), cur), pl.BlockSpec((BLOCK, KW), lambda n: (n, AW // KW)),
                  vec, vec, tab, tab, pl.BlockSpec((BLOCK, AW), cur),
                  pl.BlockSpec((BLOCK, 2 * KW), nxt), pl.BlockSpec((BLOCK, 2 * KW), cur),
                  pl.BlockSpec((BLOCK, 2 * KW), nxt), pl.BlockSpec((BLOCK, 2 * KW), cur)],
        out_specs=[pl.BlockSpec((BLOCK, AW + 2 * KW), cur), vec, vec],
        out_shape=[jax.ShapeDtypeStruct((S, AW + 2 * KW), MXU), jax.ShapeDtypeStruct((1, BLOCK), F32),
                   jax.ShapeDtypeStruct((1, BLOCK), F32)],
        compiler_params=_params())(proj, proj, qg, kg, cos, sin, dq, dkp, dkc, dvp, dvc)


SGU_LANES = 512
SGU_ROWS = 256


def _sgu_group(v, lng, lnb, w_f32, b):
    rows = v.shape[0]
    mu = jnp.mean(v, axis=-1, keepdims=True)
    vc = v - mu
    r = lax.rsqrt(jnp.mean(vc * vc, axis=-1, keepdims=True) + EPS)
    xh = vc * r
    vn = (xh * lng + lnb).astype(MXU)
    row = lax.broadcasted_iota(jnp.int32, (BLOCK, BLOCK), 0)
    col = lax.broadcasted_iota(jnp.int32, (BLOCK, BLOCK), 1)
    tri = row >= col
    w = jnp.where(tri, w_f32, 0.0).astype(MXU)
    chunks = [jnp.dot(w, vn[k * BLOCK:(k + 1) * BLOCK], preferred_element_type=F32) + b for k in range(rows // BLOCK)]
    s = chunks[0] if len(chunks) == 1 else jnp.concatenate(chunks, axis=0)
    return xh, r, vn, w, s, tri


def _sgu_layout(S, u_col):
    SW = SGU_GROUPS * BLOCK
    lb, tr = min(SGU_LANES, SW), min(SGU_ROWS, S)
    assert u_col % lb == 0 and SW % lb == 0 and S % tr == 0
    ub, nlb, gpb = u_col // lb, SW // lb, lb // BLOCK
    specs = [pl.BlockSpec((tr, lb), lambda j, i: (i, ub + j)), pl.BlockSpec((tr, lb), lambda j, i: (i, ub + nlb + j)),
             pl.BlockSpec((1, lb), lambda j, i: (0, j)), pl.BlockSpec((1, lb), lambda j, i: (0, j)),
             pl.BlockSpec((gpb, BLOCK, BLOCK), lambda j, i: (j, 0, 0)),
             pl.BlockSpec((gpb, BLOCK, 1), lambda j, i: (j, 0, 0))]
    return lb, tr, gpb, nlb, specs


def _sgu_fwd(proj, lng, lnb, ws, bs, u_col):
    S = proj.shape[0]
    lb, tr, gpb, nlb, specs = _sgu_layout(S, u_col)

    def body(pu_ref, pv_ref, lng_ref, lnb_ref, w_ref, b_ref, o_ref):
        u = _gelu(pu_ref[...])
        v = _gelu(pv_ref[...])
        outs = []
        for g in range(gpb):
            sl = slice(g * BLOCK, (g + 1) * BLOCK)
            s = _sgu_group(v[:, sl], lng_ref[:, sl], lnb_ref[:, sl], w_ref[g], b_ref[g])[4]
            outs.append(u[:, sl] * s)
        o_ref[...] = (outs[0] if gpb == 1 else jnp.concatenate(outs, axis=-1)).astype(MXU)

    return pl.pallas_call(
        body, name="sgu_fwd", grid=(nlb, S // tr), in_specs=specs,
        out_specs=pl.BlockSpec((tr, lb), lambda j, i: (i, j)),
        out_shape=jax.ShapeDtypeStruct((S, nlb * lb), MXU), compiler_params=_params())(proj, proj, lng, lnb, ws, bs)


def _sgu_bwd(proj, lng, lnb, ws, bs, dsgu, u_col, after):
    S = proj.shape[0]
    G = SGU_GROUPS
    lb, tr, gpb, nlb, specs = _sgu_layout(S, u_col)
    nch = tr // BLOCK

    def body(pu_ref, pv_ref, lng_ref, lnb_ref, w_ref, b_ref, do_ref, after_ref,
             dpu_ref, dpv_ref, dw_ref, db_ref, dlng_ref, dlnb_ref):
        pu, pv, do = pu_ref[...], pv_ref[...], do_ref[...]
        u = _gelu(pu)
        v = _gelu(pv)

        @pl.when(pl.program_id(1) == 0)
        def _():
            dw_ref[...] = jnp.zeros(dw_ref.shape, F32)
            db_ref[...] = jnp.zeros(db_ref.shape, F32)
            dlng_ref[...] = jnp.zeros(dlng_ref.shape, F32)
            dlnb_ref[...] = jnp.zeros(dlnb_ref.shape, F32)

        ss, dvs, dlng, dlnb = [], [], [], []
        for g in range(gpb):
            sl = slice(g * BLOCK, (g + 1) * BLOCK)
            xh, r, vn, w, s, tri = _sgu_group(v[:, sl], lng_ref[:, sl], lnb_ref[:, sl], w_ref[g], b_ref[g])
            ds = do[:, sl] * u[:, sl]
            dsb = ds.astype(MXU)
            dw, db, dvn = None, None, []
            for k in range(nch):
                rows = slice(k * BLOCK, (k + 1) * BLOCK)
                part = lax.dot_general(dsb[rows], vn[rows], NT, preferred_element_type=F32)
                dw = part if dw is None else dw + part
                rowsum = jnp.sum(ds[rows], axis=-1, keepdims=True)
                db = rowsum if db is None else db + rowsum
                dvn.append(lax.dot_general(w, dsb[rows], TN, preferred_element_type=F32))
            dvn = dvn[0] if nch == 1 else jnp.concatenate(dvn, axis=0)
            dw_ref[g] += jnp.where(tri, dw, 0.0)
            db_ref[g] += db
            dxh = dvn * lng_ref[:, sl]
            dvs.append(r * (dxh - jnp.mean(dxh, axis=-1, keepdims=True)
                            - xh * jnp.mean(dxh * xh, axis=-1, keepdims=True)))
            dlng.append(jnp.sum(dvn * xh, axis=0, keepdims=True))
            dlnb.append(jnp.sum(dvn, axis=0, keepdims=True))
            ss.append(s)
        cat = lambda parts: parts[0] if gpb == 1 else jnp.concatenate(parts, axis=-1)
        dpu_ref[...] = (do * cat(ss) * _gelu_grad(pu)).astype(MXU)
        dpv_ref[...] = (cat(dvs) * _gelu_grad(pv)).astype(MXU)
        dlng_ref[...] += cat(dlng)
        dlnb_ref[...] += cat(dlnb)

    tile = pl.BlockSpec((tr, lb), lambda j, i: (i, j))
    vec = pl.BlockSpec((1, lb), lambda j, i: (0, j))
    half = jax.ShapeDtypeStruct((S, G * BLOCK), MXU)
    return pl.pallas_call(
        body, name="sgu_bwd", grid=(nlb, S // tr), in_specs=specs + [tile, ANY],
        out_specs=[tile, tile, pl.BlockSpec((gpb, BLOCK, BLOCK), lambda j, i: (j, 0, 0)),
                   pl.BlockSpec((gpb, BLOCK, 1), lambda j, i: (j, 0, 0)), vec, vec],
        out_shape=[half, half, jax.ShapeDtypeStruct((G, BLOCK, BLOCK), F32),
                   jax.ShapeDtypeStruct((G, BLOCK, 1), F32),
                   jax.ShapeDtypeStruct((1, G * BLOCK), F32), jax.ShapeDtypeStruct((1, G * BLOCK), F32)],
        compiler_params=_params())(proj, proj, lng, lnb, ws, bs, dsgu, after)


def _store_f32(vals, extra, outs):
    for v, o in zip(vals, outs):
        o[...] = v


def _store_mxu(vals, extra, outs):
    for v, o in zip(vals, outs):
        o[...] = v.astype(MXU)


def _proj_in(h, w):
    S, D = h.shape
    Ns = w.shape[2]
    tm, tn = _tile(S, 1024), _tile(Ns, 1024)
    npb = Ns // tn
    return _mm("proj_in", (S // tm, N_CHIPS, npb), 0, [h, w],
               [pl.BlockSpec((tm, D), lambda i, s, j: (i, 0)), pl.BlockSpec((None, D, tn), lambda i, s, j: (s, 0, j))],
               [(0, 1, 0)], NN, 0, [jax.ShapeDtypeStruct((S, N_CHIPS * Ns), F32)],
               [pl.BlockSpec((tm, tn), lambda i, s, j: (i, s * npb + j))], [None], _store_f32)[0]


def _branches(attn, sgu, wa, ws, proj, gate0):
    S, AW = attn.shape
    SW = sgu.shape[1]
    Nb = wa.shape[2]
    D = N_CHIPS * Nb
    tm = _tile(S, 512)
    assert gate0 % Nb == 0
    ga, gb = gate0 // Nb, (gate0 + D) // Nb

    def epilogue(vals, extra, outs):
        a, b = vals
        outs[0][...] = (_sigmoid(extra[0][...]) * a + _sigmoid(extra[1][...]) * b).astype(MXU)
        outs[1][...] = a
        outs[2][...] = b

    tile = pl.BlockSpec((tm, Nb), lambda i, s: (i, s))
    wspec = lambda k: pl.BlockSpec((None, k, Nb), lambda i, s: (s, 0, 0))
    f = jax.ShapeDtypeStruct((S, D), F32)
    return _mm("branches", (S // tm, N_CHIPS), 0, [attn, sgu, wa, ws, proj, proj],
               [pl.BlockSpec((tm, AW), lambda i, s: (i, 0)), pl.BlockSpec((tm, SW), lambda i, s: (i, 0)),
                wspec(AW), wspec(SW), pl.BlockSpec((tm, Nb), lambda i, s: (i, ga + s)),
                pl.BlockSpec((tm, Nb), lambda i, s: (i, gb + s))],
               [(0, 2, 0), (1, 3, 1)], NN, 2, [jax.ShapeDtypeStruct((S, D), MXU), f, f], [tile] * 3,
               [None, None], epilogue, chunk=MXU_CHUNK)


def _rows_mm(name, a, w, res):
    S = a.shape[0]
    _, K, N = w.shape
    tm, tn = _tile(S, 1024), _tile(N, 1024)

    def epilogue(vals, extra, outs):
        outs[0][...] = extra[0][...] + vals[0]

    out = pl.BlockSpec((tm, tn), lambda i, j, s: (i, j))
    return _mm(name, (S // tm, N // tn, N_CHIPS), 1, [a, w, res],
               [pl.BlockSpec((tm, K), lambda i, j, s: (i, s)), pl.BlockSpec((None, K, tn), lambda i, j, s: (s, 0, j)), out],
               [(0, 1, 0)], NN, 1, [jax.ShapeDtypeStruct((S, N), F32)], [out], [(tm, tn)], epilogue)[0]


def _gate_up(h2, wg, wu):
    S, D = h2.shape
    Nf = wg.shape[2]
    tm = _tile(S, 256)

    def epilogue(vals, extra, outs):
        g, u = vals
        outs[0][...] = g
        outs[1][...] = u
        outs[2][...] = (g * _sigmoid(g) * u).astype(MXU)

    w = pl.BlockSpec((None, D, Nf), lambda s, i: (s, 0, 0))
    o = pl.BlockSpec((tm, Nf), lambda s, i: (i, s))
    f = jax.ShapeDtypeStruct((S, N_CHIPS * Nf), F32)
    return _mm("gate_up", (N_CHIPS, S // tm), 0, [h2, wg, wu],
               [pl.BlockSpec((tm, D), lambda s, i: (i, 0)), w, w], [(0, 1, 0), (0, 2, 1)], NN, 0,
               [f, f, jax.ShapeDtypeStruct((S, N_CHIPS * Nf), MXU)], [o, o, o], [None, None], epilogue,
               chunk=MXU_CHUNK)


def _down_bwd(dyb, wd, g, u):
    S, D = dyb.shape
    Kf = wd.shape[1]
    tm = _tile(S, 512)

    def epilogue(vals, extra, outs):
        da, gv, uv = vals[0], extra[0][...], extra[1][...]
        sg = _sigmoid(gv)
        outs[0][...] = (da * uv * sg * (1.0 + gv * (1.0 - sg))).astype(MXU)
        outs[1][...] = (da * gv * sg).astype(MXU)

    t = pl.BlockSpec((tm, Kf), lambda i, s: (i, s))
    o = jax.ShapeDtypeStruct((S, N_CHIPS * Kf), MXU)
    return _mm("down_bwd", (S // tm, N_CHIPS), 0, [dyb, wd, g, u],
               [pl.BlockSpec((tm, D), lambda i, s: (i, 0)), pl.BlockSpec((None, Kf, D), lambda i, s: (s, 0, 0)), t, t],
               [(0, 1, 0)], NT, 2, [o, o], [t, t], [None], epilogue, chunk=MXU_CHUNK)


def _out_bwd(dxb, wo, proj, ba, bb, gate0):
    S, D = dxb.shape
    Ko = wo.shape[1]
    tm = _tile(S, 512)
    assert gate0 % Ko == 0
    ga, gb = gate0 // Ko, (gate0 + D) // Ko

    def epilogue(vals, extra, outs):
        dm = vals[0]
        sa, sb = _sigmoid(extra[0][...]), _sigmoid(extra[1][...])
        outs[0][...] = (dm * sa).astype(MXU)
        outs[1][...] = (dm * sb).astype(MXU)
        outs[2][...] = (dm * extra[2][...] * sa * (1.0 - sa)).astype(MXU)
        outs[3][...] = (dm * extra[3][...] * sb * (1.0 - sb)).astype(MXU)

    t = pl.BlockSpec((tm, Ko), lambda i, s: (i, s))
    o = jax.ShapeDtypeStruct((S, D), MXU)
    return _mm("out_bwd", (S // tm, N_CHIPS), 0, [dxb, wo, proj, proj, ba, bb],
               [pl.BlockSpec((tm, D), lambda i, s: (i, 0)), pl.BlockSpec((None, Ko, D), lambda i, s: (s, 0, 0)),
                pl.BlockSpec((tm, Ko), lambda i, s: (i, ga + s)), pl.BlockSpec((tm, Ko), lambda i, s: (i, gb + s)), t, t],
               [(0, 1, 0)], NT, 4, [o] * 4, [t] * 4, [None], epilogue, chunk=MXU_CHUNK)


def _dx_cols(name, terms, n_out, after=None):
    S = terms[0][0].shape[0]
    _, K, Ns = terms[0][1].shape
    tm, tko, tn = _tile(S, 1024), _tile(K, 1024), _tile(Ns, 1920 if len(terms) == 1 else 1408)
    npb = Ns // tn
    operands, specs, pairs = [], [], []
    for t, (dy, w, k) in enumerate(terms):
        assert w.shape == (N_CHIPS, K, Ns)
        operands += [dy, w]
        specs += [pl.BlockSpec((tm, tn), lambda i, jk, s, jn: (i, s * npb + jn)),
                  pl.BlockSpec((None, tko, tn), lambda i, jk, s, jn: (s, jk, jn))]
        pairs.append((2 * t, 2 * t + 1, k))
    out = pl.BlockSpec((tm, tko), lambda i, jk, s, jn: (i, jk))
    return _mm(name, (S // tm, K // tko, N_CHIPS, npb), 2, operands, specs, pairs, NT, 0,
               [jax.ShapeDtypeStruct((S, K), F32)] * n_out, [out] * n_out, [(tm, tko)] * n_out, _store_f32, after)


def _dw_cols(name, a, dy):
    S, K = a.shape
    Ns = dy.shape[1] // N_CHIPS
    tk, tn = _tile(K, 512), _tile(Ns, 1408)
    npb = Ns // tn
    return _mm(name, (K // tk, N_CHIPS, npb), 0, [a, dy],
               [pl.BlockSpec((S, tk), lambda jk, s, jn: (0, jk)), pl.BlockSpec((S, tn), lambda jk, s, jn: (0, s * npb + jn))],
               [(0, 1, 0)], TN, 0, [jax.ShapeDtypeStruct((N_CHIPS, K, Ns), MXU)],
               [pl.BlockSpec((None, tk, tn), lambda jk, s, jn: (s, jk, jn))], [None], _store_mxu)[0]


def _dw_rows(name, a, dy):
    S = a.shape[0]
    K = a.shape[1] // N_CHIPS
    N = dy.shape[1]
    tk, tn = _tile(K, 1408), _tile(N, 1024)
    nkb = K // tk
    return _mm(name, (N_CHIPS, nkb, N // tn), 0, [a, dy],
               [pl.BlockSpec((S, tk), lambda s, jk, jn: (0, s * nkb + jk)), pl.BlockSpec((S, tn), lambda s, jk, jn: (0, jn))],
               [(0, 1, 0)], TN, 0, [jax.ShapeDtypeStruct((N_CHIPS, K, N), MXU)],
               [pl.BlockSpec((None, tk, tn), lambda s, jk, jn: (s, jk, jn))], [None], _store_mxu)[0]


def _layer_fwd(x, stream, layer, last, sp, cos, sin, dims):
    AW, KW, gate0, u_col = dims
    h = _rms_fwd("mix_norm", x, sp["mix_norm"])
    w = stream.finish(layer, 0, h)
    proj = _proj_in(h, w["w_in"])
    qr, kr, vb = _qk_prep(proj, sp["q_norm"], sp["k_norm"], cos, sin, AW, KW)
    stream.forward(layer, 1, qr)
    attn = _attn_fwd(qr, kr, vb, sp["sinks"])
    w.update(stream.finish(layer, 1, attn))
    sgu = _sgu_fwd(proj, sp["sgu_ln_g"], sp["sgu_ln_b"], sp["w_spatial"], sp["b_spatial"], u_col)
    merged, ba, bb = _branches(attn, sgu, w["w_attn_branch"], w["w_sgu_branch"], proj, gate0)
    stream.forward(layer, 2, merged)
    x1 = _rows_mm("out_proj", merged, w["w_out"], x)
    w.update(stream.finish(layer, 2, x1))
    h2 = _rms_fwd("ffn_norm", x1, sp["ffn_norm"])
    stream.forward(layer, 3, h2)
    g, u, act = _gate_up(h2, w["w_gate"], w["w_up"])
    w.update(stream.finish(layer, 3, g))
    x2 = _rows_mm("down_proj", act, w["w_down"], x1)
    if not last:
        stream.forward(layer + 1, 0, x2)
    saved = dict(x=x, h=h, proj=proj, qr=qr, kr=kr, vb=vb, attn=attn, sgu=sgu, merged=merged, ba=ba, bb=bb,
                 x1=x1, h2=h2, g=g, u=u, act=act)
    return x2, saved, w


def _layer_bwd(dy, dyb, w, sp, sv, cos, sin, dims, reducer, layer):
    AW, KW, gate0, u_col = dims
    big, small = {}, {}
    dg, du = _down_bwd(dyb, w["w_down"], sv["g"], sv["u"])
    big["w_down"] = _dw_rows("dw_down", sv["act"], dyb)
    big["w_gate"] = _dw_cols("dw_gate", sv["h2"], dg)
    big["w_up"] = _dw_cols("dw_up", sv["h2"], du)
    token = reducer.start(layer, 2, big)
    dh2 = _dx_cols("dh2", [(dg, w["w_gate"], 0), (du, w["w_up"], 0)], 1, token)[0]
    token = reducer.scatter(layer, 2, dh2)
    dx1, dx1b, small["ffn_norm"] = _rms_bwd("ffn_norm_bwd", dh2, sv["x1"], sp["ffn_norm"], dy, token)
    dba, dbb, dgla, dglb = _out_bwd(dx1b, w["w_out"], sv["proj"], sv["ba"], sv["bb"], gate0)
    big["w_out"] = _dw_rows("dw_out", sv["merged"], dx1b)
    big["w_attn_branch"] = _dw_cols("dw_attn_branch", sv["attn"], dba)
    big["w_sgu_branch"] = _dw_cols("dw_sgu_branch", sv["sgu"], dbb)
    token = reducer.start(layer, 1, big)
    dattn, dsgu = _dx_cols("dbranch_in", [(dba, w["w_attn_branch"], 0), (dbb, w["w_sgu_branch"], 1)], 2, token)
    token = reducer.scatter(layer, 1, dsgu)
    dpu, dpv, small["w_spatial"], db, small["sgu_ln_g"], small["sgu_ln_b"] = _sgu_bwd(
        sv["proj"], sp["sgu_ln_g"], sp["sgu_ln_b"], sp["w_spatial"], sp["b_spatial"], dsgu, u_col, token)
    small["b_spatial"] = db[:, :, 0]
    dq, dkp, dkc, dvp, dvc, dsink = _attn_bwd(sv["qr"], sv["kr"], sv["vb"], sp["sinks"], dattn)
    small["sinks"] = dsink[:, :sp["sinks"].shape[1]]
    dqkv, dqg, dkg = _qk_prep_bwd(sv["proj"], sp["q_norm"], sp["k_norm"], cos, sin, dq, dkp, dkc, dvp, dvc, AW, KW)
    small["q_norm"] = dqg[:, :HEAD_DIM]
    small["k_norm"] = dkg[:, :HEAD_DIM]
    dproj = jnp.concatenate([dqkv, dpu, dpv, dgla, dglb], axis=1)
    big["w_in"] = _dw_cols("dw_in", sv["h"], dproj)
    token = reducer.start(layer, 0, big)
    dh = _dx_cols("dh", [(dproj, w["w_in"], 0)], 1, token)[0]
    token = reducer.scatter(layer, 0, dh)
    dx, dxb, small["mix_norm"] = _rms_bwd("mix_norm_bwd", dh, sv["x"], sp["mix_norm"], dx1, token)
    return dx, dxb, small


def _place():
    x, y, c = lax.axis_index("x"), lax.axis_index("y"), lax.axis_index("c")
    chips = [(1 - x, y), (x, 1 - y), (1 - x, 1 - y)]
    return x, y, c, chips


def _half_rows(c, rows):
    h = rows // 2
    assert h % 16 == 0
    return pl.ds(pl.multiple_of(c * h, 16), h)


def _row_tile(rows, pref):
    best = None
    for t in range(16, min(rows, pref) + 1, 16):
        if rows % t == 0:
            best = t
    assert best is not None, rows
    return best


def _cast_own(name, chip, w, layer):
    _, R, C = w.shape
    tr = _row_tile(R, 512)

    def body(chip_ref, w_ref, o_ref):
        o_ref[...] = w_ref[...].astype(MXU)

    return pl.pallas_call(
        body, name=name, out_shape=jax.ShapeDtypeStruct((N_CHIPS, R, C), MXU),
        grid_spec=pltpu.PrefetchScalarGridSpec(
            num_scalar_prefetch=1, grid=(R // tr,),
            in_specs=[pl.BlockSpec((None, tr, C), lambda i, chip_ref: (layer, i, 0))],
            out_specs=pl.BlockSpec((None, tr, C), lambda i, chip_ref: (chip_ref[0], i, 0))),
        compiler_params=_params())(chip, w)


HBM = pl.BlockSpec(memory_space=pltpu.HBM)
SEM = pl.BlockSpec(memory_space=pltpu.SEMAPHORE)
DATAFLOW = pltpu.SideEffectType.DATAFLOW_SIDE_EFFECTING


def _gather_copies(bufs, send_sem, recv_sem):
    x, y, c, chips = _place()

    def ici(a, j, block):
        px, py = chips[j]
        blk = bufs[a].at[block, _half_rows(c, bufs[a].shape[1])]
        return pltpu.make_async_remote_copy(
            src_ref=blk, dst_ref=blk, send_sem=send_sem.at[3 * a + j], recv_sem=recv_sem.at[3 * a + j],
            device_id=(px, py, c), device_id_type=MESH)

    def d2d(a, j, core):
        px, py = chips[j]
        blk = bufs[a].at[2 * px + py, _half_rows(core, bufs[a].shape[1])]
        return pltpu.make_async_remote_copy(
            src_ref=blk, dst_ref=blk, send_sem=send_sem.at[3 * a + j], recv_sem=recv_sem.at[3 * a + j],
            device_id=(x, y, 1 - c), device_id_type=MESH)

    return ici, d2d


def _in_hbm(bufs):
    return [pltpu.with_memory_space_constraint(b, pltpu.HBM) for b in bufs]


def _gather_start(name, bufs, after):
    n = len(bufs)

    def body(*refs):
        dst = refs[n + 1:2 * n + 1]
        send_sem, recv_sem, token = refs[2 * n + 1:]
        x, y, c, chips = _place()
        ici, _ = _gather_copies(dst, send_sem, recv_sem)
        for a in range(n):
            for j in range(3):
                ici(a, j, 2 * x + y).start()
        token[...] = jnp.zeros(token.shape, token.dtype)

    sems = pltpu.SemaphoreType.DMA((3 * n,))
    outs = pl.pallas_call(
        body, name=name, in_specs=[HBM] * n + [ANY],
        out_specs=[HBM] * n + [SEM, SEM, pl.BlockSpec(memory_space=pltpu.VMEM)],
        out_shape=[pltpu.HBM(b.shape, b.dtype) for b in bufs] + [sems, sems, jax.ShapeDtypeStruct((8, BLOCK), F32)],
        input_output_aliases={a: a for a in range(n)},
        compiler_params=pltpu.CompilerParams(has_side_effects=DATAFLOW))(*_in_hbm(bufs), after)
    return outs[:n], outs[n], outs[n + 1], outs[n + 2]


def _gather_forward(name, bufs, ici_send, ici_recv, after):
    n = len(bufs)

    def body(*refs):
        ici_send_ref, ici_recv_ref = refs[n], refs[n + 1]
        dst = refs[n + 3:2 * n + 3]
        d2d_send, d2d_recv = refs[2 * n + 3:]
        x, y, c, chips = _place()
        ici, _ = _gather_copies(dst, ici_send_ref, ici_recv_ref)
        _, d2d = _gather_copies(dst, d2d_send, d2d_recv)
        for a in range(n):
            for j, (px, py) in enumerate(chips):
                ici(a, j, 2 * px + py).wait_recv()
                d2d(a, j, c).start()
        for a in range(n):
            for j in range(3):
                ici(a, j, 2 * x + y).wait_send()

    sems = pltpu.SemaphoreType.DMA((3 * n,))
    outs = pl.pallas_call(
        body, name=name, in_specs=[HBM] * n + [SEM, SEM, ANY], out_specs=[HBM] * n + [SEM, SEM],
        out_shape=[pltpu.HBM(b.shape, b.dtype) for b in bufs] + [sems, sems],
        input_output_aliases={a: a for a in range(n)},
        compiler_params=pltpu.CompilerParams(has_side_effects=DATAFLOW))(*bufs, ici_send, ici_recv, after)
    return outs[:n], outs[n], outs[n + 1]


def _gather_finish(name, bufs, d2d_send, d2d_recv, after):
    n = len(bufs)

    def body(*refs):
        send_ref, recv_ref = refs[n], refs[n + 1]
        dst = refs[n + 3:]
        x, y, c, chips = _place()
        _, d2d = _gather_copies(dst, send_ref, recv_ref)
        for a in range(n):
            for j in range(3):
                d2d(a, j, 1 - c).wait_recv()
                d2d(a, j, c).wait_send()

    return pl.pallas_call(
        body, name=name, in_specs=[HBM] * n + [SEM, SEM, ANY], out_specs=[HBM] * n,
        out_shape=[pltpu.HBM(b.shape, b.dtype) for b in bufs],
        input_output_aliases={a: a for a in range(n)},
        compiler_params=pltpu.CompilerParams(has_side_effects=DATAFLOW))(*bufs, d2d_send, d2d_recv, after)


GATHER = (("w_in",), ("w_attn_branch", "w_sgu_branch", "w_out"), ("w_gate", "w_up"), ("w_down",))
REDUCE = (("w_in",), ("w_attn_branch", "w_sgu_branch", "w_out"), ("w_gate", "w_up", "w_down"))


class _WeightStream:
    def __init__(self, started):
        self.started, self.passed = started, {}

    def forward(self, layer, group, after):
        bufs, send, recv = self.started[(layer, group)]
        self.passed[(layer, group)] = _gather_forward("gather_forward_%d_%d" % (layer, group), bufs, send, recv, after)

    def finish(self, layer, group, after):
        bufs, send, recv = self.passed[(layer, group)]
        done = _gather_finish("gather_finish_%d_%d" % (layer, group), bufs, send, recv, after)
        return dict(zip(GATHER[group], done))


def _pair_copies(grads, lands, send_sem, recv_sem):
    x, y, c, _ = _place()

    def make(a):
        theirs = _half_rows(1 - c, grads[a].shape[1])
        return pltpu.make_async_remote_copy(
            src_ref=grads[a].at[:, theirs], dst_ref=lands[a], send_sem=send_sem.at[a], recv_sem=recv_sem.at[a],
            device_id=(x, y, 1 - c), device_id_type=MESH)

    return make


def _pair_start(name, grads, after):
    n = len(grads)
    lands = [lax.empty((g.shape[0], g.shape[1] // 2, g.shape[2]), g.dtype) for g in grads]

    def body(*refs):
        src, dst = refs[2 * n + 1:3 * n + 1], refs[3 * n + 1:4 * n + 1]
        send_sem, recv_sem, token = refs[4 * n + 1:]
        copy = _pair_copies(src, dst, send_sem, recv_sem)
        for a in range(n):
            copy(a).start()
        token[...] = jnp.zeros(token.shape, token.dtype)

    sems = pltpu.SemaphoreType.DMA((n,))
    outs = pl.pallas_call(
        body, name=name, in_specs=[HBM] * (2 * n) + [ANY],
        out_specs=[HBM] * (2 * n) + [SEM, SEM, pl.BlockSpec(memory_space=pltpu.VMEM)],
        out_shape=[pltpu.HBM(b.shape, b.dtype) for b in grads + lands] + [sems, sems, jax.ShapeDtypeStruct((8, BLOCK), F32)],
        input_output_aliases={a: a for a in range(2 * n)},
        compiler_params=pltpu.CompilerParams(has_side_effects=DATAFLOW))(*_in_hbm(grads + lands), after)
    return outs[:n], outs[n:2 * n], outs[2 * n], outs[2 * n + 1], outs[2 * n + 2]


def _pair_finish(name, grads, lands, send_sem, recv_sem, after):
    n = len(grads)

    def body(*refs):
        send_ref, recv_ref = refs[2 * n], refs[2 * n + 1]
        src, dst = refs[2 * n + 3:3 * n + 3], refs[3 * n + 3:]
        copy = _pair_copies(src, dst, send_ref, recv_ref)
        for a in range(n):
            copy(a).wait_send()
            copy(a).wait_recv()

    outs = pl.pallas_call(
        body, name=name, in_specs=[HBM] * (2 * n) + [SEM, SEM, ANY], out_specs=[HBM] * (2 * n),
        out_shape=[pltpu.HBM(b.shape, b.dtype) for b in grads + lands],
        input_output_aliases={a: a for a in range(2 * n)},
        compiler_params=pltpu.CompilerParams(has_side_effects=DATAFLOW))(*grads, *lands, send_sem, recv_sem, after)
    return outs[:n], outs[n:]


def _pair_sum(name, core, g, p):
    _, h, C = p.shape
    tr = _row_tile(h, 512)
    nrb = h // tr

    def body(core_ref, g_ref, p_ref, o_ref):
        o_ref[...] = (g_ref[...].astype(F32) + p_ref[...].astype(F32)).astype(o_ref.dtype)

    spec = pl.BlockSpec((None, tr, C), lambda s, i, core_ref: (s, i, 0))
    return pl.pallas_call(
        body, name=name, out_shape=jax.ShapeDtypeStruct(p.shape, p.dtype),
        grid_spec=pltpu.PrefetchScalarGridSpec(
            num_scalar_prefetch=1, grid=(N_CHIPS, nrb),
            in_specs=[pl.BlockSpec((None, tr, C), lambda s, i, core_ref: (s, core_ref[0] * nrb + i, 0)), spec],
            out_specs=spec),
        compiler_params=_params())(core, g, p)


def _scatter_copies(sums, slots, send_sem, recv_sem):
    x, y, c, chips = _place()

    def make(a, j):
        px, py = chips[j]
        return pltpu.make_async_remote_copy(
            src_ref=sums[a].at[2 * px + py], dst_ref=slots[a].at[j], send_sem=send_sem.at[3 * a + j],
            recv_sem=recv_sem.at[3 * a + j], device_id=(px, py, c), device_id_type=MESH)

    return make


def _scatter_start(name, sums, after):
    n = len(sums)
    slots = [lax.empty((3,) + s.shape[1:], s.dtype) for s in sums]

    def body(*refs):
        src, dst = refs[2 * n + 1:3 * n + 1], refs[3 * n + 1:4 * n + 1]
        send_sem, recv_sem, token = refs[4 * n + 1:]
        copy = _scatter_copies(src, dst, send_sem, recv_sem)
        for a in range(n):
            for j in range(3):
                copy(a, j).start()
        token[...] = jnp.zeros(token.shape, token.dtype)

    sems = pltpu.SemaphoreType.DMA((3 * n,))
    outs = pl.pallas_call(
        body, name=name, in_specs=[HBM] * (2 * n) + [ANY],
        out_specs=[HBM] * (2 * n) + [SEM, SEM, pl.BlockSpec(memory_space=pltpu.VMEM)],
        out_shape=[pltpu.HBM(b.shape, b.dtype) for b in sums + slots] + [sems, sems, jax.ShapeDtypeStruct((8, BLOCK), F32)],
        input_output_aliases={a: a for a in range(2 * n)},
        compiler_params=pltpu.CompilerParams(has_side_effects=DATAFLOW))(*_in_hbm(sums + slots), after)
    return outs[:n], outs[n:2 * n], outs[2 * n], outs[2 * n + 1], outs[2 * n + 2]


def _scatter_finish(name, sums, slots, send_sem, recv_sem, after):
    n = len(sums)

    def body(*refs):
        send_ref, recv_ref = refs[2 * n], refs[2 * n + 1]
        src, dst = refs[2 * n + 3:3 * n + 3], refs[3 * n + 3:]
        copy = _scatter_copies(src, dst, send_ref, recv_ref)
        for a in range(n):
            for j in range(3):
                copy(a, j).wait_send()
                copy(a, j).wait_recv()

    outs = pl.pallas_call(
        body, name=name, in_specs=[HBM] * (2 * n) + [SEM, SEM, ANY], out_specs=[HBM] * (2 * n),
        out_shape=[pltpu.HBM(b.shape, b.dtype) for b in sums + slots],
        input_output_aliases={a: a for a in range(2 * n)},
        compiler_params=pltpu.CompilerParams(has_side_effects=DATAFLOW))(*sums, *slots, send_sem, recv_sem, after)
    return outs[:n], outs[n:]


def _slot_sum(name, place, slots, sums):
    _, h, C = slots.shape
    tr = _row_tile(h, 512)
    nrb = h // tr

    def body(place_ref, r0, r1, r2, own, o_ref):
        o_ref[...] = ((r0[...].astype(F32) + r1[...].astype(F32)) + r2[...].astype(F32)) + own[...].astype(F32)

    slot = lambda k: pl.BlockSpec((None, tr, C), lambda i, place_ref: (k, i, 0))
    return pl.pallas_call(
        body, name=name, out_shape=jax.ShapeDtypeStruct((2 * h, C), F32),
        grid_spec=pltpu.PrefetchScalarGridSpec(
            num_scalar_prefetch=1, grid=(nrb,),
            in_specs=[slot(0), slot(1), slot(2),
                      pl.BlockSpec((None, tr, C), lambda i, place_ref: (place_ref[0], i, 0))],
            out_specs=pl.BlockSpec((tr, C), lambda i, place_ref: (place_ref[1] * nrb + i, 0))),
        compiler_params=_params())(place, slots, slots, slots, sums)


def _half_copies(bufs, send_sem, recv_sem):
    x, y, c, _ = _place()

    def make(a, core):
        rows = bufs[a].at[_half_rows(core, bufs[a].shape[0])]
        return pltpu.make_async_remote_copy(
            src_ref=rows, dst_ref=rows, send_sem=send_sem.at[a], recv_sem=recv_sem.at[a],
            device_id=(x, y, 1 - c), device_id_type=MESH)

    return make


def _half_start(name, bufs, after):
    n = len(bufs)

    def body(*refs):
        dst = refs[n + 1:2 * n + 1]
        send_sem, recv_sem, token = refs[2 * n + 1:]
        c = lax.axis_index("c")
        copy = _half_copies(dst, send_sem, recv_sem)
        for a in range(n):
            copy(a, c).start()
        token[...] = jnp.zeros(token.shape, token.dtype)

    sems = pltpu.SemaphoreType.DMA((n,))
    outs = pl.pallas_call(
        body, name=name, in_specs=[HBM] * n + [ANY],
        out_specs=[HBM] * n + [SEM, SEM, pl.BlockSpec(memory_space=pltpu.VMEM)],
        out_shape=[pltpu.HBM(b.shape, b.dtype) for b in bufs] + [sems, sems, jax.ShapeDtypeStruct((8, BLOCK), F32)],
        input_output_aliases={a: a for a in range(n)},
        compiler_params=pltpu.CompilerParams(has_side_effects=DATAFLOW))(*_in_hbm(bufs), after)
    return outs[:n], outs[n], outs[n + 1], outs[n + 2]


def _half_finish(name, bufs, send_sem, recv_sem, after):
    n = len(bufs)

    def body(*refs):
        send_ref, recv_ref = refs[n], refs[n + 1]
        dst = refs[n + 3:]
        c = lax.axis_index("c")
        copy = _half_copies(dst, send_ref, recv_ref)
        for a in range(n):
            copy(a, c).wait_send()
            copy(a, 1 - c).wait_recv()

    return pl.pallas_call(
        body, name=name, in_specs=[HBM] * n + [SEM, SEM, ANY], out_specs=[HBM] * n,
        out_shape=[pltpu.HBM(b.shape, b.dtype) for b in bufs],
        input_output_aliases={a: a for a in range(n)},
        compiler_params=pltpu.CompilerParams(has_side_effects=DATAFLOW))(*bufs, send_sem, recv_sem, after)


class _GradReducer:
    def __init__(self, chip, core):
        self.core, self.place, self.pairs, self.started = core, jnp.concatenate([chip, core]), {}, []

    def start(self, layer, group, grads):
        mine = [grads[n] for n in REDUCE[group]]
        mine, lands, send, recv, token = _pair_start("grad_pair_start_%d_%d" % (layer, group), mine, self.place)
        self.pairs[(layer, group)] = (mine, lands, send, recv)
        return token

    def scatter(self, layer, group, after):
        tag = "%d_%d" % (layer, group)
        names = REDUCE[group]
        mine, lands, send, recv = self.pairs.pop((layer, group))
        mine, theirs = _pair_finish("grad_pair_finish_" + tag, mine, lands, send, recv, after)
        sums = [_pair_sum("pair_sum_%s_%d" % (n, layer), self.core, g, p) for n, g, p in zip(names, mine, theirs)]
        sums, slots, send, recv, token = _scatter_start("grad_scatter_start_" + tag, sums, self.place)
        self.started.append((layer, names, sums, slots, send, recv))
        return token

    def finish(self, after, update):
        for layer in sorted({entry[0] for entry in self.started}, reverse=True):
            exchanged = []
            for lyr, names, sums, slots, send, recv in self.started:
                if lyr != layer:
                    continue
                tag = "%s_%d" % (names[0], layer)
                sums, slots = _scatter_finish("grad_scatter_finish_" + tag, sums, slots, send, recv, after)
                halves = [_slot_sum("slot_sum_%s_%d" % (n, layer), self.place, r, s)
                          for n, r, s in zip(names, slots, sums)]
                halves, send, recv, after = _half_start("grad_half_start_" + tag, halves, self.place)
                exchanged.append((tag, names, halves, send, recv))
            for tag, names, halves, send, recv in exchanged:
                whole = _half_finish("grad_half_finish_" + tag, halves, send, recv, after)
                after = update(layer, dict(zip(names, whole)))


def _all_reduce_small(v):
    rows = v.shape[0]
    n_dev = 2 * N_CHIPS

    def body(x_ref, out_ref, gat_ref, send_sems, recv_sems, local_sem):
        x, y, c, chips = _place()
        me, sibling = (x, y, c), (x, y, 1 - c)

        def slot(px, py, pc):
            return gat_ref.at[4 * px + 2 * py + pc]

        def copy(k, block, to, src=None):
            return pltpu.make_async_remote_copy(
                src_ref=slot(*block) if src is None else src, dst_ref=slot(*block), send_sem=send_sems.at[k],
                recv_sem=recv_sems.at[k], device_id=to, device_id_type=MESH)

        mine = pltpu.make_async_copy(x_ref, slot(*me), local_sem)
        mine.start()
        first = [copy(0, me, sibling, src=x_ref)]
        first += [copy(1 + j, me, (*chip, c), src=x_ref) for j, chip in enumerate(chips)]
        for cp in first:
            cp.start()
        passed = [copy(4 + j, (*chip, c), sibling) for j, chip in enumerate(chips)]
        for j, chip in enumerate(chips):
            copy(1 + j, (*chip, c), me).wait_recv()
            passed[j].start()
        copy(0, sibling, me).wait_recv()
        for j, chip in enumerate(chips):
            copy(4 + j, (*chip, 1 - c), me).wait_recv()
        for cp in first + passed:
            cp.wait_send()
        mine.wait()
        acc = gat_ref[0]
        for d in range(1, n_dev):
            acc = acc + gat_ref[d]
        out_ref[...] = acc

    vm = pl.BlockSpec(memory_space=pltpu.VMEM)
    return pl.pallas_call(
        body, name="small_grad_all_reduce", in_specs=[vm], out_specs=vm,
        out_shape=jax.ShapeDtypeStruct(v.shape, F32),
        scratch_shapes=[pltpu.VMEM((n_dev, rows, BLOCK), F32), pltpu.SemaphoreType.DMA((7,)),
                        pltpu.SemaphoreType.DMA((7,)), pltpu.SemaphoreType.DMA],
        compiler_params=_params())(v)


def _adamw_math(w, g, m, v):
    m2 = ADAM_B1 * m + (1.0 - ADAM_B1) * g
    v2 = ADAM_B2 * v + (1.0 - ADAM_B2) * (g * g)
    m_hat = m2 / (1.0 - ADAM_B1 ** ADAM_STEP)
    v_hat = v2 / (1.0 - ADAM_B2 ** ADAM_STEP)
    delta = -ADAM_LR * (m_hat / (jnp.sqrt(v_hat) + ADAM_EPS) + ADAM_WD * w)
    return delta, m2, v2


def _adamw_big(name, layer, grad, w, m, v, others):
    L, R, C = w.shape
    tr = _row_tile(R, 256)

    def body(g_ref, w_ref, m_ref, v_ref, *rest):
        go_ref, d_ref, mo_ref, vo_ref = rest[-4:]
        g = g_ref[...]
        delta, m2, v2 = _adamw_math(w_ref[...], g, m_ref[...], v_ref[...])
        go_ref[...] = g
        d_ref[...] = delta
        mo_ref[...] = m2
        vo_ref[...] = v2

    blk = pl.BlockSpec((None, tr, C), lambda i: (layer, i, 0))
    shp = jax.ShapeDtypeStruct(w.shape, F32)
    others = [] if others is None else list(others)
    return pl.pallas_call(
        body, name=name, grid=(R // tr,),
        in_specs=[pl.BlockSpec((tr, C), lambda i: (i, 0))] + [blk] * 3 + [ANY] * len(others), out_specs=[blk] * 4,
        out_shape=[shp] * 4, input_output_aliases={4 + k: k for k in range(len(others))},
        compiler_params=_params())(grad, w, m, v, *others)


def _adamw_small(g, w, m, v):
    rows = g.shape[0]
    tr = _row_tile(rows, 512)

    def body(g_ref, w_ref, m_ref, v_ref, d_ref, mo_ref, vo_ref):
        delta, m2, v2 = _adamw_math(w_ref[...], g_ref[...], m_ref[...], v_ref[...])
        d_ref[...] = delta
        mo_ref[...] = m2
        vo_ref[...] = v2

    blk = pl.BlockSpec((tr, BLOCK), lambda i: (i, 0))
    shp = jax.ShapeDtypeStruct(g.shape, F32)
    return pl.pallas_call(
        body, name="adamw_small", grid=(rows // tr,), in_specs=[blk] * 4, out_specs=[blk] * 3, out_shape=[shp] * 3,
        compiler_params=_params())(g, w, m, v)


def _pack(arrays):
    flat = jnp.concatenate([a.reshape(-1) for a in arrays])
    pad = (-flat.shape[0]) % (16 * BLOCK)
    return jnp.pad(flat, (0, pad)).reshape(-1, BLOCK)


def _unpack(packed, like):
    flat = packed.reshape(-1)
    out, off = [], 0
    for a in like:
        out.append(flat[off:off + a.size].reshape(a.shape))
        off += a.size
    return out


BIG = ("w_in", "w_attn_branch", "w_sgu_branch", "w_out", "w_gate", "w_up", "w_down")
SMALL = ("mix_norm", "q_norm", "k_norm", "sinks", "sgu_ln_g", "sgu_ln_b", "w_spatial", "b_spatial", "ffn_norm")
ORDER = ("mix_norm", "w_in", "q_norm", "k_norm", "sinks", "sgu_ln_g", "sgu_ln_b", "w_spatial", "b_spatial",
         "w_attn_branch", "w_sgu_branch", "w_out", "ffn_norm", "w_gate", "w_up", "w_down")


def _rope_tables(seq):
    pos = jnp.arange(seq, dtype=F32)
    inv_freq = jnp.power(10000.0, -jnp.arange(0, HEAD_DIM, 2, dtype=F32) / HEAD_DIM)
    ang = pos[:, None] * inv_freq[None, :]
    cos, sin = jnp.cos(ang), jnp.sin(ang)
    reps = BLOCK // HEAD_DIM
    return (jnp.tile(jnp.concatenate([cos, cos], axis=1), (1, reps)),
            jnp.tile(jnp.concatenate([-sin, sin], axis=1), (1, reps)))


def kernel(x, mix_norm, w_in, q_norm, k_norm, sinks, sgu_ln_g, sgu_ln_b, w_spatial, b_spatial, w_attn_branch, w_sgu_branch, w_out, ffn_norm, w_gate, w_up, w_down, loss_target, m_mix_norm, m_w_in, m_q_norm, m_k_norm, m_sinks, m_sgu_ln_g, m_sgu_ln_b, m_w_spatial, m_b_spatial, m_w_attn_branch, m_w_sgu_branch, m_w_out, m_ffn_norm, m_w_gate, m_w_up, m_w_down, v_mix_norm, v_w_in, v_q_norm, v_k_norm, v_sinks, v_sgu_ln_g, v_sgu_ln_b, v_w_spatial, v_b_spatial, v_w_attn_branch, v_w_sgu_branch, v_w_out, v_ffn_norm, v_w_gate, v_w_up, v_w_down):
    weights = dict(mix_norm=mix_norm, w_in=w_in, q_norm=q_norm, k_norm=k_norm, sinks=sinks, sgu_ln_g=sgu_ln_g,
                   sgu_ln_b=sgu_ln_b, w_spatial=w_spatial, b_spatial=b_spatial, w_attn_branch=w_attn_branch,
                   w_sgu_branch=w_sgu_branch, w_out=w_out, ffn_norm=ffn_norm, w_gate=w_gate, w_up=w_up, w_down=w_down)
    mom1 = dict(mix_norm=m_mix_norm, w_in=m_w_in, q_norm=m_q_norm, k_norm=m_k_norm, sinks=m_sinks,
                sgu_ln_g=m_sgu_ln_g, sgu_ln_b=m_sgu_ln_b, w_spatial=m_w_spatial, b_spatial=m_b_spatial,
                w_attn_branch=m_w_attn_branch, w_sgu_branch=m_w_sgu_branch, w_out=m_w_out, ffn_norm=m_ffn_norm,
                w_gate=m_w_gate, w_up=m_w_up, w_down=m_w_down)
    mom2 = dict(mix_norm=v_mix_norm, w_in=v_w_in, q_norm=v_q_norm, k_norm=v_k_norm, sinks=v_sinks,
                sgu_ln_g=v_sgu_ln_g, sgu_ln_b=v_sgu_ln_b, w_spatial=v_w_spatial, b_spatial=v_b_spatial,
                w_attn_branch=v_w_attn_branch, w_sgu_branch=v_w_sgu_branch, w_out=v_w_out, ffn_norm=v_ffn_norm,
                w_gate=v_w_gate, w_up=v_w_up, w_down=v_w_down)
    xs, target = x[0], loss_target[0]
    S, D = xs.shape
    L = w_in.shape[0]
    AW, KW, SW = N_Q_HEADS * HEAD_DIM, N_KV_HEADS * HEAD_DIM, SGU_GROUPS * BLOCK
    dims = (AW, KW, AW + 2 * KW + 2 * SW, AW + 2 * KW)
    cos, sin = _rope_tables(S)
    reps = BLOCK // HEAD_DIM

    chip = (2 * lax.axis_index("x") + lax.axis_index("y")).astype(jnp.int32).reshape(1)
    core = lax.axis_index("c").astype(jnp.int32).reshape(1)
    started, token = {}, chip
    for l in range(L):
        for gi, names in enumerate(GATHER):
            bufs = [_cast_own("cast_%s_%d" % (n, l), chip, weights[n], l) for n in names]
            bufs, send, recv, token = _gather_start("gather_start_%d_%d" % (l, gi), bufs, token)
            started[(l, gi)] = (bufs, send, recv)
    stream = _WeightStream(started)
    stream.forward(0, 0, token)
    sp = [dict(mix_norm=mix_norm[l][None], ffn_norm=ffn_norm[l][None], q_norm=jnp.tile(q_norm[l][None], (1, reps)),
               k_norm=jnp.tile(k_norm[l][None], (1, reps)), sinks=sinks[l][None], sgu_ln_g=sgu_ln_g[l][None],
               sgu_ln_b=sgu_ln_b[l][None], w_spatial=w_spatial[l], b_spatial=b_spatial[l][:, :, None])
          for l in range(L)]

    act, saved, wl = xs, [], []
    for l in range(L):
        act, sv, w_all = _layer_fwd(act, stream, l, l == L - 1, sp[l], cos, sin, dims)
        saved.append(sv)
        wl.append(w_all)
    loss_part, dy, dyb = _loss_head(act, target)
    loss = lax.psum(loss_part[0, 0], ("x", "y", "c"))

    reducer = _GradReducer(chip, core)
    small_g = [None] * L
    for l in reversed(range(L)):
        dy, dyb, small_g[l] = _layer_bwd(dy, dyb, wl[l], sp[l], saved[l], cos, sin, dims, reducer, l)
    grad_x = dy[None]

    updated = {}

    def update(layer, reduced):
        for n, g in reduced.items():
            updated[n] = _adamw_big("adamw_%s_%d" % (n, layer), layer, g, weights[n], mom1[n], mom2[n],
                                    updated.get(n))
        return updated[n][0]

    reducer.finish(dy, update)
    grads, deltas, new_m, new_v = {}, {}, {}, {}
    for n in BIG:
        grads[n], deltas[n], new_m[n], new_v[n] = updated[n]

    small_like = [weights[n] for n in SMALL]
    local = [jnp.stack([small_g[l][n].reshape(weights[n].shape[1:]) for l in range(L)]) for n in SMALL]
    g_small = _all_reduce_small(_pack(local))
    d_small, m_small, v_small = _adamw_small(g_small, _pack(small_like), _pack([mom1[n] for n in SMALL]),
                                             _pack([mom2[n] for n in SMALL]))
    for n, g, d, m2, v2 in zip(SMALL, _unpack(g_small, small_like), _unpack(d_small, small_like),
                               _unpack(m_small, small_like), _unpack(v_small, small_like)):
        grads[n], deltas[n], new_m[n], new_v[n] = g, d, m2, v2

    return (loss, grad_x, *[grads[n] for n in ORDER], *[deltas[n] for n in ORDER],
            *[new_m[n] for n in ORDER], *[new_v[n] for n in ORDER])
```

```python
import functools

import jax
import jax.numpy as jnp
from jax import lax
from jax.experimental import pallas as pl
from jax.experimental.pallas import tpu as pltpu

HEAD_DIM = 64
N_Q_HEADS = 16
N_KV_HEADS = 4
SGU_GROUPS = 8
BLOCK = 128
EPS = 1e-6
ADAM_LR = 0.001
ADAM_B1 = 0.9
ADAM_B2 = 0.999
ADAM_EPS = 1e-08
ADAM_WD = 0.01
ADAM_STEP = 10
N_CHIPS = 4
VMEM_LIMIT = 52 * 1024 * 1024
MXU_CHUNK = 256
ATTN_STACK = 4

F32 = jnp.float32
MXU = jnp.bfloat16
NN = (((1,), (0,)), ((), ()))
NT = (((1,), (1,)), ((), ()))
TN = (((0,), (0,)), ((), ()))
MESH = pl.DeviceIdType.MESH
ANY = pl.BlockSpec(memory_space=pl.ANY)


def _tile(n, pref):
    if n <= pref:
        return n
    best = None
    for t in range(BLOCK, pref + 1, BLOCK):
        if n % t == 0:
            best = t
    assert best is not None, (n, pref)
    return best


def _params():
    return pltpu.CompilerParams(vmem_limit_bytes=VMEM_LIMIT)


def _mm(name, grid, n_red, operands, specs, pairs, dims, n_extra, out_shapes, out_specs,
        acc_shapes, epilogue, after=None, chunk=None):
    n_op = len(operands) - n_extra
    n_out = len(out_shapes)
    n_acc = len(acc_shapes)
    if after is not None:
        operands, specs = list(operands) + [after], list(specs) + [ANY]
    n_in = len(operands)
    axes = [ax for ax in range(len(grid) - n_red, len(grid)) if grid[ax] > 1]

    def body(*refs):
        ops = refs[:n_op]
        extra = refs[n_op:n_op + n_extra]
        outs = refs[n_in:n_in + n_out]
        accs = refs[n_in + n_out:]

        def prod(a, b, cols=None):
            rhs = ops[b]
            if cols is not None:
                rhs = rhs.at[:, cols] if dims == NN else rhs.at[cols, :]
            return lax.dot_general(ops[a][...], rhs[...], dims, preferred_element_type=F32)

        def products(cols=None):
            vals = [None] * n_acc
            for a, b, k in pairs:
                d = prod(a, b, cols)
                vals[k] = d if vals[k] is None else vals[k] + d
            return vals

        if not axes and chunk is not None:
            width = outs[0].shape[-1]
            for c0 in range(0, width, chunk):
                cols = pl.ds(c0, min(chunk, width - c0))
                epilogue(products(cols), [e.at[:, cols] for e in extra], [o.at[:, cols] for o in outs])
        elif not axes:
            epilogue(products(), extra, outs)
        else:
            first = pl.program_id(axes[0]) == 0
            last = pl.program_id(axes[0]) == grid[axes[0]] - 1
            for ax in axes[1:]:
                first = jnp.logical_and(first, pl.program_id(ax) == 0)
                last = jnp.logical_and(last, pl.program_id(ax) == grid[ax] - 1)

            @pl.when(first)
            def _():
                for acc in accs:
                    acc[...] = jnp.zeros(acc.shape, F32)

            for a, b, k in pairs:
                accs[k][...] += prod(a, b)

            @pl.when(last)
            def _():
                epilogue([acc[...] for acc in accs], extra, outs)

    scratch = [pltpu.VMEM(s, F32) for s in acc_shapes] if axes else []
    return pl.pallas_call(
        body, name=name, grid=grid, in_specs=specs, out_specs=out_specs, out_shape=out_shapes,
        scratch_shapes=scratch, compiler_params=_params())(*operands)


def _sigmoid(x):
    return 1.0 / (1.0 + jnp.exp(-x))


_GELU_C = 0.7978845608028654
_GELU_A = 0.044715


def _gelu(x):
    return 0.5 * x * (1.0 + jnp.tanh(_GELU_C * (x + _GELU_A * x * x * x)))


def _gelu_grad(x):
    t = jnp.tanh(_GELU_C * (x + _GELU_A * x * x * x))
    return 0.5 * (1.0 + t) + 0.5 * x * (1.0 - t * t) * _GELU_C * (1.0 + 3.0 * _GELU_A * x * x)


def _rms_fwd(name, x, g):
    S, D = x.shape
    tr = _tile(S, 256)

    def body(x_ref, g_ref, o_ref):
        xv = x_ref[...]
        r = lax.rsqrt(jnp.mean(xv * xv, axis=-1, keepdims=True) + EPS)
        o_ref[...] = (xv * r * g_ref[...]).astype(MXU)

    return pl.pallas_call(
        body, name=name, grid=(S // tr,),
        in_specs=[pl.BlockSpec((tr, D), lambda i: (i, 0)), pl.BlockSpec((1, D), lambda i: (0, 0))],
        out_specs=pl.BlockSpec((tr, D), lambda i: (i, 0)),
        out_shape=jax.ShapeDtypeStruct((S, D), MXU), compiler_params=_params())(x, g)


def _rms_bwd(name, dh, x, g, dres, after):
    S, D = x.shape
    tr = _tile(S, 256)

    def body(dh_ref, x_ref, g_ref, dres_ref, after_ref, dx_ref, dxb_ref, dg_ref):
        xv = x_ref[...]
        r = lax.rsqrt(jnp.mean(xv * xv, axis=-1, keepdims=True) + EPS)
        xh = xv * r
        dhv = dh_ref[...]
        dy = dhv * g_ref[...]
        dx = dres_ref[...] + r * (dy - xh * jnp.mean(dy * xh, axis=-1, keepdims=True))
        dx_ref[...] = dx
        dxb_ref[...] = dx.astype(MXU)

        @pl.when(pl.program_id(0) == 0)
        def _():
            dg_ref[...] = jnp.zeros(dg_ref.shape, F32)

        dg_ref[...] += jnp.sum(dhv * xh, axis=0, keepdims=True)

    row = pl.BlockSpec((tr, D), lambda i: (i, 0))
    vec = pl.BlockSpec((1, D), lambda i: (0, 0))
    return pl.pallas_call(
        body, name=name, grid=(S // tr,), in_specs=[row, row, vec, row, ANY], out_specs=[row, row, vec],
        out_shape=[jax.ShapeDtypeStruct((S, D), F32), jax.ShapeDtypeStruct((S, D), MXU),
                   jax.ShapeDtypeStruct((1, D), F32)],
        compiler_params=_params())(dh, x, g, dres, after)


def _loss_head(y, target):
    S, D = y.shape
    tr = _tile(S, 256)

    def body(y_ref, t_ref, loss_ref, dy_ref, dyb_ref):
        d = y_ref[...] - t_ref[...]
        dy = d * (1.0 / D)
        dy_ref[...] = dy
        dyb_ref[...] = dy.astype(MXU)

        @pl.when(pl.program_id(0) == 0)
        def _():
            loss_ref[...] = jnp.zeros(loss_ref.shape, F32)

        loss_ref[...] += (0.5 / D) * jnp.sum(jnp.sum(d * d, axis=-1, keepdims=True), axis=0, keepdims=True)

    row = pl.BlockSpec((tr, D), lambda i: (i, 0))
    return pl.pallas_call(
        body, name="loss_head", grid=(S // tr,), in_specs=[row, row],
        out_specs=[pl.BlockSpec((1, 1), lambda i: (0, 0)), row, row],
        out_shape=[jax.ShapeDtypeStruct((1, 1), F32), jax.ShapeDtypeStruct((S, D), F32),
                   jax.ShapeDtypeStruct((S, D), MXU)],
        compiler_params=_params())(y, target)


def _head_sum(v):
    r = lax.broadcasted_iota(jnp.int32, (BLOCK, BLOCK), 0) // HEAD_DIM
    c = lax.broadcasted_iota(jnp.int32, (BLOCK, BLOCK), 1) // HEAD_DIM
    ones = jnp.where(r == c, 1.0, 0.0).astype(jnp.bfloat16)
    hi = v.astype(jnp.bfloat16)
    lo = (v - hi.astype(F32)).astype(jnp.bfloat16)
    parts = []
    for t in range(v.shape[1] // BLOCK):
        sl = slice(t * BLOCK, (t + 1) * BLOCK)
        parts.append(jnp.dot(hi[:, sl], ones, preferred_element_type=F32)
                     + jnp.dot(lo[:, sl], ones, preferred_element_type=F32))
    return parts[0] if len(parts) == 1 else jnp.concatenate(parts, axis=-1)


def _swap_halves(v):
    w = v.shape[1]
    half = HEAD_DIM // 2
    lane = lax.broadcasted_iota(jnp.int32, v.shape, 1) % HEAD_DIM
    return jnp.where(lane < half, pltpu.roll(v, w - half, 1), pltpu.roll(v, half, 1))


def _norm_rope(xv, gain, cos, sin):
    r = lax.rsqrt(_head_sum(xv * xv) * (1.0 / HEAD_DIM) + EPS)
    xn = xv * r * gain
    return xn * cos + _swap_halves(xn) * sin


def _norm_rope_bwd(dy, xv, gain, cos, sin):
    r = lax.rsqrt(_head_sum(xv * xv) * (1.0 / HEAD_DIM) + EPS)
    xh = xv * r
    dxn = dy * cos + _swap_halves(dy * sin)
    dgain = jnp.sum(dxn * xh, axis=0, keepdims=True)
    dxh = dxn * gain
    dx = r * (dxh - xh * (_head_sum(dxh * xh) * (1.0 / HEAD_DIM)))
    return dx, dgain


def _fold_heads(v):
    acc = v[:, 0:BLOCK]
    for t in range(1, v.shape[1] // BLOCK):
        acc = acc + v[:, t * BLOCK:(t + 1) * BLOCK]
    return acc + pltpu.roll(acc, HEAD_DIM, 1)


def _tile_lanes(v, width):
    return v if width == BLOCK else jnp.tile(v, (1, width // BLOCK))


def _low_half(rows):
    assert BLOCK == 2 * HEAD_DIM
    return lax.broadcasted_iota(jnp.int32, (rows, BLOCK), 1) < HEAD_DIM


def _spread_heads(v):
    low = _low_half(v.shape[0])
    out = []
    for t in range(v.shape[1] // BLOCK):
        tile = v[:, t * BLOCK:(t + 1) * BLOCK]
        swapped = pltpu.roll(tile, HEAD_DIM, 1)
        out += [jnp.where(low, tile, swapped), jnp.where(low, swapped, tile)]
    return jnp.concatenate(out, axis=-1)


def _gather_heads(v):
    low = _low_half(v.shape[0])
    out = []
    for t in range(v.shape[1] // (2 * BLOCK)):
        a, b = v[:, 2 * t * BLOCK:(2 * t + 1) * BLOCK], v[:, (2 * t + 1) * BLOCK:(2 * t + 2) * BLOCK]
        out.append(jnp.where(low, a + pltpu.roll(a, HEAD_DIM, 1), b + pltpu.roll(b, HEAD_DIM, 1)))
    return out[0] if len(out) == 1 else jnp.concatenate(out, axis=-1)


def _qk_prep(proj, qg, kg, cos, sin, AW, KW):
    S = proj.shape[0]
    tr = _tile(S, 256)
    scale = HEAD_DIM ** -0.5

    def body(q_ref, k_ref, v_ref, qg_ref, kg_ref, cos_ref, sin_ref, qo_ref, ko_ref, vo_ref):
        c, s = cos_ref[...], sin_ref[...]
        q = _norm_rope(q_ref[...], _tile_lanes(qg_ref[...], AW), _tile_lanes(c, AW), _tile_lanes(s, AW))
        k = _norm_rope(k_ref[...], _tile_lanes(kg_ref[...], KW), _tile_lanes(c, KW), _tile_lanes(s, KW))
        qo_ref[...] = (q * scale).astype(MXU)
        ko_ref[...] = _spread_heads(k).astype(MXU)
        vo_ref[...] = _spread_heads(v_ref[...]).astype(MXU)

    assert AW % KW == 0
    vec = pl.BlockSpec((1, BLOCK), lambda i: (0, 0))
    tab = pl.BlockSpec((tr, BLOCK), lambda i: (i, 0))
    wide = pl.BlockSpec((tr, 2 * KW), lambda i: (i, 0))
    return pl.pallas_call(
        body, name="qk_prep", grid=(S // tr,),
        in_specs=[pl.BlockSpec((tr, AW), lambda i: (i, 0)),
                  pl.BlockSpec((tr, KW), lambda i: (i, AW // KW)),
                  pl.BlockSpec((tr, KW), lambda i: (i, AW // KW + 1)), vec, vec, tab, tab],
        out_specs=[pl.BlockSpec((tr, AW), lambda i: (i, 0)), wide, wide],
        out_shape=[jax.ShapeDtypeStruct((S, AW), MXU), jax.ShapeDtypeStruct((S, 2 * KW), MXU),
                   jax.ShapeDtypeStruct((S, 2 * KW), MXU)],
        compiler_params=_params())(proj, proj, proj, qg, kg, cos, sin)


def _stack_heads(x, h0, nh):
    low = _low_half(BLOCK)
    parts = []
    for h in range(h0, h0 + nh):
        tile = x[:, (h // 2) * BLOCK:(h // 2 + 1) * BLOCK]
        parts.append(jnp.where(low if h % 2 == 0 else jnp.logical_not(low), tile, jnp.zeros_like(tile)))
    return jnp.concatenate(parts, axis=0)


def _unstack_heads(y):
    low = _low_half(BLOCK)
    tiles = [jnp.where(low, y[2 * t * BLOCK:(2 * t + 1) * BLOCK], y[(2 * t + 1) * BLOCK:(2 * t + 2) * BLOCK])
             for t in range(y.shape[0] // (2 * BLOCK))]
    return tiles[0] if len(tiles) == 1 else jnp.concatenate(tiles, axis=-1)


def _band_t(n):
    key = lax.broadcasted_iota(jnp.int32, (2 * BLOCK, BLOCK), 0)
    qry = lax.broadcasted_iota(jnp.int32, (2 * BLOCK, BLOCK), 1)
    return (key > qry) & (key <= qry + BLOCK) & ((key >= BLOCK) | (n > 0))


def _attn_probs_t(ok, qs, kcat, h0, nh, sink_ref):
    st = jnp.where(ok, lax.dot_general(kcat, qs, NT, preferred_element_type=F32), -1e30)
    sk = jnp.concatenate([jnp.full((1, BLOCK), sink_ref[0, h], F32) for h in range(h0, h0 + nh)], axis=1)
    m = jnp.maximum(jnp.max(st, axis=0, keepdims=True), sk)
    e = jnp.exp(st - m)
    es = jnp.exp(sk - m)
    rz = 1.0 / (jnp.sum(e, axis=0, keepdims=True) + es)
    return e * rz, es * rz, rz


def _attn_fwd(qr, kr, vb, sinks):
    S, AW = qr.shape
    KW = kr.shape[1]
    nb = S // BLOCK
    nkv = KW // BLOCK
    qpk = AW // (nkv * HEAD_DIM)
    nh = min(ATTN_STACK, qpk)
    assert nh % 2 == 0 and qpk % nh == 0

    def body(sink_ref, q_ref, kp_ref, kc_ref, vp_ref, vc_ref, o_ref):
        n = pl.program_id(0)
        q, kp, kc, vp, vc = q_ref[...], kp_ref[...], kc_ref[...], vp_ref[...], vc_ref[...]
        ok = jnp.concatenate([_band_t(n)] * nh, axis=1)
        outs = []
        for g in range(nkv):
            kcat = jnp.concatenate([kp[:, g * BLOCK:(g + 1) * BLOCK], kc[:, g * BLOCK:(g + 1) * BLOCK]], axis=0)
            vcat = jnp.concatenate([vp[:, g * BLOCK:(g + 1) * BLOCK], vc[:, g * BLOCK:(g + 1) * BLOCK]], axis=0)
            for h0 in range(g * qpk, (g + 1) * qpk, nh):
                pt, _, _ = _attn_probs_t(ok, _stack_heads(q, h0, nh), kcat, h0, nh, sink_ref)
                outs.append(_unstack_heads(lax.dot_general(pt.astype(MXU), vcat, TN, preferred_element_type=F32)))
        o_ref[...] = jnp.concatenate(outs, axis=-1).astype(MXU)

    cur = lambda n: (n, 0)
    prev = lambda n: (jnp.maximum(n - 1, 0), 0)
    return pl.pallas_call(
        body, name="attn_fwd", grid=(nb,),
        in_specs=[pl.BlockSpec(memory_space=pltpu.SMEM), pl.BlockSpec((BLOCK, AW), cur),
                  pl.BlockSpec((BLOCK, KW), prev), pl.BlockSpec((BLOCK, KW), cur),
                  pl.BlockSpec((BLOCK, KW), prev), pl.BlockSpec((BLOCK, KW), cur)],
        out_specs=pl.BlockSpec((BLOCK, AW), cur),
        out_shape=jax.ShapeDtypeStruct((S, AW), MXU), compiler_params=_params())(sinks, qr, kr, kr, vb, vb)


def _attn_bwd(qr, kr, vb, sinks, dattn):
    S, AW = qr.shape
    KW = kr.shape[1]
    nb = S // BLOCK
    nkv = KW // BLOCK
    qpk = AW // (nkv * HEAD_DIM)
    nh = min(ATTN_STACK, qpk)
    scale = HEAD_DIM ** -0.5

    def body(sink_ref, q_ref, kp_ref, kc_ref, vp_ref, vc_ref, do_ref,
             dq_ref, dkp_ref, dkc_ref, dvp_ref, dvc_ref, dsink_ref):
        n = pl.program_id(0)
        q, kp, kc, vp, vc = q_ref[...], kp_ref[...], kc_ref[...], vp_ref[...], vc_ref[...]
        do = do_ref[...].astype(MXU)
        lane = lax.broadcasted_iota(jnp.int32, (1, BLOCK), 1)
        ok = jnp.concatenate([_band_t(n)] * nh, axis=1)
        dsink = jnp.zeros((1, BLOCK), F32)
        dqs, dkps, dkcs, dvps, dvcs = [], [], [], [], []
        for g in range(nkv):
            kcat = jnp.concatenate([kp[:, g * BLOCK:(g + 1) * BLOCK], kc[:, g * BLOCK:(g + 1) * BLOCK]], axis=0)
            vcat = jnp.concatenate([vp[:, g * BLOCK:(g + 1) * BLOCK], vc[:, g * BLOCK:(g + 1) * BLOCK]], axis=0)
            dk, dv = None, None
            for h0 in range(g * qpk, (g + 1) * qpk, nh):
                qs = _stack_heads(q, h0, nh)
                dos = _stack_heads(do, h0, nh)
                pt, ps, _ = _attn_probs_t(ok, qs, kcat, h0, nh, sink_ref)
                dpt = lax.dot_general(vcat, dos, NT, preferred_element_type=F32)
                delta = jnp.sum(pt * dpt, axis=0, keepdims=True)
                dst = (pt * (dpt - delta)).astype(MXU)
                dsk = -ps * delta
                dv_part = jnp.dot(pt.astype(MXU), dos, preferred_element_type=F32)
                dk_part = jnp.dot(dst, qs, preferred_element_type=F32)
                dqs.append(_unstack_heads(lax.dot_general(dst, kcat, TN, preferred_element_type=F32) * scale))
                dk = dk_part if dk is None else dk + dk_part
                dv = dv_part if dv is None else dv + dv_part
                for j in range(nh):
                    tot = jnp.sum(dsk[:, j * BLOCK:(j + 1) * BLOCK], axis=1, keepdims=True)
                    dsink = dsink + jnp.where(lane == h0 + j, tot, 0.0)
            dkps.append(dk[:BLOCK])
            dkcs.append(dk[BLOCK:])
            dvps.append(dv[:BLOCK])
            dvcs.append(dv[BLOCK:])
        dq_ref[...] = jnp.concatenate(dqs, axis=-1)
        dkp_ref[...] = jnp.concatenate(dkps, axis=-1)
        dkc_ref[...] = jnp.concatenate(dkcs, axis=-1)
        dvp_ref[...] = jnp.concatenate(dvps, axis=-1)
        dvc_ref[...] = jnp.concatenate(dvcs, axis=-1)

        @pl.when(n == 0)
        def _():
            dsink_ref[...] = jnp.zeros(dsink_ref.shape, F32)

        dsink_ref[...] += dsink

    cur = lambda n: (n, 0)
    prev = lambda n: (jnp.maximum(n - 1, 0), 0)
    kv = jax.ShapeDtypeStruct((S, KW), F32)
    kvspec = pl.BlockSpec((BLOCK, KW), cur)
    return pl.pallas_call(
        body, name="attn_bwd", grid=(nb,),
        in_specs=[pl.BlockSpec(memory_space=pltpu.SMEM), pl.BlockSpec((BLOCK, AW), cur),
                  pl.BlockSpec((BLOCK, KW), prev), kvspec, pl.BlockSpec((BLOCK, KW), prev), kvspec,
                  pl.BlockSpec((BLOCK, AW), cur)],
        out_specs=[pl.BlockSpec((BLOCK, AW), cur), kvspec, kvspec, kvspec, kvspec,
                   pl.BlockSpec((1, BLOCK), lambda n: (0, 0))],
        out_shape=[jax.ShapeDtypeStruct((S, AW), F32), kv, kv, kv, kv, jax.ShapeDtypeStruct((1, BLOCK), F32)],
        compiler_params=_params())(sinks, qr, kr, kr, vb, vb, dattn)


def _qk_prep_bwd(proj, qg, kg, cos, sin, dq, dkp, dkc, dvp, dvc, AW, KW):
    S = proj.shape[0]
    nb = S // BLOCK

    def body(q_ref, k_ref, qg_ref, kg_ref, cos_ref, sin_ref, dq_ref, dkp_ref, dkc_ref, dvp_ref, dvc_ref,
             o_ref, dqg_ref, dkg_ref):
        n = pl.program_id(0)
        c, s = cos_ref[...], sin_ref[...]
        has_next = jnp.where(n < nb - 1, 1.0, 0.0)
        dk = _gather_heads(dkc_ref[...] + has_next * dkp_ref[...])
        dv = _gather_heads(dvc_ref[...] + has_next * dvp_ref[...])
        dxq, dqg = _norm_rope_bwd(dq_ref[...], q_ref[...], _tile_lanes(qg_ref[...], AW),
                                  _tile_lanes(c, AW), _tile_lanes(s, AW))
        dxk, dkg = _norm_rope_bwd(dk, k_ref[...], _tile_lanes(kg_ref[...], KW),
                                  _tile_lanes(c, KW), _tile_lanes(s, KW))
        o_ref[...] = jnp.concatenate([dxq, dxk, dv], axis=-1).astype(MXU)

        @pl.when(n == 0)
        def _():
            dqg_ref[...] = jnp.zeros(dqg_ref.shape, F32)
            dkg_ref[...] = jnp.zeros(dkg_ref.shape, F32)

        dqg_ref[...] += _fold_heads(dqg)
        dkg_ref[...] += _fold_heads(dkg)

    cur = lambda n: (n, 0)
    nxt = lambda n: (jnp.minimum(n + 1, nb - 1), 0)
    vec = pl.BlockSpec((1, BLOCK), lambda n: (0, 0))
    tab = pl.BlockSpec((BLOCK, BLOCK), cur)
    return pl.pallas_call(
        body, name="qk_prep_bwd", grid=(nb,),
        in_specs=[pl.BlockSpec((BLOCK, AW), cur), pl.BlockSpec((BLOCK, KW), lambda n: (n, AW // KW)),
                  vec, vec, tab, tab, pl.BlockSpec((BLOCK, AW), cur),
                  pl.BlockSpec((BLOCK, 2 * KW), nxt), pl.BlockSpec((BLOCK, 2 * KW), cur),
                  pl.BlockSpec((BLOCK, 2 * KW), nxt), pl.BlockSpec((BLOCK, 2 * KW), cur)],
        out_specs=[pl.BlockSpec((BLOCK, AW + 2 * KW), cur), vec, vec],
        out_shape=[jax.ShapeDtypeStruct((S, AW + 2 * KW), MXU), jax.ShapeDtypeStruct((1, BLOCK), F32),
                   jax.ShapeDtypeStruct((1, BLOCK), F32)],
        compiler_params=_params())(proj, proj, qg, kg, cos, sin, dq, dkp, dkc, dvp, dvc)


SGU_LANES = 512
SGU_ROWS = 256


def _sgu_group(v, lng, lnb, w_f32, b):
    rows = v.shape[0]
    mu = jnp.mean(v, axis=-1, keepdims=True)
    vc = v - mu
    r = lax.rsqrt(jnp.mean(vc * vc, axis=-1, keepdims=True) + EPS)
    xh = vc * r
    vn = (xh * lng + lnb).astype(MXU)
    row = lax.broadcasted_iota(jnp.int32, (BLOCK, BLOCK), 0)
    col = lax.broadcasted_iota(jnp.int32, (BLOCK, BLOCK), 1)
    tri = row >= col
    w = jnp.where(tri, w_f32, 0.0).astype(MXU)
    chunks = [jnp.dot(w, vn[k * BLOCK:(k + 1) * BLOCK], preferred_element_type=F32) + b for k in range(rows // BLOCK)]
    s = chunks[0] if len(chunks) == 1 else jnp.concatenate(chunks, axis=0)
    return xh, r, vn, w, s, tri


def _sgu_layout(S, u_col):
    SW = SGU_GROUPS * BLOCK
    lb, tr = min(SGU_LANES, SW), min(SGU_ROWS, S)
    assert u_col % lb == 0 and SW % lb == 0 and S % tr == 0
    ub, nlb, gpb = u_col // lb, SW // lb, lb // BLOCK
    specs = [pl.BlockSpec((tr, lb), lambda j, i: (i, ub + j)), pl.BlockSpec((tr, lb), lambda j, i: (i, ub + nlb + j)),
             pl.BlockSpec((1, lb), lambda j, i: (0, j)), pl.BlockSpec((1, lb), lambda j, i: (0, j)),
             pl.BlockSpec((gpb, BLOCK, BLOCK), lambda j, i: (j, 0, 0)),
             pl.BlockSpec((gpb, BLOCK, 1), lambda j, i: (j, 0, 0))]
    return lb, tr, gpb, nlb, specs


def _sgu_fwd(proj, lng, lnb, ws, bs, u_col):
    S = proj.shape[0]
    lb, tr, gpb, nlb, specs = _sgu_layout(S, u_col)

    def body(pu_ref, pv_ref, lng_ref, lnb_ref, w_ref, b_ref, o_ref):
        u = _gelu(pu_ref[...])
        v = _gelu(pv_ref[...])
        outs = []
        for g in range(gpb):
            sl = slice(g * BLOCK, (g + 1) * BLOCK)
            s = _sgu_group(v[:, sl], lng_ref[:, sl], lnb_ref[:, sl], w_ref[g], b_ref[g])[4]
            outs.append(u[:, sl] * s)
        o_ref[...] = (outs[0] if gpb == 1 else jnp.concatenate(outs, axis=-1)).astype(MXU)

    return pl.pallas_call(
        body, name="sgu_fwd", grid=(nlb, S // tr), in_specs=specs,
        out_specs=pl.BlockSpec((tr, lb), lambda j, i: (i, j)),
        out_shape=jax.ShapeDtypeStruct((S, nlb * lb), MXU), compiler_params=_params())(proj, proj, lng, lnb, ws, bs)


def _sgu_bwd(proj, lng, lnb, ws, bs, dsgu, u_col, after):
    S = proj.shape[0]
    G = SGU_GROUPS
    lb, tr, gpb, nlb, specs = _sgu_layout(S, u_col)
    nch = tr // BLOCK

    def body(pu_ref, pv_ref, lng_ref, lnb_ref, w_ref, b_ref, do_ref, after_ref,
             dpu_ref, dpv_ref, dw_ref, db_ref, dlng_ref, dlnb_ref):
        pu, pv, do = pu_ref[...], pv_ref[...], do_ref[...]
        u = _gelu(pu)
        v = _gelu(pv)

        @pl.when(pl.program_id(1) == 0)
        def _():
            dw_ref[...] = jnp.zeros(dw_ref.shape, F32)
            db_ref[...] = jnp.zeros(db_ref.shape, F32)
            dlng_ref[...] = jnp.zeros(dlng_ref.shape, F32)
            dlnb_ref[...] = jnp.zeros(dlnb_ref.shape, F32)

        ss, dvs, dlng, dlnb = [], [], [], []
        for g in range(gpb):
            sl = slice(g * BLOCK, (g + 1) * BLOCK)
            xh, r, vn, w, s, tri = _sgu_group(v[:, sl], lng_ref[:, sl], lnb_ref[:, sl], w_ref[g], b_ref[g])
            ds = do[:, sl] * u[:, sl]
            dsb = ds.astype(MXU)
            dw, db, dvn = None, None, []
            for k in range(nch):
                rows = slice(k * BLOCK, (k + 1) * BLOCK)
                part = lax.dot_general(dsb[rows], vn[rows], NT, preferred_element_type=F32)
                dw = part if dw is None else dw + part
                rowsum = jnp.sum(ds[rows], axis=-1, keepdims=True)
                db = rowsum if db is None else db + rowsum
                dvn.append(lax.dot_general(w, dsb[rows], TN, preferred_element_type=F32))
            dvn = dvn[0] if nch == 1 else jnp.concatenate(dvn, axis=0)
            dw_ref[g] += jnp.where(tri, dw, 0.0)
            db_ref[g] += db
            dxh = dvn * lng_ref[:, sl]
            dvs.append(r * (dxh - jnp.mean(dxh, axis=-1, keepdims=True)
                            - xh * jnp.mean(dxh * xh, axis=-1, keepdims=True)))
            dlng.append(jnp.sum(dvn * xh, axis=0, keepdims=True))
            dlnb.append(jnp.sum(dvn, axis=0, keepdims=True))
            ss.append(s)
        cat = lambda parts: parts[0] if gpb == 1 else jnp.concatenate(parts, axis=-1)
        dpu_ref[...] = (do * cat(ss) * _gelu_grad(pu)).astype(MXU)
        dpv_ref[...] = (cat(dvs) * _gelu_grad(pv)).astype(MXU)
        dlng_ref[...] += cat(dlng)
        dlnb_ref[...] += cat(dlnb)

    tile = pl.BlockSpec((tr, lb), lambda j, i: (i, j))
    vec = pl.BlockSpec((1, lb), lambda j, i: (0, j))
    half = jax.ShapeDtypeStruct((S, G * BLOCK), MXU)
    return pl.pallas_call(
        body, name="sgu_bwd", grid=(nlb, S // tr), in_specs=specs + [tile, ANY],
        out_specs=[tile, tile, pl.BlockSpec((gpb, BLOCK, BLOCK), lambda j, i: (j, 0, 0)),
                   pl.BlockSpec((gpb, BLOCK, 1), lambda j, i: (j, 0, 0)), vec, vec],
        out_shape=[half, half, jax.ShapeDtypeStruct((G, BLOCK, BLOCK), F32),
                   jax.ShapeDtypeStruct((G, BLOCK, 1), F32),
                   jax.ShapeDtypeStruct((1, G * BLOCK), F32), jax.ShapeDtypeStruct((1, G * BLOCK), F32)],
        compiler_params=_params())(proj, proj, lng, lnb, ws, bs, dsgu, after)


def _store_f32(vals, extra, outs):
    for v, o in zip(vals, outs):
        o[...] = v


def _store_mxu(vals, extra, outs):
    for v, o in zip(vals, outs):
        o[...] = v.astype(MXU)


def _proj_in(h, w):
    S, D = h.shape
    Ns = w.shape[2]
    tm, tn = _tile(S, 1024), _tile(Ns, 1024)
    npb = Ns // tn
    return _mm("proj_in", (S // tm, N_CHIPS, npb), 0, [h, w],
               [pl.BlockSpec((tm, D), lambda i, s, j: (i, 0)), pl.BlockSpec((None, D, tn), lambda i, s, j: (s, 0, j))],
               [(0, 1, 0)], NN, 0, [jax.ShapeDtypeStruct((S, N_CHIPS * Ns), F32)],
               [pl.BlockSpec((tm, tn), lambda i, s, j: (i, s * npb + j))], [None], _store_f32)[0]


def _branches(attn, sgu, wa, ws, proj, gate0):
    S, AW = attn.shape
    SW = sgu.shape[1]
    Nb = wa.shape[2]
    D = N_CHIPS * Nb
    tm = _tile(S, 512)
    assert gate0 % Nb == 0
    ga, gb = gate0 // Nb, (gate0 + D) // Nb

    def epilogue(vals, extra, outs):
        a, b = vals
        outs[0][...] = (_sigmoid(extra[0][...]) * a + _sigmoid(extra[1][...]) * b).astype(MXU)
        outs[1][...] = a
        outs[2][...] = b

    tile = pl.BlockSpec((tm, Nb), lambda i, s: (i, s))
    wspec = lambda k: pl.BlockSpec((None, k, Nb), lambda i, s: (s, 0, 0))
    f = jax.ShapeDtypeStruct((S, D), F32)
    return _mm("branches", (S // tm, N_CHIPS), 0, [attn, sgu, wa, ws, proj, proj],
               [pl.BlockSpec((tm, AW), lambda i, s: (i, 0)), pl.BlockSpec((tm, SW), lambda i, s: (i, 0)),
                wspec(AW), wspec(SW), pl.BlockSpec((tm, Nb), lambda i, s: (i, ga + s)),
                pl.BlockSpec((tm, Nb), lambda i, s: (i, gb + s))],
               [(0, 2, 0), (1, 3, 1)], NN, 2, [jax.ShapeDtypeStruct((S, D), MXU), f, f], [tile] * 3,
               [None, None], epilogue, chunk=MXU_CHUNK)


def _rows_mm(name, a, w, res):
    S = a.shape[0]
    _, K, N = w.shape
    tm, tn = _tile(S, 1024), _tile(N, 1024)

    def epilogue(vals, extra, outs):
        outs[0][...] = extra[0][...] + vals[0]

    out = pl.BlockSpec((tm, tn), lambda i, j, s: (i, j))
    return _mm(name, (S // tm, N // tn, N_CHIPS), 1, [a, w, res],
               [pl.BlockSpec((tm, K), lambda i, j, s: (i, s)), pl.BlockSpec((None, K, tn), lambda i, j, s: (s, 0, j)), out],
               [(0, 1, 0)], NN, 1, [jax.ShapeDtypeStruct((S, N), F32)], [out], [(tm, tn)], epilogue)[0]


def _gate_up(h2, wg, wu):
    S, D = h2.shape
    Nf = wg.shape[2]
    tm = _tile(S, 256)

    def epilogue(vals, extra, outs):
        g, u = vals
        outs[0][...] = g
        outs[1][...] = u
        outs[2][...] = (g * _sigmoid(g) * u).astype(MXU)

    w = pl.BlockSpec((None, D, Nf), lambda s, i: (s, 0, 0))
    o = pl.BlockSpec((tm, Nf), lambda s, i: (i, s))
    f = jax.ShapeDtypeStruct((S, N_CHIPS * Nf), F32)
    return _mm("gate_up", (N_CHIPS, S // tm), 0, [h2, wg, wu],
               [pl.BlockSpec((tm, D), lambda s, i: (i, 0)), w, w], [(0, 1, 0), (0, 2, 1)], NN, 0,
               [f, f, jax.ShapeDtypeStruct((S, N_CHIPS * Nf), MXU)], [o, o, o], [None, None], epilogue,
               chunk=MXU_CHUNK)


def _down_bwd(dyb, wd, g, u):
    S, D = dyb.shape
    Kf = wd.shape[1]
    tm = _tile(S, 512)

    def epilogue(vals, extra, outs):
        da, gv, uv = vals[0], extra[0][...], extra[1][...]
        sg = _sigmoid(gv)
        outs[0][...] = (da * uv * sg * (1.0 + gv * (1.0 - sg))).astype(MXU)
        outs[1][...] = (da * gv * sg).astype(MXU)

    t = pl.BlockSpec((tm, Kf), lambda i, s: (i, s))
    o = jax.ShapeDtypeStruct((S, N_CHIPS * Kf), MXU)
    return _mm("down_bwd", (S // tm, N_CHIPS), 0, [dyb, wd, g, u],
               [pl.BlockSpec((tm, D), lambda i, s: (i, 0)), pl.BlockSpec((None, Kf, D), lambda i, s: (s, 0, 0)), t, t],
               [(0, 1, 0)], NT, 2, [o, o], [t, t], [None], epilogue, chunk=MXU_CHUNK)


def _out_bwd(dxb, wo, proj, ba, bb, gate0):
    S, D = dxb.shape
    Ko = wo.shape[1]
    tm = _tile(S, 512)
    assert gate0 % Ko == 0
    ga, gb = gate0 // Ko, (gate0 + D) // Ko

    def epilogue(vals, extra, outs):
        dm = vals[0]
        sa, sb = _sigmoid(extra[0][...]), _sigmoid(extra[1][...])
        outs[0][...] = (dm * sa).astype(MXU)
        outs[1][...] = (dm * sb).astype(MXU)
        outs[2][...] = (dm * extra[2][...] * sa * (1.0 - sa)).astype(MXU)
        outs[3][...] = (dm * extra[3][...] * sb * (1.0 - sb)).astype(MXU)

    t = pl.BlockSpec((tm, Ko), lambda i, s: (i, s))
    o = jax.ShapeDtypeStruct((S, D), MXU)
    return _mm("out_bwd", (S // tm, N_CHIPS), 0, [dxb, wo, proj, proj, ba, bb],
               [pl.BlockSpec((tm, D), lambda i, s: (i, 0)), pl.BlockSpec((None, Ko, D), lambda i, s: (s, 0, 0)),
                pl.BlockSpec((tm, Ko), lambda i, s: (i, ga + s)), pl.BlockSpec((tm, Ko), lambda i, s: (i, gb + s)), t, t],
               [(0, 1, 0)], NT, 4, [o] * 4, [t] * 4, [None], epilogue, chunk=MXU_CHUNK)


def _dx_cols(name, terms, n_out, after=None):
    S = terms[0][0].shape[0]
    _, K, Ns = terms[0][1].shape
    tm, tko, tn = _tile(S, 1024), _tile(K, 1024), _tile(Ns, 1920 if len(terms) == 1 else 1408)
    npb = Ns // tn
    operands, specs, pairs = [], [], []
    for t, (dy, w, k) in enumerate(terms):
        assert w.shape == (N_CHIPS, K, Ns)
        operands += [dy, w]
        specs += [pl.BlockSpec((tm, tn), lambda i, jk, s, jn: (i, s * npb + jn)),
                  pl.BlockSpec((None, tko, tn), lambda i, jk, s, jn: (s, jk, jn))]
        pairs.append((2 * t, 2 * t + 1, k))
    out = pl.BlockSpec((tm, tko), lambda i, jk, s, jn: (i, jk))
    return _mm(name, (S // tm, K // tko, N_CHIPS, npb), 2, operands, specs, pairs, NT, 0,
               [jax.ShapeDtypeStruct((S, K), F32)] * n_out, [out] * n_out, [(tm, tko)] * n_out, _store_f32, after)


def _dw_cols(name, a, dy):
    S, K = a.shape
    Ns = dy.shape[1] // N_CHIPS
    tk, tn = _tile(K, 512), _tile(Ns, 1408)
    npb = Ns // tn
    return _mm(name, (K // tk, N_CHIPS, npb), 0, [a, dy],
               [pl.BlockSpec((S, tk), lambda jk, s, jn: (0, jk)), pl.BlockSpec((S, tn), lambda jk, s, jn: (0, s * npb + jn))],
               [(0, 1, 0)], TN, 0, [jax.ShapeDtypeStruct((N_CHIPS, K, Ns), MXU)],
               [pl.BlockSpec((None, tk, tn), lambda jk, s, jn: (s, jk, jn))], [None], _store_mxu)[0]


def _dw_rows(name, a, dy):
    S = a.shape[0]
    K = a.shape[1] // N_CHIPS
    N = dy.shape[1]
    tk, tn = _tile(K, 1408), _tile(N, 1024)
    nkb = K // tk
    return _mm(name, (N_CHIPS, nkb, N // tn), 0, [a, dy],
               [pl.BlockSpec((S, tk), lambda s, jk, jn: (0, s * nkb + jk)), pl.BlockSpec((S, tn), lambda s, jk, jn: (0, jn))],
               [(0, 1, 0)], TN, 0, [jax.ShapeDtypeStruct((N_CHIPS, K, N), MXU)],
               [pl.BlockSpec((None, tk, tn), lambda s, jk, jn: (s, jk, jn))], [None], _store_mxu)[0]


def _layer_fwd(x, stream, layer, last, sp, cos, sin, dims):
    AW, KW, gate0, u_col = dims
    h = _rms_fwd("mix_norm", x, sp["mix_norm"])
    w = stream.finish(layer, 0, h)
    proj = _proj_in(h, w["w_in"])
    qr, kr, vb = _qk_prep(proj, sp["q_norm"], sp["k_norm"], cos, sin, AW, KW)
    stream.forward(layer, 1, qr)
    attn = _attn_fwd(qr, kr, vb, sp["sinks"])
    w.update(stream.finish(layer, 1, attn))
    sgu = _sgu_fwd(proj, sp["sgu_ln_g"], sp["sgu_ln_b"], sp["w_spatial"], sp["b_spatial"], u_col)
    merged, ba, bb = _branches(attn, sgu, w["w_attn_branch"], w["w_sgu_branch"], proj, gate0)
    stream.forward(layer, 2, merged)
    x1 = _rows_mm("out_proj", merged, w["w_out"], x)
    w.update(stream.finish(layer, 2, x1))
    h2 = _rms_fwd("ffn_norm", x1, sp["ffn_norm"])
    stream.forward(layer, 3, h2)
    g, u, act = _gate_up(h2, w["w_gate"], w["w_up"])
    w.update(stream.finish(layer, 3, g))
    x2 = _rows_mm("down_proj", act, w["w_down"], x1)
    if not last:
        stream.forward(layer + 1, 0, x2)
    saved = dict(x=x, h=h, proj=proj, qr=qr, kr=kr, vb=vb, attn=attn, sgu=sgu, merged=merged, ba=ba, bb=bb,
                 x1=x1, h2=h2, g=g, u=u, act=act)
    return x2, saved, w


def _layer_bwd(dy, dyb, w, sp, sv, cos, sin, dims, reducer, layer):
    AW, KW, gate0, u_col = dims
    big, small = {}, {}
    dg, du = _down_bwd(dyb, w["w_down"], sv["g"], sv["u"])
    big["w_down"] = _dw_rows("dw_down", sv["act"], dyb)
    big["w_gate"] = _dw_cols("dw_gate", sv["h2"], dg)
    big["w_up"] = _dw_cols("dw_up", sv["h2"], du)
    token = reducer.start(layer, 2, big)
    dh2 = _dx_cols("dh2", [(dg, w["w_gate"], 0), (du, w["w_up"], 0)], 1, token)[0]
    token = reducer.scatter(layer, 2, dh2)
    dx1, dx1b, small["ffn_norm"] = _rms_bwd("ffn_norm_bwd", dh2, sv["x1"], sp["ffn_norm"], dy, token)
    dba, dbb, dgla, dglb = _out_bwd(dx1b, w["w_out"], sv["proj"], sv["ba"], sv["bb"], gate0)
    big["w_out"] = _dw_rows("dw_out", sv["merged"], dx1b)
    big["w_attn_branch"] = _dw_cols("dw_attn_branch", sv["attn"], dba)
    big["w_sgu_branch"] = _dw_cols("dw_sgu_branch", sv["sgu"], dbb)
    token = reducer.start(layer, 1, big)
    dattn, dsgu = _dx_cols("dbranch_in", [(dba, w["w_attn_branch"], 0), (dbb, w["w_sgu_branch"], 1)], 2, token)
    token = reducer.scatter(layer, 1, dsgu)
    dpu, dpv, small["w_spatial"], db, small["sgu_ln_g"], small["sgu_ln_b"] = _sgu_bwd(
        sv["proj"], sp["sgu_ln_g"], sp["sgu_ln_b"], sp["w_spatial"], sp["b_spatial"], dsgu, u_col, token)
    small["b_spatial"] = db[:, :, 0]
    dq, dkp, dkc, dvp, dvc, dsink = _attn_bwd(sv["qr"], sv["kr"], sv["vb"], sp["sinks"], dattn)
    small["sinks"] = dsink[:, :sp["sinks"].shape[1]]
    dqkv, dqg, dkg = _qk_prep_bwd(sv["proj"], sp["q_norm"], sp["k_norm"], cos, sin, dq, dkp, dkc, dvp, dvc, AW, KW)
    small["q_norm"] = dqg[:, :HEAD_DIM]
    small["k_norm"] = dkg[:, :HEAD_DIM]
    dproj = jnp.concatenate([dqkv, dpu, dpv, dgla, dglb], axis=1)
    big["w_in"] = _dw_cols("dw_in", sv["h"], dproj)
    token = reducer.start(layer, 0, big)
    dh = _dx_cols("dh", [(dproj, w["w_in"], 0)], 1, token)[0]
    token = reducer.scatter(layer, 0, dh)
    dx, dxb, small["mix_norm"] = _rms_bwd("mix_norm_bwd", dh, sv["x"], sp["mix_norm"], dx1, token)
    return dx, dxb, small


def _place():
    x, y, c = lax.axis_index("x"), lax.axis_index("y"), lax.axis_index("c")
    chips = [(1 - x, y), (x, 1 - y), (1 - x, 1 - y)]
    return x, y, c, chips


def _half_rows(c, rows):
    h = rows // 2
    assert h % 16 == 0
    return pl.ds(pl.multiple_of(c * h, 16), h)


def _row_tile(rows, pref):
    best = None
    for t in range(16, min(rows, pref) + 1, 16):
        if rows % t == 0:
            best = t
    assert best is not None, rows
    return best


def _cast_own(name, chip, w, layer):
    _, R, C = w.shape
    tr = _row_tile(R, 512)

    def body(chip_ref, w_ref, o_ref):
        o_ref[...] = w_ref[...].astype(MXU)

    return pl.pallas_call(
        body, name=name, out_shape=jax.ShapeDtypeStruct((N_CHIPS, R, C), MXU),
        grid_spec=pltpu.PrefetchScalarGridSpec(
            num_scalar_prefetch=1, grid=(R // tr,),
            in_specs=[pl.BlockSpec((None, tr, C), lambda i, chip_ref: (layer, i, 0))],
            out_specs=pl.BlockSpec((None, tr, C), lambda i, chip_ref: (chip_ref[0], i, 0))),
        compiler_params=_params())(chip, w)


HBM = pl.BlockSpec(memory_space=pltpu.HBM)
SEM = pl.BlockSpec(memory_space=pltpu.SEMAPHORE)
DATAFLOW = pltpu.SideEffectType.DATAFLOW_SIDE_EFFECTING


def _gather_copies(bufs, send_sem, recv_sem):
    x, y, c, chips = _place()

    def ici(a, j, block):
        px, py = chips[j]
        blk = bufs[a].at[block, _half_rows(c, bufs[a].shape[1])]
        return pltpu.make_async_remote_copy(
            src_ref=blk, dst_ref=blk, send_sem=send_sem.at[3 * a + j], recv_sem=recv_sem.at[3 * a + j],
            device_id=(px, py, c), device_id_type=MESH)

    def d2d(a, j, core):
        px, py = chips[j]
        blk = bufs[a].at[2 * px + py, _half_rows(core, bufs[a].shape[1])]
        return pltpu.make_async_remote_copy(
            src_ref=blk, dst_ref=blk, send_sem=send_sem.at[3 * a + j], recv_sem=recv_sem.at[3 * a + j],
            device_id=(x, y, 1 - c), device_id_type=MESH)

    return ici, d2d


def _in_hbm(bufs):
    return [pltpu.with_memory_space_constraint(b, pltpu.HBM) for b in bufs]


def _gather_start(name, bufs, after):
    n = len(bufs)

    def body(*refs):
        dst = refs[n + 1:2 * n + 1]
        send_sem, recv_sem, token = refs[2 * n + 1:]
        x, y, c, chips = _place()
        ici, _ = _gather_copies(dst, send_sem, recv_sem)
        for a in range(n):
            for j in range(3):
                ici(a, j, 2 * x + y).start()
        token[...] = jnp.zeros(token.shape, token.dtype)

    sems = pltpu.SemaphoreType.DMA((3 * n,))
    outs = pl.pallas_call(
        body, name=name, in_specs=[HBM] * n + [ANY],
        out_specs=[HBM] * n + [SEM, SEM, pl.BlockSpec(memory_space=pltpu.VMEM)],
        out_shape=[pltpu.HBM(b.shape, b.dtype) for b in bufs] + [sems, sems, jax.ShapeDtypeStruct((8, BLOCK), F32)],
        input_output_aliases={a: a for a in range(n)},
        compiler_params=pltpu.CompilerParams(has_side_effects=DATAFLOW))(*_in_hbm(bufs), after)
    return outs[:n], outs[n], outs[n + 1], outs[n + 2]


def _gather_forward(name, bufs, ici_send, ici_recv, after):
    n = len(bufs)

    def body(*refs):
        ici_send_ref, ici_recv_ref = refs[n], refs[n + 1]
        dst = refs[n + 3:2 * n + 3]
        d2d_send, d2d_recv = refs[2 * n + 3:]
        x, y, c, chips = _place()
        ici, _ = _gather_copies(dst, ici_send_ref, ici_recv_ref)
        _, d2d = _gather_copies(dst, d2d_send, d2d_recv)
        for a in range(n):
            for j, (px, py) in enumerate(chips):
                ici(a, j, 2 * px + py).wait_recv()
                d2d(a, j, c).start()
        for a in range(n):
            for j in range(3):
                ici(a, j, 2 * x + y).wait_send()

    sems = pltpu.SemaphoreType.DMA((3 * n,))
    outs = pl.pallas_call(
        body, name=name, in_specs=[HBM] * n + [SEM, SEM, ANY], out_specs=[HBM] * n + [SEM, SEM],
        out_shape=[pltpu.HBM(b.shape, b.dtype) for b in bufs] + [sems, sems],
        input_output_aliases={a: a for a in range(n)},
        compiler_params=pltpu.CompilerParams(has_side_effects=DATAFLOW))(*bufs, ici_send, ici_recv, after)
    return outs[:n], outs[n], outs[n + 1]


def _gather_finish(name, bufs, d2d_send, d2d_recv, after):
    n = len(bufs)

    def body(*refs):
        send_ref, recv_ref = refs[n], refs[n + 1]
        dst = refs[n + 3:]
        x, y, c, chips = _place()
        _, d2d = _gather_copies(dst, send_ref, recv_ref)
        for a in range(n):
            for j in range(3):
                d2d(a, j, 1 - c).wait_recv()
                d2d(a, j, c).wait_send()

    return pl.pallas_call(
        body, name=name, in_specs=[HBM] * n + [SEM, SEM, ANY], out_specs=[HBM] * n,
        out_shape=[pltpu.HBM(b.shape, b.dtype) for b in bufs],
        input_output_aliases={a: a for a in range(n)},
        compiler_params=pltpu.CompilerParams(has_side_effects=DATAFLOW))(*bufs, d2d_send, d2d_recv, after)


GATHER = (("w_in",), ("w_attn_branch", "w_sgu_branch", "w_out"), ("w_gate", "w_up"), ("w_down",))
REDUCE = (("w_in",), ("w_attn_branch", "w_sgu_branch", "w_out"), ("w_gate", "w_up", "w_down"))


class _WeightStream:
    def __init__(self, started):
        self.started, self.passed = started, {}

    def forward(self, layer, group, after):
        bufs, send, recv = self.started[(layer, group)]
        self.passed[(layer, group)] = _gather_forward("gather_forward_%d_%d" % (layer, group), bufs, send, recv, after)

    def finish(self, layer, group, after):
        bufs, send, recv = self.passed[(layer, group)]
        done = _gather_finish("gather_finish_%d_%d" % (layer, group), bufs, send, recv, after)
        return dict(zip(GATHER[group], done))


def _pair_copies(grads, lands, send_sem, recv_sem):
    x, y, c, _ = _place()

    def make(a):
        theirs = _half_rows(1 - c, grads[a].shape[1])
        return pltpu.make_async_remote_copy(
            src_ref=grads[a].at[:, theirs], dst_ref=lands[a], send_sem=send_sem.at[a], recv_sem=recv_sem.at[a],
            device_id=(x, y, 1 - c), device_id_type=MESH)

    return make


def _pair_start(name, grads, after):
    n = len(grads)
    lands = [lax.empty((g.shape[0], g.shape[1] // 2, g.shape[2]), g.dtype) for g in grads]

    def body(*refs):
        src, dst = refs[2 * n + 1:3 * n + 1], refs[3 * n + 1:4 * n + 1]
        send_sem, recv_sem, token = refs[4 * n + 1:]
        copy = _pair_copies(src, dst, send_sem, recv_sem)
        for a in range(n):
            copy(a).start()
        token[...] = jnp.zeros(token.shape, token.dtype)

    sems = pltpu.SemaphoreType.DMA((n,))
    outs = pl.pallas_call(
        body, name=name, in_specs=[HBM] * (2 * n) + [ANY],
        out_specs=[HBM] * (2 * n) + [SEM, SEM, pl.BlockSpec(memory_space=pltpu.VMEM)],
        out_shape=[pltpu.HBM(b.shape, b.dtype) for b in grads + lands] + [sems, sems, jax.ShapeDtypeStruct((8, BLOCK), F32)],
        input_output_aliases={a: a for a in range(2 * n)},
        compiler_params=pltpu.CompilerParams(has_side_effects=DATAFLOW))(*_in_hbm(grads + lands), after)
    return outs[:n], outs[n:2 * n], outs[2 * n], outs[2 * n + 1], outs[2 * n + 2]


def _pair_finish(name, grads, lands, send_sem, recv_sem, after):
    n = len(grads)

    def body(*refs):
        send_ref, recv_ref = refs[2 * n], refs[2 * n + 1]
        src, dst = refs[2 * n + 3:3 * n + 3], refs[3 * n + 3:]
        copy = _pair_copies(src, dst, send_ref, recv_ref)
        for a in range(n):
            copy(a).wait_send()
            copy(a).wait_recv()

    outs = pl.pallas_call(
        body, name=name, in_specs=[HBM] * (2 * n) + [SEM, SEM, ANY], out_specs=[HBM] * (2 * n),
        out_shape=[pltpu.HBM(b.shape, b.dtype) for b in grads + lands],
        input_output_aliases={a: a for a in range(2 * n)},
        compiler_params=pltpu.CompilerParams(has_side_effects=DATAFLOW))(*grads, *lands, send_sem, recv_sem, after)
    return outs[:n], outs[n:]


def _pair_sum(name, core, g, p):
    _, h, C = p.shape
    tr = _row_tile(h, 512)
    nrb = h // tr

    def body(core_ref, g_ref, p_ref, o_ref):
        o_ref[...] = (g_ref[...].astype(F32) + p_ref[...].astype(F32)).astype(o_ref.dtype)

    spec = pl.BlockSpec((None, tr, C), lambda s, i, core_ref: (s, i, 0))
    return pl.pallas_call(
        body, name=name, out_shape=jax.ShapeDtypeStruct(p.shape, p.dtype),
        grid_spec=pltpu.PrefetchScalarGridSpec(
            num_scalar_prefetch=1, grid=(N_CHIPS, nrb),
            in_specs=[pl.BlockSpec((None, tr, C), lambda s, i, core_ref: (s, core_ref[0] * nrb + i, 0)), spec],
            out_specs=spec),
        compiler_params=_params())(core, g, p)


def _scatter_copies(sums, slots, send_sem, recv_sem):
    x, y, c, chips = _place()

    def make(a, j):
        px, py = chips[j]
        return pltpu.make_async_remote_copy(
            src_ref=sums[a].at[2 * px + py], dst_ref=slots[a].at[j], send_sem=send_sem.at[3 * a + j],
            recv_sem=recv_sem.at[3 * a + j], device_id=(px, py, c), device_id_type=MESH)

    return make


def _scatter_start(name, sums, after):
    n = len(sums)
    slots = [lax.empty((3,) + s.shape[1:], s.dtype) for s in sums]

    def body(*refs):
        src, dst = refs[2 * n + 1:3 * n + 1], refs[3 * n + 1:4 * n + 1]
        send_sem, recv_sem, token = refs[4 * n + 1:]
        copy = _scatter_copies(src, dst, send_sem, recv_sem)
        for a in range(n):
            for j in range(3):
                copy(a, j).start()
        token[...] = jnp.zeros(token.shape, token.dtype)

    sems = pltpu.SemaphoreType.DMA((3 * n,))
    outs = pl.pallas_call(
        body, name=name, in_specs=[HBM] * (2 * n) + [ANY],
        out_specs=[HBM] * (2 * n) + [SEM, SEM, pl.BlockSpec(memory_space=pltpu.VMEM)],
        out_shape=[pltpu.HBM(b.shape, b.dtype) for b in sums + slots] + [sems, sems, jax.ShapeDtypeStruct((8, BLOCK), F32)],
        input_output_aliases={a: a for a in range(2 * n)},
        compiler_params=pltpu.CompilerParams(has_side_effects=DATAFLOW))(*_in_hbm(sums + slots), after)
    return outs[:n], outs[n:2 * n], outs[2 * n], outs[2 * n + 1], outs[2 * n + 2]


def _scatter_finish(name, sums, slots, send_sem, recv_sem, after):
    n = len(sums)

    def body(*refs):
        send_ref, recv_ref = refs[2 * n], refs[2 * n + 1]
        src, dst = refs[2 * n + 3:3 * n + 3], refs[3 * n + 3:]
        copy = _scatter_copies(src, dst, send_ref, recv_ref)
        for a in range(n):
            for j in range(3):
                copy(a, j).wait_send()
                copy(a, j).wait_recv()

    outs = pl.pallas_call(
        body, name=name, in_specs=[HBM] * (2 * n) + [SEM, SEM, ANY], out_specs=[HBM] * (2 * n),
        out_shape=[pltpu.HBM(b.shape, b.dtype) for b in sums + slots],
        input_output_aliases={a: a for a in range(2 * n)},
        compiler_params=pltpu.CompilerParams(has_side_effects=DATAFLOW))(*sums, *slots, send_sem, recv_sem, after)
    return outs[:n], outs[n:]


def _slot_sum(name, place, slots, sums):
    _, h, C = slots.shape
    tr = _row_tile(h, 512)
    nrb = h // tr

    def body(place_ref, r0, r1, r2, own, o_ref):
        o_ref[...] = ((r0[...].astype(F32) + r1[...].astype(F32)) + r2[...].astype(F32)) + own[...].astype(F32)

    slot = lambda k: pl.BlockSpec((None, tr, C), lambda i, place_ref: (k, i, 0))
    return pl.pallas_call(
        body, name=name, out_shape=jax.ShapeDtypeStruct((2 * h, C), F32),
        grid_spec=pltpu.PrefetchScalarGridSpec(
            num_scalar_prefetch=1, grid=(nrb,),
            in_specs=[slot(0), slot(1), slot(2),
                      pl.BlockSpec((None, tr, C), lambda i, place_ref: (place_ref[0], i, 0))],
            out_specs=pl.BlockSpec((tr, C), lambda i, place_ref: (place_ref[1] * nrb + i, 0))),
        compiler_params=_params())(place, slots, slots, slots, sums)


def _half_copies(bufs, send_sem, recv_sem):
    x, y, c, _ = _place()

    def make(a, core):
        rows = bufs[a].at[_half_rows(core, bufs[a].shape[0])]
        return pltpu.make_async_remote_copy(
            src_ref=rows, dst_ref=rows, send_sem=send_sem.at[a], recv_sem=recv_sem.at[a],
            device_id=(x, y, 1 - c), device_id_type=MESH)

    return make


def _half_start(name, bufs, after):
    n = len(bufs)

    def body(*refs):
        dst = refs[n + 1:2 * n + 1]
        send_sem, recv_sem, token = refs[2 * n + 1:]
        c = lax.axis_index("c")
        copy = _half_copies(dst, send_sem, recv_sem)
        for a in range(n):
            copy(a, c).start()
        token[...] = jnp.zeros(token.shape, token.dtype)

    sems = pltpu.SemaphoreType.DMA((n,))
    outs = pl.pallas_call(
        body, name=name, in_specs=[HBM] * n + [ANY],
        out_specs=[HBM] * n + [SEM, SEM, pl.BlockSpec(memory_space=pltpu.VMEM)],
        out_shape=[pltpu.HBM(b.shape, b.dtype) for b in bufs] + [sems, sems, jax.ShapeDtypeStruct((8, BLOCK), F32)],
        input_output_aliases={a: a for a in range(n)},
        compiler_params=pltpu.CompilerParams(has_side_effects=DATAFLOW))(*_in_hbm(bufs), after)
    return outs[:n], outs[n], outs[n + 1], outs[n + 2]


def _half_finish(name, bufs, send_sem, recv_sem, after):
    n = len(bufs)

    def body(*refs):
        send_ref, recv_ref = refs[n], refs[n + 1]
        dst = refs[n + 3:]
        c = lax.axis_index("c")
        copy = _half_copies(dst, send_ref, recv_ref)
        for a in range(n):
            copy(a, c).wait_send()
            copy(a, 1 - c).wait_recv()

    return pl.pallas_call(
        body, name=name, in_specs=[HBM] * n + [SEM, SEM, ANY], out_specs=[HBM] * n,
        out_shape=[pltpu.HBM(b.shape, b.dtype) for b in bufs],
        input_output_aliases={a: a for a in range(n)},
        compiler_params=pltpu.CompilerParams(has_side_effects=DATAFLOW))(*bufs, send_sem, recv_sem, after)


class _GradReducer:
    def __init__(self, chip, core):
        self.core, self.place, self.pairs, self.started = core, jnp.concatenate([chip, core]), {}, []

    def start(self, layer, group, grads):
        mine = [grads[n] for n in REDUCE[group]]
        mine, lands, send, recv, token = _pair_start("grad_pair_start_%d_%d" % (layer, group), mine, self.place)
        self.pairs[(layer, group)] = (mine, lands, send, recv)
        return token

    def scatter(self, layer, group, after):
        tag = "%d_%d" % (layer, group)
        names = REDUCE[group]
        mine, lands, send, recv = self.pairs.pop((layer, group))
        mine, theirs = _pair_finish("grad_pair_finish_" + tag, mine, lands, send, recv, after)
        sums = [_pair_sum("pair_sum_%s_%d" % (n, layer), self.core, g, p) for n, g, p in zip(names, mine, theirs)]
        sums, slots, send, recv, token = _scatter_start("grad_scatter_start_" + tag, sums, self.place)
        self.started.append((layer, names, sums, slots, send, recv))
        return token

    def finish(self, after, update):
        for layer in sorted({entry[0] for entry in self.started}, reverse=True):
            exchanged = []
            for lyr, names, sums, slots, send, recv in self.started:
                if lyr != layer:
                    continue
                tag = "%s_%d" % (names[0], layer)
                sums, slots = _scatter_finish("grad_scatter_finish_" + tag, sums, slots, send, recv, after)
                halves = [_slot_sum("slot_sum_%s_%d" % (n, layer), self.place, r, s)
                          for n, r, s in zip(names, slots, sums)]
                halves, send, recv, after = _half_start("grad_half_start_" + tag, halves, self.place)
                exchanged.append((tag, names, halves, send, recv))
            for tag, names, halves, send, recv in exchanged:
                whole = _half_finish("grad_half_finish_" + tag, halves, send, recv, after)
                after = update(layer, dict(zip(names, whole)))


def _all_reduce_small(v):
    rows = v.shape[0]
    n_dev = 2 * N_CHIPS

    def body(x_ref, out_ref, gat_ref, send_sems, recv_sems, local_sem):
        x, y, c, chips = _place()
        me, sibling = (x, y, c), (x, y, 1 - c)

        def slot(px, py, pc):
            return gat_ref.at[4 * px + 2 * py + pc]

        def copy(k, block, to, src=None):
            return pltpu.make_async_remote_copy(
                src_ref=slot(*block) if src is None else src, dst_ref=slot(*block), send_sem=send_sems.at[k],
                recv_sem=recv_sems.at[k], device_id=to, device_id_type=MESH)

        mine = pltpu.make_async_copy(x_ref, slot(*me), local_sem)
        mine.start()
        first = [copy(0, me, sibling, src=x_ref)]
        first += [copy(1 + j, me, (*chip, c), src=x_ref) for j, chip in enumerate(chips)]
        for cp in first:
            cp.start()
        passed = [copy(4 + j, (*chip, c), sibling) for j, chip in enumerate(chips)]
        for j, chip in enumerate(chips):
            copy(1 + j, (*chip, c), me).wait_recv()
            passed[j].start()
        copy(0, sibling, me).wait_recv()
        for j, chip in enumerate(chips):
            copy(4 + j, (*chip, 1 - c), me).wait_recv()
        for cp in first + passed:
            cp.wait_send()
        mine.wait()
        acc = gat_ref[0]
        for d in range(1, n_dev):
            acc = acc + gat_ref[d]
        out_ref[...] = acc

    vm = pl.BlockSpec(memory_space=pltpu.VMEM)
    return pl.pallas_call(
        body, name="small_grad_all_reduce", in_specs=[vm], out_specs=vm,
        out_shape=jax.ShapeDtypeStruct(v.shape, F32),
        scratch_shapes=[pltpu.VMEM((n_dev, rows, BLOCK), F32), pltpu.SemaphoreType.DMA((7,)),
                        pltpu.SemaphoreType.DMA((7,)), pltpu.SemaphoreType.DMA],
        compiler_params=_params())(v)


def _adamw_math(w, g, m, v):
    m2 = ADAM_B1 * m + (1.0 - ADAM_B1) * g
    v2 = ADAM_B2 * v + (1.0 - ADAM_B2) * (g * g)
    m_hat = m2 / (1.0 - ADAM_B1 ** ADAM_STEP)
    v_hat = v2 / (1.0 - ADAM_B2 ** ADAM_STEP)
    delta = -ADAM_LR * (m_hat / (jnp.sqrt(v_hat) + ADAM_EPS) + ADAM_WD * w)
    return delta, m2, v2


def _adamw_big(name, layer, grad, w, m, v, others):
    L, R, C = w.shape
    tr = _row_tile(R, 256)

    def body(g_ref, w_ref, m_ref, v_ref, *rest):
        go_ref, d_ref, mo_ref, vo_ref = rest[-4:]
        g = g_ref[...]
        delta, m2, v2 = _adamw_math(w_ref[...], g, m_ref[...], v_ref[...])
        go_ref[...] = g
        d_ref[...] = delta
        mo_ref[...] = m2
        vo_ref[...] = v2

    blk = pl.BlockSpec((None, tr, C), lambda i: (layer, i, 0))
    shp = jax.ShapeDtypeStruct(w.shape, F32)
    others = [] if others is None else list(others)
    return pl.pallas_call(
        body, name=name, grid=(R // tr,),
        in_specs=[pl.BlockSpec((tr, C), lambda i: (i, 0))] + [blk] * 3 + [ANY] * len(others), out_specs=[blk] * 4,
        out_shape=[shp] * 4, input_output_aliases={4 + k: k for k in range(len(others))},
        compiler_params=_params())(grad, w, m, v, *others)


def _adamw_small(g, w, m, v):
    rows = g.shape[0]
    tr = _row_tile(rows, 512)

    def body(g_ref, w_ref, m_ref, v_ref, d_ref, mo_ref, vo_ref):
        delta, m2, v2 = _adamw_math(w_ref[...], g_ref[...], m_ref[...], v_ref[...])
        d_ref[...] = delta
        mo_ref[...] = m2
        vo_ref[...] = v2

    blk = pl.BlockSpec((tr, BLOCK), lambda i: (i, 0))
    shp = jax.ShapeDtypeStruct(g.shape, F32)
    return pl.pallas_call(
        body, name="adamw_small", grid=(rows // tr,), in_specs=[blk] * 4, out_specs=[blk] * 3, out_shape=[shp] * 3,
        compiler_params=_params())(g, w, m, v)


def _pack(arrays):
    flat = jnp.concatenate([a.reshape(-1) for a in arrays])
    pad = (-flat.shape[0]) % (16 * BLOCK)
    return jnp.pad(flat, (0, pad)).reshape(-1, BLOCK)


def _unpack(packed, like):
    flat = packed.reshape(-1)
    out, off = [], 0
    for a in like:
        out.append(flat[off:off + a.size].reshape(a.shape))
        off += a.size
    return out


BIG = ("w_in", "w_attn_branch", "w_sgu_branch", "w_out", "w_gate", "w_up", "w_down")
SMALL = ("mix_norm", "q_norm", "k_norm", "sinks", "sgu_ln_g", "sgu_ln_b", "w_spatial", "b_spatial", "ffn_norm")
ORDER = ("mix_norm", "w_in", "q_norm", "k_norm", "sinks", "sgu_ln_g", "sgu_ln_b", "w_spatial", "b_spatial",
         "w_attn_branch", "w_sgu_branch", "w_out", "ffn_norm", "w_gate", "w_up", "w_down")


def _rope_tables(seq):
    pos = jnp.arange(seq, dtype=F32)
    inv_freq = jnp.power(10000.0, -jnp.arange(0, HEAD_DIM, 2, dtype=F32) / HEAD_DIM)
    ang = pos[:, None] * inv_freq[None, :]
    cos, sin = jnp.cos(ang), jnp.sin(ang)
    reps = BLOCK // HEAD_DIM
    return (jnp.tile(jnp.concatenate([cos, cos], axis=1), (1, reps)),
            jnp.tile(jnp.concatenate([-sin, sin], axis=1), (1, reps)))


def kernel(x, mix_norm, w_in, q_norm, k_norm, sinks, sgu_ln_g, sgu_ln_b, w_spatial, b_spatial, w_attn_branch, w_sgu_branch, w_out, ffn_norm, w_gate, w_up, w_down, loss_target, m_mix_norm, m_w_in, m_q_norm, m_k_norm, m_sinks, m_sgu_ln_g, m_sgu_ln_b, m_w_spatial, m_b_spatial, m_w_attn_branch, m_w_sgu_branch, m_w_out, m_ffn_norm, m_w_gate, m_w_up, m_w_down, v_mix_norm, v_w_in, v_q_norm, v_k_norm, v_sinks, v_sgu_ln_g, v_sgu_ln_b, v_w_spatial, v_b_spatial, v_w_attn_branch, v_w_sgu_branch, v_w_out, v_ffn_norm, v_w_gate, v_w_up, v_w_down):
    weights = dict(mix_norm=mix_norm, w_in=w_in, q_norm=q_norm, k_norm=k_norm, sinks=sinks, sgu_ln_g=sgu_ln_g,
                   sgu_ln_b=sgu_ln_b, w_spatial=w_spatial, b_spatial=b_spatial, w_attn_branch=w_attn_branch,
                   w_sgu_branch=w_sgu_branch, w_out=w_out, ffn_norm=ffn_norm, w_gate=w_gate, w_up=w_up, w_down=w_down)
    mom1 = dict(mix_norm=m_mix_norm, w_in=m_w_in, q_norm=m_q_norm, k_norm=m_k_norm, sinks=m_sinks,
                sgu_ln_g=m_sgu_ln_g, sgu_ln_b=m_sgu_ln_b, w_spatial=m_w_spatial, b_spatial=m_b_spatial,
                w_attn_branch=m_w_attn_branch, w_sgu_branch=m_w_sgu_branch, w_out=m_w_out, ffn_norm=m_ffn_norm,
                w_gate=m_w_gate, w_up=m_w_up, w_down=m_w_down)
    mom2 = dict(mix_norm=v_mix_norm, w_in=v_w_in, q_norm=v_q_norm, k_norm=v_k_norm, sinks=v_sinks,
                sgu_ln_g=v_sgu_ln_g, sgu_ln_b=v_sgu_ln_b, w_spatial=v_w_spatial, b_spatial=v_b_spatial,
                w_attn_branch=v_w_attn_branch, w_sgu_branch=v_w_sgu_branch, w_out=v_w_out, ffn_norm=v_ffn_norm,
                w_gate=v_w_gate, w_up=v_w_up, w_down=v_w_down)
    xs, target = x[0], loss_target[0]
    S, D = xs.shape
    L = w_in.shape[0]
    AW, KW, SW = N_Q_HEADS * HEAD_DIM, N_KV_HEADS * HEAD_DIM, SGU_GROUPS * BLOCK
    dims = (AW, KW, AW + 2 * KW + 2 * SW, AW + 2 * KW)
    cos, sin = _rope_tables(S)
    reps = BLOCK // HEAD_DIM

    chip = (2 * lax.axis_index("x") + lax.axis_index("y")).astype(jnp.int32).reshape(1)
    core = lax.axis_index("c").astype(jnp.int32).reshape(1)
    started, token = {}, chip
    for l in range(L):
        for gi, names in enumerate(GATHER):
            bufs = [_cast_own("cast_%s_%d" % (n, l), chip, weights[n], l) for n in names]
            bufs, send, recv, token = _gather_start("gather_start_%d_%d" % (l, gi), bufs, token)
            started[(l, gi)] = (bufs, send, recv)
    stream = _WeightStream(started)
    stream.forward(0, 0, token)
    sp = [dict(mix_norm=mix_norm[l][None], ffn_norm=ffn_norm[l][None], q_norm=jnp.tile(q_norm[l][None], (1, reps)),
               k_norm=jnp.tile(k_norm[l][None], (1, reps)), sinks=sinks[l][None], sgu_ln_g=sgu_ln_g[l][None],
               sgu_ln_b=sgu_ln_b[l][None], w_spatial=w_spatial[l], b_spatial=b_spatial[l][:, :, None])
          for l in range(L)]

    act, saved, wl = xs, [], []
    for l in range(L):
        act, sv, w_all = _layer_fwd(act, stream, l, l == L - 1, sp[l], cos, sin, dims)
        saved.append(sv)
        wl.append(w_all)
    loss_part, dy, dyb = _loss_head(act, target)
    loss = lax.psum(loss_part[0, 0], ("x", "y", "c"))

    reducer = _GradReducer(chip, core)
    small_g = [None] * L
    for l in reversed(range(L)):
        dy, dyb, small_g[l] = _layer_bwd(dy, dyb, wl[l], sp[l], saved[l], cos, sin, dims, reducer, l)
    grad_x = dy[None]

    updated = {}

    def update(layer, reduced):
        for n, g in reduced.items():
            updated[n] = _adamw_big("adamw_%s_%d" % (n, layer), layer, g, weights[n], mom1[n], mom2[n],
                                    updated.get(n))
        return updated[n][0]

    reducer.finish(dy, update)
    grads, deltas, new_m, new_v = {}, {}, {}, {}
    for n in BIG:
        grads[n], deltas[n], new_m[n], new_v[n] = updated[n]

    small_like = [weights[n] for n in SMALL]
    local = [jnp.stack([small_g[l][n].reshape(weights[n].shape[1:]) for l in range(L)]) for n in SMALL]
    g_small = _all_reduce_small(_pack(local))
    d_small, m_small, v_small = _adamw_small(g_small, _pack(small_like), _pack([mom1[n] for n in SMALL]),
                                             _pack([mom2[n] for n in SMALL]))
    for n, g, d, m2, v2 in zip(SMALL, _unpack(g_small, small_like), _unpack(d_small, small_like),
                               _unpack(m_small, small_like), _unpack(v_small, small_like)):
        grads[n], deltas[n], new_m[n], new_v[n] = g, d, m2, v2

    return (loss, grad_x, *[grads[n] for n in ORDER], *[deltas[n] for n in ORDER],
            *[new_m[n] for n in ORDER], *[new_v[n] for n in ORDER])
```

```python
import functools

import jax
import jax.numpy as jnp
from jax import lax
from jax.experimental import pallas as pl
from jax.experimental.pallas import tpu as pltpu

HEAD_DIM = 64
N_Q_HEADS = 16
N_KV_HEADS = 4
SGU_GROUPS = 8
BLOCK = 128
EPS = 1e-6
ADAM_LR = 0.001
ADAM_B1 = 0.9
ADAM_B2 = 0.999
ADAM_EPS = 1e-08
ADAM_WD = 0.01
ADAM_STEP = 10
N_CHIPS = 4
VMEM_LIMIT = 52 * 1024 * 1024
MXU_CHUNK = 256
ATTN_STACK = 4

F32 = jnp.float32
MXU = jnp.bfloat16
NN = (((1,), (0,)), ((), ()))
NT = (((1,), (1,)), ((), ()))
TN = (((0,), (0,)), ((), ()))
MESH = pl.DeviceIdType.MESH
ANY = pl.BlockSpec(memory_space=pl.ANY)


def _tile(n, pref):
    if n <= pref:
        return n
    best = None
    for t in range(BLOCK, pref + 1, BLOCK):
        if n % t == 0:
            best = t
    assert best is not None, (n, pref)
    return best


def _params():
    return pltpu.CompilerParams(vmem_limit_bytes=VMEM_LIMIT)


def _mm(name, grid, n_red, operands, specs, pairs, dims, n_extra, out_shapes, out_specs,
        acc_shapes, epilogue, after=None, chunk=None):
    n_op = len(operands) - n_extra
    n_out = len(out_shapes)
    n_acc = len(acc_shapes)
    if after is not None:
        operands, specs = list(operands) + [after], list(specs) + [ANY]
    n_in = len(operands)
    axes = [ax for ax in range(len(grid) - n_red, len(grid)) if grid[ax] > 1]

    def body(*refs):
        ops = refs[:n_op]
        extra = refs[n_op:n_op + n_extra]
        outs = refs[n_in:n_in + n_out]
        accs = refs[n_in + n_out:]

        def prod(a, b, cols=None):
            rhs = ops[b]
            if cols is not None:
                rhs = rhs.at[:, cols] if dims == NN else rhs.at[cols, :]
            return lax.dot_general(ops[a][...], rhs[...], dims, preferred_element_type=F32)

        def products(cols=None):
            vals = [None] * n_acc
            for a, b, k in pairs:
                d = prod(a, b, cols)
                vals[k] = d if vals[k] is None else vals[k] + d
            return vals

        if not axes and chunk is not None:
            width = outs[0].shape[-1]
            for c0 in range(0, width, chunk):
                cols = pl.ds(c0, min(chunk, width - c0))
                epilogue(products(cols), [e.at[:, cols] for e in extra], [o.at[:, cols] for o in outs])
        elif not axes:
            epilogue(products(), extra, outs)
        else:
            first = pl.program_id(axes[0]) == 0
            last = pl.program_id(axes[0]) == grid[axes[0]] - 1
            for ax in axes[1:]:
                first = jnp.logical_and(first, pl.program_id(ax) == 0)
                last = jnp.logical_and(last, pl.program_id(ax) == grid[ax] - 1)

            @pl.when(first)
            def _():
                for acc in accs:
                    acc[...] = jnp.zeros(acc.shape, F32)

            for a, b, k in pairs:
                accs[k][...] += prod(a, b)

            @pl.when(last)
            def _():
                epilogue([acc[...] for acc in accs], extra, outs)

    scratch = [pltpu.VMEM(s, F32) for s in acc_shapes] if axes else []
    return pl.pallas_call(
        body, name=name, grid=grid, in_specs=specs, out_specs=out_specs, out_shape=out_shapes,
        scratch_shapes=scratch, compiler_params=_params())(*operands)


def _sigmoid(x):
    return 1.0 / (1.0 + jnp.exp(-x))


_GELU_C = 0.7978845608028654
_GELU_A = 0.044715


def _gelu(x):
    return 0.5 * x * (1.0 + jnp.tanh(_GELU_C * (x + _GELU_A * x * x * x)))


def _gelu_grad(x):
    t = jnp.tanh(_GELU_C * (x + _GELU_A * x * x * x))
    return 0.5 * (1.0 + t) + 0.5 * x * (1.0 - t * t) * _GELU_C * (1.0 + 3.0 * _GELU_A * x * x)


def _rms_fwd(name, x, g):
    S, D = x.shape
    tr = _tile(S, 256)

    def body(x_ref, g_ref, o_ref):
        xv = x_ref[...]
        r = lax.rsqrt(jnp.mean(xv * xv, axis=-1, keepdims=True) + EPS)
        o_ref[...] = (xv * r * g_ref[...]).astype(MXU)

    return pl.pallas_call(
        body, name=name, grid=(S // tr,),
        in_specs=[pl.BlockSpec((tr, D), lambda i: (i, 0)), pl.BlockSpec((1, D), lambda i: (0, 0))],
        out_specs=pl.BlockSpec((tr, D), lambda i: (i, 0)),
        out_shape=jax.ShapeDtypeStruct((S, D), MXU), compiler_params=_params())(x, g)


def _rms_bwd(name, dh, x, g, dres, after):
    S, D = x.shape
    tr = _tile(S, 256)

    def body(dh_ref, x_ref, g_ref, dres_ref, after_ref, dx_ref, dxb_ref, dg_ref):
        xv = x_ref[...]
        r = lax.rsqrt(jnp.mean(xv * xv, axis=-1, keepdims=True) + EPS)
        xh = xv * r
        dhv = dh_ref[...]
        dy = dhv * g_ref[...]
        dx = dres_ref[...] + r * (dy - xh * jnp.mean(dy * xh, axis=-1, keepdims=True))
        dx_ref[...] = dx
        dxb_ref[...] = dx.astype(MXU)

        @pl.when(pl.program_id(0) == 0)
        def _():
            dg_ref[...] = jnp.zeros(dg_ref.shape, F32)

        dg_ref[...] += jnp.sum(dhv * xh, axis=0, keepdims=True)

    row = pl.BlockSpec((tr, D), lambda i: (i, 0))
    vec = pl.BlockSpec((1, D), lambda i: (0, 0))
    return pl.pallas_call(
        body, name=name, grid=(S // tr,), in_specs=[row, row, vec, row, ANY], out_specs=[row, row, vec],
        out_shape=[jax.ShapeDtypeStruct((S, D), F32), jax.ShapeDtypeStruct((S, D), MXU),
                   jax.ShapeDtypeStruct((1, D), F32)],
        compiler_params=_params())(dh, x, g, dres, after)


def _loss_head(y, target):
    S, D = y.shape
    tr = _tile(S, 256)

    def body(y_ref, t_ref, loss_ref, dy_ref, dyb_ref):
        d = y_ref[...] - t_ref[...]
        dy = d * (1.0 / D)
        dy_ref[...] = dy
        dyb_ref[...] = dy.astype(MXU)

        @pl.when(pl.program_id(0) == 0)
        def _():
            loss_ref[...] = jnp.zeros(loss_ref.shape, F32)

        loss_ref[...] += (0.5 / D) * jnp.sum(jnp.sum(d * d, axis=-1, keepdims=True), axis=0, keepdims=True)

    row = pl.BlockSpec((tr, D), lambda i: (i, 0))
    return pl.pallas_call(
        body, name="loss_head", grid=(S // tr,), in_specs=[row, row],
        out_specs=[pl.BlockSpec((1, 1), lambda i: (0, 0)), row, row],
        out_shape=[jax.ShapeDtypeStruct((1, 1), F32), jax.ShapeDtypeStruct((S, D), F32),
                   jax.ShapeDtypeStruct((S, D), MXU)],
        compiler_params=_params())(y, target)


def _head_sum(v):
    r = lax.broadcasted_iota(jnp.int32, (BLOCK, BLOCK), 0) // HEAD_DIM
    c = lax.broadcasted_iota(jnp.int32, (BLOCK, BLOCK), 1) // HEAD_DIM
    ones = jnp.where(r == c, 1.0, 0.0).astype(jnp.bfloat16)
    hi = v.astype(jnp.bfloat16)
    lo = (v - hi.astype(F32)).astype(jnp.bfloat16)
    parts = []
    for t in range(v.shape[1] // BLOCK):
        sl = slice(t * BLOCK, (t + 1) * BLOCK)
        parts.append(jnp.dot(hi[:, sl], ones, preferred_element_type=F32)
                     + jnp.dot(lo[:, sl], ones, preferred_element_type=F32))
    return parts[0] if len(parts) == 1 else jnp.concatenate(parts, axis=-1)


def _swap_halves(v):
    w = v.shape[1]
    half = HEAD_DIM // 2
    lane = lax.broadcasted_iota(jnp.int32, v.shape, 1) % HEAD_DIM
    return jnp.where(lane < half, pltpu.roll(v, w - half, 1), pltpu.roll(v, half, 1))


def _norm_rope(xv, gain, cos, sin):
    r = lax.rsqrt(_head_sum(xv * xv) * (1.0 / HEAD_DIM) + EPS)
    xn = xv * r * gain
    return xn * cos + _swap_halves(xn) * sin


def _norm_rope_bwd(dy, xv, gain, cos, sin):
    r = lax.rsqrt(_head_sum(xv * xv) * (1.0 / HEAD_DIM) + EPS)
    xh = xv * r
    dxn = dy * cos + _swap_halves(dy * sin)
    dgain = jnp.sum(dxn * xh, axis=0, keepdims=True)
    dxh = dxn * gain
    dx = r * (dxh - xh * (_head_sum(dxh * xh) * (1.0 / HEAD_DIM)))
    return dx, dgain


def _fold_heads(v):
    acc = v[:, 0:BLOCK]
    for t in range(1, v.shape[1] // BLOCK):
        acc = acc + v[:, t * BLOCK:(t + 1) * BLOCK]
    return acc + pltpu.roll(acc, HEAD_DIM, 1)


def _tile_lanes(v, width):
    return v if width == BLOCK else jnp.tile(v, (1, width // BLOCK))


def _low_half(rows):
    assert BLOCK == 2 * HEAD_DIM
    return lax.broadcasted_iota(jnp.int32, (rows, BLOCK), 1) < HEAD_DIM


def _spread_heads(v):
    low = _low_half(v.shape[0])
    out = []
    for t in range(v.shape[1] // BLOCK):
        tile = v[:, t * BLOCK:(t + 1) * BLOCK]
        swapped = pltpu.roll(tile, HEAD_DIM, 1)
        out += [jnp.where(low, tile, swapped), jnp.where(low, swapped, tile)]
    return jnp.concatenate(out, axis=-1)


def _gather_heads(v):
    low = _low_half(v.shape[0])
    out = []
    for t in range(v.shape[1] // (2 * BLOCK)):
        a, b = v[:, 2 * t * BLOCK:(2 * t + 1) * BLOCK], v[:, (2 * t + 1) * BLOCK:(2 * t + 2) * BLOCK]
        out.append(jnp.where(low, a + pltpu.roll(a, HEAD_DIM, 1), b + pltpu.roll(b, HEAD_DIM, 1)))
    return out[0] if len(out) == 1 else jnp.concatenate(out, axis=-1)


def _qk_prep(proj, qg, kg, cos, sin, AW, KW):
    S = proj.shape[0]
    tr = _tile(S, 256)
    scale = HEAD_DIM ** -0.5

    def body(q_ref, k_ref, v_ref, qg_ref, kg_ref, cos_ref, sin_ref, qo_ref, ko_ref, vo_ref):
        c, s = cos_ref[...], sin_ref[...]
        q = _norm_rope(q_ref[...], _tile_lanes(qg_ref[...], AW), _tile_lanes(c, AW), _tile_lanes(s, AW))
        k = _norm_rope(k_ref[...], _tile_lanes(kg_ref[...], KW), _tile_lanes(c, KW), _tile_lanes(s, KW))
        qo_ref[...] = (q * scale).astype(MXU)
        ko_ref[...] = _spread_heads(k).astype(MXU)
        vo_ref[...] = _spread_heads(v_ref[...]).astype(MXU)

    assert AW % KW == 0
    vec = pl.BlockSpec((1, BLOCK), lambda i: (0, 0))
    tab = pl.BlockSpec((tr, BLOCK), lambda i: (i, 0))
    wide = pl.BlockSpec((tr, 2 * KW), lambda i: (i, 0))
    return pl.pallas_call(
        body, name="qk_prep", grid=(S // tr,),
        in_specs=[pl.BlockSpec((tr, AW), lambda i: (i, 0)),
                  pl.BlockSpec((tr, KW), lambda i: (i, AW // KW)),
                  pl.BlockSpec((tr, KW), lambda i: (i, AW // KW + 1)), vec, vec, tab, tab],
        out_specs=[pl.BlockSpec((tr, AW), lambda i: (i, 0)), wide, wide],
        out_shape=[jax.ShapeDtypeStruct((S, AW), MXU), jax.ShapeDtypeStruct((S, 2 * KW), MXU),
                   jax.ShapeDtypeStruct((S, 2 * KW), MXU)],
        compiler_params=_params())(proj, proj, proj, qg, kg, cos, sin)


def _stack_heads(x, h0, nh):
    low = _low_half(BLOCK)
    parts = []
    for h in range(h0, h0 + nh):
        tile = x[:, (h // 2) * BLOCK:(h // 2 + 1) * BLOCK]
        parts.append(jnp.where(low if h % 2 == 0 else jnp.logical_not(low), tile, jnp.zeros_like(tile)))
    return jnp.concatenate(parts, axis=0)


def _unstack_heads(y):
    low = _low_half(BLOCK)
    tiles = [jnp.where(low, y[2 * t * BLOCK:(2 * t + 1) * BLOCK], y[(2 * t + 1) * BLOCK:(2 * t + 2) * BLOCK])
             for t in range(y.shape[0] // (2 * BLOCK))]
    return tiles[0] if len(tiles) == 1 else jnp.concatenate(tiles, axis=-1)


def _band_t(n):
    key = lax.broadcasted_iota(jnp.int32, (2 * BLOCK, BLOCK), 0)
    qry = lax.broadcasted_iota(jnp.int32, (2 * BLOCK, BLOCK), 1)
    return (key > qry) & (key <= qry + BLOCK) & ((key >= BLOCK) | (n > 0))


def _attn_probs_t(ok, qs, kcat, h0, nh, sink_ref):
    st = jnp.where(ok, lax.dot_general(kcat, qs, NT, preferred_element_type=F32), -1e30)
    sk = jnp.concatenate([jnp.full((1, BLOCK), sink_ref[0, h], F32) for h in range(h0, h0 + nh)], axis=1)
    m = jnp.maximum(jnp.max(st, axis=0, keepdims=True), sk)
    e = jnp.exp(st - m)
    es = jnp.exp(sk - m)
    rz = 1.0 / (jnp.sum(e, axis=0, keepdims=True) + es)
    return e * rz, es * rz, rz


def _attn_fwd(qr, kr, vb, sinks):
    S, AW = qr.shape
    KW = kr.shape[1]
    nb = S // BLOCK
    nkv = KW // BLOCK
    qpk = AW // (nkv * HEAD_DIM)
    nh = min(ATTN_STACK, qpk)
    assert nh % 2 == 0 and qpk % nh == 0

    def body(sink_ref, q_ref, kp_ref, kc_ref, vp_ref, vc_ref, o_ref):
        n = pl.program_id(0)
        q, kp, kc, vp, vc = q_ref[...], kp_ref[...], kc_ref[...], vp_ref[...], vc_ref[...]
        ok = jnp.concatenate([_band_t(n)] * nh, axis=1)
        outs = []
        for g in range(nkv):
            kcat = jnp.concatenate([kp[:, g * BLOCK:(g + 1) * BLOCK], kc[:, g * BLOCK:(g + 1) * BLOCK]], axis=0)
            vcat = jnp.concatenate([vp[:, g * BLOCK:(g + 1) * BLOCK], vc[:, g * BLOCK:(g + 1) * BLOCK]], axis=0)
            for h0 in range(g * qpk, (g + 1) * qpk, nh):
                pt, _, _ = _attn_probs_t(ok, _stack_heads(q, h0, nh), kcat, h0, nh, sink_ref)
                outs.append(_unstack_heads(lax.dot_general(pt.astype(MXU), vcat, TN, preferred_element_type=F32)))
        o_ref[...] = jnp.concatenate(outs, axis=-1).astype(MXU)

    cur = lambda n: (n, 0)
    prev = lambda n: (jnp.maximum(n - 1, 0), 0)
    return pl.pallas_call(
        body, name="attn_fwd", grid=(nb,),
        in_specs=[pl.BlockSpec(memory_space=pltpu.SMEM), pl.BlockSpec((BLOCK, AW), cur),
                  pl.BlockSpec((BLOCK, KW), prev), pl.BlockSpec((BLOCK, KW), cur),
                  pl.BlockSpec((BLOCK, KW), prev), pl.BlockSpec((BLOCK, KW), cur)],
        out_specs=pl.BlockSpec((BLOCK, AW), cur),
        out_shape=jax.ShapeDtypeStruct((S, AW), MXU), compiler_params=_params())(sinks, qr, kr, kr, vb, vb)


def _attn_bwd(qr, kr, vb, sinks, dattn):
    S, AW = qr.shape
    KW = kr.shape[1]
    nb = S // BLOCK
    nkv = KW // BLOCK
    qpk = AW // (nkv * HEAD_DIM)
    nh = min(ATTN_STACK, qpk)
    scale = HEAD_DIM ** -0.5

    def body(sink_ref, q_ref, kp_ref, kc_ref, vp_ref, vc_ref, do_ref,
             dq_ref, dkp_ref, dkc_ref, dvp_ref, dvc_ref, dsink_ref):
        n = pl.program_id(0)
        q, kp, kc, vp, vc = q_ref[...], kp_ref[...], kc_ref[...], vp_ref[...], vc_ref[...]
        do = do_ref[...].astype(MXU)
        lane = lax.broadcasted_iota(jnp.int32, (1, BLOCK), 1)
        ok = jnp.concatenate([_band_t(n)] * nh, axis=1)
        dsink = jnp.zeros((1, BLOCK), F32)
        dqs, dkps, dkcs, dvps, dvcs = [], [], [], [], []
        for g in range(nkv):
            kcat = jnp.concatenate([kp[:, g * BLOCK:(g + 1) * BLOCK], kc[:, g * BLOCK:(g + 1) * BLOCK]], axis=0)
            vcat = jnp.concatenate([vp[:, g * BLOCK:(g + 1) * BLOCK], vc[:, g * BLOCK:(g + 1) * BLOCK]], axis=0)
            dk, dv = None, None
            for h0 in range(g * qpk, (g + 1) * qpk, nh):
                qs = _stack_heads(q, h0, nh)
                dos = _stack_heads(do, h0, nh)
                pt, ps, _ = _attn_probs_t(ok, qs, kcat, h0, nh, sink_ref)
                dpt = lax.dot_general(vcat, dos, NT, preferred_element_type=F32)
                delta = jnp.sum(pt * dpt, axis=0, keepdims=True)
                dst = (pt * (dpt - delta)).astype(MXU)
                dsk = -ps * delta
                dv_part = jnp.dot(pt.astype(MXU), dos, preferred_element_type=F32)
                dk_part = jnp.dot(dst, qs, preferred_element_type=F32)
                dqs.append(_unstack_heads(lax.dot_general(dst, kcat, TN, preferred_element_type=F32) * scale))
                dk = dk_part if dk is None else dk + dk_part
                dv = dv_part if dv is None else dv + dv_part
                for j in range(nh):
                    tot = jnp.sum(dsk[:, j * BLOCK:(j + 1) * BLOCK], axis=1, keepdims=True)
                    dsink = dsink + jnp.where(lane == h0 + j, tot, 0.0)
            dkps.append(dk[:BLOCK])
            dkcs.append(dk[BLOCK:])
            dvps.append(dv[:BLOCK])
            dvcs.append(dv[BLOCK:])
        dq_ref[...] = jnp.concatenate(dqs, axis=-1)
        dkp_ref[...] = jnp.concatenate(dkps, axis=-1)
        dkc_ref[...] = jnp.concatenate(dkcs, axis=-1)
        dvp_ref[...] = jnp.concatenate(dvps, axis=-1)
        dvc_ref[...] = jnp.concatenate(dvcs, axis=-1)

        @pl.when(n == 0)
        def _():
            dsink_ref[...] = jnp.zeros(dsink_ref.shape, F32)

        dsink_ref[...] += dsink

    cur = lambda n: (n, 0)
    prev = lambda n: (jnp.maximum(n - 1, 0), 0)
    kv = jax.ShapeDtypeStruct((S, KW), F32)
    kvspec = pl.BlockSpec((BLOCK, KW), cur)
    return pl.pallas_call(
        body, name="attn_bwd", grid=(nb,),
        in_specs=[pl.BlockSpec(memory_space=pltpu.SMEM), pl.BlockSpec((BLOCK, AW), cur),
                  pl.BlockSpec((BLOCK, KW), prev), kvspec, pl.BlockSpec((BLOCK, KW), prev), kvspec,
                  pl.BlockSpec((BLOCK, AW), cur)],
        out_specs=[pl.BlockSpec((BLOCK, AW), cur), kvspec, kvspec, kvspec, kvspec,
                   pl.BlockSpec((1, BLOCK), lambda n: (0, 0))],
        out_shape=[jax.ShapeDtypeStruct((S, AW), F32), kv, kv, kv, kv, jax.ShapeDtypeStruct((1, BLOCK), F32)],
        compiler_params=_params())(sinks, qr, kr, kr, vb, vb, dattn)


def _qk_prep_bwd(proj, qg, kg, cos, sin, dq, dkp, dkc, dvp, dvc, AW, KW):
    S = proj.shape[0]
    nb = S // BLOCK

    def body(q_ref, k_ref, qg_ref, kg_ref, cos_ref, sin_ref, dq_ref, dkp_ref, dkc_ref, dvp_ref, dvc_ref,
             o_ref, dqg_ref, dkg_ref):
        n = pl.program_id(0)
        c, s = cos_ref[...], sin_ref[...]
        has_next = jnp.where(n < nb - 1, 1.0, 0.0)
        dk = _gather_heads(dkc_ref[...] + has_next * dkp_ref[...])
        dv = _gather_heads(dvc_ref[...] + has_next * dvp_ref[...])
        dxq, dqg = _norm_rope_bwd(dq_ref[...], q_ref[...], _tile_lanes(qg_ref[...], AW),
                                  _tile_lanes(c, AW), _tile_lanes(s, AW))
        dxk, dkg = _norm_rope_bwd(dk, k_ref[...], _tile_lanes(kg_ref[...], KW),
                                  _tile_lanes(c, KW), _tile_lanes(s, KW))
        o_ref[...] = jnp.concatenate([dxq, dxk, dv], axis=-1).astype(MXU)

        @pl.when(n == 0)
        def _():
            dqg_ref[...] = jnp.zeros(dqg_ref.shape, F32)
            dkg_ref[...] = jnp.zeros(dkg_ref.shape, F32)

        dqg_ref[...] += _fold_heads(dqg)
        dkg_ref[...] += _fold_heads(dkg)

    cur = lambda n: (n, 0)
    nxt = lambda n: (jnp.minimum(n + 1, nb - 1), 0)
    vec = pl.BlockSpec((1, BLOCK), lambda n: (0, 0))
    tab = pl.BlockSpec((BLOCK, BLOCK), cur)
    return pl.pallas_call(
        body, name="qk_prep_bwd", grid=(nb,),
        in_specs=[pl.BlockSpec((BLOCK, AW), cur), pl.BlockSpec((BLOCK, KW), lambda n: (n, AW // KW)),
                  vec, vec, tab, tab, pl.BlockSpec((BLOCK, AW), cur),
                  pl.BlockSpec((BLOCK, 2 * KW), nxt), pl.BlockSpec((BLOCK, 2 * KW), cur),
                  pl.BlockSpec((BLOCK, 2 * KW), nxt), pl.BlockSpec((BLOCK, 2 * KW), cur)],
        out_specs=[pl.BlockSpec((BLOCK, AW + 2 * KW), cur), vec, vec],
        out_shape=[jax.ShapeDtypeStruct((S, AW + 2 * KW), MXU), jax.ShapeDtypeStruct((1, BLOCK), F32),
                   jax.ShapeDtypeStruct((1, BLOCK), F32)],
        compiler_params=_params())(proj, proj, qg, kg, cos, sin, dq, dkp, dkc, dvp, dvc)


SGU_LANES = 512
SGU_ROWS = 256


def _sgu_group(v, lng, lnb, w_f32, b):
    rows = v.shape[0]
    mu = jnp.mean(v, axis=-1, keepdims=True)
    vc = v - mu
    r = lax.rsqrt(jnp.mean(vc * vc, axis=-1, keepdims=True) + EPS)
    xh = vc * r
    vn = (xh * lng + lnb).astype(MXU)
    row = lax.broadcasted_iota(jnp.int32, (BLOCK, BLOCK), 0)
    col = lax.broadcasted_iota(jnp.int32, (BLOCK, BLOCK), 1)
    tri = row >= col
    w = jnp.where(tri, w_f32, 0.0).astype(MXU)
    chunks = [jnp.dot(w, vn[k * BLOCK:(k + 1) * BLOCK], preferred_element_type=F32) + b for k in range(rows // BLOCK)]
    s = chunks[0] if len(chunks) == 1 else jnp.concatenate(chunks, axis=0)
    return xh, r, vn, w, s, tri


def _sgu_layout(S, u_col):
    SW = SGU_GROUPS * BLOCK
    lb, tr = min(SGU_LANES, SW), min(SGU_ROWS, S)
    assert u_col % lb == 0 and SW % lb == 0 and S % tr == 0
    ub, nlb, gpb = u_col // lb, SW // lb, lb // BLOCK
    specs = [pl.BlockSpec((tr, lb), lambda j, i: (i, ub + j)), pl.BlockSpec((tr, lb), lambda j, i: (i, ub + nlb + j)),
             pl.BlockSpec((1, lb), lambda j, i: (0, j)), pl.BlockSpec((1, lb), lambda j, i: (0, j)),
             pl.BlockSpec((gpb, BLOCK, BLOCK), lambda j, i: (j, 0, 0)),
             pl.BlockSpec((gpb, BLOCK, 1), lambda j, i: (j, 0, 0))]
    return lb, tr, gpb, nlb, specs


def _sgu_fwd(proj, lng, lnb, ws, bs, u_col):
    S = proj.shape[0]
    lb, tr, gpb, nlb, specs = _sgu_layout(S, u_col)

    def body(pu_ref, pv_ref, lng_ref, lnb_ref, w_ref, b_ref, o_ref):
        u = _gelu(pu_ref[...])
        v = _gelu(pv_ref[...])
        outs = []
        for g in range(gpb):
            sl = slice(g * BLOCK, (g + 1) * BLOCK)
            s = _sgu_group(v[:, sl], lng_ref[:, sl], lnb_ref[:, sl], w_ref[g], b_ref[g])[4]
            outs.append(u[:, sl] * s)
        o_ref[...] = (outs[0] if gpb == 1 else jnp.concatenate(outs, axis=-1)).astype(MXU)

    return pl.pallas_call(
        body, name="sgu_fwd", grid=(nlb, S // tr), in_specs=specs,
        out_specs=pl.BlockSpec((tr, lb), lambda j, i: (i, j)),
        out_shape=jax.ShapeDtypeStruct((S, nlb * lb), MXU), compiler_params=_params())(proj, proj, lng, lnb, ws, bs)


def _sgu_bwd(proj, lng, lnb, ws, bs, dsgu, u_col, after):
    S = proj.shape[0]
    G = SGU_GROUPS
    lb, tr, gpb, nlb, specs = _sgu_layout(S, u_col)
    nch = tr // BLOCK

    def body(pu_ref, pv_ref, lng_ref, lnb_ref, w_ref, b_ref, do_ref, after_ref,
             dpu_ref, dpv_ref, dw_ref, db_ref, dlng_ref, dlnb_ref):
        pu, pv, do = pu_ref[...], pv_ref[...], do_ref[...]
        u = _gelu(pu)
        v = _gelu(pv)

        @pl.when(pl.program_id(1) == 0)
        def _():
            dw_ref[...] = jnp.zeros(dw_ref.shape, F32)
            db_ref[...] = jnp.zeros(db_ref.shape, F32)
            dlng_ref[...] = jnp.zeros(dlng_ref.shape, F32)
            dlnb_ref[...] = jnp.zeros(dlnb_ref.shape, F32)

        ss, dvs, dlng, dlnb = [], [], [], []
        for g in range(gpb):
            sl = slice(g * BLOCK, (g + 1) * BLOCK)
            xh, r, vn, w, s, tri = _sgu_group(v[:, sl], lng_ref[:, sl], lnb_ref[:, sl], w_ref[g], b_ref[g])
            ds = do[:, sl] * u[:, sl]
            dsb = ds.astype(MXU)
            dw, db, dvn = None, None, []
            for k in range(nch):
                rows = slice(k * BLOCK, (k + 1) * BLOCK)
                part = lax.dot_general(dsb[rows], vn[rows], NT, preferred_element_type=F32)
                dw = part if dw is None else dw + part
                rowsum = jnp.sum(ds[rows], axis=-1, keepdims=True)
                db = rowsum if db is None else db + rowsum
                dvn.append(lax.dot_general(w, dsb[rows], TN, preferred_element_type=F32))
            dvn = dvn[0] if nch == 1 else jnp.concatenate(dvn, axis=0)
            dw_ref[g] += jnp.where(tri, dw, 0.0)
            db_ref[g] += db
            dxh = dvn * lng_ref[:, sl]
            dvs.append(r * (dxh - jnp.mean(dxh, axis=-1, keepdims=True)
                            - xh * jnp.mean(dxh * xh, axis=-1, keepdims=True)))
            dlng.append(jnp.sum(dvn * xh, axis=0, keepdims=True))
            dlnb.append(jnp.sum(dvn, axis=0, keepdims=True))
            ss.append(s)
        cat = lambda parts: parts[0] if gpb == 1 else jnp.concatenate(parts, axis=-1)
        dpu_ref[...] = (do * cat(ss) * _gelu_grad(pu)).astype(MXU)
        dpv_ref[...] = (cat(dvs) * _gelu_grad(pv)).astype(MXU)
        dlng_ref[...] += cat(dlng)
        dlnb_ref[...] += cat(dlnb)

    tile = pl.BlockSpec((tr, lb), lambda j, i: (i, j))
    vec = pl.BlockSpec((1, lb), lambda j, i: (0, j))
    half = jax.ShapeDtypeStruct((S, G * BLOCK), MXU)
    return pl.pallas_call(
        body, name="sgu_bwd", grid=(nlb, S // tr), in_specs=specs + [tile, ANY],
        out_specs=[tile, tile, pl.BlockSpec((gpb, BLOCK, BLOCK), lambda j, i: (j, 0, 0)),
                   pl.BlockSpec((gpb, BLOCK, 1), lambda j, i: (j, 0, 0)), vec, vec],
        out_shape=[half, half, jax.ShapeDtypeStruct((G, BLOCK, BLOCK), F32),
                   jax.ShapeDtypeStruct((G, BLOCK, 1), F32),
                   jax.ShapeDtypeStruct((1, G * BLOCK), F32), jax.ShapeDtypeStruct((1, G * BLOCK), F32)],
        compiler_params=_params())(proj, proj, lng, lnb, ws, bs, dsgu, after)


def _store_f32(vals, extra, outs):
    for v, o in zip(vals, outs):
        o[...] = v


def _store_mxu(vals, extra, outs):
    for v, o in zip(vals, outs):
        o[...] = v.astype(MXU)


def _proj_in(h, w):
    S, D = h.shape
    Ns = w.shape[2]
    tm, tn = _tile(S, 1024), _tile(Ns, 1024)
    npb = Ns // tn
    return _mm("proj_in", (S // tm, N_CHIPS, npb), 0, [h, w],
               [pl.BlockSpec((tm, D), lambda i, s, j: (i, 0)), pl.BlockSpec((None, D, tn), lambda i, s, j: (s, 0, j))],
               [(0, 1, 0)], NN, 0, [jax.ShapeDtypeStruct((S, N_CHIPS * Ns), F32)],
               [pl.BlockSpec((tm, tn), lambda i, s, j: (i, s * npb + j))], [None], _store_f32)[0]


def _branches(attn, sgu, wa, ws, proj, gate0):
    S, AW = attn.shape
    SW = sgu.shape[1]
    Nb = wa.shape[2]
    D = N_CHIPS * Nb
    tm = _tile(S, 512)
    assert gate0 % Nb == 0
    ga, gb = gate0 // Nb, (gate0 + D) // Nb

    def epilogue(vals, extra, outs):
        a, b = vals
        outs[0][...] = (_sigmoid(extra[0][...]) * a + _sigmoid(extra[1][...]) * b).astype(MXU)
        outs[1][...] = a
        outs[2][...] = b

    tile = pl.BlockSpec((tm, Nb), lambda i, s: (i, s))
    wspec = lambda k: pl.BlockSpec((None, k, Nb), lambda i, s: (s, 0, 0))
    f = jax.ShapeDtypeStruct((S, D), F32)
    return _mm("branches", (S // tm, N_CHIPS), 0, [attn, sgu, wa, ws, proj, proj],
               [pl.BlockSpec((tm, AW), lambda i, s: (i, 0)), pl.BlockSpec((tm, SW), lambda i, s: (i, 0)),
                wspec(AW), wspec(SW), pl.BlockSpec((tm, Nb), lambda i, s: (i, ga + s)),
                pl.BlockSpec((tm, Nb), lambda i, s: (i, gb + s))],
               [(0, 2, 0), (1, 3, 1)], NN, 2, [jax.ShapeDtypeStruct((S, D), MXU), f, f], [tile] * 3,
               [None, None], epilogue, chunk=MXU_CHUNK)


def _rows_mm(name, a, w, res):
    S = a.shape[0]
    _, K, N = w.shape
    tm, tn = _tile(S, 1024), _tile(N, 1024)

    def epilogue(vals, extra, outs):
        outs[0][...] = extra[0][...] + vals[0]

    out = pl.BlockSpec((tm, tn), lambda i, j, s: (i, j))
    return _mm(name, (S // tm, N // tn, N_CHIPS), 1, [a, w, res],
               [pl.BlockSpec((tm, K), lambda i, j, s: (i, s)), pl.BlockSpec((None, K, tn), lambda i, j, s: (s, 0, j)), out],
               [(0, 1, 0)], NN, 1, [jax.ShapeDtypeStruct((S, N), F32)], [out], [(tm, tn)], epilogue)[0]


def _gate_up(h2, wg, wu):
    S, D = h2.shape
    Nf = wg.shape[2]
    tm = _tile(S, 256)

    def epilogue(vals, extra, outs):
        g, u = vals
        outs[0][...] = g
        outs[1][...] = u
        outs[2][...] = (g * _sigmoid(g) * u).astype(MXU)

    w = pl.BlockSpec((None, D, Nf), lambda s, i: (s, 0, 0))
    o = pl.BlockSpec((tm, Nf), lambda s, i: (i, s))
    f = jax.ShapeDtypeStruct((S, N_CHIPS * Nf), F32)
    return _mm("gate_up", (N_CHIPS, S // tm), 0, [h2, wg, wu],
               [pl.BlockSpec((tm, D), lambda s, i: (i, 0)), w, w], [(0, 1, 0), (0, 2, 1)], NN, 0,
               [f, f, jax.ShapeDtypeStruct((S, N_CHIPS * Nf), MXU)], [o, o, o], [None, None], epilogue,
               chunk=MXU_CHUNK)


def _down_bwd(dyb, wd, g, u):
    S, D = dyb.shape
    Kf = wd.shape[1]
    tm = _tile(S, 512)

    def epilogue(vals, extra, outs):
        da, gv, uv = vals[0], extra[0][...], extra[1][...]
        sg = _sigmoid(gv)
        outs[0][...] = (da * uv * sg * (1.0 + gv * (1.0 - sg))).astype(MXU)
        outs[1][...] = (da * gv * sg).astype(MXU)

    t = pl.BlockSpec((tm, Kf), lambda i, s: (i, s))
    o = jax.ShapeDtypeStruct((S, N_CHIPS * Kf), MXU)
    return _mm("down_bwd", (S // tm, N_CHIPS), 0, [dyb, wd, g, u],
               [pl.BlockSpec((tm, D), lambda i, s: (i, 0)), pl.BlockSpec((None, Kf, D), lambda i, s: (s, 0, 0)), t, t],
               [(0, 1, 0)], NT, 2, [o, o], [t, t], [None], epilogue, chunk=MXU_CHUNK)


def _out_bwd(dxb, wo, proj, ba, bb, gate0):
    S, D = dxb.shape
    Ko = wo.shape[1]
    tm = _tile(S, 512)
    assert gate0 % Ko == 0
    ga, gb = gate0 // Ko, (gate0 + D) // Ko

    def epilogue(vals, extra, outs):
        dm = vals[0]
        sa, sb = _sigmoid(extra[0][...]), _sigmoid(extra[1][...])
        outs[0][...] = (dm * sa).astype(MXU)
        outs[1][...] = (dm * sb).astype(MXU)
        outs[2][...] = (dm * extra[2][...] * sa * (1.0 - sa)).astype(MXU)
        outs[3][...] = (dm * extra[3][...] * sb * (1.0 - sb)).astype(MXU)

    t = pl.BlockSpec((tm, Ko), lambda i, s: (i, s))
    o = jax.ShapeDtypeStruct((S, D), MXU)
    return _mm("out_bwd", (S // tm, N_CHIPS), 0, [dxb, wo, proj, proj, ba, bb],
               [pl.BlockSpec((tm, D), lambda i, s: (i, 0)), pl.BlockSpec((None, Ko, D), lambda i, s: (s, 0, 0)),
                pl.BlockSpec((tm, Ko), lambda i, s: (i, ga + s)), pl.BlockSpec((tm, Ko), lambda i, s: (i, gb + s)), t, t],
               [(0, 1, 0)], NT, 4, [o] * 4, [t] * 4, [None], epilogue, chunk=MXU_CHUNK)


def _dx_cols(name, terms, n_out, after=None):
    S = terms[0][0].shape[0]
    _, K, Ns = terms[0][1].shape
    tm, tko, tn = _tile(S, 1024), _tile(K, 1024), _tile(Ns, 1920 if len(terms) == 1 else 1408)
    npb = Ns // tn
    operands, specs, pairs = [], [], []
    for t, (dy, w, k) in enumerate(terms):
        assert w.shape == (N_CHIPS, K, Ns)
        operands += [dy, w]
        specs += [pl.BlockSpec((tm, tn), lambda i, jk, s, jn: (i, s * npb + jn)),
                  pl.BlockSpec((None, tko, tn), lambda i, jk, s, jn: (s, jk, jn))]
        pairs.append((2 * t, 2 * t + 1, k))
    out = pl.BlockSpec((tm, tko), lambda i, jk, s, jn: (i, jk))
    return _mm(name, (S // tm, K // tko, N_CHIPS, npb), 2, operands, specs, pairs, NT, 0,
               [jax.ShapeDtypeStruct((S, K), F32)] * n_out, [out] * n_out, [(tm, tko)] * n_out, _store_f32, after)


def _dw_cols(name, a, dy):
    S, K = a.shape
    Ns = dy.shape[1] // N_CHIPS
    tk, tn = _tile(K, 512), _tile(Ns, 1408)
    npb = Ns // tn
    return _mm(name, (K // tk, N_CHIPS, npb), 0, [a, dy],
               [pl.BlockSpec((S, tk), lambda jk, s, jn: (0, jk)), pl.BlockSpec((S, tn), lambda jk, s, jn: (0, s * npb + jn))],
               [(0, 1, 0)], TN, 0, [jax.ShapeDtypeStruct((N_CHIPS, K, Ns), MXU)],
               [pl.BlockSpec((None, tk, tn), lambda jk, s, jn: (s, jk, jn))], [None], _store_mxu)[0]


def _dw_rows(name, a, dy):
    S = a.shape[0]
    K = a.shape[1] // N_CHIPS
    N = dy.shape[1]
    tk, tn = _tile(K, 1408), _tile(N, 1024)
    nkb = K // tk
    return _mm(name, (N_CHIPS, nkb, N // tn), 0, [a, dy],
               [pl.BlockSpec((S, tk), lambda s, jk, jn: (0, s * nkb + jk)), pl.BlockSpec((S, tn), lambda s, jk, jn: (0, jn))],
               [(0, 1, 0)], TN, 0, [jax.ShapeDtypeStruct((N_CHIPS, K, N), MXU)],
               [pl.BlockSpec((None, tk, tn), lambda s, jk, jn: (s, jk, jn))], [None], _store_mxu)[0]


def _layer_fwd(x, stream, layer, last, sp, cos, sin, dims):
    AW, KW, gate0, u_col = dims
    h = _rms_fwd("mix_norm", x, sp["mix_norm"])
    w = stream.finish(layer, 0, h)
    proj = _proj_in(h, w["w_in"])
    qr, kr, vb = _qk_prep(proj, sp["q_norm"], sp["k_norm"], cos, sin, AW, KW)
    stream.forward(layer, 1, qr)
    attn = _attn_fwd(qr, kr, vb, sp["sinks"])
    w.update(stream.finish(layer, 1, attn))
    sgu = _sgu_fwd(proj, sp["sgu_ln_g"], sp["sgu_ln_b"], sp["w_spatial"], sp["b_spatial"], u_col)
    merged, ba, bb = _branches(attn, sgu, w["w_attn_branch"], w["w_sgu_branch"], proj, gate0)
    stream.forward(layer, 2, merged)
    x1 = _rows_mm("out_proj", merged, w["w_out"], x)
    w.update(stream.finish(layer, 2, x1))
    h2 = _rms_fwd("ffn_norm", x1, sp["ffn_norm"])
    stream.forward(layer, 3, h2)
    g, u, act = _gate_up(h2, w["w_gate"], w["w_up"])
    w.update(stream.finish(layer, 3, g))
    x2 = _rows_mm("down_proj", act, w["w_down"], x1)
    if not last:
        stream.forward(layer + 1, 0, x2)
    saved = dict(x=x, h=h, proj=proj, qr=qr, kr=kr, vb=vb, attn=attn, sgu=sgu, merged=merged, ba=ba, bb=bb,
                 x1=x1, h2=h2, g=g, u=u, act=act)
    return x2, saved, w


def _layer_bwd(dy, dyb, w, sp, sv, cos, sin, dims, reducer, layer):
    AW, KW, gate0, u_col = dims
    big, small = {}, {}
    dg, du = _down_bwd(dyb, w["w_down"], sv["g"], sv["u"])
    big["w_down"] = _dw_rows("dw_down", sv["act"], dyb)
    big["w_gate"] = _dw_cols("dw_gate", sv["h2"], dg)
    big["w_up"] = _dw_cols("dw_up", sv["h2"], du)
    token = reducer.start(layer, 2, big)
    dh2 = _dx_cols("dh2", [(dg, w["w_gate"], 0), (du, w["w_up"], 0)], 1, token)[0]
    token = reducer.scatter(layer, 2, dh2)
    dx1, dx1b, small["ffn_norm"] = _rms_bwd("ffn_norm_bwd", dh2, sv["x1"], sp["ffn_norm"], dy, token)
    dba, dbb, dgla, dglb = _out_bwd(dx1b, w["w_out"], sv["proj"], sv["ba"], sv["bb"], gate0)
    big["w_out"] = _dw_rows("dw_out", sv["merged"], dx1b)
    big["w_attn_branch"] = _dw_cols("dw_attn_branch", sv["attn"], dba)
    big["w_sgu_branch"] = _dw_cols("dw_sgu_branch", sv["sgu"], dbb)
    token = reducer.start(layer, 1, big)
    dattn, dsgu = _dx_cols("dbranch_in", [(dba, w["w_attn_branch"], 0), (dbb, w["w_sgu_branch"], 1)], 2, token)
    token = reducer.scatter(layer, 1, dsgu)
    dpu, dpv, small["w_spatial"], db, small["sgu_ln_g"], small["sgu_ln_b"] = _sgu_bwd(
        sv["proj"], sp["sgu_ln_g"], sp["sgu_ln_b"], sp["w_spatial"], sp["b_spatial"], dsgu, u_col, token)
    small["b_spatial"] = db[:, :, 0]
    dq, dkp, dkc, dvp, dvc, dsink = _attn_bwd(sv["qr"], sv["kr"], sv["vb"], sp["sinks"], dattn)
    small["sinks"] = dsink[:, :sp["sinks"].shape[1]]
    dqkv, dqg, dkg = _qk_prep_bwd(sv["proj"], sp["q_norm"], sp["k_norm"], cos, sin, dq, dkp, dkc, dvp, dvc, AW, KW)
    small["q_norm"] = dqg[:, :HEAD_DIM]
    small["k_norm"] = dkg[:, :HEAD_DIM]
    dproj = jnp.concatenate([dqkv, dpu, dpv, dgla, dglb], axis=1)
    big["w_in"] = _dw_cols("dw_in", sv["h"], dproj)
    token = reducer.start(layer, 0, big)
    dh = _dx_cols("dh", [(dproj, w["w_in"], 0)], 1, token)[0]
    token = reducer.scatter(layer, 0, dh)
    dx, dxb, small["mix_norm"] = _rms_bwd("mix_norm_bwd", dh, sv["x"], sp["mix_norm"], dx1, token)
    return dx, dxb, small


def _place():
    x, y, c = lax.axis_index("x"), lax.axis_index("y"), lax.axis_index("c")
    chips = [(1 - x, y), (x, 1 - y), (1 - x, 1 - y)]
    return x, y, c, chips


def _half_rows(c, rows):
    h = rows // 2
    assert h % 16 == 0
    return pl.ds(pl.multiple_of(c * h, 16), h)


def _row_tile(rows, pref):
    best = None
    for t in range(16, min(rows, pref) + 1, 16):
        if rows % t == 0:
            best = t
    assert best is not None, rows
    return best


def _cast_own(name, chip, w, layer):
    _, R, C = w.shape
    tr = _row_tile(R, 512)

    def body(chip_ref, w_ref, o_ref):
        o_ref[...] = w_ref[...].astype(MXU)

    return pl.pallas_call(
        body, name=name, out_shape=jax.ShapeDtypeStruct((N_CHIPS, R, C), MXU),
        grid_spec=pltpu.PrefetchScalarGridSpec(
            num_scalar_prefetch=1, grid=(R // tr,),
            in_specs=[pl.BlockSpec((None, tr, C), lambda i, chip_ref: (layer, i, 0))],
            out_specs=pl.BlockSpec((None, tr, C), lambda i, chip_ref: (chip_ref[0], i, 0))),
        compiler_params=_params())(chip, w)


HBM = pl.BlockSpec(memory_space=pltpu.HBM)
SEM = pl.BlockSpec(memory_space=pltpu.SEMAPHORE)
DATAFLOW = pltpu.SideEffectType.DATAFLOW_SIDE_EFFECTING


def _gather_copies(bufs, send_sem, recv_sem):
    x, y, c, chips = _place()

    def ici(a, j, block):
        px, py = chips[j]
        blk = bufs[a].at[block, _half_rows(c, bufs[a].shape[1])]
        return pltpu.make_async_remote_copy(
            src_ref=blk, dst_ref=blk, send_sem=send_sem.at[3 * a + j], recv_sem=recv_sem.at[3 * a + j],
            device_id=(px, py, c), device_id_type=MESH)

    def d2d(a, j, core):
        px, py = chips[j]
        blk = bufs[a].at[2 * px + py, _half_rows(core, bufs[a].shape[1])]
        return pltpu.make_async_remote_copy(
            src_ref=blk, dst_ref=blk, send_sem=send_sem.at[3 * a + j], recv_sem=recv_sem.at[3 * a + j],
            device_id=(x, y, 1 - c), device_id_type=MESH)

    return ici, d2d


def _in_hbm(bufs):
    return [pltpu.with_memory_space_constraint(b, pltpu.HBM) for b in bufs]


def _gather_start(name, bufs, after):
    n = len(bufs)

    def body(*refs):
        dst = refs[n + 1:2 * n + 1]
        send_sem, recv_sem, token = refs[2 * n + 1:]
        x, y, c, chips = _place()
        ici, _ = _gather_copies(dst, send_sem, recv_sem)
        for a in range(n):
            for j in range(3):
                ici(a, j, 2 * x + y).start()
        token[...] = jnp.zeros(token.shape, token.dtype)

    sems = pltpu.SemaphoreType.DMA((3 * n,))
    outs = pl.pallas_call(
        body, name=name, in_specs=[HBM] * n + [ANY],
        out_specs=[HBM] * n + [SEM, SEM, pl.BlockSpec(memory_space=pltpu.VMEM)],
        out_shape=[pltpu.HBM(b.shape, b.dtype) for b in bufs] + [sems, sems, jax.ShapeDtypeStruct((8, BLOCK), F32)],
        input_output_aliases={a: a for a in range(n)},
        compiler_params=pltpu.CompilerParams(has_side_effects=DATAFLOW))(*_in_hbm(bufs), after)
    return outs[:n], outs[n], outs[n + 1], outs[n + 2]


def _gather_forward(name, bufs, ici_send, ici_recv, after):
    n = len(bufs)

    def body(*refs):
        ici_send_ref, ici_recv_ref = refs[n], refs[n + 1]
        dst = refs[n + 3:2 * n + 3]
        d2d_send, d2d_recv = refs[2 * n + 3:]
        x, y, c, chips = _place()
        ici, _ = _gather_copies(dst, ici_send_ref, ici_recv_ref)
        _, d2d = _gather_copies(dst, d2d_send, d2d_recv)
        for a in range(n):
            for j, (px, py) in enumerate(chips):
                ici(a, j, 2 * px + py).wait_recv()
                d2d(a, j, c).start()
        for a in range(n):
            for j in range(3):
                ici(a, j, 2 * x + y).wait_send()

    sems = pltpu.SemaphoreType.DMA((3 * n,))
    outs = pl.pallas_call(
        body, name=name, in_specs=[HBM] * n + [SEM, SEM, ANY], out_specs=[HBM] * n + [SEM, SEM],
        out_shape=[pltpu.HBM(b.shape, b.dtype) for b in bufs] + [sems, sems],
        input_output_aliases={a: a for a in range(n)},
        compiler_params=pltpu.CompilerParams(has_side_effects=DATAFLOW))(*bufs, ici_send, ici_recv, after)
    return outs[:n], outs[n], outs[n + 1]


def _gather_finish(name, bufs, d2d_send, d2d_recv, after):
    n = len(bufs)

    def body(*refs):
        send_ref, recv_ref = refs[n], refs[n + 1]
        dst = refs[n + 3:]
        x, y, c, chips = _place()
        _, d2d = _gather_copies(dst, send_ref, recv_ref)
        for a in range(n):
            for j in range(3):
                d2d(a, j, 1 - c).wait_recv()
                d2d(a, j, c).wait_send()

    return pl.pallas_call(
        body, name=name, in_specs=[HBM] * n + [SEM, SEM, ANY], out_specs=[HBM] * n,
        out_shape=[pltpu.HBM(b.shape, b.dtype) for b in bufs],
        input_output_aliases={a: a for a in range(n)},
        compiler_params=pltpu.CompilerParams(has_side_effects=DATAFLOW))(*bufs, d2d_send, d2d_recv, after)


GATHER = (("w_in",), ("w_attn_branch", "w_sgu_branch", "w_out"), ("w_gate", "w_up"), ("w_down",))
REDUCE = (("w_in",), ("w_attn_branch", "w_sgu_branch", "w_out"), ("w_gate", "w_up", "w_down"))


class _WeightStream:
    def __init__(self, started):
        self.started, self.passed = started, {}

    def forward(self, layer, group, after):
        bufs, send, recv = self.started[(layer, group)]
        self.passed[(layer, group)] = _gather_forward("gather_forward_%d_%d" % (layer, group), bufs, send, recv, after)

    def finish(self, layer, group, after):
        bufs, send, recv = self.passed[(layer, group)]
        done = _gather_finish("gather_finish_%d_%d" % (layer, group), bufs, send, recv, after)
        return dict(zip(GATHER[group], done))


def _pair_copies(grads, lands, send_sem, recv_sem):
    x, y, c, _ = _place()

    def make(a):
        theirs = _half_rows(1 - c, grads[a].shape[1])
        return pltpu.make_async_remote_copy(
            src_ref=grads[a].at[:, theirs], dst_ref=lands[a], send_sem=send_sem.at[a], recv_sem=recv_sem.at[a],
            device_id=(x, y, 1 - c), device_id_type=MESH)

    return make


def _pair_start(name, grads, after):
    n = len(grads)
    lands = [lax.empty((g.shape[0], g.shape[1] // 2, g.shape[2]), g.dtype) for g in grads]

    def body(*refs):
        src, dst = refs[2 * n + 1:3 * n + 1], refs[3 * n + 1:4 * n + 1]
        send_sem, recv_sem, token = refs[4 * n + 1:]
        copy = _pair_copies(src, dst, send_sem, recv_sem)
        for a in range(n):
            copy(a).start()
        token[...] = jnp.zeros(token.shape, token.dtype)

    sems = pltpu.SemaphoreType.DMA((n,))
    outs = pl.pallas_call(
        body, name=name, in_specs=[HBM] * (2 * n) + [ANY],
        out_specs=[HBM] * (2 * n) + [SEM, SEM, pl.BlockSpec(memory_space=pltpu.VMEM)],
        out_shape=[pltpu.HBM(b.shape, b.dtype) for b in grads + lands] + [sems, sems, jax.ShapeDtypeStruct((8, BLOCK), F32)],
        input_output_aliases={a: a for a in range(2 * n)},
        compiler_params=pltpu.CompilerParams(has_side_effects=DATAFLOW))(*_in_hbm(grads + lands), after)
    return outs[:n], outs[n:2 * n], outs[2 * n], outs[2 * n + 1], outs[2 * n + 2]


def _pair_finish(name, grads, lands, send_sem, recv_sem, after):
    n = len(grads)

    def body(*refs):
        send_ref, recv_ref = refs[2 * n], refs[2 * n + 1]
        src, dst = refs[2 * n + 3:3 * n + 3], refs[3 * n + 3:]
        copy = _pair_copies(src, dst, send_ref, recv_ref)
        for a in range(n):
            copy(a).wait_send()
            copy(a).wait_recv()

    outs = pl.pallas_call(
        body, name=name, in_specs=[HBM] * (2 * n) + [SEM, SEM, ANY], out_specs=[HBM] * (2 * n),
        out_shape=[pltpu.HBM(b.shape, b.dtype) for b in grads + lands],
        input_output_aliases={a: a for a in range(2 * n)},
        compiler_params=pltpu.CompilerParams(has_side_effects=DATAFLOW))(*grads, *lands, send_sem, recv_sem, after)
    return outs[:n], outs[n:]


def _pair_sum(name, core, g, p):
    _, h, C = p.shape
    tr = _row_tile(h, 512)
    nrb = h // tr

    def body(core_ref, g_ref, p_ref, o_ref):
        o_ref[...] = (g_ref[...].astype(F32) + p_ref[...].astype(F32)).astype(o_ref.dtype)

    spec = pl.BlockSpec((None, tr, C), lambda s, i, core_ref: (s, i, 0))
    return pl.pallas_call(
        body, name=name, out_shape=jax.ShapeDtypeStruct(p.shape, p.dtype),
        grid_spec=pltpu.PrefetchScalarGridSpec(
            num_scalar_prefetch=1, grid=(N_CHIPS, nrb),
            in_specs=[pl.BlockSpec((None, tr, C), lambda s, i, core_ref: (s, core_ref[0] * nrb + i, 0)), spec],
            out_specs=spec),
        compiler_params=_params())(core, g, p)


def _scatter_copies(sums, slots, send_sem, recv_sem):
    x, y, c, chips = _place()

    def make(a, j):
        px, py = chips[j]
        return pltpu.make_async_remote_copy(
            src_ref=sums[a].at[2 * px + py], dst_ref=slots[a].at[j], send_sem=send_sem.at[3 * a + j],
            recv_sem=recv_sem.at[3 * a + j], device_id=(px, py, c), device_id_type=MESH)

    return make


def _scatter_start(name, sums, after):
    n = len(sums)
    slots = [lax.empty((3,) + s.shape[1:], s.dtype) for s in sums]

    def body(*refs):
        src, dst = refs[2 * n + 1:3 * n + 1], refs[3 * n + 1:4 * n + 1]
        send_sem, recv_sem, token = refs[4 * n + 1:]
        copy = _scatter_copies(src, dst, send_sem, recv_sem)
        for a in range(n):
            for j in range(3):
                copy(a, j).start()
        token[...] = jnp.zeros(token.shape, token.dtype)

    sems = pltpu.SemaphoreType.DMA((3 * n,))
    outs = pl.pallas_call(
        body, name=name, in_specs=[HBM] * (2 * n) + [ANY],
        out_specs=[HBM] * (2 * n) + [SEM, SEM, pl.BlockSpec(memory_space=pltpu.VMEM)],
        out_shape=[pltpu.HBM(b.shape, b.dtype) for b in sums + slots] + [sems, sems, jax.ShapeDtypeStruct((8, BLOCK), F32)],
        input_output_aliases={a: a for a in range(2 * n)},
        compiler_params=pltpu.CompilerParams(has_side_effects=DATAFLOW))(*_in_hbm(sums + slots), after)
    return outs[:n], outs[n:2 * n], outs[2 * n], outs[2 * n + 1], outs[2 * n + 2]


def _scatter_finish(name, sums, slots, send_sem, recv_sem, after):
    n = len(sums)

    def body(*refs):
        send_ref, recv_ref = refs[2 * n], refs[2 * n + 1]
        src, dst = refs[2 * n + 3:3 * n + 3], refs[3 * n + 3:]
        copy = _scatter_copies(src, dst, send_ref, recv_ref)
        for a in range(n):
            for j in range(3):
                copy(a, j).wait_send()
                copy(a, j).wait_recv()

    outs = pl.pallas_call(
        body, name=name, in_specs=[HBM] * (2 * n) + [SEM, SEM, ANY], out_specs=[HBM] * (2 * n),
        out_shape=[pltpu.HBM(b.shape, b.dtype) for b in sums + slots],
        input_output_aliases={a: a for a in range(2 * n)},
        compiler_params=pltpu.CompilerParams(has_side_effects=DATAFLOW))(*sums, *slots, send_sem, recv_sem, after)
    return outs[:n], outs[n:]


def _slot_sum(name, place, slots, sums):
    _, h, C = slots.shape
    tr = _row_tile(h, 512)
    nrb = h // tr

    def body(place_ref, r0, r1, r2, own, o_ref):
        o_ref[...] = ((r0[...].astype(F32) + r1[...].astype(F32)) + r2[...].astype(F32)) + own[...].astype(F32)

    slot = lambda k: pl.BlockSpec((None, tr, C), lambda i, place_ref: (k, i, 0))
    return pl.pallas_call(
        body, name=name, out_shape=jax.ShapeDtypeStruct((2 * h, C), F32),
        grid_spec=pltpu.PrefetchScalarGridSpec(
            num_scalar_prefetch=1, grid=(nrb,),
            in_specs=[slot(0), slot(1), slot(2),
                      pl.BlockSpec((None, tr, C), lambda i, place_ref: (place_ref[0], i, 0))],
            out_specs=pl.BlockSpec((tr, C), lambda i, place_ref: (place_ref[1] * nrb + i, 0))),
        compiler_params=_params())(place, slots, slots, slots, sums)


def _half_copies(bufs, send_sem, recv_sem):
    x, y, c, _ = _place()

    def make(a, core):
        rows = bufs[a].at[_half_rows(core, bufs[a].shape[0])]
        return pltpu.make_async_remote_copy(
            src_ref=rows, dst_ref=rows, send_sem=send_sem.at[a], recv_sem=recv_sem.at[a],
            device_id=(x, y, 1 - c), device_id_type=MESH)

    return make


def _half_start(name, bufs, after):
    n = len(bufs)

    def body(*refs):
        dst = refs[n + 1:2 * n + 1]
        send_sem, recv_sem, token = refs[2 * n + 1:]
        c = lax.axis_index("c")
        copy = _half_copies(dst, send_sem, recv_sem)
        for a in range(n):
            copy(a, c).start()
        token[...] = jnp.zeros(token.shape, token.dtype)

    sems = pltpu.SemaphoreType.DMA((n,))
    outs = pl.pallas_call(
        body, name=name, in_specs=[HBM] * n + [ANY],
        out_specs=[HBM] * n + [SEM, SEM, pl.BlockSpec(memory_space=pltpu.VMEM)],
        out_shape=[pltpu.HBM(b.shape, b.dtype) for b in bufs] + [sems, sems, jax.ShapeDtypeStruct((8, BLOCK), F32)],
        input_output_aliases={a: a for a in range(n)},
        compiler_params=pltpu.CompilerParams(has_side_effects=DATAFLOW))(*_in_hbm(bufs), after)
    return outs[:n], outs[n], outs[n + 1], outs[n + 2]


def _half_finish(name, bufs, send_sem, recv_sem, after):
    n = len(bufs)

    def body(*refs):
        send_ref, recv_ref = refs[n], refs[n + 1]
        dst = refs[n + 3:]
        c = lax.axis_index("c")
        copy = _half_copies(dst, send_ref, recv_ref)
        for a in range(n):
            copy(a, c).wait_send()
            copy(a, 1 - c).wait_recv()

    return pl.pallas_call(
        body, name=name, in_specs=[HBM] * n + [SEM, SEM, ANY], out_specs=[HBM] * n,
        out_shape=[pltpu.HBM(b.shape, b.dtype) for b in bufs],
        input_output_aliases={a: a for a in range(n)},
        compiler_params=pltpu.CompilerParams(has_side_effects=DATAFLOW))(*bufs, send_sem, recv_sem, after)


class _GradReducer:
    def __init__(self, chip, core):
        self.core, self.place, self.pairs, self.started = core, jnp.concatenate([chip, core]), {}, []

    def start(self, layer, group, grads):
        mine = [grads[n] for n in REDUCE[group]]
        mine, lands, send, recv, token = _pair_start("grad_pair_start_%d_%d" % (layer, group), mine, self.place)
        self.pairs[(layer, group)] = (mine, lands, send, recv)
        return token

    def scatter(self, layer, group, after):
        tag = "%d_%d" % (layer, group)
        names = REDUCE[group]
        mine, lands, send, recv = self.pairs.pop((layer, group))
        mine, theirs = _pair_finish("grad_pair_finish_" + tag, mine, lands, send, recv, after)
        sums = [_pair_sum("pair_sum_%s_%d" % (n, layer), self.core, g, p) for n, g, p in zip(names, mine, theirs)]
        sums, slots, send, recv, token = _scatter_start("grad_scatter_start_" + tag, sums, self.place)
        self.started.append((layer, names, sums, slots, send, recv))
        return token

    def finish(self, after, update):
        for layer in sorted({entry[0] for entry in self.started}, reverse=True):
            exchanged = []
            for lyr, names, sums, slots, send, recv in self.started:
                if lyr != layer:
                    continue
                tag = "%s_%d" % (names[0], layer)
                sums, slots = _scatter_finish("grad_scatter_finish_" + tag, sums, slots, send, recv, after)
                halves = [_slot_sum("slot_sum_%s_%d" % (n, layer), self.place, r, s)
                          for n, r, s in zip(names, slots, sums)]
                halves, send, recv, after = _half_start("grad_half_start_" + tag, halves, self.place)
                exchanged.append((tag, names, halves, send, recv))
            for tag, names, halves, send, recv in exchanged:
                whole = _half_finish("grad_half_finish_" + tag, halves, send, recv, after)
                after = update(layer, dict(zip(names, whole)))


def _all_reduce_small(arrays):
    n = len(arrays)
    n_dev = 2 * N_CHIPS

    def body(*refs):
        xs, outs, gats = refs[:n], refs[n:2 * n], refs[2 * n:3 * n]
        send_sems, recv_sems, local_sems = refs[3 * n:]
        x, y, c, chips = _place()
        me, sibling = (x, y, c), (x, y, 1 - c)

        def slot(a, px, py, pc):
            return gats[a].at[4 * px + 2 * py + pc]

        def copy(a, k, block, to, src=None):
            return pltpu.make_async_remote_copy(
                src_ref=slot(a, *block) if src is None else src, dst_ref=slot(a, *block),
                send_sem=send_sems.at[7 * a + k], recv_sem=recv_sems.at[7 * a + k], device_id=to, device_id_type=MESH)

        local = [pltpu.make_async_copy(xs[a], slot(a, *me), local_sems.at[a]) for a in range(n)]
        for cp in local:
            cp.start()
        sends = []
        for a in range(n):
            sends.append(copy(a, 0, me, sibling, src=xs[a]))
            sends += [copy(a, 1 + j, me, (*chip, c), src=xs[a]) for j, chip in enumerate(chips)]
        for cp in sends:
            cp.start()
        for a in range(n):
            for j, chip in enumerate(chips):
                copy(a, 1 + j, (*chip, c), me).wait_recv()
                sends.append(copy(a, 4 + j, (*chip, c), sibling))
                sends[-1].start()
        for a in range(n):
            copy(a, 0, sibling, me).wait_recv()
            for j, chip in enumerate(chips):
                copy(a, 4 + j, (*chip, 1 - c), me).wait_recv()
        for cp in sends:
            cp.wait_send()
        for cp in local:
            cp.wait()
        for a in range(n):
            acc = gats[a][0]
            for d in range(1, n_dev):
                acc = acc + gats[a][d]
            outs[a][...] = acc

    vm = pl.BlockSpec(memory_space=pltpu.VMEM)
    return pl.pallas_call(
        body, name="small_grad_all_reduce", in_specs=[vm] * n, out_specs=[vm] * n,
        out_shape=[jax.ShapeDtypeStruct(a.shape, F32) for a in arrays],
        scratch_shapes=[pltpu.VMEM((n_dev,) + a.shape, F32) for a in arrays]
        + [pltpu.SemaphoreType.DMA((7 * n,)), pltpu.SemaphoreType.DMA((7 * n,)), pltpu.SemaphoreType.DMA((n,))],
        compiler_params=_params())(*arrays)


def _adamw_math(w, g, m, v):
    m2 = ADAM_B1 * m + (1.0 - ADAM_B1) * g
    v2 = ADAM_B2 * v + (1.0 - ADAM_B2) * (g * g)
    m_hat = m2 / (1.0 - ADAM_B1 ** ADAM_STEP)
    v_hat = v2 / (1.0 - ADAM_B2 ** ADAM_STEP)
    delta = -ADAM_LR * (m_hat / (jnp.sqrt(v_hat) + ADAM_EPS) + ADAM_WD * w)
    return delta, m2, v2


def _adamw_big(name, layer, grad, w, m, v, others):
    L, R, C = w.shape
    tr = _row_tile(R, 256)

    def body(g_ref, w_ref, m_ref, v_ref, *rest):
        go_ref, d_ref, mo_ref, vo_ref = rest[-4:]
        g = g_ref[...]
        delta, m2, v2 = _adamw_math(w_ref[...], g, m_ref[...], v_ref[...])
        go_ref[...] = g
        d_ref[...] = delta
        mo_ref[...] = m2
        vo_ref[...] = v2

    blk = pl.BlockSpec((None, tr, C), lambda i: (layer, i, 0))
    shp = jax.ShapeDtypeStruct(w.shape, F32)
    others = [] if others is None else list(others)
    return pl.pallas_call(
        body, name=name, grid=(R // tr,),
        in_specs=[pl.BlockSpec((tr, C), lambda i: (i, 0))] + [blk] * 3 + [ANY] * len(others), out_specs=[blk] * 4,
        out_shape=[shp] * 4, input_output_aliases={4 + k: k for k in range(len(others))},
        compiler_params=_params())(grad, w, m, v, *others)


def _adamw_small(gs, ws, ms, vs):
    n = len(gs)

    def body(*refs):
        for a in range(n):
            g_ref, w_ref, m_ref, v_ref = refs[a], refs[n + a], refs[2 * n + a], refs[3 * n + a]
            delta, m2, v2 = _adamw_math(w_ref[...], g_ref[...], m_ref[...], v_ref[...])
            refs[4 * n + a][...] = delta
            refs[5 * n + a][...] = m2
            refs[6 * n + a][...] = v2

    vm = pl.BlockSpec(memory_space=pltpu.VMEM)
    shapes = [jax.ShapeDtypeStruct(g.shape, F32) for g in gs]
    outs = pl.pallas_call(
        body, name="adamw_small", in_specs=[vm] * (4 * n), out_specs=[vm] * (3 * n), out_shape=shapes * 3,
        compiler_params=_params())(*gs, *ws, *ms, *vs)
    return outs[:n], outs[n:2 * n], outs[2 * n:]


def _rows2d(a):
    return a if a.ndim == 2 else a.reshape(-1, a.shape[-1])


BIG = ("w_in", "w_attn_branch", "w_sgu_branch", "w_out", "w_gate", "w_up", "w_down")
SMALL = ("mix_norm", "q_norm", "k_norm", "sinks", "sgu_ln_g", "sgu_ln_b", "w_spatial", "b_spatial", "ffn_norm")
ORDER = ("mix_norm", "w_in", "q_norm", "k_norm", "sinks", "sgu_ln_g", "sgu_ln_b", "w_spatial", "b_spatial",
         "w_attn_branch", "w_sgu_branch", "w_out", "ffn_norm", "w_gate", "w_up", "w_down")


def _rope_tables(seq):
    pos = jnp.arange(seq, dtype=F32)
    inv_freq = jnp.power(10000.0, -jnp.arange(0, HEAD_DIM, 2, dtype=F32) / HEAD_DIM)
    ang = pos[:, None] * inv_freq[None, :]
    cos, sin = jnp.cos(ang), jnp.sin(ang)
    reps = BLOCK // HEAD_DIM
    return (jnp.tile(jnp.concatenate([cos, cos], axis=1), (1, reps)),
            jnp.tile(jnp.concatenate([-sin, sin], axis=1), (1, reps)))


def kernel(x, mix_norm, w_in, q_norm, k_norm, sinks, sgu_ln_g, sgu_ln_b, w_spatial, b_spatial, w_attn_branch, w_sgu_branch, w_out, ffn_norm, w_gate, w_up, w_down, loss_target, m_mix_norm, m_w_in, m_q_norm, m_k_norm, m_sinks, m_sgu_ln_g, m_sgu_ln_b, m_w_spatial, m_b_spatial, m_w_attn_branch, m_w_sgu_branch, m_w_out, m_ffn_norm, m_w_gate, m_w_up, m_w_down, v_mix_norm, v_w_in, v_q_norm, v_k_norm, v_sinks, v_sgu_ln_g, v_sgu_ln_b, v_w_spatial, v_b_spatial, v_w_attn_branch, v_w_sgu_branch, v_w_out, v_ffn_norm, v_w_gate, v_w_up, v_w_down):
    weights = dict(mix_norm=mix_norm, w_in=w_in, q_norm=q_norm, k_norm=k_norm, sinks=sinks, sgu_ln_g=sgu_ln_g,
                   sgu_ln_b=sgu_ln_b, w_spatial=w_spatial, b_spatial=b_spatial, w_attn_branch=w_attn_branch,
                   w_sgu_branch=w_sgu_branch, w_out=w_out, ffn_norm=ffn_norm, w_gate=w_gate, w_up=w_up, w_down=w_down)
    mom1 = dict(mix_norm=m_mix_norm, w_in=m_w_in, q_norm=m_q_norm, k_norm=m_k_norm, sinks=m_sinks,
                sgu_ln_g=m_sgu_ln_g, sgu_ln_b=m_sgu_ln_b, w_spatial=m_w_spatial, b_spatial=m_b_spatial,
                w_attn_branch=m_w_attn_branch, w_sgu_branch=m_w_sgu_branch, w_out=m_w_out, ffn_norm=m_ffn_norm,
                w_gate=m_w_gate, w_up=m_w_up, w_down=m_w_down)
    mom2 = dict(mix_norm=v_mix_norm, w_in=v_w_in, q_norm=v_q_norm, k_norm=v_k_norm, sinks=v_sinks,
                sgu_ln_g=v_sgu_ln_g, sgu_ln_b=v_sgu_ln_b, w_spatial=v_w_spatial, b_spatial=v_b_spatial,
                w_attn_branch=v_w_attn_branch, w_sgu_branch=v_w_sgu_branch, w_out=v_w_out, ffn_norm=v_ffn_norm,
                w_gate=v_w_gate, w_up=v_w_up, w_down=v_w_down)
    xs, target = x[0], loss_target[0]
    S, D = xs.shape
    L = w_in.shape[0]
    AW, KW, SW = N_Q_HEADS * HEAD_DIM, N_KV_HEADS * HEAD_DIM, SGU_GROUPS * BLOCK
    dims = (AW, KW, AW + 2 * KW + 2 * SW, AW + 2 * KW)
    cos, sin = _rope_tables(S)
    reps = BLOCK // HEAD_DIM

    chip = (2 * lax.axis_index("x") + lax.axis_index("y")).astype(jnp.int32).reshape(1)
    core = lax.axis_index("c").astype(jnp.int32).reshape(1)
    started, token = {}, chip
    for l in range(L):
        for gi, names in enumerate(GATHER):
            bufs = [_cast_own("cast_%s_%d" % (n, l), chip, weights[n], l) for n in names]
            bufs, send, recv, token = _gather_start("gather_start_%d_%d" % (l, gi), bufs, token)
            started[(l, gi)] = (bufs, send, recv)
    stream = _WeightStream(started)
    stream.forward(0, 0, token)
    sp = [dict(mix_norm=mix_norm[l][None], ffn_norm=ffn_norm[l][None], q_norm=jnp.tile(q_norm[l][None], (1, reps)),
               k_norm=jnp.tile(k_norm[l][None], (1, reps)), sinks=sinks[l][None], sgu_ln_g=sgu_ln_g[l][None],
               sgu_ln_b=sgu_ln_b[l][None], w_spatial=w_spatial[l], b_spatial=b_spatial[l][:, :, None])
          for l in range(L)]

    act, saved, wl = xs, [], []
    for l in range(L):
        act, sv, w_all = _layer_fwd(act, stream, l, l == L - 1, sp[l], cos, sin, dims)
        saved.append(sv)
        wl.append(w_all)
    loss_part, dy, dyb = _loss_head(act, target)
    loss = lax.psum(loss_part[0, 0], ("x", "y", "c"))

    reducer = _GradReducer(chip, core)
    small_g = [None] * L
    for l in reversed(range(L)):
        dy, dyb, small_g[l] = _layer_bwd(dy, dyb, wl[l], sp[l], saved[l], cos, sin, dims, reducer, l)
    grad_x = dy[None]

    updated = {}

    def update(layer, reduced):
        for n, g in reduced.items():
            updated[n] = _adamw_big("adamw_%s_%d" % (n, layer), layer, g, weights[n], mom1[n], mom2[n],
                                    updated.get(n))
        return updated[n][0]

    reducer.finish(dy, update)
    grads, deltas, new_m, new_v = {}, {}, {}, {}
    for n in BIG:
        grads[n], deltas[n], new_m[n], new_v[n] = updated[n]

    local = [_rows2d(jnp.stack([small_g[l][n].reshape(weights[n].shape[1:]) for l in range(L)])) for n in SMALL]
    g_small = _all_reduce_small(local)
    d_small, m_small, v_small = _adamw_small(g_small, [_rows2d(weights[n]) for n in SMALL],
                                             [_rows2d(mom1[n]) for n in SMALL], [_rows2d(mom2[n]) for n in SMALL])
    for n, g, d, m2, v2 in zip(SMALL, g_small, d_small, m_small, v_small):
        shape = weights[n].shape
        grads[n], deltas[n], new_m[n], new_v[n] = g.reshape(shape), d.reshape(shape), m2.reshape(shape), v2.reshape(shape)

    return (loss, grad_x, *[grads[n] for n in ORDER], *[deltas[n] for n in ORDER],
            *[new_m[n] for n in ORDER], *[new_v[n] for n in ORDER])
```

```python
import functools

import jax
import jax.numpy as jnp
from jax import lax
from jax.experimental import pallas as pl
from jax.experimental.pallas import tpu as pltpu

HEAD_DIM = 64
N_Q_HEADS = 16
N_KV_HEADS = 4
SGU_GROUPS = 8
BLOCK = 128
EPS = 1e-6
ADAM_LR = 0.001
ADAM_B1 = 0.9
ADAM_B2 = 0.999
ADAM_EPS = 1e-08
ADAM_WD = 0.01
ADAM_STEP = 10
N_CHIPS = 4
VMEM_LIMIT = 52 * 1024 * 1024
MXU_CHUNK = 256
ATTN_STACK = 4

F32 = jnp.float32
MXU = jnp.bfloat16
NN = (((1,), (0,)), ((), ()))
NT = (((1,), (1,)), ((), ()))
TN = (((0,), (0,)), ((), ()))
MESH = pl.DeviceIdType.MESH
ANY = pl.BlockSpec(memory_space=pl.ANY)


def _tile(n, pref):
    if n <= pref:
        return n
    best = None
    for t in range(BLOCK, pref + 1, BLOCK):
        if n % t == 0:
            best = t
    assert best is not None, (n, pref)
    return best


def _params():
    return pltpu.CompilerParams(vmem_limit_bytes=VMEM_LIMIT)


def _mm(name, grid, n_red, operands, specs, pairs, dims, n_extra, out_shapes, out_specs,
        acc_shapes, epilogue, after=None, chunk=None):
    n_op = len(operands) - n_extra
    n_out = len(out_shapes)
    n_acc = len(acc_shapes)
    if after is not None:
        operands, specs = list(operands) + [after], list(specs) + [ANY]
    n_in = len(operands)
    axes = [ax for ax in range(len(grid) - n_red, len(grid)) if grid[ax] > 1]

    def body(*refs):
        ops = refs[:n_op]
        extra = refs[n_op:n_op + n_extra]
        outs = refs[n_in:n_in + n_out]
        accs = refs[n_in + n_out:]

        def prod(a, b, cols=None):
            rhs = ops[b]
            if cols is not None:
                rhs = rhs.at[:, cols] if dims == NN else rhs.at[cols, :]
            return lax.dot_general(ops[a][...], rhs[...], dims, preferred_element_type=F32)

        def products(cols=None):
            vals = [None] * n_acc
            for a, b, k in pairs:
                d = prod(a, b, cols)
                vals[k] = d if vals[k] is None else vals[k] + d
            return vals

        if not axes and chunk is not None:
            width = outs[0].shape[-1]
            for c0 in range(0, width, chunk):
                cols = pl.ds(c0, min(chunk, width - c0))
                epilogue(products(cols), [e.at[:, cols] for e in extra], [o.at[:, cols] for o in outs])
        elif not axes:
            epilogue(products(), extra, outs)
        else:
            first = pl.program_id(axes[0]) == 0
            last = pl.program_id(axes[0]) == grid[axes[0]] - 1
            for ax in axes[1:]:
                first = jnp.logical_and(first, pl.program_id(ax) == 0)
                last = jnp.logical_and(last, pl.program_id(ax) == grid[ax] - 1)

            @pl.when(first)
            def _():
                for acc in accs:
                    acc[...] = jnp.zeros(acc.shape, F32)

            for a, b, k in pairs:
                accs[k][...] += prod(a, b)

            @pl.when(last)
            def _():
                epilogue([acc[...] for acc in accs], extra, outs)

    scratch = [pltpu.VMEM(s, F32) for s in acc_shapes] if axes else []
    return pl.pallas_call(
        body, name=name, grid=grid, in_specs=specs, out_specs=out_specs, out_shape=out_shapes,
        scratch_shapes=scratch, compiler_params=_params())(*operands)


def _sigmoid(x):
    return 1.0 / (1.0 + jnp.exp(-x))


_GELU_C = 0.7978845608028654
_GELU_A = 0.044715


def _gelu(x):
    return 0.5 * x * (1.0 + jnp.tanh(_GELU_C * (x + _GELU_A * x * x * x)))


def _gelu_grad(x):
    t = jnp.tanh(_GELU_C * (x + _GELU_A * x * x * x))
    return 0.5 * (1.0 + t) + 0.5 * x * (1.0 - t * t) * _GELU_C * (1.0 + 3.0 * _GELU_A * x * x)


def _rms_fwd(name, x, g):
    S, D = x.shape
    tr = _tile(S, 256)

    def body(x_ref, g_ref, o_ref):
        xv = x_ref[...]
        r = lax.rsqrt(jnp.mean(xv * xv, axis=-1, keepdims=True) + EPS)
        o_ref[...] = (xv * r * g_ref[...]).astype(MXU)

    return pl.pallas_call(
        body, name=name, grid=(S // tr,),
        in_specs=[pl.BlockSpec((tr, D), lambda i: (i, 0)), pl.BlockSpec((1, D), lambda i: (0, 0))],
        out_specs=pl.BlockSpec((tr, D), lambda i: (i, 0)),
        out_shape=jax.ShapeDtypeStruct((S, D), MXU), compiler_params=_params())(x, g)


def _rms_bwd(name, dh, x, g, dres, after):
    S, D = x.shape
    tr = _tile(S, 256)

    def body(dh_ref, x_ref, g_ref, dres_ref, after_ref, dx_ref, dxb_ref, dg_ref):
        xv = x_ref[...]
        r = lax.rsqrt(jnp.mean(xv * xv, axis=-1, keepdims=True) + EPS)
        xh = xv * r
        dhv = dh_ref[...]
        dy = dhv * g_ref[...]
        dx = dres_ref[...] + r * (dy - xh * jnp.mean(dy * xh, axis=-1, keepdims=True))
        dx_ref[...] = dx
        dxb_ref[...] = dx.astype(MXU)

        @pl.when(pl.program_id(0) == 0)
        def _():
            dg_ref[...] = jnp.zeros(dg_ref.shape, F32)

        dg_ref[...] += jnp.sum(dhv * xh, axis=0, keepdims=True)

    row = pl.BlockSpec((tr, D), lambda i: (i, 0))
    vec = pl.BlockSpec((1, D), lambda i: (0, 0))
    return pl.pallas_call(
        body, name=name, grid=(S // tr,), in_specs=[row, row, vec, row, ANY], out_specs=[row, row, vec],
        out_shape=[jax.ShapeDtypeStruct((S, D), F32), jax.ShapeDtypeStruct((S, D), MXU),
                   jax.ShapeDtypeStruct((1, D), F32)],
        compiler_params=_params())(dh, x, g, dres, after)


def _loss_head(y, target):
    S, D = y.shape
    tr = _tile(S, 256)

    def body(y_ref, t_ref, loss_ref, dy_ref, dyb_ref):
        d = y_ref[...] - t_ref[...]
        dy = d * (1.0 / D)
        dy_ref[...] = dy
        dyb_ref[...] = dy.astype(MXU)

        @pl.when(pl.program_id(0) == 0)
        def _():
            loss_ref[...] = jnp.zeros(loss_ref.shape, F32)

        loss_ref[...] += (0.5 / D) * jnp.sum(jnp.sum(d * d, axis=-1, keepdims=True), axis=0, keepdims=True)

    row = pl.BlockSpec((tr, D), lambda i: (i, 0))
    return pl.pallas_call(
        body, name="loss_head", grid=(S // tr,), in_specs=[row, row],
        out_specs=[pl.BlockSpec((1, 1), lambda i: (0, 0)), row, row],
        out_shape=[jax.ShapeDtypeStruct((1, 1), F32), jax.ShapeDtypeStruct((S, D), F32),
                   jax.ShapeDtypeStruct((S, D), MXU)],
        compiler_params=_params())(y, target)


def _head_sum(v):
    r = lax.broadcasted_iota(jnp.int32, (BLOCK, BLOCK), 0) // HEAD_DIM
    c = lax.broadcasted_iota(jnp.int32, (BLOCK, BLOCK), 1) // HEAD_DIM
    ones = jnp.where(r == c, 1.0, 0.0).astype(jnp.bfloat16)
    hi = v.astype(jnp.bfloat16)
    lo = (v - hi.astype(F32)).astype(jnp.bfloat16)
    parts = []
    for t in range(v.shape[1] // BLOCK):
        sl = slice(t * BLOCK, (t + 1) * BLOCK)
        parts.append(jnp.dot(hi[:, sl], ones, preferred_element_type=F32)
                     + jnp.dot(lo[:, sl], ones, preferred_element_type=F32))
    return parts[0] if len(parts) == 1 else jnp.concatenate(parts, axis=-1)


def _swap_halves(v):
    w = v.shape[1]
    half = HEAD_DIM // 2
    lane = lax.broadcasted_iota(jnp.int32, v.shape, 1) % HEAD_DIM
    return jnp.where(lane < half, pltpu.roll(v, w - half, 1), pltpu.roll(v, half, 1))


def _norm_rope(xv, gain, cos, sin):
    r = lax.rsqrt(_head_sum(xv * xv) * (1.0 / HEAD_DIM) + EPS)
    xn = xv * r * gain
    return xn * cos + _swap_halves(xn) * sin


def _norm_rope_bwd(dy, xv, gain, cos, sin):
    r = lax.rsqrt(_head_sum(xv * xv) * (1.0 / HEAD_DIM) + EPS)
    xh = xv * r
    dxn = dy * cos + _swap_halves(dy * sin)
    dgain = jnp.sum(dxn * xh, axis=0, keepdims=True)
    dxh = dxn * gain
    dx = r * (dxh - xh * (_head_sum(dxh * xh) * (1.0 / HEAD_DIM)))
    return dx, dgain


def _fold_heads(v):
    acc = v[:, 0:BLOCK]
    for t in range(1, v.shape[1] // BLOCK):
        acc = acc + v[:, t * BLOCK:(t + 1) * BLOCK]
    return acc + pltpu.roll(acc, HEAD_DIM, 1)


def _tile_lanes(v, width):
    return v if width == BLOCK else jnp.tile(v, (1, width // BLOCK))


def _low_half(rows):
    assert BLOCK == 2 * HEAD_DIM
    return lax.broadcasted_iota(jnp.int32, (rows, BLOCK), 1) < HEAD_DIM


def _spread_heads(v):
    low = _low_half(v.shape[0])
    out = []
    for t in range(v.shape[1] // BLOCK):
        tile = v[:, t * BLOCK:(t + 1) * BLOCK]
        swapped = pltpu.roll(tile, HEAD_DIM, 1)
        out += [jnp.where(low, tile, swapped), jnp.where(low, swapped, tile)]
    return jnp.concatenate(out, axis=-1)


def _gather_heads(v):
    low = _low_half(v.shape[0])
    out = []
    for t in range(v.shape[1] // (2 * BLOCK)):
        a, b = v[:, 2 * t * BLOCK:(2 * t + 1) * BLOCK], v[:, (2 * t + 1) * BLOCK:(2 * t + 2) * BLOCK]
        out.append(jnp.where(low, a + pltpu.roll(a, HEAD_DIM, 1), b + pltpu.roll(b, HEAD_DIM, 1)))
    return out[0] if len(out) == 1 else jnp.concatenate(out, axis=-1)


def _qk_prep(proj, qg, kg, cos, sin, AW, KW):
    S = proj.shape[0]
    tr = _tile(S, 256)
    scale = HEAD_DIM ** -0.5

    def body(q_ref, k_ref, v_ref, qg_ref, kg_ref, cos_ref, sin_ref, qo_ref, ko_ref, vo_ref):
        c, s = cos_ref[...], sin_ref[...]
        q = _norm_rope(q_ref[...], _tile_lanes(qg_ref[...], AW), _tile_lanes(c, AW), _tile_lanes(s, AW))
        k = _norm_rope(k_ref[...], _tile_lanes(kg_ref[...], KW), _tile_lanes(c, KW), _tile_lanes(s, KW))
        qo_ref[...] = (q * scale).astype(MXU)
        ko_ref[...] = _spread_heads(k).astype(MXU)
        vo_ref[...] = _spread_heads(v_ref[...]).astype(MXU)

    assert AW % KW == 0
    vec = pl.BlockSpec((1, BLOCK), lambda i: (0, 0))
    tab = pl.BlockSpec((tr, BLOCK), lambda i: (i, 0))
    wide = pl.BlockSpec((tr, 2 * KW), lambda i: (i, 0))
    return pl.pallas_call(
        body, name="qk_prep", grid=(S // tr,),
        in_specs=[pl.BlockSpec((tr, AW), lambda i: (i, 0)),
                  pl.BlockSpec((tr, KW), lambda i: (i, AW // KW)),
                  pl.BlockSpec((tr, KW), lambda i: (i, AW // KW + 1)), vec, vec, tab, tab],
        out_specs=[pl.BlockSpec((tr, AW), lambda i: (i, 0)), wide, wide],
        out_shape=[jax.ShapeDtypeStruct((S, AW), MXU), jax.ShapeDtypeStruct((S, 2 * KW), MXU),
                   jax.ShapeDtypeStruct((S, 2 * KW), MXU)],
        compiler_params=_params())(proj, proj, proj, qg, kg, cos, sin)


def _stack_heads(x, h0, nh):
    low = _low_half(BLOCK)
    parts = []
    for h in range(h0, h0 + nh):
        tile = x[:, (h // 2) * BLOCK:(h // 2 + 1) * BLOCK]
        parts.append(jnp.where(low if h % 2 == 0 else jnp.logical_not(low), tile, jnp.zeros_like(tile)))
    return jnp.concatenate(parts, axis=0)


def _unstack_heads(y):
    low = _low_half(BLOCK)
    tiles = [jnp.where(low, y[2 * t * BLOCK:(2 * t + 1) * BLOCK], y[(2 * t + 1) * BLOCK:(2 * t + 2) * BLOCK])
             for t in range(y.shape[0] // (2 * BLOCK))]
    return tiles[0] if len(tiles) == 1 else jnp.concatenate(tiles, axis=-1)


def _band_t(n):
    key = lax.broadcasted_iota(jnp.int32, (2 * BLOCK, BLOCK), 0)
    qry = lax.broadcasted_iota(jnp.int32, (2 * BLOCK, BLOCK), 1)
    return (key > qry) & (key <= qry + BLOCK) & ((key >= BLOCK) | (n > 0))


def _attn_probs_t(ok, qs, kcat, h0, nh, sink_ref):
    st = jnp.where(ok, lax.dot_general(kcat, qs, NT, preferred_element_type=F32), -1e30)
    sk = jnp.concatenate([jnp.full((1, BLOCK), sink_ref[0, h], F32) for h in range(h0, h0 + nh)], axis=1)
    m = jnp.maximum(jnp.max(st, axis=0, keepdims=True), sk)
    e = jnp.exp(st - m)
    es = jnp.exp(sk - m)
    rz = 1.0 / (jnp.sum(e, axis=0, keepdims=True) + es)
    return e * rz, es * rz, rz


def _attn_fwd(qr, kr, vb, sinks):
    S, AW = qr.shape
    KW = kr.shape[1]
    nb = S // BLOCK
    nkv = KW // BLOCK
    qpk = AW // (nkv * HEAD_DIM)
    nh = min(ATTN_STACK, qpk)
    assert nh % 2 == 0 and qpk % nh == 0

    def body(sink_ref, q_ref, kp_ref, kc_ref, vp_ref, vc_ref, o_ref):
        n = pl.program_id(0)
        q, kp, kc, vp, vc = q_ref[...], kp_ref[...], kc_ref[...], vp_ref[...], vc_ref[...]
        ok = jnp.concatenate([_band_t(n)] * nh, axis=1)
        outs = []
        for g in range(nkv):
            kcat = jnp.concatenate([kp[:, g * BLOCK:(g + 1) * BLOCK], kc[:, g * BLOCK:(g + 1) * BLOCK]], axis=0)
            vcat = jnp.concatenate([vp[:, g * BLOCK:(g + 1) * BLOCK], vc[:, g * BLOCK:(g + 1) * BLOCK]], axis=0)
            for h0 in range(g * qpk, (g + 1) * qpk, nh):
                pt, _, _ = _attn_probs_t(ok, _stack_heads(q, h0, nh), kcat, h0, nh, sink_ref)
                outs.append(_unstack_heads(lax.dot_general(pt.astype(MXU), vcat, TN, preferred_element_type=F32)))
        o_ref[...] = jnp.concatenate(outs, axis=-1).astype(MXU)

    cur = lambda n: (n, 0)
    prev = lambda n: (jnp.maximum(n - 1, 0), 0)
    return pl.pallas_call(
        body, name="attn_fwd", grid=(nb,),
        in_specs=[pl.BlockSpec(memory_space=pltpu.SMEM), pl.BlockSpec((BLOCK, AW), cur),
                  pl.BlockSpec((BLOCK, KW), prev), pl.BlockSpec((BLOCK, KW), cur),
                  pl.BlockSpec((BLOCK, KW), prev), pl.BlockSpec((BLOCK, KW), cur)],
        out_specs=pl.BlockSpec((BLOCK, AW), cur),
        out_shape=jax.ShapeDtypeStruct((S, AW), MXU), compiler_params=_params())(sinks, qr, kr, kr, vb, vb)


def _attn_bwd(qr, kr, vb, sinks, dattn):
    S, AW = qr.shape
    KW = kr.shape[1]
    nb = S // BLOCK
    nkv = KW // BLOCK
    qpk = AW // (nkv * HEAD_DIM)
    nh = min(ATTN_STACK, qpk)
    scale = HEAD_DIM ** -0.5

    def body(sink_ref, q_ref, kp_ref, kc_ref, vp_ref, vc_ref, do_ref,
             dq_ref, dkp_ref, dkc_ref, dvp_ref, dvc_ref, dsink_ref):
        n = pl.program_id(0)
        q, kp, kc, vp, vc = q_ref[...], kp_ref[...], kc_ref[...], vp_ref[...], vc_ref[...]
        do = do_ref[...].astype(MXU)
        lane = lax.broadcasted_iota(jnp.int32, (1, BLOCK), 1)
        ok = jnp.concatenate([_band_t(n)] * nh, axis=1)
        dsink = jnp.zeros((1, BLOCK), F32)
        dqs, dkps, dkcs, dvps, dvcs = [], [], [], [], []
        for g in range(nkv):
            kcat = jnp.concatenate([kp[:, g * BLOCK:(g + 1) * BLOCK], kc[:, g * BLOCK:(g + 1) * BLOCK]], axis=0)
            vcat = jnp.concatenate([vp[:, g * BLOCK:(g + 1) * BLOCK], vc[:, g * BLOCK:(g + 1) * BLOCK]], axis=0)
            dk, dv = None, None
            for h0 in range(g * qpk, (g + 1) * qpk, nh):
                qs = _stack_heads(q, h0, nh)
                dos = _stack_heads(do, h0, nh)
                pt, ps, _ = _attn_probs_t(ok, qs, kcat, h0, nh, sink_ref)
                dpt = lax.dot_general(vcat, dos, NT, preferred_element_type=F32)
                delta = jnp.sum(pt * dpt, axis=0, keepdims=True)
                dst = (pt * (dpt - delta)).astype(MXU)
                dsk = -ps * delta
                dv_part = jnp.dot(pt.astype(MXU), dos, preferred_element_type=F32)
                dk_part = jnp.dot(dst, qs, preferred_element_type=F32)
                dqs.append(_unstack_heads(lax.dot_general(dst, kcat, TN, preferred_element_type=F32) * scale))
                dk = dk_part if dk is None else dk + dk_part
                dv = dv_part if dv is None else dv + dv_part
                for j in range(nh):
                    tot = jnp.sum(dsk[:, j * BLOCK:(j + 1) * BLOCK], axis=1, keepdims=True)
                    dsink = dsink + jnp.where(lane == h0 + j, tot, 0.0)
            dkps.append(dk[:BLOCK])
            dkcs.append(dk[BLOCK:])
            dvps.append(dv[:BLOCK])
            dvcs.append(dv[BLOCK:])
        dq_ref[...] = jnp.concatenate(dqs, axis=-1)
        dkp_ref[...] = jnp.concatenate(dkps, axis=-1)
        dkc_ref[...] = jnp.concatenate(dkcs, axis=-1)
        dvp_ref[...] = jnp.concatenate(dvps, axis=-1)
        dvc_ref[...] = jnp.concatenate(dvcs, axis=-1)

        @pl.when(n == 0)
        def _():
            dsink_ref[...] = jnp.zeros(dsink_ref.shape, F32)

        dsink_ref[...] += dsink

    cur = lambda n: (n, 0)
    prev = lambda n: (jnp.maximum(n - 1, 0), 0)
    kv = jax.ShapeDtypeStruct((S, KW), F32)
    kvspec = pl.BlockSpec((BLOCK, KW), cur)
    return pl.pallas_call(
        body, name="attn_bwd", grid=(nb,),
        in_specs=[pl.BlockSpec(memory_space=pltpu.SMEM), pl.BlockSpec((BLOCK, AW), cur),
                  pl.BlockSpec((BLOCK, KW), prev), kvspec, pl.BlockSpec((BLOCK, KW), prev), kvspec,
                  pl.BlockSpec((BLOCK, AW), cur)],
        out_specs=[pl.BlockSpec((BLOCK, AW), cur), kvspec, kvspec, kvspec, kvspec,
                   pl.BlockSpec((1, BLOCK), lambda n: (0, 0))],
        out_shape=[jax.ShapeDtypeStruct((S, AW), F32), kv, kv, kv, kv, jax.ShapeDtypeStruct((1, BLOCK), F32)],
        compiler_params=_params())(sinks, qr, kr, kr, vb, vb, dattn)


def _qk_prep_bwd(proj, qg, kg, cos, sin, dq, dkp, dkc, dvp, dvc, AW, KW):
    S = proj.shape[0]
    nb = S // BLOCK

    def body(q_ref, k_ref, qg_ref, kg_ref, cos_ref, sin_ref, dq_ref, dkp_ref, dkc_ref, dvp_ref, dvc_ref,
             o_ref, dqg_ref, dkg_ref):
        n = pl.program_id(0)
        c, s = cos_ref[...], sin_ref[...]
        has_next = jnp.where(n < nb - 1, 1.0, 0.0)
        dk = _gather_heads(dkc_ref[...] + has_next * dkp_ref[...])
        dv = _gather_heads(dvc_ref[...] + has_next * dvp_ref[...])
        dxq, dqg = _norm_rope_bwd(dq_ref[...], q_ref[...], _tile_lanes(qg_ref[...], AW),
                                  _tile_lanes(c, AW), _tile_lanes(s, AW))
        dxk, dkg = _norm_rope_bwd(dk, k_ref[...], _tile_lanes(kg_ref[...], KW),
                                  _tile_lanes(c, KW), _tile_lanes(s, KW))
        o_ref[...] = jnp.concatenate([dxq, dxk, dv], axis=-1).astype(MXU)

        @pl.when(n == 0)
        def _():
            dqg_ref[...] = jnp.zeros(dqg_ref.shape, F32)
            dkg_ref[...] = jnp.zeros(dkg_ref.shape, F32)

        dqg_ref[...] += _fold_heads(dqg)
        dkg_ref[...] += _fold_heads(dkg)

    cur = lambda n: (n, 0)
    nxt = lambda n: (jnp.minimum(n + 1, nb - 1), 0)
    vec = pl.BlockSpec((1, BLOCK), lambda n: (0, 0))
    tab = pl.BlockSpec((BLOCK, BLOCK), cur)
    return pl.pallas_call(
        body, name="qk_prep_bwd", grid=(nb,),
        in_specs=[pl.BlockSpec((BLOCK, AW), cur), pl.BlockSpec((BLOCK, KW), lambda n: (n, AW // KW)),
                  vec, vec, tab, tab, pl.BlockSpec((BLOCK, AW), cur),
                  pl.BlockSpec((BLOCK, 2 * KW), nxt), pl.BlockSpec((BLOCK, 2 * KW), cur),
                  pl.BlockSpec((BLOCK, 2 * KW), nxt), pl.BlockSpec((BLOCK, 2 * KW), cur)],
        out_specs=[pl.BlockSpec((BLOCK, AW + 2 * KW), cur), vec, vec],
        out_shape=[jax.ShapeDtypeStruct((S, AW + 2 * KW), MXU), jax.ShapeDtypeStruct((1, BLOCK), F32),
                   jax.ShapeDtypeStruct((1, BLOCK), F32)],
        compiler_params=_params())(proj, proj, qg, kg, cos, sin, dq, dkp, dkc, dvp, dvc)


SGU_LANES = 512
SGU_ROWS = 256


def _sgu_group(v, lng, lnb, w_f32, b):
    rows = v.shape[0]
    mu = jnp.mean(v, axis=-1, keepdims=True)
    vc = v - mu
    r = lax.rsqrt(jnp.mean(vc * vc, axis=-1, keepdims=True) + EPS)
    xh = vc * r
    vn = (xh * lng + lnb).astype(MXU)
    row = lax.broadcasted_iota(jnp.int32, (BLOCK, BLOCK), 0)
    col = lax.broadcasted_iota(jnp.int32, (BLOCK, BLOCK), 1)
    tri = row >= col
    w = jnp.where(tri, w_f32, 0.0).astype(MXU)
    chunks = [jnp.dot(w, vn[k * BLOCK:(k + 1) * BLOCK], preferred_element_type=F32) + b for k in range(rows // BLOCK)]
    s = chunks[0] if len(chunks) == 1 else jnp.concatenate(chunks, axis=0)
    return xh, r, vn, w, s, tri


def _sgu_layout(S, u_col):
    SW = SGU_GROUPS * BLOCK
    lb, tr = min(SGU_LANES, SW), min(SGU_ROWS, S)
    assert u_col % lb == 0 and SW % lb == 0 and S % tr == 0
    ub, nlb, gpb = u_col // lb, SW // lb, lb // BLOCK
    specs = [pl.BlockSpec((tr, lb), lambda j, i: (i, ub + j)), pl.BlockSpec((tr, lb), lambda j, i: (i, ub + nlb + j)),
             pl.BlockSpec((1, lb), lambda j, i: (0, j)), pl.BlockSpec((1, lb), lambda j, i: (0, j)),
             pl.BlockSpec((gpb, BLOCK, BLOCK), lambda j, i: (j, 0, 0)),
             pl.BlockSpec((gpb, BLOCK, 1), lambda j, i: (j, 0, 0))]
    return lb, tr, gpb, nlb, specs


def _sgu_fwd(proj, lng, lnb, ws, bs, u_col):
    S = proj.shape[0]
    lb, tr, gpb, nlb, specs = _sgu_layout(S, u_col)

    def body(pu_ref, pv_ref, lng_ref, lnb_ref, w_ref, b_ref, o_ref):
        u = _gelu(pu_ref[...])
        v = _gelu(pv_ref[...])
        outs = []
        for g in range(gpb):
            sl = slice(g * BLOCK, (g + 1) * BLOCK)
            s = _sgu_group(v[:, sl], lng_ref[:, sl], lnb_ref[:, sl], w_ref[g], b_ref[g])[4]
            outs.append(u[:, sl] * s)
        o_ref[...] = (outs[0] if gpb == 1 else jnp.concatenate(outs, axis=-1)).astype(MXU)

    return pl.pallas_call(
        body, name="sgu_fwd", grid=(nlb, S // tr), in_specs=specs,
        out_specs=pl.BlockSpec((tr, lb), lambda j, i: (i, j)),
        out_shape=jax.ShapeDtypeStruct((S, nlb * lb), MXU), compiler_params=_params())(proj, proj, lng, lnb, ws, bs)


def _sgu_bwd(proj, lng, lnb, ws, bs, dsgu, u_col, after):
    S = proj.shape[0]
    G = SGU_GROUPS
    lb, tr, gpb, nlb, specs = _sgu_layout(S, u_col)
    nch = tr // BLOCK

    def body(pu_ref, pv_ref, lng_ref, lnb_ref, w_ref, b_ref, do_ref, after_ref,
             dpu_ref, dpv_ref, dw_ref, db_ref, dlng_ref, dlnb_ref):
        pu, pv, do = pu_ref[...], pv_ref[...], do_ref[...]
        u = _gelu(pu)
        v = _gelu(pv)

        @pl.when(pl.program_id(1) == 0)
        def _():
            dw_ref[...] = jnp.zeros(dw_ref.shape, F32)
            db_ref[...] = jnp.zeros(db_ref.shape, F32)
            dlng_ref[...] = jnp.zeros(dlng_ref.shape, F32)
            dlnb_ref[...] = jnp.zeros(dlnb_ref.shape, F32)

        ss, dvs, dlng, dlnb = [], [], [], []
        for g in range(gpb):
            sl = slice(g * BLOCK, (g + 1) * BLOCK)
            xh, r, vn, w, s, tri = _sgu_group(v[:, sl], lng_ref[:, sl], lnb_ref[:, sl], w_ref[g], b_ref[g])
            ds = do[:, sl] * u[:, sl]
            dsb = ds.astype(MXU)
            dw, db, dvn = None, None, []
            for k in range(nch):
                rows = slice(k * BLOCK, (k + 1) * BLOCK)
                part = lax.dot_general(dsb[rows], vn[rows], NT, preferred_element_type=F32)
                dw = part if dw is None else dw + part
                rowsum = jnp.sum(ds[rows], axis=-1, keepdims=True)
                db = rowsum if db is None else db + rowsum
                dvn.append(lax.dot_general(w, dsb[rows], TN, preferred_element_type=F32))
            dvn = dvn[0] if nch == 1 else jnp.concatenate(dvn, axis=0)
            dw_ref[g] += jnp.where(tri, dw, 0.0)
            db_ref[g] += db
            dxh = dvn * lng_ref[:, sl]
            dvs.append(r * (dxh - jnp.mean(dxh, axis=-1, keepdims=True)
                            - xh * jnp.mean(dxh * xh, axis=-1, keepdims=True)))
            dlng.append(jnp.sum(dvn * xh, axis=0, keepdims=True))
            dlnb.append(jnp.sum(dvn, axis=0, keepdims=True))
            ss.append(s)
        cat = lambda parts: parts[0] if gpb == 1 else jnp.concatenate(parts, axis=-1)
        dpu_ref[...] = (do * cat(ss) * _gelu_grad(pu)).astype(MXU)
        dpv_ref[...] = (cat(dvs) * _gelu_grad(pv)).astype(MXU)
        dlng_ref[...] += cat(dlng)
        dlnb_ref[...] += cat(dlnb)

    tile = pl.BlockSpec((tr, lb), lambda j, i: (i, j))
    vec = pl.BlockSpec((1, lb), lambda j, i: (0, j))
    half = jax.ShapeDtypeStruct((S, G * BLOCK), MXU)
    return pl.pallas_call(
        body, name="sgu_bwd", grid=(nlb, S // tr), in_specs=specs + [tile, ANY],
        out_specs=[tile, tile, pl.BlockSpec((gpb, BLOCK, BLOCK), lambda j, i: (j, 0, 0)),
                   pl.BlockSpec((gpb, BLOCK, 1), lambda j, i: (j, 0, 0)), vec, vec],
        out_shape=[half, half, jax.ShapeDtypeStruct((G, BLOCK, BLOCK), F32),
                   jax.ShapeDtypeStruct((G, BLOCK, 1), F32),
                   jax.ShapeDtypeStruct((1, G * BLOCK), F32), jax.ShapeDtypeStruct((1, G * BLOCK), F32)],
        compiler_params=_params())(proj, proj, lng, lnb, ws, bs, dsgu, after)


def _store_f32(vals, extra, outs):
    for v, o in zip(vals, outs):
        o[...] = v


def _store_mxu(vals, extra, outs):
    for v, o in zip(vals, outs):
        o[...] = v.astype(MXU)


def _proj_in(h, w):
    S, D = h.shape
    Ns = w.shape[2]
    tm, tn = _tile(S, 1024), _tile(Ns, 1024)
    npb = Ns // tn
    return _mm("proj_in", (S // tm, N_CHIPS, npb), 0, [h, w],
               [pl.BlockSpec((tm, D), lambda i, s, j: (i, 0)), pl.BlockSpec((None, D, tn), lambda i, s, j: (s, 0, j))],
               [(0, 1, 0)], NN, 0, [jax.ShapeDtypeStruct((S, N_CHIPS * Ns), F32)],
               [pl.BlockSpec((tm, tn), lambda i, s, j: (i, s * npb + j))], [None], _store_f32)[0]


def _branches(attn, sgu, wa, ws, proj, gate0):
    S, AW = attn.shape
    SW = sgu.shape[1]
    Nb = wa.shape[2]
    D = N_CHIPS * Nb
    tm = _tile(S, 512)
    assert gate0 % Nb == 0
    ga, gb = gate0 // Nb, (gate0 + D) // Nb

    def epilogue(vals, extra, outs):
        a, b = vals
        outs[0][...] = (_sigmoid(extra[0][...]) * a + _sigmoid(extra[1][...]) * b).astype(MXU)
        outs[1][...] = a
        outs[2][...] = b

    tile = pl.BlockSpec((tm, Nb), lambda i, s: (i, s))
    wspec = lambda k: pl.BlockSpec((None, k, Nb), lambda i, s: (s, 0, 0))
    f = jax.ShapeDtypeStruct((S, D), F32)
    return _mm("branches", (S // tm, N_CHIPS), 0, [attn, sgu, wa, ws, proj, proj],
               [pl.BlockSpec((tm, AW), lambda i, s: (i, 0)), pl.BlockSpec((tm, SW), lambda i, s: (i, 0)),
                wspec(AW), wspec(SW), pl.BlockSpec((tm, Nb), lambda i, s: (i, ga + s)),
                pl.BlockSpec((tm, Nb), lambda i, s: (i, gb + s))],
               [(0, 2, 0), (1, 3, 1)], NN, 2, [jax.ShapeDtypeStruct((S, D), MXU), f, f], [tile] * 3,
               [None, None], epilogue, chunk=MXU_CHUNK)


def _rows_mm(name, a, w, res):
    S = a.shape[0]
    _, K, N = w.shape
    tm, tn = _tile(S, 1024), _tile(N, 1024)

    def epilogue(vals, extra, outs):
        outs[0][...] = extra[0][...] + vals[0]

    out = pl.BlockSpec((tm, tn), lambda i, j, s: (i, j))
    return _mm(name, (S // tm, N // tn, N_CHIPS), 1, [a, w, res],
               [pl.BlockSpec((tm, K), lambda i, j, s: (i, s)), pl.BlockSpec((None, K, tn), lambda i, j, s: (s, 0, j)), out],
               [(0, 1, 0)], NN, 1, [jax.ShapeDtypeStruct((S, N), F32)], [out], [(tm, tn)], epilogue)[0]


def _gate_up(h2, wg, wu):
    S, D = h2.shape
    Nf = wg.shape[2]
    tm = _tile(S, 256)

    def epilogue(vals, extra, outs):
        g, u = vals
        outs[0][...] = g
        outs[1][...] = u
        outs[2][...] = (g * _sigmoid(g) * u).astype(MXU)

    w = pl.BlockSpec((None, D, Nf), lambda s, i: (s, 0, 0))
    o = pl.BlockSpec((tm, Nf), lambda s, i: (i, s))
    f = jax.ShapeDtypeStruct((S, N_CHIPS * Nf), F32)
    return _mm("gate_up", (N_CHIPS, S // tm), 0, [h2, wg, wu],
               [pl.BlockSpec((tm, D), lambda s, i: (i, 0)), w, w], [(0, 1, 0), (0, 2, 1)], NN, 0,
               [f, f, jax.ShapeDtypeStruct((S, N_CHIPS * Nf), MXU)], [o, o, o], [None, None], epilogue,
               chunk=MXU_CHUNK)


def _down_bwd(dyb, wd, g, u):
    S, D = dyb.shape
    Kf = wd.shape[1]
    tm = _tile(S, 512)

    def epilogue(vals, extra, outs):
        da, gv, uv = vals[0], extra[0][...], extra[1][...]
        sg = _sigmoid(gv)
        outs[0][...] = (da * uv * sg * (1.0 + gv * (1.0 - sg))).astype(MXU)
        outs[1][...] = (da * gv * sg).astype(MXU)

    t = pl.BlockSpec((tm, Kf), lambda i, s: (i, s))
    o = jax.ShapeDtypeStruct((S, N_CHIPS * Kf), MXU)
    return _mm("down_bwd", (S // tm, N_CHIPS), 0, [dyb, wd, g, u],
               [pl.BlockSpec((tm, D), lambda i, s: (i, 0)), pl.BlockSpec((None, Kf, D), lambda i, s: (s, 0, 0)), t, t],
               [(0, 1, 0)], NT, 2, [o, o], [t, t], [None], epilogue, chunk=MXU_CHUNK)


def _out_bwd(dxb, wo, proj, ba, bb, gate0):
    S, D = dxb.shape
    Ko = wo.shape[1]
    tm = _tile(S, 512)
    assert gate0 % Ko == 0
    ga, gb = gate0 // Ko, (gate0 + D) // Ko

    def epilogue(vals, extra, outs):
        dm = vals[0]
        sa, sb = _sigmoid(extra[0][...]), _sigmoid(extra[1][...])
        outs[0][...] = (dm * sa).astype(MXU)
        outs[1][...] = (dm * sb).astype(MXU)
        outs[2][...] = (dm * extra[2][...] * sa * (1.0 - sa)).astype(MXU)
        outs[3][...] = (dm * extra[3][...] * sb * (1.0 - sb)).astype(MXU)

    t = pl.BlockSpec((tm, Ko), lambda i, s: (i, s))
    o = jax.ShapeDtypeStruct((S, D), MXU)
    return _mm("out_bwd", (S // tm, N_CHIPS), 0, [dxb, wo, proj, proj, ba, bb],
               [pl.BlockSpec((tm, D), lambda i, s: (i, 0)), pl.BlockSpec((None, Ko, D), lambda i, s: (s, 0, 0)),
                pl.BlockSpec((tm, Ko), lambda i, s: (i, ga + s)), pl.BlockSpec((tm, Ko), lambda i, s: (i, gb + s)), t, t],
               [(0, 1, 0)], NT, 4, [o] * 4, [t] * 4, [None], epilogue, chunk=MXU_CHUNK)


def _dx_cols(name, terms, n_out, after=None):
    S = terms[0][0].shape[0]
    _, K, Ns = terms[0][1].shape
    tm, tko, tn = _tile(S, 1024), _tile(K, 1024), _tile(Ns, 1920 if len(terms) == 1 else 1408)
    npb = Ns // tn
    operands, specs, pairs = [], [], []
    for t, (dy, w, k) in enumerate(terms):
        assert w.shape == (N_CHIPS, K, Ns)
        operands += [dy, w]
        specs += [pl.BlockSpec((tm, tn), lambda i, jk, s, jn: (i, s * npb + jn)),
                  pl.BlockSpec((None, tko, tn), lambda i, jk, s, jn: (s, jk, jn))]
        pairs.append((2 * t, 2 * t + 1, k))
    out = pl.BlockSpec((tm, tko), lambda i, jk, s, jn: (i, jk))
    return _mm(name, (S // tm, K // tko, N_CHIPS, npb), 2, operands, specs, pairs, NT, 0,
               [jax.ShapeDtypeStruct((S, K), F32)] * n_out, [out] * n_out, [(tm, tko)] * n_out, _store_f32, after)


def _dw_cols(name, a, dy):
    S, K = a.shape
    Ns = dy.shape[1] // N_CHIPS
    tk, tn = _tile(K, 512), _tile(Ns, 1408)
    npb = Ns // tn
    return _mm(name, (K // tk, N_CHIPS, npb), 0, [a, dy],
               [pl.BlockSpec((S, tk), lambda jk, s, jn: (0, jk)), pl.BlockSpec((S, tn), lambda jk, s, jn: (0, s * npb + jn))],
               [(0, 1, 0)], TN, 0, [jax.ShapeDtypeStruct((N_CHIPS, K, Ns), MXU)],
               [pl.BlockSpec((None, tk, tn), lambda jk, s, jn: (s, jk, jn))], [None], _store_mxu)[0]


def _dw_rows(name, a, dy):
    S = a.shape[0]
    K = a.shape[1] // N_CHIPS
    N = dy.shape[1]
    tk, tn = _tile(K, 1408), _tile(N, 1024)
    nkb = K // tk
    return _mm(name, (N_CHIPS, nkb, N // tn), 0, [a, dy],
               [pl.BlockSpec((S, tk), lambda s, jk, jn: (0, s * nkb + jk)), pl.BlockSpec((S, tn), lambda s, jk, jn: (0, jn))],
               [(0, 1, 0)], TN, 0, [jax.ShapeDtypeStruct((N_CHIPS, K, N), MXU)],
               [pl.BlockSpec((None, tk, tn), lambda s, jk, jn: (s, jk, jn))], [None], _store_mxu)[0]


def _layer_fwd(x, stream, layer, last, sp, cos, sin, dims):
    AW, KW, gate0, u_col = dims
    h = _rms_fwd("mix_norm", x, sp["mix_norm"])
    w = stream.finish(layer, 0, h)
    proj = _proj_in(h, w["w_in"])
    qr, kr, vb = _qk_prep(proj, sp["q_norm"], sp["k_norm"], cos, sin, AW, KW)
    stream.forward(layer, 1, qr)
    attn = _attn_fwd(qr, kr, vb, sp["sinks"])
    w.update(stream.finish(layer, 1, attn))
    sgu = _sgu_fwd(proj, sp["sgu_ln_g"], sp["sgu_ln_b"], sp["w_spatial"], sp["b_spatial"], u_col)
    merged, ba, bb = _branches(attn, sgu, w["w_attn_branch"], w["w_sgu_branch"], proj, gate0)
    stream.forward(layer, 2, merged)
    x1 = _rows_mm("out_proj", merged, w["w_out"], x)
    w.update(stream.finish(layer, 2, x1))
    h2 = _rms_fwd("ffn_norm", x1, sp["ffn_norm"])
    stream.forward(layer, 3, h2)
    g, u, act = _gate_up(h2, w["w_gate"], w["w_up"])
    w.update(stream.finish(layer, 3, g))
    x2 = _rows_mm("down_proj", act, w["w_down"], x1)
    if not last:
        stream.forward(layer + 1, 0, x2)
    saved = dict(x=x, h=h, proj=proj, qr=qr, kr=kr, vb=vb, attn=attn, sgu=sgu, merged=merged, ba=ba, bb=bb,
                 x1=x1, h2=h2, g=g, u=u, act=act)
    return x2, saved, w


def _layer_bwd(dy, dyb, w, sp, sv, cos, sin, dims, reducer, layer):
    AW, KW, gate0, u_col = dims
    big, small = {}, {}
    dg, du = _down_bwd(dyb, w["w_down"], sv["g"], sv["u"])
    big["w_down"] = _dw_rows("dw_down", sv["act"], dyb)
    big["w_gate"] = _dw_cols("dw_gate", sv["h2"], dg)
    big["w_up"] = _dw_cols("dw_up", sv["h2"], du)
    token = reducer.start(layer, 2, big)
    dh2 = _dx_cols("dh2", [(dg, w["w_gate"], 0), (du, w["w_up"], 0)], 1, token)[0]
    token = reducer.scatter(layer, 2, dh2)
    dx1, dx1b, small["ffn_norm"] = _rms_bwd("ffn_norm_bwd", dh2, sv["x1"], sp["ffn_norm"], dy, token)
    dba, dbb, dgla, dglb = _out_bwd(dx1b, w["w_out"], sv["proj"], sv["ba"], sv["bb"], gate0)
    big["w_out"] = _dw_rows("dw_out", sv["merged"], dx1b)
    big["w_attn_branch"] = _dw_cols("dw_attn_branch", sv["attn"], dba)
    big["w_sgu_branch"] = _dw_cols("dw_sgu_branch", sv["sgu"], dbb)
    token = reducer.start(layer, 1, big)
    dattn, dsgu = _dx_cols("dbranch_in", [(dba, w["w_attn_branch"], 0), (dbb, w["w_sgu_branch"], 1)], 2, token)
    token = reducer.scatter(layer, 1, dsgu)
    dpu, dpv, small["w_spatial"], db, small["sgu_ln_g"], small["sgu_ln_b"] = _sgu_bwd(
        sv["proj"], sp["sgu_ln_g"], sp["sgu_ln_b"], sp["w_spatial"], sp["b_spatial"], dsgu, u_col, token)
    small["b_spatial"] = db[:, :, 0]
    dq, dkp, dkc, dvp, dvc, dsink = _attn_bwd(sv["qr"], sv["kr"], sv["vb"], sp["sinks"], dattn)
    small["sinks"] = dsink[:, :sp["sinks"].shape[1]]
    dqkv, dqg, dkg = _qk_prep_bwd(sv["proj"], sp["q_norm"], sp["k_norm"], cos, sin, dq, dkp, dkc, dvp, dvc, AW, KW)
    small["q_norm"] = dqg[:, :HEAD_DIM]
    small["k_norm"] = dkg[:, :HEAD_DIM]
    dproj = jnp.concatenate([dqkv, dpu, dpv, dgla, dglb], axis=1)
    big["w_in"] = _dw_cols("dw_in", sv["h"], dproj)
    token = reducer.start(layer, 0, big)
    dh = _dx_cols("dh", [(dproj, w["w_in"], 0)], 1, token)[0]
    token = reducer.scatter(layer, 0, dh)
    dx, dxb, small["mix_norm"] = _rms_bwd("mix_norm_bwd", dh, sv["x"], sp["mix_norm"], dx1, token)
    return dx, dxb, small


def _place():
    x, y, c = lax.axis_index("x"), lax.axis_index("y"), lax.axis_index("c")
    chips = [(1 - x, y), (x, 1 - y), (1 - x, 1 - y)]
    return x, y, c, chips


def _half_rows(c, rows):
    h = rows // 2
    assert h % 16 == 0
    return pl.ds(pl.multiple_of(c * h, 16), h)


def _row_tile(rows, pref):
    best = None
    for t in range(16, min(rows, pref) + 1, 16):
        if rows % t == 0:
            best = t
    assert best is not None, rows
    return best


def _cast_own(name, chip, w, layer):
    _, R, C = w.shape
    tr = _row_tile(R, 512)

    def body(chip_ref, w_ref, o_ref):
        o_ref[...] = w_ref[...].astype(MXU)

    return pl.pallas_call(
        body, name=name, out_shape=jax.ShapeDtypeStruct((N_CHIPS, R, C), MXU),
        grid_spec=pltpu.PrefetchScalarGridSpec(
            num_scalar_prefetch=1, grid=(R // tr,),
            in_specs=[pl.BlockSpec((None, tr, C), lambda i, chip_ref: (layer, i, 0))],
            out_specs=pl.BlockSpec((None, tr, C), lambda i, chip_ref: (chip_ref[0], i, 0))),
        compiler_params=_params())(chip, w)


HBM = pl.BlockSpec(memory_space=pltpu.HBM)
SEM = pl.BlockSpec(memory_space=pltpu.SEMAPHORE)
DATAFLOW = pltpu.SideEffectType.DATAFLOW_SIDE_EFFECTING


def _gather_copies(bufs, send_sem, recv_sem):
    x, y, c, chips = _place()

    def ici(a, j, block):
        px, py = chips[j]
        blk = bufs[a].at[block, _half_rows(c, bufs[a].shape[1])]
        return pltpu.make_async_remote_copy(
            src_ref=blk, dst_ref=blk, send_sem=send_sem.at[3 * a + j], recv_sem=recv_sem.at[3 * a + j],
            device_id=(px, py, c), device_id_type=MESH)

    def d2d(a, j, core):
        px, py = chips[j]
        blk = bufs[a].at[2 * px + py, _half_rows(core, bufs[a].shape[1])]
        return pltpu.make_async_remote_copy(
            src_ref=blk, dst_ref=blk, send_sem=send_sem.at[3 * a + j], recv_sem=recv_sem.at[3 * a + j],
            device_id=(x, y, 1 - c), device_id_type=MESH)

    return ici, d2d


def _in_hbm(bufs):
    return [pltpu.with_memory_space_constraint(b, pltpu.HBM) for b in bufs]


def _gather_start(name, bufs, after):
    n = len(bufs)

    def body(*refs):
        dst = refs[n + 1:2 * n + 1]
        send_sem, recv_sem, token = refs[2 * n + 1:]
        x, y, c, chips = _place()
        ici, _ = _gather_copies(dst, send_sem, recv_sem)
        for a in range(n):
            for j in range(3):
                ici(a, j, 2 * x + y).start()
        token[...] = jnp.zeros(token.shape, token.dtype)

    sems = pltpu.SemaphoreType.DMA((3 * n,))
    outs = pl.pallas_call(
        body, name=name, in_specs=[HBM] * n + [ANY],
        out_specs=[HBM] * n + [SEM, SEM, pl.BlockSpec(memory_space=pltpu.VMEM)],
        out_shape=[pltpu.HBM(b.shape, b.dtype) for b in bufs] + [sems, sems, jax.ShapeDtypeStruct((8, BLOCK), F32)],
        input_output_aliases={a: a for a in range(n)},
        compiler_params=pltpu.CompilerParams(has_side_effects=DATAFLOW))(*_in_hbm(bufs), after)
    return outs[:n], outs[n], outs[n + 1], outs[n + 2]


def _gather_forward(name, bufs, ici_send, ici_recv, after):
    n = len(bufs)

    def body(*refs):
        ici_send_ref, ici_recv_ref = refs[n], refs[n + 1]
        dst = refs[n + 3:2 * n + 3]
        d2d_send, d2d_recv = refs[2 * n + 3:]
        x, y, c, chips = _place()
        ici, _ = _gather_copies(dst, ici_send_ref, ici_recv_ref)
        _, d2d = _gather_copies(dst, d2d_send, d2d_recv)
        for a in range(n):
            for j, (px, py) in enumerate(chips):
                ici(a, j, 2 * px + py).wait_recv()
                d2d(a, j, c).start()
        for a in range(n):
            for j in range(3):
                ici(a, j, 2 * x + y).wait_send()

    sems = pltpu.SemaphoreType.DMA((3 * n,))
    outs = pl.pallas_call(
        body, name=name, in_specs=[HBM] * n + [SEM, SEM, ANY], out_specs=[HBM] * n + [SEM, SEM],
        out_shape=[pltpu.HBM(b.shape, b.dtype) for b in bufs] + [sems, sems],
        input_output_aliases={a: a for a in range(n)},
        compiler_params=pltpu.CompilerParams(has_side_effects=DATAFLOW))(*bufs, ici_send, ici_recv, after)
    return outs[:n], outs[n], outs[n + 1]


def _gather_finish(name, bufs, d2d_send, d2d_recv, after):
    n = len(bufs)

    def body(*refs):
        send_ref, recv_ref = refs[n], refs[n + 1]
        dst = refs[n + 3:]
        x, y, c, chips = _place()
        _, d2d = _gather_copies(dst, send_ref, recv_ref)
        for a in range(n):
            for j in range(3):
                d2d(a, j, 1 - c).wait_recv()
                d2d(a, j, c).wait_send()

    return pl.pallas_call(
        body, name=name, in_specs=[HBM] * n + [SEM, SEM, ANY], out_specs=[HBM] * n,
        out_shape=[pltpu.HBM(b.shape, b.dtype) for b in bufs],
        input_output_aliases={a: a for a in range(n)},
        compiler_params=pltpu.CompilerParams(has_side_effects=DATAFLOW))(*bufs, d2d_send, d2d_recv, after)


GATHER = (("w_in",), ("w_attn_branch", "w_sgu_branch", "w_out"), ("w_gate", "w_up"), ("w_down",))
REDUCE = (("w_in",), ("w_attn_branch", "w_sgu_branch", "w_out"), ("w_gate", "w_up", "w_down"))


class _WeightStream:
    def __init__(self, started):
        self.started, self.passed = started, {}

    def forward(self, layer, group, after):
        bufs, send, recv = self.started[(layer, group)]
        self.passed[(layer, group)] = _gather_forward("gather_forward_%d_%d" % (layer, group), bufs, send, recv, after)

    def finish(self, layer, group, after):
        bufs, send, recv = self.passed[(layer, group)]
        done = _gather_finish("gather_finish_%d_%d" % (layer, group), bufs, send, recv, after)
        return dict(zip(GATHER[group], done))


def _pair_copies(grads, lands, send_sem, recv_sem):
    x, y, c, _ = _place()

    def make(a):
        theirs = _half_rows(1 - c, grads[a].shape[1])
        return pltpu.make_async_remote_copy(
            src_ref=grads[a].at[:, theirs], dst_ref=lands[a], send_sem=send_sem.at[a], recv_sem=recv_sem.at[a],
            device_id=(x, y, 1 - c), device_id_type=MESH)

    return make


def _pair_start(name, grads, after):
    n = len(grads)
    lands = [lax.empty((g.shape[0], g.shape[1] // 2, g.shape[2]), g.dtype) for g in grads]

    def body(*refs):
        src, dst = refs[2 * n + 1:3 * n + 1], refs[3 * n + 1:4 * n + 1]
        send_sem, recv_sem, token = refs[4 * n + 1:]
        copy = _pair_copies(src, dst, send_sem, recv_sem)
        for a in range(n):
            copy(a).start()
        token[...] = jnp.zeros(token.shape, token.dtype)

    sems = pltpu.SemaphoreType.DMA((n,))
    outs = pl.pallas_call(
        body, name=name, in_specs=[HBM] * (2 * n) + [ANY],
        out_specs=[HBM] * (2 * n) + [SEM, SEM, pl.BlockSpec(memory_space=pltpu.VMEM)],
        out_shape=[pltpu.HBM(b.shape, b.dtype) for b in grads + lands] + [sems, sems, jax.ShapeDtypeStruct((8, BLOCK), F32)],
        input_output_aliases={a: a for a in range(2 * n)},
        compiler_params=pltpu.CompilerParams(has_side_effects=DATAFLOW))(*_in_hbm(grads + lands), after)
    return outs[:n], outs[n:2 * n], outs[2 * n], outs[2 * n + 1], outs[2 * n + 2]


def _pair_finish(name, grads, lands, send_sem, recv_sem, after):
    n = len(grads)

    def body(*refs):
        send_ref, recv_ref = refs[2 * n], refs[2 * n + 1]
        src, dst = refs[2 * n + 3:3 * n + 3], refs[3 * n + 3:]
        copy = _pair_copies(src, dst, send_ref, recv_ref)
        for a in range(n):
            copy(a).wait_send()
            copy(a).wait_recv()

    outs = pl.pallas_call(
        body, name=name, in_specs=[HBM] * (2 * n) + [SEM, SEM, ANY], out_specs=[HBM] * (2 * n),
        out_shape=[pltpu.HBM(b.shape, b.dtype) for b in grads + lands],
        input_output_aliases={a: a for a in range(2 * n)},
        compiler_params=pltpu.CompilerParams(has_side_effects=DATAFLOW))(*grads, *lands, send_sem, recv_sem, after)
    return outs[:n], outs[n:]


def _pair_sum(name, core, g, p):
    _, h, C = p.shape
    tr = _row_tile(h, 512)
    nrb = h // tr

    def body(core_ref, g_ref, p_ref, o_ref):
        o_ref[...] = (g_ref[...].astype(F32) + p_ref[...].astype(F32)).astype(o_ref.dtype)

    spec = pl.BlockSpec((None, tr, C), lambda s, i, core_ref: (s, i, 0))
    return pl.pallas_call(
        body, name=name, out_shape=jax.ShapeDtypeStruct(p.shape, p.dtype),
        grid_spec=pltpu.PrefetchScalarGridSpec(
            num_scalar_prefetch=1, grid=(N_CHIPS, nrb),
            in_specs=[pl.BlockSpec((None, tr, C), lambda s, i, core_ref: (s, core_ref[0] * nrb + i, 0)), spec],
            out_specs=spec),
        compiler_params=_params())(core, g, p)


def _scatter_copies(sums, slots, send_sem, recv_sem):
    x, y, c, chips = _place()

    def make(a, j):
        px, py = chips[j]
        return pltpu.make_async_remote_copy(
            src_ref=sums[a].at[2 * px + py], dst_ref=slots[a].at[j], send_sem=send_sem.at[3 * a + j],
            recv_sem=recv_sem.at[3 * a + j], device_id=(px, py, c), device_id_type=MESH)

    return make


def _scatter_start(name, sums, after):
    n = len(sums)
    slots = [lax.empty((3,) + s.shape[1:], s.dtype) for s in sums]

    def body(*refs):
        src, dst = refs[2 * n + 1:3 * n + 1], refs[3 * n + 1:4 * n + 1]
        send_sem, recv_sem, token = refs[4 * n + 1:]
        copy = _scatter_copies(src, dst, send_sem, recv_sem)
        for a in range(n):
            for j in range(3):
                copy(a, j).start()
        token[...] = jnp.zeros(token.shape, token.dtype)

    sems = pltpu.SemaphoreType.DMA((3 * n,))
    outs = pl.pallas_call(
        body, name=name, in_specs=[HBM] * (2 * n) + [ANY],
        out_specs=[HBM] * (2 * n) + [SEM, SEM, pl.BlockSpec(memory_space=pltpu.VMEM)],
        out_shape=[pltpu.HBM(b.shape, b.dtype) for b in sums + slots] + [sems, sems, jax.ShapeDtypeStruct((8, BLOCK), F32)],
        input_output_aliases={a: a for a in range(2 * n)},
        compiler_params=pltpu.CompilerParams(has_side_effects=DATAFLOW))(*_in_hbm(sums + slots), after)
    return outs[:n], outs[n:2 * n], outs[2 * n], outs[2 * n + 1], outs[2 * n + 2]


def _scatter_finish(name, sums, slots, send_sem, recv_sem, after):
    n = len(sums)

    def body(*refs):
        send_ref, recv_ref = refs[2 * n], refs[2 * n + 1]
        src, dst = refs[2 * n + 3:3 * n + 3], refs[3 * n + 3:]
        copy = _scatter_copies(src, dst, send_ref, recv_ref)
        for a in range(n):
            for j in range(3):
                copy(a, j).wait_send()
                copy(a, j).wait_recv()

    outs = pl.pallas_call(
        body, name=name, in_specs=[HBM] * (2 * n) + [SEM, SEM, ANY], out_specs=[HBM] * (2 * n),
        out_shape=[pltpu.HBM(b.shape, b.dtype) for b in sums + slots],
        input_output_aliases={a: a for a in range(2 * n)},
        compiler_params=pltpu.CompilerParams(has_side_effects=DATAFLOW))(*sums, *slots, send_sem, recv_sem, after)
    return outs[:n], outs[n:]


def _slot_sum(name, place, slots, sums):
    _, h, C = slots.shape
    tr = _row_tile(h, 512)
    nrb = h // tr

    def body(place_ref, r0, r1, r2, own, o_ref):
        o_ref[...] = ((r0[...].astype(F32) + r1[...].astype(F32)) + r2[...].astype(F32)) + own[...].astype(F32)

    slot = lambda k: pl.BlockSpec((None, tr, C), lambda i, place_ref: (k, i, 0))
    return pl.pallas_call(
        body, name=name, out_shape=jax.ShapeDtypeStruct((2 * h, C), F32),
        grid_spec=pltpu.PrefetchScalarGridSpec(
            num_scalar_prefetch=1, grid=(nrb,),
            in_specs=[slot(0), slot(1), slot(2),
                      pl.BlockSpec((None, tr, C), lambda i, place_ref: (place_ref[0], i, 0))],
            out_specs=pl.BlockSpec((tr, C), lambda i, place_ref: (place_ref[1] * nrb + i, 0))),
        compiler_params=_params())(place, slots, slots, slots, sums)


def _half_copies(bufs, send_sem, recv_sem):
    x, y, c, _ = _place()

    def make(a, core):
        rows = bufs[a].at[_half_rows(core, bufs[a].shape[0])]
        return pltpu.make_async_remote_copy(
            src_ref=rows, dst_ref=rows, send_sem=send_sem.at[a], recv_sem=recv_sem.at[a],
            device_id=(x, y, 1 - c), device_id_type=MESH)

    return make


def _half_start(name, bufs, after):
    n = len(bufs)

    def body(*refs):
        dst = refs[n + 1:2 * n + 1]
        send_sem, recv_sem, token = refs[2 * n + 1:]
        c = lax.axis_index("c")
        copy = _half_copies(dst, send_sem, recv_sem)
        for a in range(n):
            copy(a, c).start()
        token[...] = jnp.zeros(token.shape, token.dtype)

    sems = pltpu.SemaphoreType.DMA((n,))
    outs = pl.pallas_call(
        body, name=name, in_specs=[HBM] * n + [ANY],
        out_specs=[HBM] * n + [SEM, SEM, pl.BlockSpec(memory_space=pltpu.VMEM)],
        out_shape=[pltpu.HBM(b.shape, b.dtype) for b in bufs] + [sems, sems, jax.ShapeDtypeStruct((8, BLOCK), F32)],
        input_output_aliases={a: a for a in range(n)},
        compiler_params=pltpu.CompilerParams(has_side_effects=DATAFLOW))(*_in_hbm(bufs), after)
    return outs[:n], outs[n], outs[n + 1], outs[n + 2]


def _half_finish(name, bufs, send_sem, recv_sem, after):
    n = len(bufs)

    def body(*refs):
        send_ref, recv_ref = refs[n], refs[n + 1]
        dst = refs[n + 3:]
        c = lax.axis_index("c")
        copy = _half_copies(dst, send_ref, recv_ref)
        for a in range(n):
            copy(a, c).wait_send()
            copy(a, 1 - c).wait_recv()

    return pl.pallas_call(
        body, name=name, in_specs=[HBM] * n + [SEM, SEM, ANY], out_specs=[HBM] * n,
        out_shape=[pltpu.HBM(b.shape, b.dtype) for b in bufs],
        input_output_aliases={a: a for a in range(n)},
        compiler_params=pltpu.CompilerParams(has_side_effects=DATAFLOW))(*bufs, send_sem, recv_sem, after)


class _GradReducer:
    def __init__(self, chip, core):
        self.core, self.place, self.pairs, self.started = core, jnp.concatenate([chip, core]), {}, []

    def start(self, layer, group, grads):
        mine = [grads[n] for n in REDUCE[group]]
        mine, lands, send, recv, token = _pair_start("grad_pair_start_%d_%d" % (layer, group), mine, self.place)
        self.pairs[(layer, group)] = (mine, lands, send, recv)
        return token

    def scatter(self, layer, group, after):
        tag = "%d_%d" % (layer, group)
        names = REDUCE[group]
        mine, lands, send, recv = self.pairs.pop((layer, group))
        mine, theirs = _pair_finish("grad_pair_finish_" + tag, mine, lands, send, recv, after)
        sums = [_pair_sum("pair_sum_%s_%d" % (n, layer), self.core, g, p) for n, g, p in zip(names, mine, theirs)]
        sums, slots, send, recv, token = _scatter_start("grad_scatter_start_" + tag, sums, self.place)
        self.started.append((layer, names, sums, slots, send, recv))
        return token

    def finish(self, after, update):
        for layer in sorted({entry[0] for entry in self.started}, reverse=True):
            exchanged = []
            for lyr, names, sums, slots, send, recv in self.started:
                if lyr != layer:
                    continue
                tag = "%s_%d" % (names[0], layer)
                sums, slots = _scatter_finish("grad_scatter_finish_" + tag, sums, slots, send, recv, after)
                halves = [_slot_sum("slot_sum_%s_%d" % (n, layer), self.place, r, s)
                          for n, r, s in zip(names, slots, sums)]
                halves, send, recv, after = _half_start("grad_half_start_" + tag, halves, self.place)
                exchanged.append((tag, names, halves, send, recv))
            for tag, names, halves, send, recv in exchanged:
                whole = _half_finish("grad_half_finish_" + tag, halves, send, recv, after)
                after = update(layer, dict(zip(names, whole)))


def _all_reduce_small(arrays, after):
    n = len(arrays)
    n_dev = 2 * N_CHIPS

    def body(*refs):
        xs, outs, gats = refs[:n], refs[n + 1:2 * n + 1], refs[2 * n + 1:3 * n + 1]
        send_sems, recv_sems, local_sems = refs[3 * n + 1:]
        x, y, c, chips = _place()
        me, sibling = (x, y, c), (x, y, 1 - c)

        def slot(a, px, py, pc):
            return gats[a].at[4 * px + 2 * py + pc]

        def copy(a, k, block, to, src=None):
            return pltpu.make_async_remote_copy(
                src_ref=slot(a, *block) if src is None else src, dst_ref=slot(a, *block),
                send_sem=send_sems.at[7 * a + k], recv_sem=recv_sems.at[7 * a + k], device_id=to, device_id_type=MESH)

        local = [pltpu.make_async_copy(xs[a], slot(a, *me), local_sems.at[a]) for a in range(n)]
        for cp in local:
            cp.start()
        sends = []
        for a in range(n):
            sends.append(copy(a, 0, me, sibling, src=xs[a]))
            sends += [copy(a, 1 + j, me, (*chip, c), src=xs[a]) for j, chip in enumerate(chips)]
        for cp in sends:
            cp.start()
        for a in range(n):
            for j, chip in enumerate(chips):
                copy(a, 1 + j, (*chip, c), me).wait_recv()
                sends.append(copy(a, 4 + j, (*chip, c), sibling))
                sends[-1].start()
        for a in range(n):
            copy(a, 0, sibling, me).wait_recv()
            for j, chip in enumerate(chips):
                copy(a, 4 + j, (*chip, 1 - c), me).wait_recv()
        for cp in sends:
            cp.wait_send()
        for cp in local:
            cp.wait()
        for a in range(n):
            acc = gats[a][0]
            for d in range(1, n_dev):
                acc = acc + gats[a][d]
            outs[a][...] = acc

    vm = pl.BlockSpec(memory_space=pltpu.VMEM)
    return pl.pallas_call(
        body, name="small_grad_all_reduce", in_specs=[vm] * n + [ANY], out_specs=[vm] * n,
        out_shape=[jax.ShapeDtypeStruct(a.shape, F32) for a in arrays],
        scratch_shapes=[pltpu.VMEM((n_dev,) + a.shape, F32) for a in arrays]
        + [pltpu.SemaphoreType.DMA((7 * n,)), pltpu.SemaphoreType.DMA((7 * n,)), pltpu.SemaphoreType.DMA((n,))],
        compiler_params=_params())(*arrays, after)


def _adamw_math(w, g, m, v):
    m2 = ADAM_B1 * m + (1.0 - ADAM_B1) * g
    v2 = ADAM_B2 * v + (1.0 - ADAM_B2) * (g * g)
    m_hat = m2 / (1.0 - ADAM_B1 ** ADAM_STEP)
    v_hat = v2 / (1.0 - ADAM_B2 ** ADAM_STEP)
    delta = -ADAM_LR * (m_hat / (jnp.sqrt(v_hat) + ADAM_EPS) + ADAM_WD * w)
    return delta, m2, v2


def _adamw_big(name, layer, grad, w, m, v, others):
    L, R, C = w.shape
    tr = _row_tile(R, 256)

    def body(g_ref, w_ref, m_ref, v_ref, *rest):
        go_ref, d_ref, mo_ref, vo_ref = rest[-4:]
        g = g_ref[...]
        delta, m2, v2 = _adamw_math(w_ref[...], g, m_ref[...], v_ref[...])
        go_ref[...] = g
        d_ref[...] = delta
        mo_ref[...] = m2
        vo_ref[...] = v2

    blk = pl.BlockSpec((None, tr, C), lambda i: (layer, i, 0))
    shp = jax.ShapeDtypeStruct(w.shape, F32)
    others = [] if others is None else list(others)
    return pl.pallas_call(
        body, name=name, grid=(R // tr,),
        in_specs=[pl.BlockSpec((tr, C), lambda i: (i, 0))] + [blk] * 3 + [ANY] * len(others), out_specs=[blk] * 4,
        out_shape=[shp] * 4, input_output_aliases={4 + k: k for k in range(len(others))},
        compiler_params=_params())(grad, w, m, v, *others)


def _adamw_small(gs, ws, ms, vs):
    n = len(gs)

    def body(*refs):
        for a in range(n):
            g_ref, w_ref, m_ref, v_ref = refs[a], refs[n + a], refs[2 * n + a], refs[3 * n + a]
            delta, m2, v2 = _adamw_math(w_ref[...], g_ref[...], m_ref[...], v_ref[...])
            refs[4 * n + a][...] = delta
            refs[5 * n + a][...] = m2
            refs[6 * n + a][...] = v2

    vm = pl.BlockSpec(memory_space=pltpu.VMEM)
    shapes = [jax.ShapeDtypeStruct(g.shape, F32) for g in gs]
    outs = pl.pallas_call(
        body, name="adamw_small", in_specs=[vm] * (4 * n), out_specs=[vm] * (3 * n), out_shape=shapes * 3,
        compiler_params=_params())(*gs, *ws, *ms, *vs)
    return outs[:n], outs[n:2 * n], outs[2 * n:]


def _rows2d(a):
    return a if a.ndim == 2 else a.reshape(-1, a.shape[-1])


BIG = ("w_in", "w_attn_branch", "w_sgu_branch", "w_out", "w_gate", "w_up", "w_down")
SMALL = ("mix_norm", "q_norm", "k_norm", "sinks", "sgu_ln_g", "sgu_ln_b", "w_spatial", "b_spatial", "ffn_norm")
ORDER = ("mix_norm", "w_in", "q_norm", "k_norm", "sinks", "sgu_ln_g", "sgu_ln_b", "w_spatial", "b_spatial",
         "w_attn_branch", "w_sgu_branch", "w_out", "ffn_norm", "w_gate", "w_up", "w_down")


def _rope_tables(seq):
    pos = jnp.arange(seq, dtype=F32)
    inv_freq = jnp.power(10000.0, -jnp.arange(0, HEAD_DIM, 2, dtype=F32) / HEAD_DIM)
    ang = pos[:, None] * inv_freq[None, :]
    cos, sin = jnp.cos(ang), jnp.sin(ang)
    reps = BLOCK // HEAD_DIM
    return (jnp.tile(jnp.concatenate([cos, cos], axis=1), (1, reps)),
            jnp.tile(jnp.concatenate([-sin, sin], axis=1), (1, reps)))


def kernel(x, mix_norm, w_in, q_norm, k_norm, sinks, sgu_ln_g, sgu_ln_b, w_spatial, b_spatial, w_attn_branch, w_sgu_branch, w_out, ffn_norm, w_gate, w_up, w_down, loss_target, m_mix_norm, m_w_in, m_q_norm, m_k_norm, m_sinks, m_sgu_ln_g, m_sgu_ln_b, m_w_spatial, m_b_spatial, m_w_attn_branch, m_w_sgu_branch, m_w_out, m_ffn_norm, m_w_gate, m_w_up, m_w_down, v_mix_norm, v_w_in, v_q_norm, v_k_norm, v_sinks, v_sgu_ln_g, v_sgu_ln_b, v_w_spatial, v_b_spatial, v_w_attn_branch, v_w_sgu_branch, v_w_out, v_ffn_norm, v_w_gate, v_w_up, v_w_down):
    weights = dict(mix_norm=mix_norm, w_in=w_in, q_norm=q_norm, k_norm=k_norm, sinks=sinks, sgu_ln_g=sgu_ln_g,
                   sgu_ln_b=sgu_ln_b, w_spatial=w_spatial, b_spatial=b_spatial, w_attn_branch=w_attn_branch,
                   w_sgu_branch=w_sgu_branch, w_out=w_out, ffn_norm=ffn_norm, w_gate=w_gate, w_up=w_up, w_down=w_down)
    mom1 = dict(mix_norm=m_mix_norm, w_in=m_w_in, q_norm=m_q_norm, k_norm=m_k_norm, sinks=m_sinks,
                sgu_ln_g=m_sgu_ln_g, sgu_ln_b=m_sgu_ln_b, w_spatial=m_w_spatial, b_spatial=m_b_spatial,
                w_attn_branch=m_w_attn_branch, w_sgu_branch=m_w_sgu_branch, w_out=m_w_out, ffn_norm=m_ffn_norm,
                w_gate=m_w_gate, w_up=m_w_up, w_down=m_w_down)
    mom2 = dict(mix_norm=v_mix_norm, w_in=v_w_in, q_norm=v_q_norm, k_norm=v_k_norm, sinks=v_sinks,
                sgu_ln_g=v_sgu_ln_g, sgu_ln_b=v_sgu_ln_b, w_spatial=v_w_spatial, b_spatial=v_b_spatial,
                w_attn_branch=v_w_attn_branch, w_sgu_branch=v_w_sgu_branch, w_out=v_w_out, ffn_norm=v_ffn_norm,
                w_gate=v_w_gate, w_up=v_w_up, w_down=v_w_down)
    xs, target = x[0], loss_target[0]
    S, D = xs.shape
    L = w_in.shape[0]
    AW, KW, SW = N_Q_HEADS * HEAD_DIM, N_KV_HEADS * HEAD_DIM, SGU_GROUPS * BLOCK
    dims = (AW, KW, AW + 2 * KW + 2 * SW, AW + 2 * KW)
    cos, sin = _rope_tables(S)
    reps = BLOCK // HEAD_DIM

    chip = (2 * lax.axis_index("x") + lax.axis_index("y")).astype(jnp.int32).reshape(1)
    core = lax.axis_index("c").astype(jnp.int32).reshape(1)
    started, token = {}, chip
    for l in range(L):
        for gi, names in enumerate(GATHER):
            bufs = [_cast_own("cast_%s_%d" % (n, l), chip, weights[n], l) for n in names]
            bufs, send, recv, token = _gather_start("gather_start_%d_%d" % (l, gi), bufs, token)
            started[(l, gi)] = (bufs, send, recv)
    stream = _WeightStream(started)
    stream.forward(0, 0, token)
    sp = [dict(mix_norm=mix_norm[l][None], ffn_norm=ffn_norm[l][None], q_norm=jnp.tile(q_norm[l][None], (1, reps)),
               k_norm=jnp.tile(k_norm[l][None], (1, reps)), sinks=sinks[l][None], sgu_ln_g=sgu_ln_g[l][None],
               sgu_ln_b=sgu_ln_b[l][None], w_spatial=w_spatial[l], b_spatial=b_spatial[l][:, :, None])
          for l in range(L)]

    act, saved, wl = xs, [], []
    for l in range(L):
        act, sv, w_all = _layer_fwd(act, stream, l, l == L - 1, sp[l], cos, sin, dims)
        saved.append(sv)
        wl.append(w_all)
    loss_part, dy, dyb = _loss_head(act, target)
    loss = lax.psum(loss_part[0, 0], ("x", "y", "c"))

    reducer = _GradReducer(chip, core)
    small_g = [None] * L
    for l in reversed(range(L)):
        dy, dyb, small_g[l] = _layer_bwd(dy, dyb, wl[l], sp[l], saved[l], cos, sin, dims, reducer, l)
    grad_x = dy[None]

    updated = {}

    def update(layer, reduced):
        for n, g in reduced.items():
            updated[n] = _adamw_big("adamw_%s_%d" % (n, layer), layer, g, weights[n], mom1[n], mom2[n],
                                    updated.get(n))
        return updated[n][0]

    reducer.finish(dy, update)
    grads, deltas, new_m, new_v = {}, {}, {}, {}
    for n in BIG:
        grads[n], deltas[n], new_m[n], new_v[n] = updated[n]

    local = [_rows2d(jnp.stack([small_g[l][n].reshape(weights[n].shape[1:]) for l in range(L)])) for n in SMALL]
    g_small = _all_reduce_small(local, updated[BIG[0]][0])
    d_small, m_small, v_small = _adamw_small(g_small, [_rows2d(weights[n]) for n in SMALL],
                                             [_rows2d(mom1[n]) for n in SMALL], [_rows2d(mom2[n]) for n in SMALL])
    for n, g, d, m2, v2 in zip(SMALL, g_small, d_small, m_small, v_small):
        shape = weights[n].shape
        grads[n], deltas[n], new_m[n], new_v[n] = g.reshape(shape), d.reshape(shape), m2.reshape(shape), v2.reshape(shape)

    return (loss, grad_x, *[grads[n] for n in ORDER], *[deltas[n] for n in ORDER],
            *[new_m[n] for n in ORDER], *[new_v[n] for n in ORDER])
```

```python
import functools

import jax
import jax.numpy as jnp
from jax import lax
from jax.experimental import pallas as pl
from jax.experimental.pallas import tpu as pltpu

HEAD_DIM = 64
N_Q_HEADS = 16
N_KV_HEADS = 4
SGU_GROUPS = 8
BLOCK = 128
EPS = 1e-6
ADAM_LR = 0.001
ADAM_B1 = 0.9
ADAM_B2 = 0.999
ADAM_EPS = 1e-08
ADAM_WD = 0.01
ADAM_STEP = 10
N_CHIPS = 4
VMEM_LIMIT = 52 * 1024 * 1024
MXU_CHUNK = 256
ATTN_STACK = 4

F32 = jnp.float32
MXU = jnp.bfloat16
NN = (((1,), (0,)), ((), ()))
NT = (((1,), (1,)), ((), ()))
TN = (((0,), (0,)), ((), ()))
MESH = pl.DeviceIdType.MESH
ANY = pl.BlockSpec(memory_space=pl.ANY)


def _tile(n, pref):
    if n <= pref:
        return n
    best = None
    for t in range(BLOCK, pref + 1, BLOCK):
        if n % t == 0:
            best = t
    assert best is not None, (n, pref)
    return best


def _params():
    return pltpu.CompilerParams(vmem_limit_bytes=VMEM_LIMIT)


def _mm(name, grid, n_red, operands, specs, pairs, dims, n_extra, out_shapes, out_specs,
        acc_shapes, epilogue, after=None, chunk=None):
    n_op = len(operands) - n_extra
    n_out = len(out_shapes)
    n_acc = len(acc_shapes)
    if after is not None:
        operands, specs = list(operands) + [after], list(specs) + [ANY]
    n_in = len(operands)
    axes = [ax for ax in range(len(grid) - n_red, len(grid)) if grid[ax] > 1]

    def body(*refs):
        ops = refs[:n_op]
        extra = refs[n_op:n_op + n_extra]
        outs = refs[n_in:n_in + n_out]
        accs = refs[n_in + n_out:]

        def prod(a, b, cols=None):
            rhs = ops[b]
            if cols is not None:
                rhs = rhs.at[:, cols] if dims == NN else rhs.at[cols, :]
            return lax.dot_general(ops[a][...], rhs[...], dims, preferred_element_type=F32)

        def products(cols=None):
            vals = [None] * n_acc
            for a, b, k in pairs:
                d = prod(a, b, cols)
                vals[k] = d if vals[k] is None else vals[k] + d
            return vals

        if not axes and chunk is not None:
            width = outs[0].shape[-1]
            for c0 in range(0, width, chunk):
                cols = pl.ds(c0, min(chunk, width - c0))
                epilogue(products(cols), [e.at[:, cols] for e in extra], [o.at[:, cols] for o in outs])
        elif not axes:
            epilogue(products(), extra, outs)
        else:
            first = pl.program_id(axes[0]) == 0
            last = pl.program_id(axes[0]) == grid[axes[0]] - 1
            for ax in axes[1:]:
                first = jnp.logical_and(first, pl.program_id(ax) == 0)
                last = jnp.logical_and(last, pl.program_id(ax) == grid[ax] - 1)

            @pl.when(first)
            def _():
                for acc in accs:
                    acc[...] = jnp.zeros(acc.shape, F32)

            for a, b, k in pairs:
                accs[k][...] += prod(a, b)

            @pl.when(last)
            def _():
                epilogue([acc[...] for acc in accs], extra, outs)

    scratch = [pltpu.VMEM(s, F32) for s in acc_shapes] if axes else []
    return pl.pallas_call(
        body, name=name, grid=grid, in_specs=specs, out_specs=out_specs, out_shape=out_shapes,
        scratch_shapes=scratch, compiler_params=_params())(*operands)


def _sigmoid(x):
    return 1.0 / (1.0 + jnp.exp(-x))


_GELU_C = 0.7978845608028654
_GELU_A = 0.044715


def _gelu(x):
    return 0.5 * x * (1.0 + jnp.tanh(_GELU_C * (x + _GELU_A * x * x * x)))


def _gelu_grad(x):
    t = jnp.tanh(_GELU_C * (x + _GELU_A * x * x * x))
    return 0.5 * (1.0 + t) + 0.5 * x * (1.0 - t * t) * _GELU_C * (1.0 + 3.0 * _GELU_A * x * x)


def _rms_fwd(name, x, g):
    S, D = x.shape
    tr = _tile(S, 256)

    def body(x_ref, g_ref, o_ref):
        xv = x_ref[...]
        r = lax.rsqrt(jnp.mean(xv * xv, axis=-1, keepdims=True) + EPS)
        o_ref[...] = (xv * r * g_ref[...]).astype(MXU)

    return pl.pallas_call(
        body, name=name, grid=(S // tr,),
        in_specs=[pl.BlockSpec((tr, D), lambda i: (i, 0)), pl.BlockSpec((1, D), lambda i: (0, 0))],
        out_specs=pl.BlockSpec((tr, D), lambda i: (i, 0)),
        out_shape=jax.ShapeDtypeStruct((S, D), MXU), compiler_params=_params())(x, g)


def _rms_bwd(name, dh, x, g, dres, after):
    S, D = x.shape
    tr = _tile(S, 256)

    def body(dh_ref, x_ref, g_ref, dres_ref, after_ref, dx_ref, dxb_ref, dg_ref):
        xv = x_ref[...]
        r = lax.rsqrt(jnp.mean(xv * xv, axis=-1, keepdims=True) + EPS)
        xh = xv * r
        dhv = dh_ref[...]
        dy = dhv * g_ref[...]
        dx = dres_ref[...] + r * (dy - xh * jnp.mean(dy * xh, axis=-1, keepdims=True))
        dx_ref[...] = dx
        dxb_ref[...] = dx.astype(MXU)

        @pl.when(pl.program_id(0) == 0)
        def _():
            dg_ref[...] = jnp.zeros(dg_ref.shape, F32)

        dg_ref[...] += jnp.sum(dhv * xh, axis=0, keepdims=True)

    row = pl.BlockSpec((tr, D), lambda i: (i, 0))
    vec = pl.BlockSpec((1, D), lambda i: (0, 0))
    return pl.pallas_call(
        body, name=name, grid=(S // tr,), in_specs=[row, row, vec, row, ANY], out_specs=[row, row, vec],
        out_shape=[jax.ShapeDtypeStruct((S, D), F32), jax.ShapeDtypeStruct((S, D), MXU),
                   jax.ShapeDtypeStruct((1, D), F32)],
        compiler_params=_params())(dh, x, g, dres, after)


def _join_columns(name, parts):
    S = parts[0].shape[0]
    tr = _tile(S, 256)
    widths = [p.shape[1] for p in parts]
    assert all(w % BLOCK == 0 for w in widths)

    def body(*refs):
        o_ref, off = refs[-1], 0
        for ref, w in zip(refs[:-1], widths):
            o_ref[:, off:off + w] = ref[...]
            off += w

    return pl.pallas_call(
        body, name=name, grid=(S // tr,), in_specs=[pl.BlockSpec((tr, w), lambda i: (i, 0)) for w in widths],
        out_specs=pl.BlockSpec((tr, sum(widths)), lambda i: (i, 0)),
        out_shape=jax.ShapeDtypeStruct((S, sum(widths)), parts[0].dtype), compiler_params=_params())(*parts)


def _loss_head(y, target):
    S, D = y.shape
    tr = _tile(S, 256)

    def body(y_ref, t_ref, loss_ref, dy_ref, dyb_ref):
        d = y_ref[...] - t_ref[...]
        dy = d * (1.0 / D)
        dy_ref[...] = dy
        dyb_ref[...] = dy.astype(MXU)

        @pl.when(pl.program_id(0) == 0)
        def _():
            loss_ref[...] = jnp.zeros(loss_ref.shape, F32)

        loss_ref[...] += (0.5 / D) * jnp.sum(jnp.sum(d * d, axis=-1, keepdims=True), axis=0, keepdims=True)

    row = pl.BlockSpec((tr, D), lambda i: (i, 0))
    return pl.pallas_call(
        body, name="loss_head", grid=(S // tr,), in_specs=[row, row],
        out_specs=[pl.BlockSpec((1, 1), lambda i: (0, 0)), row, row],
        out_shape=[jax.ShapeDtypeStruct((1, 1), F32), jax.ShapeDtypeStruct((S, D), F32),
                   jax.ShapeDtypeStruct((S, D), MXU)],
        compiler_params=_params())(y, target)


def _head_sum(v):
    r = lax.broadcasted_iota(jnp.int32, (BLOCK, BLOCK), 0) // HEAD_DIM
    c = lax.broadcasted_iota(jnp.int32, (BLOCK, BLOCK), 1) // HEAD_DIM
    ones = jnp.where(r == c, 1.0, 0.0).astype(jnp.bfloat16)
    hi = v.astype(jnp.bfloat16)
    lo = (v - hi.astype(F32)).astype(jnp.bfloat16)
    parts = []
    for t in range(v.shape[1] // BLOCK):
        sl = slice(t * BLOCK, (t + 1) * BLOCK)
        parts.append(jnp.dot(hi[:, sl], ones, preferred_element_type=F32)
                     + jnp.dot(lo[:, sl], ones, preferred_element_type=F32))
    return parts[0] if len(parts) == 1 else jnp.concatenate(parts, axis=-1)


def _swap_halves(v):
    w = v.shape[1]
    half = HEAD_DIM // 2
    lane = lax.broadcasted_iota(jnp.int32, v.shape, 1) % HEAD_DIM
    return jnp.where(lane < half, pltpu.roll(v, w - half, 1), pltpu.roll(v, half, 1))


def _norm_rope(xv, gain, cos, sin):
    r = lax.rsqrt(_head_sum(xv * xv) * (1.0 / HEAD_DIM) + EPS)
    xn = xv * r * gain
    return xn * cos + _swap_halves(xn) * sin


def _norm_rope_bwd(dy, xv, gain, cos, sin):
    r = lax.rsqrt(_head_sum(xv * xv) * (1.0 / HEAD_DIM) + EPS)
    xh = xv * r
    dxn = dy * cos + _swap_halves(dy * sin)
    dgain = jnp.sum(dxn * xh, axis=0, keepdims=True)
    dxh = dxn * gain
    dx = r * (dxh - xh * (_head_sum(dxh * xh) * (1.0 / HEAD_DIM)))
    return dx, dgain


def _fold_heads(v):
    acc = v[:, 0:BLOCK]
    for t in range(1, v.shape[1] // BLOCK):
        acc = acc + v[:, t * BLOCK:(t + 1) * BLOCK]
    return acc + pltpu.roll(acc, HEAD_DIM, 1)


def _tile_lanes(v, width):
    return v if width == BLOCK else jnp.tile(v, (1, width // BLOCK))


def _low_half(rows):
    assert BLOCK == 2 * HEAD_DIM
    return lax.broadcasted_iota(jnp.int32, (rows, BLOCK), 1) < HEAD_DIM


def _spread_heads(v):
    low = _low_half(v.shape[0])
    out = []
    for t in range(v.shape[1] // BLOCK):
        tile = v[:, t * BLOCK:(t + 1) * BLOCK]
        swapped = pltpu.roll(tile, HEAD_DIM, 1)
        out += [jnp.where(low, tile, swapped), jnp.where(low, swapped, tile)]
    return jnp.concatenate(out, axis=-1)


def _gather_heads(v):
    low = _low_half(v.shape[0])
    out = []
    for t in range(v.shape[1] // (2 * BLOCK)):
        a, b = v[:, 2 * t * BLOCK:(2 * t + 1) * BLOCK], v[:, (2 * t + 1) * BLOCK:(2 * t + 2) * BLOCK]
        out.append(jnp.where(low, a + pltpu.roll(a, HEAD_DIM, 1), b + pltpu.roll(b, HEAD_DIM, 1)))
    return out[0] if len(out) == 1 else jnp.concatenate(out, axis=-1)


def _qk_prep(proj, qg, kg, cos, sin, AW, KW):
    S = proj.shape[0]
    tr = _tile(S, 256)
    scale = HEAD_DIM ** -0.5

    def body(q_ref, k_ref, v_ref, qg_ref, kg_ref, cos_ref, sin_ref, qo_ref, ko_ref, vo_ref):
        c, s = cos_ref[...], sin_ref[...]
        q = _norm_rope(q_ref[...], _tile_lanes(qg_ref[...], AW), _tile_lanes(c, AW), _tile_lanes(s, AW))
        k = _norm_rope(k_ref[...], _tile_lanes(kg_ref[...], KW), _tile_lanes(c, KW), _tile_lanes(s, KW))
        qo_ref[...] = (q * scale).astype(MXU)
        ko_ref[...] = _spread_heads(k).astype(MXU)
        vo_ref[...] = _spread_heads(v_ref[...]).astype(MXU)

    assert AW % KW == 0
    vec = pl.BlockSpec((1, BLOCK), lambda i: (0, 0))
    tab = pl.BlockSpec((tr, BLOCK), lambda i: (i, 0))
    wide = pl.BlockSpec((tr, 2 * KW), lambda i: (i, 0))
    return pl.pallas_call(
        body, name="qk_prep", grid=(S // tr,),
        in_specs=[pl.BlockSpec((tr, AW), lambda i: (i, 0)),
                  pl.BlockSpec((tr, KW), lambda i: (i, AW // KW)),
                  pl.BlockSpec((tr, KW), lambda i: (i, AW // KW + 1)), vec, vec, tab, tab],
        out_specs=[pl.BlockSpec((tr, AW), lambda i: (i, 0)), wide, wide],
        out_shape=[jax.ShapeDtypeStruct((S, AW), MXU), jax.ShapeDtypeStruct((S, 2 * KW), MXU),
                   jax.ShapeDtypeStruct((S, 2 * KW), MXU)],
        compiler_params=_params())(proj, proj, proj, qg, kg, cos, sin)


def _stack_heads(x, h0, nh):
    low = _low_half(BLOCK)
    parts = []
    for h in range(h0, h0 + nh):
        tile = x[:, (h // 2) * BLOCK:(h // 2 + 1) * BLOCK]
        parts.append(jnp.where(low if h % 2 == 0 else jnp.logical_not(low), tile, jnp.zeros_like(tile)))
    return jnp.concatenate(parts, axis=0)


def _unstack_heads(y):
    low = _low_half(BLOCK)
    tiles = [jnp.where(low, y[2 * t * BLOCK:(2 * t + 1) * BLOCK], y[(2 * t + 1) * BLOCK:(2 * t + 2) * BLOCK])
             for t in range(y.shape[0] // (2 * BLOCK))]
    return tiles[0] if len(tiles) == 1 else jnp.concatenate(tiles, axis=-1)


def _band_t(n):
    key = lax.broadcasted_iota(jnp.int32, (2 * BLOCK, BLOCK), 0)
    qry = lax.broadcasted_iota(jnp.int32, (2 * BLOCK, BLOCK), 1)
    return (key > qry) & (key <= qry + BLOCK) & ((key >= BLOCK) | (n > 0))


def _attn_probs_t(ok, qs, kcat, h0, nh, sink_ref):
    st = jnp.where(ok, lax.dot_general(kcat, qs, NT, preferred_element_type=F32), -1e30)
    sk = jnp.concatenate([jnp.full((1, BLOCK), sink_ref[0, h], F32) for h in range(h0, h0 + nh)], axis=1)
    m = jnp.maximum(jnp.max(st, axis=0, keepdims=True), sk)
    e = jnp.exp(st - m)
    es = jnp.exp(sk - m)
    rz = 1.0 / (jnp.sum(e, axis=0, keepdims=True) + es)
    return e * rz, es * rz, rz


def _attn_fwd(qr, kr, vb, sinks):
    S, AW = qr.shape
    KW = kr.shape[1]
    nb = S // BLOCK
    nkv = KW // BLOCK
    qpk = AW // (nkv * HEAD_DIM)
    nh = min(ATTN_STACK, qpk)
    assert nh % 2 == 0 and qpk % nh == 0

    def body(sink_ref, q_ref, kp_ref, kc_ref, vp_ref, vc_ref, o_ref):
        n = pl.program_id(0)
        q, kp, kc, vp, vc = q_ref[...], kp_ref[...], kc_ref[...], vp_ref[...], vc_ref[...]
        ok = jnp.concatenate([_band_t(n)] * nh, axis=1)
        outs = []
        for g in range(nkv):
            kcat = jnp.concatenate([kp[:, g * BLOCK:(g + 1) * BLOCK], kc[:, g * BLOCK:(g + 1) * BLOCK]], axis=0)
            vcat = jnp.concatenate([vp[:, g * BLOCK:(g + 1) * BLOCK], vc[:, g * BLOCK:(g + 1) * BLOCK]], axis=0)
            for h0 in range(g * qpk, (g + 1) * qpk, nh):
                pt, _, _ = _attn_probs_t(ok, _stack_heads(q, h0, nh), kcat, h0, nh, sink_ref)
                outs.append(_unstack_heads(lax.dot_general(pt.astype(MXU), vcat, TN, preferred_element_type=F32)))
        o_ref[...] = jnp.concatenate(outs, axis=-1).astype(MXU)

    cur = lambda n: (n, 0)
    prev = lambda n: (jnp.maximum(n - 1, 0), 0)
    return pl.pallas_call(
        body, name="attn_fwd", grid=(nb,),
        in_specs=[pl.BlockSpec(memory_space=pltpu.SMEM), pl.BlockSpec((BLOCK, AW), cur),
                  pl.BlockSpec((BLOCK, KW), prev), pl.BlockSpec((BLOCK, KW), cur),
                  pl.BlockSpec((BLOCK, KW), prev), pl.BlockSpec((BLOCK, KW), cur)],
        out_specs=pl.BlockSpec((BLOCK, AW), cur),
        out_shape=jax.ShapeDtypeStruct((S, AW), MXU), compiler_params=_params())(sinks, qr, kr, kr, vb, vb)


def _attn_bwd(qr, kr, vb, sinks, dattn):
    S, AW = qr.shape
    KW = kr.shape[1]
    nb = S // BLOCK
    nkv = KW // BLOCK
    qpk = AW // (nkv * HEAD_DIM)
    nh = min(ATTN_STACK, qpk)
    scale = HEAD_DIM ** -0.5

    def body(sink_ref, q_ref, kp_ref, kc_ref, vp_ref, vc_ref, do_ref,
             dq_ref, dkp_ref, dkc_ref, dvp_ref, dvc_ref, dsink_ref):
        n = pl.program_id(0)
        q, kp, kc, vp, vc = q_ref[...], kp_ref[...], kc_ref[...], vp_ref[...], vc_ref[...]
        do = do_ref[...].astype(MXU)
        lane = lax.broadcasted_iota(jnp.int32, (1, BLOCK), 1)
        ok = jnp.concatenate([_band_t(n)] * nh, axis=1)
        dsink = jnp.zeros((1, BLOCK), F32)
        dqs, dkps, dkcs, dvps, dvcs = [], [], [], [], []
        for g in range(nkv):
            kcat = jnp.concatenate([kp[:, g * BLOCK:(g + 1) * BLOCK], kc[:, g * BLOCK:(g + 1) * BLOCK]], axis=0)
            vcat = jnp.concatenate([vp[:, g * BLOCK:(g + 1) * BLOCK], vc[:, g * BLOCK:(g + 1) * BLOCK]], axis=0)
            dk, dv = None, None
            for h0 in range(g * qpk, (g + 1) * qpk, nh):
                qs = _stack_heads(q, h0, nh)
                dos = _stack_heads(do, h0, nh)
                pt, ps, _ = _attn_probs_t(ok, qs, kcat, h0, nh, sink_ref)
                dpt = lax.dot_general(vcat, dos, NT, preferred_element_type=F32)
                delta = jnp.sum(pt * dpt, axis=0, keepdims=True)
                dst = (pt * (dpt - delta)).astype(MXU)
                dsk = -ps * delta
                dv_part = jnp.dot(pt.astype(MXU), dos, preferred_element_type=F32)
                dk_part = jnp.dot(dst, qs, preferred_element_type=F32)
                dqs.append(_unstack_heads(lax.dot_general(dst, kcat, TN, preferred_element_type=F32) * scale))
                dk = dk_part if dk is None else dk + dk_part
                dv = dv_part if dv is None else dv + dv_part
                for j in range(nh):
                    tot = jnp.sum(dsk[:, j * BLOCK:(j + 1) * BLOCK], axis=1, keepdims=True)
                    dsink = dsink + jnp.where(lane == h0 + j, tot, 0.0)
            dkps.append(dk[:BLOCK])
            dkcs.append(dk[BLOCK:])
            dvps.append(dv[:BLOCK])
            dvcs.append(dv[BLOCK:])
        dq_ref[...] = jnp.concatenate(dqs, axis=-1)
        dkp_ref[...] = jnp.concatenate(dkps, axis=-1)
        dkc_ref[...] = jnp.concatenate(dkcs, axis=-1)
        dvp_ref[...] = jnp.concatenate(dvps, axis=-1)
        dvc_ref[...] = jnp.concatenate(dvcs, axis=-1)

        @pl.when(n == 0)
        def _():
            dsink_ref[...] = jnp.zeros(dsink_ref.shape, F32)

        dsink_ref[...] += dsink

    cur = lambda n: (n, 0)
    prev = lambda n: (jnp.maximum(n - 1, 0), 0)
    kv = jax.ShapeDtypeStruct((S, KW), F32)
    kvspec = pl.BlockSpec((BLOCK, KW), cur)
    return pl.pallas_call(
        body, name="attn_bwd", grid=(nb,),
        in_specs=[pl.BlockSpec(memory_space=pltpu.SMEM), pl.BlockSpec((BLOCK, AW), cur),
                  pl.BlockSpec((BLOCK, KW), prev), kvspec, pl.BlockSpec((BLOCK, KW), prev), kvspec,
                  pl.BlockSpec((BLOCK, AW), cur)],
        out_specs=[pl.BlockSpec((BLOCK, AW), cur), kvspec, kvspec, kvspec, kvspec,
                   pl.BlockSpec((1, BLOCK), lambda n: (0, 0))],
        out_shape=[jax.ShapeDtypeStruct((S, AW), F32), kv, kv, kv, kv, jax.ShapeDtypeStruct((1, BLOCK), F32)],
        compiler_params=_params())(sinks, qr, kr, kr, vb, vb, dattn)


def _qk_prep_bwd(proj, qg, kg, cos, sin, dq, dkp, dkc, dvp, dvc, AW, KW):
    S = proj.shape[0]
    nb = S // BLOCK

    def body(q_ref, k_ref, qg_ref, kg_ref, cos_ref, sin_ref, dq_ref, dkp_ref, dkc_ref, dvp_ref, dvc_ref,
             o_ref, dqg_ref, dkg_ref):
        n = pl.program_id(0)
        c, s = cos_ref[...], sin_ref[...]
        has_next = jnp.where(n < nb - 1, 1.0, 0.0)
        dk = _gather_heads(dkc_ref[...] + has_next * dkp_ref[...])
        dv = _gather_heads(dvc_ref[...] + has_next * dvp_ref[...])
        dxq, dqg = _norm_rope_bwd(dq_ref[...], q_ref[...], _tile_lanes(qg_ref[...], AW),
                                  _tile_lanes(c, AW), _tile_lanes(s, AW))
        dxk, dkg = _norm_rope_bwd(dk, k_ref[...], _tile_lanes(kg_ref[...], KW),
                                  _tile_lanes(c, KW), _tile_lanes(s, KW))
        o_ref[...] = jnp.concatenate([dxq, dxk, dv], axis=-1).astype(MXU)

        @pl.when(n == 0)
        def _():
            dqg_ref[...] = jnp.zeros(dqg_ref.shape, F32)
            dkg_ref[...] = jnp.zeros(dkg_ref.shape, F32)

        dqg_ref[...] += _fold_heads(dqg)
        dkg_ref[...] += _fold_heads(dkg)

    cur = lambda n: (n, 0)
    nxt = lambda n: (jnp.minimum(n + 1, nb - 1), 0)
    vec = pl.BlockSpec((1, BLOCK), lambda n: (0, 0))
    tab = pl.BlockSpec((BLOCK, BLOCK), cur)
    return pl.pallas_call(
        body, name="qk_prep_bwd", grid=(nb,),
        in_specs=[pl.BlockSpec((BLOCK, AW), cur), pl.BlockSpec((BLOCK, KW), lambda n: (n, AW // KW)),
                  vec, vec, tab, tab, pl.BlockSpec((BLOCK, AW), cur),
                  pl.BlockSpec((BLOCK, 2 * KW), nxt), pl.BlockSpec((BLOCK, 2 * KW), cur),
                  pl.BlockSpec((BLOCK, 2 * KW), nxt), pl.BlockSpec((BLOCK, 2 * KW), cur)],
        out_specs=[pl.BlockSpec((BLOCK, AW + 2 * KW), cur), vec, vec],
        out_shape=[jax.ShapeDtypeStruct((S, AW + 2 * KW), MXU), jax.ShapeDtypeStruct((1, BLOCK), F32),
                   jax.ShapeDtypeStruct((1, BLOCK), F32)],
        compiler_params=_params())(proj, proj, qg, kg, cos, sin, dq, dkp, dkc, dvp, dvc)


SGU_LANES = 512
SGU_ROWS = 256


def _sgu_group(v, lng, lnb, w_f32, b):
    rows = v.shape[0]
    mu = jnp.mean(v, axis=-1, keepdims=True)
    vc = v - mu
    r = lax.rsqrt(jnp.mean(vc * vc, axis=-1, keepdims=True) + EPS)
    xh = vc * r
    vn = (xh * lng + lnb).astype(MXU)
    row = lax.broadcasted_iota(jnp.int32, (BLOCK, BLOCK), 0)
    col = lax.broadcasted_iota(jnp.int32, (BLOCK, BLOCK), 1)
    tri = row >= col
    w = jnp.where(tri, w_f32, 0.0).astype(MXU)
    chunks = [jnp.dot(w, vn[k * BLOCK:(k + 1) * BLOCK], preferred_element_type=F32) + b for k in range(rows // BLOCK)]
    s = chunks[0] if len(chunks) == 1 else jnp.concatenate(chunks, axis=0)
    return xh, r, vn, w, s, tri


def _sgu_layout(S, u_col):
    SW = SGU_GROUPS * BLOCK
    lb, tr = min(SGU_LANES, SW), min(SGU_ROWS, S)
    assert u_col % lb == 0 and SW % lb == 0 and S % tr == 0
    ub, nlb, gpb = u_col // lb, SW // lb, lb // BLOCK
    specs = [pl.BlockSpec((tr, lb), lambda j, i: (i, ub + j)), pl.BlockSpec((tr, lb), lambda j, i: (i, ub + nlb + j)),
             pl.BlockSpec((1, lb), lambda j, i: (0, j)), pl.BlockSpec((1, lb), lambda j, i: (0, j)),
             pl.BlockSpec((gpb, BLOCK, BLOCK), lambda j, i: (j, 0, 0)),
             pl.BlockSpec((gpb, BLOCK, 1), lambda j, i: (j, 0, 0))]
    return lb, tr, gpb, nlb, specs


def _sgu_fwd(proj, lng, lnb, ws, bs, u_col):
    S = proj.shape[0]
    lb, tr, gpb, nlb, specs = _sgu_layout(S, u_col)

    def body(pu_ref, pv_ref, lng_ref, lnb_ref, w_ref, b_ref, o_ref):
        u = _gelu(pu_ref[...])
        v = _gelu(pv_ref[...])
        outs = []
        for g in range(gpb):
            sl = slice(g * BLOCK, (g + 1) * BLOCK)
            s = _sgu_group(v[:, sl], lng_ref[:, sl], lnb_ref[:, sl], w_ref[g], b_ref[g])[4]
            outs.append(u[:, sl] * s)
        o_ref[...] = (outs[0] if gpb == 1 else jnp.concatenate(outs, axis=-1)).astype(MXU)

    return pl.pallas_call(
        body, name="sgu_fwd", grid=(nlb, S // tr), in_specs=specs,
        out_specs=pl.BlockSpec((tr, lb), lambda j, i: (i, j)),
        out_shape=jax.ShapeDtypeStruct((S, nlb * lb), MXU), compiler_params=_params())(proj, proj, lng, lnb, ws, bs)


def _sgu_bwd(proj, lng, lnb, ws, bs, dsgu, u_col, after):
    S = proj.shape[0]
    G = SGU_GROUPS
    lb, tr, gpb, nlb, specs = _sgu_layout(S, u_col)
    nch = tr // BLOCK

    def body(pu_ref, pv_ref, lng_ref, lnb_ref, w_ref, b_ref, do_ref, after_ref,
             dpu_ref, dpv_ref, dw_ref, db_ref, dlng_ref, dlnb_ref):
        pu, pv, do = pu_ref[...], pv_ref[...], do_ref[...]
        u = _gelu(pu)
        v = _gelu(pv)

        @pl.when(pl.program_id(1) == 0)
        def _():
            dw_ref[...] = jnp.zeros(dw_ref.shape, F32)
            db_ref[...] = jnp.zeros(db_ref.shape, F32)
            dlng_ref[...] = jnp.zeros(dlng_ref.shape, F32)
            dlnb_ref[...] = jnp.zeros(dlnb_ref.shape, F32)

        ss, dvs, dlng, dlnb = [], [], [], []
        for g in range(gpb):
            sl = slice(g * BLOCK, (g + 1) * BLOCK)
            xh, r, vn, w, s, tri = _sgu_group(v[:, sl], lng_ref[:, sl], lnb_ref[:, sl], w_ref[g], b_ref[g])
            ds = do[:, sl] * u[:, sl]
            dsb = ds.astype(MXU)
            dw, db, dvn = None, None, []
            for k in range(nch):
                rows = slice(k * BLOCK, (k + 1) * BLOCK)
                part = lax.dot_general(dsb[rows], vn[rows], NT, preferred_element_type=F32)
                dw = part if dw is None else dw + part
                rowsum = jnp.sum(ds[rows], axis=-1, keepdims=True)
                db = rowsum if db is None else db + rowsum
                dvn.append(lax.dot_general(w, dsb[rows], TN, preferred_element_type=F32))
            dvn = dvn[0] if nch == 1 else jnp.concatenate(dvn, axis=0)
            dw_ref[g] += jnp.where(tri, dw, 0.0)
            db_ref[g] += db
            dxh = dvn * lng_ref[:, sl]
            dvs.append(r * (dxh - jnp.mean(dxh, axis=-1, keepdims=True)
                            - xh * jnp.mean(dxh * xh, axis=-1, keepdims=True)))
            dlng.append(jnp.sum(dvn * xh, axis=0, keepdims=True))
            dlnb.append(jnp.sum(dvn, axis=0, keepdims=True))
            ss.append(s)
        cat = lambda parts: parts[0] if gpb == 1 else jnp.concatenate(parts, axis=-1)
        dpu_ref[...] = (do * cat(ss) * _gelu_grad(pu)).astype(MXU)
        dpv_ref[...] = (cat(dvs) * _gelu_grad(pv)).astype(MXU)
        dlng_ref[...] += cat(dlng)
        dlnb_ref[...] += cat(dlnb)

    tile = pl.BlockSpec((tr, lb), lambda j, i: (i, j))
    vec = pl.BlockSpec((1, lb), lambda j, i: (0, j))
    half = jax.ShapeDtypeStruct((S, G * BLOCK), MXU)
    return pl.pallas_call(
        body, name="sgu_bwd", grid=(nlb, S // tr), in_specs=specs + [tile, ANY],
        out_specs=[tile, tile, pl.BlockSpec((gpb, BLOCK, BLOCK), lambda j, i: (j, 0, 0)),
                   pl.BlockSpec((gpb, BLOCK, 1), lambda j, i: (j, 0, 0)), vec, vec],
        out_shape=[half, half, jax.ShapeDtypeStruct((G, BLOCK, BLOCK), F32),
                   jax.ShapeDtypeStruct((G, BLOCK, 1), F32),
                   jax.ShapeDtypeStruct((1, G * BLOCK), F32), jax.ShapeDtypeStruct((1, G * BLOCK), F32)],
        compiler_params=_params())(proj, proj, lng, lnb, ws, bs, dsgu, after)


def _store_f32(vals, extra, outs):
    for v, o in zip(vals, outs):
        o[...] = v


def _store_mxu(vals, extra, outs):
    for v, o in zip(vals, outs):
        o[...] = v.astype(MXU)


def _proj_in(h, w):
    S, D = h.shape
    Ns = w.shape[2]
    tm, tn = _tile(S, 1024), _tile(Ns, 1024)
    npb = Ns // tn
    return _mm("proj_in", (S // tm, N_CHIPS, npb), 0, [h, w],
               [pl.BlockSpec((tm, D), lambda i, s, j: (i, 0)), pl.BlockSpec((None, D, tn), lambda i, s, j: (s, 0, j))],
               [(0, 1, 0)], NN, 0, [jax.ShapeDtypeStruct((S, N_CHIPS * Ns), F32)],
               [pl.BlockSpec((tm, tn), lambda i, s, j: (i, s * npb + j))], [None], _store_f32)[0]


def _branches(attn, sgu, wa, ws, proj, gate0):
    S, AW = attn.shape
    SW = sgu.shape[1]
    Nb = wa.shape[2]
    D = N_CHIPS * Nb
    tm = _tile(S, 512)
    assert gate0 % Nb == 0
    ga, gb = gate0 // Nb, (gate0 + D) // Nb

    def epilogue(vals, extra, outs):
        a, b = vals
        outs[0][...] = (_sigmoid(extra[0][...]) * a + _sigmoid(extra[1][...]) * b).astype(MXU)
        outs[1][...] = a
        outs[2][...] = b

    tile = pl.BlockSpec((tm, Nb), lambda i, s: (i, s))
    wspec = lambda k: pl.BlockSpec((None, k, Nb), lambda i, s: (s, 0, 0))
    f = jax.ShapeDtypeStruct((S, D), F32)
    return _mm("branches", (S // tm, N_CHIPS), 0, [attn, sgu, wa, ws, proj, proj],
               [pl.BlockSpec((tm, AW), lambda i, s: (i, 0)), pl.BlockSpec((tm, SW), lambda i, s: (i, 0)),
                wspec(AW), wspec(SW), pl.BlockSpec((tm, Nb), lambda i, s: (i, ga + s)),
                pl.BlockSpec((tm, Nb), lambda i, s: (i, gb + s))],
               [(0, 2, 0), (1, 3, 1)], NN, 2, [jax.ShapeDtypeStruct((S, D), MXU), f, f], [tile] * 3,
               [None, None], epilogue, chunk=MXU_CHUNK)


def _rows_mm(name, a, w, res):
    S = a.shape[0]
    _, K, N = w.shape
    tm, tn = _tile(S, 1024), _tile(N, 1024)

    def epilogue(vals, extra, outs):
        outs[0][...] = extra[0][...] + vals[0]

    out = pl.BlockSpec((tm, tn), lambda i, j, s: (i, j))
    return _mm(name, (S // tm, N // tn, N_CHIPS), 1, [a, w, res],
               [pl.BlockSpec((tm, K), lambda i, j, s: (i, s)), pl.BlockSpec((None, K, tn), lambda i, j, s: (s, 0, j)), out],
               [(0, 1, 0)], NN, 1, [jax.ShapeDtypeStruct((S, N), F32)], [out], [(tm, tn)], epilogue)[0]


def _gate_up(h2, wg, wu):
    S, D = h2.shape
    Nf = wg.shape[2]
    tm = _tile(S, 256)

    def epilogue(vals, extra, outs):
        g, u = vals
        outs[0][...] = g
        outs[1][...] = u
        outs[2][...] = (g * _sigmoid(g) * u).astype(MXU)

    w = pl.BlockSpec((None, D, Nf), lambda s, i: (s, 0, 0))
    o = pl.BlockSpec((tm, Nf), lambda s, i: (i, s))
    f = jax.ShapeDtypeStruct((S, N_CHIPS * Nf), F32)
    return _mm("gate_up", (N_CHIPS, S // tm), 0, [h2, wg, wu],
               [pl.BlockSpec((tm, D), lambda s, i: (i, 0)), w, w], [(0, 1, 0), (0, 2, 1)], NN, 0,
               [f, f, jax.ShapeDtypeStruct((S, N_CHIPS * Nf), MXU)], [o, o, o], [None, None], epilogue,
               chunk=MXU_CHUNK)


def _down_bwd(dyb, wd, g, u):
    S, D = dyb.shape
    Kf = wd.shape[1]
    tm = _tile(S, 512)

    def epilogue(vals, extra, outs):
        da, gv, uv = vals[0], extra[0][...], extra[1][...]
        sg = _sigmoid(gv)
        outs[0][...] = (da * uv * sg * (1.0 + gv * (1.0 - sg))).astype(MXU)
        outs[1][...] = (da * gv * sg).astype(MXU)

    t = pl.BlockSpec((tm, Kf), lambda i, s: (i, s))
    o = jax.ShapeDtypeStruct((S, N_CHIPS * Kf), MXU)
    return _mm("down_bwd", (S // tm, N_CHIPS), 0, [dyb, wd, g, u],
               [pl.BlockSpec((tm, D), lambda i, s: (i, 0)), pl.BlockSpec((None, Kf, D), lambda i, s: (s, 0, 0)), t, t],
               [(0, 1, 0)], NT, 2, [o, o], [t, t], [None], epilogue, chunk=MXU_CHUNK)


def _out_bwd(dxb, wo, proj, ba, bb, gate0):
    S, D = dxb.shape
    Ko = wo.shape[1]
    tm = _tile(S, 512)
    assert gate0 % Ko == 0
    ga, gb = gate0 // Ko, (gate0 + D) // Ko

    def epilogue(vals, extra, outs):
        dm = vals[0]
        sa, sb = _sigmoid(extra[0][...]), _sigmoid(extra[1][...])
        outs[0][...] = (dm * sa).astype(MXU)
        outs[1][...] = (dm * sb).astype(MXU)
        outs[2][...] = (dm * extra[2][...] * sa * (1.0 - sa)).astype(MXU)
        outs[3][...] = (dm * extra[3][...] * sb * (1.0 - sb)).astype(MXU)

    t = pl.BlockSpec((tm, Ko), lambda i, s: (i, s))
    o = jax.ShapeDtypeStruct((S, D), MXU)
    return _mm("out_bwd", (S // tm, N_CHIPS), 0, [dxb, wo, proj, proj, ba, bb],
               [pl.BlockSpec((tm, D), lambda i, s: (i, 0)), pl.BlockSpec((None, Ko, D), lambda i, s: (s, 0, 0)),
                pl.BlockSpec((tm, Ko), lambda i, s: (i, ga + s)), pl.BlockSpec((tm, Ko), lambda i, s: (i, gb + s)), t, t],
               [(0, 1, 0)], NT, 4, [o] * 4, [t] * 4, [None], epilogue, chunk=MXU_CHUNK)


def _dx_cols(name, terms, n_out, after=None):
    S = terms[0][0].shape[0]
    _, K, Ns = terms[0][1].shape
    tm, tko, tn = _tile(S, 1024), _tile(K, 1024), _tile(Ns, 1920 if len(terms) == 1 else 1408)
    npb = Ns // tn
    operands, specs, pairs = [], [], []
    for t, (dy, w, k) in enumerate(terms):
        assert w.shape == (N_CHIPS, K, Ns)
        operands += [dy, w]
        specs += [pl.BlockSpec((tm, tn), lambda i, jk, s, jn: (i, s * npb + jn)),
                  pl.BlockSpec((None, tko, tn), lambda i, jk, s, jn: (s, jk, jn))]
        pairs.append((2 * t, 2 * t + 1, k))
    out = pl.BlockSpec((tm, tko), lambda i, jk, s, jn: (i, jk))
    return _mm(name, (S // tm, K // tko, N_CHIPS, npb), 2, operands, specs, pairs, NT, 0,
               [jax.ShapeDtypeStruct((S, K), F32)] * n_out, [out] * n_out, [(tm, tko)] * n_out, _store_f32, after)


def _dw_cols(name, a, dy):
    S, K = a.shape
    Ns = dy.shape[1] // N_CHIPS
    tk, tn = _tile(K, 512), _tile(Ns, 1408)
    npb = Ns // tn
    return _mm(name, (K // tk, N_CHIPS, npb), 0, [a, dy],
               [pl.BlockSpec((S, tk), lambda jk, s, jn: (0, jk)), pl.BlockSpec((S, tn), lambda jk, s, jn: (0, s * npb + jn))],
               [(0, 1, 0)], TN, 0, [jax.ShapeDtypeStruct((N_CHIPS, K, Ns), MXU)],
               [pl.BlockSpec((None, tk, tn), lambda jk, s, jn: (s, jk, jn))], [None], _store_mxu)[0]


def _dw_rows(name, a, dy):
    S = a.shape[0]
    K = a.shape[1] // N_CHIPS
    N = dy.shape[1]
    tk, tn = _tile(K, 1408), _tile(N, 1024)
    nkb = K // tk
    return _mm(name, (N_CHIPS, nkb, N // tn), 0, [a, dy],
               [pl.BlockSpec((S, tk), lambda s, jk, jn: (0, s * nkb + jk)), pl.BlockSpec((S, tn), lambda s, jk, jn: (0, jn))],
               [(0, 1, 0)], TN, 0, [jax.ShapeDtypeStruct((N_CHIPS, K, N), MXU)],
               [pl.BlockSpec((None, tk, tn), lambda s, jk, jn: (s, jk, jn))], [None], _store_mxu)[0]


def _layer_fwd(x, stream, layer, last, sp, cos, sin, dims):
    AW, KW, gate0, u_col = dims
    h = _rms_fwd("mix_norm", x, sp["mix_norm"])
    w = stream.finish(layer, 0, h)
    proj = _proj_in(h, w["w_in"])
    qr, kr, vb = _qk_prep(proj, sp["q_norm"], sp["k_norm"], cos, sin, AW, KW)
    stream.forward(layer, 1, qr)
    attn = _attn_fwd(qr, kr, vb, sp["sinks"])
    w.update(stream.finish(layer, 1, attn))
    sgu = _sgu_fwd(proj, sp["sgu_ln_g"], sp["sgu_ln_b"], sp["w_spatial"], sp["b_spatial"], u_col)
    merged, ba, bb = _branches(attn, sgu, w["w_attn_branch"], w["w_sgu_branch"], proj, gate0)
    stream.forward(layer, 2, merged)
    x1 = _rows_mm("out_proj", merged, w["w_out"], x)
    w.update(stream.finish(layer, 2, x1))
    h2 = _rms_fwd("ffn_norm", x1, sp["ffn_norm"])
    stream.forward(layer, 3, h2)
    g, u, act = _gate_up(h2, w["w_gate"], w["w_up"])
    w.update(stream.finish(layer, 3, g))
    x2 = _rows_mm("down_proj", act, w["w_down"], x1)
    if not last:
        stream.forward(layer + 1, 0, x2)
    saved = dict(x=x, h=h, proj=proj, qr=qr, kr=kr, vb=vb, attn=attn, sgu=sgu, merged=merged, ba=ba, bb=bb,
                 x1=x1, h2=h2, g=g, u=u, act=act)
    return x2, saved, w


def _layer_bwd(dy, dyb, w, sp, sv, cos, sin, dims, reducer, layer):
    AW, KW, gate0, u_col = dims
    big, small = {}, {}
    dg, du = _down_bwd(dyb, w["w_down"], sv["g"], sv["u"])
    big["w_down"] = _dw_rows("dw_down", sv["act"], dyb)
    big["w_gate"] = _dw_cols("dw_gate", sv["h2"], dg)
    big["w_up"] = _dw_cols("dw_up", sv["h2"], du)
    token = reducer.start(layer, 2, big)
    dh2 = _dx_cols("dh2", [(dg, w["w_gate"], 0), (du, w["w_up"], 0)], 1, token)[0]
    token = reducer.scatter(layer, 2, dh2)
    dx1, dx1b, small["ffn_norm"] = _rms_bwd("ffn_norm_bwd", dh2, sv["x1"], sp["ffn_norm"], dy, token)
    dba, dbb, dgla, dglb = _out_bwd(dx1b, w["w_out"], sv["proj"], sv["ba"], sv["bb"], gate0)
    big["w_out"] = _dw_rows("dw_out", sv["merged"], dx1b)
    big["w_attn_branch"] = _dw_cols("dw_attn_branch", sv["attn"], dba)
    big["w_sgu_branch"] = _dw_cols("dw_sgu_branch", sv["sgu"], dbb)
    token = reducer.start(layer, 1, big)
    dattn, dsgu = _dx_cols("dbranch_in", [(dba, w["w_attn_branch"], 0), (dbb, w["w_sgu_branch"], 1)], 2, token)
    token = reducer.scatter(layer, 1, dsgu)
    dpu, dpv, small["w_spatial"], db, small["sgu_ln_g"], small["sgu_ln_b"] = _sgu_bwd(
        sv["proj"], sp["sgu_ln_g"], sp["sgu_ln_b"], sp["w_spatial"], sp["b_spatial"], dsgu, u_col, token)
    small["b_spatial"] = db[:, :, 0]
    dq, dkp, dkc, dvp, dvc, dsink = _attn_bwd(sv["qr"], sv["kr"], sv["vb"], sp["sinks"], dattn)
    small["sinks"] = dsink[:, :sp["sinks"].shape[1]]
    dqkv, dqg, dkg = _qk_prep_bwd(sv["proj"], sp["q_norm"], sp["k_norm"], cos, sin, dq, dkp, dkc, dvp, dvc, AW, KW)
    small["q_norm"] = dqg[:, :HEAD_DIM]
    small["k_norm"] = dkg[:, :HEAD_DIM]
    dproj = _join_columns("dproj", [dqkv, dpu, dpv, dgla, dglb])
    big["w_in"] = _dw_cols("dw_in", sv["h"], dproj)
    token = reducer.start(layer, 0, big)
    dh = _dx_cols("dh", [(dproj, w["w_in"], 0)], 1, token)[0]
    token = reducer.scatter(layer, 0, dh)
    dx, dxb, small["mix_norm"] = _rms_bwd("mix_norm_bwd", dh, sv["x"], sp["mix_norm"], dx1, token)
    return dx, dxb, small


def _place():
    x, y, c = lax.axis_index("x"), lax.axis_index("y"), lax.axis_index("c")
    chips = [(1 - x, y), (x, 1 - y), (1 - x, 1 - y)]
    return x, y, c, chips


def _half_rows(c, rows):
    h = rows // 2
    assert h % 16 == 0
    return pl.ds(pl.multiple_of(c * h, 16), h)


def _row_tile(rows, pref):
    best = None
    for t in range(16, min(rows, pref) + 1, 16):
        if rows % t == 0:
            best = t
    assert best is not None, rows
    return best


def _cast_own(name, chip, w, layer):
    _, R, C = w.shape
    tr = _row_tile(R, 512)

    def body(chip_ref, w_ref, o_ref):
        o_ref[...] = w_ref[...].astype(MXU)

    return pl.pallas_call(
        body, name=name, out_shape=jax.ShapeDtypeStruct((N_CHIPS, R, C), MXU),
        grid_spec=pltpu.PrefetchScalarGridSpec(
            num_scalar_prefetch=1, grid=(R // tr,),
            in_specs=[pl.BlockSpec((None, tr, C), lambda i, chip_ref: (layer, i, 0))],
            out_specs=pl.BlockSpec((None, tr, C), lambda i, chip_ref: (chip_ref[0], i, 0))),
        compiler_params=_params())(chip, w)


HBM = pl.BlockSpec(memory_space=pltpu.HBM)
SEM = pl.BlockSpec(memory_space=pltpu.SEMAPHORE)
DATAFLOW = pltpu.SideEffectType.DATAFLOW_SIDE_EFFECTING


def _gather_copies(bufs, send_sem, recv_sem):
    x, y, c, chips = _place()

    def ici(a, j, block):
        px, py = chips[j]
        blk = bufs[a].at[block, _half_rows(c, bufs[a].shape[1])]
        return pltpu.make_async_remote_copy(
            src_ref=blk, dst_ref=blk, send_sem=send_sem.at[3 * a + j], recv_sem=recv_sem.at[3 * a + j],
            device_id=(px, py, c), device_id_type=MESH)

    def d2d(a, j, core):
        px, py = chips[j]
        blk = bufs[a].at[2 * px + py, _half_rows(core, bufs[a].shape[1])]
        return pltpu.make_async_remote_copy(
            src_ref=blk, dst_ref=blk, send_sem=send_sem.at[3 * a + j], recv_sem=recv_sem.at[3 * a + j],
            device_id=(x, y, 1 - c), device_id_type=MESH)

    return ici, d2d


def _in_hbm(bufs):
    return [pltpu.with_memory_space_constraint(b, pltpu.HBM) for b in bufs]


def _gather_start(name, bufs, after):
    n = len(bufs)

    def body(*refs):
        dst = refs[n + 1:2 * n + 1]
        send_sem, recv_sem, token = refs[2 * n + 1:]
        x, y, c, chips = _place()
        ici, _ = _gather_copies(dst, send_sem, recv_sem)
        for a in range(n):
            for j in range(3):
                ici(a, j, 2 * x + y).start()
        token[...] = jnp.zeros(token.shape, token.dtype)

    sems = pltpu.SemaphoreType.DMA((3 * n,))
    outs = pl.pallas_call(
        body, name=name, in_specs=[HBM] * n + [ANY],
        out_specs=[HBM] * n + [SEM, SEM, pl.BlockSpec(memory_space=pltpu.VMEM)],
        out_shape=[pltpu.HBM(b.shape, b.dtype) for b in bufs] + [sems, sems, jax.ShapeDtypeStruct((8, BLOCK), F32)],
        input_output_aliases={a: a for a in range(n)},
        compiler_params=pltpu.CompilerParams(has_side_effects=DATAFLOW))(*_in_hbm(bufs), after)
    return outs[:n], outs[n], outs[n + 1], outs[n + 2]


def _gather_forward(name, bufs, ici_send, ici_recv, after):
    n = len(bufs)

    def body(*refs):
        ici_send_ref, ici_recv_ref = refs[n], refs[n + 1]
        dst = refs[n + 3:2 * n + 3]
        d2d_send, d2d_recv = refs[2 * n + 3:]
        x, y, c, chips = _place()
        ici, _ = _gather_copies(dst, ici_send_ref, ici_recv_ref)
        _, d2d = _gather_copies(dst, d2d_send, d2d_recv)
        for a in range(n):
            for j, (px, py) in enumerate(chips):
                ici(a, j, 2 * px + py).wait_recv()
                d2d(a, j, c).start()
        for a in range(n):
            for j in range(3):
                ici(a, j, 2 * x + y).wait_send()

    sems = pltpu.SemaphoreType.DMA((3 * n,))
    outs = pl.pallas_call(
        body, name=name, in_specs=[HBM] * n + [SEM, SEM, ANY], out_specs=[HBM] * n + [SEM, SEM],
        out_shape=[pltpu.HBM(b.shape, b.dtype) for b in bufs] + [sems, sems],
        input_output_aliases={a: a for a in range(n)},
        compiler_params=pltpu.CompilerParams(has_side_effects=DATAFLOW))(*bufs, ici_send, ici_recv, after)
    return outs[:n], outs[n], outs[n + 1]


def _gather_finish(name, bufs, d2d_send, d2d_recv, after):
    n = len(bufs)

    def body(*refs):
        send_ref, recv_ref = refs[n], refs[n + 1]
        dst = refs[n + 3:]
        x, y, c, chips = _place()
        _, d2d = _gather_copies(dst, send_ref, recv_ref)
        for a in range(n):
            for j in range(3):
                d2d(a, j, 1 - c).wait_recv()
                d2d(a, j, c).wait_send()

    return pl.pallas_call(
        body, name=name, in_specs=[HBM] * n + [SEM, SEM, ANY], out_specs=[HBM] * n,
        out_shape=[pltpu.HBM(b.shape, b.dtype) for b in bufs],
        input_output_aliases={a: a for a in range(n)},
        compiler_params=pltpu.CompilerParams(has_side_effects=DATAFLOW))(*bufs, d2d_send, d2d_recv, after)


GATHER = (("w_in",), ("w_attn_branch", "w_sgu_branch", "w_out"), ("w_gate", "w_up"), ("w_down",))
REDUCE = (("w_in",), ("w_attn_branch", "w_sgu_branch", "w_out"), ("w_gate", "w_up", "w_down"))


class _WeightStream:
    def __init__(self, started):
        self.started, self.passed = started, {}

    def forward(self, layer, group, after):
        bufs, send, recv = self.started[(layer, group)]
        self.passed[(layer, group)] = _gather_forward("gather_forward_%d_%d" % (layer, group), bufs, send, recv, after)

    def finish(self, layer, group, after):
        bufs, send, recv = self.passed[(layer, group)]
        done = _gather_finish("gather_finish_%d_%d" % (layer, group), bufs, send, recv, after)
        return dict(zip(GATHER[group], done))


def _pair_copies(grads, lands, send_sem, recv_sem):
    x, y, c, _ = _place()

    def make(a):
        theirs = _half_rows(1 - c, grads[a].shape[1])
        return pltpu.make_async_remote_copy(
            src_ref=grads[a].at[:, theirs], dst_ref=lands[a], send_sem=send_sem.at[a], recv_sem=recv_sem.at[a],
            device_id=(x, y, 1 - c), device_id_type=MESH)

    return make


def _pair_start(name, grads, after):
    n = len(grads)
    lands = [lax.empty((g.shape[0], g.shape[1] // 2, g.shape[2]), g.dtype) for g in grads]

    def body(*refs):
        src, dst = refs[2 * n + 1:3 * n + 1], refs[3 * n + 1:4 * n + 1]
        send_sem, recv_sem, token = refs[4 * n + 1:]
        copy = _pair_copies(src, dst, send_sem, recv_sem)
        for a in range(n):
            copy(a).start()
        token[...] = jnp.zeros(token.shape, token.dtype)

    sems = pltpu.SemaphoreType.DMA((n,))
    outs = pl.pallas_call(
        body, name=name, in_specs=[HBM] * (2 * n) + [ANY],
        out_specs=[HBM] * (2 * n) + [SEM, SEM, pl.BlockSpec(memory_space=pltpu.VMEM)],
        out_shape=[pltpu.HBM(b.shape, b.dtype) for b in grads + lands] + [sems, sems, jax.ShapeDtypeStruct((8, BLOCK), F32)],
        input_output_aliases={a: a for a in range(2 * n)},
        compiler_params=pltpu.CompilerParams(has_side_effects=DATAFLOW))(*_in_hbm(grads + lands), after)
    return outs[:n], outs[n:2 * n], outs[2 * n], outs[2 * n + 1], outs[2 * n + 2]


def _pair_finish(name, grads, lands, send_sem, recv_sem, after):
    n = len(grads)

    def body(*refs):
        send_ref, recv_ref = refs[2 * n], refs[2 * n + 1]
        src, dst = refs[2 * n + 3:3 * n + 3], refs[3 * n + 3:]
        copy = _pair_copies(src, dst, send_ref, recv_ref)
        for a in range(n):
            copy(a).wait_send()
            copy(a).wait_recv()

    outs = pl.pallas_call(
        body, name=name, in_specs=[HBM] * (2 * n) + [SEM, SEM, ANY], out_specs=[HBM] * (2 * n),
        out_shape=[pltpu.HBM(b.shape, b.dtype) for b in grads + lands],
        input_output_aliases={a: a for a in range(2 * n)},
        compiler_params=pltpu.CompilerParams(has_side_effects=DATAFLOW))(*grads, *lands, send_sem, recv_sem, after)
    return outs[:n], outs[n:]


def _pair_sum(name, core, g, p):
    _, h, C = p.shape
    tr = _row_tile(h, 512)
    nrb = h // tr

    def body(core_ref, g_ref, p_ref, o_ref):
        o_ref[...] = (g_ref[...].astype(F32) + p_ref[...].astype(F32)).astype(o_ref.dtype)

    spec = pl.BlockSpec((None, tr, C), lambda s, i, core_ref: (s, i, 0))
    return pl.pallas_call(
        body, name=name, out_shape=jax.ShapeDtypeStruct(p.shape, p.dtype),
        grid_spec=pltpu.PrefetchScalarGridSpec(
            num_scalar_prefetch=1, grid=(N_CHIPS, nrb),
            in_specs=[pl.BlockSpec((None, tr, C), lambda s, i, core_ref: (s, core_ref[0] * nrb + i, 0)), spec],
            out_specs=spec),
        compiler_params=_params())(core, g, p)


def _scatter_copies(sums, slots, send_sem, recv_sem):
    x, y, c, chips = _place()

    def make(a, j):
        px, py = chips[j]
        return pltpu.make_async_remote_copy(
            src_ref=sums[a].at[2 * px + py], dst_ref=slots[a].at[j], send_sem=send_sem.at[3 * a + j],
            recv_sem=recv_sem.at[3 * a + j], device_id=(px, py, c), device_id_type=MESH)

    return make


def _scatter_start(name, sums, after):
    n = len(sums)
    slots = [lax.empty((3,) + s.shape[1:], s.dtype) for s in sums]

    def body(*refs):
        src, dst = refs[2 * n + 1:3 * n + 1], refs[3 * n + 1:4 * n + 1]
        send_sem, recv_sem, token = refs[4 * n + 1:]
        copy = _scatter_copies(src, dst, send_sem, recv_sem)
        for a in range(n):
            for j in range(3):
                copy(a, j).start()
        token[...] = jnp.zeros(token.shape, token.dtype)

    sems = pltpu.SemaphoreType.DMA((3 * n,))
    outs = pl.pallas_call(
        body, name=name, in_specs=[HBM] * (2 * n) + [ANY],
        out_specs=[HBM] * (2 * n) + [SEM, SEM, pl.BlockSpec(memory_space=pltpu.VMEM)],
        out_shape=[pltpu.HBM(b.shape, b.dtype) for b in sums + slots] + [sems, sems, jax.ShapeDtypeStruct((8, BLOCK), F32)],
        input_output_aliases={a: a for a in range(2 * n)},
        compiler_params=pltpu.CompilerParams(has_side_effects=DATAFLOW))(*_in_hbm(sums + slots), after)
    return outs[:n], outs[n:2 * n], outs[2 * n], outs[2 * n + 1], outs[2 * n + 2]


def _scatter_finish(name, sums, slots, send_sem, recv_sem, after):
    n = len(sums)

    def body(*refs):
        send_ref, recv_ref = refs[2 * n], refs[2 * n + 1]
        src, dst = refs[2 * n + 3:3 * n + 3], refs[3 * n + 3:]
        copy = _scatter_copies(src, dst, send_ref, recv_ref)
        for a in range(n):
            for j in range(3):
                copy(a, j).wait_send()
                copy(a, j).wait_recv()

    outs = pl.pallas_call(
        body, name=name, in_specs=[HBM] * (2 * n) + [SEM, SEM, ANY], out_specs=[HBM] * (2 * n),
        out_shape=[pltpu.HBM(b.shape, b.dtype) for b in sums + slots],
        input_output_aliases={a: a for a in range(2 * n)},
        compiler_params=pltpu.CompilerParams(has_side_effects=DATAFLOW))(*sums, *slots, send_sem, recv_sem, after)
    return outs[:n], outs[n:]


def _slot_sum(name, place, slots, sums):
    _, h, C = slots.shape
    tr = _row_tile(h, 512)
    nrb = h // tr

    def body(place_ref, r0, r1, r2, own, o_ref):
        o_ref[...] = ((r0[...].astype(F32) + r1[...].astype(F32)) + r2[...].astype(F32)) + own[...].astype(F32)

    slot = lambda k: pl.BlockSpec((None, tr, C), lambda i, place_ref: (k, i, 0))
    return pl.pallas_call(
        body, name=name, out_shape=jax.ShapeDtypeStruct((2 * h, C), F32),
        grid_spec=pltpu.PrefetchScalarGridSpec(
            num_scalar_prefetch=1, grid=(nrb,),
            in_specs=[slot(0), slot(1), slot(2),
                      pl.BlockSpec((None, tr, C), lambda i, place_ref: (place_ref[0], i, 0))],
            out_specs=pl.BlockSpec((tr, C), lambda i, place_ref: (place_ref[1] * nrb + i, 0))),
        compiler_params=_params())(place, slots, slots, slots, sums)


def _half_copies(bufs, send_sem, recv_sem):
    x, y, c, _ = _place()

    def make(a, core):
        rows = bufs[a].at[_half_rows(core, bufs[a].shape[0])]
        return pltpu.make_async_remote_copy(
            src_ref=rows, dst_ref=rows, send_sem=send_sem.at[a], recv_sem=recv_sem.at[a],
            device_id=(x, y, 1 - c), device_id_type=MESH)

    return make


def _half_start(name, bufs, after):
    n = len(bufs)

    def body(*refs):
        dst = refs[n + 1:2 * n + 1]
        send_sem, recv_sem, token = refs[2 * n + 1:]
        c = lax.axis_index("c")
        copy = _half_copies(dst, send_sem, recv_sem)
        for a in range(n):
            copy(a, c).start()
        token[...] = jnp.zeros(token.shape, token.dtype)

    sems = pltpu.SemaphoreType.DMA((n,))
    outs = pl.pallas_call(
        body, name=name, in_specs=[HBM] * n + [ANY],
        out_specs=[HBM] * n + [SEM, SEM, pl.BlockSpec(memory_space=pltpu.VMEM)],
        out_shape=[pltpu.HBM(b.shape, b.dtype) for b in bufs] + [sems, sems, jax.ShapeDtypeStruct((8, BLOCK), F32)],
        input_output_aliases={a: a for a in range(n)},
        compiler_params=pltpu.CompilerParams(has_side_effects=DATAFLOW))(*_in_hbm(bufs), after)
    return outs[:n], outs[n], outs[n + 1], outs[n + 2]


def _half_finish(name, bufs, send_sem, recv_sem, after):
    n = len(bufs)

    def body(*refs):
        send_ref, recv_ref = refs[n], refs[n + 1]
        dst = refs[n + 3:]
        c = lax.axis_index("c")
        copy = _half_copies(dst, send_ref, recv_ref)
        for a in range(n):
            copy(a, c).wait_send()
            copy(a, 1 - c).wait_recv()

    return pl.pallas_call(
        body, name=name, in_specs=[HBM] * n + [SEM, SEM, ANY], out_specs=[HBM] * n,
        out_shape=[pltpu.HBM(b.shape, b.dtype) for b in bufs],
        input_output_aliases={a: a for a in range(n)},
        compiler_params=pltpu.CompilerParams(has_side_effects=DATAFLOW))(*bufs, send_sem, recv_sem, after)


class _GradReducer:
    def __init__(self, chip, core):
        self.core, self.place, self.pairs, self.started = core, jnp.concatenate([chip, core]), {}, []

    def start(self, layer, group, grads):
        mine = [grads[n] for n in REDUCE[group]]
        mine, lands, send, recv, token = _pair_start("grad_pair_start_%d_%d" % (layer, group), mine, self.place)
        self.pairs[(layer, group)] = (mine, lands, send, recv)
        return token

    def scatter(self, layer, group, after):
        tag = "%d_%d" % (layer, group)
        names = REDUCE[group]
        mine, lands, send, recv = self.pairs.pop((layer, group))
        mine, theirs = _pair_finish("grad_pair_finish_" + tag, mine, lands, send, recv, after)
        sums = [_pair_sum("pair_sum_%s_%d" % (n, layer), self.core, g, p) for n, g, p in zip(names, mine, theirs)]
        sums, slots, send, recv, token = _scatter_start("grad_scatter_start_" + tag, sums, self.place)
        self.started.append((layer, names, sums, slots, send, recv))
        return token

    def finish(self, after, update):
        for layer in sorted({entry[0] for entry in self.started}, reverse=True):
            exchanged = []
            for lyr, names, sums, slots, send, recv in self.started:
                if lyr != layer:
                    continue
                tag = "%s_%d" % (names[0], layer)
                sums, slots = _scatter_finish("grad_scatter_finish_" + tag, sums, slots, send, recv, after)
                halves = [_slot_sum("slot_sum_%s_%d" % (n, layer), self.place, r, s)
                          for n, r, s in zip(names, slots, sums)]
                halves, send, recv, after = _half_start("grad_half_start_" + tag, halves, self.place)
                exchanged.append((tag, names, halves, send, recv))
            for tag, names, halves, send, recv in exchanged:
                whole = _half_finish("grad_half_finish_" + tag, halves, send, recv, after)
                after = update(layer, dict(zip(names, whole)))


def _all_reduce_small(arrays, after):
    n = len(arrays)
    n_dev = 2 * N_CHIPS

    def body(*refs):
        xs, outs, gats = refs[:n], refs[n + 1:2 * n + 1], refs[2 * n + 1:3 * n + 1]
        send_sems, recv_sems, local_sems = refs[3 * n + 1:]
        x, y, c, chips = _place()
        me, sibling = (x, y, c), (x, y, 1 - c)

        def slot(a, px, py, pc):
            return gats[a].at[4 * px + 2 * py + pc]

        def copy(a, k, block, to, src=None):
            return pltpu.make_async_remote_copy(
                src_ref=slot(a, *block) if src is None else src, dst_ref=slot(a, *block),
                send_sem=send_sems.at[7 * a + k], recv_sem=recv_sems.at[7 * a + k], device_id=to, device_id_type=MESH)

        local = [pltpu.make_async_copy(xs[a], slot(a, *me), local_sems.at[a]) for a in range(n)]
        for cp in local:
            cp.start()
        sends = []
        for a in range(n):
            sends.append(copy(a, 0, me, sibling, src=xs[a]))
            sends += [copy(a, 1 + j, me, (*chip, c), src=xs[a]) for j, chip in enumerate(chips)]
        for cp in sends:
            cp.start()
        for a in range(n):
            for j, chip in enumerate(chips):
                copy(a, 1 + j, (*chip, c), me).wait_recv()
                sends.append(copy(a, 4 + j, (*chip, c), sibling))
                sends[-1].start()
        for a in range(n):
            copy(a, 0, sibling, me).wait_recv()
            for j, chip in enumerate(chips):
                copy(a, 4 + j, (*chip, 1 - c), me).wait_recv()
        for cp in sends:
            cp.wait_send()
        for cp in local:
            cp.wait()
        for a in range(n):
            acc = gats[a][0]
            for d in range(1, n_dev):
                acc = acc + gats[a][d]
            outs[a][...] = acc

    vm = pl.BlockSpec(memory_space=pltpu.VMEM)
    return pl.pallas_call(
        body, name="small_grad_all_reduce", in_specs=[vm] * n + [ANY], out_specs=[vm] * n,
        out_shape=[jax.ShapeDtypeStruct(a.shape, F32) for a in arrays],
        scratch_shapes=[pltpu.VMEM((n_dev,) + a.shape, F32) for a in arrays]
        + [pltpu.SemaphoreType.DMA((7 * n,)), pltpu.SemaphoreType.DMA((7 * n,)), pltpu.SemaphoreType.DMA((n,))],
        compiler_params=_params())(*arrays, after)


def _adamw_math(w, g, m, v):
    m2 = ADAM_B1 * m + (1.0 - ADAM_B1) * g
    v2 = ADAM_B2 * v + (1.0 - ADAM_B2) * (g * g)
    m_hat = m2 / (1.0 - ADAM_B1 ** ADAM_STEP)
    v_hat = v2 / (1.0 - ADAM_B2 ** ADAM_STEP)
    delta = -ADAM_LR * (m_hat / (jnp.sqrt(v_hat) + ADAM_EPS) + ADAM_WD * w)
    return delta, m2, v2


def _adamw_big(name, layer, grad, w, m, v, others):
    L, R, C = w.shape
    tr = _row_tile(R, 256)

    def body(g_ref, w_ref, m_ref, v_ref, *rest):
        go_ref, d_ref, mo_ref, vo_ref = rest[-4:]
        g = g_ref[...]
        delta, m2, v2 = _adamw_math(w_ref[...], g, m_ref[...], v_ref[...])
        go_ref[...] = g
        d_ref[...] = delta
        mo_ref[...] = m2
        vo_ref[...] = v2

    blk = pl.BlockSpec((None, tr, C), lambda i: (layer, i, 0))
    shp = jax.ShapeDtypeStruct(w.shape, F32)
    others = [] if others is None else list(others)
    return pl.pallas_call(
        body, name=name, grid=(R // tr,),
        in_specs=[pl.BlockSpec((tr, C), lambda i: (i, 0))] + [blk] * 3 + [ANY] * len(others), out_specs=[blk] * 4,
        out_shape=[shp] * 4, input_output_aliases={4 + k: k for k in range(len(others))},
        compiler_params=_params())(grad, w, m, v, *others)


def _adamw_small(gs, ws, ms, vs):
    n = len(gs)

    def body(*refs):
        for a in range(n):
            g_ref, w_ref, m_ref, v_ref = refs[a], refs[n + a], refs[2 * n + a], refs[3 * n + a]
            delta, m2, v2 = _adamw_math(w_ref[...], g_ref[...], m_ref[...], v_ref[...])
            refs[4 * n + a][...] = delta
            refs[5 * n + a][...] = m2
            refs[6 * n + a][...] = v2

    vm = pl.BlockSpec(memory_space=pltpu.VMEM)
    shapes = [jax.ShapeDtypeStruct(g.shape, F32) for g in gs]
    outs = pl.pallas_call(
        body, name="adamw_small", in_specs=[vm] * (4 * n), out_specs=[vm] * (3 * n), out_shape=shapes * 3,
        compiler_params=_params())(*gs, *ws, *ms, *vs)
    return outs[:n], outs[n:2 * n], outs[2 * n:]


def _rows2d(a):
    return a if a.ndim == 2 else a.reshape(-1, a.shape[-1])


BIG = ("w_in", "w_attn_branch", "w_sgu_branch", "w_out", "w_gate", "w_up", "w_down")
SMALL = ("mix_norm", "q_norm", "k_norm", "sinks", "sgu_ln_g", "sgu_ln_b", "w_spatial", "b_spatial", "ffn_norm")
ORDER = ("mix_norm", "w_in", "q_norm", "k_norm", "sinks", "sgu_ln_g", "sgu_ln_b", "w_spatial", "b_spatial",
         "w_attn_branch", "w_sgu_branch", "w_out", "ffn_norm", "w_gate", "w_up", "w_down")


def _rope_tables(seq):
    pos = jnp.arange(seq, dtype=F32)
    inv_freq = jnp.power(10000.0, -jnp.arange(0, HEAD_DIM, 2, dtype=F32) / HEAD_DIM)
    ang = pos[:, None] * inv_freq[None, :]
    cos, sin = jnp.cos(ang), jnp.sin(ang)
    reps = BLOCK // HEAD_DIM
    return (jnp.tile(jnp.concatenate([cos, cos], axis=1), (1, reps)),
            jnp.tile(jnp.concatenate([-sin, sin], axis=1), (1, reps)))


def kernel(x, mix_norm, w_in, q_norm, k_norm, sinks, sgu_ln_g, sgu_ln_b, w_spatial, b_spatial, w_attn_branch, w_sgu_branch, w_out, ffn_norm, w_gate, w_up, w_down, loss_target, m_mix_norm, m_w_in, m_q_norm, m_k_norm, m_sinks, m_sgu_ln_g, m_sgu_ln_b, m_w_spatial, m_b_spatial, m_w_attn_branch, m_w_sgu_branch, m_w_out, m_ffn_norm, m_w_gate, m_w_up, m_w_down, v_mix_norm, v_w_in, v_q_norm, v_k_norm, v_sinks, v_sgu_ln_g, v_sgu_ln_b, v_w_spatial, v_b_spatial, v_w_attn_branch, v_w_sgu_branch, v_w_out, v_ffn_norm, v_w_gate, v_w_up, v_w_down):
    weights = dict(mix_norm=mix_norm, w_in=w_in, q_norm=q_norm, k_norm=k_norm, sinks=sinks, sgu_ln_g=sgu_ln_g,
                   sgu_ln_b=sgu_ln_b, w_spatial=w_spatial, b_spatial=b_spatial, w_attn_branch=w_attn_branch,
                   w_sgu_branch=w_sgu_branch, w_out=w_out, ffn_norm=ffn_norm, w_gate=w_gate, w_up=w_up, w_down=w_down)
    mom1 = dict(mix_norm=m_mix_norm, w_in=m_w_in, q_norm=m_q_norm, k_norm=m_k_norm, sinks=m_sinks,
                sgu_ln_g=m_sgu_ln_g, sgu_ln_b=m_sgu_ln_b, w_spatial=m_w_spatial, b_spatial=m_b_spatial,
                w_attn_branch=m_w_attn_branch, w_sgu_branch=m_w_sgu_branch, w_out=m_w_out, ffn_norm=m_ffn_norm,
                w_gate=m_w_gate, w_up=m_w_up, w_down=m_w_down)
    mom2 = dict(mix_norm=v_mix_norm, w_in=v_w_in, q_norm=v_q_norm, k_norm=v_k_norm, sinks=v_sinks,
                sgu_ln_g=v_sgu_ln_g, sgu_ln_b=v_sgu_ln_b, w_spatial=v_w_spatial, b_spatial=v_b_spatial,
                w_attn_branch=v_w_attn_branch, w_sgu_branch=v_w_sgu_branch, w_out=v_w_out, ffn_norm=v_ffn_norm,
                w_gate=v_w_gate, w_up=v_w_up, w_down=v_w_down)
    xs, target = x[0], loss_target[0]
    S, D = xs.shape
    L = w_in.shape[0]
    AW, KW, SW = N_Q_HEADS * HEAD_DIM, N_KV_HEADS * HEAD_DIM, SGU_GROUPS * BLOCK
    dims = (AW, KW, AW + 2 * KW + 2 * SW, AW + 2 * KW)
    cos, sin = _rope_tables(S)
    reps = BLOCK // HEAD_DIM

    chip = (2 * lax.axis_index("x") + lax.axis_index("y")).astype(jnp.int32).reshape(1)
    core = lax.axis_index("c").astype(jnp.int32).reshape(1)
    started, token = {}, chip
    for l in range(L):
        for gi, names in enumerate(GATHER):
            bufs = [_cast_own("cast_%s_%d" % (n, l), chip, weights[n], l) for n in names]
            bufs, send, recv, token = _gather_start("gather_start_%d_%d" % (l, gi), bufs, token)
            started[(l, gi)] = (bufs, send, recv)
    stream = _WeightStream(started)
    stream.forward(0, 0, token)
    sp = [dict(mix_norm=mix_norm[l][None], ffn_norm=ffn_norm[l][None], q_norm=jnp.tile(q_norm[l][None], (1, reps)),
               k_norm=jnp.tile(k_norm[l][None], (1, reps)), sinks=sinks[l][None], sgu_ln_g=sgu_ln_g[l][None],
               sgu_ln_b=sgu_ln_b[l][None], w_spatial=w_spatial[l], b_spatial=b_spatial[l][:, :, None])
          for l in range(L)]

    act, saved, wl = xs, [], []
    for l in range(L):
        act, sv, w_all = _layer_fwd(act, stream, l, l == L - 1, sp[l], cos, sin, dims)
        saved.append(sv)
        wl.append(w_all)
    loss_part, dy, dyb = _loss_head(act, target)
    loss = lax.psum(loss_part[0, 0], ("x", "y", "c"))

    reducer = _GradReducer(chip, core)
    small_g = [None] * L
    for l in reversed(range(L)):
        dy, dyb, small_g[l] = _layer_bwd(dy, dyb, wl[l], sp[l], saved[l], cos, sin, dims, reducer, l)
    grad_x = dy[None]

    updated = {}

    def update(layer, reduced):
        for n, g in reduced.items():
            updated[n] = _adamw_big("adamw_%s_%d" % (n, layer), layer, g, weights[n], mom1[n], mom2[n],
                                    updated.get(n))
        return updated[n][0]

    reducer.finish(dy, update)
    grads, deltas, new_m, new_v = {}, {}, {}, {}
    for n in BIG:
        grads[n], deltas[n], new_m[n], new_v[n] = updated[n]

    local = [_rows2d(jnp.stack([small_g[l][n].reshape(weights[n].shape[1:]) for l in range(L)])) for n in SMALL]
    g_small = _all_reduce_small(local, updated[BIG[0]][0])
    d_small, m_small, v_small = _adamw_small(g_small, [_rows2d(weights[n]) for n in SMALL],
                                             [_rows2d(mom1[n]) for n in SMALL], [_rows2d(mom2[n]) for n in SMALL])
    for n, g, d, m2, v2 in zip(SMALL, g_small, d_small, m_small, v_small):
        shape = weights[n].shape
        grads[n], deltas[n], new_m[n], new_v[n] = g.reshape(shape), d.reshape(shape), m2.reshape(shape), v2.reshape(shape)

    return (loss, grad_x, *[grads[n] for n in ORDER], *[deltas[n] for n in ORDER],
            *[new_m[n] for n in ORDER], *[new_v[n] for n in ORDER])
```

```python
import functools

import jax
import jax.numpy as jnp
from jax import lax
from jax.experimental import pallas as pl
from jax.experimental.pallas import tpu as pltpu

HEAD_DIM = 64
N_Q_HEADS = 16
N_KV_HEADS = 4
SGU_GROUPS = 8
BLOCK = 128
EPS = 1e-6
ADAM_LR = 0.001
ADAM_B1 = 0.9
ADAM_B2 = 0.999
ADAM_EPS = 1e-08
ADAM_WD = 0.01
ADAM_STEP = 10
N_CHIPS = 4
VMEM_LIMIT = 52 * 1024 * 1024
MXU_CHUNK = 256
ATTN_STACK = 4

F32 = jnp.float32
MXU = jnp.bfloat16
NN = (((1,), (0,)), ((), ()))
NT = (((1,), (1,)), ((), ()))
TN = (((0,), (0,)), ((), ()))
MESH = pl.DeviceIdType.MESH
ANY = pl.BlockSpec(memory_space=pl.ANY)


def _tile(n, pref):
    if n <= pref:
        return n
    best = None
    for t in range(BLOCK, pref + 1, BLOCK):
        if n % t == 0:
            best = t
    assert best is not None, (n, pref)
    return best


def _params():
    return pltpu.CompilerParams(vmem_limit_bytes=VMEM_LIMIT)


def _mm(name, grid, n_red, operands, specs, pairs, dims, n_extra, out_shapes, out_specs,
        acc_shapes, epilogue, after=None, chunk=None):
    n_op = len(operands) - n_extra
    n_out = len(out_shapes)
    n_acc = len(acc_shapes)
    if after is not None:
        operands, specs = list(operands) + [after], list(specs) + [ANY]
    n_in = len(operands)
    axes = [ax for ax in range(len(grid) - n_red, len(grid)) if grid[ax] > 1]

    def body(*refs):
        ops = refs[:n_op]
        extra = refs[n_op:n_op + n_extra]
        outs = refs[n_in:n_in + n_out]
        accs = refs[n_in + n_out:]

        def prod(a, b, cols=None):
            rhs = ops[b]
            if cols is not None:
                rhs = rhs.at[:, cols] if dims == NN else rhs.at[cols, :]
            return lax.dot_general(ops[a][...], rhs[...], dims, preferred_element_type=F32)

        def products(cols=None):
            vals = [None] * n_acc
            for a, b, k in pairs:
                d = prod(a, b, cols)
                vals[k] = d if vals[k] is None else vals[k] + d
            return vals

        if not axes and chunk is not None:
            width = outs[0].shape[-1]
            for c0 in range(0, width, chunk):
                cols = pl.ds(c0, min(chunk, width - c0))
                epilogue(products(cols), [e.at[:, cols] for e in extra], [o.at[:, cols] for o in outs])
        elif not axes:
            epilogue(products(), extra, outs)
        else:
            first = pl.program_id(axes[0]) == 0
            last = pl.program_id(axes[0]) == grid[axes[0]] - 1
            for ax in axes[1:]:
                first = jnp.logical_and(first, pl.program_id(ax) == 0)
                last = jnp.logical_and(last, pl.program_id(ax) == grid[ax] - 1)

            @pl.when(first)
            def _():
                for acc in accs:
                    acc[...] = jnp.zeros(acc.shape, F32)

            for a, b, k in pairs:
                accs[k][...] += prod(a, b)

            @pl.when(last)
            def _():
                epilogue([acc[...] for acc in accs], extra, outs)

    scratch = [pltpu.VMEM(s, F32) for s in acc_shapes] if axes else []
    return pl.pallas_call(
        body, name=name, grid=grid, in_specs=specs, out_specs=out_specs, out_shape=out_shapes,
        scratch_shapes=scratch, compiler_params=_params())(*operands)


def _sigmoid(x):
    return 1.0 / (1.0 + jnp.exp(-x))


_GELU_C = 0.7978845608028654
_GELU_A = 0.044715


def _gelu(x):
    return 0.5 * x * (1.0 + jnp.tanh(_GELU_C * (x + _GELU_A * x * x * x)))


def _gelu_grad(x):
    t = jnp.tanh(_GELU_C * (x + _GELU_A * x * x * x))
    return 0.5 * (1.0 + t) + 0.5 * x * (1.0 - t * t) * _GELU_C * (1.0 + 3.0 * _GELU_A * x * x)


def _rms_fwd(name, x, g):
    S, D = x.shape
    tr = _tile(S, 256)

    def body(x_ref, g_ref, o_ref):
        xv = x_ref[...]
        r = lax.rsqrt(jnp.mean(xv * xv, axis=-1, keepdims=True) + EPS)
        o_ref[...] = (xv * r * g_ref[...]).astype(MXU)

    return pl.pallas_call(
        body, name=name, grid=(S // tr,),
        in_specs=[pl.BlockSpec((tr, D), lambda i: (i, 0)), pl.BlockSpec((1, D), lambda i: (0, 0))],
        out_specs=pl.BlockSpec((tr, D), lambda i: (i, 0)),
        out_shape=jax.ShapeDtypeStruct((S, D), MXU), compiler_params=_params())(x, g)


def _rms_bwd(name, dh, x, g, dres, after):
    S, D = x.shape
    tr = _tile(S, 256)

    def body(dh_ref, x_ref, g_ref, dres_ref, after_ref, dx_ref, dxb_ref, dg_ref):
        xv = x_ref[...]
        r = lax.rsqrt(jnp.mean(xv * xv, axis=-1, keepdims=True) + EPS)
        xh = xv * r
        dhv = dh_ref[...]
        dy = dhv * g_ref[...]
        dx = dres_ref[...] + r * (dy - xh * jnp.mean(dy * xh, axis=-1, keepdims=True))
        dx_ref[...] = dx
        dxb_ref[...] = dx.astype(MXU)

        @pl.when(pl.program_id(0) == 0)
        def _():
            dg_ref[...] = jnp.zeros(dg_ref.shape, F32)

        dg_ref[...] += jnp.sum(dhv * xh, axis=0, keepdims=True)

    row = pl.BlockSpec((tr, D), lambda i: (i, 0))
    vec = pl.BlockSpec((1, D), lambda i: (0, 0))
    return pl.pallas_call(
        body, name=name, grid=(S // tr,), in_specs=[row, row, vec, row, ANY], out_specs=[row, row, vec],
        out_shape=[jax.ShapeDtypeStruct((S, D), F32), jax.ShapeDtypeStruct((S, D), MXU),
                   jax.ShapeDtypeStruct((1, D), F32)],
        compiler_params=_params())(dh, x, g, dres, after)


def _join_columns(name, parts):
    S = parts[0].shape[0]
    tr = _tile(S, 256)
    widths = [p.shape[1] for p in parts]
    assert all(w % BLOCK == 0 for w in widths)

    def body(*refs):
        o_ref, off = refs[-1], 0
        for ref, w in zip(refs[:-1], widths):
            o_ref[:, off:off + w] = ref[...]
            off += w

    return pl.pallas_call(
        body, name=name, grid=(S // tr,), in_specs=[pl.BlockSpec((tr, w), lambda i: (i, 0)) for w in widths],
        out_specs=pl.BlockSpec((tr, sum(widths)), lambda i: (i, 0)),
        out_shape=jax.ShapeDtypeStruct((S, sum(widths)), parts[0].dtype), compiler_params=_params())(*parts)


def _loss_head(y, target):
    S, D = y.shape
    tr = _tile(S, 256)

    def body(y_ref, t_ref, loss_ref, dy_ref, dyb_ref):
        d = y_ref[...] - t_ref[...]
        dy = d * (1.0 / D)
        dy_ref[...] = dy
        dyb_ref[...] = dy.astype(MXU)

        @pl.when(pl.program_id(0) == 0)
        def _():
            loss_ref[...] = jnp.zeros(loss_ref.shape, F32)

        loss_ref[...] += (0.5 / D) * jnp.sum(jnp.sum(d * d, axis=-1, keepdims=True), axis=0, keepdims=True)

    row = pl.BlockSpec((tr, D), lambda i: (i, 0))
    return pl.pallas_call(
        body, name="loss_head", grid=(S // tr,), in_specs=[row, row],
        out_specs=[pl.BlockSpec((1, 1), lambda i: (0, 0)), row, row],
        out_shape=[jax.ShapeDtypeStruct((1, 1), F32), jax.ShapeDtypeStruct((S, D), F32),
                   jax.ShapeDtypeStruct((S, D), MXU)],
        compiler_params=_params())(y, target)


def _head_sum(v):
    r = lax.broadcasted_iota(jnp.int32, (BLOCK, BLOCK), 0) // HEAD_DIM
    c = lax.broadcasted_iota(jnp.int32, (BLOCK, BLOCK), 1) // HEAD_DIM
    ones = jnp.where(r == c, 1.0, 0.0).astype(jnp.bfloat16)
    hi = v.astype(jnp.bfloat16)
    lo = (v - hi.astype(F32)).astype(jnp.bfloat16)
    parts = []
    for t in range(v.shape[1] // BLOCK):
        sl = slice(t * BLOCK, (t + 1) * BLOCK)
        parts.append(jnp.dot(hi[:, sl], ones, preferred_element_type=F32)
                     + jnp.dot(lo[:, sl], ones, preferred_element_type=F32))
    return parts[0] if len(parts) == 1 else jnp.concatenate(parts, axis=-1)


def _swap_halves(v):
    w = v.shape[1]
    half = HEAD_DIM // 2
    lane = lax.broadcasted_iota(jnp.int32, v.shape, 1) % HEAD_DIM
    return jnp.where(lane < half, pltpu.roll(v, w - half, 1), pltpu.roll(v, half, 1))


def _norm_rope(xv, gain, cos, sin):
    r = lax.rsqrt(_head_sum(xv * xv) * (1.0 / HEAD_DIM) + EPS)
    xn = xv * r * gain
    return xn * cos + _swap_halves(xn) * sin


def _norm_rope_bwd(dy, xv, gain, cos, sin):
    r = lax.rsqrt(_head_sum(xv * xv) * (1.0 / HEAD_DIM) + EPS)
    xh = xv * r
    dxn = dy * cos + _swap_halves(dy * sin)
    dgain = jnp.sum(dxn * xh, axis=0, keepdims=True)
    dxh = dxn * gain
    dx = r * (dxh - xh * (_head_sum(dxh * xh) * (1.0 / HEAD_DIM)))
    return dx, dgain


def _fold_heads(v):
    acc = v[:, 0:BLOCK]
    for t in range(1, v.shape[1] // BLOCK):
        acc = acc + v[:, t * BLOCK:(t + 1) * BLOCK]
    return acc + pltpu.roll(acc, HEAD_DIM, 1)


def _tile_lanes(v, width):
    return v if width == BLOCK else jnp.tile(v, (1, width // BLOCK))


def _low_half(rows):
    assert BLOCK == 2 * HEAD_DIM
    return lax.broadcasted_iota(jnp.int32, (rows, BLOCK), 1) < HEAD_DIM


def _spread_heads(v):
    low = _low_half(v.shape[0])
    out = []
    for t in range(v.shape[1] // BLOCK):
        tile = v[:, t * BLOCK:(t + 1) * BLOCK]
        swapped = pltpu.roll(tile, HEAD_DIM, 1)
        out += [jnp.where(low, tile, swapped), jnp.where(low, swapped, tile)]
    return jnp.concatenate(out, axis=-1)


def _gather_heads(v):
    low = _low_half(v.shape[0])
    out = []
    for t in range(v.shape[1] // (2 * BLOCK)):
        a, b = v[:, 2 * t * BLOCK:(2 * t + 1) * BLOCK], v[:, (2 * t + 1) * BLOCK:(2 * t + 2) * BLOCK]
        out.append(jnp.where(low, a + pltpu.roll(a, HEAD_DIM, 1), b + pltpu.roll(b, HEAD_DIM, 1)))
    return out[0] if len(out) == 1 else jnp.concatenate(out, axis=-1)


def _qk_prep(proj, qg, kg, cos, sin, AW, KW):
    S = proj.shape[0]
    tr = _tile(S, 256)
    scale = HEAD_DIM ** -0.5

    def body(q_ref, k_ref, v_ref, qg_ref, kg_ref, cos_ref, sin_ref, qo_ref, ko_ref, vo_ref):
        c, s = cos_ref[...], sin_ref[...]
        q = _norm_rope(q_ref[...], _tile_lanes(qg_ref[...], AW), _tile_lanes(c, AW), _tile_lanes(s, AW))
        k = _norm_rope(k_ref[...], _tile_lanes(kg_ref[...], KW), _tile_lanes(c, KW), _tile_lanes(s, KW))
        qo_ref[...] = (q * scale).astype(MXU)
        ko_ref[...] = _spread_heads(k).astype(MXU)
        vo_ref[...] = _spread_heads(v_ref[...]).astype(MXU)

    assert AW % KW == 0
    vec = pl.BlockSpec((1, BLOCK), lambda i: (0, 0))
    tab = pl.BlockSpec((tr, BLOCK), lambda i: (i, 0))
    wide = pl.BlockSpec((tr, 2 * KW), lambda i: (i, 0))
    return pl.pallas_call(
        body, name="qk_prep", grid=(S // tr,),
        in_specs=[pl.BlockSpec((tr, AW), lambda i: (i, 0)),
                  pl.BlockSpec((tr, KW), lambda i: (i, AW // KW)),
                  pl.BlockSpec((tr, KW), lambda i: (i, AW // KW + 1)), vec, vec, tab, tab],
        out_specs=[pl.BlockSpec((tr, AW), lambda i: (i, 0)), wide, wide],
        out_shape=[jax.ShapeDtypeStruct((S, AW), MXU), jax.ShapeDtypeStruct((S, 2 * KW), MXU),
                   jax.ShapeDtypeStruct((S, 2 * KW), MXU)],
        compiler_params=_params())(proj, proj, proj, qg, kg, cos, sin)


def _stack_heads(x, h0, nh):
    low = _low_half(BLOCK)
    parts = []
    for h in range(h0, h0 + nh):
        tile = x[:, (h // 2) * BLOCK:(h // 2 + 1) * BLOCK]
        parts.append(jnp.where(low if h % 2 == 0 else jnp.logical_not(low), tile, jnp.zeros_like(tile)))
    return jnp.concatenate(parts, axis=0)


def _unstack_heads(y):
    low = _low_half(BLOCK)
    tiles = [jnp.where(low, y[2 * t * BLOCK:(2 * t + 1) * BLOCK], y[(2 * t + 1) * BLOCK:(2 * t + 2) * BLOCK])
             for t in range(y.shape[0] // (2 * BLOCK))]
    return tiles[0] if len(tiles) == 1 else jnp.concatenate(tiles, axis=-1)


def _band_t(n):
    key = lax.broadcasted_iota(jnp.int32, (2 * BLOCK, BLOCK), 0)
    qry = lax.broadcasted_iota(jnp.int32, (2 * BLOCK, BLOCK), 1)
    return (key > qry) & (key <= qry + BLOCK) & ((key >= BLOCK) | (n > 0))


def _attn_probs_t(ok, qs, kcat, h0, nh, sink_ref):
    st = jnp.where(ok, lax.dot_general(kcat, qs, NT, preferred_element_type=F32), -1e30)
    sk = jnp.concatenate([jnp.full((1, BLOCK), sink_ref[0, h], F32) for h in range(h0, h0 + nh)], axis=1)
    m = jnp.maximum(jnp.max(st, axis=0, keepdims=True), sk)
    e = jnp.exp(st - m)
    es = jnp.exp(sk - m)
    rz = 1.0 / (jnp.sum(e, axis=0, keepdims=True) + es)
    return e * rz, es * rz, rz


def _attn_fwd(qr, kr, vb, sinks):
    S, AW = qr.shape
    KW = kr.shape[1]
    nb = S // BLOCK
    nkv = KW // BLOCK
    qpk = AW // (nkv * HEAD_DIM)
    nh = min(ATTN_STACK, qpk)
    assert nh % 2 == 0 and qpk % nh == 0

    def body(sink_ref, q_ref, kp_ref, kc_ref, vp_ref, vc_ref, o_ref):
        n = pl.program_id(0)
        q, kp, kc, vp, vc = q_ref[...], kp_ref[...], kc_ref[...], vp_ref[...], vc_ref[...]
        ok = jnp.concatenate([_band_t(n)] * nh, axis=1)
        outs = []
        for g in range(nkv):
            kcat = jnp.concatenate([kp[:, g * BLOCK:(g + 1) * BLOCK], kc[:, g * BLOCK:(g + 1) * BLOCK]], axis=0)
            vcat = jnp.concatenate([vp[:, g * BLOCK:(g + 1) * BLOCK], vc[:, g * BLOCK:(g + 1) * BLOCK]], axis=0)
            for h0 in range(g * qpk, (g + 1) * qpk, nh):
                pt, _, _ = _attn_probs_t(ok, _stack_heads(q, h0, nh), kcat, h0, nh, sink_ref)
                outs.append(_unstack_heads(lax.dot_general(pt.astype(MXU), vcat, TN, preferred_element_type=F32)))
        o_ref[...] = jnp.concatenate(outs, axis=-1).astype(MXU)

    cur = lambda n: (n, 0)
    prev = lambda n: (jnp.maximum(n - 1, 0), 0)
    return pl.pallas_call(
        body, name="attn_fwd", grid=(nb,),
        in_specs=[pl.BlockSpec(memory_space=pltpu.SMEM), pl.BlockSpec((BLOCK, AW), cur),
                  pl.BlockSpec((BLOCK, KW), prev), pl.BlockSpec((BLOCK, KW), cur),
                  pl.BlockSpec((BLOCK, KW), prev), pl.BlockSpec((BLOCK, KW), cur)],
        out_specs=pl.BlockSpec((BLOCK, AW), cur),
        out_shape=jax.ShapeDtypeStruct((S, AW), MXU), compiler_params=_params())(sinks, qr, kr, kr, vb, vb)


def _attn_bwd(qr, kr, vb, sinks, dattn):
    S, AW = qr.shape
    KW = kr.shape[1]
    nb = S // BLOCK
    nkv = KW // BLOCK
    qpk = AW // (nkv * HEAD_DIM)
    nh = min(ATTN_STACK, qpk)
    scale = HEAD_DIM ** -0.5

    def body(sink_ref, q_ref, kp_ref, kc_ref, vp_ref, vc_ref, do_ref,
             dq_ref, dkp_ref, dkc_ref, dvp_ref, dvc_ref, dsink_ref):
        n = pl.program_id(0)
        q, kp, kc, vp, vc = q_ref[...], kp_ref[...], kc_ref[...], vp_ref[...], vc_ref[...]
        do = do_ref[...].astype(MXU)
        lane = lax.broadcasted_iota(jnp.int32, (1, BLOCK), 1)
        ok = jnp.concatenate([_band_t(n)] * nh, axis=1)
        dsink = jnp.zeros((1, BLOCK), F32)
        dqs, dkps, dkcs, dvps, dvcs = [], [], [], [], []
        for g in range(nkv):
            kcat = jnp.concatenate([kp[:, g * BLOCK:(g + 1) * BLOCK], kc[:, g * BLOCK:(g + 1) * BLOCK]], axis=0)
            vcat = jnp.concatenate([vp[:, g * BLOCK:(g + 1) * BLOCK], vc[:, g * BLOCK:(g + 1) * BLOCK]], axis=0)
            dk, dv = None, None
            for h0 in range(g * qpk, (g + 1) * qpk, nh):
                qs = _stack_heads(q, h0, nh)
                dos = _stack_heads(do, h0, nh)
                pt, ps, _ = _attn_probs_t(ok, qs, kcat, h0, nh, sink_ref)
                dpt = lax.dot_general(vcat, dos, NT, preferred_element_type=F32)
                delta = jnp.sum(pt * dpt, axis=0, keepdims=True)
                dst = (pt * (dpt - delta)).astype(MXU)
                dsk = -ps * delta
                dv_part = jnp.dot(pt.astype(MXU), dos, preferred_element_type=F32)
                dk_part = jnp.dot(dst, qs, preferred_element_type=F32)
                dqs.append(_unstack_heads(lax.dot_general(dst, kcat, TN, preferred_element_type=F32) * scale))
                dk = dk_part if dk is None else dk + dk_part
                dv = dv_part if dv is None else dv + dv_part
                for j in range(nh):
                    tot = jnp.sum(dsk[:, j * BLOCK:(j + 1) * BLOCK], axis=1, keepdims=True)
                    dsink = dsink + jnp.where(lane == h0 + j, tot, 0.0)
            dkps.append(dk[:BLOCK])
            dkcs.append(dk[BLOCK:])
            dvps.append(dv[:BLOCK])
            dvcs.append(dv[BLOCK:])
        dq_ref[...] = jnp.concatenate(dqs, axis=-1)
        dkp_ref[...] = jnp.concatenate(dkps, axis=-1)
        dkc_ref[...] = jnp.concatenate(dkcs, axis=-1)
        dvp_ref[...] = jnp.concatenate(dvps, axis=-1)
        dvc_ref[...] = jnp.concatenate(dvcs, axis=-1)

        @pl.when(n == 0)
        def _():
            dsink_ref[...] = jnp.zeros(dsink_ref.shape, F32)

        dsink_ref[...] += dsink

    cur = lambda n: (n, 0)
    prev = lambda n: (jnp.maximum(n - 1, 0), 0)
    kv = jax.ShapeDtypeStruct((S, KW), F32)
    kvspec = pl.BlockSpec((BLOCK, KW), cur)
    return pl.pallas_call(
        body, name="attn_bwd", grid=(nb,),
        in_specs=[pl.BlockSpec(memory_space=pltpu.SMEM), pl.BlockSpec((BLOCK, AW), cur),
                  pl.BlockSpec((BLOCK, KW), prev), kvspec, pl.BlockSpec((BLOCK, KW), prev), kvspec,
                  pl.BlockSpec((BLOCK, AW), cur)],
        out_specs=[pl.BlockSpec((BLOCK, AW), cur), kvspec, kvspec, kvspec, kvspec,
                   pl.BlockSpec((1, BLOCK), lambda n: (0, 0))],
        out_shape=[jax.ShapeDtypeStruct((S, AW), F32), kv, kv, kv, kv, jax.ShapeDtypeStruct((1, BLOCK), F32)],
        compiler_params=_params())(sinks, qr, kr, kr, vb, vb, dattn)


def _qk_prep_bwd(proj, qg, kg, cos, sin, dq, dkp, dkc, dvp, dvc, AW, KW):
    S = proj.shape[0]
    nb = S // BLOCK

    def body(q_ref, k_ref, qg_ref, kg_ref, cos_ref, sin_ref, dq_ref, dkp_ref, dkc_ref, dvp_ref, dvc_ref,
             o_ref, dqg_ref, dkg_ref):
        n = pl.program_id(0)
        c, s = cos_ref[...], sin_ref[...]
        has_next = jnp.where(n < nb - 1, 1.0, 0.0)
        dk = _gather_heads(dkc_ref[...] + has_next * dkp_ref[...])
        dv = _gather_heads(dvc_ref[...] + has_next * dvp_ref[...])
        dxq, dqg = _norm_rope_bwd(dq_ref[...], q_ref[...], _tile_lanes(qg_ref[...], AW),
                                  _tile_lanes(c, AW), _tile_lanes(s, AW))
        dxk, dkg = _norm_rope_bwd(dk, k_ref[...], _tile_lanes(kg_ref[...], KW),
                                  _tile_lanes(c, KW), _tile_lanes(s, KW))
        o_ref[...] = jnp.concatenate([dxq, dxk, dv], axis=-1).astype(MXU)

        @pl.when(n == 0)
        def _():
            dqg_ref[...] = jnp.zeros(dqg_ref.shape, F32)
            dkg_ref[...] = jnp.zeros(dkg_ref.shape, F32)

        dqg_ref[...] += _fold_heads(dqg)
        dkg_ref[...] += _fold_heads(dkg)

    cur = lambda n: (n, 0)
    nxt = lambda n: (jnp.minimum(n + 1, nb - 1), 0)
    vec = pl.BlockSpec((1, BLOCK), lambda n: (0, 0))
    tab = pl.BlockSpec((BLOCK, BLOCK), cur)
    return pl.pallas_call(
        body, name="qk_prep_bwd", grid=(nb,),
        in_specs=[pl.BlockSpec((BLOCK, AW), cur), pl.BlockSpec((BLOCK, KW), lambda n: (n, AW // KW)),
                  vec, vec, tab, tab, pl.BlockSpec((BLOCK, AW), cur),
                  pl.BlockSpec((BLOCK, 2 * KW), nxt), pl.BlockSpec((BLOCK, 2 * KW), cur),
                  pl.BlockSpec((BLOCK, 2 * KW), nxt), pl.BlockSpec((BLOCK, 2 * KW), cur)],
        out_specs=[pl.BlockSpec((BLOCK, AW + 2 * KW), cur), vec, vec],
        out_shape=[jax.ShapeDtypeStruct((S, AW + 2 * KW), MXU), jax.ShapeDtypeStruct((1, BLOCK), F32),
                   jax.ShapeDtypeStruct((1, BLOCK), F32)],
        compiler_params=_params())(proj, proj, qg, kg, cos, sin, dq, dkp, dkc, dvp, dvc)


SGU_LANES = 512
SGU_ROWS = 256


def _sgu_group(v, lng, lnb, w_f32, b):
    rows = v.shape[0]
    mu = jnp.mean(v, axis=-1, keepdims=True)
    vc = v - mu
    r = lax.rsqrt(jnp.mean(vc * vc, axis=-1, keepdims=True) + EPS)
    xh = vc * r
    vn = (xh * lng + lnb).astype(MXU)
    row = lax.broadcasted_iota(jnp.int32, (BLOCK, BLOCK), 0)
    col = lax.broadcasted_iota(jnp.int32, (BLOCK, BLOCK), 1)
    tri = row >= col
    w = jnp.where(tri, w_f32, 0.0).astype(MXU)
    chunks = [jnp.dot(w, vn[k * BLOCK:(k + 1) * BLOCK], preferred_element_type=F32) + b for k in range(rows // BLOCK)]
    s = chunks[0] if len(chunks) == 1 else jnp.concatenate(chunks, axis=0)
    return xh, r, vn, w, s, tri


def _sgu_layout(S, u_col):
    SW = SGU_GROUPS * BLOCK
    lb, tr = min(SGU_LANES, SW), min(SGU_ROWS, S)
    assert u_col % lb == 0 and SW % lb == 0 and S % tr == 0
    ub, nlb, gpb = u_col // lb, SW // lb, lb // BLOCK
    specs = [pl.BlockSpec((tr, lb), lambda j, i: (i, ub + j)), pl.BlockSpec((tr, lb), lambda j, i: (i, ub + nlb + j)),
             pl.BlockSpec((1, lb), lambda j, i: (0, j)), pl.BlockSpec((1, lb), lambda j, i: (0, j)),
             pl.BlockSpec((gpb, BLOCK, BLOCK), lambda j, i: (j, 0, 0)),
             pl.BlockSpec((gpb, BLOCK, 1), lambda j, i: (j, 0, 0))]
    return lb, tr, gpb, nlb, specs


def _sgu_fwd(proj, lng, lnb, ws, bs, u_col):
    S = proj.shape[0]
    lb, tr, gpb, nlb, specs = _sgu_layout(S, u_col)

    def body(pu_ref, pv_ref, lng_ref, lnb_ref, w_ref, b_ref, o_ref):
        u = _gelu(pu_ref[...])
        v = _gelu(pv_ref[...])
        outs = []
        for g in range(gpb):
            sl = slice(g * BLOCK, (g + 1) * BLOCK)
            s = _sgu_group(v[:, sl], lng_ref[:, sl], lnb_ref[:, sl], w_ref[g], b_ref[g])[4]
            outs.append(u[:, sl] * s)
        o_ref[...] = (outs[0] if gpb == 1 else jnp.concatenate(outs, axis=-1)).astype(MXU)

    return pl.pallas_call(
        body, name="sgu_fwd", grid=(nlb, S // tr), in_specs=specs,
        out_specs=pl.BlockSpec((tr, lb), lambda j, i: (i, j)),
        out_shape=jax.ShapeDtypeStruct((S, nlb * lb), MXU), compiler_params=_params())(proj, proj, lng, lnb, ws, bs)


def _sgu_bwd(proj, lng, lnb, ws, bs, dsgu, u_col, after):
    S = proj.shape[0]
    G = SGU_GROUPS
    lb, tr, gpb, nlb, specs = _sgu_layout(S, u_col)
    nch = tr // BLOCK

    def body(pu_ref, pv_ref, lng_ref, lnb_ref, w_ref, b_ref, do_ref, after_ref,
             dpu_ref, dpv_ref, dw_ref, db_ref, dlng_ref, dlnb_ref):
        pu, pv, do = pu_ref[...], pv_ref[...], do_ref[...]
        u = _gelu(pu)
        v = _gelu(pv)

        @pl.when(pl.program_id(1) == 0)
        def _():
            dw_ref[...] = jnp.zeros(dw_ref.shape, F32)
            db_ref[...] = jnp.zeros(db_ref.shape, F32)
            dlng_ref[...] = jnp.zeros(dlng_ref.shape, F32)
            dlnb_ref[...] = jnp.zeros(dlnb_ref.shape, F32)

        ss, dvs, dlng, dlnb = [], [], [], []
        for g in range(gpb):
            sl = slice(g * BLOCK, (g + 1) * BLOCK)
            xh, r, vn, w, s, tri = _sgu_group(v[:, sl], lng_ref[:, sl], lnb_ref[:, sl], w_ref[g], b_ref[g])
            ds = do[:, sl] * u[:, sl]
            dsb = ds.astype(MXU)
            dw, db, dvn = None, None, []
            for k in range(nch):
                rows = slice(k * BLOCK, (k + 1) * BLOCK)
                part = lax.dot_general(dsb[rows], vn[rows], NT, preferred_element_type=F32)
                dw = part if dw is None else dw + part
                rowsum = jnp.sum(ds[rows], axis=-1, keepdims=True)
                db = rowsum if db is None else db + rowsum
                dvn.append(lax.dot_general(w, dsb[rows], TN, preferred_element_type=F32))
            dvn = dvn[0] if nch == 1 else jnp.concatenate(dvn, axis=0)
            dw_ref[g] += jnp.where(tri, dw, 0.0)
            db_ref[g] += db
            dxh = dvn * lng_ref[:, sl]
            dvs.append(r * (dxh - jnp.mean(dxh, axis=-1, keepdims=True)
                            - xh * jnp.mean(dxh * xh, axis=-1, keepdims=True)))
            dlng.append(jnp.sum(dvn * xh, axis=0, keepdims=True))
            dlnb.append(jnp.sum(dvn, axis=0, keepdims=True))
            ss.append(s)
        cat = lambda parts: parts[0] if gpb == 1 else jnp.concatenate(parts, axis=-1)
        dpu_ref[...] = (do * cat(ss) * _gelu_grad(pu)).astype(MXU)
        dpv_ref[...] = (cat(dvs) * _gelu_grad(pv)).astype(MXU)
        dlng_ref[...] += cat(dlng)
        dlnb_ref[...] += cat(dlnb)

    tile = pl.BlockSpec((tr, lb), lambda j, i: (i, j))
    vec = pl.BlockSpec((1, lb), lambda j, i: (0, j))
    half = jax.ShapeDtypeStruct((S, G * BLOCK), MXU)
    return pl.pallas_call(
        body, name="sgu_bwd", grid=(nlb, S // tr), in_specs=specs + [tile, ANY],
        out_specs=[tile, tile, pl.BlockSpec((gpb, BLOCK, BLOCK), lambda j, i: (j, 0, 0)),
                   pl.BlockSpec((gpb, BLOCK, 1), lambda j, i: (j, 0, 0)), vec, vec],
        out_shape=[half, half, jax.ShapeDtypeStruct((G, BLOCK, BLOCK), F32),
                   jax.ShapeDtypeStruct((G, BLOCK, 1), F32),
                   jax.ShapeDtypeStruct((1, G * BLOCK), F32), jax.ShapeDtypeStruct((1, G * BLOCK), F32)],
        compiler_params=_params())(proj, proj, lng, lnb, ws, bs, dsgu, after)


def _store_f32(vals, extra, outs):
    for v, o in zip(vals, outs):
        o[...] = v


def _store_mxu(vals, extra, outs):
    for v, o in zip(vals, outs):
        o[...] = v.astype(MXU)


def _proj_in(h, w):
    S, D = h.shape
    Ns = w.shape[2]
    tm, tn = _tile(S, 1024), _tile(Ns, 1024)
    npb = Ns // tn
    return _mm("proj_in", (S // tm, N_CHIPS, npb), 0, [h, w],
               [pl.BlockSpec((tm, D), lambda i, s, j: (i, 0)), pl.BlockSpec((None, D, tn), lambda i, s, j: (s, 0, j))],
               [(0, 1, 0)], NN, 0, [jax.ShapeDtypeStruct((S, N_CHIPS * Ns), F32)],
               [pl.BlockSpec((tm, tn), lambda i, s, j: (i, s * npb + j))], [None], _store_f32)[0]


def _branches(attn, sgu, wa, ws, proj, gate0):
    S, AW = attn.shape
    SW = sgu.shape[1]
    Nb = wa.shape[2]
    D = N_CHIPS * Nb
    tm = _tile(S, 512)
    assert gate0 % Nb == 0
    ga, gb = gate0 // Nb, (gate0 + D) // Nb

    def epilogue(vals, extra, outs):
        a, b = vals
        outs[0][...] = (_sigmoid(extra[0][...]) * a + _sigmoid(extra[1][...]) * b).astype(MXU)
        outs[1][...] = a
        outs[2][...] = b

    tile = pl.BlockSpec((tm, Nb), lambda i, s: (i, s))
    wspec = lambda k: pl.BlockSpec((None, k, Nb), lambda i, s: (s, 0, 0))
    f = jax.ShapeDtypeStruct((S, D), F32)
    return _mm("branches", (S // tm, N_CHIPS), 0, [attn, sgu, wa, ws, proj, proj],
               [pl.BlockSpec((tm, AW), lambda i, s: (i, 0)), pl.BlockSpec((tm, SW), lambda i, s: (i, 0)),
                wspec(AW), wspec(SW), pl.BlockSpec((tm, Nb), lambda i, s: (i, ga + s)),
                pl.BlockSpec((tm, Nb), lambda i, s: (i, gb + s))],
               [(0, 2, 0), (1, 3, 1)], NN, 2, [jax.ShapeDtypeStruct((S, D), MXU), f, f], [tile] * 3,
               [None, None], epilogue, chunk=MXU_CHUNK)


def _rows_mm(name, a, w, res):
    S = a.shape[0]
    _, K, N = w.shape
    tm, tn = _tile(S, 1024), _tile(N, 1024)

    def epilogue(vals, extra, outs):
        outs[0][...] = extra[0][...] + vals[0]

    out = pl.BlockSpec((tm, tn), lambda i, j, s: (i, j))
    return _mm(name, (S // tm, N // tn, N_CHIPS), 1, [a, w, res],
               [pl.BlockSpec((tm, K), lambda i, j, s: (i, s)), pl.BlockSpec((None, K, tn), lambda i, j, s: (s, 0, j)), out],
               [(0, 1, 0)], NN, 1, [jax.ShapeDtypeStruct((S, N), F32)], [out], [(tm, tn)], epilogue)[0]


def _gate_up(h2, wg, wu):
    S, D = h2.shape
    Nf = wg.shape[2]
    tm = _tile(S, 256)

    def epilogue(vals, extra, outs):
        g, u = vals
        outs[0][...] = g
        outs[1][...] = u
        outs[2][...] = (g * _sigmoid(g) * u).astype(MXU)

    w = pl.BlockSpec((None, D, Nf), lambda s, i: (s, 0, 0))
    o = pl.BlockSpec((tm, Nf), lambda s, i: (i, s))
    f = jax.ShapeDtypeStruct((S, N_CHIPS * Nf), F32)
    return _mm("gate_up", (N_CHIPS, S // tm), 0, [h2, wg, wu],
               [pl.BlockSpec((tm, D), lambda s, i: (i, 0)), w, w], [(0, 1, 0), (0, 2, 1)], NN, 0,
               [f, f, jax.ShapeDtypeStruct((S, N_CHIPS * Nf), MXU)], [o, o, o], [None, None], epilogue,
               chunk=MXU_CHUNK)


def _down_bwd(dyb, wd, g, u):
    S, D = dyb.shape
    Kf = wd.shape[1]
    tm = _tile(S, 512)

    def epilogue(vals, extra, outs):
        da, gv, uv = vals[0], extra[0][...], extra[1][...]
        sg = _sigmoid(gv)
        outs[0][...] = (da * uv * sg * (1.0 + gv * (1.0 - sg))).astype(MXU)
        outs[1][...] = (da * gv * sg).astype(MXU)

    t = pl.BlockSpec((tm, Kf), lambda i, s: (i, s))
    o = jax.ShapeDtypeStruct((S, N_CHIPS * Kf), MXU)
    return _mm("down_bwd", (S // tm, N_CHIPS), 0, [dyb, wd, g, u],
               [pl.BlockSpec((tm, D), lambda i, s: (i, 0)), pl.BlockSpec((None, Kf, D), lambda i, s: (s, 0, 0)), t, t],
               [(0, 1, 0)], NT, 2, [o, o], [t, t], [None], epilogue, chunk=MXU_CHUNK)


def _out_bwd(dxb, wo, proj, ba, bb, gate0):
    S, D = dxb.shape
    Ko = wo.shape[1]
    tm = _tile(S, 512)
    assert gate0 % Ko == 0
    ga, gb = gate0 // Ko, (gate0 + D) // Ko

    def epilogue(vals, extra, outs):
        dm = vals[0]
        sa, sb = _sigmoid(extra[0][...]), _sigmoid(extra[1][...])
        outs[0][...] = (dm * sa).astype(MXU)
        outs[1][...] = (dm * sb).astype(MXU)
        outs[2][...] = (dm * extra[2][...] * sa * (1.0 - sa)).astype(MXU)
        outs[3][...] = (dm * extra[3][...] * sb * (1.0 - sb)).astype(MXU)

    t = pl.BlockSpec((tm, Ko), lambda i, s: (i, s))
    o = jax.ShapeDtypeStruct((S, D), MXU)
    return _mm("out_bwd", (S // tm, N_CHIPS), 0, [dxb, wo, proj, proj, ba, bb],
               [pl.BlockSpec((tm, D), lambda i, s: (i, 0)), pl.BlockSpec((None, Ko, D), lambda i, s: (s, 0, 0)),
                pl.BlockSpec((tm, Ko), lambda i, s: (i, ga + s)), pl.BlockSpec((tm, Ko), lambda i, s: (i, gb + s)), t, t],
               [(0, 1, 0)], NT, 4, [o] * 4, [t] * 4, [None], epilogue, chunk=MXU_CHUNK)


def _dx_cols(name, terms, n_out, after=None):
    S = terms[0][0].shape[0]
    _, K, Ns = terms[0][1].shape
    tm, tko, tn = _tile(S, 1024), _tile(K, 1024), _tile(Ns, 1920 if len(terms) == 1 else 1408)
    npb = Ns // tn
    operands, specs, pairs = [], [], []
    for t, (dy, w, k) in enumerate(terms):
        assert w.shape == (N_CHIPS, K, Ns)
        operands += [dy, w]
        specs += [pl.BlockSpec((tm, tn), lambda i, jk, s, jn: (i, s * npb + jn)),
                  pl.BlockSpec((None, tko, tn), lambda i, jk, s, jn: (s, jk, jn))]
        pairs.append((2 * t, 2 * t + 1, k))
    out = pl.BlockSpec((tm, tko), lambda i, jk, s, jn: (i, jk))
    return _mm(name, (S // tm, K // tko, N_CHIPS, npb), 2, operands, specs, pairs, NT, 0,
               [jax.ShapeDtypeStruct((S, K), F32)] * n_out, [out] * n_out, [(tm, tko)] * n_out, _store_f32, after)


def _dw_cols(name, a, dy):
    S, K = a.shape
    Ns = dy.shape[1] // N_CHIPS
    tk, tn = _tile(K, 512), _tile(Ns, 1408)
    npb = Ns // tn
    return _mm(name, (K // tk, N_CHIPS, npb), 0, [a, dy],
               [pl.BlockSpec((S, tk), lambda jk, s, jn: (0, jk)), pl.BlockSpec((S, tn), lambda jk, s, jn: (0, s * npb + jn))],
               [(0, 1, 0)], TN, 0, [jax.ShapeDtypeStruct((N_CHIPS, K, Ns), MXU)],
               [pl.BlockSpec((None, tk, tn), lambda jk, s, jn: (s, jk, jn))], [None], _store_mxu)[0]


def _dw_rows(name, a, dy):
    S = a.shape[0]
    K = a.shape[1] // N_CHIPS
    N = dy.shape[1]
    tk, tn = _tile(K, 1408), _tile(N, 1024)
    nkb = K // tk
    return _mm(name, (N_CHIPS, nkb, N // tn), 0, [a, dy],
               [pl.BlockSpec((S, tk), lambda s, jk, jn: (0, s * nkb + jk)), pl.BlockSpec((S, tn), lambda s, jk, jn: (0, jn))],
               [(0, 1, 0)], TN, 0, [jax.ShapeDtypeStruct((N_CHIPS, K, N), MXU)],
               [pl.BlockSpec((None, tk, tn), lambda s, jk, jn: (s, jk, jn))], [None], _store_mxu)[0]


def _layer_fwd(x, stream, layer, last, sp, cos, sin, dims):
    AW, KW, gate0, u_col = dims
    h = _rms_fwd("mix_norm", x, sp["mix_norm"])
    w = stream.finish(layer, 0, h)
    proj = _proj_in(h, w["w_in"])
    qr, kr, vb = _qk_prep(proj, sp["q_norm"], sp["k_norm"], cos, sin, AW, KW)
    stream.forward(layer, 1, qr)
    attn = _attn_fwd(qr, kr, vb, sp["sinks"])
    w.update(stream.finish(layer, 1, attn))
    sgu = _sgu_fwd(proj, sp["sgu_ln_g"], sp["sgu_ln_b"], sp["w_spatial"], sp["b_spatial"], u_col)
    merged, ba, bb = _branches(attn, sgu, w["w_attn_branch"], w["w_sgu_branch"], proj, gate0)
    stream.forward(layer, 2, merged)
    x1 = _rows_mm("out_proj", merged, w["w_out"], x)
    w.update(stream.finish(layer, 2, x1))
    h2 = _rms_fwd("ffn_norm", x1, sp["ffn_norm"])
    stream.forward(layer, 3, h2)
    g, u, act = _gate_up(h2, w["w_gate"], w["w_up"])
    w.update(stream.finish(layer, 3, g))
    x2 = _rows_mm("down_proj", act, w["w_down"], x1)
    if not last:
        stream.forward(layer + 1, 0, x2)
    saved = dict(x=x, h=h, proj=proj, qr=qr, kr=kr, vb=vb, attn=attn, sgu=sgu, merged=merged, ba=ba, bb=bb,
                 x1=x1, h2=h2, g=g, u=u, act=act)
    return x2, saved, w


def _layer_bwd(dy, dyb, w, sp, sv, cos, sin, dims, reducer, layer):
    AW, KW, gate0, u_col = dims
    big, small = {}, {}
    dg, du = _down_bwd(dyb, w["w_down"], sv["g"], sv["u"])
    big["w_down"] = _dw_rows("dw_down", sv["act"], dyb)
    big["w_gate"] = _dw_cols("dw_gate", sv["h2"], dg)
    big["w_up"] = _dw_cols("dw_up", sv["h2"], du)
    token = reducer.start(layer, 2, big)
    dh2 = _dx_cols("dh2", [(dg, w["w_gate"], 0), (du, w["w_up"], 0)], 1, token)[0]
    token = reducer.scatter(layer, 2, dh2)
    dx1, dx1b, small["ffn_norm"] = _rms_bwd("ffn_norm_bwd", dh2, sv["x1"], sp["ffn_norm"], dy, token)
    dba, dbb, dgla, dglb = _out_bwd(dx1b, w["w_out"], sv["proj"], sv["ba"], sv["bb"], gate0)
    big["w_out"] = _dw_rows("dw_out", sv["merged"], dx1b)
    big["w_attn_branch"] = _dw_cols("dw_attn_branch", sv["attn"], dba)
    big["w_sgu_branch"] = _dw_cols("dw_sgu_branch", sv["sgu"], dbb)
    token = reducer.start(layer, 1, big)
    dattn, dsgu = _dx_cols("dbranch_in", [(dba, w["w_attn_branch"], 0), (dbb, w["w_sgu_branch"], 1)], 2, token)
    token = reducer.scatter(layer, 1, dsgu)
    dpu, dpv, small["w_spatial"], db, small["sgu_ln_g"], small["sgu_ln_b"] = _sgu_bwd(
        sv["proj"], sp["sgu_ln_g"], sp["sgu_ln_b"], sp["w_spatial"], sp["b_spatial"], dsgu, u_col, token)
    small["b_spatial"] = db[:, :, 0]
    dq, dkp, dkc, dvp, dvc, dsink = _attn_bwd(sv["qr"], sv["kr"], sv["vb"], sp["sinks"], dattn)
    small["sinks"] = dsink[:, :sp["sinks"].shape[1]]
    dqkv, dqg, dkg = _qk_prep_bwd(sv["proj"], sp["q_norm"], sp["k_norm"], cos, sin, dq, dkp, dkc, dvp, dvc, AW, KW)
    small["q_norm"] = dqg[:, :HEAD_DIM]
    small["k_norm"] = dkg[:, :HEAD_DIM]
    dproj = _join_columns("dproj", [dqkv, dpu, dpv, dgla, dglb])
    big["w_in"] = _dw_cols("dw_in", sv["h"], dproj)
    token = reducer.start(layer, 0, big)
    dh = _dx_cols("dh", [(dproj, w["w_in"], 0)], 1, token)[0]
    token = reducer.scatter(layer, 0, dh)
    dx, dxb, small["mix_norm"] = _rms_bwd("mix_norm_bwd", dh, sv["x"], sp["mix_norm"], dx1, token)
    return dx, dxb, small


def _place():
    x, y, c = lax.axis_index("x"), lax.axis_index("y"), lax.axis_index("c")
    chips = [(1 - x, y), (x, 1 - y), (1 - x, 1 - y)]
    return x, y, c, chips


def _half_rows(c, rows):
    h = rows // 2
    assert h % 16 == 0
    return pl.ds(pl.multiple_of(c * h, 16), h)


def _row_tile(rows, pref):
    best = None
    for t in range(16, min(rows, pref) + 1, 16):
        if rows % t == 0:
            best = t
    assert best is not None, rows
    return best


def _cast_own(name, chip, w, layer):
    _, R, C = w.shape
    tr = _row_tile(R, 512)

    def body(chip_ref, w_ref, o_ref):
        o_ref[...] = w_ref[...].astype(MXU)

    return pl.pallas_call(
        body, name=name, out_shape=jax.ShapeDtypeStruct((N_CHIPS, R, C), MXU),
        grid_spec=pltpu.PrefetchScalarGridSpec(
            num_scalar_prefetch=1, grid=(R // tr,),
            in_specs=[pl.BlockSpec((None, tr, C), lambda i, chip_ref: (layer, i, 0))],
            out_specs=pl.BlockSpec((None, tr, C), lambda i, chip_ref: (chip_ref[0], i, 0))),
        compiler_params=_params())(chip, w)


HBM = pl.BlockSpec(memory_space=pltpu.HBM)
SEM = pl.BlockSpec(memory_space=pltpu.SEMAPHORE)
DATAFLOW = pltpu.SideEffectType.DATAFLOW_SIDE_EFFECTING


def _gather_copies(bufs, send_sem, recv_sem):
    x, y, c, chips = _place()

    def ici(a, j, block):
        px, py = chips[j]
        blk = bufs[a].at[block, _half_rows(c, bufs[a].shape[1])]
        return pltpu.make_async_remote_copy(
            src_ref=blk, dst_ref=blk, send_sem=send_sem.at[3 * a + j], recv_sem=recv_sem.at[3 * a + j],
            device_id=(px, py, c), device_id_type=MESH)

    def d2d(a, j, core):
        px, py = chips[j]
        blk = bufs[a].at[2 * px + py, _half_rows(core, bufs[a].shape[1])]
        return pltpu.make_async_remote_copy(
            src_ref=blk, dst_ref=blk, send_sem=send_sem.at[3 * a + j], recv_sem=recv_sem.at[3 * a + j],
            device_id=(x, y, 1 - c), device_id_type=MESH)

    return ici, d2d


def _in_hbm(bufs):
    return [pltpu.with_memory_space_constraint(b, pltpu.HBM) for b in bufs]


def _gather_start(name, bufs, after):
    n = len(bufs)

    def body(*refs):
        dst = refs[n + 1:2 * n + 1]
        send_sem, recv_sem, token = refs[2 * n + 1:]
        x, y, c, chips = _place()
        ici, _ = _gather_copies(dst, send_sem, recv_sem)
        for a in range(n):
            for j in range(3):
                ici(a, j, 2 * x + y).start()
        token[...] = jnp.zeros(token.shape, token.dtype)

    sems = pltpu.SemaphoreType.DMA((3 * n,))
    outs = pl.pallas_call(
        body, name=name, in_specs=[HBM] * n + [ANY],
        out_specs=[HBM] * n + [SEM, SEM, pl.BlockSpec(memory_space=pltpu.VMEM)],
        out_shape=[pltpu.HBM(b.shape, b.dtype) for b in bufs] + [sems, sems, jax.ShapeDtypeStruct((8, BLOCK), F32)],
        input_output_aliases={a: a for a in range(n)},
        compiler_params=pltpu.CompilerParams(has_side_effects=DATAFLOW))(*_in_hbm(bufs), after)
    return outs[:n], outs[n], outs[n + 1], outs[n + 2]


def _gather_forward(name, bufs, ici_send, ici_recv, after):
    n = len(bufs)

    def body(*refs):
        ici_send_ref, ici_recv_ref = refs[n], refs[n + 1]
        dst = refs[n + 3:2 * n + 3]
        d2d_send, d2d_recv = refs[2 * n + 3:]
        x, y, c, chips = _place()
        ici, _ = _gather_copies(dst, ici_send_ref, ici_recv_ref)
        _, d2d = _gather_copies(dst, d2d_send, d2d_recv)
        for a in range(n):
            for j, (px, py) in enumerate(chips):
                ici(a, j, 2 * px + py).wait_recv()
                d2d(a, j, c).start()
        for a in range(n):
            for j in range(3):
                ici(a, j, 2 * x + y).wait_send()

    sems = pltpu.SemaphoreType.DMA((3 * n,))
    outs = pl.pallas_call(
        body, name=name, in_specs=[HBM] * n + [SEM, SEM, ANY], out_specs=[HBM] * n + [SEM, SEM],
        out_shape=[pltpu.HBM(b.shape, b.dtype) for b in bufs] + [sems, sems],
        input_output_aliases={a: a for a in range(n)},
        compiler_params=pltpu.CompilerParams(has_side_effects=DATAFLOW))(*bufs, ici_send, ici_recv, after)
    return outs[:n], outs[n], outs[n + 1]


def _gather_finish(name, bufs, d2d_send, d2d_recv, after):
    n = len(bufs)

    def body(*refs):
        send_ref, recv_ref = refs[n], refs[n + 1]
        dst = refs[n + 3:]
        x, y, c, chips = _place()
        _, d2d = _gather_copies(dst, send_ref, recv_ref)
        for a in range(n):
            for j in range(3):
                d2d(a, j, 1 - c).wait_recv()
                d2d(a, j, c).wait_send()

    return pl.pallas_call(
        body, name=name, in_specs=[HBM] * n + [SEM, SEM, ANY], out_specs=[HBM] * n,
        out_shape=[pltpu.HBM(b.shape, b.dtype) for b in bufs],
        input_output_aliases={a: a for a in range(n)},
        compiler_params=pltpu.CompilerParams(has_side_effects=DATAFLOW))(*bufs, d2d_send, d2d_recv, after)


GATHER = (("w_in",), ("w_attn_branch", "w_sgu_branch", "w_out"), ("w_gate", "w_up"), ("w_down",))
REDUCE = (("w_in",), ("w_attn_branch", "w_sgu_branch", "w_out"), ("w_gate", "w_up", "w_down"))


class _WeightStream:
    def __init__(self, started):
        self.started, self.passed = started, {}

    def forward(self, layer, group, after):
        bufs, send, recv = self.started[(layer, group)]
        self.passed[(layer, group)] = _gather_forward("gather_forward_%d_%d" % (layer, group), bufs, send, recv, after)

    def finish(self, layer, group, after):
        bufs, send, recv = self.passed[(layer, group)]
        done = _gather_finish("gather_finish_%d_%d" % (layer, group), bufs, send, recv, after)
        return dict(zip(GATHER[group], done))


def _pair_copies(grads, lands, send_sem, recv_sem):
    x, y, c, _ = _place()

    def make(a):
        theirs = _half_rows(1 - c, grads[a].shape[1])
        return pltpu.make_async_remote_copy(
            src_ref=grads[a].at[:, theirs], dst_ref=lands[a], send_sem=send_sem.at[a], recv_sem=recv_sem.at[a],
            device_id=(x, y, 1 - c), device_id_type=MESH)

    return make


def _pair_start(name, grads, after):
    n = len(grads)
    lands = [lax.empty((g.shape[0], g.shape[1] // 2, g.shape[2]), g.dtype) for g in grads]

    def body(*refs):
        src, dst = refs[2 * n + 1:3 * n + 1], refs[3 * n + 1:4 * n + 1]
        send_sem, recv_sem, token = refs[4 * n + 1:]
        copy = _pair_copies(src, dst, send_sem, recv_sem)
        for a in range(n):
            copy(a).start()
        token[...] = jnp.zeros(token.shape, token.dtype)

    sems = pltpu.SemaphoreType.DMA((n,))
    outs = pl.pallas_call(
        body, name=name, in_specs=[HBM] * (2 * n) + [ANY],
        out_specs=[HBM] * (2 * n) + [SEM, SEM, pl.BlockSpec(memory_space=pltpu.VMEM)],
        out_shape=[pltpu.HBM(b.shape, b.dtype) for b in grads + lands] + [sems, sems, jax.ShapeDtypeStruct((8, BLOCK), F32)],
        input_output_aliases={a: a for a in range(2 * n)},
        compiler_params=pltpu.CompilerParams(has_side_effects=DATAFLOW))(*_in_hbm(grads + lands), after)
    return outs[:n], outs[n:2 * n], outs[2 * n], outs[2 * n + 1], outs[2 * n + 2]


def _pair_finish(name, grads, lands, send_sem, recv_sem, after):
    n = len(grads)

    def body(*refs):
        send_ref, recv_ref = refs[2 * n], refs[2 * n + 1]
        src, dst = refs[2 * n + 3:3 * n + 3], refs[3 * n + 3:]
        copy = _pair_copies(src, dst, send_ref, recv_ref)
        for a in range(n):
            copy(a).wait_send()
            copy(a).wait_recv()

    outs = pl.pallas_call(
        body, name=name, in_specs=[HBM] * (2 * n) + [SEM, SEM, ANY], out_specs=[HBM] * (2 * n),
        out_shape=[pltpu.HBM(b.shape, b.dtype) for b in grads + lands],
        input_output_aliases={a: a for a in range(2 * n)},
        compiler_params=pltpu.CompilerParams(has_side_effects=DATAFLOW))(*grads, *lands, send_sem, recv_sem, after)
    return outs[:n], outs[n:]


def _pair_sum(name, core, g, p):
    _, h, C = p.shape
    tr = _row_tile(h, 512)
    nrb = h // tr

    def body(core_ref, g_ref, p_ref, o_ref):
        o_ref[...] = (g_ref[...].astype(F32) + p_ref[...].astype(F32)).astype(o_ref.dtype)

    spec = pl.BlockSpec((None, tr, C), lambda s, i, core_ref: (s, i, 0))
    return pl.pallas_call(
        body, name=name, out_shape=jax.ShapeDtypeStruct(p.shape, p.dtype),
        grid_spec=pltpu.PrefetchScalarGridSpec(
            num_scalar_prefetch=1, grid=(N_CHIPS, nrb),
            in_specs=[pl.BlockSpec((None, tr, C), lambda s, i, core_ref: (s, core_ref[0] * nrb + i, 0)), spec],
            out_specs=spec),
        compiler_params=_params())(core, g, p)


def _scatter_copies(sums, slots, send_sem, recv_sem):
    x, y, c, chips = _place()

    def make(a, j):
        px, py = chips[j]
        return pltpu.make_async_remote_copy(
            src_ref=sums[a].at[2 * px + py], dst_ref=slots[a].at[j], send_sem=send_sem.at[3 * a + j],
            recv_sem=recv_sem.at[3 * a + j], device_id=(px, py, c), device_id_type=MESH)

    return make


def _scatter_start(name, sums, after):
    n = len(sums)
    slots = [lax.empty((3,) + s.shape[1:], s.dtype) for s in sums]

    def body(*refs):
        src, dst = refs[2 * n + 1:3 * n + 1], refs[3 * n + 1:4 * n + 1]
        send_sem, recv_sem, token = refs[4 * n + 1:]
        copy = _scatter_copies(src, dst, send_sem, recv_sem)
        for a in range(n):
            for j in range(3):
                copy(a, j).start()
        token[...] = jnp.zeros(token.shape, token.dtype)

    sems = pltpu.SemaphoreType.DMA((3 * n,))
    outs = pl.pallas_call(
        body, name=name, in_specs=[HBM] * (2 * n) + [ANY],
        out_specs=[HBM] * (2 * n) + [SEM, SEM, pl.BlockSpec(memory_space=pltpu.VMEM)],
        out_shape=[pltpu.HBM(b.shape, b.dtype) for b in sums + slots] + [sems, sems, jax.ShapeDtypeStruct((8, BLOCK), F32)],
        input_output_aliases={a: a for a in range(2 * n)},
        compiler_params=pltpu.CompilerParams(has_side_effects=DATAFLOW))(*_in_hbm(sums + slots), after)
    return outs[:n], outs[n:2 * n], outs[2 * n], outs[2 * n + 1], outs[2 * n + 2]


def _scatter_finish(name, sums, slots, send_sem, recv_sem, after):
    n = len(sums)

    def body(*refs):
        send_ref, recv_ref = refs[2 * n], refs[2 * n + 1]
        src, dst = refs[2 * n + 3:3 * n + 3], refs[3 * n + 3:]
        copy = _scatter_copies(src, dst, send_ref, recv_ref)
        for a in range(n):
            for j in range(3):
                copy(a, j).wait_send()
                copy(a, j).wait_recv()

    outs = pl.pallas_call(
        body, name=name, in_specs=[HBM] * (2 * n) + [SEM, SEM, ANY], out_specs=[HBM] * (2 * n),
        out_shape=[pltpu.HBM(b.shape, b.dtype) for b in sums + slots],
        input_output_aliases={a: a for a in range(2 * n)},
        compiler_params=pltpu.CompilerParams(has_side_effects=DATAFLOW))(*sums, *slots, send_sem, recv_sem, after)
    return outs[:n], outs[n:]


def _slot_sum(name, place, slots, sums):
    _, h, C = slots.shape
    tr = _row_tile(h, 512)
    nrb = h // tr

    def body(place_ref, r0, r1, r2, own, o_ref):
        o_ref[...] = ((r0[...].astype(F32) + r1[...].astype(F32)) + r2[...].astype(F32)) + own[...].astype(F32)

    slot = lambda k: pl.BlockSpec((None, tr, C), lambda i, place_ref: (k, i, 0))
    return pl.pallas_call(
        body, name=name, out_shape=jax.ShapeDtypeStruct((2 * h, C), F32),
        grid_spec=pltpu.PrefetchScalarGridSpec(
            num_scalar_prefetch=1, grid=(nrb,),
            in_specs=[slot(0), slot(1), slot(2),
                      pl.BlockSpec((None, tr, C), lambda i, place_ref: (place_ref[0], i, 0))],
            out_specs=pl.BlockSpec((tr, C), lambda i, place_ref: (place_ref[1] * nrb + i, 0))),
        compiler_params=_params())(place, slots, slots, slots, sums)


def _half_copies(bufs, send_sem, recv_sem):
    x, y, c, _ = _place()

    def make(a, core):
        rows = bufs[a].at[_half_rows(core, bufs[a].shape[0])]
        return pltpu.make_async_remote_copy(
            src_ref=rows, dst_ref=rows, send_sem=send_sem.at[a], recv_sem=recv_sem.at[a],
            device_id=(x, y, 1 - c), device_id_type=MESH)

    return make


def _half_start(name, bufs, after):
    n = len(bufs)

    def body(*refs):
        dst = refs[n + 1:2 * n + 1]
        send_sem, recv_sem, token = refs[2 * n + 1:]
        c = lax.axis_index("c")
        copy = _half_copies(dst, send_sem, recv_sem)
        for a in range(n):
            copy(a, c).start()
        token[...] = jnp.zeros(token.shape, token.dtype)

    sems = pltpu.SemaphoreType.DMA((n,))
    outs = pl.pallas_call(
        body, name=name, in_specs=[HBM] * n + [ANY],
        out_specs=[HBM] * n + [SEM, SEM, pl.BlockSpec(memory_space=pltpu.VMEM)],
        out_shape=[pltpu.HBM(b.shape, b.dtype) for b in bufs] + [sems, sems, jax.ShapeDtypeStruct((8, BLOCK), F32)],
        input_output_aliases={a: a for a in range(n)},
        compiler_params=pltpu.CompilerParams(has_side_effects=DATAFLOW))(*_in_hbm(bufs), after)
    return outs[:n], outs[n], outs[n + 1], outs[n + 2]


def _half_finish(name, bufs, send_sem, recv_sem, after):
    n = len(bufs)

    def body(*refs):
        send_ref, recv_ref = refs[n], refs[n + 1]
        dst = refs[n + 3:]
        c = lax.axis_index("c")
        copy = _half_copies(dst, send_ref, recv_ref)
        for a in range(n):
            copy(a, c).wait_send()
            copy(a, 1 - c).wait_recv()

    return pl.pallas_call(
        body, name=name, in_specs=[HBM] * n + [SEM, SEM, ANY], out_specs=[HBM] * n,
        out_shape=[pltpu.HBM(b.shape, b.dtype) for b in bufs],
        input_output_aliases={a: a for a in range(n)},
        compiler_params=pltpu.CompilerParams(has_side_effects=DATAFLOW))(*bufs, send_sem, recv_sem, after)


class _GradReducer:
    def __init__(self, chip, core):
        self.core, self.place, self.pairs, self.started = core, jnp.concatenate([chip, core]), {}, []

    def start(self, layer, group, grads):
        mine = [grads[n] for n in REDUCE[group]]
        mine, lands, send, recv, token = _pair_start("grad_pair_start_%d_%d" % (layer, group), mine, self.place)
        self.pairs[(layer, group)] = (mine, lands, send, recv)
        return token

    def scatter(self, layer, group, after):
        tag = "%d_%d" % (layer, group)
        names = REDUCE[group]
        mine, lands, send, recv = self.pairs.pop((layer, group))
        mine, theirs = _pair_finish("grad_pair_finish_" + tag, mine, lands, send, recv, after)
        sums = [_pair_sum("pair_sum_%s_%d" % (n, layer), self.core, g, p) for n, g, p in zip(names, mine, theirs)]
        sums, slots, send, recv, token = _scatter_start("grad_scatter_start_" + tag, sums, self.place)
        self.started.append((layer, names, sums, slots, send, recv))
        return token

    def finish(self, after, update):
        for layer in sorted({entry[0] for entry in self.started}, reverse=True):
            exchanged = []
            for lyr, names, sums, slots, send, recv in self.started:
                if lyr != layer:
                    continue
                tag = "%s_%d" % (names[0], layer)
                sums, slots = _scatter_finish("grad_scatter_finish_" + tag, sums, slots, send, recv, after)
                halves = [_slot_sum("slot_sum_%s_%d" % (n, layer), self.place, r, s)
                          for n, r, s in zip(names, slots, sums)]
                halves, send, recv, after = _half_start("grad_half_start_" + tag, halves, self.place)
                exchanged.append((tag, names, halves, send, recv))
            for tag, names, halves, send, recv in exchanged:
                whole = _half_finish("grad_half_finish_" + tag, halves, send, recv, after)
                after = update(layer, dict(zip(names, whole)))


def _small_copies(arrays, lands, own_sems, pass_sems):
    def slot(a, block):
        px, py, pc = block
        return lands[a].at[4 * px + 2 * py + pc]

    def own(a, k, block, to):
        return pltpu.make_async_remote_copy(
            src_ref=arrays[a], dst_ref=slot(a, block), send_sem=own_sems[0].at[4 * a + k],
            recv_sem=own_sems[1].at[4 * a + k], device_id=to, device_id_type=MESH)

    def passed(a, j, block, to):
        return pltpu.make_async_remote_copy(
            src_ref=slot(a, block), dst_ref=slot(a, block), send_sem=pass_sems[0].at[3 * a + j],
            recv_sem=pass_sems[1].at[3 * a + j], device_id=to, device_id_type=MESH)

    return own, passed


def _small_start(arrays, after):
    n = len(arrays)
    lands = [lax.empty((2 * N_CHIPS,) + a.shape, a.dtype) for a in arrays]

    def body(*refs):
        src, dst = refs[2 * n + 1:3 * n + 1], refs[3 * n + 1:4 * n + 1]
        send_sem, recv_sem, token = refs[4 * n + 1:]
        x, y, c, chips = _place()
        own, _ = _small_copies(src, dst, (send_sem, recv_sem), None)
        for a in range(n):
            own(a, 0, (x, y, c), (x, y, 1 - c)).start()
            for j, (px, py) in enumerate(chips):
                own(a, 1 + j, (x, y, c), (px, py, c)).start()
        token[...] = jnp.zeros(token.shape, token.dtype)

    sems = pltpu.SemaphoreType.DMA((4 * n,))
    outs = pl.pallas_call(
        body, name="small_grad_start", in_specs=[HBM] * (2 * n) + [ANY],
        out_specs=[HBM] * (2 * n) + [SEM, SEM, pl.BlockSpec(memory_space=pltpu.VMEM)],
        out_shape=[pltpu.HBM(b.shape, b.dtype) for b in arrays + lands]
        + [sems, sems, jax.ShapeDtypeStruct((8, BLOCK), F32)],
        input_output_aliases={a: a for a in range(2 * n)},
        compiler_params=pltpu.CompilerParams(has_side_effects=DATAFLOW))(*_in_hbm(arrays + lands), after)
    return outs[:n], outs[n:2 * n], (outs[2 * n], outs[2 * n + 1]), outs[2 * n + 2]


def _small_forward(arrays, lands, own_sems, after):
    n = len(arrays)

    def body(*refs):
        own_refs = (refs[2 * n], refs[2 * n + 1])
        src, dst = refs[2 * n + 3:3 * n + 3], refs[3 * n + 3:4 * n + 3]
        pass_refs = (refs[4 * n + 3], refs[4 * n + 4])
        x, y, c, chips = _place()
        own, passed = _small_copies(src, dst, own_refs, pass_refs)
        for a in range(n):
            for j, (px, py) in enumerate(chips):
                own(a, 1 + j, (px, py, c), (x, y, c)).wait_recv()
                passed(a, j, (px, py, c), (x, y, 1 - c)).start()
        for a in range(n):
            own(a, 0, (x, y, 1 - c), (x, y, c)).wait_recv()
            own(a, 0, (x, y, c), (x, y, 1 - c)).wait_send()
            for j, (px, py) in enumerate(chips):
                own(a, 1 + j, (x, y, c), (px, py, c)).wait_send()

    sems = pltpu.SemaphoreType.DMA((3 * n,))
    outs = pl.pallas_call(
        body, name="small_grad_forward", in_specs=[HBM] * (2 * n) + [SEM, SEM, ANY],
        out_specs=[HBM] * (2 * n) + [SEM, SEM],
        out_shape=[pltpu.HBM(b.shape, b.dtype) for b in arrays + lands] + [sems, sems],
        input_output_aliases={a: a for a in range(2 * n)},
        compiler_params=pltpu.CompilerParams(has_side_effects=DATAFLOW))(*arrays, *lands, *own_sems, after)
    return outs[:n], outs[n:2 * n], (outs[2 * n], outs[2 * n + 1])


def _small_finish(arrays, lands, pass_sems, after):
    n = len(arrays)

    def body(*refs):
        pass_refs = (refs[2 * n], refs[2 * n + 1])
        src, dst = refs[2 * n + 3:3 * n + 3], refs[3 * n + 3:]
        x, y, c, chips = _place()
        _, passed = _small_copies(src, dst, None, pass_refs)
        for a in range(n):
            for j, (px, py) in enumerate(chips):
                passed(a, j, (px, py, 1 - c), (x, y, c)).wait_recv()
                passed(a, j, (px, py, c), (x, y, 1 - c)).wait_send()

    outs = pl.pallas_call(
        body, name="small_grad_finish", in_specs=[HBM] * (2 * n) + [SEM, SEM, ANY], out_specs=[HBM] * (2 * n),
        out_shape=[pltpu.HBM(b.shape, b.dtype) for b in arrays + lands],
        input_output_aliases={a: a for a in range(2 * n)},
        compiler_params=pltpu.CompilerParams(has_side_effects=DATAFLOW))(*arrays, *lands, *pass_sems, after)
    return outs[:n], outs[n:]


def _small_sum(place, arrays, lands):
    n = len(arrays)
    n_dev = 2 * N_CHIPS

    def body(place_ref, *refs):
        me = 2 * place_ref[0] + place_ref[1]
        for a in range(n):
            own, land, out = refs[a], refs[n + a], refs[2 * n + a]
            acc = None
            for d in range(n_dev):
                term = jnp.where(me == d, own[...], land[jnp.where(me == d, d ^ 1, d)])
                acc = term if acc is None else acc + term
            out[...] = acc

    vm = pl.BlockSpec(memory_space=pltpu.VMEM)
    return pl.pallas_call(
        body, name="small_grad_sum", in_specs=[pl.BlockSpec(memory_space=pltpu.SMEM)] + [vm] * (2 * n),
        out_specs=[vm] * n, out_shape=[jax.ShapeDtypeStruct(a.shape, F32) for a in arrays],
        compiler_params=_params())(place, *arrays, *lands)


def _adamw_math(w, g, m, v):
    m2 = ADAM_B1 * m + (1.0 - ADAM_B1) * g
    v2 = ADAM_B2 * v + (1.0 - ADAM_B2) * (g * g)
    m_hat = m2 / (1.0 - ADAM_B1 ** ADAM_STEP)
    v_hat = v2 / (1.0 - ADAM_B2 ** ADAM_STEP)
    delta = -ADAM_LR * (m_hat / (jnp.sqrt(v_hat) + ADAM_EPS) + ADAM_WD * w)
    return delta, m2, v2


def _adamw_big(name, layer, grad, w, m, v, others):
    L, R, C = w.shape
    tr = _row_tile(R, 256)

    def body(g_ref, w_ref, m_ref, v_ref, *rest):
        go_ref, d_ref, mo_ref, vo_ref = rest[-4:]
        g = g_ref[...]
        delta, m2, v2 = _adamw_math(w_ref[...], g, m_ref[...], v_ref[...])
        go_ref[...] = g
        d_ref[...] = delta
        mo_ref[...] = m2
        vo_ref[...] = v2

    blk = pl.BlockSpec((None, tr, C), lambda i: (layer, i, 0))
    shp = jax.ShapeDtypeStruct(w.shape, F32)
    others = [] if others is None else list(others)
    return pl.pallas_call(
        body, name=name, grid=(R // tr,),
        in_specs=[pl.BlockSpec((tr, C), lambda i: (i, 0))] + [blk] * 3 + [ANY] * len(others), out_specs=[blk] * 4,
        out_shape=[shp] * 4, input_output_aliases={4 + k: k for k in range(len(others))},
        compiler_params=_params())(grad, w, m, v, *others)


def _adamw_small(gs, ws, ms, vs):
    n = len(gs)

    def body(*refs):
        for a in range(n):
            g_ref, w_ref, m_ref, v_ref = refs[a], refs[n + a], refs[2 * n + a], refs[3 * n + a]
            delta, m2, v2 = _adamw_math(w_ref[...], g_ref[...], m_ref[...], v_ref[...])
            refs[4 * n + a][...] = delta
            refs[5 * n + a][...] = m2
            refs[6 * n + a][...] = v2

    vm = pl.BlockSpec(memory_space=pltpu.VMEM)
    shapes = [jax.ShapeDtypeStruct(g.shape, F32) for g in gs]
    outs = pl.pallas_call(
        body, name="adamw_small", in_specs=[vm] * (4 * n), out_specs=[vm] * (3 * n), out_shape=shapes * 3,
        compiler_params=_params())(*gs, *ws, *ms, *vs)
    return outs[:n], outs[n:2 * n], outs[2 * n:]


def _rows2d(a):
    return a if a.ndim == 2 else a.reshape(-1, a.shape[-1])


BIG = ("w_in", "w_attn_branch", "w_sgu_branch", "w_out", "w_gate", "w_up", "w_down")
SMALL = ("mix_norm", "q_norm", "k_norm", "sinks", "sgu_ln_g", "sgu_ln_b", "w_spatial", "b_spatial", "ffn_norm")
ORDER = ("mix_norm", "w_in", "q_norm", "k_norm", "sinks", "sgu_ln_g", "sgu_ln_b", "w_spatial", "b_spatial",
         "w_attn_branch", "w_sgu_branch", "w_out", "ffn_norm", "w_gate", "w_up", "w_down")


def _rope_tables(seq):
    pos = jnp.arange(seq, dtype=F32)
    inv_freq = jnp.power(10000.0, -jnp.arange(0, HEAD_DIM, 2, dtype=F32) / HEAD_DIM)
    ang = pos[:, None] * inv_freq[None, :]
    cos, sin = jnp.cos(ang), jnp.sin(ang)
    reps = BLOCK // HEAD_DIM
    return (jnp.tile(jnp.concatenate([cos, cos], axis=1), (1, reps)),
            jnp.tile(jnp.concatenate([-sin, sin], axis=1), (1, reps)))


def kernel(x, mix_norm, w_in, q_norm, k_norm, sinks, sgu_ln_g, sgu_ln_b, w_spatial, b_spatial, w_attn_branch, w_sgu_branch, w_out, ffn_norm, w_gate, w_up, w_down, loss_target, m_mix_norm, m_w_in, m_q_norm, m_k_norm, m_sinks, m_sgu_ln_g, m_sgu_ln_b, m_w_spatial, m_b_spatial, m_w_attn_branch, m_w_sgu_branch, m_w_out, m_ffn_norm, m_w_gate, m_w_up, m_w_down, v_mix_norm, v_w_in, v_q_norm, v_k_norm, v_sinks, v_sgu_ln_g, v_sgu_ln_b, v_w_spatial, v_b_spatial, v_w_attn_branch, v_w_sgu_branch, v_w_out, v_ffn_norm, v_w_gate, v_w_up, v_w_down):
    weights = dict(mix_norm=mix_norm, w_in=w_in, q_norm=q_norm, k_norm=k_norm, sinks=sinks, sgu_ln_g=sgu_ln_g,
                   sgu_ln_b=sgu_ln_b, w_spatial=w_spatial, b_spatial=b_spatial, w_attn_branch=w_attn_branch,
                   w_sgu_branch=w_sgu_branch, w_out=w_out, ffn_norm=ffn_norm, w_gate=w_gate, w_up=w_up, w_down=w_down)
    mom1 = dict(mix_norm=m_mix_norm, w_in=m_w_in, q_norm=m_q_norm, k_norm=m_k_norm, sinks=m_sinks,
                sgu_ln_g=m_sgu_ln_g, sgu_ln_b=m_sgu_ln_b, w_spatial=m_w_spatial, b_spatial=m_b_spatial,
                w_attn_branch=m_w_attn_branch, w_sgu_branch=m_w_sgu_branch, w_out=m_w_out, ffn_norm=m_ffn_norm,
                w_gate=m_w_gate, w_up=m_w_up, w_down=m_w_down)
    mom2 = dict(mix_norm=v_mix_norm, w_in=v_w_in, q_norm=v_q_norm, k_norm=v_k_norm, sinks=v_sinks,
                sgu_ln_g=v_sgu_ln_g, sgu_ln_b=v_sgu_ln_b, w_spatial=v_w_spatial, b_spatial=v_b_spatial,
                w_attn_branch=v_w_attn_branch, w_sgu_branch=v_w_sgu_branch, w_out=v_w_out, ffn_norm=v_ffn_norm,
                w_gate=v_w_gate, w_up=v_w_up, w_down=v_w_down)
    xs, target = x[0], loss_target[0]
    S, D = xs.shape
    L = w_in.shape[0]
    AW, KW, SW = N_Q_HEADS * HEAD_DIM, N_KV_HEADS * HEAD_DIM, SGU_GROUPS * BLOCK
    dims = (AW, KW, AW + 2 * KW + 2 * SW, AW + 2 * KW)
    cos, sin = _rope_tables(S)
    reps = BLOCK // HEAD_DIM

    chip = (2 * lax.axis_index("x") + lax.axis_index("y")).astype(jnp.int32).reshape(1)
    core = lax.axis_index("c").astype(jnp.int32).reshape(1)
    started, token = {}, chip
    for l in range(L):
        for gi, names in enumerate(GATHER):
            bufs = [_cast_own("cast_%s_%d" % (n, l), chip, weights[n], l) for n in names]
            bufs, send, recv, token = _gather_start("gather_start_%d_%d" % (l, gi), bufs, token)
            started[(l, gi)] = (bufs, send, recv)
    stream = _WeightStream(started)
    stream.forward(0, 0, token)
    sp = [dict(mix_norm=mix_norm[l][None], ffn_norm=ffn_norm[l][None], q_norm=jnp.tile(q_norm[l][None], (1, reps)),
               k_norm=jnp.tile(k_norm[l][None], (1, reps)), sinks=sinks[l][None], sgu_ln_g=sgu_ln_g[l][None],
               sgu_ln_b=sgu_ln_b[l][None], w_spatial=w_spatial[l], b_spatial=b_spatial[l][:, :, None])
          for l in range(L)]

    act, saved, wl = xs, [], []
    for l in range(L):
        act, sv, w_all = _layer_fwd(act, stream, l, l == L - 1, sp[l], cos, sin, dims)
        saved.append(sv)
        wl.append(w_all)
    loss_part, dy, dyb = _loss_head(act, target)
    loss = lax.psum(loss_part[0, 0], ("x", "y", "c"))

    reducer = _GradReducer(chip, core)
    small_g = [None] * L
    for l in reversed(range(L)):
        dy, dyb, small_g[l] = _layer_bwd(dy, dyb, wl[l], sp[l], saved[l], cos, sin, dims, reducer, l)
    grad_x = dy[None]

    local = [_rows2d(jnp.stack([small_g[l][n].reshape(weights[n].shape[1:]) for l in range(L)])) for n in SMALL]
    small = list(_small_start(local, dy))
    updated = {}

    def update(layer, reduced):
        for n, g in reduced.items():
            updated[n] = _adamw_big("adamw_%s_%d" % (n, layer), layer, g, weights[n], mom1[n], mom2[n],
                                    updated.get(n))
        if len(small) == 4:
            arrays, lands, own_sems, _ = small
            small[:] = _small_forward(arrays, lands, own_sems, updated[n][0])
        return updated[n][0]

    reducer.finish(small[3], update)
    grads, deltas, new_m, new_v = {}, {}, {}, {}
    for n in BIG:
        grads[n], deltas[n], new_m[n], new_v[n] = updated[n]

    arrays, lands = _small_finish(*small, updated[BIG[0]][0])
    g_small = _small_sum(jnp.concatenate([chip, core]), arrays, lands)
    d_small, m_small, v_small = _adamw_small(g_small, [_rows2d(weights[n]) for n in SMALL],
                                             [_rows2d(mom1[n]) for n in SMALL], [_rows2d(mom2[n]) for n in SMALL])
    for n, g, d, m2, v2 in zip(SMALL, g_small, d_small, m_small, v_small):
        shape = weights[n].shape
        grads[n], deltas[n], new_m[n], new_v[n] = g.reshape(shape), d.reshape(shape), m2.reshape(shape), v2.reshape(shape)

    return (loss, grad_x, *[grads[n] for n in ORDER], *[deltas[n] for n in ORDER],
            *[new_m[n] for n in ORDER], *[new_v[n] for n in ORDER])
```

```python
import functools

import jax
import jax.numpy as jnp
from jax import lax
from jax.experimental import pallas as pl
from jax.experimental.pallas import tpu as pltpu

HEAD_DIM = 64
N_Q_HEADS = 16
N_KV_HEADS = 4
SGU_GROUPS = 8
BLOCK = 128
EPS = 1e-6
ADAM_LR = 0.001
ADAM_B1 = 0.9
ADAM_B2 = 0.999
ADAM_EPS = 1e-08
ADAM_WD = 0.01
ADAM_STEP = 10
N_CHIPS = 4
VMEM_LIMIT = 52 * 1024 * 1024
MXU_CHUNK = 256
ATTN_STACK = 4

F32 = jnp.float32
MXU = jnp.bfloat16
NN = (((1,), (0,)), ((), ()))
NT = (((1,), (1,)), ((), ()))
TN = (((0,), (0,)), ((), ()))
MESH = pl.DeviceIdType.MESH
ANY = pl.BlockSpec(memory_space=pl.ANY)


def _tile(n, pref):
    if n <= pref:
        return n
    best = None
    for t in range(BLOCK, pref + 1, BLOCK):
        if n % t == 0:
            best = t
    assert best is not None, (n, pref)
    return best


def _params():
    return pltpu.CompilerParams(vmem_limit_bytes=VMEM_LIMIT)


def _mm(name, grid, n_red, operands, specs, pairs, dims, n_extra, out_shapes, out_specs,
        acc_shapes, epilogue, after=None, chunk=None):
    n_op = len(operands) - n_extra
    n_out = len(out_shapes)
    n_acc = len(acc_shapes)
    if after is not None:
        operands, specs = list(operands) + [after], list(specs) + [ANY]
    n_in = len(operands)
    axes = [ax for ax in range(len(grid) - n_red, len(grid)) if grid[ax] > 1]

    def body(*refs):
        ops = refs[:n_op]
        extra = refs[n_op:n_op + n_extra]
        outs = refs[n_in:n_in + n_out]
        accs = refs[n_in + n_out:]

        def prod(a, b, cols=None):
            rhs = ops[b]
            if cols is not None:
                rhs = rhs.at[:, cols] if dims == NN else rhs.at[cols, :]
            return lax.dot_general(ops[a][...], rhs[...], dims, preferred_element_type=F32)

        def products(cols=None):
            vals = [None] * n_acc
            for a, b, k in pairs:
                d = prod(a, b, cols)
                vals[k] = d if vals[k] is None else vals[k] + d
            return vals

        if not axes and chunk is not None:
            width = outs[0].shape[-1]
            for c0 in range(0, width, chunk):
                cols = pl.ds(c0, min(chunk, width - c0))
                epilogue(products(cols), [e.at[:, cols] for e in extra], [o.at[:, cols] for o in outs])
        elif not axes:
            epilogue(products(), extra, outs)
        else:
            first = pl.program_id(axes[0]) == 0
            last = pl.program_id(axes[0]) == grid[axes[0]] - 1
            for ax in axes[1:]:
                first = jnp.logical_and(first, pl.program_id(ax) == 0)
                last = jnp.logical_and(last, pl.program_id(ax) == grid[ax] - 1)

            @pl.when(first)
            def _():
                for acc in accs:
                    acc[...] = jnp.zeros(acc.shape, F32)

            for a, b, k in pairs:
                accs[k][...] += prod(a, b)

            @pl.when(last)
            def _():
                epilogue([acc[...] for acc in accs], extra, outs)

    scratch = [pltpu.VMEM(s, F32) for s in acc_shapes] if axes else []
    return pl.pallas_call(
        body, name=name, grid=grid, in_specs=specs, out_specs=out_specs, out_shape=out_shapes,
        scratch_shapes=scratch, compiler_params=_params())(*operands)


def _sigmoid(x):
    return 1.0 / (1.0 + jnp.exp(-x))


_GELU_C = 0.7978845608028654
_GELU_A = 0.044715


def _gelu(x):
    return 0.5 * x * (1.0 + jnp.tanh(_GELU_C * (x + _GELU_A * x * x * x)))


def _gelu_grad(x):
    t = jnp.tanh(_GELU_C * (x + _GELU_A * x * x * x))
    return 0.5 * (1.0 + t) + 0.5 * x * (1.0 - t * t) * _GELU_C * (1.0 + 3.0 * _GELU_A * x * x)


def _rms_fwd(name, x, g):
    S, D = x.shape
    tr = _tile(S, 256)

    def body(x_ref, g_ref, o_ref):
        xv = x_ref[...]
        r = lax.rsqrt(jnp.mean(xv * xv, axis=-1, keepdims=True) + EPS)
        o_ref[...] = (xv * r * g_ref[...]).astype(MXU)

    return pl.pallas_call(
        body, name=name, grid=(S // tr,),
        in_specs=[pl.BlockSpec((tr, D), lambda i: (i, 0)), pl.BlockSpec((1, D), lambda i: (0, 0))],
        out_specs=pl.BlockSpec((tr, D), lambda i: (i, 0)),
        out_shape=jax.ShapeDtypeStruct((S, D), MXU), compiler_params=_params())(x, g)


def _rms_bwd(name, dh, x, g, dres, after):
    S, D = x.shape
    tr = _tile(S, 256)

    def body(dh_ref, x_ref, g_ref, dres_ref, after_ref, dx_ref, dxb_ref, dg_ref):
        xv = x_ref[...]
        r = lax.rsqrt(jnp.mean(xv * xv, axis=-1, keepdims=True) + EPS)
        xh = xv * r
        dhv = dh_ref[...]
        dy = dhv * g_ref[...]
        dx = dres_ref[...] + r * (dy - xh * jnp.mean(dy * xh, axis=-1, keepdims=True))
        dx_ref[...] = dx
        dxb_ref[...] = dx.astype(MXU)

        @pl.when(pl.program_id(0) == 0)
        def _():
            dg_ref[...] = jnp.zeros(dg_ref.shape, F32)

        dg_ref[...] += jnp.sum(dhv * xh, axis=0, keepdims=True)

    row = pl.BlockSpec((tr, D), lambda i: (i, 0))
    vec = pl.BlockSpec((1, D), lambda i: (0, 0))
    return pl.pallas_call(
        body, name=name, grid=(S // tr,), in_specs=[row, row, vec, row, ANY], out_specs=[row, row, vec],
        out_shape=[jax.ShapeDtypeStruct((S, D), F32), jax.ShapeDtypeStruct((S, D), MXU),
                   jax.ShapeDtypeStruct((1, D), F32)],
        compiler_params=_params())(dh, x, g, dres, after)


def _join_columns(name, parts):
    S = parts[0].shape[0]
    tr = _tile(S, 256)
    widths = [p.shape[1] for p in parts]
    assert all(w % BLOCK == 0 for w in widths)

    def body(*refs):
        o_ref, off = refs[-1], 0
        for ref, w in zip(refs[:-1], widths):
            o_ref[:, off:off + w] = ref[...]
            off += w

    return pl.pallas_call(
        body, name=name, grid=(S // tr,), in_specs=[pl.BlockSpec((tr, w), lambda i: (i, 0)) for w in widths],
        out_specs=pl.BlockSpec((tr, sum(widths)), lambda i: (i, 0)),
        out_shape=jax.ShapeDtypeStruct((S, sum(widths)), parts[0].dtype), compiler_params=_params())(*parts)


def _loss_head(y, target):
    S, D = y.shape
    tr = _tile(S, 256)

    def body(y_ref, t_ref, loss_ref, dy_ref, dyb_ref):
        d = y_ref[...] - t_ref[...]
        dy = d * (1.0 / D)
        dy_ref[...] = dy
        dyb_ref[...] = dy.astype(MXU)

        @pl.when(pl.program_id(0) == 0)
        def _():
            loss_ref[...] = jnp.zeros(loss_ref.shape, F32)

        loss_ref[...] += (0.5 / D) * jnp.sum(jnp.sum(d * d, axis=-1, keepdims=True), axis=0, keepdims=True)

    row = pl.BlockSpec((tr, D), lambda i: (i, 0))
    return pl.pallas_call(
        body, name="loss_head", grid=(S // tr,), in_specs=[row, row],
        out_specs=[pl.BlockSpec((1, 1), lambda i: (0, 0)), row, row],
        out_shape=[jax.ShapeDtypeStruct((1, 1), F32), jax.ShapeDtypeStruct((S, D), F32),
                   jax.ShapeDtypeStruct((S, D), MXU)],
        compiler_params=_params())(y, target)


def _head_sum(v):
    r = lax.broadcasted_iota(jnp.int32, (BLOCK, BLOCK), 0) // HEAD_DIM
    c = lax.broadcasted_iota(jnp.int32, (BLOCK, BLOCK), 1) // HEAD_DIM
    ones = jnp.where(r == c, 1.0, 0.0).astype(jnp.bfloat16)
    hi = v.astype(jnp.bfloat16)
    lo = (v - hi.astype(F32)).astype(jnp.bfloat16)
    parts = []
    for t in range(v.shape[1] // BLOCK):
        sl = slice(t * BLOCK, (t + 1) * BLOCK)
        parts.append(jnp.dot(hi[:, sl], ones, preferred_element_type=F32)
                     + jnp.dot(lo[:, sl], ones, preferred_element_type=F32))
    return parts[0] if len(parts) == 1 else jnp.concatenate(parts, axis=-1)


def _swap_halves(v):
    w = v.shape[1]
    half = HEAD_DIM // 2
    lane = lax.broadcasted_iota(jnp.int32, v.shape, 1) % HEAD_DIM
    return jnp.where(lane < half, pltpu.roll(v, w - half, 1), pltpu.roll(v, half, 1))


def _norm_rope(xv, gain, cos, sin):
    r = lax.rsqrt(_head_sum(xv * xv) * (1.0 / HEAD_DIM) + EPS)
    xn = xv * r * gain
    return xn * cos + _swap_halves(xn) * sin


def _norm_rope_bwd(dy, xv, gain, cos, sin):
    r = lax.rsqrt(_head_sum(xv * xv) * (1.0 / HEAD_DIM) + EPS)
    xh = xv * r
    dxn = dy * cos + _swap_halves(dy * sin)
    dgain = jnp.sum(dxn * xh, axis=0, keepdims=True)
    dxh = dxn * gain
    dx = r * (dxh - xh * (_head_sum(dxh * xh) * (1.0 / HEAD_DIM)))
    return dx, dgain


def _fold_heads(v):
    acc = v[:, 0:BLOCK]
    for t in range(1, v.shape[1] // BLOCK):
        acc = acc + v[:, t * BLOCK:(t + 1) * BLOCK]
    return acc + pltpu.roll(acc, HEAD_DIM, 1)


def _tile_lanes(v, width):
    return v if width == BLOCK else jnp.tile(v, (1, width // BLOCK))


def _low_half(rows):
    assert BLOCK == 2 * HEAD_DIM
    return lax.broadcasted_iota(jnp.int32, (rows, BLOCK), 1) < HEAD_DIM


def _spread_heads(v):
    low = _low_half(v.shape[0])
    out = []
    for t in range(v.shape[1] // BLOCK):
        tile = v[:, t * BLOCK:(t + 1) * BLOCK]
        swapped = pltpu.roll(tile, HEAD_DIM, 1)
        out += [jnp.where(low, tile, swapped), jnp.where(low, swapped, tile)]
    return jnp.concatenate(out, axis=-1)


def _gather_heads(v):
    low = _low_half(v.shape[0])
    out = []
    for t in range(v.shape[1] // (2 * BLOCK)):
        a, b = v[:, 2 * t * BLOCK:(2 * t + 1) * BLOCK], v[:, (2 * t + 1) * BLOCK:(2 * t + 2) * BLOCK]
        out.append(jnp.where(low, a + pltpu.roll(a, HEAD_DIM, 1), b + pltpu.roll(b, HEAD_DIM, 1)))
    return out[0] if len(out) == 1 else jnp.concatenate(out, axis=-1)


def _qk_prep(proj, qg, kg, cos, sin, AW, KW):
    S = proj.shape[0]
    tr = _tile(S, 256)
    scale = HEAD_DIM ** -0.5

    def body(q_ref, k_ref, v_ref, qg_ref, kg_ref, cos_ref, sin_ref, qo_ref, ko_ref, vo_ref):
        c, s = cos_ref[...], sin_ref[...]
        q = _norm_rope(q_ref[...], _tile_lanes(qg_ref[...], AW), _tile_lanes(c, AW), _tile_lanes(s, AW))
        k = _norm_rope(k_ref[...], _tile_lanes(kg_ref[...], KW), _tile_lanes(c, KW), _tile_lanes(s, KW))
        qo_ref[...] = (q * scale).astype(MXU)
        ko_ref[...] = _spread_heads(k).astype(MXU)
        vo_ref[...] = _spread_heads(v_ref[...]).astype(MXU)

    assert AW % KW == 0
    vec = pl.BlockSpec((1, BLOCK), lambda i: (0, 0))
    tab = pl.BlockSpec((tr, BLOCK), lambda i: (i, 0))
    wide = pl.BlockSpec((tr, 2 * KW), lambda i: (i, 0))
    return pl.pallas_call(
        body, name="qk_prep", grid=(S // tr,),
        in_specs=[pl.BlockSpec((tr, AW), lambda i: (i, 0)),
                  pl.BlockSpec((tr, KW), lambda i: (i, AW // KW)),
                  pl.BlockSpec((tr, KW), lambda i: (i, AW // KW + 1)), vec, vec, tab, tab],
        out_specs=[pl.BlockSpec((tr, AW), lambda i: (i, 0)), wide, wide],
        out_shape=[jax.ShapeDtypeStruct((S, AW), MXU), jax.ShapeDtypeStruct((S, 2 * KW), MXU),
                   jax.ShapeDtypeStruct((S, 2 * KW), MXU)],
        compiler_params=_params())(proj, proj, proj, qg, kg, cos, sin)


def _stack_heads(x, h0, nh):
    low = _low_half(BLOCK)
    parts = []
    for h in range(h0, h0 + nh):
        tile = x[:, (h // 2) * BLOCK:(h // 2 + 1) * BLOCK]
        parts.append(jnp.where(low if h % 2 == 0 else jnp.logical_not(low), tile, jnp.zeros_like(tile)))
    return jnp.concatenate(parts, axis=0)


def _unstack_heads(y):
    low = _low_half(BLOCK)
    tiles = [jnp.where(low, y[2 * t * BLOCK:(2 * t + 1) * BLOCK], y[(2 * t + 1) * BLOCK:(2 * t + 2) * BLOCK])
             for t in range(y.shape[0] // (2 * BLOCK))]
    return tiles[0] if len(tiles) == 1 else jnp.concatenate(tiles, axis=-1)


def _band_t(n):
    key = lax.broadcasted_iota(jnp.int32, (2 * BLOCK, BLOCK), 0)
    qry = lax.broadcasted_iota(jnp.int32, (2 * BLOCK, BLOCK), 1)
    return (key > qry) & (key <= qry + BLOCK) & ((key >= BLOCK) | (n > 0))


def _attn_probs_t(ok, qs, kcat, h0, nh, sink_ref):
    st = jnp.where(ok, lax.dot_general(kcat, qs, NT, preferred_element_type=F32), -1e30)
    sk = jnp.concatenate([jnp.full((1, BLOCK), sink_ref[0, h], F32) for h in range(h0, h0 + nh)], axis=1)
    m = jnp.maximum(jnp.max(st, axis=0, keepdims=True), sk)
    e = jnp.exp(st - m)
    es = jnp.exp(sk - m)
    rz = 1.0 / (jnp.sum(e, axis=0, keepdims=True) + es)
    return e * rz, es * rz, rz


def _attn_fwd(qr, kr, vb, sinks):
    S, AW = qr.shape
    KW = kr.shape[1]
    nb = S // BLOCK
    nkv = KW // BLOCK
    qpk = AW // (nkv * HEAD_DIM)
    nh = min(ATTN_STACK, qpk)
    assert nh % 2 == 0 and qpk % nh == 0

    def body(sink_ref, q_ref, kp_ref, kc_ref, vp_ref, vc_ref, o_ref):
        n = pl.program_id(0)
        q, kp, kc, vp, vc = q_ref[...], kp_ref[...], kc_ref[...], vp_ref[...], vc_ref[...]
        ok = jnp.concatenate([_band_t(n)] * nh, axis=1)
        outs = []
        for g in range(nkv):
            kcat = jnp.concatenate([kp[:, g * BLOCK:(g + 1) * BLOCK], kc[:, g * BLOCK:(g + 1) * BLOCK]], axis=0)
            vcat = jnp.concatenate([vp[:, g * BLOCK:(g + 1) * BLOCK], vc[:, g * BLOCK:(g + 1) * BLOCK]], axis=0)
            for h0 in range(g * qpk, (g + 1) * qpk, nh):
                pt, _, _ = _attn_probs_t(ok, _stack_heads(q, h0, nh), kcat, h0, nh, sink_ref)
                outs.append(_unstack_heads(lax.dot_general(pt.astype(MXU), vcat, TN, preferred_element_type=F32)))
        o_ref[...] = jnp.concatenate(outs, axis=-1).astype(MXU)

    cur = lambda n: (n, 0)
    prev = lambda n: (jnp.maximum(n - 1, 0), 0)
    return pl.pallas_call(
        body, name="attn_fwd", grid=(nb,),
        in_specs=[pl.BlockSpec(memory_space=pltpu.SMEM), pl.BlockSpec((BLOCK, AW), cur),
                  pl.BlockSpec((BLOCK, KW), prev), pl.BlockSpec((BLOCK, KW), cur),
                  pl.BlockSpec((BLOCK, KW), prev), pl.BlockSpec((BLOCK, KW), cur)],
        out_specs=pl.BlockSpec((BLOCK, AW), cur),
        out_shape=jax.ShapeDtypeStruct((S, AW), MXU), compiler_params=_params())(sinks, qr, kr, kr, vb, vb)


def _attn_bwd(qr, kr, vb, sinks, dattn):
    S, AW = qr.shape
    KW = kr.shape[1]
    nb = S // BLOCK
    nkv = KW // BLOCK
    qpk = AW // (nkv * HEAD_DIM)
    nh = min(ATTN_STACK, qpk)
    scale = HEAD_DIM ** -0.5

    def body(sink_ref, q_ref, kp_ref, kc_ref, vp_ref, vc_ref, do_ref,
             dq_ref, dkp_ref, dkc_ref, dvp_ref, dvc_ref, dsink_ref):
        n = pl.program_id(0)
        q, kp, kc, vp, vc = q_ref[...], kp_ref[...], kc_ref[...], vp_ref[...], vc_ref[...]
        do = do_ref[...].astype(MXU)
        lane = lax.broadcasted_iota(jnp.int32, (1, BLOCK), 1)
        ok = jnp.concatenate([_band_t(n)] * nh, axis=1)
        dsink = jnp.zeros((1, BLOCK), F32)
        dqs, dkps, dkcs, dvps, dvcs = [], [], [], [], []
        for g in range(nkv):
            kcat = jnp.concatenate([kp[:, g * BLOCK:(g + 1) * BLOCK], kc[:, g * BLOCK:(g + 1) * BLOCK]], axis=0)
            vcat = jnp.concatenate([vp[:, g * BLOCK:(g + 1) * BLOCK], vc[:, g * BLOCK:(g + 1) * BLOCK]], axis=0)
            dk, dv = None, None
            for h0 in range(g * qpk, (g + 1) * qpk, nh):
                qs = _stack_heads(q, h0, nh)
                dos = _stack_heads(do, h0, nh)
                pt, ps, _ = _attn_probs_t(ok, qs, kcat, h0, nh, sink_ref)
                dpt = lax.dot_general(vcat, dos, NT, preferred_element_type=F32)
                delta = jnp.sum(pt * dpt, axis=0, keepdims=True)
                dst = (pt * (dpt - delta)).astype(MXU)
                dsk = -ps * delta
                dv_part = jnp.dot(pt.astype(MXU), dos, preferred_element_type=F32)
                dk_part = jnp.dot(dst, qs, preferred_element_type=F32)
                dqs.append(_unstack_heads(lax.dot_general(dst, kcat, TN, preferred_element_type=F32) * scale))
                dk = dk_part if dk is None else dk + dk_part
                dv = dv_part if dv is None else dv + dv_part
                for j in range(nh):
                    tot = jnp.sum(dsk[:, j * BLOCK:(j + 1) * BLOCK], axis=1, keepdims=True)
                    dsink = dsink + jnp.where(lane == h0 + j, tot, 0.0)
            dkps.append(dk[:BLOCK])
            dkcs.append(dk[BLOCK:])
            dvps.append(dv[:BLOCK])
            dvcs.append(dv[BLOCK:])
        dq_ref[...] = jnp.concatenate(dqs, axis=-1)
        dkp_ref[...] = jnp.concatenate(dkps, axis=-1)
        dkc_ref[...] = jnp.concatenate(dkcs, axis=-1)
        dvp_ref[...] = jnp.concatenate(dvps, axis=-1)
        dvc_ref[...] = jnp.concatenate(dvcs, axis=-1)

        @pl.when(n == 0)
        def _():
            dsink_ref[...] = jnp.zeros(dsink_ref.shape, F32)

        dsink_ref[...] += dsink

    cur = lambda n: (n, 0)
    prev = lambda n: (jnp.maximum(n - 1, 0), 0)
    kv = jax.ShapeDtypeStruct((S, KW), F32)
    kvspec = pl.BlockSpec((BLOCK, KW), cur)
    return pl.pallas_call(
        body, name="attn_bwd", grid=(nb,),
        in_specs=[pl.BlockSpec(memory_space=pltpu.SMEM), pl.BlockSpec((BLOCK, AW), cur),
                  pl.BlockSpec((BLOCK, KW), prev), kvspec, pl.BlockSpec((BLOCK, KW), prev), kvspec,
                  pl.BlockSpec((BLOCK, AW), cur)],
        out_specs=[pl.BlockSpec((BLOCK, AW), cur), kvspec, kvspec, kvspec, kvspec,
                   pl.BlockSpec((1, BLOCK), lambda n: (0, 0))],
        out_shape=[jax.ShapeDtypeStruct((S, AW), F32), kv, kv, kv, kv, jax.ShapeDtypeStruct((1, BLOCK), F32)],
        compiler_params=_params())(sinks, qr, kr, kr, vb, vb, dattn)


def _qk_prep_bwd(proj, qg, kg, cos, sin, dq, dkp, dkc, dvp, dvc, AW, KW):
    S = proj.shape[0]
    nb = S // BLOCK

    def body(q_ref, k_ref, qg_ref, kg_ref, cos_ref, sin_ref, dq_ref, dkp_ref, dkc_ref, dvp_ref, dvc_ref,
             o_ref, dqg_ref, dkg_ref):
        n = pl.program_id(0)
        c, s = cos_ref[...], sin_ref[...]
        has_next = jnp.where(n < nb - 1, 1.0, 0.0)
        dk = _gather_heads(dkc_ref[...] + has_next * dkp_ref[...])
        dv = _gather_heads(dvc_ref[...] + has_next * dvp_ref[...])
        dxq, dqg = _norm_rope_bwd(dq_ref[...], q_ref[...], _tile_lanes(qg_ref[...], AW),
                                  _tile_lanes(c, AW), _tile_lanes(s, AW))
        dxk, dkg = _norm_rope_bwd(dk, k_ref[...], _tile_lanes(kg_ref[...], KW),
                                  _tile_lanes(c, KW), _tile_lanes(s, KW))
        o_ref[...] = jnp.concatenate([dxq, dxk, dv], axis=-1).astype(MXU)

        @pl.when(n == 0)
        def _():
            dqg_ref[...] = jnp.zeros(dqg_ref.shape, F32)
            dkg_ref[...] = jnp.zeros(dkg_ref.shape, F32)

        dqg_ref[...] += _fold_heads(dqg)
        dkg_ref[...] += _fold_heads(dkg)

    cur = lambda n: (n, 0)
    nxt = lambda n: (jnp.minimum(n + 1, nb - 1), 0)
    vec = pl.BlockSpec((1, BLOCK), lambda n: (0, 0))
    tab = pl.BlockSpec((BLOCK, BLOCK), cur)
    return pl.pallas_call(
        body, name="qk_prep_bwd", grid=(nb,),
        in_specs=[pl.BlockSpec((BLOCK, AW), cur), pl.BlockSpec((BLOCK, KW), lambda n: (n, AW // KW)),
                  vec, vec, tab, tab, pl.BlockSpec((BLOCK, AW), cur),
                  pl.BlockSpec((BLOCK, 2 * KW), nxt), pl.BlockSpec((BLOCK, 2 * KW), cur),
                  pl.BlockSpec((BLOCK, 2 * KW), nxt), pl.BlockSpec((BLOCK, 2 * KW), cur)],
        out_specs=[pl.BlockSpec((BLOCK, AW + 2 * KW), cur), vec, vec],
        out_shape=[jax.ShapeDtypeStruct((S, AW + 2 * KW), MXU), jax.ShapeDtypeStruct((1, BLOCK), F32),
                   jax.ShapeDtypeStruct((1, BLOCK), F32)],
        compiler_params=_params())(proj, proj, qg, kg, cos, sin, dq, dkp, dkc, dvp, dvc)


SGU_LANES = 512
SGU_ROWS = 256


def _sgu_group(v, lng, lnb, w_f32, b):
    rows = v.shape[0]
    mu = jnp.mean(v, axis=-1, keepdims=True)
    vc = v - mu
    r = lax.rsqrt(jnp.mean(vc * vc, axis=-1, keepdims=True) + EPS)
    xh = vc * r
    vn = (xh * lng + lnb).astype(MXU)
    row = lax.broadcasted_iota(jnp.int32, (BLOCK, BLOCK), 0)
    col = lax.broadcasted_iota(jnp.int32, (BLOCK, BLOCK), 1)
    tri = row >= col
    w = jnp.where(tri, w_f32, 0.0).astype(MXU)
    chunks = [jnp.dot(w, vn[k * BLOCK:(k + 1) * BLOCK], preferred_element_type=F32) + b for k in range(rows // BLOCK)]
    s = chunks[0] if len(chunks) == 1 else jnp.concatenate(chunks, axis=0)
    return xh, r, vn, w, s, tri


def _sgu_layout(S, u_col):
    SW = SGU_GROUPS * BLOCK
    lb, tr = min(SGU_LANES, SW), min(SGU_ROWS, S)
    assert u_col % lb == 0 and SW % lb == 0 and S % tr == 0
    ub, nlb, gpb = u_col // lb, SW // lb, lb // BLOCK
    specs = [pl.BlockSpec((tr, lb), lambda j, i: (i, ub + j)), pl.BlockSpec((tr, lb), lambda j, i: (i, ub + nlb + j)),
             pl.BlockSpec((1, lb), lambda j, i: (0, j)), pl.BlockSpec((1, lb), lambda j, i: (0, j)),
             pl.BlockSpec((gpb, BLOCK, BLOCK), lambda j, i: (j, 0, 0)),
             pl.BlockSpec((gpb, BLOCK, 1), lambda j, i: (j, 0, 0))]
    return lb, tr, gpb, nlb, specs


def _sgu_fwd(proj, lng, lnb, ws, bs, u_col):
    S = proj.shape[0]
    lb, tr, gpb, nlb, specs = _sgu_layout(S, u_col)

    def body(pu_ref, pv_ref, lng_ref, lnb_ref, w_ref, b_ref, o_ref):
        u = _gelu(pu_ref[...])
        v = _gelu(pv_ref[...])
        outs = []
        for g in range(gpb):
            sl = slice(g * BLOCK, (g + 1) * BLOCK)
            s = _sgu_group(v[:, sl], lng_ref[:, sl], lnb_ref[:, sl], w_ref[g], b_ref[g])[4]
            outs.append(u[:, sl] * s)
        o_ref[...] = (outs[0] if gpb == 1 else jnp.concatenate(outs, axis=-1)).astype(MXU)

    return pl.pallas_call(
        body, name="sgu_fwd", grid=(nlb, S // tr), in_specs=specs,
        out_specs=pl.BlockSpec((tr, lb), lambda j, i: (i, j)),
        out_shape=jax.ShapeDtypeStruct((S, nlb * lb), MXU), compiler_params=_params())(proj, proj, lng, lnb, ws, bs)


def _sgu_bwd(proj, lng, lnb, ws, bs, dsgu, u_col, after):
    S = proj.shape[0]
    G = SGU_GROUPS
    lb, tr, gpb, nlb, specs = _sgu_layout(S, u_col)
    nch = tr // BLOCK

    def body(pu_ref, pv_ref, lng_ref, lnb_ref, w_ref, b_ref, do_ref, after_ref,
             dpu_ref, dpv_ref, dw_ref, db_ref, dlng_ref, dlnb_ref):
        pu, pv, do = pu_ref[...], pv_ref[...], do_ref[...]
        u = _gelu(pu)
        v = _gelu(pv)

        @pl.when(pl.program_id(1) == 0)
        def _():
            dw_ref[...] = jnp.zeros(dw_ref.shape, F32)
            db_ref[...] = jnp.zeros(db_ref.shape, F32)
            dlng_ref[...] = jnp.zeros(dlng_ref.shape, F32)
            dlnb_ref[...] = jnp.zeros(dlnb_ref.shape, F32)

        ss, dvs, dlng, dlnb = [], [], [], []
        for g in range(gpb):
            sl = slice(g * BLOCK, (g + 1) * BLOCK)
            xh, r, vn, w, s, tri = _sgu_group(v[:, sl], lng_ref[:, sl], lnb_ref[:, sl], w_ref[g], b_ref[g])
            ds = do[:, sl] * u[:, sl]
            dsb = ds.astype(MXU)
            dw, db, dvn = None, None, []
            for k in range(nch):
                rows = slice(k * BLOCK, (k + 1) * BLOCK)
                part = lax.dot_general(dsb[rows], vn[rows], NT, preferred_element_type=F32)
                dw = part if dw is None else dw + part
                rowsum = jnp.sum(ds[rows], axis=-1, keepdims=True)
                db = rowsum if db is None else db + rowsum
                dvn.append(lax.dot_general(w, dsb[rows], TN, preferred_element_type=F32))
            dvn = dvn[0] if nch == 1 else jnp.concatenate(dvn, axis=0)
            dw_ref[g] += jnp.where(tri, dw, 0.0)
            db_ref[g] += db
            dxh = dvn * lng_ref[:, sl]
            dvs.append(r * (dxh - jnp.mean(dxh, axis=-1, keepdims=True)
                            - xh * jnp.mean(dxh * xh, axis=-1, keepdims=True)))
            dlng.append(jnp.sum(dvn * xh, axis=0, keepdims=True))
            dlnb.append(jnp.sum(dvn, axis=0, keepdims=True))
            ss.append(s)
        cat = lambda parts: parts[0] if gpb == 1 else jnp.concatenate(parts, axis=-1)
        dpu_ref[...] = (do * cat(ss) * _gelu_grad(pu)).astype(MXU)
        dpv_ref[...] = (cat(dvs) * _gelu_grad(pv)).astype(MXU)
        dlng_ref[...] += cat(dlng)
        dlnb_ref[...] += cat(dlnb)

    tile = pl.BlockSpec((tr, lb), lambda j, i: (i, j))
    vec = pl.BlockSpec((1, lb), lambda j, i: (0, j))
    half = jax.ShapeDtypeStruct((S, G * BLOCK), MXU)
    return pl.pallas_call(
        body, name="sgu_bwd", grid=(nlb, S // tr), in_specs=specs + [tile, ANY],
        out_specs=[tile, tile, pl.BlockSpec((gpb, BLOCK, BLOCK), lambda j, i: (j, 0, 0)),
                   pl.BlockSpec((gpb, BLOCK, 1), lambda j, i: (j, 0, 0)), vec, vec],
        out_shape=[half, half, jax.ShapeDtypeStruct((G, BLOCK, BLOCK), F32),
                   jax.ShapeDtypeStruct((G, BLOCK, 1), F32),
                   jax.ShapeDtypeStruct((1, G * BLOCK), F32), jax.ShapeDtypeStruct((1, G * BLOCK), F32)],
        compiler_params=_params())(proj, proj, lng, lnb, ws, bs, dsgu, after)


def _store_f32(vals, extra, outs):
    for v, o in zip(vals, outs):
        o[...] = v


def _store_mxu(vals, extra, outs):
    for v, o in zip(vals, outs):
        o[...] = v.astype(MXU)


def _proj_in(h, w):
    S, D = h.shape
    Ns = w.shape[2]
    tm, tn = _tile(S, 1024), _tile(Ns, 1024)
    npb = Ns // tn
    return _mm("proj_in", (S // tm, N_CHIPS, npb), 0, [h, w],
               [pl.BlockSpec((tm, D), lambda i, s, j: (i, 0)), pl.BlockSpec((None, D, tn), lambda i, s, j: (s, 0, j))],
               [(0, 1, 0)], NN, 0, [jax.ShapeDtypeStruct((S, N_CHIPS * Ns), F32)],
               [pl.BlockSpec((tm, tn), lambda i, s, j: (i, s * npb + j))], [None], _store_f32)[0]


def _branches(attn, sgu, wa, ws, proj, gate0):
    S, AW = attn.shape
    SW = sgu.shape[1]
    Nb = wa.shape[2]
    D = N_CHIPS * Nb
    tm = _tile(S, 512)
    assert gate0 % Nb == 0
    ga, gb = gate0 // Nb, (gate0 + D) // Nb

    def epilogue(vals, extra, outs):
        a, b = vals
        outs[0][...] = (_sigmoid(extra[0][...]) * a + _sigmoid(extra[1][...]) * b).astype(MXU)
        outs[1][...] = a
        outs[2][...] = b

    tile = pl.BlockSpec((tm, Nb), lambda i, s: (i, s))
    wspec = lambda k: pl.BlockSpec((None, k, Nb), lambda i, s: (s, 0, 0))
    f = jax.ShapeDtypeStruct((S, D), F32)
    return _mm("branches", (S // tm, N_CHIPS), 0, [attn, sgu, wa, ws, proj, proj],
               [pl.BlockSpec((tm, AW), lambda i, s: (i, 0)), pl.BlockSpec((tm, SW), lambda i, s: (i, 0)),
                wspec(AW), wspec(SW), pl.BlockSpec((tm, Nb), lambda i, s: (i, ga + s)),
                pl.BlockSpec((tm, Nb), lambda i, s: (i, gb + s))],
               [(0, 2, 0), (1, 3, 1)], NN, 2, [jax.ShapeDtypeStruct((S, D), MXU), f, f], [tile] * 3,
               [None, None], epilogue, chunk=MXU_CHUNK)


def _rows_mm(name, a, w, res):
    S = a.shape[0]
    _, K, N = w.shape
    tm, tn = _tile(S, 1024), _tile(N, 1024)

    def epilogue(vals, extra, outs):
        outs[0][...] = extra[0][...] + vals[0]

    out = pl.BlockSpec((tm, tn), lambda i, j, s: (i, j))
    return _mm(name, (S // tm, N // tn, N_CHIPS), 1, [a, w, res],
               [pl.BlockSpec((tm, K), lambda i, j, s: (i, s)), pl.BlockSpec((None, K, tn), lambda i, j, s: (s, 0, j)), out],
               [(0, 1, 0)], NN, 1, [jax.ShapeDtypeStruct((S, N), F32)], [out], [(tm, tn)], epilogue)[0]


def _gate_up(h2, wg, wu):
    S, D = h2.shape
    Nf = wg.shape[2]
    tm = _tile(S, 256)

    def epilogue(vals, extra, outs):
        g, u = vals
        outs[0][...] = g
        outs[1][...] = u
        outs[2][...] = (g * _sigmoid(g) * u).astype(MXU)

    w = pl.BlockSpec((None, D, Nf), lambda s, i: (s, 0, 0))
    o = pl.BlockSpec((tm, Nf), lambda s, i: (i, s))
    f = jax.ShapeDtypeStruct((S, N_CHIPS * Nf), F32)
    return _mm("gate_up", (N_CHIPS, S // tm), 0, [h2, wg, wu],
               [pl.BlockSpec((tm, D), lambda s, i: (i, 0)), w, w], [(0, 1, 0), (0, 2, 1)], NN, 0,
               [f, f, jax.ShapeDtypeStruct((S, N_CHIPS * Nf), MXU)], [o, o, o], [None, None], epilogue,
               chunk=MXU_CHUNK)


def _down_bwd(dyb, wd, g, u):
    S, D = dyb.shape
    Kf = wd.shape[1]
    tm = _tile(S, 512)

    def epilogue(vals, extra, outs):
        da, gv, uv = vals[0], extra[0][...], extra[1][...]
        sg = _sigmoid(gv)
        outs[0][...] = (da * uv * sg * (1.0 + gv * (1.0 - sg))).astype(MXU)
        outs[1][...] = (da * gv * sg).astype(MXU)

    t = pl.BlockSpec((tm, Kf), lambda i, s: (i, s))
    o = jax.ShapeDtypeStruct((S, N_CHIPS * Kf), MXU)
    return _mm("down_bwd", (S // tm, N_CHIPS), 0, [dyb, wd, g, u],
               [pl.BlockSpec((tm, D), lambda i, s: (i, 0)), pl.BlockSpec((None, Kf, D), lambda i, s: (s, 0, 0)), t, t],
               [(0, 1, 0)], NT, 2, [o, o], [t, t], [None], epilogue, chunk=MXU_CHUNK)


def _out_bwd(dxb, wo, proj, ba, bb, gate0):
    S, D = dxb.shape
    Ko = wo.shape[1]
    tm = _tile(S, 512)
    assert gate0 % Ko == 0
    ga, gb = gate0 // Ko, (gate0 + D) // Ko

    def epilogue(vals, extra, outs):
        dm = vals[0]
        sa, sb = _sigmoid(extra[0][...]), _sigmoid(extra[1][...])
        outs[0][...] = (dm * sa).astype(MXU)
        outs[1][...] = (dm * sb).astype(MXU)
        outs[2][...] = (dm * extra[2][...] * sa * (1.0 - sa)).astype(MXU)
        outs[3][...] = (dm * extra[3][...] * sb * (1.0 - sb)).astype(MXU)

    t = pl.BlockSpec((tm, Ko), lambda i, s: (i, s))
    o = jax.ShapeDtypeStruct((S, D), MXU)
    return _mm("out_bwd", (S // tm, N_CHIPS), 0, [dxb, wo, proj, proj, ba, bb],
               [pl.BlockSpec((tm, D), lambda i, s: (i, 0)), pl.BlockSpec((None, Ko, D), lambda i, s: (s, 0, 0)),
                pl.BlockSpec((tm, Ko), lambda i, s: (i, ga + s)), pl.BlockSpec((tm, Ko), lambda i, s: (i, gb + s)), t, t],
               [(0, 1, 0)], NT, 4, [o] * 4, [t] * 4, [None], epilogue, chunk=MXU_CHUNK)


def _dx_cols(name, terms, n_out, after=None):
    S = terms[0][0].shape[0]
    _, K, Ns = terms[0][1].shape
    tm, tko, tn = _tile(S, 1024), _tile(K, 1024), _tile(Ns, 1920 if len(terms) == 1 else 1408)
    npb = Ns // tn
    operands, specs, pairs = [], [], []
    for t, (dy, w, k) in enumerate(terms):
        assert w.shape == (N_CHIPS, K, Ns)
        operands += [dy, w]
        specs += [pl.BlockSpec((tm, tn), lambda i, jk, s, jn: (i, s * npb + jn)),
                  pl.BlockSpec((None, tko, tn), lambda i, jk, s, jn: (s, jk, jn))]
        pairs.append((2 * t, 2 * t + 1, k))
    out = pl.BlockSpec((tm, tko), lambda i, jk, s, jn: (i, jk))
    return _mm(name, (S // tm, K // tko, N_CHIPS, npb), 2, operands, specs, pairs, NT, 0,
               [jax.ShapeDtypeStruct((S, K), F32)] * n_out, [out] * n_out, [(tm, tko)] * n_out, _store_f32, after)


def _dw_cols(name, a, dy):
    S, K = a.shape
    Ns = dy.shape[1] // N_CHIPS
    tk, tn = _tile(K, 512), _tile(Ns, 1408)
    npb = Ns // tn
    return _mm(name, (K // tk, N_CHIPS, npb), 0, [a, dy],
               [pl.BlockSpec((S, tk), lambda jk, s, jn: (0, jk)), pl.BlockSpec((S, tn), lambda jk, s, jn: (0, s * npb + jn))],
               [(0, 1, 0)], TN, 0, [jax.ShapeDtypeStruct((N_CHIPS, K, Ns), MXU)],
               [pl.BlockSpec((None, tk, tn), lambda jk, s, jn: (s, jk, jn))], [None], _store_mxu)[0]


def _dw_rows(name, a, dy):
    S = a.shape[0]
    K = a.shape[1] // N_CHIPS
    N = dy.shape[1]
    tk, tn = _tile(K, 1408), _tile(N, 1024)
    nkb = K // tk
    return _mm(name, (N_CHIPS, nkb, N // tn), 0, [a, dy],
               [pl.BlockSpec((S, tk), lambda s, jk, jn: (0, s * nkb + jk)), pl.BlockSpec((S, tn), lambda s, jk, jn: (0, jn))],
               [(0, 1, 0)], TN, 0, [jax.ShapeDtypeStruct((N_CHIPS, K, N), MXU)],
               [pl.BlockSpec((None, tk, tn), lambda s, jk, jn: (s, jk, jn))], [None], _store_mxu)[0]


def _layer_fwd(x, stream, layer, last, sp, cos, sin, dims):
    AW, KW, gate0, u_col = dims
    h = _rms_fwd("mix_norm", x, sp["mix_norm"])
    w = stream.finish(layer, 0, h)
    proj = _proj_in(h, w["w_in"])
    qr, kr, vb = _qk_prep(proj, sp["q_norm"], sp["k_norm"], cos, sin, AW, KW)
    attn = _attn_fwd(qr, kr, vb, sp["sinks"])
    stream.forward(layer, 1, attn)
    sgu = _sgu_fwd(proj, sp["sgu_ln_g"], sp["sgu_ln_b"], sp["w_spatial"], sp["b_spatial"], u_col)
    w.update(stream.finish(layer, 1, sgu))
    merged, ba, bb = _branches(attn, sgu, w["w_attn_branch"], w["w_sgu_branch"], proj, gate0)
    x1 = _rows_mm("out_proj", merged, w["w_out"], x)
    h2 = _rms_fwd("ffn_norm", x1, sp["ffn_norm"])
    stream.forward(layer, 2, h2)
    w.update(stream.finish(layer, 2, h2))
    g, u, act = _gate_up(h2, w["w_gate"], w["w_up"])
    stream.forward(layer, 3, g)
    w.update(stream.finish(layer, 3, g))
    x2 = _rows_mm("down_proj", act, w["w_down"], x1)
    if not last:
        stream.forward(layer + 1, 0, x2)
    saved = dict(x=x, h=h, proj=proj, qr=qr, kr=kr, vb=vb, attn=attn, sgu=sgu, merged=merged, ba=ba, bb=bb,
                 x1=x1, h2=h2, g=g, u=u, act=act)
    return x2, saved, w


def _layer_bwd(dy, dyb, w, sp, sv, cos, sin, dims, reducer, layer):
    AW, KW, gate0, u_col = dims
    big, small = {}, {}
    dg, du = _down_bwd(dyb, w["w_down"], sv["g"], sv["u"])
    big["w_down"] = _dw_rows("dw_down", sv["act"], dyb)
    big["w_gate"] = _dw_cols("dw_gate", sv["h2"], dg)
    big["w_up"] = _dw_cols("dw_up", sv["h2"], du)
    token = reducer.start(layer, 2, big)
    dh2 = _dx_cols("dh2", [(dg, w["w_gate"], 0), (du, w["w_up"], 0)], 1, token)[0]
    token = reducer.scatter(layer, 2, dh2)
    dx1, dx1b, small["ffn_norm"] = _rms_bwd("ffn_norm_bwd", dh2, sv["x1"], sp["ffn_norm"], dy, token)
    dba, dbb, dgla, dglb = _out_bwd(dx1b, w["w_out"], sv["proj"], sv["ba"], sv["bb"], gate0)
    big["w_out"] = _dw_rows("dw_out", sv["merged"], dx1b)
    big["w_attn_branch"] = _dw_cols("dw_attn_branch", sv["attn"], dba)
    big["w_sgu_branch"] = _dw_cols("dw_sgu_branch", sv["sgu"], dbb)
    token = reducer.start(layer, 1, big)
    dattn, dsgu = _dx_cols("dbranch_in", [(dba, w["w_attn_branch"], 0), (dbb, w["w_sgu_branch"], 1)], 2, token)
    token = reducer.scatter(layer, 1, dsgu)
    dpu, dpv, small["w_spatial"], db, small["sgu_ln_g"], small["sgu_ln_b"] = _sgu_bwd(
        sv["proj"], sp["sgu_ln_g"], sp["sgu_ln_b"], sp["w_spatial"], sp["b_spatial"], dsgu, u_col, token)
    small["b_spatial"] = db[:, :, 0]
    dq, dkp, dkc, dvp, dvc, dsink = _attn_bwd(sv["qr"], sv["kr"], sv["vb"], sp["sinks"], dattn)
    small["sinks"] = dsink[:, :sp["sinks"].shape[1]]
    dqkv, dqg, dkg = _qk_prep_bwd(sv["proj"], sp["q_norm"], sp["k_norm"], cos, sin, dq, dkp, dkc, dvp, dvc, AW, KW)
    small["q_norm"] = dqg[:, :HEAD_DIM]
    small["k_norm"] = dkg[:, :HEAD_DIM]
    dproj = _join_columns("dproj", [dqkv, dpu, dpv, dgla, dglb])
    big["w_in"] = _dw_cols("dw_in", sv["h"], dproj)
    token = reducer.start(layer, 0, big)
    dh = _dx_cols("dh", [(dproj, w["w_in"], 0)], 1, token)[0]
    token = reducer.scatter(layer, 0, dh)
    dx, dxb, small["mix_norm"] = _rms_bwd("mix_norm_bwd", dh, sv["x"], sp["mix_norm"], dx1, token)
    return dx, dxb, small


def _place():
    x, y, c = lax.axis_index("x"), lax.axis_index("y"), lax.axis_index("c")
    chips = [(1 - x, y), (x, 1 - y), (1 - x, 1 - y)]
    return x, y, c, chips


def _half_rows(c, rows):
    h = rows // 2
    assert h % 16 == 0
    return pl.ds(pl.multiple_of(c * h, 16), h)


def _row_tile(rows, pref):
    best = None
    for t in range(16, min(rows, pref) + 1, 16):
        if rows % t == 0:
            best = t
    assert best is not None, rows
    return best


def _cast_own(name, chip, w, layer):
    _, R, C = w.shape
    tr = _row_tile(R, 512)

    def body(chip_ref, w_ref, o_ref):
        o_ref[...] = w_ref[...].astype(MXU)

    return pl.pallas_call(
        body, name=name, out_shape=jax.ShapeDtypeStruct((N_CHIPS, R, C), MXU),
        grid_spec=pltpu.PrefetchScalarGridSpec(
            num_scalar_prefetch=1, grid=(R // tr,),
            in_specs=[pl.BlockSpec((None, tr, C), lambda i, chip_ref: (layer, i, 0))],
            out_specs=pl.BlockSpec((None, tr, C), lambda i, chip_ref: (chip_ref[0], i, 0))),
        compiler_params=_params())(chip, w)


HBM = pl.BlockSpec(memory_space=pltpu.HBM)
SEM = pl.BlockSpec(memory_space=pltpu.SEMAPHORE)
DATAFLOW = pltpu.SideEffectType.DATAFLOW_SIDE_EFFECTING


def _gather_copies(bufs, send_sem, recv_sem):
    x, y, c, chips = _place()

    def ici(a, j, block):
        px, py = chips[j]
        blk = bufs[a].at[block, _half_rows(c, bufs[a].shape[1])]
        return pltpu.make_async_remote_copy(
            src_ref=blk, dst_ref=blk, send_sem=send_sem.at[3 * a + j], recv_sem=recv_sem.at[3 * a + j],
            device_id=(px, py, c), device_id_type=MESH)

    def d2d(a, j, core):
        px, py = chips[j]
        blk = bufs[a].at[2 * px + py, _half_rows(core, bufs[a].shape[1])]
        return pltpu.make_async_remote_copy(
            src_ref=blk, dst_ref=blk, send_sem=send_sem.at[3 * a + j], recv_sem=recv_sem.at[3 * a + j],
            device_id=(x, y, 1 - c), device_id_type=MESH)

    return ici, d2d


def _in_hbm(bufs):
    return [pltpu.with_memory_space_constraint(b, pltpu.HBM) for b in bufs]


def _gather_start(name, bufs, after):
    n = len(bufs)

    def body(*refs):
        dst = refs[n + 1:2 * n + 1]
        send_sem, recv_sem, token = refs[2 * n + 1:]
        x, y, c, chips = _place()
        ici, _ = _gather_copies(dst, send_sem, recv_sem)
        for a in range(n):
            for j in range(3):
                ici(a, j, 2 * x + y).start()
        token[...] = jnp.zeros(token.shape, token.dtype)

    sems = pltpu.SemaphoreType.DMA((3 * n,))
    outs = pl.pallas_call(
        body, name=name, in_specs=[HBM] * n + [ANY],
        out_specs=[HBM] * n + [SEM, SEM, pl.BlockSpec(memory_space=pltpu.VMEM)],
        out_shape=[pltpu.HBM(b.shape, b.dtype) for b in bufs] + [sems, sems, jax.ShapeDtypeStruct((8, BLOCK), F32)],
        input_output_aliases={a: a for a in range(n)},
        compiler_params=pltpu.CompilerParams(has_side_effects=DATAFLOW))(*_in_hbm(bufs), after)
    return outs[:n], outs[n], outs[n + 1], outs[n + 2]


def _gather_forward(name, bufs, ici_send, ici_recv, after):
    n = len(bufs)

    def body(*refs):
        ici_send_ref, ici_recv_ref = refs[n], refs[n + 1]
        dst = refs[n + 3:2 * n + 3]
        d2d_send, d2d_recv = refs[2 * n + 3:]
        x, y, c, chips = _place()
        ici, _ = _gather_copies(dst, ici_send_ref, ici_recv_ref)
        _, d2d = _gather_copies(dst, d2d_send, d2d_recv)
        for a in range(n):
            for j, (px, py) in enumerate(chips):
                ici(a, j, 2 * px + py).wait_recv()
                d2d(a, j, c).start()
        for a in range(n):
            for j in range(3):
                ici(a, j, 2 * x + y).wait_send()

    sems = pltpu.SemaphoreType.DMA((3 * n,))
    outs = pl.pallas_call(
        body, name=name, in_specs=[HBM] * n + [SEM, SEM, ANY], out_specs=[HBM] * n + [SEM, SEM],
        out_shape=[pltpu.HBM(b.shape, b.dtype) for b in bufs] + [sems, sems],
        input_output_aliases={a: a for a in range(n)},
        compiler_params=pltpu.CompilerParams(has_side_effects=DATAFLOW))(*bufs, ici_send, ici_recv, after)
    return outs[:n], outs[n], outs[n + 1]


def _gather_finish(name, bufs, d2d_send, d2d_recv, after):
    n = len(bufs)

    def body(*refs):
        send_ref, recv_ref = refs[n], refs[n + 1]
        dst = refs[n + 3:]
        x, y, c, chips = _place()
        _, d2d = _gather_copies(dst, send_ref, recv_ref)
        for a in range(n):
            for j in range(3):
                d2d(a, j, 1 - c).wait_recv()
                d2d(a, j, c).wait_send()

    return pl.pallas_call(
        body, name=name, in_specs=[HBM] * n + [SEM, SEM, ANY], out_specs=[HBM] * n,
        out_shape=[pltpu.HBM(b.shape, b.dtype) for b in bufs],
        input_output_aliases={a: a for a in range(n)},
        compiler_params=pltpu.CompilerParams(has_side_effects=DATAFLOW))(*bufs, d2d_send, d2d_recv, after)


GATHER = (("w_in",), ("w_attn_branch", "w_sgu_branch", "w_out"), ("w_gate", "w_up"), ("w_down",))
REDUCE = (("w_in",), ("w_attn_branch", "w_sgu_branch", "w_out"), ("w_gate", "w_up", "w_down"))


class _WeightStream:
    def __init__(self, started):
        self.started, self.passed = started, {}

    def forward(self, layer, group, after):
        bufs, send, recv = self.started[(layer, group)]
        self.passed[(layer, group)] = _gather_forward("gather_forward_%d_%d" % (layer, group), bufs, send, recv, after)

    def finish(self, layer, group, after):
        bufs, send, recv = self.passed[(layer, group)]
        done = _gather_finish("gather_finish_%d_%d" % (layer, group), bufs, send, recv, after)
        return dict(zip(GATHER[group], done))


def _pair_copies(grads, lands, send_sem, recv_sem):
    x, y, c, _ = _place()

    def make(a):
        theirs = _half_rows(1 - c, grads[a].shape[1])
        return pltpu.make_async_remote_copy(
            src_ref=grads[a].at[:, theirs], dst_ref=lands[a], send_sem=send_sem.at[a], recv_sem=recv_sem.at[a],
            device_id=(x, y, 1 - c), device_id_type=MESH)

    return make


def _pair_start(name, grads, after):
    n = len(grads)
    lands = [lax.empty((g.shape[0], g.shape[1] // 2, g.shape[2]), g.dtype) for g in grads]

    def body(*refs):
        src, dst = refs[2 * n + 1:3 * n + 1], refs[3 * n + 1:4 * n + 1]
        send_sem, recv_sem, token = refs[4 * n + 1:]
        copy = _pair_copies(src, dst, send_sem, recv_sem)
        for a in range(n):
            copy(a).start()
        token[...] = jnp.zeros(token.shape, token.dtype)

    sems = pltpu.SemaphoreType.DMA((n,))
    outs = pl.pallas_call(
        body, name=name, in_specs=[HBM] * (2 * n) + [ANY],
        out_specs=[HBM] * (2 * n) + [SEM, SEM, pl.BlockSpec(memory_space=pltpu.VMEM)],
        out_shape=[pltpu.HBM(b.shape, b.dtype) for b in grads + lands] + [sems, sems, jax.ShapeDtypeStruct((8, BLOCK), F32)],
        input_output_aliases={a: a for a in range(2 * n)},
        compiler_params=pltpu.CompilerParams(has_side_effects=DATAFLOW))(*_in_hbm(grads + lands), after)
    return outs[:n], outs[n:2 * n], outs[2 * n], outs[2 * n + 1], outs[2 * n + 2]


def _pair_finish(name, grads, lands, send_sem, recv_sem, after):
    n = len(grads)

    def body(*refs):
        send_ref, recv_ref = refs[2 * n], refs[2 * n + 1]
        src, dst = refs[2 * n + 3:3 * n + 3], refs[3 * n + 3:]
        copy = _pair_copies(src, dst, send_ref, recv_ref)
        for a in range(n):
            copy(a).wait_send()
            copy(a).wait_recv()

    outs = pl.pallas_call(
        body, name=name, in_specs=[HBM] * (2 * n) + [SEM, SEM, ANY], out_specs=[HBM] * (2 * n),
        out_shape=[pltpu.HBM(b.shape, b.dtype) for b in grads + lands],
        input_output_aliases={a: a for a in range(2 * n)},
        compiler_params=pltpu.CompilerParams(has_side_effects=DATAFLOW))(*grads, *lands, send_sem, recv_sem, after)
    return outs[:n], outs[n:]


def _pair_sum(name, core, g, p):
    _, h, C = p.shape
    tr = _row_tile(h, 512)
    nrb = h // tr

    def body(core_ref, g_ref, p_ref, o_ref):
        o_ref[...] = (g_ref[...].astype(F32) + p_ref[...].astype(F32)).astype(o_ref.dtype)

    spec = pl.BlockSpec((None, tr, C), lambda s, i, core_ref: (s, i, 0))
    return pl.pallas_call(
        body, name=name, out_shape=jax.ShapeDtypeStruct(p.shape, p.dtype),
        grid_spec=pltpu.PrefetchScalarGridSpec(
            num_scalar_prefetch=1, grid=(N_CHIPS, nrb),
            in_specs=[pl.BlockSpec((None, tr, C), lambda s, i, core_ref: (s, core_ref[0] * nrb + i, 0)), spec],
            out_specs=spec),
        compiler_params=_params())(core, g, p)


def _scatter_copies(sums, slots, send_sem, recv_sem):
    x, y, c, chips = _place()

    def make(a, j):
        px, py = chips[j]
        return pltpu.make_async_remote_copy(
            src_ref=sums[a].at[2 * px + py], dst_ref=slots[a].at[j], send_sem=send_sem.at[3 * a + j],
            recv_sem=recv_sem.at[3 * a + j], device_id=(px, py, c), device_id_type=MESH)

    return make


def _scatter_start(name, sums, after):
    n = len(sums)
    slots = [lax.empty((3,) + s.shape[1:], s.dtype) for s in sums]

    def body(*refs):
        src, dst = refs[2 * n + 1:3 * n + 1], refs[3 * n + 1:4 * n + 1]
        send_sem, recv_sem, token = refs[4 * n + 1:]
        copy = _scatter_copies(src, dst, send_sem, recv_sem)
        for a in range(n):
            for j in range(3):
                copy(a, j).start()
        token[...] = jnp.zeros(token.shape, token.dtype)

    sems = pltpu.SemaphoreType.DMA((3 * n,))
    outs = pl.pallas_call(
        body, name=name, in_specs=[HBM] * (2 * n) + [ANY],
        out_specs=[HBM] * (2 * n) + [SEM, SEM, pl.BlockSpec(memory_space=pltpu.VMEM)],
        out_shape=[pltpu.HBM(b.shape, b.dtype) for b in sums + slots] + [sems, sems, jax.ShapeDtypeStruct((8, BLOCK), F32)],
        input_output_aliases={a: a for a in range(2 * n)},
        compiler_params=pltpu.CompilerParams(has_side_effects=DATAFLOW))(*_in_hbm(sums + slots), after)
    return outs[:n], outs[n:2 * n], outs[2 * n], outs[2 * n + 1], outs[2 * n + 2]


def _scatter_finish(name, sums, slots, send_sem, recv_sem, after):
    n = len(sums)

    def body(*refs):
        send_ref, recv_ref = refs[2 * n], refs[2 * n + 1]
        src, dst = refs[2 * n + 3:3 * n + 3], refs[3 * n + 3:]
        copy = _scatter_copies(src, dst, send_ref, recv_ref)
        for a in range(n):
            for j in range(3):
                copy(a, j).wait_send()
                copy(a, j).wait_recv()

    outs = pl.pallas_call(
        body, name=name, in_specs=[HBM] * (2 * n) + [SEM, SEM, ANY], out_specs=[HBM] * (2 * n),
        out_shape=[pltpu.HBM(b.shape, b.dtype) for b in sums + slots],
        input_output_aliases={a: a for a in range(2 * n)},
        compiler_params=pltpu.CompilerParams(has_side_effects=DATAFLOW))(*sums, *slots, send_sem, recv_sem, after)
    return outs[:n], outs[n:]


def _slot_sum(name, place, slots, sums):
    _, h, C = slots.shape
    tr = _row_tile(h, 512)
    nrb = h // tr

    def body(place_ref, r0, r1, r2, own, o_ref):
        o_ref[...] = ((r0[...].astype(F32) + r1[...].astype(F32)) + r2[...].astype(F32)) + own[...].astype(F32)

    slot = lambda k: pl.BlockSpec((None, tr, C), lambda i, place_ref: (k, i, 0))
    return pl.pallas_call(
        body, name=name, out_shape=jax.ShapeDtypeStruct((2 * h, C), F32),
        grid_spec=pltpu.PrefetchScalarGridSpec(
            num_scalar_prefetch=1, grid=(nrb,),
            in_specs=[slot(0), slot(1), slot(2),
                      pl.BlockSpec((None, tr, C), lambda i, place_ref: (place_ref[0], i, 0))],
            out_specs=pl.BlockSpec((tr, C), lambda i, place_ref: (place_ref[1] * nrb + i, 0))),
        compiler_params=_params())(place, slots, slots, slots, sums)


def _half_copies(bufs, send_sem, recv_sem):
    x, y, c, _ = _place()

    def make(a, core):
        rows = bufs[a].at[_half_rows(core, bufs[a].shape[0])]
        return pltpu.make_async_remote_copy(
            src_ref=rows, dst_ref=rows, send_sem=send_sem.at[a], recv_sem=recv_sem.at[a],
            device_id=(x, y, 1 - c), device_id_type=MESH)

    return make


def _half_start(name, bufs, after):
    n = len(bufs)

    def body(*refs):
        dst = refs[n + 1:2 * n + 1]
        send_sem, recv_sem, token = refs[2 * n + 1:]
        c = lax.axis_index("c")
        copy = _half_copies(dst, send_sem, recv_sem)
        for a in range(n):
            copy(a, c).start()
        token[...] = jnp.zeros(token.shape, token.dtype)

    sems = pltpu.SemaphoreType.DMA((n,))
    outs = pl.pallas_call(
        body, name=name, in_specs=[HBM] * n + [ANY],
        out_specs=[HBM] * n + [SEM, SEM, pl.BlockSpec(memory_space=pltpu.VMEM)],
        out_shape=[pltpu.HBM(b.shape, b.dtype) for b in bufs] + [sems, sems, jax.ShapeDtypeStruct((8, BLOCK), F32)],
        input_output_aliases={a: a for a in range(n)},
        compiler_params=pltpu.CompilerParams(has_side_effects=DATAFLOW))(*_in_hbm(bufs), after)
    return outs[:n], outs[n], outs[n + 1], outs[n + 2]


def _half_finish(name, bufs, send_sem, recv_sem, after):
    n = len(bufs)

    def body(*refs):
        send_ref, recv_ref = refs[n], refs[n + 1]
        dst = refs[n + 3:]
        c = lax.axis_index("c")
        copy = _half_copies(dst, send_ref, recv_ref)
        for a in range(n):
            copy(a, c).wait_send()
            copy(a, 1 - c).wait_recv()

    return pl.pallas_call(
        body, name=name, in_specs=[HBM] * n + [SEM, SEM, ANY], out_specs=[HBM] * n,
        out_shape=[pltpu.HBM(b.shape, b.dtype) for b in bufs],
        input_output_aliases={a: a for a in range(n)},
        compiler_params=pltpu.CompilerParams(has_side_effects=DATAFLOW))(*bufs, send_sem, recv_sem, after)


class _GradReducer:
    def __init__(self, chip, core):
        self.core, self.place, self.pairs, self.started = core, jnp.concatenate([chip, core]), {}, []

    def start(self, layer, group, grads):
        mine = [grads[n] for n in REDUCE[group]]
        mine, lands, send, recv, token = _pair_start("grad_pair_start_%d_%d" % (layer, group), mine, self.place)
        self.pairs[(layer, group)] = (mine, lands, send, recv)
        return token

    def scatter(self, layer, group, after):
        tag = "%d_%d" % (layer, group)
        names = REDUCE[group]
        mine, lands, send, recv = self.pairs.pop((layer, group))
        mine, theirs = _pair_finish("grad_pair_finish_" + tag, mine, lands, send, recv, after)
        sums = [_pair_sum("pair_sum_%s_%d" % (n, layer), self.core, g, p) for n, g, p in zip(names, mine, theirs)]
        sums, slots, send, recv, token = _scatter_start("grad_scatter_start_" + tag, sums, self.place)
        self.started.append((layer, names, sums, slots, send, recv))
        return token

    def finish(self, after, update):
        for layer in sorted({entry[0] for entry in self.started}, reverse=True):
            exchanged = []
            for lyr, names, sums, slots, send, recv in self.started:
                if lyr != layer:
                    continue
                tag = "%s_%d" % (names[0], layer)
                sums, slots = _scatter_finish("grad_scatter_finish_" + tag, sums, slots, send, recv, after)
                halves = [_slot_sum("slot_sum_%s_%d" % (n, layer), self.place, r, s)
                          for n, r, s in zip(names, slots, sums)]
                halves, send, recv, after = _half_start("grad_half_start_" + tag, halves, self.place)
                exchanged.append((tag, names, halves, send, recv))
            for tag, names, halves, send, recv in exchanged:
                whole = _half_finish("grad_half_finish_" + tag, halves, send, recv, after)
                after = update(layer, dict(zip(names, whole)))


def _small_copies(arrays, lands, own_sems, pass_sems):
    def slot(a, block):
        px, py, pc = block
        return lands[a].at[4 * px + 2 * py + pc]

    def own(a, k, block, to):
        return pltpu.make_async_remote_copy(
            src_ref=arrays[a], dst_ref=slot(a, block), send_sem=own_sems[0].at[4 * a + k],
            recv_sem=own_sems[1].at[4 * a + k], device_id=to, device_id_type=MESH)

    def passed(a, j, block, to):
        return pltpu.make_async_remote_copy(
            src_ref=slot(a, block), dst_ref=slot(a, block), send_sem=pass_sems[0].at[3 * a + j],
            recv_sem=pass_sems[1].at[3 * a + j], device_id=to, device_id_type=MESH)

    return own, passed


def _small_start(arrays, after):
    n = len(arrays)
    lands = [lax.empty((2 * N_CHIPS,) + a.shape, a.dtype) for a in arrays]

    def body(*refs):
        src, dst = refs[2 * n + 1:3 * n + 1], refs[3 * n + 1:4 * n + 1]
        send_sem, recv_sem, token = refs[4 * n + 1:]
        x, y, c, chips = _place()
        own, _ = _small_copies(src, dst, (send_sem, recv_sem), None)
        for a in range(n):
            own(a, 0, (x, y, c), (x, y, 1 - c)).start()
            for j, (px, py) in enumerate(chips):
                own(a, 1 + j, (x, y, c), (px, py, c)).start()
        token[...] = jnp.zeros(token.shape, token.dtype)

    sems = pltpu.SemaphoreType.DMA((4 * n,))
    outs = pl.pallas_call(
        body, name="small_grad_start", in_specs=[HBM] * (2 * n) + [ANY],
        out_specs=[HBM] * (2 * n) + [SEM, SEM, pl.BlockSpec(memory_space=pltpu.VMEM)],
        out_shape=[pltpu.HBM(b.shape, b.dtype) for b in arrays + lands]
        + [sems, sems, jax.ShapeDtypeStruct((8, BLOCK), F32)],
        input_output_aliases={a: a for a in range(2 * n)},
        compiler_params=pltpu.CompilerParams(has_side_effects=DATAFLOW))(*_in_hbm(arrays + lands), after)
    return outs[:n], outs[n:2 * n], (outs[2 * n], outs[2 * n + 1]), outs[2 * n + 2]


def _small_forward(arrays, lands, own_sems, after):
    n = len(arrays)

    def body(*refs):
        own_refs = (refs[2 * n], refs[2 * n + 1])
        src, dst = refs[2 * n + 3:3 * n + 3], refs[3 * n + 3:4 * n + 3]
        pass_refs = (refs[4 * n + 3], refs[4 * n + 4])
        x, y, c, chips = _place()
        own, passed = _small_copies(src, dst, own_refs, pass_refs)
        for a in range(n):
            for j, (px, py) in enumerate(chips):
                own(a, 1 + j, (px, py, c), (x, y, c)).wait_recv()
                passed(a, j, (px, py, c), (x, y, 1 - c)).start()
        for a in range(n):
            own(a, 0, (x, y, 1 - c), (x, y, c)).wait_recv()
            own(a, 0, (x, y, c), (x, y, 1 - c)).wait_send()
            for j, (px, py) in enumerate(chips):
                own(a, 1 + j, (x, y, c), (px, py, c)).wait_send()

    sems = pltpu.SemaphoreType.DMA((3 * n,))
    outs = pl.pallas_call(
        body, name="small_grad_forward", in_specs=[HBM] * (2 * n) + [SEM, SEM, ANY],
        out_specs=[HBM] * (2 * n) + [SEM, SEM],
        out_shape=[pltpu.HBM(b.shape, b.dtype) for b in arrays + lands] + [sems, sems],
        input_output_aliases={a: a for a in range(2 * n)},
        compiler_params=pltpu.CompilerParams(has_side_effects=DATAFLOW))(*arrays, *lands, *own_sems, after)
    return outs[:n], outs[n:2 * n], (outs[2 * n], outs[2 * n + 1])


def _small_finish(arrays, lands, pass_sems, after):
    n = len(arrays)

    def body(*refs):
        pass_refs = (refs[2 * n], refs[2 * n + 1])
        src, dst = refs[2 * n + 3:3 * n + 3], refs[3 * n + 3:]
        x, y, c, chips = _place()
        _, passed = _small_copies(src, dst, None, pass_refs)
        for a in range(n):
            for j, (px, py) in enumerate(chips):
                passed(a, j, (px, py, 1 - c), (x, y, c)).wait_recv()
                passed(a, j, (px, py, c), (x, y, 1 - c)).wait_send()

    outs = pl.pallas_call(
        body, name="small_grad_finish", in_specs=[HBM] * (2 * n) + [SEM, SEM, ANY], out_specs=[HBM] * (2 * n),
        out_shape=[pltpu.HBM(b.shape, b.dtype) for b in arrays + lands],
        input_output_aliases={a: a for a in range(2 * n)},
        compiler_params=pltpu.CompilerParams(has_side_effects=DATAFLOW))(*arrays, *lands, *pass_sems, after)
    return outs[:n], outs[n:]


def _small_sum(place, arrays, lands):
    n = len(arrays)
    n_dev = 2 * N_CHIPS

    def body(place_ref, *refs):
        me = 2 * place_ref[0] + place_ref[1]
        for a in range(n):
            own, land, out = refs[a], refs[n + a], refs[2 * n + a]
            acc = None
            for d in range(n_dev):
                term = jnp.where(me == d, own[...], land[jnp.where(me == d, d ^ 1, d)])
                acc = term if acc is None else acc + term
            out[...] = acc

    vm = pl.BlockSpec(memory_space=pltpu.VMEM)
    return pl.pallas_call(
        body, name="small_grad_sum", in_specs=[pl.BlockSpec(memory_space=pltpu.SMEM)] + [vm] * (2 * n),
        out_specs=[vm] * n, out_shape=[jax.ShapeDtypeStruct(a.shape, F32) for a in arrays],
        compiler_params=_params())(place, *arrays, *lands)


def _adamw_math(w, g, m, v):
    m2 = ADAM_B1 * m + (1.0 - ADAM_B1) * g
    v2 = ADAM_B2 * v + (1.0 - ADAM_B2) * (g * g)
    m_hat = m2 / (1.0 - ADAM_B1 ** ADAM_STEP)
    v_hat = v2 / (1.0 - ADAM_B2 ** ADAM_STEP)
    delta = -ADAM_LR * (m_hat / (jnp.sqrt(v_hat) + ADAM_EPS) + ADAM_WD * w)
    return delta, m2, v2


def _adamw_big(name, layer, grad, w, m, v, others):
    L, R, C = w.shape
    tr = _row_tile(R, 256)

    def body(g_ref, w_ref, m_ref, v_ref, *rest):
        go_ref, d_ref, mo_ref, vo_ref = rest[-4:]
        g = g_ref[...]
        delta, m2, v2 = _adamw_math(w_ref[...], g, m_ref[...], v_ref[...])
        go_ref[...] = g
        d_ref[...] = delta
        mo_ref[...] = m2
        vo_ref[...] = v2

    blk = pl.BlockSpec((None, tr, C), lambda i: (layer, i, 0))
    shp = jax.ShapeDtypeStruct(w.shape, F32)
    others = [] if others is None else list(others)
    return pl.pallas_call(
        body, name=name, grid=(R // tr,),
        in_specs=[pl.BlockSpec((tr, C), lambda i: (i, 0))] + [blk] * 3 + [ANY] * len(others), out_specs=[blk] * 4,
        out_shape=[shp] * 4, input_output_aliases={4 + k: k for k in range(len(others))},
        compiler_params=_params())(grad, w, m, v, *others)


def _adamw_small(gs, ws, ms, vs):
    n = len(gs)

    def body(*refs):
        for a in range(n):
            g_ref, w_ref, m_ref, v_ref = refs[a], refs[n + a], refs[2 * n + a], refs[3 * n + a]
            delta, m2, v2 = _adamw_math(w_ref[...], g_ref[...], m_ref[...], v_ref[...])
            refs[4 * n + a][...] = delta
            refs[5 * n + a][...] = m2
            refs[6 * n + a][...] = v2

    vm = pl.BlockSpec(memory_space=pltpu.VMEM)
    shapes = [jax.ShapeDtypeStruct(g.shape, F32) for g in gs]
    outs = pl.pallas_call(
        body, name="adamw_small", in_specs=[vm] * (4 * n), out_specs=[vm] * (3 * n), out_shape=shapes * 3,
        compiler_params=_params())(*gs, *ws, *ms, *vs)
    return outs[:n], outs[n:2 * n], outs[2 * n:]


def _rows2d(a):
    return a if a.ndim == 2 else a.reshape(-1, a.shape[-1])


BIG = ("w_in", "w_attn_branch", "w_sgu_branch", "w_out", "w_gate", "w_up", "w_down")
SMALL = ("mix_norm", "q_norm", "k_norm", "sinks", "sgu_ln_g", "sgu_ln_b", "w_spatial", "b_spatial", "ffn_norm")
ORDER = ("mix_norm", "w_in", "q_norm", "k_norm", "sinks", "sgu_ln_g", "sgu_ln_b", "w_spatial", "b_spatial",
         "w_attn_branch", "w_sgu_branch", "w_out", "ffn_norm", "w_gate", "w_up", "w_down")


def _rope_tables(seq):
    pos = jnp.arange(seq, dtype=F32)
    inv_freq = jnp.power(10000.0, -jnp.arange(0, HEAD_DIM, 2, dtype=F32) / HEAD_DIM)
    ang = pos[:, None] * inv_freq[None, :]
    cos, sin = jnp.cos(ang), jnp.sin(ang)
    reps = BLOCK // HEAD_DIM
    return (jnp.tile(jnp.concatenate([cos, cos], axis=1), (1, reps)),
            jnp.tile(jnp.concatenate([-sin, sin], axis=1), (1, reps)))


def kernel(x, mix_norm, w_in, q_norm, k_norm, sinks, sgu_ln_g, sgu_ln_b, w_spatial, b_spatial, w_attn_branch, w_sgu_branch, w_out, ffn_norm, w_gate, w_up, w_down, loss_target, m_mix_norm, m_w_in, m_q_norm, m_k_norm, m_sinks, m_sgu_ln_g, m_sgu_ln_b, m_w_spatial, m_b_spatial, m_w_attn_branch, m_w_sgu_branch, m_w_out, m_ffn_norm, m_w_gate, m_w_up, m_w_down, v_mix_norm, v_w_in, v_q_norm, v_k_norm, v_sinks, v_sgu_ln_g, v_sgu_ln_b, v_w_spatial, v_b_spatial, v_w_attn_branch, v_w_sgu_branch, v_w_out, v_ffn_norm, v_w_gate, v_w_up, v_w_down):
    weights = dict(mix_norm=mix_norm, w_in=w_in, q_norm=q_norm, k_norm=k_norm, sinks=sinks, sgu_ln_g=sgu_ln_g,
                   sgu_ln_b=sgu_ln_b, w_spatial=w_spatial, b_spatial=b_spatial, w_attn_branch=w_attn_branch,
                   w_sgu_branch=w_sgu_branch, w_out=w_out, ffn_norm=ffn_norm, w_gate=w_gate, w_up=w_up, w_down=w_down)
    mom1 = dict(mix_norm=m_mix_norm, w_in=m_w_in, q_norm=m_q_norm, k_norm=m_k_norm, sinks=m_sinks,
                sgu_ln_g=m_sgu_ln_g, sgu_ln_b=m_sgu_ln_b, w_spatial=m_w_spatial, b_spatial=m_b_spatial,
                w_attn_branch=m_w_attn_branch, w_sgu_branch=m_w_sgu_branch, w_out=m_w_out, ffn_norm=m_ffn_norm,
                w_gate=m_w_gate, w_up=m_w_up, w_down=m_w_down)
    mom2 = dict(mix_norm=v_mix_norm, w_in=v_w_in, q_norm=v_q_norm, k_norm=v_k_norm, sinks=v_sinks,
                sgu_ln_g=v_sgu_ln_g, sgu_ln_b=v_sgu_ln_b, w_spatial=v_w_spatial, b_spatial=v_b_spatial,
                w_attn_branch=v_w_attn_branch, w_sgu_branch=v_w_sgu_branch, w_out=v_w_out, ffn_norm=v_ffn_norm,
                w_gate=v_w_gate, w_up=v_w_up, w_down=v_w_down)
    xs, target = x[0], loss_target[0]
    S, D = xs.shape
    L = w_in.shape[0]
    AW, KW, SW = N_Q_HEADS * HEAD_DIM, N_KV_HEADS * HEAD_DIM, SGU_GROUPS * BLOCK
    dims = (AW, KW, AW + 2 * KW + 2 * SW, AW + 2 * KW)
    cos, sin = _rope_tables(S)
    reps = BLOCK // HEAD_DIM

    chip = (2 * lax.axis_index("x") + lax.axis_index("y")).astype(jnp.int32).reshape(1)
    core = lax.axis_index("c").astype(jnp.int32).reshape(1)
    started, token = {}, chip
    for l in range(L):
        for gi, names in enumerate(GATHER):
            bufs = [_cast_own("cast_%s_%d" % (n, l), chip, weights[n], l) for n in names]
            bufs, send, recv, token = _gather_start("gather_start_%d_%d" % (l, gi), bufs, token)
            started[(l, gi)] = (bufs, send, recv)
    stream = _WeightStream(started)
    stream.forward(0, 0, token)
    sp = [dict(mix_norm=mix_norm[l][None], ffn_norm=ffn_norm[l][None], q_norm=jnp.tile(q_norm[l][None], (1, reps)),
               k_norm=jnp.tile(k_norm[l][None], (1, reps)), sinks=sinks[l][None], sgu_ln_g=sgu_ln_g[l][None],
               sgu_ln_b=sgu_ln_b[l][None], w_spatial=w_spatial[l], b_spatial=b_spatial[l][:, :, None])
          for l in range(L)]

    act, saved, wl = xs, [], []
    for l in range(L):
        act, sv, w_all = _layer_fwd(act, stream, l, l == L - 1, sp[l], cos, sin, dims)
        saved.append(sv)
        wl.append(w_all)
    loss_part, dy, dyb = _loss_head(act, target)
    loss = lax.psum(loss_part[0, 0], ("x", "y", "c"))

    reducer = _GradReducer(chip, core)
    small_g = [None] * L
    for l in reversed(range(L)):
        dy, dyb, small_g[l] = _layer_bwd(dy, dyb, wl[l], sp[l], saved[l], cos, sin, dims, reducer, l)
    grad_x = dy[None]

    local = [_rows2d(jnp.stack([small_g[l][n].reshape(weights[n].shape[1:]) for l in range(L)])) for n in SMALL]
    small = list(_small_start(local, dy))
    updated = {}

    def update(layer, reduced):
        for n, g in reduced.items():
            updated[n] = _adamw_big("adamw_%s_%d" % (n, layer), layer, g, weights[n], mom1[n], mom2[n],
                                    updated.get(n))
        if len(small) == 4:
            arrays, lands, own_sems, _ = small
            small[:] = _small_forward(arrays, lands, own_sems, updated[n][0])
        return updated[n][0]

    reducer.finish(small[3], update)
    grads, deltas, new_m, new_v = {}, {}, {}, {}
    for n in BIG:
        grads[n], deltas[n], new_m[n], new_v[n] = updated[n]

    arrays, lands = _small_finish(*small, updated[BIG[0]][0])
    g_small = _small_sum(jnp.concatenate([chip, core]), arrays, lands)
    d_small, m_small, v_small = _adamw_small(g_small, [_rows2d(weights[n]) for n in SMALL],
                                             [_rows2d(mom1[n]) for n in SMALL], [_rows2d(mom2[n]) for n in SMALL])
    for n, g, d, m2, v2 in zip(SMALL, g_small, d_small, m_small, v_small):
        shape = weights[n].shape
        grads[n], deltas[n], new_m[n], new_v[n] = g.reshape(shape), d.reshape(shape), m2.reshape(shape), v2.reshape(shape)

    return (loss, grad_x, *[grads[n] for n in ORDER], *[deltas[n] for n in ORDER],
            *[new_m[n] for n in ORDER], *[new_v[n] for n in ORDER])
```

```python
import functools

import jax
import jax.numpy as jnp
from jax import lax
from jax.experimental import pallas as pl
from jax.experimental.pallas import tpu as pltpu

HEAD_DIM = 64
N_Q_HEADS = 16
N_KV_HEADS = 4
SGU_GROUPS = 8
BLOCK = 128
EPS = 1e-6
ADAM_LR = 0.001
ADAM_B1 = 0.9
ADAM_B2 = 0.999
ADAM_EPS = 1e-08
ADAM_WD = 0.01
ADAM_STEP = 10
N_CHIPS = 4
VMEM_LIMIT = 52 * 1024 * 1024
MXU_CHUNK = 256
ATTN_STACK = 4

F32 = jnp.float32
MXU = jnp.bfloat16
NN = (((1,), (0,)), ((), ()))
NT = (((1,), (1,)), ((), ()))
TN = (((0,), (0,)), ((), ()))
MESH = pl.DeviceIdType.MESH
ANY = pl.BlockSpec(memory_space=pl.ANY)


def _tile(n, pref):
    if n <= pref:
        return n
    best = None
    for t in range(BLOCK, pref + 1, BLOCK):
        if n % t == 0:
            best = t
    assert best is not None, (n, pref)
    return best


def _params():
    return pltpu.CompilerParams(vmem_limit_bytes=VMEM_LIMIT)


def _mm(name, grid, n_red, operands, specs, pairs, dims, n_extra, out_shapes, out_specs,
        acc_shapes, epilogue, after=None, chunk=None):
    n_op = len(operands) - n_extra
    n_out = len(out_shapes)
    n_acc = len(acc_shapes)
    if after is not None:
        operands, specs = list(operands) + [after], list(specs) + [ANY]
    n_in = len(operands)
    axes = [ax for ax in range(len(grid) - n_red, len(grid)) if grid[ax] > 1]

    def body(*refs):
        ops = refs[:n_op]
        extra = refs[n_op:n_op + n_extra]
        outs = refs[n_in:n_in + n_out]
        accs = refs[n_in + n_out:]

        def prod(a, b, cols=None):
            rhs = ops[b]
            if cols is not None:
                rhs = rhs.at[:, cols] if dims == NN else rhs.at[cols, :]
            return lax.dot_general(ops[a][...], rhs[...], dims, preferred_element_type=F32)

        def products(cols=None):
            vals = [None] * n_acc
            for a, b, k in pairs:
                d = prod(a, b, cols)
                vals[k] = d if vals[k] is None else vals[k] + d
            return vals

        if not axes and chunk is not None:
            width = outs[0].shape[-1]
            for c0 in range(0, width, chunk):
                cols = pl.ds(c0, min(chunk, width - c0))
                epilogue(products(cols), [e.at[:, cols] for e in extra], [o.at[:, cols] for o in outs])
        elif not axes:
            epilogue(products(), extra, outs)
        else:
            first = pl.program_id(axes[0]) == 0
            last = pl.program_id(axes[0]) == grid[axes[0]] - 1
            for ax in axes[1:]:
                first = jnp.logical_and(first, pl.program_id(ax) == 0)
                last = jnp.logical_and(last, pl.program_id(ax) == grid[ax] - 1)

            @pl.when(first)
            def _():
                for acc in accs:
                    acc[...] = jnp.zeros(acc.shape, F32)

            for a, b, k in pairs:
                accs[k][...] += prod(a, b)

            @pl.when(last)
            def _():
                epilogue([acc[...] for acc in accs], extra, outs)

    scratch = [pltpu.VMEM(s, F32) for s in acc_shapes] if axes else []
    return pl.pallas_call(
        body, name=name, grid=grid, in_specs=specs, out_specs=out_specs, out_shape=out_shapes,
        scratch_shapes=scratch, compiler_params=_params())(*operands)


def _sigmoid(x):
    return 1.0 / (1.0 + jnp.exp(-x))


_GELU_C = 0.7978845608028654
_GELU_A = 0.044715


def _gelu(x):
    return 0.5 * x * (1.0 + jnp.tanh(_GELU_C * (x + _GELU_A * x * x * x)))


def _gelu_grad(x):
    t = jnp.tanh(_GELU_C * (x + _GELU_A * x * x * x))
    return 0.5 * (1.0 + t) + 0.5 * x * (1.0 - t * t) * _GELU_C * (1.0 + 3.0 * _GELU_A * x * x)


def _rms_fwd(name, x, g, after):
    S, D = x.shape
    tr = _tile(S, 256)

    def body(x_ref, g_ref, after_ref, o_ref):
        xv = x_ref[...]
        r = lax.rsqrt(jnp.mean(xv * xv, axis=-1, keepdims=True) + EPS)
        o_ref[...] = (xv * r * g_ref[...]).astype(MXU)

    return pl.pallas_call(
        body, name=name, grid=(S // tr,),
        in_specs=[pl.BlockSpec((tr, D), lambda i: (i, 0)), pl.BlockSpec((1, D), lambda i: (0, 0)), ANY],
        out_specs=pl.BlockSpec((tr, D), lambda i: (i, 0)),
        out_shape=jax.ShapeDtypeStruct((S, D), MXU), compiler_params=_params())(x, g, after)


def _rms_bwd(name, dh, x, g, dres, after):
    S, D = x.shape
    tr = _tile(S, 256)

    def body(dh_ref, x_ref, g_ref, dres_ref, after_ref, dx_ref, dxb_ref, dg_ref):
        xv = x_ref[...]
        r = lax.rsqrt(jnp.mean(xv * xv, axis=-1, keepdims=True) + EPS)
        xh = xv * r
        dhv = dh_ref[...]
        dy = dhv * g_ref[...]
        dx = dres_ref[...] + r * (dy - xh * jnp.mean(dy * xh, axis=-1, keepdims=True))
        dx_ref[...] = dx
        dxb_ref[...] = dx.astype(MXU)

        @pl.when(pl.program_id(0) == 0)
        def _():
            dg_ref[...] = jnp.zeros(dg_ref.shape, F32)

        dg_ref[...] += jnp.sum(dhv * xh, axis=0, keepdims=True)

    row = pl.BlockSpec((tr, D), lambda i: (i, 0))
    vec = pl.BlockSpec((1, D), lambda i: (0, 0))
    return pl.pallas_call(
        body, name=name, grid=(S // tr,), in_specs=[row, row, vec, row, ANY], out_specs=[row, row, vec],
        out_shape=[jax.ShapeDtypeStruct((S, D), F32), jax.ShapeDtypeStruct((S, D), MXU),
                   jax.ShapeDtypeStruct((1, D), F32)],
        compiler_params=_params())(dh, x, g, dres, after)


def _join_columns(name, parts):
    S = parts[0].shape[0]
    tr = _tile(S, 256)
    widths = [p.shape[1] for p in parts]
    assert all(w % BLOCK == 0 for w in widths)

    def body(*refs):
        o_ref, off = refs[-1], 0
        for ref, w in zip(refs[:-1], widths):
            o_ref[:, off:off + w] = ref[...]
            off += w

    return pl.pallas_call(
        body, name=name, grid=(S // tr,), in_specs=[pl.BlockSpec((tr, w), lambda i: (i, 0)) for w in widths],
        out_specs=pl.BlockSpec((tr, sum(widths)), lambda i: (i, 0)),
        out_shape=jax.ShapeDtypeStruct((S, sum(widths)), parts[0].dtype), compiler_params=_params())(*parts)


def _loss_head(y, target):
    S, D = y.shape
    tr = _tile(S, 256)

    def body(y_ref, t_ref, loss_ref, dy_ref, dyb_ref):
        d = y_ref[...] - t_ref[...]
        dy = d * (1.0 / D)
        dy_ref[...] = dy
        dyb_ref[...] = dy.astype(MXU)

        @pl.when(pl.program_id(0) == 0)
        def _():
            loss_ref[...] = jnp.zeros(loss_ref.shape, F32)

        loss_ref[...] += (0.5 / D) * jnp.sum(jnp.sum(d * d, axis=-1, keepdims=True), axis=0, keepdims=True)

    row = pl.BlockSpec((tr, D), lambda i: (i, 0))
    return pl.pallas_call(
        body, name="loss_head", grid=(S // tr,), in_specs=[row, row],
        out_specs=[pl.BlockSpec((1, 1), lambda i: (0, 0)), row, row],
        out_shape=[jax.ShapeDtypeStruct((1, 1), F32), jax.ShapeDtypeStruct((S, D), F32),
                   jax.ShapeDtypeStruct((S, D), MXU)],
        compiler_params=_params())(y, target)


def _head_sum(v):
    r = lax.broadcasted_iota(jnp.int32, (BLOCK, BLOCK), 0) // HEAD_DIM
    c = lax.broadcasted_iota(jnp.int32, (BLOCK, BLOCK), 1) // HEAD_DIM
    ones = jnp.where(r == c, 1.0, 0.0).astype(jnp.bfloat16)
    hi = v.astype(jnp.bfloat16)
    lo = (v - hi.astype(F32)).astype(jnp.bfloat16)
    parts = []
    for t in range(v.shape[1] // BLOCK):
        sl = slice(t * BLOCK, (t + 1) * BLOCK)
        parts.append(jnp.dot(hi[:, sl], ones, preferred_element_type=F32)
                     + jnp.dot(lo[:, sl], ones, preferred_element_type=F32))
    return parts[0] if len(parts) == 1 else jnp.concatenate(parts, axis=-1)


def _swap_halves(v):
    w = v.shape[1]
    half = HEAD_DIM // 2
    lane = lax.broadcasted_iota(jnp.int32, v.shape, 1) % HEAD_DIM
    return jnp.where(lane < half, pltpu.roll(v, w - half, 1), pltpu.roll(v, half, 1))


def _norm_rope(xv, gain, cos, sin):
    r = lax.rsqrt(_head_sum(xv * xv) * (1.0 / HEAD_DIM) + EPS)
    xn = xv * r * gain
    return xn * cos + _swap_halves(xn) * sin


def _norm_rope_bwd(dy, xv, gain, cos, sin):
    r = lax.rsqrt(_head_sum(xv * xv) * (1.0 / HEAD_DIM) + EPS)
    xh = xv * r
    dxn = dy * cos + _swap_halves(dy * sin)
    dgain = jnp.sum(dxn * xh, axis=0, keepdims=True)
    dxh = dxn * gain
    dx = r * (dxh - xh * (_head_sum(dxh * xh) * (1.0 / HEAD_DIM)))
    return dx, dgain


def _fold_heads(v):
    acc = v[:, 0:BLOCK]
    for t in range(1, v.shape[1] // BLOCK):
        acc = acc + v[:, t * BLOCK:(t + 1) * BLOCK]
    return acc + pltpu.roll(acc, HEAD_DIM, 1)


def _tile_lanes(v, width):
    return v if width == BLOCK else jnp.tile(v, (1, width // BLOCK))


def _low_half(rows):
    assert BLOCK == 2 * HEAD_DIM
    return lax.broadcasted_iota(jnp.int32, (rows, BLOCK), 1) < HEAD_DIM


def _spread_heads(v):
    low = _low_half(v.shape[0])
    out = []
    for t in range(v.shape[1] // BLOCK):
        tile = v[:, t * BLOCK:(t + 1) * BLOCK]
        swapped = pltpu.roll(tile, HEAD_DIM, 1)
        out += [jnp.where(low, tile, swapped), jnp.where(low, swapped, tile)]
    return jnp.concatenate(out, axis=-1)


def _gather_heads(v):
    low = _low_half(v.shape[0])
    out = []
    for t in range(v.shape[1] // (2 * BLOCK)):
        a, b = v[:, 2 * t * BLOCK:(2 * t + 1) * BLOCK], v[:, (2 * t + 1) * BLOCK:(2 * t + 2) * BLOCK]
        out.append(jnp.where(low, a + pltpu.roll(a, HEAD_DIM, 1), b + pltpu.roll(b, HEAD_DIM, 1)))
    return out[0] if len(out) == 1 else jnp.concatenate(out, axis=-1)


def _qk_prep(proj, qg, kg, cos, sin, AW, KW):
    S = proj.shape[0]
    tr = _tile(S, 256)
    scale = HEAD_DIM ** -0.5

    def body(q_ref, k_ref, v_ref, qg_ref, kg_ref, cos_ref, sin_ref, qo_ref, ko_ref, vo_ref):
        c, s = cos_ref[...], sin_ref[...]
        q = _norm_rope(q_ref[...], _tile_lanes(qg_ref[...], AW), _tile_lanes(c, AW), _tile_lanes(s, AW))
        k = _norm_rope(k_ref[...], _tile_lanes(kg_ref[...], KW), _tile_lanes(c, KW), _tile_lanes(s, KW))
        qo_ref[...] = (q * scale).astype(MXU)
        ko_ref[...] = _spread_heads(k).astype(MXU)
        vo_ref[...] = _spread_heads(v_ref[...]).astype(MXU)

    assert AW % KW == 0
    vec = pl.BlockSpec((1, BLOCK), lambda i: (0, 0))
    tab = pl.BlockSpec((tr, BLOCK), lambda i: (i, 0))
    wide = pl.BlockSpec((tr, 2 * KW), lambda i: (i, 0))
    return pl.pallas_call(
        body, name="qk_prep", grid=(S // tr,),
        in_specs=[pl.BlockSpec((tr, AW), lambda i: (i, 0)),
                  pl.BlockSpec((tr, KW), lambda i: (i, AW // KW)),
                  pl.BlockSpec((tr, KW), lambda i: (i, AW // KW + 1)), vec, vec, tab, tab],
        out_specs=[pl.BlockSpec((tr, AW), lambda i: (i, 0)), wide, wide],
        out_shape=[jax.ShapeDtypeStruct((S, AW), MXU), jax.ShapeDtypeStruct((S, 2 * KW), MXU),
                   jax.ShapeDtypeStruct((S, 2 * KW), MXU)],
        compiler_params=_params())(proj, proj, proj, qg, kg, cos, sin)


def _stack_heads(x, h0, nh):
    low = _low_half(BLOCK)
    parts = []
    for h in range(h0, h0 + nh):
        tile = x[:, (h // 2) * BLOCK:(h // 2 + 1) * BLOCK]
        parts.append(jnp.where(low if h % 2 == 0 else jnp.logical_not(low), tile, jnp.zeros_like(tile)))
    return jnp.concatenate(parts, axis=0)


def _unstack_heads(y):
    low = _low_half(BLOCK)
    tiles = [jnp.where(low, y[2 * t * BLOCK:(2 * t + 1) * BLOCK], y[(2 * t + 1) * BLOCK:(2 * t + 2) * BLOCK])
             for t in range(y.shape[0] // (2 * BLOCK))]
    return tiles[0] if len(tiles) == 1 else jnp.concatenate(tiles, axis=-1)


def _band_t(n):
    key = lax.broadcasted_iota(jnp.int32, (2 * BLOCK, BLOCK), 0)
    qry = lax.broadcasted_iota(jnp.int32, (2 * BLOCK, BLOCK), 1)
    return (key > qry) & (key <= qry + BLOCK) & ((key >= BLOCK) | (n > 0))


def _attn_probs_t(ok, qs, kcat, h0, nh, sink_ref):
    st = jnp.where(ok, lax.dot_general(kcat, qs, NT, preferred_element_type=F32), -1e30)
    sk = jnp.concatenate([jnp.full((1, BLOCK), sink_ref[0, h], F32) for h in range(h0, h0 + nh)], axis=1)
    m = jnp.maximum(jnp.max(st, axis=0, keepdims=True), sk)
    e = jnp.exp(st - m)
    es = jnp.exp(sk - m)
    rz = 1.0 / (jnp.sum(e, axis=0, keepdims=True) + es)
    return e * rz, es * rz, rz


def _attn_fwd(qr, kr, vb, sinks):
    S, AW = qr.shape
    KW = kr.shape[1]
    nb = S // BLOCK
    nkv = KW // BLOCK
    qpk = AW // (nkv * HEAD_DIM)
    nh = min(ATTN_STACK, qpk)
    assert nh % 2 == 0 and qpk % nh == 0

    def body(sink_ref, q_ref, kp_ref, kc_ref, vp_ref, vc_ref, o_ref):
        n = pl.program_id(0)
        q, kp, kc, vp, vc = q_ref[...], kp_ref[...], kc_ref[...], vp_ref[...], vc_ref[...]
        ok = jnp.concatenate([_band_t(n)] * nh, axis=1)
        outs = []
        for g in range(nkv):
            kcat = jnp.concatenate([kp[:, g * BLOCK:(g + 1) * BLOCK], kc[:, g * BLOCK:(g + 1) * BLOCK]], axis=0)
            vcat = jnp.concatenate([vp[:, g * BLOCK:(g + 1) * BLOCK], vc[:, g * BLOCK:(g + 1) * BLOCK]], axis=0)
            for h0 in range(g * qpk, (g + 1) * qpk, nh):
                pt, _, _ = _attn_probs_t(ok, _stack_heads(q, h0, nh), kcat, h0, nh, sink_ref)
                outs.append(_unstack_heads(lax.dot_general(pt.astype(MXU), vcat, TN, preferred_element_type=F32)))
        o_ref[...] = jnp.concatenate(outs, axis=-1).astype(MXU)

    cur = lambda n: (n, 0)
    prev = lambda n: (jnp.maximum(n - 1, 0), 0)
    return pl.pallas_call(
        body, name="attn_fwd", grid=(nb,),
        in_specs=[pl.BlockSpec(memory_space=pltpu.SMEM), pl.BlockSpec((BLOCK, AW), cur),
                  pl.BlockSpec((BLOCK, KW), prev), pl.BlockSpec((BLOCK, KW), cur),
                  pl.BlockSpec((BLOCK, KW), prev), pl.BlockSpec((BLOCK, KW), cur)],
        out_specs=pl.BlockSpec((BLOCK, AW), cur),
        out_shape=jax.ShapeDtypeStruct((S, AW), MXU), compiler_params=_params())(sinks, qr, kr, kr, vb, vb)


def _attn_bwd(qr, kr, vb, sinks, dattn):
    S, AW = qr.shape
    KW = kr.shape[1]
    nb = S // BLOCK
    nkv = KW // BLOCK
    qpk = AW // (nkv * HEAD_DIM)
    nh = min(ATTN_STACK, qpk)
    scale = HEAD_DIM ** -0.5

    def body(sink_ref, q_ref, kp_ref, kc_ref, vp_ref, vc_ref, do_ref,
             dq_ref, dkp_ref, dkc_ref, dvp_ref, dvc_ref, dsink_ref):
        n = pl.program_id(0)
        q, kp, kc, vp, vc = q_ref[...], kp_ref[...], kc_ref[...], vp_ref[...], vc_ref[...]
        do = do_ref[...].astype(MXU)
        lane = lax.broadcasted_iota(jnp.int32, (1, BLOCK), 1)
        ok = jnp.concatenate([_band_t(n)] * nh, axis=1)
        dsink = jnp.zeros((1, BLOCK), F32)
        dqs, dkps, dkcs, dvps, dvcs = [], [], [], [], []
        for g in range(nkv):
            kcat = jnp.concatenate([kp[:, g * BLOCK:(g + 1) * BLOCK], kc[:, g * BLOCK:(g + 1) * BLOCK]], axis=0)
            vcat = jnp.concatenate([vp[:, g * BLOCK:(g + 1) * BLOCK], vc[:, g * BLOCK:(g + 1) * BLOCK]], axis=0)
            dk, dv = None, None
            for h0 in range(g * qpk, (g + 1) * qpk, nh):
                qs = _stack_heads(q, h0, nh)
                dos = _stack_heads(do, h0, nh)
                pt, ps, _ = _attn_probs_t(ok, qs, kcat, h0, nh, sink_ref)
                dpt = lax.dot_general(vcat, dos, NT, preferred_element_type=F32)
                delta = jnp.sum(pt * dpt, axis=0, keepdims=True)
                dst = (pt * (dpt - delta)).astype(MXU)
                dsk = -ps * delta
                dv_part = jnp.dot(pt.astype(MXU), dos, preferred_element_type=F32)
                dk_part = jnp.dot(dst, qs, preferred_element_type=F32)
                dqs.append(_unstack_heads(lax.dot_general(dst, kcat, TN, preferred_element_type=F32) * scale))
                dk = dk_part if dk is None else dk + dk_part
                dv = dv_part if dv is None else dv + dv_part
                for j in range(nh):
                    tot = jnp.sum(dsk[:, j * BLOCK:(j + 1) * BLOCK], axis=1, keepdims=True)
                    dsink = dsink + jnp.where(lane == h0 + j, tot, 0.0)
            dkps.append(dk[:BLOCK])
            dkcs.append(dk[BLOCK:])
            dvps.append(dv[:BLOCK])
            dvcs.append(dv[BLOCK:])
        dq_ref[...] = jnp.concatenate(dqs, axis=-1)
        dkp_ref[...] = jnp.concatenate(dkps, axis=-1)
        dkc_ref[...] = jnp.concatenate(dkcs, axis=-1)
        dvp_ref[...] = jnp.concatenate(dvps, axis=-1)
        dvc_ref[...] = jnp.concatenate(dvcs, axis=-1)

        @pl.when(n == 0)
        def _():
            dsink_ref[...] = jnp.zeros(dsink_ref.shape, F32)

        dsink_ref[...] += dsink

    cur = lambda n: (n, 0)
    prev = lambda n: (jnp.maximum(n - 1, 0), 0)
    kv = jax.ShapeDtypeStruct((S, KW), F32)
    kvspec = pl.BlockSpec((BLOCK, KW), cur)
    return pl.pallas_call(
        body, name="attn_bwd", grid=(nb,),
        in_specs=[pl.BlockSpec(memory_space=pltpu.SMEM), pl.BlockSpec((BLOCK, AW), cur),
                  pl.BlockSpec((BLOCK, KW), prev), kvspec, pl.BlockSpec((BLOCK, KW), prev), kvspec,
                  pl.BlockSpec((BLOCK, AW), cur)],
        out_specs=[pl.BlockSpec((BLOCK, AW), cur), kvspec, kvspec, kvspec, kvspec,
                   pl.BlockSpec((1, BLOCK), lambda n: (0, 0))],
        out_shape=[jax.ShapeDtypeStruct((S, AW), F32), kv, kv, kv, kv, jax.ShapeDtypeStruct((1, BLOCK), F32)],
        compiler_params=_params())(sinks, qr, kr, kr, vb, vb, dattn)


def _qk_prep_bwd(proj, qg, kg, cos, sin, dq, dkp, dkc, dvp, dvc, AW, KW):
    S = proj.shape[0]
    nb = S // BLOCK

    def body(q_ref, k_ref, qg_ref, kg_ref, cos_ref, sin_ref, dq_ref, dkp_ref, dkc_ref, dvp_ref, dvc_ref,
             o_ref, dqg_ref, dkg_ref):
        n = pl.program_id(0)
        c, s = cos_ref[...], sin_ref[...]
        has_next = jnp.where(n < nb - 1, 1.0, 0.0)
        dk = _gather_heads(dkc_ref[...] + has_next * dkp_ref[...])
        dv = _gather_heads(dvc_ref[...] + has_next * dvp_ref[...])
        dxq, dqg = _norm_rope_bwd(dq_ref[...], q_ref[...], _tile_lanes(qg_ref[...], AW),
                                  _tile_lanes(c, AW), _tile_lanes(s, AW))
        dxk, dkg = _norm_rope_bwd(dk, k_ref[...], _tile_lanes(kg_ref[...], KW),
                                  _tile_lanes(c, KW), _tile_lanes(s, KW))
        o_ref[...] = jnp.concatenate([dxq, dxk, dv], axis=-1).astype(MXU)

        @pl.when(n == 0)
        def _():
            dqg_ref[...] = jnp.zeros(dqg_ref.shape, F32)
            dkg_ref[...] = jnp.zeros(dkg_ref.shape, F32)

        dqg_ref[...] += _fold_heads(dqg)
        dkg_ref[...] += _fold_heads(dkg)

    cur = lambda n: (n, 0)
    nxt = lambda n: (jnp.minimum(n + 1, nb - 1), 0)
    vec = pl.BlockSpec((1, BLOCK), lambda n: (0, 0))
    tab = pl.BlockSpec((BLOCK, BLOCK), cur)
    return pl.pallas_call(
        body, name="qk_prep_bwd", grid=(nb,),
        in_specs=[pl.BlockSpec((BLOCK, AW), cur), pl.BlockSpec((BLOCK, KW), lambda n: (n, AW // KW)),
                  vec, vec, tab, tab, pl.BlockSpec((BLOCK, AW), cur),
                  pl.BlockSpec((BLOCK, 2 * KW), nxt), pl.BlockSpec((BLOCK, 2 * KW), cur),
                  pl.BlockSpec((BLOCK, 2 * KW), nxt), pl.BlockSpec((BLOCK, 2 * KW), cur)],
        out_specs=[pl.BlockSpec((BLOCK, AW + 2 * KW), cur), vec, vec],
        out_shape=[jax.ShapeDtypeStruct((S, AW + 2 * KW), MXU), jax.ShapeDtypeStruct((1, BLOCK), F32),
                   jax.ShapeDtypeStruct((1, BLOCK), F32)],
        compiler_params=_params())(proj, proj, qg, kg, cos, sin, dq, dkp, dkc, dvp, dvc)


SGU_LANES = 512
SGU_ROWS = 256


def _sgu_group(v, lng, lnb, w_f32, b):
    rows = v.shape[0]
    mu = jnp.mean(v, axis=-1, keepdims=True)
    vc = v - mu
    r = lax.rsqrt(jnp.mean(vc * vc, axis=-1, keepdims=True) + EPS)
    xh = vc * r
    vn = (xh * lng + lnb).astype(MXU)
    row = lax.broadcasted_iota(jnp.int32, (BLOCK, BLOCK), 0)
    col = lax.broadcasted_iota(jnp.int32, (BLOCK, BLOCK), 1)
    tri = row >= col
    w = jnp.where(tri, w_f32, 0.0).astype(MXU)
    chunks = [jnp.dot(w, vn[k * BLOCK:(k + 1) * BLOCK], preferred_element_type=F32) + b for k in range(rows // BLOCK)]
    s = chunks[0] if len(chunks) == 1 else jnp.concatenate(chunks, axis=0)
    return xh, r, vn, w, s, tri


def _sgu_layout(S, u_col):
    SW = SGU_GROUPS * BLOCK
    lb, tr = min(SGU_LANES, SW), min(SGU_ROWS, S)
    assert u_col % lb == 0 and SW % lb == 0 and S % tr == 0
    ub, nlb, gpb = u_col // lb, SW // lb, lb // BLOCK
    specs = [pl.BlockSpec((tr, lb), lambda j, i: (i, ub + j)), pl.BlockSpec((tr, lb), lambda j, i: (i, ub + nlb + j)),
             pl.BlockSpec((1, lb), lambda j, i: (0, j)), pl.BlockSpec((1, lb), lambda j, i: (0, j)),
             pl.BlockSpec((gpb, BLOCK, BLOCK), lambda j, i: (j, 0, 0)),
             pl.BlockSpec((gpb, BLOCK, 1), lambda j, i: (j, 0, 0))]
    return lb, tr, gpb, nlb, specs


def _sgu_fwd(proj, lng, lnb, ws, bs, u_col):
    S = proj.shape[0]
    lb, tr, gpb, nlb, specs = _sgu_layout(S, u_col)

    def body(pu_ref, pv_ref, lng_ref, lnb_ref, w_ref, b_ref, o_ref):
        u = _gelu(pu_ref[...])
        v = _gelu(pv_ref[...])
        outs = []
        for g in range(gpb):
            sl = slice(g * BLOCK, (g + 1) * BLOCK)
            s = _sgu_group(v[:, sl], lng_ref[:, sl], lnb_ref[:, sl], w_ref[g], b_ref[g])[4]
            outs.append(u[:, sl] * s)
        o_ref[...] = (outs[0] if gpb == 1 else jnp.concatenate(outs, axis=-1)).astype(MXU)

    return pl.pallas_call(
        body, name="sgu_fwd", grid=(nlb, S // tr), in_specs=specs,
        out_specs=pl.BlockSpec((tr, lb), lambda j, i: (i, j)),
        out_shape=jax.ShapeDtypeStruct((S, nlb * lb), MXU), compiler_params=_params())(proj, proj, lng, lnb, ws, bs)


def _sgu_bwd(proj, lng, lnb, ws, bs, dsgu, u_col, after):
    S = proj.shape[0]
    G = SGU_GROUPS
    lb, tr, gpb, nlb, specs = _sgu_layout(S, u_col)
    nch = tr // BLOCK

    def body(pu_ref, pv_ref, lng_ref, lnb_ref, w_ref, b_ref, do_ref, after_ref,
             dpu_ref, dpv_ref, dw_ref, db_ref, dlng_ref, dlnb_ref):
        pu, pv, do = pu_ref[...], pv_ref[...], do_ref[...]
        u = _gelu(pu)
        v = _gelu(pv)

        @pl.when(pl.program_id(1) == 0)
        def _():
            dw_ref[...] = jnp.zeros(dw_ref.shape, F32)
            db_ref[...] = jnp.zeros(db_ref.shape, F32)
            dlng_ref[...] = jnp.zeros(dlng_ref.shape, F32)
            dlnb_ref[...] = jnp.zeros(dlnb_ref.shape, F32)

        ss, dvs, dlng, dlnb = [], [], [], []
        for g in range(gpb):
            sl = slice(g * BLOCK, (g + 1) * BLOCK)
            xh, r, vn, w, s, tri = _sgu_group(v[:, sl], lng_ref[:, sl], lnb_ref[:, sl], w_ref[g], b_ref[g])
            ds = do[:, sl] * u[:, sl]
            dsb = ds.astype(MXU)
            dw, db, dvn = None, None, []
            for k in range(nch):
                rows = slice(k * BLOCK, (k + 1) * BLOCK)
                part = lax.dot_general(dsb[rows], vn[rows], NT, preferred_element_type=F32)
                dw = part if dw is None else dw + part
                rowsum = jnp.sum(ds[rows], axis=-1, keepdims=True)
                db = rowsum if db is None else db + rowsum
                dvn.append(lax.dot_general(w, dsb[rows], TN, preferred_element_type=F32))
            dvn = dvn[0] if nch == 1 else jnp.concatenate(dvn, axis=0)
            dw_ref[g] += jnp.where(tri, dw, 0.0)
            db_ref[g] += db
            dxh = dvn * lng_ref[:, sl]
            dvs.append(r * (dxh - jnp.mean(dxh, axis=-1, keepdims=True)
                            - xh * jnp.mean(dxh * xh, axis=-1, keepdims=True)))
            dlng.append(jnp.sum(dvn * xh, axis=0, keepdims=True))
            dlnb.append(jnp.sum(dvn, axis=0, keepdims=True))
            ss.append(s)
        cat = lambda parts: parts[0] if gpb == 1 else jnp.concatenate(parts, axis=-1)
        dpu_ref[...] = (do * cat(ss) * _gelu_grad(pu)).astype(MXU)
        dpv_ref[...] = (cat(dvs) * _gelu_grad(pv)).astype(MXU)
        dlng_ref[...] += cat(dlng)
        dlnb_ref[...] += cat(dlnb)

    tile = pl.BlockSpec((tr, lb), lambda j, i: (i, j))
    vec = pl.BlockSpec((1, lb), lambda j, i: (0, j))
    half = jax.ShapeDtypeStruct((S, G * BLOCK), MXU)
    return pl.pallas_call(
        body, name="sgu_bwd", grid=(nlb, S // tr), in_specs=specs + [tile, ANY],
        out_specs=[tile, tile, pl.BlockSpec((gpb, BLOCK, BLOCK), lambda j, i: (j, 0, 0)),
                   pl.BlockSpec((gpb, BLOCK, 1), lambda j, i: (j, 0, 0)), vec, vec],
        out_shape=[half, half, jax.ShapeDtypeStruct((G, BLOCK, BLOCK), F32),
                   jax.ShapeDtypeStruct((G, BLOCK, 1), F32),
                   jax.ShapeDtypeStruct((1, G * BLOCK), F32), jax.ShapeDtypeStruct((1, G * BLOCK), F32)],
        compiler_params=_params())(proj, proj, lng, lnb, ws, bs, dsgu, after)


def _store_f32(vals, extra, outs):
    for v, o in zip(vals, outs):
        o[...] = v


def _store_mxu(vals, extra, outs):
    for v, o in zip(vals, outs):
        o[...] = v.astype(MXU)


def _proj_in(h, w):
    S, D = h.shape
    Ns = w.shape[2]
    tm, tn = _tile(S, 1024), _tile(Ns, 1024)
    npb = Ns // tn
    return _mm("proj_in", (S // tm, N_CHIPS, npb), 0, [h, w],
               [pl.BlockSpec((tm, D), lambda i, s, j: (i, 0)), pl.BlockSpec((None, D, tn), lambda i, s, j: (s, 0, j))],
               [(0, 1, 0)], NN, 0, [jax.ShapeDtypeStruct((S, N_CHIPS * Ns), F32)],
               [pl.BlockSpec((tm, tn), lambda i, s, j: (i, s * npb + j))], [None], _store_f32)[0]


def _branches(attn, sgu, wa, ws, proj, gate0):
    S, AW = attn.shape
    SW = sgu.shape[1]
    Nb = wa.shape[2]
    D = N_CHIPS * Nb
    tm = _tile(S, 512)
    assert gate0 % Nb == 0
    ga, gb = gate0 // Nb, (gate0 + D) // Nb

    def epilogue(vals, extra, outs):
        a, b = vals
        outs[0][...] = (_sigmoid(extra[0][...]) * a + _sigmoid(extra[1][...]) * b).astype(MXU)
        outs[1][...] = a
        outs[2][...] = b

    tile = pl.BlockSpec((tm, Nb), lambda i, s: (i, s))
    wspec = lambda k: pl.BlockSpec((None, k, Nb), lambda i, s: (s, 0, 0))
    f = jax.ShapeDtypeStruct((S, D), F32)
    return _mm("branches", (S // tm, N_CHIPS), 0, [attn, sgu, wa, ws, proj, proj],
               [pl.BlockSpec((tm, AW), lambda i, s: (i, 0)), pl.BlockSpec((tm, SW), lambda i, s: (i, 0)),
                wspec(AW), wspec(SW), pl.BlockSpec((tm, Nb), lambda i, s: (i, ga + s)),
                pl.BlockSpec((tm, Nb), lambda i, s: (i, gb + s))],
               [(0, 2, 0), (1, 3, 1)], NN, 2, [jax.ShapeDtypeStruct((S, D), MXU), f, f], [tile] * 3,
               [None, None], epilogue, chunk=MXU_CHUNK)


def _rows_mm(name, a, w, res):
    S = a.shape[0]
    _, K, N = w.shape
    tm, tn = _tile(S, 1024), _tile(N, 1024)

    def epilogue(vals, extra, outs):
        outs[0][...] = extra[0][...] + vals[0]

    out = pl.BlockSpec((tm, tn), lambda i, j, s: (i, j))
    return _mm(name, (S // tm, N // tn, N_CHIPS), 1, [a, w, res],
               [pl.BlockSpec((tm, K), lambda i, j, s: (i, s)), pl.BlockSpec((None, K, tn), lambda i, j, s: (s, 0, j)), out],
               [(0, 1, 0)], NN, 1, [jax.ShapeDtypeStruct((S, N), F32)], [out], [(tm, tn)], epilogue)[0]


def _gate_up(h2, wg, wu):
    S, D = h2.shape
    Nf = wg.shape[2]
    tm = _tile(S, 256)

    def epilogue(vals, extra, outs):
        g, u = vals
        outs[0][...] = g
        outs[1][...] = u
        outs[2][...] = (g * _sigmoid(g) * u).astype(MXU)

    w = pl.BlockSpec((None, D, Nf), lambda s, i: (s, 0, 0))
    o = pl.BlockSpec((tm, Nf), lambda s, i: (i, s))
    f = jax.ShapeDtypeStruct((S, N_CHIPS * Nf), F32)
    return _mm("gate_up", (N_CHIPS, S // tm), 0, [h2, wg, wu],
               [pl.BlockSpec((tm, D), lambda s, i: (i, 0)), w, w], [(0, 1, 0), (0, 2, 1)], NN, 0,
               [f, f, jax.ShapeDtypeStruct((S, N_CHIPS * Nf), MXU)], [o, o, o], [None, None], epilogue,
               chunk=MXU_CHUNK)


def _down_bwd(dyb, wd, g, u):
    S, D = dyb.shape
    Kf = wd.shape[1]
    tm = _tile(S, 512)

    def epilogue(vals, extra, outs):
        da, gv, uv = vals[0], extra[0][...], extra[1][...]
        sg = _sigmoid(gv)
        outs[0][...] = (da * uv * sg * (1.0 + gv * (1.0 - sg))).astype(MXU)
        outs[1][...] = (da * gv * sg).astype(MXU)

    t = pl.BlockSpec((tm, Kf), lambda i, s: (i, s))
    o = jax.ShapeDtypeStruct((S, N_CHIPS * Kf), MXU)
    return _mm("down_bwd", (S // tm, N_CHIPS), 0, [dyb, wd, g, u],
               [pl.BlockSpec((tm, D), lambda i, s: (i, 0)), pl.BlockSpec((None, Kf, D), lambda i, s: (s, 0, 0)), t, t],
               [(0, 1, 0)], NT, 2, [o, o], [t, t], [None], epilogue, chunk=MXU_CHUNK)


def _out_bwd(dxb, wo, proj, ba, bb, gate0):
    S, D = dxb.shape
    Ko = wo.shape[1]
    tm = _tile(S, 512)
    assert gate0 % Ko == 0
    ga, gb = gate0 // Ko, (gate0 + D) // Ko

    def epilogue(vals, extra, outs):
        dm = vals[0]
        sa, sb = _sigmoid(extra[0][...]), _sigmoid(extra[1][...])
        outs[0][...] = (dm * sa).astype(MXU)
        outs[1][...] = (dm * sb).astype(MXU)
        outs[2][...] = (dm * extra[2][...] * sa * (1.0 - sa)).astype(MXU)
        outs[3][...] = (dm * extra[3][...] * sb * (1.0 - sb)).astype(MXU)

    t = pl.BlockSpec((tm, Ko), lambda i, s: (i, s))
    o = jax.ShapeDtypeStruct((S, D), MXU)
    return _mm("out_bwd", (S // tm, N_CHIPS), 0, [dxb, wo, proj, proj, ba, bb],
               [pl.BlockSpec((tm, D), lambda i, s: (i, 0)), pl.BlockSpec((None, Ko, D), lambda i, s: (s, 0, 0)),
                pl.BlockSpec((tm, Ko), lambda i, s: (i, ga + s)), pl.BlockSpec((tm, Ko), lambda i, s: (i, gb + s)), t, t],
               [(0, 1, 0)], NT, 4, [o] * 4, [t] * 4, [None], epilogue, chunk=MXU_CHUNK)


def _dx_cols(name, terms, n_out, after=None):
    S = terms[0][0].shape[0]
    _, K, Ns = terms[0][1].shape
    tm, tko, tn = _tile(S, 1024), _tile(K, 1024), _tile(Ns, 1920 if len(terms) == 1 else 1408)
    npb = Ns // tn
    operands, specs, pairs = [], [], []
    for t, (dy, w, k) in enumerate(terms):
        assert w.shape == (N_CHIPS, K, Ns)
        operands += [dy, w]
        specs += [pl.BlockSpec((tm, tn), lambda i, jk, s, jn: (i, s * npb + jn)),
                  pl.BlockSpec((None, tko, tn), lambda i, jk, s, jn: (s, jk, jn))]
        pairs.append((2 * t, 2 * t + 1, k))
    out = pl.BlockSpec((tm, tko), lambda i, jk, s, jn: (i, jk))
    return _mm(name, (S // tm, K // tko, N_CHIPS, npb), 2, operands, specs, pairs, NT, 0,
               [jax.ShapeDtypeStruct((S, K), F32)] * n_out, [out] * n_out, [(tm, tko)] * n_out, _store_f32, after)


def _dw_cols(name, a, dy):
    S, K = a.shape
    Ns = dy.shape[1] // N_CHIPS
    tk, tn = _tile(K, 512), _tile(Ns, 1408)
    npb = Ns // tn
    return _mm(name, (K // tk, N_CHIPS, npb), 0, [a, dy],
               [pl.BlockSpec((S, tk), lambda jk, s, jn: (0, jk)), pl.BlockSpec((S, tn), lambda jk, s, jn: (0, s * npb + jn))],
               [(0, 1, 0)], TN, 0, [jax.ShapeDtypeStruct((N_CHIPS, K, Ns), MXU)],
               [pl.BlockSpec((None, tk, tn), lambda jk, s, jn: (s, jk, jn))], [None], _store_mxu)[0]


def _dw_rows(name, a, dy):
    S = a.shape[0]
    K = a.shape[1] // N_CHIPS
    N = dy.shape[1]
    tk, tn = _tile(K, 1408), _tile(N, 1024)
    nkb = K // tk
    return _mm(name, (N_CHIPS, nkb, N // tn), 0, [a, dy],
               [pl.BlockSpec((S, tk), lambda s, jk, jn: (0, s * nkb + jk)), pl.BlockSpec((S, tn), lambda s, jk, jn: (0, jn))],
               [(0, 1, 0)], TN, 0, [jax.ShapeDtypeStruct((N_CHIPS, K, N), MXU)],
               [pl.BlockSpec((None, tk, tn), lambda s, jk, jn: (s, jk, jn))], [None], _store_mxu)[0]


def _layer_fwd(x, stream, layer, after, sp, cos, sin, dims):
    AW, KW, gate0, u_col = dims
    h = _rms_fwd("mix_norm", x, sp["mix_norm"], after)
    stream.forward(layer, 0, h)
    w = stream.finish(layer, 0, h)
    proj = _proj_in(h, w["w_in"])
    qr, kr, vb = _qk_prep(proj, sp["q_norm"], sp["k_norm"], cos, sin, AW, KW)
    attn = _attn_fwd(qr, kr, vb, sp["sinks"])
    stream.forward(layer, 1, attn)
    sgu = _sgu_fwd(proj, sp["sgu_ln_g"], sp["sgu_ln_b"], sp["w_spatial"], sp["b_spatial"], u_col)
    w.update(stream.finish(layer, 1, sgu))
    merged, ba, bb = _branches(attn, sgu, w["w_attn_branch"], w["w_sgu_branch"], proj, gate0)
    x1 = _rows_mm("out_proj", merged, w["w_out"], x)
    h2 = _rms_fwd("ffn_norm", x1, sp["ffn_norm"], x1)
    stream.forward(layer, 2, h2)
    w.update(stream.finish(layer, 2, h2))
    g, u, act = _gate_up(h2, w["w_gate"], w["w_up"])
    stream.forward(layer, 3, g)
    w.update(stream.finish(layer, 3, g))
    x2 = _rows_mm("down_proj", act, w["w_down"], x1)
    saved = dict(x=x, h=h, proj=proj, qr=qr, kr=kr, vb=vb, attn=attn, sgu=sgu, merged=merged, ba=ba, bb=bb,
                 x1=x1, h2=h2, g=g, u=u, act=act)
    return x2, saved, w


def _layer_bwd(dy, dyb, w, sp, sv, cos, sin, dims, reducer, layer):
    AW, KW, gate0, u_col = dims
    big, small = {}, {}
    dg, du = _down_bwd(dyb, w["w_down"], sv["g"], sv["u"])
    big["w_down"] = _dw_rows("dw_down", sv["act"], dyb)
    big["w_gate"] = _dw_cols("dw_gate", sv["h2"], dg)
    big["w_up"] = _dw_cols("dw_up", sv["h2"], du)
    token = reducer.start(layer, 2, big)
    dh2 = _dx_cols("dh2", [(dg, w["w_gate"], 0), (du, w["w_up"], 0)], 1, token)[0]
    token = reducer.scatter(layer, 2, dh2)
    dx1, dx1b, small["ffn_norm"] = _rms_bwd("ffn_norm_bwd", dh2, sv["x1"], sp["ffn_norm"], dy, token)
    dba, dbb, dgla, dglb = _out_bwd(dx1b, w["w_out"], sv["proj"], sv["ba"], sv["bb"], gate0)
    big["w_out"] = _dw_rows("dw_out", sv["merged"], dx1b)
    big["w_attn_branch"] = _dw_cols("dw_attn_branch", sv["attn"], dba)
    big["w_sgu_branch"] = _dw_cols("dw_sgu_branch", sv["sgu"], dbb)
    token = reducer.start(layer, 1, big)
    dattn, dsgu = _dx_cols("dbranch_in", [(dba, w["w_attn_branch"], 0), (dbb, w["w_sgu_branch"], 1)], 2, token)
    token = reducer.scatter(layer, 1, dsgu)
    dpu, dpv, small["w_spatial"], db, small["sgu_ln_g"], small["sgu_ln_b"] = _sgu_bwd(
        sv["proj"], sp["sgu_ln_g"], sp["sgu_ln_b"], sp["w_spatial"], sp["b_spatial"], dsgu, u_col, token)
    small["b_spatial"] = db[:, :, 0]
    dq, dkp, dkc, dvp, dvc, dsink = _attn_bwd(sv["qr"], sv["kr"], sv["vb"], sp["sinks"], dattn)
    small["sinks"] = dsink[:, :sp["sinks"].shape[1]]
    dqkv, dqg, dkg = _qk_prep_bwd(sv["proj"], sp["q_norm"], sp["k_norm"], cos, sin, dq, dkp, dkc, dvp, dvc, AW, KW)
    small["q_norm"] = dqg[:, :HEAD_DIM]
    small["k_norm"] = dkg[:, :HEAD_DIM]
    dproj = _join_columns("dproj", [dqkv, dpu, dpv, dgla, dglb])
    big["w_in"] = _dw_cols("dw_in", sv["h"], dproj)
    token = reducer.start(layer, 0, big)
    dh = _dx_cols("dh", [(dproj, w["w_in"], 0)], 1, token)[0]
    token = reducer.scatter(layer, 0, dh)
    dx, dxb, small["mix_norm"] = _rms_bwd("mix_norm_bwd", dh, sv["x"], sp["mix_norm"], dx1, token)
    return dx, dxb, small


def _place():
    x, y, c = lax.axis_index("x"), lax.axis_index("y"), lax.axis_index("c")
    chips = [(1 - x, y), (x, 1 - y), (1 - x, 1 - y)]
    return x, y, c, chips


def _half_rows(c, rows):
    h = rows // 2
    assert h % 16 == 0
    return pl.ds(pl.multiple_of(c * h, 16), h)


def _row_tile(rows, pref):
    best = None
    for t in range(16, min(rows, pref) + 1, 16):
        if rows % t == 0:
            best = t
    assert best is not None, rows
    return best


def _cast_own(name, chip, w, layer):
    _, R, C = w.shape
    tr = _row_tile(R, 512)

    def body(chip_ref, w_ref, o_ref):
        o_ref[...] = w_ref[...].astype(MXU)

    return pl.pallas_call(
        body, name=name, out_shape=jax.ShapeDtypeStruct((N_CHIPS, R, C), MXU),
        grid_spec=pltpu.PrefetchScalarGridSpec(
            num_scalar_prefetch=1, grid=(R // tr,),
            in_specs=[pl.BlockSpec((None, tr, C), lambda i, chip_ref: (layer, i, 0))],
            out_specs=pl.BlockSpec((None, tr, C), lambda i, chip_ref: (chip_ref[0], i, 0))),
        compiler_params=_params())(chip, w)


HBM = pl.BlockSpec(memory_space=pltpu.HBM)
SEM = pl.BlockSpec(memory_space=pltpu.SEMAPHORE)
DATAFLOW = pltpu.SideEffectType.DATAFLOW_SIDE_EFFECTING


def _gather_copies(bufs, send_sem, recv_sem):
    x, y, c, chips = _place()

    def ici(a, j, block):
        px, py = chips[j]
        blk = bufs[a].at[block, _half_rows(c, bufs[a].shape[1])]
        return pltpu.make_async_remote_copy(
            src_ref=blk, dst_ref=blk, send_sem=send_sem.at[3 * a + j], recv_sem=recv_sem.at[3 * a + j],
            device_id=(px, py, c), device_id_type=MESH)

    def d2d(a, j, core):
        px, py = chips[j]
        blk = bufs[a].at[2 * px + py, _half_rows(core, bufs[a].shape[1])]
        return pltpu.make_async_remote_copy(
            src_ref=blk, dst_ref=blk, send_sem=send_sem.at[3 * a + j], recv_sem=recv_sem.at[3 * a + j],
            device_id=(x, y, 1 - c), device_id_type=MESH)

    return ici, d2d


def _in_hbm(bufs):
    return [pltpu.with_memory_space_constraint(b, pltpu.HBM) for b in bufs]


def _gather_start(name, bufs, after):
    n = len(bufs)

    def body(*refs):
        dst = refs[n + 1:2 * n + 1]
        send_sem, recv_sem, token = refs[2 * n + 1:]
        x, y, c, chips = _place()
        ici, _ = _gather_copies(dst, send_sem, recv_sem)
        for a in range(n):
            for j in range(3):
                ici(a, j, 2 * x + y).start()
        token[...] = jnp.zeros(token.shape, token.dtype)

    sems = pltpu.SemaphoreType.DMA((3 * n,))
    outs = pl.pallas_call(
        body, name=name, in_specs=[HBM] * n + [ANY],
        out_specs=[HBM] * n + [SEM, SEM, pl.BlockSpec(memory_space=pltpu.VMEM)],
        out_shape=[pltpu.HBM(b.shape, b.dtype) for b in bufs] + [sems, sems, jax.ShapeDtypeStruct((8, BLOCK), F32)],
        input_output_aliases={a: a for a in range(n)},
        compiler_params=pltpu.CompilerParams(has_side_effects=DATAFLOW))(*_in_hbm(bufs), after)
    return outs[:n], outs[n], outs[n + 1], outs[n + 2]


def _gather_forward(name, bufs, ici_send, ici_recv, after):
    n = len(bufs)

    def body(*refs):
        ici_send_ref, ici_recv_ref = refs[n], refs[n + 1]
        dst = refs[n + 3:2 * n + 3]
        d2d_send, d2d_recv = refs[2 * n + 3:]
        x, y, c, chips = _place()
        ici, _ = _gather_copies(dst, ici_send_ref, ici_recv_ref)
        _, d2d = _gather_copies(dst, d2d_send, d2d_recv)
        for a in range(n):
            for j, (px, py) in enumerate(chips):
                ici(a, j, 2 * px + py).wait_recv()
                d2d(a, j, c).start()
        for a in range(n):
            for j in range(3):
                ici(a, j, 2 * x + y).wait_send()

    sems = pltpu.SemaphoreType.DMA((3 * n,))
    outs = pl.pallas_call(
        body, name=name, in_specs=[HBM] * n + [SEM, SEM, ANY], out_specs=[HBM] * n + [SEM, SEM],
        out_shape=[pltpu.HBM(b.shape, b.dtype) for b in bufs] + [sems, sems],
        input_output_aliases={a: a for a in range(n)},
        compiler_params=pltpu.CompilerParams(has_side_effects=DATAFLOW))(*bufs, ici_send, ici_recv, after)
    return outs[:n], outs[n], outs[n + 1]


def _gather_finish(name, bufs, d2d_send, d2d_recv, after):
    n = len(bufs)

    def body(*refs):
        send_ref, recv_ref = refs[n], refs[n + 1]
        dst = refs[n + 3:]
        x, y, c, chips = _place()
        _, d2d = _gather_copies(dst, send_ref, recv_ref)
        for a in range(n):
            for j in range(3):
                d2d(a, j, 1 - c).wait_recv()
                d2d(a, j, c).wait_send()

    return pl.pallas_call(
        body, name=name, in_specs=[HBM] * n + [SEM, SEM, ANY], out_specs=[HBM] * n,
        out_shape=[pltpu.HBM(b.shape, b.dtype) for b in bufs],
        input_output_aliases={a: a for a in range(n)},
        compiler_params=pltpu.CompilerParams(has_side_effects=DATAFLOW))(*bufs, d2d_send, d2d_recv, after)


GATHER = (("w_in",), ("w_attn_branch", "w_sgu_branch", "w_out"), ("w_gate", "w_up"), ("w_down",))
REDUCE = (("w_in",), ("w_attn_branch", "w_sgu_branch", "w_out"), ("w_gate", "w_up", "w_down"))


class _WeightStream:
    def __init__(self, started):
        self.started, self.passed = started, {}

    def forward(self, layer, group, after):
        bufs, send, recv = self.started[(layer, group)]
        self.passed[(layer, group)] = _gather_forward("gather_forward_%d_%d" % (layer, group), bufs, send, recv, after)

    def finish(self, layer, group, after):
        bufs, send, recv = self.passed[(layer, group)]
        done = _gather_finish("gather_finish_%d_%d" % (layer, group), bufs, send, recv, after)
        return dict(zip(GATHER[group], done))


def _pair_copies(grads, lands, send_sem, recv_sem):
    x, y, c, _ = _place()

    def make(a):
        theirs = _half_rows(1 - c, grads[a].shape[1])
        return pltpu.make_async_remote_copy(
            src_ref=grads[a].at[:, theirs], dst_ref=lands[a], send_sem=send_sem.at[a], recv_sem=recv_sem.at[a],
            device_id=(x, y, 1 - c), device_id_type=MESH)

    return make


def _pair_start(name, grads, after):
    n = len(grads)
    lands = [lax.empty((g.shape[0], g.shape[1] // 2, g.shape[2]), g.dtype) for g in grads]

    def body(*refs):
        src, dst = refs[2 * n + 1:3 * n + 1], refs[3 * n + 1:4 * n + 1]
        send_sem, recv_sem, token = refs[4 * n + 1:]
        copy = _pair_copies(src, dst, send_sem, recv_sem)
        for a in range(n):
            copy(a).start()
        token[...] = jnp.zeros(token.shape, token.dtype)

    sems = pltpu.SemaphoreType.DMA((n,))
    outs = pl.pallas_call(
        body, name=name, in_specs=[HBM] * (2 * n) + [ANY],
        out_specs=[HBM] * (2 * n) + [SEM, SEM, pl.BlockSpec(memory_space=pltpu.VMEM)],
        out_shape=[pltpu.HBM(b.shape, b.dtype) for b in grads + lands] + [sems, sems, jax.ShapeDtypeStruct((8, BLOCK), F32)],
        input_output_aliases={a: a for a in range(2 * n)},
        compiler_params=pltpu.CompilerParams(has_side_effects=DATAFLOW))(*_in_hbm(grads + lands), after)
    return outs[:n], outs[n:2 * n], outs[2 * n], outs[2 * n + 1], outs[2 * n + 2]


def _pair_finish(name, grads, lands, send_sem, recv_sem, after):
    n = len(grads)

    def body(*refs):
        send_ref, recv_ref = refs[2 * n], refs[2 * n + 1]
        src, dst = refs[2 * n + 3:3 * n + 3], refs[3 * n + 3:]
        copy = _pair_copies(src, dst, send_ref, recv_ref)
        for a in range(n):
            copy(a).wait_send()
            copy(a).wait_recv()

    outs = pl.pallas_call(
        body, name=name, in_specs=[HBM] * (2 * n) + [SEM, SEM, ANY], out_specs=[HBM] * (2 * n),
        out_shape=[pltpu.HBM(b.shape, b.dtype) for b in grads + lands],
        input_output_aliases={a: a for a in range(2 * n)},
        compiler_params=pltpu.CompilerParams(has_side_effects=DATAFLOW))(*grads, *lands, send_sem, recv_sem, after)
    return outs[:n], outs[n:]


def _pair_sum(name, core, g, p):
    _, h, C = p.shape
    tr = _row_tile(h, 512)
    nrb = h // tr

    def body(core_ref, g_ref, p_ref, o_ref):
        o_ref[...] = (g_ref[...].astype(F32) + p_ref[...].astype(F32)).astype(o_ref.dtype)

    spec = pl.BlockSpec((None, tr, C), lambda s, i, core_ref: (s, i, 0))
    return pl.pallas_call(
        body, name=name, out_shape=jax.ShapeDtypeStruct(p.shape, p.dtype),
        grid_spec=pltpu.PrefetchScalarGridSpec(
            num_scalar_prefetch=1, grid=(N_CHIPS, nrb),
            in_specs=[pl.BlockSpec((None, tr, C), lambda s, i, core_ref: (s, core_ref[0] * nrb + i, 0)), spec],
            out_specs=spec),
        compiler_params=_params())(core, g, p)


def _scatter_copies(sums, slots, send_sem, recv_sem):
    x, y, c, chips = _place()

    def make(a, j):
        px, py = chips[j]
        return pltpu.make_async_remote_copy(
            src_ref=sums[a].at[2 * px + py], dst_ref=slots[a].at[j], send_sem=send_sem.at[3 * a + j],
            recv_sem=recv_sem.at[3 * a + j], device_id=(px, py, c), device_id_type=MESH)

    return make


def _scatter_start(name, sums, after):
    n = len(sums)
    slots = [lax.empty((3,) + s.shape[1:], s.dtype) for s in sums]

    def body(*refs):
        src, dst = refs[2 * n + 1:3 * n + 1], refs[3 * n + 1:4 * n + 1]
        send_sem, recv_sem, token = refs[4 * n + 1:]
        copy = _scatter_copies(src, dst, send_sem, recv_sem)
        for a in range(n):
            for j in range(3):
                copy(a, j).start()
        token[...] = jnp.zeros(token.shape, token.dtype)

    sems = pltpu.SemaphoreType.DMA((3 * n,))
    outs = pl.pallas_call(
        body, name=name, in_specs=[HBM] * (2 * n) + [ANY],
        out_specs=[HBM] * (2 * n) + [SEM, SEM, pl.BlockSpec(memory_space=pltpu.VMEM)],
        out_shape=[pltpu.HBM(b.shape, b.dtype) for b in sums + slots] + [sems, sems, jax.ShapeDtypeStruct((8, BLOCK), F32)],
        input_output_aliases={a: a for a in range(2 * n)},
        compiler_params=pltpu.CompilerParams(has_side_effects=DATAFLOW))(*_in_hbm(sums + slots), after)
    return outs[:n], outs[n:2 * n], outs[2 * n], outs[2 * n + 1], outs[2 * n + 2]


def _scatter_finish(name, sums, slots, send_sem, recv_sem, after):
    n = len(sums)

    def body(*refs):
        send_ref, recv_ref = refs[2 * n], refs[2 * n + 1]
        src, dst = refs[2 * n + 3:3 * n + 3], refs[3 * n + 3:]
        copy = _scatter_copies(src, dst, send_ref, recv_ref)
        for a in range(n):
            for j in range(3):
                copy(a, j).wait_send()
                copy(a, j).wait_recv()

    outs = pl.pallas_call(
        body, name=name, in_specs=[HBM] * (2 * n) + [SEM, SEM, ANY], out_specs=[HBM] * (2 * n),
        out_shape=[pltpu.HBM(b.shape, b.dtype) for b in sums + slots],
        input_output_aliases={a: a for a in range(2 * n)},
        compiler_params=pltpu.CompilerParams(has_side_effects=DATAFLOW))(*sums, *slots, send_sem, recv_sem, after)
    return outs[:n], outs[n:]


def _slot_sum(name, place, slots, sums):
    _, h, C = slots.shape
    tr = _row_tile(h, 512)
    nrb = h // tr

    def body(place_ref, r0, r1, r2, own, o_ref):
        o_ref[...] = ((r0[...].astype(F32) + r1[...].astype(F32)) + r2[...].astype(F32)) + own[...].astype(F32)

    slot = lambda k: pl.BlockSpec((None, tr, C), lambda i, place_ref: (k, i, 0))
    return pl.pallas_call(
        body, name=name, out_shape=jax.ShapeDtypeStruct((2 * h, C), F32),
        grid_spec=pltpu.PrefetchScalarGridSpec(
            num_scalar_prefetch=1, grid=(nrb,),
            in_specs=[slot(0), slot(1), slot(2),
                      pl.BlockSpec((None, tr, C), lambda i, place_ref: (place_ref[0], i, 0))],
            out_specs=pl.BlockSpec((tr, C), lambda i, place_ref: (place_ref[1] * nrb + i, 0))),
        compiler_params=_params())(place, slots, slots, slots, sums)


def _half_copies(bufs, send_sem, recv_sem):
    x, y, c, _ = _place()

    def make(a, core):
        rows = bufs[a].at[_half_rows(core, bufs[a].shape[0])]
        return pltpu.make_async_remote_copy(
            src_ref=rows, dst_ref=rows, send_sem=send_sem.at[a], recv_sem=recv_sem.at[a],
            device_id=(x, y, 1 - c), device_id_type=MESH)

    return make


def _half_start(name, bufs, after):
    n = len(bufs)

    def body(*refs):
        dst = refs[n + 1:2 * n + 1]
        send_sem, recv_sem, token = refs[2 * n + 1:]
        c = lax.axis_index("c")
        copy = _half_copies(dst, send_sem, recv_sem)
        for a in range(n):
            copy(a, c).start()
        token[...] = jnp.zeros(token.shape, token.dtype)

    sems = pltpu.SemaphoreType.DMA((n,))
    outs = pl.pallas_call(
        body, name=name, in_specs=[HBM] * n + [ANY],
        out_specs=[HBM] * n + [SEM, SEM, pl.BlockSpec(memory_space=pltpu.VMEM)],
        out_shape=[pltpu.HBM(b.shape, b.dtype) for b in bufs] + [sems, sems, jax.ShapeDtypeStruct((8, BLOCK), F32)],
        input_output_aliases={a: a for a in range(n)},
        compiler_params=pltpu.CompilerParams(has_side_effects=DATAFLOW))(*_in_hbm(bufs), after)
    return outs[:n], outs[n], outs[n + 1], outs[n + 2]


def _half_finish(name, bufs, send_sem, recv_sem, after):
    n = len(bufs)

    def body(*refs):
        send_ref, recv_ref = refs[n], refs[n + 1]
        dst = refs[n + 3:]
        c = lax.axis_index("c")
        copy = _half_copies(dst, send_ref, recv_ref)
        for a in range(n):
            copy(a, c).wait_send()
            copy(a, 1 - c).wait_recv()

    return pl.pallas_call(
        body, name=name, in_specs=[HBM] * n + [SEM, SEM, ANY], out_specs=[HBM] * n,
        out_shape=[pltpu.HBM(b.shape, b.dtype) for b in bufs],
        input_output_aliases={a: a for a in range(n)},
        compiler_params=pltpu.CompilerParams(has_side_effects=DATAFLOW))(*bufs, send_sem, recv_sem, after)


class _GradReducer:
    def __init__(self, chip, core):
        self.core, self.place, self.pairs, self.started = core, jnp.concatenate([chip, core]), {}, []

    def start(self, layer, group, grads):
        mine = [grads[n] for n in REDUCE[group]]
        mine, lands, send, recv, token = _pair_start("grad_pair_start_%d_%d" % (layer, group), mine, self.place)
        self.pairs[(layer, group)] = (mine, lands, send, recv)
        return token

    def scatter(self, layer, group, after):
        tag = "%d_%d" % (layer, group)
        names = REDUCE[group]
        mine, lands, send, recv = self.pairs.pop((layer, group))
        mine, theirs = _pair_finish("grad_pair_finish_" + tag, mine, lands, send, recv, after)
        sums = [_pair_sum("pair_sum_%s_%d" % (n, layer), self.core, g, p) for n, g, p in zip(names, mine, theirs)]
        sums, slots, send, recv, token = _scatter_start("grad_scatter_start_" + tag, sums, self.place)
        self.started.append((layer, names, sums, slots, send, recv))
        return token

    def finish(self, after, update):
        for layer in sorted({entry[0] for entry in self.started}, reverse=True):
            exchanged = []
            for lyr, names, sums, slots, send, recv in self.started:
                if lyr != layer:
                    continue
                tag = "%s_%d" % (names[0], layer)
                sums, slots = _scatter_finish("grad_scatter_finish_" + tag, sums, slots, send, recv, after)
                halves = [_slot_sum("slot_sum_%s_%d" % (n, layer), self.place, r, s)
                          for n, r, s in zip(names, slots, sums)]
                halves, send, recv, after = _half_start("grad_half_start_" + tag, halves, self.place)
                exchanged.append((tag, names, halves, send, recv))
            for tag, names, halves, send, recv in exchanged:
                whole = _half_finish("grad_half_finish_" + tag, halves, send, recv, after)
                after = update(layer, dict(zip(names, whole)))


def _small_copies(arrays, lands, own_sems, pass_sems):
    def slot(a, block):
        px, py, pc = block
        return lands[a].at[4 * px + 2 * py + pc]

    def own(a, k, block, to):
        return pltpu.make_async_remote_copy(
            src_ref=arrays[a], dst_ref=slot(a, block), send_sem=own_sems[0].at[4 * a + k],
            recv_sem=own_sems[1].at[4 * a + k], device_id=to, device_id_type=MESH)

    def passed(a, j, block, to):
        return pltpu.make_async_remote_copy(
            src_ref=slot(a, block), dst_ref=slot(a, block), send_sem=pass_sems[0].at[3 * a + j],
            recv_sem=pass_sems[1].at[3 * a + j], device_id=to, device_id_type=MESH)

    return own, passed


def _small_start(arrays, after):
    n = len(arrays)
    lands = [lax.empty((2 * N_CHIPS,) + a.shape, a.dtype) for a in arrays]

    def body(*refs):
        src, dst = refs[2 * n + 1:3 * n + 1], refs[3 * n + 1:4 * n + 1]
        send_sem, recv_sem, token = refs[4 * n + 1:]
        x, y, c, chips = _place()
        own, _ = _small_copies(src, dst, (send_sem, recv_sem), None)
        for a in range(n):
            own(a, 0, (x, y, c), (x, y, 1 - c)).start()
            for j, (px, py) in enumerate(chips):
                own(a, 1 + j, (x, y, c), (px, py, c)).start()
        token[...] = jnp.zeros(token.shape, token.dtype)

    sems = pltpu.SemaphoreType.DMA((4 * n,))
    outs = pl.pallas_call(
        body, name="small_grad_start", in_specs=[HBM] * (2 * n) + [ANY],
        out_specs=[HBM] * (2 * n) + [SEM, SEM, pl.BlockSpec(memory_space=pltpu.VMEM)],
        out_shape=[pltpu.HBM(b.shape, b.dtype) for b in arrays + lands]
        + [sems, sems, jax.ShapeDtypeStruct((8, BLOCK), F32)],
        input_output_aliases={a: a for a in range(2 * n)},
        compiler_params=pltpu.CompilerParams(has_side_effects=DATAFLOW))(*_in_hbm(arrays + lands), after)
    return outs[:n], outs[n:2 * n], (outs[2 * n], outs[2 * n + 1]), outs[2 * n + 2]


def _small_forward(arrays, lands, own_sems, after):
    n = len(arrays)

    def body(*refs):
        own_refs = (refs[2 * n], refs[2 * n + 1])
        src, dst = refs[2 * n + 3:3 * n + 3], refs[3 * n + 3:4 * n + 3]
        pass_refs = (refs[4 * n + 3], refs[4 * n + 4])
        x, y, c, chips = _place()
        own, passed = _small_copies(src, dst, own_refs, pass_refs)
        for a in range(n):
            for j, (px, py) in enumerate(chips):
                own(a, 1 + j, (px, py, c), (x, y, c)).wait_recv()
                passed(a, j, (px, py, c), (x, y, 1 - c)).start()
        for a in range(n):
            own(a, 0, (x, y, 1 - c), (x, y, c)).wait_recv()
            own(a, 0, (x, y, c), (x, y, 1 - c)).wait_send()
            for j, (px, py) in enumerate(chips):
                own(a, 1 + j, (x, y, c), (px, py, c)).wait_send()

    sems = pltpu.SemaphoreType.DMA((3 * n,))
    outs = pl.pallas_call(
        body, name="small_grad_forward", in_specs=[HBM] * (2 * n) + [SEM, SEM, ANY],
        out_specs=[HBM] * (2 * n) + [SEM, SEM],
        out_shape=[pltpu.HBM(b.shape, b.dtype) for b in arrays + lands] + [sems, sems],
        input_output_aliases={a: a for a in range(2 * n)},
        compiler_params=pltpu.CompilerParams(has_side_effects=DATAFLOW))(*arrays, *lands, *own_sems, after)
    return outs[:n], outs[n:2 * n], (outs[2 * n], outs[2 * n + 1])


def _small_finish(arrays, lands, pass_sems, after):
    n = len(arrays)

    def body(*refs):
        pass_refs = (refs[2 * n], refs[2 * n + 1])
        src, dst = refs[2 * n + 3:3 * n + 3], refs[3 * n + 3:]
        x, y, c, chips = _place()
        _, passed = _small_copies(src, dst, None, pass_refs)
        for a in range(n):
            for j, (px, py) in enumerate(chips):
                passed(a, j, (px, py, 1 - c), (x, y, c)).wait_recv()
                passed(a, j, (px, py, c), (x, y, 1 - c)).wait_send()

    outs = pl.pallas_call(
        body, name="small_grad_finish", in_specs=[HBM] * (2 * n) + [SEM, SEM, ANY], out_specs=[HBM] * (2 * n),
        out_shape=[pltpu.HBM(b.shape, b.dtype) for b in arrays + lands],
        input_output_aliases={a: a for a in range(2 * n)},
        compiler_params=pltpu.CompilerParams(has_side_effects=DATAFLOW))(*arrays, *lands, *pass_sems, after)
    return outs[:n], outs[n:]


def _small_sum(place, arrays, lands):
    n = len(arrays)
    n_dev = 2 * N_CHIPS

    def body(place_ref, *refs):
        me = 2 * place_ref[0] + place_ref[1]
        for a in range(n):
            own, land, out = refs[a], refs[n + a], refs[2 * n + a]
            acc = None
            for d in range(n_dev):
                term = jnp.where(me == d, own[...], land[jnp.where(me == d, d ^ 1, d)])
                acc = term if acc is None else acc + term
            out[...] = acc

    vm = pl.BlockSpec(memory_space=pltpu.VMEM)
    return pl.pallas_call(
        body, name="small_grad_sum", in_specs=[pl.BlockSpec(memory_space=pltpu.SMEM)] + [vm] * (2 * n),
        out_specs=[vm] * n, out_shape=[jax.ShapeDtypeStruct(a.shape, F32) for a in arrays],
        compiler_params=_params())(place, *arrays, *lands)


def _adamw_math(w, g, m, v):
    m2 = ADAM_B1 * m + (1.0 - ADAM_B1) * g
    v2 = ADAM_B2 * v + (1.0 - ADAM_B2) * (g * g)
    m_hat = m2 / (1.0 - ADAM_B1 ** ADAM_STEP)
    v_hat = v2 / (1.0 - ADAM_B2 ** ADAM_STEP)
    delta = -ADAM_LR * (m_hat / (jnp.sqrt(v_hat) + ADAM_EPS) + ADAM_WD * w)
    return delta, m2, v2


def _adamw_big(name, layer, grad, w, m, v, others):
    L, R, C = w.shape
    tr = _row_tile(R, 256)

    def body(g_ref, w_ref, m_ref, v_ref, *rest):
        go_ref, d_ref, mo_ref, vo_ref = rest[-4:]
        g = g_ref[...]
        delta, m2, v2 = _adamw_math(w_ref[...], g, m_ref[...], v_ref[...])
        go_ref[...] = g
        d_ref[...] = delta
        mo_ref[...] = m2
        vo_ref[...] = v2

    blk = pl.BlockSpec((None, tr, C), lambda i: (layer, i, 0))
    shp = jax.ShapeDtypeStruct(w.shape, F32)
    others = [] if others is None else list(others)
    return pl.pallas_call(
        body, name=name, grid=(R // tr,),
        in_specs=[pl.BlockSpec((tr, C), lambda i: (i, 0))] + [blk] * 3 + [ANY] * len(others), out_specs=[blk] * 4,
        out_shape=[shp] * 4, input_output_aliases={4 + k: k for k in range(len(others))},
        compiler_params=_params())(grad, w, m, v, *others)


def _adamw_small(gs, ws, ms, vs):
    n = len(gs)

    def body(*refs):
        for a in range(n):
            g_ref, w_ref, m_ref, v_ref = refs[a], refs[n + a], refs[2 * n + a], refs[3 * n + a]
            delta, m2, v2 = _adamw_math(w_ref[...], g_ref[...], m_ref[...], v_ref[...])
            refs[4 * n + a][...] = delta
            refs[5 * n + a][...] = m2
            refs[6 * n + a][...] = v2

    vm = pl.BlockSpec(memory_space=pltpu.VMEM)
    shapes = [jax.ShapeDtypeStruct(g.shape, F32) for g in gs]
    outs = pl.pallas_call(
        body, name="adamw_small", in_specs=[vm] * (4 * n), out_specs=[vm] * (3 * n), out_shape=shapes * 3,
        compiler_params=_params())(*gs, *ws, *ms, *vs)
    return outs[:n], outs[n:2 * n], outs[2 * n:]


def _rows2d(a):
    return a if a.ndim == 2 else a.reshape(-1, a.shape[-1])


BIG = ("w_in", "w_attn_branch", "w_sgu_branch", "w_out", "w_gate", "w_up", "w_down")
SMALL = ("mix_norm", "q_norm", "k_norm", "sinks", "sgu_ln_g", "sgu_ln_b", "w_spatial", "b_spatial", "ffn_norm")
ORDER = ("mix_norm", "w_in", "q_norm", "k_norm", "sinks", "sgu_ln_g", "sgu_ln_b", "w_spatial", "b_spatial",
         "w_attn_branch", "w_sgu_branch", "w_out", "ffn_norm", "w_gate", "w_up", "w_down")


def _rope_tables(seq):
    pos = jnp.arange(seq, dtype=F32)
    inv_freq = jnp.power(10000.0, -jnp.arange(0, HEAD_DIM, 2, dtype=F32) / HEAD_DIM)
    ang = pos[:, None] * inv_freq[None, :]
    cos, sin = jnp.cos(ang), jnp.sin(ang)
    reps = BLOCK // HEAD_DIM
    return (jnp.tile(jnp.concatenate([cos, cos], axis=1), (1, reps)),
            jnp.tile(jnp.concatenate([-sin, sin], axis=1), (1, reps)))


def kernel(x, mix_norm, w_in, q_norm, k_norm, sinks, sgu_ln_g, sgu_ln_b, w_spatial, b_spatial, w_attn_branch, w_sgu_branch, w_out, ffn_norm, w_gate, w_up, w_down, loss_target, m_mix_norm, m_w_in, m_q_norm, m_k_norm, m_sinks, m_sgu_ln_g, m_sgu_ln_b, m_w_spatial, m_b_spatial, m_w_attn_branch, m_w_sgu_branch, m_w_out, m_ffn_norm, m_w_gate, m_w_up, m_w_down, v_mix_norm, v_w_in, v_q_norm, v_k_norm, v_sinks, v_sgu_ln_g, v_sgu_ln_b, v_w_spatial, v_b_spatial, v_w_attn_branch, v_w_sgu_branch, v_w_out, v_ffn_norm, v_w_gate, v_w_up, v_w_down):
    weights = dict(mix_norm=mix_norm, w_in=w_in, q_norm=q_norm, k_norm=k_norm, sinks=sinks, sgu_ln_g=sgu_ln_g,
                   sgu_ln_b=sgu_ln_b, w_spatial=w_spatial, b_spatial=b_spatial, w_attn_branch=w_attn_branch,
                   w_sgu_branch=w_sgu_branch, w_out=w_out, ffn_norm=ffn_norm, w_gate=w_gate, w_up=w_up, w_down=w_down)
    mom1 = dict(mix_norm=m_mix_norm, w_in=m_w_in, q_norm=m_q_norm, k_norm=m_k_norm, sinks=m_sinks,
                sgu_ln_g=m_sgu_ln_g, sgu_ln_b=m_sgu_ln_b, w_spatial=m_w_spatial, b_spatial=m_b_spatial,
                w_attn_branch=m_w_attn_branch, w_sgu_branch=m_w_sgu_branch, w_out=m_w_out, ffn_norm=m_ffn_norm,
                w_gate=m_w_gate, w_up=m_w_up, w_down=m_w_down)
    mom2 = dict(mix_norm=v_mix_norm, w_in=v_w_in, q_norm=v_q_norm, k_norm=v_k_norm, sinks=v_sinks,
                sgu_ln_g=v_sgu_ln_g, sgu_ln_b=v_sgu_ln_b, w_spatial=v_w_spatial, b_spatial=v_b_spatial,
                w_attn_branch=v_w_attn_branch, w_sgu_branch=v_w_sgu_branch, w_out=v_w_out, ffn_norm=v_ffn_norm,
                w_gate=v_w_gate, w_up=v_w_up, w_down=v_w_down)
    xs, target = x[0], loss_target[0]
    S, D = xs.shape
    L = w_in.shape[0]
    AW, KW, SW = N_Q_HEADS * HEAD_DIM, N_KV_HEADS * HEAD_DIM, SGU_GROUPS * BLOCK
    dims = (AW, KW, AW + 2 * KW + 2 * SW, AW + 2 * KW)
    cos, sin = _rope_tables(S)
    reps = BLOCK // HEAD_DIM

    chip = (2 * lax.axis_index("x") + lax.axis_index("y")).astype(jnp.int32).reshape(1)
    core = lax.axis_index("c").astype(jnp.int32).reshape(1)
    started, token = {}, chip
    for l in range(L):
        for gi, names in enumerate(GATHER):
            bufs = [_cast_own("cast_%s_%d" % (n, l), chip, weights[n], l) for n in names]
            bufs, send, recv, token = _gather_start("gather_start_%d_%d" % (l, gi), bufs, token)
            started[(l, gi)] = (bufs, send, recv)
    stream = _WeightStream(started)
    sp = [dict(mix_norm=mix_norm[l][None], ffn_norm=ffn_norm[l][None], q_norm=jnp.tile(q_norm[l][None], (1, reps)),
               k_norm=jnp.tile(k_norm[l][None], (1, reps)), sinks=sinks[l][None], sgu_ln_g=sgu_ln_g[l][None],
               sgu_ln_b=sgu_ln_b[l][None], w_spatial=w_spatial[l], b_spatial=b_spatial[l][:, :, None])
          for l in range(L)]

    act, saved, wl = xs, [], []
    for l in range(L):
        act, sv, w_all = _layer_fwd(act, stream, l, token if l == 0 else act, sp[l], cos, sin, dims)
        saved.append(sv)
        wl.append(w_all)
    loss_part, dy, dyb = _loss_head(act, target)
    loss = lax.psum(loss_part[0, 0], ("x", "y", "c"))

    reducer = _GradReducer(chip, core)
    small_g = [None] * L
    for l in reversed(range(L)):
        dy, dyb, small_g[l] = _layer_bwd(dy, dyb, wl[l], sp[l], saved[l], cos, sin, dims, reducer, l)
    grad_x = dy[None]

    local = [_rows2d(jnp.stack([small_g[l][n].reshape(weights[n].shape[1:]) for l in range(L)])) for n in SMALL]
    small = list(_small_start(local, dy))
    updated = {}

    def update(layer, reduced):
        for n, g in reduced.items():
            updated[n] = _adamw_big("adamw_%s_%d" % (n, layer), layer, g, weights[n], mom1[n], mom2[n],
                                    updated.get(n))
        if len(small) == 4:
            arrays, lands, own_sems, _ = small
            small[:] = _small_forward(arrays, lands, own_sems, updated[n][0])
        return updated[n][0]

    reducer.finish(small[3], update)
    grads, deltas, new_m, new_v = {}, {}, {}, {}
    for n in BIG:
        grads[n], deltas[n], new_m[n], new_v[n] = updated[n]

    arrays, lands = _small_finish(*small, updated[BIG[0]][0])
    g_small = _small_sum(jnp.concatenate([chip, core]), arrays, lands)
    d_small, m_small, v_small = _adamw_small(g_small, [_rows2d(weights[n]) for n in SMALL],
                                             [_rows2d(mom1[n]) for n in SMALL], [_rows2d(mom2[n]) for n in SMALL])
    for n, g, d, m2, v2 in zip(SMALL, g_small, d_small, m_small, v_small):
        shape = weights[n].shape
        grads[n], deltas[n], new_m[n], new_v[n] = g.reshape(shape), d.reshape(shape), m2.reshape(shape), v2.reshape(shape)

    return (loss, grad_x, *[grads[n] for n in ORDER], *[deltas[n] for n in ORDER],
            *[new_m[n] for n in ORDER], *[new_v[n] for n in ORDER])
```

```python
import functools

import jax
import jax.numpy as jnp
from jax import lax
from jax.experimental import pallas as pl
from jax.experimental.pallas import tpu as pltpu

HEAD_DIM = 64
N_Q_HEADS = 16
N_KV_HEADS = 4
SGU_GROUPS = 8
BLOCK = 128
EPS = 1e-6
ADAM_LR = 0.001
ADAM_B1 = 0.9
ADAM_B2 = 0.999
ADAM_EPS = 1e-08
ADAM_WD = 0.01
ADAM_STEP = 10
N_CHIPS = 4
VMEM_LIMIT = 52 * 1024 * 1024
MXU_CHUNK = 256
ATTN_STACK = 4

F32 = jnp.float32
MXU = jnp.bfloat16
NN = (((1,), (0,)), ((), ()))
NT = (((1,), (1,)), ((), ()))
TN = (((0,), (0,)), ((), ()))
MESH = pl.DeviceIdType.MESH
ANY = pl.BlockSpec(memory_space=pl.ANY)


def _tile(n, pref):
    if n <= pref:
        return n
    best = None
    for t in range(BLOCK, pref + 1, BLOCK):
        if n % t == 0:
            best = t
    assert best is not None, (n, pref)
    return best


def _params():
    return pltpu.CompilerParams(vmem_limit_bytes=VMEM_LIMIT)


def _mm(name, grid, n_red, operands, specs, pairs, dims, n_extra, out_shapes, out_specs,
        acc_shapes, epilogue, after=None, chunk=None):
    n_op = len(operands) - n_extra
    n_out = len(out_shapes)
    n_acc = len(acc_shapes)
    if after is not None:
        operands, specs = list(operands) + [after], list(specs) + [ANY]
    n_in = len(operands)
    axes = [ax for ax in range(len(grid) - n_red, len(grid)) if grid[ax] > 1]

    def body(*refs):
        ops = refs[:n_op]
        extra = refs[n_op:n_op + n_extra]
        outs = refs[n_in:n_in + n_out]
        accs = refs[n_in + n_out:]

        def prod(a, b, cols=None):
            rhs = ops[b]
            if cols is not None:
                rhs = rhs.at[:, cols] if dims == NN else rhs.at[cols, :]
            return lax.dot_general(ops[a][...], rhs[...], dims, preferred_element_type=F32)

        def products(cols=None):
            vals = [None] * n_acc
            for a, b, k in pairs:
                d = prod(a, b, cols)
                vals[k] = d if vals[k] is None else vals[k] + d
            return vals

        if not axes and chunk is not None:
            width = outs[0].shape[-1]
            for c0 in range(0, width, chunk):
                cols = pl.ds(c0, min(chunk, width - c0))
                epilogue(products(cols), [e.at[:, cols] for e in extra], [o.at[:, cols] for o in outs])
        elif not axes:
            epilogue(products(), extra, outs)
        else:
            first = pl.program_id(axes[0]) == 0
            last = pl.program_id(axes[0]) == grid[axes[0]] - 1
            for ax in axes[1:]:
                first = jnp.logical_and(first, pl.program_id(ax) == 0)
                last = jnp.logical_and(last, pl.program_id(ax) == grid[ax] - 1)

            @pl.when(first)
            def _():
                for acc in accs:
                    acc[...] = jnp.zeros(acc.shape, F32)

            for a, b, k in pairs:
                accs[k][...] += prod(a, b)

            @pl.when(last)
            def _():
                epilogue([acc[...] for acc in accs], extra, outs)

    scratch = [pltpu.VMEM(s, F32) for s in acc_shapes] if axes else []
    return pl.pallas_call(
        body, name=name, grid=grid, in_specs=specs, out_specs=out_specs, out_shape=out_shapes,
        scratch_shapes=scratch, compiler_params=_params())(*operands)


def _sigmoid(x):
    return 1.0 / (1.0 + jnp.exp(-x))


_GELU_C = 0.7978845608028654
_GELU_A = 0.044715


def _gelu(x):
    return 0.5 * x * (1.0 + jnp.tanh(_GELU_C * (x + _GELU_A * x * x * x)))


def _gelu_grad(x):
    t = jnp.tanh(_GELU_C * (x + _GELU_A * x * x * x))
    return 0.5 * (1.0 + t) + 0.5 * x * (1.0 - t * t) * _GELU_C * (1.0 + 3.0 * _GELU_A * x * x)


def _rms_fwd(name, x, g, after):
    S, D = x.shape
    tr = _tile(S, 256)

    def body(x_ref, g_ref, after_ref, o_ref):
        xv = x_ref[...]
        r = lax.rsqrt(jnp.mean(xv * xv, axis=-1, keepdims=True) + EPS)
        o_ref[...] = (xv * r * g_ref[...]).astype(MXU)

    return pl.pallas_call(
        body, name=name, grid=(S // tr,),
        in_specs=[pl.BlockSpec((tr, D), lambda i: (i, 0)), pl.BlockSpec((1, D), lambda i: (0, 0)), ANY],
        out_specs=pl.BlockSpec((tr, D), lambda i: (i, 0)),
        out_shape=jax.ShapeDtypeStruct((S, D), MXU), compiler_params=_params())(x, g, after)


def _rms_bwd(name, dh, x, g, dres, after):
    S, D = x.shape
    tr = _tile(S, 256)

    def body(dh_ref, x_ref, g_ref, dres_ref, after_ref, dx_ref, dxb_ref, dg_ref):
        xv = x_ref[...]
        r = lax.rsqrt(jnp.mean(xv * xv, axis=-1, keepdims=True) + EPS)
        xh = xv * r
        dhv = dh_ref[...]
        dy = dhv * g_ref[...]
        dx = dres_ref[...] + r * (dy - xh * jnp.mean(dy * xh, axis=-1, keepdims=True))
        dx_ref[...] = dx
        dxb_ref[...] = dx.astype(MXU)

        @pl.when(pl.program_id(0) == 0)
        def _():
            dg_ref[...] = jnp.zeros(dg_ref.shape, F32)

        dg_ref[...] += jnp.sum(dhv * xh, axis=0, keepdims=True)

    row = pl.BlockSpec((tr, D), lambda i: (i, 0))
    vec = pl.BlockSpec((1, D), lambda i: (0, 0))
    return pl.pallas_call(
        body, name=name, grid=(S // tr,), in_specs=[row, row, vec, row, ANY], out_specs=[row, row, vec],
        out_shape=[jax.ShapeDtypeStruct((S, D), F32), jax.ShapeDtypeStruct((S, D), MXU),
                   jax.ShapeDtypeStruct((1, D), F32)],
        compiler_params=_params())(dh, x, g, dres, after)


def _join_columns(name, parts):
    S = parts[0].shape[0]
    tr = _tile(S, 256)
    widths = [p.shape[1] for p in parts]
    assert all(w % BLOCK == 0 for w in widths)

    def body(*refs):
        o_ref, off = refs[-1], 0
        for ref, w in zip(refs[:-1], widths):
            o_ref[:, off:off + w] = ref[...]
            off += w

    return pl.pallas_call(
        body, name=name, grid=(S // tr,), in_specs=[pl.BlockSpec((tr, w), lambda i: (i, 0)) for w in widths],
        out_specs=pl.BlockSpec((tr, sum(widths)), lambda i: (i, 0)),
        out_shape=jax.ShapeDtypeStruct((S, sum(widths)), parts[0].dtype), compiler_params=_params())(*parts)


def _loss_head(y, target):
    S, D = y.shape
    tr = _tile(S, 256)

    def body(y_ref, t_ref, loss_ref, dy_ref, dyb_ref):
        d = y_ref[...] - t_ref[...]
        dy = d * (1.0 / D)
        dy_ref[...] = dy
        dyb_ref[...] = dy.astype(MXU)

        @pl.when(pl.program_id(0) == 0)
        def _():
            loss_ref[...] = jnp.zeros(loss_ref.shape, F32)

        loss_ref[...] += (0.5 / D) * jnp.sum(jnp.sum(d * d, axis=-1, keepdims=True), axis=0, keepdims=True)

    row = pl.BlockSpec((tr, D), lambda i: (i, 0))
    return pl.pallas_call(
        body, name="loss_head", grid=(S // tr,), in_specs=[row, row],
        out_specs=[pl.BlockSpec((1, 1), lambda i: (0, 0)), row, row],
        out_shape=[jax.ShapeDtypeStruct((1, 1), F32), jax.ShapeDtypeStruct((S, D), F32),
                   jax.ShapeDtypeStruct((S, D), MXU)],
        compiler_params=_params())(y, target)


def _head_sum(v):
    r = lax.broadcasted_iota(jnp.int32, (BLOCK, BLOCK), 0) // HEAD_DIM
    c = lax.broadcasted_iota(jnp.int32, (BLOCK, BLOCK), 1) // HEAD_DIM
    ones = jnp.where(r == c, 1.0, 0.0).astype(jnp.bfloat16)
    hi = v.astype(jnp.bfloat16)
    lo = (v - hi.astype(F32)).astype(jnp.bfloat16)
    parts = []
    for t in range(v.shape[1] // BLOCK):
        sl = slice(t * BLOCK, (t + 1) * BLOCK)
        parts.append(jnp.dot(hi[:, sl], ones, preferred_element_type=F32)
                     + jnp.dot(lo[:, sl], ones, preferred_element_type=F32))
    return parts[0] if len(parts) == 1 else jnp.concatenate(parts, axis=-1)


def _swap_halves(v):
    w = v.shape[1]
    half = HEAD_DIM // 2
    lane = lax.broadcasted_iota(jnp.int32, v.shape, 1) % HEAD_DIM
    return jnp.where(lane < half, pltpu.roll(v, w - half, 1), pltpu.roll(v, half, 1))


def _norm_rope(xv, gain, cos, sin):
    r = lax.rsqrt(_head_sum(xv * xv) * (1.0 / HEAD_DIM) + EPS)
    xn = xv * r * gain
    return xn * cos + _swap_halves(xn) * sin


def _norm_rope_bwd(dy, xv, gain, cos, sin):
    r = lax.rsqrt(_head_sum(xv * xv) * (1.0 / HEAD_DIM) + EPS)
    xh = xv * r
    dxn = dy * cos + _swap_halves(dy * sin)
    dgain = jnp.sum(dxn * xh, axis=0, keepdims=True)
    dxh = dxn * gain
    dx = r * (dxh - xh * (_head_sum(dxh * xh) * (1.0 / HEAD_DIM)))
    return dx, dgain


def _fold_heads(v):
    acc = v[:, 0:BLOCK]
    for t in range(1, v.shape[1] // BLOCK):
        acc = acc + v[:, t * BLOCK:(t + 1) * BLOCK]
    return acc + pltpu.roll(acc, HEAD_DIM, 1)


def _tile_lanes(v, width):
    return v if width == BLOCK else jnp.tile(v, (1, width // BLOCK))


def _low_half(rows):
    assert BLOCK == 2 * HEAD_DIM
    return lax.broadcasted_iota(jnp.int32, (rows, BLOCK), 1) < HEAD_DIM


def _spread_heads(v):
    low = _low_half(v.shape[0])
    out = []
    for t in range(v.shape[1] // BLOCK):
        tile = v[:, t * BLOCK:(t + 1) * BLOCK]
        swapped = pltpu.roll(tile, HEAD_DIM, 1)
        out += [jnp.where(low, tile, swapped), jnp.where(low, swapped, tile)]
    return jnp.concatenate(out, axis=-1)


def _gather_heads(v):
    low = _low_half(v.shape[0])
    out = []
    for t in range(v.shape[1] // (2 * BLOCK)):
        a, b = v[:, 2 * t * BLOCK:(2 * t + 1) * BLOCK], v[:, (2 * t + 1) * BLOCK:(2 * t + 2) * BLOCK]
        out.append(jnp.where(low, a + pltpu.roll(a, HEAD_DIM, 1), b + pltpu.roll(b, HEAD_DIM, 1)))
    return out[0] if len(out) == 1 else jnp.concatenate(out, axis=-1)


def _qk_prep(proj, qg, kg, cos, sin, AW, KW):
    S = proj.shape[0]
    tr = _tile(S, 256)
    scale = HEAD_DIM ** -0.5

    def body(q_ref, k_ref, v_ref, qg_ref, kg_ref, cos_ref, sin_ref, qo_ref, ko_ref, vo_ref):
        c, s = cos_ref[...], sin_ref[...]
        q = _norm_rope(q_ref[...], _tile_lanes(qg_ref[...], AW), _tile_lanes(c, AW), _tile_lanes(s, AW))
        k = _norm_rope(k_ref[...], _tile_lanes(kg_ref[...], KW), _tile_lanes(c, KW), _tile_lanes(s, KW))
        qo_ref[...] = (q * scale).astype(MXU)
        ko_ref[...] = _spread_heads(k).astype(MXU)
        vo_ref[...] = _spread_heads(v_ref[...]).astype(MXU)

    assert AW % KW == 0
    vec = pl.BlockSpec((1, BLOCK), lambda i: (0, 0))
    tab = pl.BlockSpec((tr, BLOCK), lambda i: (i, 0))
    wide = pl.BlockSpec((tr, 2 * KW), lambda i: (i, 0))
    return pl.pallas_call(
        body, name="qk_prep", grid=(S // tr,),
        in_specs=[pl.BlockSpec((tr, AW), lambda i: (i, 0)),
                  pl.BlockSpec((tr, KW), lambda i: (i, AW // KW)),
                  pl.BlockSpec((tr, KW), lambda i: (i, AW // KW + 1)), vec, vec, tab, tab],
        out_specs=[pl.BlockSpec((tr, AW), lambda i: (i, 0)), wide, wide],
        out_shape=[jax.ShapeDtypeStruct((S, AW), MXU), jax.ShapeDtypeStruct((S, 2 * KW), MXU),
                   jax.ShapeDtypeStruct((S, 2 * KW), MXU)],
        compiler_params=_params())(proj, proj, proj, qg, kg, cos, sin)


def _stack_heads(x, h0, nh):
    low = _low_half(BLOCK)
    parts = []
    for h in range(h0, h0 + nh):
        tile = x[:, (h // 2) * BLOCK:(h // 2 + 1) * BLOCK]
        parts.append(jnp.where(low if h % 2 == 0 else jnp.logical_not(low), tile, jnp.zeros_like(tile)))
    return jnp.concatenate(parts, axis=0)


def _unstack_heads(y):
    low = _low_half(BLOCK)
    tiles = [jnp.where(low, y[2 * t * BLOCK:(2 * t + 1) * BLOCK], y[(2 * t + 1) * BLOCK:(2 * t + 2) * BLOCK])
             for t in range(y.shape[0] // (2 * BLOCK))]
    return tiles[0] if len(tiles) == 1 else jnp.concatenate(tiles, axis=-1)


def _band_t(n):
    key = lax.broadcasted_iota(jnp.int32, (2 * BLOCK, BLOCK), 0)
    qry = lax.broadcasted_iota(jnp.int32, (2 * BLOCK, BLOCK), 1)
    return (key > qry) & (key <= qry + BLOCK) & ((key >= BLOCK) | (n > 0))


def _attn_probs_t(ok, qs, kcat, h0, nh, sink_ref):
    st = jnp.where(ok, lax.dot_general(kcat, qs, NT, preferred_element_type=F32), -1e30)
    sk = jnp.concatenate([jnp.full((1, BLOCK), sink_ref[0, h], F32) for h in range(h0, h0 + nh)], axis=1)
    m = jnp.maximum(jnp.max(st, axis=0, keepdims=True), sk)
    e = jnp.exp(st - m)
    es = jnp.exp(sk - m)
    rz = 1.0 / (jnp.sum(e, axis=0, keepdims=True) + es)
    return e * rz, es * rz, rz


def _attn_fwd(qr, kr, vb, sinks):
    S, AW = qr.shape
    KW = kr.shape[1]
    nb = S // BLOCK
    nkv = KW // BLOCK
    qpk = AW // (nkv * HEAD_DIM)
    nh = min(ATTN_STACK, qpk)
    assert nh % 2 == 0 and qpk % nh == 0

    def body(sink_ref, q_ref, kp_ref, kc_ref, vp_ref, vc_ref, o_ref):
        n = pl.program_id(0)
        q, kp, kc, vp, vc = q_ref[...], kp_ref[...], kc_ref[...], vp_ref[...], vc_ref[...]
        ok = jnp.concatenate([_band_t(n)] * nh, axis=1)
        outs = []
        for g in range(nkv):
            kcat = jnp.concatenate([kp[:, g * BLOCK:(g + 1) * BLOCK], kc[:, g * BLOCK:(g + 1) * BLOCK]], axis=0)
            vcat = jnp.concatenate([vp[:, g * BLOCK:(g + 1) * BLOCK], vc[:, g * BLOCK:(g + 1) * BLOCK]], axis=0)
            for h0 in range(g * qpk, (g + 1) * qpk, nh):
                pt, _, _ = _attn_probs_t(ok, _stack_heads(q, h0, nh), kcat, h0, nh, sink_ref)
                outs.append(_unstack_heads(lax.dot_general(pt.astype(MXU), vcat, TN, preferred_element_type=F32)))
        o_ref[...] = jnp.concatenate(outs, axis=-1).astype(MXU)

    cur = lambda n: (n, 0)
    prev = lambda n: (jnp.maximum(n - 1, 0), 0)
    return pl.pallas_call(
        body, name="attn_fwd", grid=(nb,),
        in_specs=[pl.BlockSpec(memory_space=pltpu.SMEM), pl.BlockSpec((BLOCK, AW), cur),
                  pl.BlockSpec((BLOCK, KW), prev), pl.BlockSpec((BLOCK, KW), cur),
                  pl.BlockSpec((BLOCK, KW), prev), pl.BlockSpec((BLOCK, KW), cur)],
        out_specs=pl.BlockSpec((BLOCK, AW), cur),
        out_shape=jax.ShapeDtypeStruct((S, AW), MXU), compiler_params=_params())(sinks, qr, kr, kr, vb, vb)


def _attn_bwd(qr, kr, vb, sinks, dattn):
    S, AW = qr.shape
    KW = kr.shape[1]
    nb = S // BLOCK
    nkv = KW // BLOCK
    qpk = AW // (nkv * HEAD_DIM)
    nh = min(ATTN_STACK, qpk)
    scale = HEAD_DIM ** -0.5

    def body(sink_ref, q_ref, kp_ref, kc_ref, vp_ref, vc_ref, do_ref,
             dq_ref, dkp_ref, dkc_ref, dvp_ref, dvc_ref, dsink_ref):
        n = pl.program_id(0)
        q, kp, kc, vp, vc = q_ref[...], kp_ref[...], kc_ref[...], vp_ref[...], vc_ref[...]
        do = do_ref[...].astype(MXU)
        lane = lax.broadcasted_iota(jnp.int32, (1, BLOCK), 1)
        ok = jnp.concatenate([_band_t(n)] * nh, axis=1)
        dsink = jnp.zeros((1, BLOCK), F32)
        dqs, dkps, dkcs, dvps, dvcs = [], [], [], [], []
        for g in range(nkv):
            kcat = jnp.concatenate([kp[:, g * BLOCK:(g + 1) * BLOCK], kc[:, g * BLOCK:(g + 1) * BLOCK]], axis=0)
            vcat = jnp.concatenate([vp[:, g * BLOCK:(g + 1) * BLOCK], vc[:, g * BLOCK:(g + 1) * BLOCK]], axis=0)
            dk, dv = None, None
            for h0 in range(g * qpk, (g + 1) * qpk, nh):
                qs = _stack_heads(q, h0, nh)
                dos = _stack_heads(do, h0, nh)
                pt, ps, _ = _attn_probs_t(ok, qs, kcat, h0, nh, sink_ref)
                dpt = lax.dot_general(vcat, dos, NT, preferred_element_type=F32)
                delta = jnp.sum(pt * dpt, axis=0, keepdims=True)
                dst = (pt * (dpt - delta)).astype(MXU)
                dsk = -ps * delta
                dv_part = jnp.dot(pt.astype(MXU), dos, preferred_element_type=F32)
                dk_part = jnp.dot(dst, qs, preferred_element_type=F32)
                dqs.append(_unstack_heads(lax.dot_general(dst, kcat, TN, preferred_element_type=F32) * scale))
                dk = dk_part if dk is None else dk + dk_part
                dv = dv_part if dv is None else dv + dv_part
                for j in range(nh):
                    tot = jnp.sum(dsk[:, j * BLOCK:(j + 1) * BLOCK], axis=1, keepdims=True)
                    dsink = dsink + jnp.where(lane == h0 + j, tot, 0.0)
            dkps.append(dk[:BLOCK])
            dkcs.append(dk[BLOCK:])
            dvps.append(dv[:BLOCK])
            dvcs.append(dv[BLOCK:])
        dq_ref[...] = jnp.concatenate(dqs, axis=-1)
        dkp_ref[...] = jnp.concatenate(dkps, axis=-1)
        dkc_ref[...] = jnp.concatenate(dkcs, axis=-1)
        dvp_ref[...] = jnp.concatenate(dvps, axis=-1)
        dvc_ref[...] = jnp.concatenate(dvcs, axis=-1)

        @pl.when(n == 0)
        def _():
            dsink_ref[...] = jnp.zeros(dsink_ref.shape, F32)

        dsink_ref[...] += dsink

    cur = lambda n: (n, 0)
    prev = lambda n: (jnp.maximum(n - 1, 0), 0)
    kv = jax.ShapeDtypeStruct((S, KW), F32)
    kvspec = pl.BlockSpec((BLOCK, KW), cur)
    return pl.pallas_call(
        body, name="attn_bwd", grid=(nb,),
        in_specs=[pl.BlockSpec(memory_space=pltpu.SMEM), pl.BlockSpec((BLOCK, AW), cur),
                  pl.BlockSpec((BLOCK, KW), prev), kvspec, pl.BlockSpec((BLOCK, KW), prev), kvspec,
                  pl.BlockSpec((BLOCK, AW), cur)],
        out_specs=[pl.BlockSpec((BLOCK, AW), cur), kvspec, kvspec, kvspec, kvspec,
                   pl.BlockSpec((1, BLOCK), lambda n: (0, 0))],
        out_shape=[jax.ShapeDtypeStruct((S, AW), F32), kv, kv, kv, kv, jax.ShapeDtypeStruct((1, BLOCK), F32)],
        compiler_params=_params())(sinks, qr, kr, kr, vb, vb, dattn)


def _qk_prep_bwd(proj, qg, kg, cos, sin, dq, dkp, dkc, dvp, dvc, AW, KW):
    S = proj.shape[0]
    nb = S // BLOCK

    def body(q_ref, k_ref, qg_ref, kg_ref, cos_ref, sin_ref, dq_ref, dkp_ref, dkc_ref, dvp_ref, dvc_ref,
             o_ref, dqg_ref, dkg_ref):
        n = pl.program_id(0)
        c, s = cos_ref[...], sin_ref[...]
        has_next = jnp.where(n < nb - 1, 1.0, 0.0)
        dk = _gather_heads(dkc_ref[...] + has_next * dkp_ref[...])
        dv = _gather_heads(dvc_ref[...] + has_next * dvp_ref[...])
        dxq, dqg = _norm_rope_bwd(dq_ref[...], q_ref[...], _tile_lanes(qg_ref[...], AW),
                                  _tile_lanes(c, AW), _tile_lanes(s, AW))
        dxk, dkg = _norm_rope_bwd(dk, k_ref[...], _tile_lanes(kg_ref[...], KW),
                                  _tile_lanes(c, KW), _tile_lanes(s, KW))
        o_ref[...] = jnp.concatenate([dxq, dxk, dv], axis=-1).astype(MXU)

        @pl.when(n == 0)
        def _():
            dqg_ref[...] = jnp.zeros(dqg_ref.shape, F32)
            dkg_ref[...] = jnp.zeros(dkg_ref.shape, F32)

        dqg_ref[...] += _fold_heads(dqg)
        dkg_ref[...] += _fold_heads(dkg)

    cur = lambda n: (n, 0)
    nxt = lambda n: (jnp.minimum(n + 1, nb - 1), 0)
    vec = pl.BlockSpec((1, BLOCK), lambda n: (0, 0))
    tab = pl.BlockSpec((BLOCK, BLOCK), cur)
    return pl.pallas_call(
        body, name="qk_prep_bwd", grid=(nb,),
        in_specs=[pl.BlockSpec((BLOCK, AW), cur), pl.BlockSpec((BLOCK, KW), lambda n: (n, AW // KW)),
                  vec, vec, tab, tab, pl.BlockSpec((BLOCK, AW), cur),
                  pl.BlockSpec((BLOCK, 2 * KW), nxt), pl.BlockSpec((BLOCK, 2 * KW), cur),
                  pl.BlockSpec((BLOCK, 2 * KW), nxt), pl.BlockSpec((BLOCK, 2 * KW), cur)],
        out_specs=[pl.BlockSpec((BLOCK, AW + 2 * KW), cur), vec, vec],
        out_shape=[jax.ShapeDtypeStruct((S, AW + 2 * KW), MXU), jax.ShapeDtypeStruct((1, BLOCK), F32),
                   jax.ShapeDtypeStruct((1, BLOCK), F32)],
        compiler_params=_params())(proj, proj, qg, kg, cos, sin, dq, dkp, dkc, dvp, dvc)


SGU_LANES = 512
SGU_ROWS = 256


def _sgu_group(v, lng, lnb, w_f32, b):
    rows = v.shape[0]
    mu = jnp.mean(v, axis=-1, keepdims=True)
    vc = v - mu
    r = lax.rsqrt(jnp.mean(vc * vc, axis=-1, keepdims=True) + EPS)
    xh = vc * r
    vn = (xh * lng + lnb).astype(MXU)
    row = lax.broadcasted_iota(jnp.int32, (BLOCK, BLOCK), 0)
    col = lax.broadcasted_iota(jnp.int32, (BLOCK, BLOCK), 1)
    tri = row >= col
    w = jnp.where(tri, w_f32, 0.0).astype(MXU)
    chunks = [jnp.dot(w, vn[k * BLOCK:(k + 1) * BLOCK], preferred_element_type=F32) + b for k in range(rows // BLOCK)]
    s = chunks[0] if len(chunks) == 1 else jnp.concatenate(chunks, axis=0)
    return xh, r, vn, w, s, tri


def _sgu_layout(S, u_col):
    SW = SGU_GROUPS * BLOCK
    lb, tr = min(SGU_LANES, SW), min(SGU_ROWS, S)
    assert u_col % lb == 0 and SW % lb == 0 and S % tr == 0
    ub, nlb, gpb = u_col // lb, SW // lb, lb // BLOCK
    specs = [pl.BlockSpec((tr, lb), lambda j, i: (i, ub + j)), pl.BlockSpec((tr, lb), lambda j, i: (i, ub + nlb + j)),
             pl.BlockSpec((1, lb), lambda j, i: (0, j)), pl.BlockSpec((1, lb), lambda j, i: (0, j)),
             pl.BlockSpec((gpb, BLOCK, BLOCK), lambda j, i: (j, 0, 0)),
             pl.BlockSpec((gpb, BLOCK, 1), lambda j, i: (j, 0, 0))]
    return lb, tr, gpb, nlb, specs


def _sgu_fwd(proj, lng, lnb, ws, bs, u_col):
    S = proj.shape[0]
    lb, tr, gpb, nlb, specs = _sgu_layout(S, u_col)

    def body(pu_ref, pv_ref, lng_ref, lnb_ref, w_ref, b_ref, o_ref):
        u = _gelu(pu_ref[...])
        v = _gelu(pv_ref[...])
        outs = []
        for g in range(gpb):
            sl = slice(g * BLOCK, (g + 1) * BLOCK)
            s = _sgu_group(v[:, sl], lng_ref[:, sl], lnb_ref[:, sl], w_ref[g], b_ref[g])[4]
            outs.append(u[:, sl] * s)
        o_ref[...] = (outs[0] if gpb == 1 else jnp.concatenate(outs, axis=-1)).astype(MXU)

    return pl.pallas_call(
        body, name="sgu_fwd", grid=(nlb, S // tr), in_specs=specs,
        out_specs=pl.BlockSpec((tr, lb), lambda j, i: (i, j)),
        out_shape=jax.ShapeDtypeStruct((S, nlb * lb), MXU), compiler_params=_params())(proj, proj, lng, lnb, ws, bs)


def _sgu_bwd(proj, lng, lnb, ws, bs, dsgu, u_col, after):
    S = proj.shape[0]
    G = SGU_GROUPS
    lb, tr, gpb, nlb, specs = _sgu_layout(S, u_col)
    nch = tr // BLOCK

    def body(pu_ref, pv_ref, lng_ref, lnb_ref, w_ref, b_ref, do_ref, after_ref,
             dpu_ref, dpv_ref, dw_ref, db_ref, dlng_ref, dlnb_ref):
        pu, pv, do = pu_ref[...], pv_ref[...], do_ref[...]
        u = _gelu(pu)
        v = _gelu(pv)

        @pl.when(pl.program_id(1) == 0)
        def _():
            dw_ref[...] = jnp.zeros(dw_ref.shape, F32)
            db_ref[...] = jnp.zeros(db_ref.shape, F32)
            dlng_ref[...] = jnp.zeros(dlng_ref.shape, F32)
            dlnb_ref[...] = jnp.zeros(dlnb_ref.shape, F32)

        ss, dvs, dlng, dlnb = [], [], [], []
        for g in range(gpb):
            sl = slice(g * BLOCK, (g + 1) * BLOCK)
            xh, r, vn, w, s, tri = _sgu_group(v[:, sl], lng_ref[:, sl], lnb_ref[:, sl], w_ref[g], b_ref[g])
            ds = do[:, sl] * u[:, sl]
            dsb = ds.astype(MXU)
            dw, db, dvn = None, None, []
            for k in range(nch):
                rows = slice(k * BLOCK, (k + 1) * BLOCK)
                part = lax.dot_general(dsb[rows], vn[rows], NT, preferred_element_type=F32)
                dw = part if dw is None else dw + part
                rowsum = jnp.sum(ds[rows], axis=-1, keepdims=True)
                db = rowsum if db is None else db + rowsum
                dvn.append(lax.dot_general(w, dsb[rows], TN, preferred_element_type=F32))
            dvn = dvn[0] if nch == 1 else jnp.concatenate(dvn, axis=0)
            dw_ref[g] += jnp.where(tri, dw, 0.0)
            db_ref[g] += db
            dxh = dvn * lng_ref[:, sl]
            dvs.append(r * (dxh - jnp.mean(dxh, axis=-1, keepdims=True)
                            - xh * jnp.mean(dxh * xh, axis=-1, keepdims=True)))
            dlng.append(jnp.sum(dvn * xh, axis=0, keepdims=True))
            dlnb.append(jnp.sum(dvn, axis=0, keepdims=True))
            ss.append(s)
        cat = lambda parts: parts[0] if gpb == 1 else jnp.concatenate(parts, axis=-1)
        dpu_ref[...] = (do * cat(ss) * _gelu_grad(pu)).astype(MXU)
        dpv_ref[...] = (cat(dvs) * _gelu_grad(pv)).astype(MXU)
        dlng_ref[...] += cat(dlng)
        dlnb_ref[...] += cat(dlnb)

    tile = pl.BlockSpec((tr, lb), lambda j, i: (i, j))
    vec = pl.BlockSpec((1, lb), lambda j, i: (0, j))
    half = jax.ShapeDtypeStruct((S, G * BLOCK), MXU)
    return pl.pallas_call(
        body, name="sgu_bwd", grid=(nlb, S // tr), in_specs=specs + [tile, ANY],
        out_specs=[tile, tile, pl.BlockSpec((gpb, BLOCK, BLOCK), lambda j, i: (j, 0, 0)),
                   pl.BlockSpec((gpb, BLOCK, 1), lambda j, i: (j, 0, 0)), vec, vec],
        out_shape=[half, half, jax.ShapeDtypeStruct((G, BLOCK, BLOCK), F32),
                   jax.ShapeDtypeStruct((G, BLOCK, 1), F32),
                   jax.ShapeDtypeStruct((1, G * BLOCK), F32), jax.ShapeDtypeStruct((1, G * BLOCK), F32)],
        compiler_params=_params())(proj, proj, lng, lnb, ws, bs, dsgu, after)


def _store_f32(vals, extra, outs):
    for v, o in zip(vals, outs):
        o[...] = v


def _store_mxu(vals, extra, outs):
    for v, o in zip(vals, outs):
        o[...] = v.astype(MXU)


def _proj_in(h, w):
    S, D = h.shape
    Ns = w.shape[2]
    tm, tn = _tile(S, 1024), _tile(Ns, 1024)
    npb = Ns // tn
    return _mm("proj_in", (S // tm, N_CHIPS, npb), 0, [h, w],
               [pl.BlockSpec((tm, D), lambda i, s, j: (i, 0)), pl.BlockSpec((None, D, tn), lambda i, s, j: (s, 0, j))],
               [(0, 1, 0)], NN, 0, [jax.ShapeDtypeStruct((S, N_CHIPS * Ns), F32)],
               [pl.BlockSpec((tm, tn), lambda i, s, j: (i, s * npb + j))], [None], _store_f32)[0]


def _branches(attn, sgu, wa, ws, proj, gate0):
    S, AW = attn.shape
    SW = sgu.shape[1]
    Nb = wa.shape[2]
    D = N_CHIPS * Nb
    tm = _tile(S, 512)
    assert gate0 % Nb == 0
    ga, gb = gate0 // Nb, (gate0 + D) // Nb

    def epilogue(vals, extra, outs):
        a, b = vals
        outs[0][...] = (_sigmoid(extra[0][...]) * a + _sigmoid(extra[1][...]) * b).astype(MXU)
        outs[1][...] = a
        outs[2][...] = b

    tile = pl.BlockSpec((tm, Nb), lambda i, s: (i, s))
    wspec = lambda k: pl.BlockSpec((None, k, Nb), lambda i, s: (s, 0, 0))
    f = jax.ShapeDtypeStruct((S, D), F32)
    return _mm("branches", (S // tm, N_CHIPS), 0, [attn, sgu, wa, ws, proj, proj],
               [pl.BlockSpec((tm, AW), lambda i, s: (i, 0)), pl.BlockSpec((tm, SW), lambda i, s: (i, 0)),
                wspec(AW), wspec(SW), pl.BlockSpec((tm, Nb), lambda i, s: (i, ga + s)),
                pl.BlockSpec((tm, Nb), lambda i, s: (i, gb + s))],
               [(0, 2, 0), (1, 3, 1)], NN, 2, [jax.ShapeDtypeStruct((S, D), MXU), f, f], [tile] * 3,
               [None, None], epilogue, chunk=MXU_CHUNK)


def _rows_mm(name, a, w, res):
    S = a.shape[0]
    _, K, N = w.shape
    tm, tn = _tile(S, 1024), _tile(N, 1024)

    def epilogue(vals, extra, outs):
        outs[0][...] = extra[0][...] + vals[0]

    out = pl.BlockSpec((tm, tn), lambda i, j, s: (i, j))
    return _mm(name, (S // tm, N // tn, N_CHIPS), 1, [a, w, res],
               [pl.BlockSpec((tm, K), lambda i, j, s: (i, s)), pl.BlockSpec((None, K, tn), lambda i, j, s: (s, 0, j)), out],
               [(0, 1, 0)], NN, 1, [jax.ShapeDtypeStruct((S, N), F32)], [out], [(tm, tn)], epilogue)[0]


def _gate(h2, wg):
    S, D = h2.shape
    Nf = wg.shape[2]
    tm = _tile(S, 256)
    return _mm("gate", (N_CHIPS, S // tm), 0, [h2, wg],
               [pl.BlockSpec((tm, D), lambda s, i: (i, 0)), pl.BlockSpec((None, D, Nf), lambda s, i: (s, 0, 0))],
               [(0, 1, 0)], NN, 0, [jax.ShapeDtypeStruct((S, N_CHIPS * Nf), F32)],
               [pl.BlockSpec((tm, Nf), lambda s, i: (i, s))], [None], _store_f32, chunk=MXU_CHUNK)[0]


def _up_act(h2, wu, g):
    S, D = h2.shape
    Nf = wu.shape[2]
    tm = _tile(S, 256)

    def epilogue(vals, extra, outs):
        u, gv = vals[0], extra[0][...]
        outs[0][...] = u
        outs[1][...] = (gv * _sigmoid(gv) * u).astype(MXU)

    o = pl.BlockSpec((tm, Nf), lambda s, i: (i, s))
    return _mm("up_act", (N_CHIPS, S // tm), 0, [h2, wu, g],
               [pl.BlockSpec((tm, D), lambda s, i: (i, 0)), pl.BlockSpec((None, D, Nf), lambda s, i: (s, 0, 0)), o],
               [(0, 1, 0)], NN, 1, [jax.ShapeDtypeStruct((S, N_CHIPS * Nf), F32),
                                    jax.ShapeDtypeStruct((S, N_CHIPS * Nf), MXU)], [o, o], [None], epilogue,
               chunk=MXU_CHUNK)


def _down_bwd(dyb, wd, g, u):
    S, D = dyb.shape
    Kf = wd.shape[1]
    tm = _tile(S, 512)

    def epilogue(vals, extra, outs):
        da, gv, uv = vals[0], extra[0][...], extra[1][...]
        sg = _sigmoid(gv)
        outs[0][...] = (da * uv * sg * (1.0 + gv * (1.0 - sg))).astype(MXU)
        outs[1][...] = (da * gv * sg).astype(MXU)

    t = pl.BlockSpec((tm, Kf), lambda i, s: (i, s))
    o = jax.ShapeDtypeStruct((S, N_CHIPS * Kf), MXU)
    return _mm("down_bwd", (S // tm, N_CHIPS), 0, [dyb, wd, g, u],
               [pl.BlockSpec((tm, D), lambda i, s: (i, 0)), pl.BlockSpec((None, Kf, D), lambda i, s: (s, 0, 0)), t, t],
               [(0, 1, 0)], NT, 2, [o, o], [t, t], [None], epilogue, chunk=MXU_CHUNK)


def _out_bwd(dxb, wo, proj, ba, bb, gate0):
    S, D = dxb.shape
    Ko = wo.shape[1]
    tm = _tile(S, 512)
    assert gate0 % Ko == 0
    ga, gb = gate0 // Ko, (gate0 + D) // Ko

    def epilogue(vals, extra, outs):
        dm = vals[0]
        sa, sb = _sigmoid(extra[0][...]), _sigmoid(extra[1][...])
        outs[0][...] = (dm * sa).astype(MXU)
        outs[1][...] = (dm * sb).astype(MXU)
        outs[2][...] = (dm * extra[2][...] * sa * (1.0 - sa)).astype(MXU)
        outs[3][...] = (dm * extra[3][...] * sb * (1.0 - sb)).astype(MXU)

    t = pl.BlockSpec((tm, Ko), lambda i, s: (i, s))
    o = jax.ShapeDtypeStruct((S, D), MXU)
    return _mm("out_bwd", (S // tm, N_CHIPS), 0, [dxb, wo, proj, proj, ba, bb],
               [pl.BlockSpec((tm, D), lambda i, s: (i, 0)), pl.BlockSpec((None, Ko, D), lambda i, s: (s, 0, 0)),
                pl.BlockSpec((tm, Ko), lambda i, s: (i, ga + s)), pl.BlockSpec((tm, Ko), lambda i, s: (i, gb + s)), t, t],
               [(0, 1, 0)], NT, 4, [o] * 4, [t] * 4, [None], epilogue, chunk=MXU_CHUNK)


def _dx_cols(name, terms, n_out, after=None):
    S = terms[0][0].shape[0]
    _, K, Ns = terms[0][1].shape
    tm, tko, tn = _tile(S, 1024), _tile(K, 1024), _tile(Ns, 1920 if len(terms) == 1 else 1408)
    npb = Ns // tn
    operands, specs, pairs = [], [], []
    for t, (dy, w, k) in enumerate(terms):
        assert w.shape == (N_CHIPS, K, Ns)
        operands += [dy, w]
        specs += [pl.BlockSpec((tm, tn), lambda i, jk, s, jn: (i, s * npb + jn)),
                  pl.BlockSpec((None, tko, tn), lambda i, jk, s, jn: (s, jk, jn))]
        pairs.append((2 * t, 2 * t + 1, k))
    out = pl.BlockSpec((tm, tko), lambda i, jk, s, jn: (i, jk))
    return _mm(name, (S // tm, K // tko, N_CHIPS, npb), 2, operands, specs, pairs, NT, 0,
               [jax.ShapeDtypeStruct((S, K), F32)] * n_out, [out] * n_out, [(tm, tko)] * n_out, _store_f32, after)


def _dw_cols(name, a, dy):
    S, K = a.shape
    Ns = dy.shape[1] // N_CHIPS
    tk, tn = _tile(K, 512), _tile(Ns, 1408)
    npb = Ns // tn
    return _mm(name, (K // tk, N_CHIPS, npb), 0, [a, dy],
               [pl.BlockSpec((S, tk), lambda jk, s, jn: (0, jk)), pl.BlockSpec((S, tn), lambda jk, s, jn: (0, s * npb + jn))],
               [(0, 1, 0)], TN, 0, [jax.ShapeDtypeStruct((N_CHIPS, K, Ns), MXU)],
               [pl.BlockSpec((None, tk, tn), lambda jk, s, jn: (s, jk, jn))], [None], _store_mxu)[0]


def _dw_rows(name, a, dy):
    S = a.shape[0]
    K = a.shape[1] // N_CHIPS
    N = dy.shape[1]
    tk, tn = _tile(K, 1408), _tile(N, 1024)
    nkb = K // tk
    return _mm(name, (N_CHIPS, nkb, N // tn), 0, [a, dy],
               [pl.BlockSpec((S, tk), lambda s, jk, jn: (0, s * nkb + jk)), pl.BlockSpec((S, tn), lambda s, jk, jn: (0, jn))],
               [(0, 1, 0)], TN, 0, [jax.ShapeDtypeStruct((N_CHIPS, K, N), MXU)],
               [pl.BlockSpec((None, tk, tn), lambda s, jk, jn: (s, jk, jn))], [None], _store_mxu)[0]


def _layer_fwd(x, stream, layer, after, sp, cos, sin, dims):
    AW, KW, gate0, u_col = dims
    h = _rms_fwd("mix_norm", x, sp["mix_norm"], after)
    stream.forward(layer, 0, h)
    w = stream.finish(layer, 0, h)
    proj = _proj_in(h, w["w_in"])
    qr, kr, vb = _qk_prep(proj, sp["q_norm"], sp["k_norm"], cos, sin, AW, KW)
    attn = _attn_fwd(qr, kr, vb, sp["sinks"])
    stream.forward(layer, 1, attn)
    sgu = _sgu_fwd(proj, sp["sgu_ln_g"], sp["sgu_ln_b"], sp["w_spatial"], sp["b_spatial"], u_col)
    w.update(stream.finish(layer, 1, sgu))
    merged, ba, bb = _branches(attn, sgu, w["w_attn_branch"], w["w_sgu_branch"], proj, gate0)
    x1 = _rows_mm("out_proj", merged, w["w_out"], x)
    h2 = _rms_fwd("ffn_norm", x1, sp["ffn_norm"], x1)
    stream.forward(layer, 2, h2)
    w.update(stream.finish(layer, 2, h2))
    g = _gate(h2, w["w_gate"])
    stream.forward(layer, 3, g)
    w.update(stream.finish(layer, 3, g))
    u, act = _up_act(h2, w["w_up"], g)
    stream.forward(layer, 4, u)
    w.update(stream.finish(layer, 4, u))
    x2 = _rows_mm("down_proj", act, w["w_down"], x1)
    saved = dict(x=x, h=h, proj=proj, qr=qr, kr=kr, vb=vb, attn=attn, sgu=sgu, merged=merged, ba=ba, bb=bb,
                 x1=x1, h2=h2, g=g, u=u, act=act)
    return x2, saved, w


def _layer_bwd(dy, dyb, w, sp, sv, cos, sin, dims, reducer, layer):
    AW, KW, gate0, u_col = dims
    big, small = {}, {}
    dg, du = _down_bwd(dyb, w["w_down"], sv["g"], sv["u"])
    big["w_down"] = _dw_rows("dw_down", sv["act"], dyb)
    big["w_gate"] = _dw_cols("dw_gate", sv["h2"], dg)
    big["w_up"] = _dw_cols("dw_up", sv["h2"], du)
    token = reducer.start(layer, 2, big)
    dh2 = _dx_cols("dh2", [(dg, w["w_gate"], 0), (du, w["w_up"], 0)], 1, token)[0]
    token = reducer.scatter(layer, 2, dh2)
    dx1, dx1b, small["ffn_norm"] = _rms_bwd("ffn_norm_bwd", dh2, sv["x1"], sp["ffn_norm"], dy, token)
    dba, dbb, dgla, dglb = _out_bwd(dx1b, w["w_out"], sv["proj"], sv["ba"], sv["bb"], gate0)
    big["w_out"] = _dw_rows("dw_out", sv["merged"], dx1b)
    big["w_attn_branch"] = _dw_cols("dw_attn_branch", sv["attn"], dba)
    big["w_sgu_branch"] = _dw_cols("dw_sgu_branch", sv["sgu"], dbb)
    token = reducer.start(layer, 1, big)
    dattn, dsgu = _dx_cols("dbranch_in", [(dba, w["w_attn_branch"], 0), (dbb, w["w_sgu_branch"], 1)], 2, token)
    token = reducer.scatter(layer, 1, dsgu)
    dpu, dpv, small["w_spatial"], db, small["sgu_ln_g"], small["sgu_ln_b"] = _sgu_bwd(
        sv["proj"], sp["sgu_ln_g"], sp["sgu_ln_b"], sp["w_spatial"], sp["b_spatial"], dsgu, u_col, token)
    small["b_spatial"] = db[:, :, 0]
    dq, dkp, dkc, dvp, dvc, dsink = _attn_bwd(sv["qr"], sv["kr"], sv["vb"], sp["sinks"], dattn)
    small["sinks"] = dsink[:, :sp["sinks"].shape[1]]
    dqkv, dqg, dkg = _qk_prep_bwd(sv["proj"], sp["q_norm"], sp["k_norm"], cos, sin, dq, dkp, dkc, dvp, dvc, AW, KW)
    small["q_norm"] = dqg[:, :HEAD_DIM]
    small["k_norm"] = dkg[:, :HEAD_DIM]
    dproj = _join_columns("dproj", [dqkv, dpu, dpv, dgla, dglb])
    big["w_in"] = _dw_cols("dw_in", sv["h"], dproj)
    token = reducer.start(layer, 0, big)
    dh = _dx_cols("dh", [(dproj, w["w_in"], 0)], 1, token)[0]
    token = reducer.scatter(layer, 0, dh)
    dx, dxb, small["mix_norm"] = _rms_bwd("mix_norm_bwd", dh, sv["x"], sp["mix_norm"], dx1, token)
    return dx, dxb, small


def _place():
    x, y, c = lax.axis_index("x"), lax.axis_index("y"), lax.axis_index("c")
    chips = [(1 - x, y), (x, 1 - y), (1 - x, 1 - y)]
    return x, y, c, chips


def _half_rows(c, rows):
    h = rows // 2
    assert h % 16 == 0
    return pl.ds(pl.multiple_of(c * h, 16), h)


def _row_tile(rows, pref):
    best = None
    for t in range(16, min(rows, pref) + 1, 16):
        if rows % t == 0:
            best = t
    assert best is not None, rows
    return best


def _cast_own(name, chip, w, layer):
    _, R, C = w.shape
    tr = _row_tile(R, 512)

    def body(chip_ref, w_ref, o_ref):
        o_ref[...] = w_ref[...].astype(MXU)

    return pl.pallas_call(
        body, name=name, out_shape=jax.ShapeDtypeStruct((N_CHIPS, R, C), MXU),
        grid_spec=pltpu.PrefetchScalarGridSpec(
            num_scalar_prefetch=1, grid=(R // tr,),
            in_specs=[pl.BlockSpec((None, tr, C), lambda i, chip_ref: (layer, i, 0))],
            out_specs=pl.BlockSpec((None, tr, C), lambda i, chip_ref: (chip_ref[0], i, 0))),
        compiler_params=_params())(chip, w)


HBM = pl.BlockSpec(memory_space=pltpu.HBM)
SEM = pl.BlockSpec(memory_space=pltpu.SEMAPHORE)
DATAFLOW = pltpu.SideEffectType.DATAFLOW_SIDE_EFFECTING


def _gather_copies(bufs, send_sem, recv_sem):
    x, y, c, chips = _place()

    def ici(a, j, block):
        px, py = chips[j]
        blk = bufs[a].at[block, _half_rows(c, bufs[a].shape[1])]
        return pltpu.make_async_remote_copy(
            src_ref=blk, dst_ref=blk, send_sem=send_sem.at[3 * a + j], recv_sem=recv_sem.at[3 * a + j],
            device_id=(px, py, c), device_id_type=MESH)

    def d2d(a, j, core):
        px, py = chips[j]
        blk = bufs[a].at[2 * px + py, _half_rows(core, bufs[a].shape[1])]
        return pltpu.make_async_remote_copy(
            src_ref=blk, dst_ref=blk, send_sem=send_sem.at[3 * a + j], recv_sem=recv_sem.at[3 * a + j],
            device_id=(x, y, 1 - c), device_id_type=MESH)

    return ici, d2d


def _in_hbm(bufs):
    return [pltpu.with_memory_space_constraint(b, pltpu.HBM) for b in bufs]


def _gather_start(name, bufs, after):
    n = len(bufs)

    def body(*refs):
        dst = refs[n + 1:2 * n + 1]
        send_sem, recv_sem, token = refs[2 * n + 1:]
        x, y, c, chips = _place()
        ici, _ = _gather_copies(dst, send_sem, recv_sem)
        for a in range(n):
            for j in range(3):
                ici(a, j, 2 * x + y).start()
        token[...] = jnp.zeros(token.shape, token.dtype)

    sems = pltpu.SemaphoreType.DMA((3 * n,))
    outs = pl.pallas_call(
        body, name=name, in_specs=[HBM] * n + [ANY],
        out_specs=[HBM] * n + [SEM, SEM, pl.BlockSpec(memory_space=pltpu.VMEM)],
        out_shape=[pltpu.HBM(b.shape, b.dtype) for b in bufs] + [sems, sems, jax.ShapeDtypeStruct((8, BLOCK), F32)],
        input_output_aliases={a: a for a in range(n)},
        compiler_params=pltpu.CompilerParams(has_side_effects=DATAFLOW))(*_in_hbm(bufs), after)
    return outs[:n], outs[n], outs[n + 1], outs[n + 2]


def _gather_forward(name, bufs, ici_send, ici_recv, after):
    n = len(bufs)

    def body(*refs):
        ici_send_ref, ici_recv_ref = refs[n], refs[n + 1]
        dst = refs[n + 3:2 * n + 3]
        d2d_send, d2d_recv = refs[2 * n + 3:]
        x, y, c, chips = _place()
        ici, _ = _gather_copies(dst, ici_send_ref, ici_recv_ref)
        _, d2d = _gather_copies(dst, d2d_send, d2d_recv)
        for a in range(n):
            for j, (px, py) in enumerate(chips):
                ici(a, j, 2 * px + py).wait_recv()
                d2d(a, j, c).start()
        for a in range(n):
            for j in range(3):
                ici(a, j, 2 * x + y).wait_send()

    sems = pltpu.SemaphoreType.DMA((3 * n,))
    outs = pl.pallas_call(
        body, name=name, in_specs=[HBM] * n + [SEM, SEM, ANY], out_specs=[HBM] * n + [SEM, SEM],
        out_shape=[pltpu.HBM(b.shape, b.dtype) for b in bufs] + [sems, sems],
        input_output_aliases={a: a for a in range(n)},
        compiler_params=pltpu.CompilerParams(has_side_effects=DATAFLOW))(*bufs, ici_send, ici_recv, after)
    return outs[:n], outs[n], outs[n + 1]


def _gather_finish(name, bufs, d2d_send, d2d_recv, after):
    n = len(bufs)

    def body(*refs):
        send_ref, recv_ref = refs[n], refs[n + 1]
        dst = refs[n + 3:]
        x, y, c, chips = _place()
        _, d2d = _gather_copies(dst, send_ref, recv_ref)
        for a in range(n):
            for j in range(3):
                d2d(a, j, 1 - c).wait_recv()
                d2d(a, j, c).wait_send()

    return pl.pallas_call(
        body, name=name, in_specs=[HBM] * n + [SEM, SEM, ANY], out_specs=[HBM] * n,
        out_shape=[pltpu.HBM(b.shape, b.dtype) for b in bufs],
        input_output_aliases={a: a for a in range(n)},
        compiler_params=pltpu.CompilerParams(has_side_effects=DATAFLOW))(*bufs, d2d_send, d2d_recv, after)


GATHER = (("w_in",), ("w_attn_branch", "w_sgu_branch", "w_out"), ("w_gate",), ("w_up",), ("w_down",))
REDUCE = (("w_in",), ("w_attn_branch", "w_sgu_branch", "w_out"), ("w_gate", "w_up", "w_down"))


class _WeightStream:
    def __init__(self, started):
        self.started, self.passed = started, {}

    def forward(self, layer, group, after):
        bufs, send, recv = self.started[(layer, group)]
        self.passed[(layer, group)] = _gather_forward("gather_forward_%d_%d" % (layer, group), bufs, send, recv, after)

    def finish(self, layer, group, after):
        bufs, send, recv = self.passed[(layer, group)]
        done = _gather_finish("gather_finish_%d_%d" % (layer, group), bufs, send, recv, after)
        return dict(zip(GATHER[group], done))


def _pair_copies(grads, lands, send_sem, recv_sem):
    x, y, c, _ = _place()

    def make(a):
        theirs = _half_rows(1 - c, grads[a].shape[1])
        return pltpu.make_async_remote_copy(
            src_ref=grads[a].at[:, theirs], dst_ref=lands[a], send_sem=send_sem.at[a], recv_sem=recv_sem.at[a],
            device_id=(x, y, 1 - c), device_id_type=MESH)

    return make


def _pair_start(name, grads, after):
    n = len(grads)
    lands = [lax.empty((g.shape[0], g.shape[1] // 2, g.shape[2]), g.dtype) for g in grads]

    def body(*refs):
        src, dst = refs[2 * n + 1:3 * n + 1], refs[3 * n + 1:4 * n + 1]
        send_sem, recv_sem, token = refs[4 * n + 1:]
        copy = _pair_copies(src, dst, send_sem, recv_sem)
        for a in range(n):
            copy(a).start()
        token[...] = jnp.zeros(token.shape, token.dtype)

    sems = pltpu.SemaphoreType.DMA((n,))
    outs = pl.pallas_call(
        body, name=name, in_specs=[HBM] * (2 * n) + [ANY],
        out_specs=[HBM] * (2 * n) + [SEM, SEM, pl.BlockSpec(memory_space=pltpu.VMEM)],
        out_shape=[pltpu.HBM(b.shape, b.dtype) for b in grads + lands] + [sems, sems, jax.ShapeDtypeStruct((8, BLOCK), F32)],
        input_output_aliases={a: a for a in range(2 * n)},
        compiler_params=pltpu.CompilerParams(has_side_effects=DATAFLOW))(*_in_hbm(grads + lands), after)
    return outs[:n], outs[n:2 * n], outs[2 * n], outs[2 * n + 1], outs[2 * n + 2]


def _pair_finish(name, grads, lands, send_sem, recv_sem, after):
    n = len(grads)

    def body(*refs):
        send_ref, recv_ref = refs[2 * n], refs[2 * n + 1]
        src, dst = refs[2 * n + 3:3 * n + 3], refs[3 * n + 3:]
        copy = _pair_copies(src, dst, send_ref, recv_ref)
        for a in range(n):
            copy(a).wait_send()
            copy(a).wait_recv()

    outs = pl.pallas_call(
        body, name=name, in_specs=[HBM] * (2 * n) + [SEM, SEM, ANY], out_specs=[HBM] * (2 * n),
        out_shape=[pltpu.HBM(b.shape, b.dtype) for b in grads + lands],
        input_output_aliases={a: a for a in range(2 * n)},
        compiler_params=pltpu.CompilerParams(has_side_effects=DATAFLOW))(*grads, *lands, send_sem, recv_sem, after)
    return outs[:n], outs[n:]


def _pair_sum(name, core, g, p):
    _, h, C = p.shape
    tr = _row_tile(h, 512)
    nrb = h // tr

    def body(core_ref, g_ref, p_ref, o_ref):
        o_ref[...] = (g_ref[...].astype(F32) + p_ref[...].astype(F32)).astype(o_ref.dtype)

    spec = pl.BlockSpec((None, tr, C), lambda s, i, core_ref: (s, i, 0))
    return pl.pallas_call(
        body, name=name, out_shape=jax.ShapeDtypeStruct(p.shape, p.dtype),
        grid_spec=pltpu.PrefetchScalarGridSpec(
            num_scalar_prefetch=1, grid=(N_CHIPS, nrb),
            in_specs=[pl.BlockSpec((None, tr, C), lambda s, i, core_ref: (s, core_ref[0] * nrb + i, 0)), spec],
            out_specs=spec),
        compiler_params=_params())(core, g, p)


def _scatter_copies(sums, slots, send_sem, recv_sem):
    x, y, c, chips = _place()

    def make(a, j):
        px, py = chips[j]
        return pltpu.make_async_remote_copy(
            src_ref=sums[a].at[2 * px + py], dst_ref=slots[a].at[j], send_sem=send_sem.at[3 * a + j],
            recv_sem=recv_sem.at[3 * a + j], device_id=(px, py, c), device_id_type=MESH)

    return make


def _scatter_start(name, sums, after):
    n = len(sums)
    slots = [lax.empty((3,) + s.shape[1:], s.dtype) for s in sums]

    def body(*refs):
        src, dst = refs[2 * n + 1:3 * n + 1], refs[3 * n + 1:4 * n + 1]
        send_sem, recv_sem, token = refs[4 * n + 1:]
        copy = _scatter_copies(src, dst, send_sem, recv_sem)
        for a in range(n):
            for j in range(3):
                copy(a, j).start()
        token[...] = jnp.zeros(token.shape, token.dtype)

    sems = pltpu.SemaphoreType.DMA((3 * n,))
    outs = pl.pallas_call(
        body, name=name, in_specs=[HBM] * (2 * n) + [ANY],
        out_specs=[HBM] * (2 * n) + [SEM, SEM, pl.BlockSpec(memory_space=pltpu.VMEM)],
        out_shape=[pltpu.HBM(b.shape, b.dtype) for b in sums + slots] + [sems, sems, jax.ShapeDtypeStruct((8, BLOCK), F32)],
        input_output_aliases={a: a for a in range(2 * n)},
        compiler_params=pltpu.CompilerParams(has_side_effects=DATAFLOW))(*_in_hbm(sums + slots), after)
    return outs[:n], outs[n:2 * n], outs[2 * n], outs[2 * n + 1], outs[2 * n + 2]


def _scatter_finish(name, sums, slots, send_sem, recv_sem, after):
    n = len(sums)

    def body(*refs):
        send_ref, recv_ref = refs[2 * n], refs[2 * n + 1]
        src, dst = refs[2 * n + 3:3 * n + 3], refs[3 * n + 3:]
        copy = _scatter_copies(src, dst, send_ref, recv_ref)
        for a in range(n):
            for j in range(3):
                copy(a, j).wait_send()
                copy(a, j).wait_recv()

    outs = pl.pallas_call(
        body, name=name, in_specs=[HBM] * (2 * n) + [SEM, SEM, ANY], out_specs=[HBM] * (2 * n),
        out_shape=[pltpu.HBM(b.shape, b.dtype) for b in sums + slots],
        input_output_aliases={a: a for a in range(2 * n)},
        compiler_params=pltpu.CompilerParams(has_side_effects=DATAFLOW))(*sums, *slots, send_sem, recv_sem, after)
    return outs[:n], outs[n:]


def _slot_sum(name, place, slots, sums):
    _, h, C = slots.shape
    tr = _row_tile(h, 512)
    nrb = h // tr

    def body(place_ref, r0, r1, r2, own, o_ref):
        o_ref[...] = ((r0[...].astype(F32) + r1[...].astype(F32)) + r2[...].astype(F32)) + own[...].astype(F32)

    slot = lambda k: pl.BlockSpec((None, tr, C), lambda i, place_ref: (k, i, 0))
    return pl.pallas_call(
        body, name=name, out_shape=jax.ShapeDtypeStruct((2 * h, C), F32),
        grid_spec=pltpu.PrefetchScalarGridSpec(
            num_scalar_prefetch=1, grid=(nrb,),
            in_specs=[slot(0), slot(1), slot(2),
                      pl.BlockSpec((None, tr, C), lambda i, place_ref: (place_ref[0], i, 0))],
            out_specs=pl.BlockSpec((tr, C), lambda i, place_ref: (place_ref[1] * nrb + i, 0))),
        compiler_params=_params())(place, slots, slots, slots, sums)


def _half_copies(bufs, send_sem, recv_sem):
    x, y, c, _ = _place()

    def make(a, core):
        rows = bufs[a].at[_half_rows(core, bufs[a].shape[0])]
        return pltpu.make_async_remote_copy(
            src_ref=rows, dst_ref=rows, send_sem=send_sem.at[a], recv_sem=recv_sem.at[a],
            device_id=(x, y, 1 - c), device_id_type=MESH)

    return make


def _half_start(name, bufs, after):
    n = len(bufs)

    def body(*refs):
        dst = refs[n + 1:2 * n + 1]
        send_sem, recv_sem, token = refs[2 * n + 1:]
        c = lax.axis_index("c")
        copy = _half_copies(dst, send_sem, recv_sem)
        for a in range(n):
            copy(a, c).start()
        token[...] = jnp.zeros(token.shape, token.dtype)

    sems = pltpu.SemaphoreType.DMA((n,))
    outs = pl.pallas_call(
        body, name=name, in_specs=[HBM] * n + [ANY],
        out_specs=[HBM] * n + [SEM, SEM, pl.BlockSpec(memory_space=pltpu.VMEM)],
        out_shape=[pltpu.HBM(b.shape, b.dtype) for b in bufs] + [sems, sems, jax.ShapeDtypeStruct((8, BLOCK), F32)],
        input_output_aliases={a: a for a in range(n)},
        compiler_params=pltpu.CompilerParams(has_side_effects=DATAFLOW))(*_in_hbm(bufs), after)
    return outs[:n], outs[n], outs[n + 1], outs[n + 2]


def _half_finish(name, bufs, send_sem, recv_sem, after):
    n = len(bufs)

    def body(*refs):
        send_ref, recv_ref = refs[n], refs[n + 1]
        dst = refs[n + 3:]
        c = lax.axis_index("c")
        copy = _half_copies(dst, send_ref, recv_ref)
        for a in range(n):
            copy(a, c).wait_send()
            copy(a, 1 - c).wait_recv()

    return pl.pallas_call(
        body, name=name, in_specs=[HBM] * n + [SEM, SEM, ANY], out_specs=[HBM] * n,
        out_shape=[pltpu.HBM(b.shape, b.dtype) for b in bufs],
        input_output_aliases={a: a for a in range(n)},
        compiler_params=pltpu.CompilerParams(has_side_effects=DATAFLOW))(*bufs, send_sem, recv_sem, after)


class _GradReducer:
    def __init__(self, chip, core):
        self.core, self.place, self.pairs, self.started = core, jnp.concatenate([chip, core]), {}, []

    def start(self, layer, group, grads):
        mine = [grads[n] for n in REDUCE[group]]
        mine, lands, send, recv, token = _pair_start("grad_pair_start_%d_%d" % (layer, group), mine, self.place)
        self.pairs[(layer, group)] = (mine, lands, send, recv)
        return token

    def scatter(self, layer, group, after):
        tag = "%d_%d" % (layer, group)
        names = REDUCE[group]
        mine, lands, send, recv = self.pairs.pop((layer, group))
        mine, theirs = _pair_finish("grad_pair_finish_" + tag, mine, lands, send, recv, after)
        sums = [_pair_sum("pair_sum_%s_%d" % (n, layer), self.core, g, p) for n, g, p in zip(names, mine, theirs)]
        sums, slots, send, recv, token = _scatter_start("grad_scatter_start_" + tag, sums, self.place)
        self.started.append((layer, names, sums, slots, send, recv))
        return token

    def finish(self, after, update):
        for layer in sorted({entry[0] for entry in self.started}, reverse=True):
            exchanged = []
            for lyr, names, sums, slots, send, recv in self.started:
                if lyr != layer:
                    continue
                tag = "%s_%d" % (names[0], layer)
                sums, slots = _scatter_finish("grad_scatter_finish_" + tag, sums, slots, send, recv, after)
                halves = [_slot_sum("slot_sum_%s_%d" % (n, layer), self.place, r, s)
                          for n, r, s in zip(names, slots, sums)]
                halves, send, recv, after = _half_start("grad_half_start_" + tag, halves, self.place)
                exchanged.append((tag, names, halves, send, recv))
            for tag, names, halves, send, recv in exchanged:
                whole = _half_finish("grad_half_finish_" + tag, halves, send, recv, after)
                after = update(layer, dict(zip(names, whole)))


def _small_copies(arrays, lands, own_sems, pass_sems):
    def slot(a, block):
        px, py, pc = block
        return lands[a].at[4 * px + 2 * py + pc]

    def own(a, k, block, to):
        return pltpu.make_async_remote_copy(
            src_ref=arrays[a], dst_ref=slot(a, block), send_sem=own_sems[0].at[4 * a + k],
            recv_sem=own_sems[1].at[4 * a + k], device_id=to, device_id_type=MESH)

    def passed(a, j, block, to):
        return pltpu.make_async_remote_copy(
            src_ref=slot(a, block), dst_ref=slot(a, block), send_sem=pass_sems[0].at[3 * a + j],
            recv_sem=pass_sems[1].at[3 * a + j], device_id=to, device_id_type=MESH)

    return own, passed


def _small_start(arrays, after):
    n = len(arrays)
    lands = [lax.empty((2 * N_CHIPS,) + a.shape, a.dtype) for a in arrays]

    def body(*refs):
        src, dst = refs[2 * n + 1:3 * n + 1], refs[3 * n + 1:4 * n + 1]
        send_sem, recv_sem, token = refs[4 * n + 1:]
        x, y, c, chips = _place()
        own, _ = _small_copies(src, dst, (send_sem, recv_sem), None)
        for a in range(n):
            own(a, 0, (x, y, c), (x, y, 1 - c)).start()
            for j, (px, py) in enumerate(chips):
                own(a, 1 + j, (x, y, c), (px, py, c)).start()
        token[...] = jnp.zeros(token.shape, token.dtype)

    sems = pltpu.SemaphoreType.DMA((4 * n,))
    outs = pl.pallas_call(
        body, name="small_grad_start", in_specs=[HBM] * (2 * n) + [ANY],
        out_specs=[HBM] * (2 * n) + [SEM, SEM, pl.BlockSpec(memory_space=pltpu.VMEM)],
        out_shape=[pltpu.HBM(b.shape, b.dtype) for b in arrays + lands]
        + [sems, sems, jax.ShapeDtypeStruct((8, BLOCK), F32)],
        input_output_aliases={a: a for a in range(2 * n)},
        compiler_params=pltpu.CompilerParams(has_side_effects=DATAFLOW))(*_in_hbm(arrays + lands), after)
    return outs[:n], outs[n:2 * n], (outs[2 * n], outs[2 * n + 1]), outs[2 * n + 2]


def _small_forward(arrays, lands, own_sems, after):
    n = len(arrays)

    def body(*refs):
        own_refs = (refs[2 * n], refs[2 * n + 1])
        src, dst = refs[2 * n + 3:3 * n + 3], refs[3 * n + 3:4 * n + 3]
        pass_refs = (refs[4 * n + 3], refs[4 * n + 4])
        x, y, c, chips = _place()
        own, passed = _small_copies(src, dst, own_refs, pass_refs)
        for a in range(n):
            for j, (px, py) in enumerate(chips):
                own(a, 1 + j, (px, py, c), (x, y, c)).wait_recv()
                passed(a, j, (px, py, c), (x, y, 1 - c)).start()
        for a in range(n):
            own(a, 0, (x, y, 1 - c), (x, y, c)).wait_recv()
            own(a, 0, (x, y, c), (x, y, 1 - c)).wait_send()
            for j, (px, py) in enumerate(chips):
                own(a, 1 + j, (x, y, c), (px, py, c)).wait_send()

    sems = pltpu.SemaphoreType.DMA((3 * n,))
    outs = pl.pallas_call(
        body, name="small_grad_forward", in_specs=[HBM] * (2 * n) + [SEM, SEM, ANY],
        out_specs=[HBM] * (2 * n) + [SEM, SEM],
        out_shape=[pltpu.HBM(b.shape, b.dtype) for b in arrays + lands] + [sems, sems],
        input_output_aliases={a: a for a in range(2 * n)},
        compiler_params=pltpu.CompilerParams(has_side_effects=DATAFLOW))(*arrays, *lands, *own_sems, after)
    return outs[:n], outs[n:2 * n], (outs[2 * n], outs[2 * n + 1])


def _small_finish(arrays, lands, pass_sems, after):
    n = len(arrays)

    def body(*refs):
        pass_refs = (refs[2 * n], refs[2 * n + 1])
        src, dst = refs[2 * n + 3:3 * n + 3], refs[3 * n + 3:]
        x, y, c, chips = _place()
        _, passed = _small_copies(src, dst, None, pass_refs)
        for a in range(n):
            for j, (px, py) in enumerate(chips):
                passed(a, j, (px, py, 1 - c), (x, y, c)).wait_recv()
                passed(a, j, (px, py, c), (x, y, 1 - c)).wait_send()

    outs = pl.pallas_call(
        body, name="small_grad_finish", in_specs=[HBM] * (2 * n) + [SEM, SEM, ANY], out_specs=[HBM] * (2 * n),
        out_shape=[pltpu.HBM(b.shape, b.dtype) for b in arrays + lands],
        input_output_aliases={a: a for a in range(2 * n)},
        compiler_params=pltpu.CompilerParams(has_side_effects=DATAFLOW))(*arrays, *lands, *pass_sems, after)
    return outs[:n], outs[n:]


def _small_sum(place, arrays, lands):
    n = len(arrays)
    n_dev = 2 * N_CHIPS

    def body(place_ref, *refs):
        me = 2 * place_ref[0] + place_ref[1]
        for a in range(n):
            own, land, out = refs[a], refs[n + a], refs[2 * n + a]
            acc = None
            for d in range(n_dev):
                term = jnp.where(me == d, own[...], land[jnp.where(me == d, d ^ 1, d)])
                acc = term if acc is None else acc + term
            out[...] = acc

    vm = pl.BlockSpec(memory_space=pltpu.VMEM)
    return pl.pallas_call(
        body, name="small_grad_sum", in_specs=[pl.BlockSpec(memory_space=pltpu.SMEM)] + [vm] * (2 * n),
        out_specs=[vm] * n, out_shape=[jax.ShapeDtypeStruct(a.shape, F32) for a in arrays],
        compiler_params=_params())(place, *arrays, *lands)


def _adamw_math(w, g, m, v):
    m2 = ADAM_B1 * m + (1.0 - ADAM_B1) * g
    v2 = ADAM_B2 * v + (1.0 - ADAM_B2) * (g * g)
    m_hat = m2 / (1.0 - ADAM_B1 ** ADAM_STEP)
    v_hat = v2 / (1.0 - ADAM_B2 ** ADAM_STEP)
    delta = -ADAM_LR * (m_hat / (jnp.sqrt(v_hat) + ADAM_EPS) + ADAM_WD * w)
    return delta, m2, v2


def _adamw_big(name, layer, grad, w, m, v, others):
    L, R, C = w.shape
    tr = _row_tile(R, 256)

    def body(g_ref, w_ref, m_ref, v_ref, *rest):
        go_ref, d_ref, mo_ref, vo_ref = rest[-4:]
        g = g_ref[...]
        delta, m2, v2 = _adamw_math(w_ref[...], g, m_ref[...], v_ref[...])
        go_ref[...] = g
        d_ref[...] = delta
        mo_ref[...] = m2
        vo_ref[...] = v2

    blk = pl.BlockSpec((None, tr, C), lambda i: (layer, i, 0))
    shp = jax.ShapeDtypeStruct(w.shape, F32)
    others = [] if others is None else list(others)
    return pl.pallas_call(
        body, name=name, grid=(R // tr,),
        in_specs=[pl.BlockSpec((tr, C), lambda i: (i, 0))] + [blk] * 3 + [ANY] * len(others), out_specs=[blk] * 4,
        out_shape=[shp] * 4, input_output_aliases={4 + k: k for k in range(len(others))},
        compiler_params=_params())(grad, w, m, v, *others)


def _adamw_small(gs, ws, ms, vs):
    n = len(gs)

    def body(*refs):
        for a in range(n):
            g_ref, w_ref, m_ref, v_ref = refs[a], refs[n + a], refs[2 * n + a], refs[3 * n + a]
            delta, m2, v2 = _adamw_math(w_ref[...], g_ref[...], m_ref[...], v_ref[...])
            refs[4 * n + a][...] = delta
            refs[5 * n + a][...] = m2
            refs[6 * n + a][...] = v2

    vm = pl.BlockSpec(memory_space=pltpu.VMEM)
    shapes = [jax.ShapeDtypeStruct(g.shape, F32) for g in gs]
    outs = pl.pallas_call(
        body, name="adamw_small", in_specs=[vm] * (4 * n), out_specs=[vm] * (3 * n), out_shape=shapes * 3,
        compiler_params=_params())(*gs, *ws, *ms, *vs)
    return outs[:n], outs[n:2 * n], outs[2 * n:]


def _rows2d(a):
    return a if a.ndim == 2 else a.reshape(-1, a.shape[-1])


BIG = ("w_in", "w_attn_branch", "w_sgu_branch", "w_out", "w_gate", "w_up", "w_down")
SMALL = ("mix_norm", "q_norm", "k_norm", "sinks", "sgu_ln_g", "sgu_ln_b", "w_spatial", "b_spatial", "ffn_norm")
ORDER = ("mix_norm", "w_in", "q_norm", "k_norm", "sinks", "sgu_ln_g", "sgu_ln_b", "w_spatial", "b_spatial",
         "w_attn_branch", "w_sgu_branch", "w_out", "ffn_norm", "w_gate", "w_up", "w_down")


def _rope_tables(seq):
    pos = jnp.arange(seq, dtype=F32)
    inv_freq = jnp.power(10000.0, -jnp.arange(0, HEAD_DIM, 2, dtype=F32) / HEAD_DIM)
    ang = pos[:, None] * inv_freq[None, :]
    cos, sin = jnp.cos(ang), jnp.sin(ang)
    reps = BLOCK // HEAD_DIM
    return (jnp.tile(jnp.concatenate([cos, cos], axis=1), (1, reps)),
            jnp.tile(jnp.concatenate([-sin, sin], axis=1), (1, reps)))


def kernel(x, mix_norm, w_in, q_norm, k_norm, sinks, sgu_ln_g, sgu_ln_b, w_spatial, b_spatial, w_attn_branch, w_sgu_branch, w_out, ffn_norm, w_gate, w_up, w_down, loss_target, m_mix_norm, m_w_in, m_q_norm, m_k_norm, m_sinks, m_sgu_ln_g, m_sgu_ln_b, m_w_spatial, m_b_spatial, m_w_attn_branch, m_w_sgu_branch, m_w_out, m_ffn_norm, m_w_gate, m_w_up, m_w_down, v_mix_norm, v_w_in, v_q_norm, v_k_norm, v_sinks, v_sgu_ln_g, v_sgu_ln_b, v_w_spatial, v_b_spatial, v_w_attn_branch, v_w_sgu_branch, v_w_out, v_ffn_norm, v_w_gate, v_w_up, v_w_down):
    weights = dict(mix_norm=mix_norm, w_in=w_in, q_norm=q_norm, k_norm=k_norm, sinks=sinks, sgu_ln_g=sgu_ln_g,
                   sgu_ln_b=sgu_ln_b, w_spatial=w_spatial, b_spatial=b_spatial, w_attn_branch=w_attn_branch,
                   w_sgu_branch=w_sgu_branch, w_out=w_out, ffn_norm=ffn_norm, w_gate=w_gate, w_up=w_up, w_down=w_down)
    mom1 = dict(mix_norm=m_mix_norm, w_in=m_w_in, q_norm=m_q_norm, k_norm=m_k_norm, sinks=m_sinks,
                sgu_ln_g=m_sgu_ln_g, sgu_ln_b=m_sgu_ln_b, w_spatial=m_w_spatial, b_spatial=m_b_spatial,
                w_attn_branch=m_w_attn_branch, w_sgu_branch=m_w_sgu_branch, w_out=m_w_out, ffn_norm=m_ffn_norm,
                w_gate=m_w_gate, w_up=m_w_up, w_down=m_w_down)
    mom2 = dict(mix_norm=v_mix_norm, w_in=v_w_in, q_norm=v_q_norm, k_norm=v_k_norm, sinks=v_sinks,
                sgu_ln_g=v_sgu_ln_g, sgu_ln_b=v_sgu_ln_b, w_spatial=v_w_spatial, b_spatial=v_b_spatial,
                w_attn_branch=v_w_attn_branch, w_sgu_branch=v_w_sgu_branch, w_out=v_w_out, ffn_norm=v_ffn_norm,
                w_gate=v_w_gate, w_up=v_w_up, w_down=v_w_down)
    xs, target = x[0], loss_target[0]
    S, D = xs.shape
    L = w_in.shape[0]
    AW, KW, SW = N_Q_HEADS * HEAD_DIM, N_KV_HEADS * HEAD_DIM, SGU_GROUPS * BLOCK
    dims = (AW, KW, AW + 2 * KW + 2 * SW, AW + 2 * KW)
    cos, sin = _rope_tables(S)
    reps = BLOCK // HEAD_DIM

    chip = (2 * lax.axis_index("x") + lax.axis_index("y")).astype(jnp.int32).reshape(1)
    core = lax.axis_index("c").astype(jnp.int32).reshape(1)
    started, token = {}, chip
    for l in range(L):
        for gi, names in enumerate(GATHER):
            bufs = [_cast_own("cast_%s_%d" % (n, l), chip, weights[n], l) for n in names]
            bufs, send, recv, token = _gather_start("gather_start_%d_%d" % (l, gi), bufs, token)
            started[(l, gi)] = (bufs, send, recv)
    stream = _WeightStream(started)
    sp = [dict(mix_norm=mix_norm[l][None], ffn_norm=ffn_norm[l][None], q_norm=jnp.tile(q_norm[l][None], (1, reps)),
               k_norm=jnp.tile(k_norm[l][None], (1, reps)), sinks=sinks[l][None], sgu_ln_g=sgu_ln_g[l][None],
               sgu_ln_b=sgu_ln_b[l][None], w_spatial=w_spatial[l], b_spatial=b_spatial[l][:, :, None])
          for l in range(L)]

    act, saved, wl = xs, [], []
    for l in range(L):
        act, sv, w_all = _layer_fwd(act, stream, l, token if l == 0 else act, sp[l], cos, sin, dims)
        saved.append(sv)
        wl.append(w_all)
    loss_part, dy, dyb = _loss_head(act, target)
    loss = lax.psum(loss_part[0, 0], ("x", "y", "c"))

    reducer = _GradReducer(chip, core)
    small_g = [None] * L
    for l in reversed(range(L)):
        dy, dyb, small_g[l] = _layer_bwd(dy, dyb, wl[l], sp[l], saved[l], cos, sin, dims, reducer, l)
    grad_x = dy[None]

    local = [_rows2d(jnp.stack([small_g[l][n].reshape(weights[n].shape[1:]) for l in range(L)])) for n in SMALL]
    small = list(_small_start(local, dy))
    updated = {}

    def update(layer, reduced):
        for n, g in reduced.items():
            updated[n] = _adamw_big("adamw_%s_%d" % (n, layer), layer, g, weights[n], mom1[n], mom2[n],
                                    updated.get(n))
        if len(small) == 4:
            arrays, lands, own_sems, _ = small
            small[:] = _small_forward(arrays, lands, own_sems, updated[n][0])
        return updated[n][0]

    reducer.finish(small[3], update)
    grads, deltas, new_m, new_v = {}, {}, {}, {}
    for n in BIG:
        grads[n], deltas[n], new_m[n], new_v[n] = updated[n]

    arrays, lands = _small_finish(*small, updated[BIG[0]][0])
    g_small = _small_sum(jnp.concatenate([chip, core]), arrays, lands)
    d_small, m_small, v_small = _adamw_small(g_small, [_rows2d(weights[n]) for n in SMALL],
                                             [_rows2d(mom1[n]) for n in SMALL], [_rows2d(mom2[n]) for n in SMALL])
    for n, g, d, m2, v2 in zip(SMALL, g_small, d_small, m_small, v_small):
        shape = weights[n].shape
        grads[n], deltas[n], new_m[n], new_v[n] = g.reshape(shape), d.reshape(shape), m2.reshape(shape), v2.reshape(shape)

    return (loss, grad_x, *[grads[n] for n in ORDER], *[deltas[n] for n in ORDER],
            *[new_m[n] for n in ORDER], *[new_v[n] for n in ORDER])
```

```python
import functools

import jax
import jax.numpy as jnp
from jax import lax
from jax.experimental import pallas as pl
from jax.experimental.pallas import tpu as pltpu

HEAD_DIM = 64
N_Q_HEADS = 16
N_KV_HEADS = 4
SGU_GROUPS = 8
BLOCK = 128
EPS = 1e-6
ADAM_LR = 0.001
ADAM_B1 = 0.9
ADAM_B2 = 0.999
ADAM_EPS = 1e-08
ADAM_WD = 0.01
ADAM_STEP = 10
N_CHIPS = 4
VMEM_LIMIT = 52 * 1024 * 1024
MXU_CHUNK = 256
ATTN_STACK = 4

F32 = jnp.float32
MXU = jnp.bfloat16
NN = (((1,), (0,)), ((), ()))
NT = (((1,), (1,)), ((), ()))
TN = (((0,), (0,)), ((), ()))
MESH = pl.DeviceIdType.MESH
ANY = pl.BlockSpec(memory_space=pl.ANY)


def _tile(n, pref):
    if n <= pref:
        return n
    best = None
    for t in range(BLOCK, pref + 1, BLOCK):
        if n % t == 0:
            best = t
    assert best is not None, (n, pref)
    return best


def _params():
    return pltpu.CompilerParams(vmem_limit_bytes=VMEM_LIMIT)


def _mm(name, grid, n_red, operands, specs, pairs, dims, n_extra, out_shapes, out_specs,
        acc_shapes, epilogue, after=None, chunk=None):
    n_op = len(operands) - n_extra
    n_out = len(out_shapes)
    n_acc = len(acc_shapes)
    if after is not None:
        operands, specs = list(operands) + [after], list(specs) + [ANY]
    n_in = len(operands)
    axes = [ax for ax in range(len(grid) - n_red, len(grid)) if grid[ax] > 1]

    def body(*refs):
        ops = refs[:n_op]
        extra = refs[n_op:n_op + n_extra]
        outs = refs[n_in:n_in + n_out]
        accs = refs[n_in + n_out:]

        def prod(a, b, cols=None):
            rhs = ops[b]
            if cols is not None:
                rhs = rhs.at[:, cols] if dims == NN else rhs.at[cols, :]
            return lax.dot_general(ops[a][...], rhs[...], dims, preferred_element_type=F32)

        def products(cols=None):
            vals = [None] * n_acc
            for a, b, k in pairs:
                d = prod(a, b, cols)
                vals[k] = d if vals[k] is None else vals[k] + d
            return vals

        if not axes and chunk is not None:
            width = outs[0].shape[-1]
            for c0 in range(0, width, chunk):
                cols = pl.ds(c0, min(chunk, width - c0))
                epilogue(products(cols), [e.at[:, cols] for e in extra], [o.at[:, cols] for o in outs])
        elif not axes:
            epilogue(products(), extra, outs)
        else:
            first = pl.program_id(axes[0]) == 0
            last = pl.program_id(axes[0]) == grid[axes[0]] - 1
            for ax in axes[1:]:
                first = jnp.logical_and(first, pl.program_id(ax) == 0)
                last = jnp.logical_and(last, pl.program_id(ax) == grid[ax] - 1)

            @pl.when(first)
            def _():
                for acc in accs:
                    acc[...] = jnp.zeros(acc.shape, F32)

            for a, b, k in pairs:
                accs[k][...] += prod(a, b)

            @pl.when(last)
            def _():
                epilogue([acc[...] for acc in accs], extra, outs)

    scratch = [pltpu.VMEM(s, F32) for s in acc_shapes] if axes else []
    return pl.pallas_call(
        body, name=name, grid=grid, in_specs=specs, out_specs=out_specs, out_shape=out_shapes,
        scratch_shapes=scratch, compiler_params=_params())(*operands)


def _sigmoid(x):
    return 1.0 / (1.0 + jnp.exp(-x))


_GELU_C = 0.7978845608028654
_GELU_A = 0.044715


def _gelu(x):
    return 0.5 * x * (1.0 + jnp.tanh(_GELU_C * (x + _GELU_A * x * x * x)))


def _gelu_grad(x):
    t = jnp.tanh(_GELU_C * (x + _GELU_A * x * x * x))
    return 0.5 * (1.0 + t) + 0.5 * x * (1.0 - t * t) * _GELU_C * (1.0 + 3.0 * _GELU_A * x * x)


def _rms_fwd(name, x, g, after):
    S, D = x.shape
    tr = _tile(S, 256)

    def body(x_ref, g_ref, after_ref, o_ref):
        xv = x_ref[...]
        r = lax.rsqrt(jnp.mean(xv * xv, axis=-1, keepdims=True) + EPS)
        o_ref[...] = (xv * r * g_ref[...]).astype(MXU)

    return pl.pallas_call(
        body, name=name, grid=(S // tr,),
        in_specs=[pl.BlockSpec((tr, D), lambda i: (i, 0)), pl.BlockSpec((1, D), lambda i: (0, 0)), ANY],
        out_specs=pl.BlockSpec((tr, D), lambda i: (i, 0)),
        out_shape=jax.ShapeDtypeStruct((S, D), MXU), compiler_params=_params())(x, g, after)


def _rms_bwd(name, dh, x, g, dres, after):
    S, D = x.shape
    tr = _tile(S, 256)

    def body(dh_ref, x_ref, g_ref, dres_ref, after_ref, dx_ref, dxb_ref, dg_ref):
        xv = x_ref[...]
        r = lax.rsqrt(jnp.mean(xv * xv, axis=-1, keepdims=True) + EPS)
        xh = xv * r
        dhv = dh_ref[...]
        dy = dhv * g_ref[...]
        dx = dres_ref[...] + r * (dy - xh * jnp.mean(dy * xh, axis=-1, keepdims=True))
        dx_ref[...] = dx
        dxb_ref[...] = dx.astype(MXU)

        @pl.when(pl.program_id(0) == 0)
        def _():
            dg_ref[...] = jnp.zeros(dg_ref.shape, F32)

        dg_ref[...] += jnp.sum(dhv * xh, axis=0, keepdims=True)

    row = pl.BlockSpec((tr, D), lambda i: (i, 0))
    vec = pl.BlockSpec((1, D), lambda i: (0, 0))
    return pl.pallas_call(
        body, name=name, grid=(S // tr,), in_specs=[row, row, vec, row, ANY], out_specs=[row, row, vec],
        out_shape=[jax.ShapeDtypeStruct((S, D), F32), jax.ShapeDtypeStruct((S, D), MXU),
                   jax.ShapeDtypeStruct((1, D), F32)],
        compiler_params=_params())(dh, x, g, dres, after)


def _join_columns(name, parts):
    S = parts[0].shape[0]
    tr = _tile(S, 256)
    widths = [p.shape[1] for p in parts]
    assert all(w % BLOCK == 0 for w in widths)

    def body(*refs):
        o_ref, off = refs[-1], 0
        for ref, w in zip(refs[:-1], widths):
            o_ref[:, off:off + w] = ref[...]
            off += w

    return pl.pallas_call(
        body, name=name, grid=(S // tr,), in_specs=[pl.BlockSpec((tr, w), lambda i: (i, 0)) for w in widths],
        out_specs=pl.BlockSpec((tr, sum(widths)), lambda i: (i, 0)),
        out_shape=jax.ShapeDtypeStruct((S, sum(widths)), parts[0].dtype), compiler_params=_params())(*parts)


def _loss_head(y, target):
    S, D = y.shape
    tr = _tile(S, 256)

    def body(y_ref, t_ref, loss_ref, dy_ref, dyb_ref):
        d = y_ref[...] - t_ref[...]
        dy = d * (1.0 / D)
        dy_ref[...] = dy
        dyb_ref[...] = dy.astype(MXU)

        @pl.when(pl.program_id(0) == 0)
        def _():
            loss_ref[...] = jnp.zeros(loss_ref.shape, F32)

        loss_ref[...] += (0.5 / D) * jnp.sum(jnp.sum(d * d, axis=-1, keepdims=True), axis=0, keepdims=True)

    row = pl.BlockSpec((tr, D), lambda i: (i, 0))
    return pl.pallas_call(
        body, name="loss_head", grid=(S // tr,), in_specs=[row, row],
        out_specs=[pl.BlockSpec((1, 1), lambda i: (0, 0)), row, row],
        out_shape=[jax.ShapeDtypeStruct((1, 1), F32), jax.ShapeDtypeStruct((S, D), F32),
                   jax.ShapeDtypeStruct((S, D), MXU)],
        compiler_params=_params())(y, target)


def _head_sum(v):
    r = lax.broadcasted_iota(jnp.int32, (BLOCK, BLOCK), 0) // HEAD_DIM
    c = lax.broadcasted_iota(jnp.int32, (BLOCK, BLOCK), 1) // HEAD_DIM
    ones = jnp.where(r == c, 1.0, 0.0).astype(jnp.bfloat16)
    hi = v.astype(jnp.bfloat16)
    lo = (v - hi.astype(F32)).astype(jnp.bfloat16)
    parts = []
    for t in range(v.shape[1] // BLOCK):
        sl = slice(t * BLOCK, (t + 1) * BLOCK)
        parts.append(jnp.dot(hi[:, sl], ones, preferred_element_type=F32)
                     + jnp.dot(lo[:, sl], ones, preferred_element_type=F32))
    return parts[0] if len(parts) == 1 else jnp.concatenate(parts, axis=-1)


def _swap_halves(v):
    w = v.shape[1]
    half = HEAD_DIM // 2
    lane = lax.broadcasted_iota(jnp.int32, v.shape, 1) % HEAD_DIM
    return jnp.where(lane < half, pltpu.roll(v, w - half, 1), pltpu.roll(v, half, 1))


def _norm_rope(xv, gain, cos, sin):
    r = lax.rsqrt(_head_sum(xv * xv) * (1.0 / HEAD_DIM) + EPS)
    xn = xv * r * gain
    return xn * cos + _swap_halves(xn) * sin


def _norm_rope_bwd(dy, xv, gain, cos, sin):
    r = lax.rsqrt(_head_sum(xv * xv) * (1.0 / HEAD_DIM) + EPS)
    xh = xv * r
    dxn = dy * cos + _swap_halves(dy * sin)
    dgain = jnp.sum(dxn * xh, axis=0, keepdims=True)
    dxh = dxn * gain
    dx = r * (dxh - xh * (_head_sum(dxh * xh) * (1.0 / HEAD_DIM)))
    return dx, dgain


def _fold_heads(v):
    acc = v[:, 0:BLOCK]
    for t in range(1, v.shape[1] // BLOCK):
        acc = acc + v[:, t * BLOCK:(t + 1) * BLOCK]
    return acc + pltpu.roll(acc, HEAD_DIM, 1)


def _tile_lanes(v, width):
    return v if width == BLOCK else jnp.tile(v, (1, width // BLOCK))


def _low_half(rows):
    assert BLOCK == 2 * HEAD_DIM
    return lax.broadcasted_iota(jnp.int32, (rows, BLOCK), 1) < HEAD_DIM


def _spread_heads(v):
    low = _low_half(v.shape[0])
    out = []
    for t in range(v.shape[1] // BLOCK):
        tile = v[:, t * BLOCK:(t + 1) * BLOCK]
        swapped = pltpu.roll(tile, HEAD_DIM, 1)
        out += [jnp.where(low, tile, swapped), jnp.where(low, swapped, tile)]
    return jnp.concatenate(out, axis=-1)


def _gather_heads(v):
    low = _low_half(v.shape[0])
    out = []
    for t in range(v.shape[1] // (2 * BLOCK)):
        a, b = v[:, 2 * t * BLOCK:(2 * t + 1) * BLOCK], v[:, (2 * t + 1) * BLOCK:(2 * t + 2) * BLOCK]
        out.append(jnp.where(low, a + pltpu.roll(a, HEAD_DIM, 1), b + pltpu.roll(b, HEAD_DIM, 1)))
    return out[0] if len(out) == 1 else jnp.concatenate(out, axis=-1)


def _qk_prep(proj, qg, kg, cos, sin, AW, KW):
    S = proj.shape[0]
    tr = _tile(S, 256)
    scale = HEAD_DIM ** -0.5

    def body(q_ref, k_ref, v_ref, qg_ref, kg_ref, cos_ref, sin_ref, qo_ref, ko_ref, vo_ref):
        c, s = cos_ref[...], sin_ref[...]
        q = _norm_rope(q_ref[...], _tile_lanes(qg_ref[...], AW), _tile_lanes(c, AW), _tile_lanes(s, AW))
        k = _norm_rope(k_ref[...], _tile_lanes(kg_ref[...], KW), _tile_lanes(c, KW), _tile_lanes(s, KW))
        qo_ref[...] = (q * scale).astype(MXU)
        ko_ref[...] = _spread_heads(k).astype(MXU)
        vo_ref[...] = _spread_heads(v_ref[...]).astype(MXU)

    assert AW % KW == 0
    vec = pl.BlockSpec((1, BLOCK), lambda i: (0, 0))
    tab = pl.BlockSpec((tr, BLOCK), lambda i: (i, 0))
    wide = pl.BlockSpec((tr, 2 * KW), lambda i: (i, 0))
    return pl.pallas_call(
        body, name="qk_prep", grid=(S // tr,),
        in_specs=[pl.BlockSpec((tr, AW), lambda i: (i, 0)),
                  pl.BlockSpec((tr, KW), lambda i: (i, AW // KW)),
                  pl.BlockSpec((tr, KW), lambda i: (i, AW // KW + 1)), vec, vec, tab, tab],
        out_specs=[pl.BlockSpec((tr, AW), lambda i: (i, 0)), wide, wide],
        out_shape=[jax.ShapeDtypeStruct((S, AW), MXU), jax.ShapeDtypeStruct((S, 2 * KW), MXU),
                   jax.ShapeDtypeStruct((S, 2 * KW), MXU)],
        compiler_params=_params())(proj, proj, proj, qg, kg, cos, sin)


def _stack_heads(x, h0, nh):
    low = _low_half(BLOCK)
    parts = []
    for h in range(h0, h0 + nh):
        tile = x[:, (h // 2) * BLOCK:(h // 2 + 1) * BLOCK]
        parts.append(jnp.where(low if h % 2 == 0 else jnp.logical_not(low), tile, jnp.zeros_like(tile)))
    return jnp.concatenate(parts, axis=0)


def _unstack_heads(y):
    low = _low_half(BLOCK)
    tiles = [jnp.where(low, y[2 * t * BLOCK:(2 * t + 1) * BLOCK], y[(2 * t + 1) * BLOCK:(2 * t + 2) * BLOCK])
             for t in range(y.shape[0] // (2 * BLOCK))]
    return tiles[0] if len(tiles) == 1 else jnp.concatenate(tiles, axis=-1)


def _band_t(n):
    key = lax.broadcasted_iota(jnp.int32, (2 * BLOCK, BLOCK), 0)
    qry = lax.broadcasted_iota(jnp.int32, (2 * BLOCK, BLOCK), 1)
    return (key > qry) & (key <= qry + BLOCK) & ((key >= BLOCK) | (n > 0))


def _attn_probs_t(ok, qs, kcat, h0, nh, sink_ref):
    st = jnp.where(ok, lax.dot_general(kcat, qs, NT, preferred_element_type=F32), -1e30)
    sk = jnp.concatenate([jnp.full((1, BLOCK), sink_ref[0, h], F32) for h in range(h0, h0 + nh)], axis=1)
    m = jnp.maximum(jnp.max(st, axis=0, keepdims=True), sk)
    e = jnp.exp(st - m)
    es = jnp.exp(sk - m)
    rz = 1.0 / (jnp.sum(e, axis=0, keepdims=True) + es)
    return e * rz, es * rz, rz


def _attn_fwd(qr, kr, vb, sinks):
    S, AW = qr.shape
    KW = kr.shape[1]
    nb = S // BLOCK
    nkv = KW // BLOCK
    qpk = AW // (nkv * HEAD_DIM)
    nh = min(ATTN_STACK, qpk)
    assert nh % 2 == 0 and qpk % nh == 0

    def body(sink_ref, q_ref, kp_ref, kc_ref, vp_ref, vc_ref, o_ref):
        n = pl.program_id(0)
        q, kp, kc, vp, vc = q_ref[...], kp_ref[...], kc_ref[...], vp_ref[...], vc_ref[...]
        ok = jnp.concatenate([_band_t(n)] * nh, axis=1)
        outs = []
        for g in range(nkv):
            kcat = jnp.concatenate([kp[:, g * BLOCK:(g + 1) * BLOCK], kc[:, g * BLOCK:(g + 1) * BLOCK]], axis=0)
            vcat = jnp.concatenate([vp[:, g * BLOCK:(g + 1) * BLOCK], vc[:, g * BLOCK:(g + 1) * BLOCK]], axis=0)
            for h0 in range(g * qpk, (g + 1) * qpk, nh):
                pt, _, _ = _attn_probs_t(ok, _stack_heads(q, h0, nh), kcat, h0, nh, sink_ref)
                outs.append(_unstack_heads(lax.dot_general(pt.astype(MXU), vcat, TN, preferred_element_type=F32)))
        o_ref[...] = jnp.concatenate(outs, axis=-1).astype(MXU)

    cur = lambda n: (n, 0)
    prev = lambda n: (jnp.maximum(n - 1, 0), 0)
    return pl.pallas_call(
        body, name="attn_fwd", grid=(nb,),
        in_specs=[pl.BlockSpec(memory_space=pltpu.SMEM), pl.BlockSpec((BLOCK, AW), cur),
                  pl.BlockSpec((BLOCK, KW), prev), pl.BlockSpec((BLOCK, KW), cur),
                  pl.BlockSpec((BLOCK, KW), prev), pl.BlockSpec((BLOCK, KW), cur)],
        out_specs=pl.BlockSpec((BLOCK, AW), cur),
        out_shape=jax.ShapeDtypeStruct((S, AW), MXU), compiler_params=_params())(sinks, qr, kr, kr, vb, vb)


def _attn_bwd(qr, kr, vb, sinks, dattn):
    S, AW = qr.shape
    KW = kr.shape[1]
    nb = S // BLOCK
    nkv = KW // BLOCK
    qpk = AW // (nkv * HEAD_DIM)
    nh = min(ATTN_STACK, qpk)
    scale = HEAD_DIM ** -0.5

    def body(sink_ref, q_ref, kp_ref, kc_ref, vp_ref, vc_ref, do_ref,
             dq_ref, dkp_ref, dkc_ref, dvp_ref, dvc_ref, dsink_ref):
        n = pl.program_id(0)
        q, kp, kc, vp, vc = q_ref[...], kp_ref[...], kc_ref[...], vp_ref[...], vc_ref[...]
        do = do_ref[...].astype(MXU)
        lane = lax.broadcasted_iota(jnp.int32, (1, BLOCK), 1)
        ok = jnp.concatenate([_band_t(n)] * nh, axis=1)
        dsink = jnp.zeros((1, BLOCK), F32)
        dqs, dkps, dkcs, dvps, dvcs = [], [], [], [], []
        for g in range(nkv):
            kcat = jnp.concatenate([kp[:, g * BLOCK:(g + 1) * BLOCK], kc[:, g * BLOCK:(g + 1) * BLOCK]], axis=0)
            vcat = jnp.concatenate([vp[:, g * BLOCK:(g + 1) * BLOCK], vc[:, g * BLOCK:(g + 1) * BLOCK]], axis=0)
            dk, dv = None, None
            for h0 in range(g * qpk, (g + 1) * qpk, nh):
                qs = _stack_heads(q, h0, nh)
                dos = _stack_heads(do, h0, nh)
                pt, ps, _ = _attn_probs_t(ok, qs, kcat, h0, nh, sink_ref)
                dpt = lax.dot_general(vcat, dos, NT, preferred_element_type=F32)
                delta = jnp.sum(pt * dpt, axis=0, keepdims=True)
                dst = (pt * (dpt - delta)).astype(MXU)
                dsk = -ps * delta
                dv_part = jnp.dot(pt.astype(MXU), dos, preferred_element_type=F32)
                dk_part = jnp.dot(dst, qs, preferred_element_type=F32)
                dqs.append(_unstack_heads(lax.dot_general(dst, kcat, TN, preferred_element_type=F32) * scale))
                dk = dk_part if dk is None else dk + dk_part
                dv = dv_part if dv is None else dv + dv_part
                for j in range(nh):
                    tot = jnp.sum(dsk[:, j * BLOCK:(j + 1) * BLOCK], axis=1, keepdims=True)
                    dsink = dsink + jnp.where(lane == h0 + j, tot, 0.0)
            dkps.append(dk[:BLOCK])
            dkcs.append(dk[BLOCK:])
            dvps.append(dv[:BLOCK])
            dvcs.append(dv[BLOCK:])
        dq_ref[...] = jnp.concatenate(dqs, axis=-1)
        dkp_ref[...] = jnp.concatenate(dkps, axis=-1)
        dkc_ref[...] = jnp.concatenate(dkcs, axis=-1)
        dvp_ref[...] = jnp.concatenate(dvps, axis=-1)
        dvc_ref[...] = jnp.concatenate(dvcs, axis=-1)

        @pl.when(n == 0)
        def _():
            dsink_ref[...] = jnp.zeros(dsink_ref.shape, F32)

        dsink_ref[...] += dsink

    cur = lambda n: (n, 0)
    prev = lambda n: (jnp.maximum(n - 1, 0), 0)
    kv = jax.ShapeDtypeStruct((S, KW), F32)
    kvspec = pl.BlockSpec((BLOCK, KW), cur)
    return pl.pallas_call(
        body, name="attn_bwd", grid=(nb,),
        in_specs=[pl.BlockSpec(memory_space=pltpu.SMEM), pl.BlockSpec((BLOCK, AW), cur),
                  pl.BlockSpec((BLOCK, KW), prev), kvspec, pl.BlockSpec((BLOCK, KW), prev), kvspec,
                  pl.BlockSpec((BLOCK, AW), cur)],
        out_specs=[pl.BlockSpec((BLOCK, AW), cur), kvspec, kvspec, kvspec, kvspec,
                   pl.BlockSpec((1, BLOCK), lambda n: (0, 0))],
        out_shape=[jax.ShapeDtypeStruct((S, AW), F32), kv, kv, kv, kv, jax.ShapeDtypeStruct((1, BLOCK), F32)],
        compiler_params=_params())(sinks, qr, kr, kr, vb, vb, dattn)


def _qk_prep_bwd(proj, qg, kg, cos, sin, dq, dkp, dkc, dvp, dvc, AW, KW):
    S = proj.shape[0]
    nb = S // BLOCK

    def body(q_ref, k_ref, qg_ref, kg_ref, cos_ref, sin_ref, dq_ref, dkp_ref, dkc_ref, dvp_ref, dvc_ref,
             o_ref, dqg_ref, dkg_ref):
        n = pl.program_id(0)
        c, s = cos_ref[...], sin_ref[...]
        has_next = jnp.where(n < nb - 1, 1.0, 0.0)
        dk = _gather_heads(dkc_ref[...] + has_next * dkp_ref[...])
        dv = _gather_heads(dvc_ref[...] + has_next * dvp_ref[...])
        dxq, dqg = _norm_rope_bwd(dq_ref[...], q_ref[...], _tile_lanes(qg_ref[...], AW),
                                  _tile_lanes(c, AW), _tile_lanes(s, AW))
        dxk, dkg = _norm_rope_bwd(dk, k_ref[...], _tile_lanes(kg_ref[...], KW),
                                  _tile_lanes(c, KW), _tile_lanes(s, KW))
        o_ref[...] = jnp.concatenate([dxq, dxk, dv], axis=-1).astype(MXU)

        @pl.when(n == 0)
        def _():
            dqg_ref[...] = jnp.zeros(dqg_ref.shape, F32)
            dkg_ref[...] = jnp.zeros(dkg_ref.shape, F32)

        dqg_ref[...] += _fold_heads(dqg)
        dkg_ref[...] += _fold_heads(dkg)

    cur = lambda n: (n, 0)
    nxt = lambda n: (jnp.minimum(n + 1, nb - 1), 0)
    vec = pl.BlockSpec((1, BLOCK), lambda n: (0, 0))
    tab = pl.BlockSpec((BLOCK, BLOCK), cur)
    return pl.pallas_call(
        body, name="qk_prep_bwd", grid=(nb,),
        in_specs=[pl.BlockSpec((BLOCK, AW), cur), pl.BlockSpec((BLOCK, KW), lambda n: (n, AW // KW)),
                  vec, vec, tab, tab, pl.BlockSpec((BLOCK, AW), cur),
                  pl.BlockSpec((BLOCK, 2 * KW), nxt), pl.BlockSpec((BLOCK, 2 * KW), cur),
                  pl.BlockSpec((BLOCK, 2 * KW), nxt), pl.BlockSpec((BLOCK, 2 * KW), cur)],
        out_specs=[pl.BlockSpec((BLOCK, AW + 2 * KW), cur), vec, vec],
        out_shape=[jax.ShapeDtypeStruct((S, AW + 2 * KW), MXU), jax.ShapeDtypeStruct((1, BLOCK), F32),
                   jax.ShapeDtypeStruct((1, BLOCK), F32)],
        compiler_params=_params())(proj, proj, qg, kg, cos, sin, dq, dkp, dkc, dvp, dvc)


SGU_LANES = 512
SGU_ROWS = 256


def _sgu_group(v, lng, lnb, w_f32, b):
    rows = v.shape[0]
    mu = jnp.mean(v, axis=-1, keepdims=True)
    vc = v - mu
    r = lax.rsqrt(jnp.mean(vc * vc, axis=-1, keepdims=True) + EPS)
    xh = vc * r
    vn = (xh * lng + lnb).astype(MXU)
    row = lax.broadcasted_iota(jnp.int32, (BLOCK, BLOCK), 0)
    col = lax.broadcasted_iota(jnp.int32, (BLOCK, BLOCK), 1)
    tri = row >= col
    w = jnp.where(tri, w_f32, 0.0).astype(MXU)
    chunks = [jnp.dot(w, vn[k * BLOCK:(k + 1) * BLOCK], preferred_element_type=F32) + b for k in range(rows // BLOCK)]
    s = chunks[0] if len(chunks) == 1 else jnp.concatenate(chunks, axis=0)
    return xh, r, vn, w, s, tri


def _sgu_layout(S, u_col):
    SW = SGU_GROUPS * BLOCK
    lb, tr = min(SGU_LANES, SW), min(SGU_ROWS, S)
    assert u_col % lb == 0 and SW % lb == 0 and S % tr == 0
    ub, nlb, gpb = u_col // lb, SW // lb, lb // BLOCK
    specs = [pl.BlockSpec((tr, lb), lambda j, i: (i, ub + j)), pl.BlockSpec((tr, lb), lambda j, i: (i, ub + nlb + j)),
             pl.BlockSpec((1, lb), lambda j, i: (0, j)), pl.BlockSpec((1, lb), lambda j, i: (0, j)),
             pl.BlockSpec((gpb, BLOCK, BLOCK), lambda j, i: (j, 0, 0)),
             pl.BlockSpec((gpb, BLOCK, 1), lambda j, i: (j, 0, 0))]
    return lb, tr, gpb, nlb, specs


def _sgu_fwd(proj, lng, lnb, ws, bs, u_col):
    S = proj.shape[0]
    lb, tr, gpb, nlb, specs = _sgu_layout(S, u_col)

    def body(pu_ref, pv_ref, lng_ref, lnb_ref, w_ref, b_ref, o_ref):
        u = _gelu(pu_ref[...])
        v = _gelu(pv_ref[...])
        outs = []
        for g in range(gpb):
            sl = slice(g * BLOCK, (g + 1) * BLOCK)
            s = _sgu_group(v[:, sl], lng_ref[:, sl], lnb_ref[:, sl], w_ref[g], b_ref[g])[4]
            outs.append(u[:, sl] * s)
        o_ref[...] = (outs[0] if gpb == 1 else jnp.concatenate(outs, axis=-1)).astype(MXU)

    return pl.pallas_call(
        body, name="sgu_fwd", grid=(nlb, S // tr), in_specs=specs,
        out_specs=pl.BlockSpec((tr, lb), lambda j, i: (i, j)),
        out_shape=jax.ShapeDtypeStruct((S, nlb * lb), MXU), compiler_params=_params())(proj, proj, lng, lnb, ws, bs)


def _sgu_bwd(proj, lng, lnb, ws, bs, dsgu, u_col, after):
    S = proj.shape[0]
    G = SGU_GROUPS
    lb, tr, gpb, nlb, specs = _sgu_layout(S, u_col)
    nch = tr // BLOCK

    def body(pu_ref, pv_ref, lng_ref, lnb_ref, w_ref, b_ref, do_ref, after_ref,
             dpu_ref, dpv_ref, dw_ref, db_ref, dlng_ref, dlnb_ref):
        pu, pv, do = pu_ref[...], pv_ref[...], do_ref[...]
        u = _gelu(pu)
        v = _gelu(pv)

        @pl.when(pl.program_id(1) == 0)
        def _():
            dw_ref[...] = jnp.zeros(dw_ref.shape, F32)
            db_ref[...] = jnp.zeros(db_ref.shape, F32)
            dlng_ref[...] = jnp.zeros(dlng_ref.shape, F32)
            dlnb_ref[...] = jnp.zeros(dlnb_ref.shape, F32)

        ss, dvs, dlng, dlnb = [], [], [], []
        for g in range(gpb):
            sl = slice(g * BLOCK, (g + 1) * BLOCK)
            xh, r, vn, w, s, tri = _sgu_group(v[:, sl], lng_ref[:, sl], lnb_ref[:, sl], w_ref[g], b_ref[g])
            ds = do[:, sl] * u[:, sl]
            dsb = ds.astype(MXU)
            dw, db, dvn = None, None, []
            for k in range(nch):
                rows = slice(k * BLOCK, (k + 1) * BLOCK)
                part = lax.dot_general(dsb[rows], vn[rows], NT, preferred_element_type=F32)
                dw = part if dw is None else dw + part
                rowsum = jnp.sum(ds[rows], axis=-1, keepdims=True)
                db = rowsum if db is None else db + rowsum
                dvn.append(lax.dot_general(w, dsb[rows], TN, preferred_element_type=F32))
            dvn = dvn[0] if nch == 1 else jnp.concatenate(dvn, axis=0)
            dw_ref[g] += jnp.where(tri, dw, 0.0)
            db_ref[g] += db
            dxh = dvn * lng_ref[:, sl]
            dvs.append(r * (dxh - jnp.mean(dxh, axis=-1, keepdims=True)
                            - xh * jnp.mean(dxh * xh, axis=-1, keepdims=True)))
            dlng.append(jnp.sum(dvn * xh, axis=0, keepdims=True))
            dlnb.append(jnp.sum(dvn, axis=0, keepdims=True))
            ss.append(s)
        cat = lambda parts: parts[0] if gpb == 1 else jnp.concatenate(parts, axis=-1)
        dpu_ref[...] = (do * cat(ss) * _gelu_grad(pu)).astype(MXU)
        dpv_ref[...] = (cat(dvs) * _gelu_grad(pv)).astype(MXU)
        dlng_ref[...] += cat(dlng)
        dlnb_ref[...] += cat(dlnb)

    tile = pl.BlockSpec((tr, lb), lambda j, i: (i, j))
    vec = pl.BlockSpec((1, lb), lambda j, i: (0, j))
    half = jax.ShapeDtypeStruct((S, G * BLOCK), MXU)
    return pl.pallas_call(
        body, name="sgu_bwd", grid=(nlb, S // tr), in_specs=specs + [tile, ANY],
        out_specs=[tile, tile, pl.BlockSpec((gpb, BLOCK, BLOCK), lambda j, i: (j, 0, 0)),
                   pl.BlockSpec((gpb, BLOCK, 1), lambda j, i: (j, 0, 0)), vec, vec],
        out_shape=[half, half, jax.ShapeDtypeStruct((G, BLOCK, BLOCK), F32),
                   jax.ShapeDtypeStruct((G, BLOCK, 1), F32),
                   jax.ShapeDtypeStruct((1, G * BLOCK), F32), jax.ShapeDtypeStruct((1, G * BLOCK), F32)],
        compiler_params=_params())(proj, proj, lng, lnb, ws, bs, dsgu, after)


def _store_f32(vals, extra, outs):
    for v, o in zip(vals, outs):
        o[...] = v


def _store_mxu(vals, extra, outs):
    for v, o in zip(vals, outs):
        o[...] = v.astype(MXU)


def _proj_in(h, w):
    S, D = h.shape
    Ns = w.shape[2]
    tm, tn = _tile(S, 1024), _tile(Ns, 1024)
    npb = Ns // tn
    return _mm("proj_in", (S // tm, N_CHIPS, npb), 0, [h, w],
               [pl.BlockSpec((tm, D), lambda i, s, j: (i, 0)), pl.BlockSpec((None, D, tn), lambda i, s, j: (s, 0, j))],
               [(0, 1, 0)], NN, 0, [jax.ShapeDtypeStruct((S, N_CHIPS * Ns), F32)],
               [pl.BlockSpec((tm, tn), lambda i, s, j: (i, s * npb + j))], [None], _store_f32)[0]


def _branches(attn, sgu, wa, ws, proj, gate0):
    S, AW = attn.shape
    SW = sgu.shape[1]
    Nb = wa.shape[2]
    D = N_CHIPS * Nb
    tm = _tile(S, 512)
    assert gate0 % Nb == 0
    ga, gb = gate0 // Nb, (gate0 + D) // Nb

    def epilogue(vals, extra, outs):
        a, b = vals
        outs[0][...] = (_sigmoid(extra[0][...]) * a + _sigmoid(extra[1][...]) * b).astype(MXU)
        outs[1][...] = a
        outs[2][...] = b

    tile = pl.BlockSpec((tm, Nb), lambda i, s: (i, s))
    wspec = lambda k: pl.BlockSpec((None, k, Nb), lambda i, s: (s, 0, 0))
    f = jax.ShapeDtypeStruct((S, D), F32)
    return _mm("branches", (S // tm, N_CHIPS), 0, [attn, sgu, wa, ws, proj, proj],
               [pl.BlockSpec((tm, AW), lambda i, s: (i, 0)), pl.BlockSpec((tm, SW), lambda i, s: (i, 0)),
                wspec(AW), wspec(SW), pl.BlockSpec((tm, Nb), lambda i, s: (i, ga + s)),
                pl.BlockSpec((tm, Nb), lambda i, s: (i, gb + s))],
               [(0, 2, 0), (1, 3, 1)], NN, 2, [jax.ShapeDtypeStruct((S, D), MXU), f, f], [tile] * 3,
               [None, None], epilogue, chunk=MXU_CHUNK)


def _rows_mm(name, a, w, res):
    S = a.shape[0]
    _, K, N = w.shape
    tm, tn = _tile(S, 1024), _tile(N, 1024)

    def epilogue(vals, extra, outs):
        outs[0][...] = extra[0][...] + vals[0]

    out = pl.BlockSpec((tm, tn), lambda i, j, s: (i, j))
    return _mm(name, (S // tm, N // tn, N_CHIPS), 1, [a, w, res],
               [pl.BlockSpec((tm, K), lambda i, j, s: (i, s)), pl.BlockSpec((None, K, tn), lambda i, j, s: (s, 0, j)), out],
               [(0, 1, 0)], NN, 1, [jax.ShapeDtypeStruct((S, N), F32)], [out], [(tm, tn)], epilogue)[0]


def _gate(h2, wg):
    S, D = h2.shape
    Nf = wg.shape[2]
    tm = _tile(S, 256)
    return _mm("gate", (N_CHIPS, S // tm), 0, [h2, wg],
               [pl.BlockSpec((tm, D), lambda s, i: (i, 0)), pl.BlockSpec((None, D, Nf), lambda s, i: (s, 0, 0))],
               [(0, 1, 0)], NN, 0, [jax.ShapeDtypeStruct((S, N_CHIPS * Nf), F32)],
               [pl.BlockSpec((tm, Nf), lambda s, i: (i, s))], [None], _store_f32, chunk=MXU_CHUNK)[0]


def _up_act(h2, wu, g):
    S, D = h2.shape
    Nf = wu.shape[2]
    tm = _tile(S, 256)

    def epilogue(vals, extra, outs):
        u, gv = vals[0], extra[0][...]
        outs[0][...] = u
        outs[1][...] = (gv * _sigmoid(gv) * u).astype(MXU)

    o = pl.BlockSpec((tm, Nf), lambda s, i: (i, s))
    return _mm("up_act", (N_CHIPS, S // tm), 0, [h2, wu, g],
               [pl.BlockSpec((tm, D), lambda s, i: (i, 0)), pl.BlockSpec((None, D, Nf), lambda s, i: (s, 0, 0)), o],
               [(0, 1, 0)], NN, 1, [jax.ShapeDtypeStruct((S, N_CHIPS * Nf), F32),
                                    jax.ShapeDtypeStruct((S, N_CHIPS * Nf), MXU)], [o, o], [None], epilogue,
               chunk=MXU_CHUNK)


def _down_bwd(dyb, wd, g, u):
    S, D = dyb.shape
    Kf = wd.shape[1]
    tm = _tile(S, 512)

    def epilogue(vals, extra, outs):
        da, gv, uv = vals[0], extra[0][...], extra[1][...]
        sg = _sigmoid(gv)
        outs[0][...] = (da * uv * sg * (1.0 + gv * (1.0 - sg))).astype(MXU)
        outs[1][...] = (da * gv * sg).astype(MXU)

    t = pl.BlockSpec((tm, Kf), lambda i, s: (i, s))
    o = jax.ShapeDtypeStruct((S, N_CHIPS * Kf), MXU)
    return _mm("down_bwd", (S // tm, N_CHIPS), 0, [dyb, wd, g, u],
               [pl.BlockSpec((tm, D), lambda i, s: (i, 0)), pl.BlockSpec((None, Kf, D), lambda i, s: (s, 0, 0)), t, t],
               [(0, 1, 0)], NT, 2, [o, o], [t, t], [None], epilogue, chunk=MXU_CHUNK)


def _out_bwd(dxb, wo, proj, ba, bb, gate0):
    S, D = dxb.shape
    Ko = wo.shape[1]
    tm = _tile(S, 512)
    assert gate0 % Ko == 0
    ga, gb = gate0 // Ko, (gate0 + D) // Ko

    def epilogue(vals, extra, outs):
        dm = vals[0]
        sa, sb = _sigmoid(extra[0][...]), _sigmoid(extra[1][...])
        outs[0][...] = (dm * sa).astype(MXU)
        outs[1][...] = (dm * sb).astype(MXU)
        outs[2][...] = (dm * extra[2][...] * sa * (1.0 - sa)).astype(MXU)
        outs[3][...] = (dm * extra[3][...] * sb * (1.0 - sb)).astype(MXU)

    t = pl.BlockSpec((tm, Ko), lambda i, s: (i, s))
    o = jax.ShapeDtypeStruct((S, D), MXU)
    return _mm("out_bwd", (S // tm, N_CHIPS), 0, [dxb, wo, proj, proj, ba, bb],
               [pl.BlockSpec((tm, D), lambda i, s: (i, 0)), pl.BlockSpec((None, Ko, D), lambda i, s: (s, 0, 0)),
                pl.BlockSpec((tm, Ko), lambda i, s: (i, ga + s)), pl.BlockSpec((tm, Ko), lambda i, s: (i, gb + s)), t, t],
               [(0, 1, 0)], NT, 4, [o] * 4, [t] * 4, [None], epilogue, chunk=MXU_CHUNK)


def _dx_cols(name, terms, n_out, after=None):
    S = terms[0][0].shape[0]
    _, K, Ns = terms[0][1].shape
    tm, tko, tn = _tile(S, 1024), _tile(K, 1024), _tile(Ns, 1920 if len(terms) == 1 else 1408)
    npb = Ns // tn
    operands, specs, pairs = [], [], []
    for t, (dy, w, k) in enumerate(terms):
        assert w.shape == (N_CHIPS, K, Ns)
        operands += [dy, w]
        specs += [pl.BlockSpec((tm, tn), lambda i, jk, s, jn: (i, s * npb + jn)),
                  pl.BlockSpec((None, tko, tn), lambda i, jk, s, jn: (s, jk, jn))]
        pairs.append((2 * t, 2 * t + 1, k))
    out = pl.BlockSpec((tm, tko), lambda i, jk, s, jn: (i, jk))
    return _mm(name, (S // tm, K // tko, N_CHIPS, npb), 2, operands, specs, pairs, NT, 0,
               [jax.ShapeDtypeStruct((S, K), F32)] * n_out, [out] * n_out, [(tm, tko)] * n_out, _store_f32, after)


def _dw_cols(name, a, dy):
    S, K = a.shape
    Ns = dy.shape[1] // N_CHIPS
    tk, tn = _tile(K, 512), _tile(Ns, 1408)
    npb = Ns // tn
    return _mm(name, (K // tk, N_CHIPS, npb), 0, [a, dy],
               [pl.BlockSpec((S, tk), lambda jk, s, jn: (0, jk)), pl.BlockSpec((S, tn), lambda jk, s, jn: (0, s * npb + jn))],
               [(0, 1, 0)], TN, 0, [jax.ShapeDtypeStruct((N_CHIPS, K, Ns), MXU)],
               [pl.BlockSpec((None, tk, tn), lambda jk, s, jn: (s, jk, jn))], [None], _store_mxu)[0]


def _dw_rows(name, a, dy):
    S = a.shape[0]
    K = a.shape[1] // N_CHIPS
    N = dy.shape[1]
    tk, tn = _tile(K, 1408), _tile(N, 1024)
    nkb = K // tk
    return _mm(name, (N_CHIPS, nkb, N // tn), 0, [a, dy],
               [pl.BlockSpec((S, tk), lambda s, jk, jn: (0, s * nkb + jk)), pl.BlockSpec((S, tn), lambda s, jk, jn: (0, jn))],
               [(0, 1, 0)], TN, 0, [jax.ShapeDtypeStruct((N_CHIPS, K, N), MXU)],
               [pl.BlockSpec((None, tk, tn), lambda s, jk, jn: (s, jk, jn))], [None], _store_mxu)[0]


def _layer_fwd(x, stream, layer, after, sp, cos, sin, dims):
    AW, KW, gate0, u_col = dims
    h = _rms_fwd("mix_norm", x, sp["mix_norm"], after)
    stream.forward(layer, 0, h)
    w = stream.finish(layer, 0, h)
    proj = _proj_in(h, w["w_in"])
    qr, kr, vb = _qk_prep(proj, sp["q_norm"], sp["k_norm"], cos, sin, AW, KW)
    attn = _attn_fwd(qr, kr, vb, sp["sinks"])
    stream.forward(layer, 1, attn)
    sgu = _sgu_fwd(proj, sp["sgu_ln_g"], sp["sgu_ln_b"], sp["w_spatial"], sp["b_spatial"], u_col)
    w.update(stream.finish(layer, 1, sgu))
    merged, ba, bb = _branches(attn, sgu, w["w_attn_branch"], w["w_sgu_branch"], proj, gate0)
    stream.forward(layer, 2, merged)
    w.update(stream.finish(layer, 2, merged))
    x1 = _rows_mm("out_proj", merged, w["w_out"], x)
    h2 = _rms_fwd("ffn_norm", x1, sp["ffn_norm"], x1)
    stream.forward(layer, 3, h2)
    w.update(stream.finish(layer, 3, h2))
    g = _gate(h2, w["w_gate"])
    stream.forward(layer, 4, g)
    w.update(stream.finish(layer, 4, g))
    u, act = _up_act(h2, w["w_up"], g)
    stream.forward(layer, 5, u)
    w.update(stream.finish(layer, 5, u))
    x2 = _rows_mm("down_proj", act, w["w_down"], x1)
    saved = dict(x=x, h=h, proj=proj, qr=qr, kr=kr, vb=vb, attn=attn, sgu=sgu, merged=merged, ba=ba, bb=bb,
                 x1=x1, h2=h2, g=g, u=u, act=act)
    return x2, saved, w


def _layer_bwd(dy, dyb, w, sp, sv, cos, sin, dims, reducer, layer):
    AW, KW, gate0, u_col = dims
    big, small = {}, {}
    dg, du = _down_bwd(dyb, w["w_down"], sv["g"], sv["u"])
    big["w_down"] = _dw_rows("dw_down", sv["act"], dyb)
    big["w_gate"] = _dw_cols("dw_gate", sv["h2"], dg)
    big["w_up"] = _dw_cols("dw_up", sv["h2"], du)
    token = reducer.start(layer, 2, big)
    dh2 = _dx_cols("dh2", [(dg, w["w_gate"], 0), (du, w["w_up"], 0)], 1, token)[0]
    token = reducer.scatter(layer, 2, dh2)
    dx1, dx1b, small["ffn_norm"] = _rms_bwd("ffn_norm_bwd", dh2, sv["x1"], sp["ffn_norm"], dy, token)
    dba, dbb, dgla, dglb = _out_bwd(dx1b, w["w_out"], sv["proj"], sv["ba"], sv["bb"], gate0)
    big["w_out"] = _dw_rows("dw_out", sv["merged"], dx1b)
    big["w_attn_branch"] = _dw_cols("dw_attn_branch", sv["attn"], dba)
    big["w_sgu_branch"] = _dw_cols("dw_sgu_branch", sv["sgu"], dbb)
    token = reducer.start(layer, 1, big)
    dattn, dsgu = _dx_cols("dbranch_in", [(dba, w["w_attn_branch"], 0), (dbb, w["w_sgu_branch"], 1)], 2, token)
    token = reducer.scatter(layer, 1, dsgu)
    dpu, dpv, small["w_spatial"], db, small["sgu_ln_g"], small["sgu_ln_b"] = _sgu_bwd(
        sv["proj"], sp["sgu_ln_g"], sp["sgu_ln_b"], sp["w_spatial"], sp["b_spatial"], dsgu, u_col, token)
    small["b_spatial"] = db[:, :, 0]
    dq, dkp, dkc, dvp, dvc, dsink = _attn_bwd(sv["qr"], sv["kr"], sv["vb"], sp["sinks"], dattn)
    small["sinks"] = dsink[:, :sp["sinks"].shape[1]]
    dqkv, dqg, dkg = _qk_prep_bwd(sv["proj"], sp["q_norm"], sp["k_norm"], cos, sin, dq, dkp, dkc, dvp, dvc, AW, KW)
    small["q_norm"] = dqg[:, :HEAD_DIM]
    small["k_norm"] = dkg[:, :HEAD_DIM]
    dproj = _join_columns("dproj", [dqkv, dpu, dpv, dgla, dglb])
    big["w_in"] = _dw_cols("dw_in", sv["h"], dproj)
    token = reducer.start(layer, 0, big)
    dh = _dx_cols("dh", [(dproj, w["w_in"], 0)], 1, token)[0]
    token = reducer.scatter(layer, 0, dh)
    dx, dxb, small["mix_norm"] = _rms_bwd("mix_norm_bwd", dh, sv["x"], sp["mix_norm"], dx1, token)
    return dx, dxb, small


def _place():
    x, y, c = lax.axis_index("x"), lax.axis_index("y"), lax.axis_index("c")
    chips = [(1 - x, y), (x, 1 - y), (1 - x, 1 - y)]
    return x, y, c, chips


def _half_rows(c, rows):
    h = rows // 2
    assert h % 16 == 0
    return pl.ds(pl.multiple_of(c * h, 16), h)


def _row_tile(rows, pref):
    best = None
    for t in range(16, min(rows, pref) + 1, 16):
        if rows % t == 0:
            best = t
    assert best is not None, rows
    return best


def _cast_own(name, chip, w, layer):
    _, R, C = w.shape
    tr = _row_tile(R, 512)

    def body(chip_ref, w_ref, o_ref):
        o_ref[...] = w_ref[...].astype(MXU)

    return pl.pallas_call(
        body, name=name, out_shape=jax.ShapeDtypeStruct((N_CHIPS, R, C), MXU),
        grid_spec=pltpu.PrefetchScalarGridSpec(
            num_scalar_prefetch=1, grid=(R // tr,),
            in_specs=[pl.BlockSpec((None, tr, C), lambda i, chip_ref: (layer, i, 0))],
            out_specs=pl.BlockSpec((None, tr, C), lambda i, chip_ref: (chip_ref[0], i, 0))),
        compiler_params=_params())(chip, w)


HBM = pl.BlockSpec(memory_space=pltpu.HBM)
SEM = pl.BlockSpec(memory_space=pltpu.SEMAPHORE)
DATAFLOW = pltpu.SideEffectType.DATAFLOW_SIDE_EFFECTING


def _gather_copies(bufs, send_sem, recv_sem):
    x, y, c, chips = _place()

    def ici(a, j, block):
        px, py = chips[j]
        blk = bufs[a].at[block, _half_rows(c, bufs[a].shape[1])]
        return pltpu.make_async_remote_copy(
            src_ref=blk, dst_ref=blk, send_sem=send_sem.at[3 * a + j], recv_sem=recv_sem.at[3 * a + j],
            device_id=(px, py, c), device_id_type=MESH)

    def d2d(a, j, core):
        px, py = chips[j]
        blk = bufs[a].at[2 * px + py, _half_rows(core, bufs[a].shape[1])]
        return pltpu.make_async_remote_copy(
            src_ref=blk, dst_ref=blk, send_sem=send_sem.at[3 * a + j], recv_sem=recv_sem.at[3 * a + j],
            device_id=(x, y, 1 - c), device_id_type=MESH)

    return ici, d2d


def _in_hbm(bufs):
    return [pltpu.with_memory_space_constraint(b, pltpu.HBM) for b in bufs]


def _gather_start(name, bufs, after):
    n = len(bufs)

    def body(*refs):
        dst = refs[n + 1:2 * n + 1]
        send_sem, recv_sem, token = refs[2 * n + 1:]
        x, y, c, chips = _place()
        ici, _ = _gather_copies(dst, send_sem, recv_sem)
        for a in range(n):
            for j in range(3):
                ici(a, j, 2 * x + y).start()
        token[...] = jnp.zeros(token.shape, token.dtype)

    sems = pltpu.SemaphoreType.DMA((3 * n,))
    outs = pl.pallas_call(
        body, name=name, in_specs=[HBM] * n + [ANY],
        out_specs=[HBM] * n + [SEM, SEM, pl.BlockSpec(memory_space=pltpu.VMEM)],
        out_shape=[pltpu.HBM(b.shape, b.dtype) for b in bufs] + [sems, sems, jax.ShapeDtypeStruct((8, BLOCK), F32)],
        input_output_aliases={a: a for a in range(n)},
        compiler_params=pltpu.CompilerParams(has_side_effects=DATAFLOW))(*_in_hbm(bufs), after)
    return outs[:n], outs[n], outs[n + 1], outs[n + 2]


def _gather_forward(name, bufs, ici_send, ici_recv, after):
    n = len(bufs)

    def body(*refs):
        ici_send_ref, ici_recv_ref = refs[n], refs[n + 1]
        dst = refs[n + 3:2 * n + 3]
        d2d_send, d2d_recv = refs[2 * n + 3:]
        x, y, c, chips = _place()
        ici, _ = _gather_copies(dst, ici_send_ref, ici_recv_ref)
        _, d2d = _gather_copies(dst, d2d_send, d2d_recv)
        for a in range(n):
            for j, (px, py) in enumerate(chips):
                ici(a, j, 2 * px + py).wait_recv()
                d2d(a, j, c).start()
        for a in range(n):
            for j in range(3):
                ici(a, j, 2 * x + y).wait_send()

    sems = pltpu.SemaphoreType.DMA((3 * n,))
    outs = pl.pallas_call(
        body, name=name, in_specs=[HBM] * n + [SEM, SEM, ANY], out_specs=[HBM] * n + [SEM, SEM],
        out_shape=[pltpu.HBM(b.shape, b.dtype) for b in bufs] + [sems, sems],
        input_output_aliases={a: a for a in range(n)},
        compiler_params=pltpu.CompilerParams(has_side_effects=DATAFLOW))(*bufs, ici_send, ici_recv, after)
    return outs[:n], outs[n], outs[n + 1]


def _gather_finish(name, bufs, d2d_send, d2d_recv, after):
    n = len(bufs)

    def body(*refs):
        send_ref, recv_ref = refs[n], refs[n + 1]
        dst = refs[n + 3:]
        x, y, c, chips = _place()
        _, d2d = _gather_copies(dst, send_ref, recv_ref)
        for a in range(n):
            for j in range(3):
                d2d(a, j, 1 - c).wait_recv()
                d2d(a, j, c).wait_send()

    return pl.pallas_call(
        body, name=name, in_specs=[HBM] * n + [SEM, SEM, ANY], out_specs=[HBM] * n,
        out_shape=[pltpu.HBM(b.shape, b.dtype) for b in bufs],
        input_output_aliases={a: a for a in range(n)},
        compiler_params=pltpu.CompilerParams(has_side_effects=DATAFLOW))(*bufs, d2d_send, d2d_recv, after)


GATHER = (("w_in",), ("w_attn_branch", "w_sgu_branch"), ("w_out",), ("w_gate",), ("w_up",), ("w_down",))
REDUCE = (("w_in",), ("w_attn_branch", "w_sgu_branch", "w_out"), ("w_gate", "w_up", "w_down"))


class _WeightStream:
    def __init__(self, started):
        self.started, self.passed = started, {}

    def forward(self, layer, group, after):
        bufs, send, recv = self.started[(layer, group)]
        self.passed[(layer, group)] = _gather_forward("gather_forward_%d_%d" % (layer, group), bufs, send, recv, after)

    def finish(self, layer, group, after):
        bufs, send, recv = self.passed[(layer, group)]
        done = _gather_finish("gather_finish_%d_%d" % (layer, group), bufs, send, recv, after)
        return dict(zip(GATHER[group], done))


def _pair_copies(grads, lands, send_sem, recv_sem):
    x, y, c, _ = _place()

    def make(a):
        theirs = _half_rows(1 - c, grads[a].shape[1])
        return pltpu.make_async_remote_copy(
            src_ref=grads[a].at[:, theirs], dst_ref=lands[a], send_sem=send_sem.at[a], recv_sem=recv_sem.at[a],
            device_id=(x, y, 1 - c), device_id_type=MESH)

    return make


def _pair_start(name, grads, after):
    n = len(grads)
    lands = [lax.empty((g.shape[0], g.shape[1] // 2, g.shape[2]), g.dtype) for g in grads]

    def body(*refs):
        src, dst = refs[2 * n + 1:3 * n + 1], refs[3 * n + 1:4 * n + 1]
        send_sem, recv_sem, token = refs[4 * n + 1:]
        copy = _pair_copies(src, dst, send_sem, recv_sem)
        for a in range(n):
            copy(a).start()
        token[...] = jnp.zeros(token.shape, token.dtype)

    sems = pltpu.SemaphoreType.DMA((n,))
    outs = pl.pallas_call(
        body, name=name, in_specs=[HBM] * (2 * n) + [ANY],
        out_specs=[HBM] * (2 * n) + [SEM, SEM, pl.BlockSpec(memory_space=pltpu.VMEM)],
        out_shape=[pltpu.HBM(b.shape, b.dtype) for b in grads + lands] + [sems, sems, jax.ShapeDtypeStruct((8, BLOCK), F32)],
        input_output_aliases={a: a for a in range(2 * n)},
        compiler_params=pltpu.CompilerParams(has_side_effects=DATAFLOW))(*_in_hbm(grads + lands), after)
    return outs[:n], outs[n:2 * n], outs[2 * n], outs[2 * n + 1], outs[2 * n + 2]


def _pair_finish(name, grads, lands, send_sem, recv_sem, after):
    n = len(grads)

    def body(*refs):
        send_ref, recv_ref = refs[2 * n], refs[2 * n + 1]
        src, dst = refs[2 * n + 3:3 * n + 3], refs[3 * n + 3:]
        copy = _pair_copies(src, dst, send_ref, recv_ref)
        for a in range(n):
            copy(a).wait_send()
            copy(a).wait_recv()

    outs = pl.pallas_call(
        body, name=name, in_specs=[HBM] * (2 * n) + [SEM, SEM, ANY], out_specs=[HBM] * (2 * n),
        out_shape=[pltpu.HBM(b.shape, b.dtype) for b in grads + lands],
        input_output_aliases={a: a for a in range(2 * n)},
        compiler_params=pltpu.CompilerParams(has_side_effects=DATAFLOW))(*grads, *lands, send_sem, recv_sem, after)
    return outs[:n], outs[n:]


def _pair_sum(name, core, g, p):
    _, h, C = p.shape
    tr = _row_tile(h, 512)
    nrb = h // tr

    def body(core_ref, g_ref, p_ref, o_ref):
        o_ref[...] = (g_ref[...].astype(F32) + p_ref[...].astype(F32)).astype(o_ref.dtype)

    spec = pl.BlockSpec((None, tr, C), lambda s, i, core_ref: (s, i, 0))
    return pl.pallas_call(
        body, name=name, out_shape=jax.ShapeDtypeStruct(p.shape, p.dtype),
        grid_spec=pltpu.PrefetchScalarGridSpec(
            num_scalar_prefetch=1, grid=(N_CHIPS, nrb),
            in_specs=[pl.BlockSpec((None, tr, C), lambda s, i, core_ref: (s, core_ref[0] * nrb + i, 0)), spec],
            out_specs=spec),
        compiler_params=_params())(core, g, p)


def _scatter_copies(sums, slots, send_sem, recv_sem):
    x, y, c, chips = _place()

    def make(a, j):
        px, py = chips[j]
        return pltpu.make_async_remote_copy(
            src_ref=sums[a].at[2 * px + py], dst_ref=slots[a].at[j], send_sem=send_sem.at[3 * a + j],
            recv_sem=recv_sem.at[3 * a + j], device_id=(px, py, c), device_id_type=MESH)

    return make


def _scatter_start(name, sums, after):
    n = len(sums)
    slots = [lax.empty((3,) + s.shape[1:], s.dtype) for s in sums]

    def body(*refs):
        src, dst = refs[2 * n + 1:3 * n + 1], refs[3 * n + 1:4 * n + 1]
        send_sem, recv_sem, token = refs[4 * n + 1:]
        copy = _scatter_copies(src, dst, send_sem, recv_sem)
        for a in range(n):
            for j in range(3):
                copy(a, j).start()
        token[...] = jnp.zeros(token.shape, token.dtype)

    sems = pltpu.SemaphoreType.DMA((3 * n,))
    outs = pl.pallas_call(
        body, name=name, in_specs=[HBM] * (2 * n) + [ANY],
        out_specs=[HBM] * (2 * n) + [SEM, SEM, pl.BlockSpec(memory_space=pltpu.VMEM)],
        out_shape=[pltpu.HBM(b.shape, b.dtype) for b in sums + slots] + [sems, sems, jax.ShapeDtypeStruct((8, BLOCK), F32)],
        input_output_aliases={a: a for a in range(2 * n)},
        compiler_params=pltpu.CompilerParams(has_side_effects=DATAFLOW))(*_in_hbm(sums + slots), after)
    return outs[:n], outs[n:2 * n], outs[2 * n], outs[2 * n + 1], outs[2 * n + 2]


def _scatter_finish(name, sums, slots, send_sem, recv_sem, after):
    n = len(sums)

    def body(*refs):
        send_ref, recv_ref = refs[2 * n], refs[2 * n + 1]
        src, dst = refs[2 * n + 3:3 * n + 3], refs[3 * n + 3:]
        copy = _scatter_copies(src, dst, send_ref, recv_ref)
        for a in range(n):
            for j in range(3):
                copy(a, j).wait_send()
                copy(a, j).wait_recv()

    outs = pl.pallas_call(
        body, name=name, in_specs=[HBM] * (2 * n) + [SEM, SEM, ANY], out_specs=[HBM] * (2 * n),
        out_shape=[pltpu.HBM(b.shape, b.dtype) for b in sums + slots],
        input_output_aliases={a: a for a in range(2 * n)},
        compiler_params=pltpu.CompilerParams(has_side_effects=DATAFLOW))(*sums, *slots, send_sem, recv_sem, after)
    return outs[:n], outs[n:]


def _slot_sum(name, place, slots, sums):
    _, h, C = slots.shape
    tr = _row_tile(h, 512)
    nrb = h // tr

    def body(place_ref, r0, r1, r2, own, o_ref):
        o_ref[...] = ((r0[...].astype(F32) + r1[...].astype(F32)) + r2[...].astype(F32)) + own[...].astype(F32)

    slot = lambda k: pl.BlockSpec((None, tr, C), lambda i, place_ref: (k, i, 0))
    return pl.pallas_call(
        body, name=name, out_shape=jax.ShapeDtypeStruct((2 * h, C), F32),
        grid_spec=pltpu.PrefetchScalarGridSpec(
            num_scalar_prefetch=1, grid=(nrb,),
            in_specs=[slot(0), slot(1), slot(2),
                      pl.BlockSpec((None, tr, C), lambda i, place_ref: (place_ref[0], i, 0))],
            out_specs=pl.BlockSpec((tr, C), lambda i, place_ref: (place_ref[1] * nrb + i, 0))),
        compiler_params=_params())(place, slots, slots, slots, sums)


def _half_copies(bufs, send_sem, recv_sem):
    x, y, c, _ = _place()

    def make(a, core):
        rows = bufs[a].at[_half_rows(core, bufs[a].shape[0])]
        return pltpu.make_async_remote_copy(
            src_ref=rows, dst_ref=rows, send_sem=send_sem.at[a], recv_sem=recv_sem.at[a],
            device_id=(x, y, 1 - c), device_id_type=MESH)

    return make


def _half_start(name, bufs, after):
    n = len(bufs)

    def body(*refs):
        dst = refs[n + 1:2 * n + 1]
        send_sem, recv_sem, token = refs[2 * n + 1:]
        c = lax.axis_index("c")
        copy = _half_copies(dst, send_sem, recv_sem)
        for a in range(n):
            copy(a, c).start()
        token[...] = jnp.zeros(token.shape, token.dtype)

    sems = pltpu.SemaphoreType.DMA((n,))
    outs = pl.pallas_call(
        body, name=name, in_specs=[HBM] * n + [ANY],
        out_specs=[HBM] * n + [SEM, SEM, pl.BlockSpec(memory_space=pltpu.VMEM)],
        out_shape=[pltpu.HBM(b.shape, b.dtype) for b in bufs] + [sems, sems, jax.ShapeDtypeStruct((8, BLOCK), F32)],
        input_output_aliases={a: a for a in range(n)},
        compiler_params=pltpu.CompilerParams(has_side_effects=DATAFLOW))(*_in_hbm(bufs), after)
    return outs[:n], outs[n], outs[n + 1], outs[n + 2]


def _half_finish(name, bufs, send_sem, recv_sem, after):
    n = len(bufs)

    def body(*refs):
        send_ref, recv_ref = refs[n], refs[n + 1]
        dst = refs[n + 3:]
        c = lax.axis_index("c")
        copy = _half_copies(dst, send_ref, recv_ref)
        for a in range(n):
            copy(a, c).wait_send()
            copy(a, 1 - c).wait_recv()

    return pl.pallas_call(
        body, name=name, in_specs=[HBM] * n + [SEM, SEM, ANY], out_specs=[HBM] * n,
        out_shape=[pltpu.HBM(b.shape, b.dtype) for b in bufs],
        input_output_aliases={a: a for a in range(n)},
        compiler_params=pltpu.CompilerParams(has_side_effects=DATAFLOW))(*bufs, send_sem, recv_sem, after)


class _GradReducer:
    def __init__(self, chip, core):
        self.core, self.place, self.pairs, self.started = core, jnp.concatenate([chip, core]), {}, []

    def start(self, layer, group, grads):
        mine = [grads[n] for n in REDUCE[group]]
        mine, lands, send, recv, token = _pair_start("grad_pair_start_%d_%d" % (layer, group), mine, self.place)
        self.pairs[(layer, group)] = (mine, lands, send, recv)
        return token

    def scatter(self, layer, group, after):
        tag = "%d_%d" % (layer, group)
        names = REDUCE[group]
        mine, lands, send, recv = self.pairs.pop((layer, group))
        mine, theirs = _pair_finish("grad_pair_finish_" + tag, mine, lands, send, recv, after)
        sums = [_pair_sum("pair_sum_%s_%d" % (n, layer), self.core, g, p) for n, g, p in zip(names, mine, theirs)]
        sums, slots, send, recv, token = _scatter_start("grad_scatter_start_" + tag, sums, self.place)
        self.started.append((layer, names, sums, slots, send, recv))
        return token

    def finish(self, after, update):
        for layer in sorted({entry[0] for entry in self.started}, reverse=True):
            exchanged = []
            for lyr, names, sums, slots, send, recv in self.started:
                if lyr != layer:
                    continue
                tag = "%s_%d" % (names[0], layer)
                sums, slots = _scatter_finish("grad_scatter_finish_" + tag, sums, slots, send, recv, after)
                halves = [_slot_sum("slot_sum_%s_%d" % (n, layer), self.place, r, s)
                          for n, r, s in zip(names, slots, sums)]
                halves, send, recv, after = _half_start("grad_half_start_" + tag, halves, self.place)
                exchanged.append((tag, names, halves, send, recv))
            for tag, names, halves, send, recv in exchanged:
                whole = _half_finish("grad_half_finish_" + tag, halves, send, recv, after)
                after = update(layer, dict(zip(names, whole)))


def _small_copies(arrays, lands, own_sems, pass_sems):
    def slot(a, block):
        px, py, pc = block
        return lands[a].at[4 * px + 2 * py + pc]

    def own(a, k, block, to):
        return pltpu.make_async_remote_copy(
            src_ref=arrays[a], dst_ref=slot(a, block), send_sem=own_sems[0].at[4 * a + k],
            recv_sem=own_sems[1].at[4 * a + k], device_id=to, device_id_type=MESH)

    def passed(a, j, block, to):
        return pltpu.make_async_remote_copy(
            src_ref=slot(a, block), dst_ref=slot(a, block), send_sem=pass_sems[0].at[3 * a + j],
            recv_sem=pass_sems[1].at[3 * a + j], device_id=to, device_id_type=MESH)

    return own, passed


def _small_start(arrays, after):
    n = len(arrays)
    lands = [lax.empty((2 * N_CHIPS,) + a.shape, a.dtype) for a in arrays]

    def body(*refs):
        src, dst = refs[2 * n + 1:3 * n + 1], refs[3 * n + 1:4 * n + 1]
        send_sem, recv_sem, token = refs[4 * n + 1:]
        x, y, c, chips = _place()
        own, _ = _small_copies(src, dst, (send_sem, recv_sem), None)
        for a in range(n):
            own(a, 0, (x, y, c), (x, y, 1 - c)).start()
            for j, (px, py) in enumerate(chips):
                own(a, 1 + j, (x, y, c), (px, py, c)).start()
        token[...] = jnp.zeros(token.shape, token.dtype)

    sems = pltpu.SemaphoreType.DMA((4 * n,))
    outs = pl.pallas_call(
        body, name="small_grad_start", in_specs=[HBM] * (2 * n) + [ANY],
        out_specs=[HBM] * (2 * n) + [SEM, SEM, pl.BlockSpec(memory_space=pltpu.VMEM)],
        out_shape=[pltpu.HBM(b.shape, b.dtype) for b in arrays + lands]
        + [sems, sems, jax.ShapeDtypeStruct((8, BLOCK), F32)],
        input_output_aliases={a: a for a in range(2 * n)},
        compiler_params=pltpu.CompilerParams(has_side_effects=DATAFLOW))(*_in_hbm(arrays + lands), after)
    return outs[:n], outs[n:2 * n], (outs[2 * n], outs[2 * n + 1]), outs[2 * n + 2]


def _small_forward(arrays, lands, own_sems, after):
    n = len(arrays)

    def body(*refs):
        own_refs = (refs[2 * n], refs[2 * n + 1])
        src, dst = refs[2 * n + 3:3 * n + 3], refs[3 * n + 3:4 * n + 3]
        pass_refs = (refs[4 * n + 3], refs[4 * n + 4])
        x, y, c, chips = _place()
        own, passed = _small_copies(src, dst, own_refs, pass_refs)
        for a in range(n):
            for j, (px, py) in enumerate(chips):
                own(a, 1 + j, (px, py, c), (x, y, c)).wait_recv()
                passed(a, j, (px, py, c), (x, y, 1 - c)).start()
        for a in range(n):
            own(a, 0, (x, y, 1 - c), (x, y, c)).wait_recv()
            own(a, 0, (x, y, c), (x, y, 1 - c)).wait_send()
            for j, (px, py) in enumerate(chips):
                own(a, 1 + j, (x, y, c), (px, py, c)).wait_send()

    sems = pltpu.SemaphoreType.DMA((3 * n,))
    outs = pl.pallas_call(
        body, name="small_grad_forward", in_specs=[HBM] * (2 * n) + [SEM, SEM, ANY],
        out_specs=[HBM] * (2 * n) + [SEM, SEM],
        out_shape=[pltpu.HBM(b.shape, b.dtype) for b in arrays + lands] + [sems, sems],
        input_output_aliases={a: a for a in range(2 * n)},
        compiler_params=pltpu.CompilerParams(has_side_effects=DATAFLOW))(*arrays, *lands, *own_sems, after)
    return outs[:n], outs[n:2 * n], (outs[2 * n], outs[2 * n + 1])


def _small_finish(arrays, lands, pass_sems, after):
    n = len(arrays)

    def body(*refs):
        pass_refs = (refs[2 * n], refs[2 * n + 1])
        src, dst = refs[2 * n + 3:3 * n + 3], refs[3 * n + 3:]
        x, y, c, chips = _place()
        _, passed = _small_copies(src, dst, None, pass_refs)
        for a in range(n):
            for j, (px, py) in enumerate(chips):
                passed(a, j, (px, py, 1 - c), (x, y, c)).wait_recv()
                passed(a, j, (px, py, c), (x, y, 1 - c)).wait_send()

    outs = pl.pallas_call(
        body, name="small_grad_finish", in_specs=[HBM] * (2 * n) + [SEM, SEM, ANY], out_specs=[HBM] * (2 * n),
        out_shape=[pltpu.HBM(b.shape, b.dtype) for b in arrays + lands],
        input_output_aliases={a: a for a in range(2 * n)},
        compiler_params=pltpu.CompilerParams(has_side_effects=DATAFLOW))(*arrays, *lands, *pass_sems, after)
    return outs[:n], outs[n:]


def _small_sum(place, arrays, lands):
    n = len(arrays)
    n_dev = 2 * N_CHIPS

    def body(place_ref, *refs):
        me = 2 * place_ref[0] + place_ref[1]
        for a in range(n):
            own, land, out = refs[a], refs[n + a], refs[2 * n + a]
            acc = None
            for d in range(n_dev):
                term = jnp.where(me == d, own[...], land[jnp.where(me == d, d ^ 1, d)])
                acc = term if acc is None else acc + term
            out[...] = acc

    vm = pl.BlockSpec(memory_space=pltpu.VMEM)
    return pl.pallas_call(
        body, name="small_grad_sum", in_specs=[pl.BlockSpec(memory_space=pltpu.SMEM)] + [vm] * (2 * n),
        out_specs=[vm] * n, out_shape=[jax.ShapeDtypeStruct(a.shape, F32) for a in arrays],
        compiler_params=_params())(place, *arrays, *lands)


def _adamw_math(w, g, m, v):
    m2 = ADAM_B1 * m + (1.0 - ADAM_B1) * g
    v2 = ADAM_B2 * v + (1.0 - ADAM_B2) * (g * g)
    m_hat = m2 / (1.0 - ADAM_B1 ** ADAM_STEP)
    v_hat = v2 / (1.0 - ADAM_B2 ** ADAM_STEP)
    delta = -ADAM_LR * (m_hat / (jnp.sqrt(v_hat) + ADAM_EPS) + ADAM_WD * w)
    return delta, m2, v2


def _adamw_big(name, layer, grad, w, m, v, others):
    L, R, C = w.shape
    tr = _row_tile(R, 256)

    def body(g_ref, w_ref, m_ref, v_ref, *rest):
        go_ref, d_ref, mo_ref, vo_ref = rest[-4:]
        g = g_ref[...]
        delta, m2, v2 = _adamw_math(w_ref[...], g, m_ref[...], v_ref[...])
        go_ref[...] = g
        d_ref[...] = delta
        mo_ref[...] = m2
        vo_ref[...] = v2

    blk = pl.BlockSpec((None, tr, C), lambda i: (layer, i, 0))
    shp = jax.ShapeDtypeStruct(w.shape, F32)
    others = [] if others is None else list(others)
    return pl.pallas_call(
        body, name=name, grid=(R // tr,),
        in_specs=[pl.BlockSpec((tr, C), lambda i: (i, 0))] + [blk] * 3 + [ANY] * len(others), out_specs=[blk] * 4,
        out_shape=[shp] * 4, input_output_aliases={4 + k: k for k in range(len(others))},
        compiler_params=_params())(grad, w, m, v, *others)


def _adamw_small(gs, ws, ms, vs):
    n = len(gs)

    def body(*refs):
        for a in range(n):
            g_ref, w_ref, m_ref, v_ref = refs[a], refs[n + a], refs[2 * n + a], refs[3 * n + a]
            delta, m2, v2 = _adamw_math(w_ref[...], g_ref[...], m_ref[...], v_ref[...])
            refs[4 * n + a][...] = delta
            refs[5 * n + a][...] = m2
            refs[6 * n + a][...] = v2

    vm = pl.BlockSpec(memory_space=pltpu.VMEM)
    shapes = [jax.ShapeDtypeStruct(g.shape, F32) for g in gs]
    outs = pl.pallas_call(
        body, name="adamw_small", in_specs=[vm] * (4 * n), out_specs=[vm] * (3 * n), out_shape=shapes * 3,
        compiler_params=_params())(*gs, *ws, *ms, *vs)
    return outs[:n], outs[n:2 * n], outs[2 * n:]


def _rows2d(a):
    return a if a.ndim == 2 else a.reshape(-1, a.shape[-1])


BIG = ("w_in", "w_attn_branch", "w_sgu_branch", "w_out", "w_gate", "w_up", "w_down")
SMALL = ("mix_norm", "q_norm", "k_norm", "sinks", "sgu_ln_g", "sgu_ln_b", "w_spatial", "b_spatial", "ffn_norm")
ORDER = ("mix_norm", "w_in", "q_norm", "k_norm", "sinks", "sgu_ln_g", "sgu_ln_b", "w_spatial", "b_spatial",
         "w_attn_branch", "w_sgu_branch", "w_out", "ffn_norm", "w_gate", "w_up", "w_down")


def _rope_tables(seq):
    pos = jnp.arange(seq, dtype=F32)
    inv_freq = jnp.power(10000.0, -jnp.arange(0, HEAD_DIM, 2, dtype=F32) / HEAD_DIM)
    ang = pos[:, None] * inv_freq[None, :]
    cos, sin = jnp.cos(ang), jnp.sin(ang)
    reps = BLOCK // HEAD_DIM
    return (jnp.tile(jnp.concatenate([cos, cos], axis=1), (1, reps)),
            jnp.tile(jnp.concatenate([-sin, sin], axis=1), (1, reps)))


def kernel(x, mix_norm, w_in, q_norm, k_norm, sinks, sgu_ln_g, sgu_ln_b, w_spatial, b_spatial, w_attn_branch, w_sgu_branch, w_out, ffn_norm, w_gate, w_up, w_down, loss_target, m_mix_norm, m_w_in, m_q_norm, m_k_norm, m_sinks, m_sgu_ln_g, m_sgu_ln_b, m_w_spatial, m_b_spatial, m_w_attn_branch, m_w_sgu_branch, m_w_out, m_ffn_norm, m_w_gate, m_w_up, m_w_down, v_mix_norm, v_w_in, v_q_norm, v_k_norm, v_sinks, v_sgu_ln_g, v_sgu_ln_b, v_w_spatial, v_b_spatial, v_w_attn_branch, v_w_sgu_branch, v_w_out, v_ffn_norm, v_w_gate, v_w_up, v_w_down):
    weights = dict(mix_norm=mix_norm, w_in=w_in, q_norm=q_norm, k_norm=k_norm, sinks=sinks, sgu_ln_g=sgu_ln_g,
                   sgu_ln_b=sgu_ln_b, w_spatial=w_spatial, b_spatial=b_spatial, w_attn_branch=w_attn_branch,
                   w_sgu_branch=w_sgu_branch, w_out=w_out, ffn_norm=ffn_norm, w_gate=w_gate, w_up=w_up, w_down=w_down)
    mom1 = dict(mix_norm=m_mix_norm, w_in=m_w_in, q_norm=m_q_norm, k_norm=m_k_norm, sinks=m_sinks,
                sgu_ln_g=m_sgu_ln_g, sgu_ln_b=m_sgu_ln_b, w_spatial=m_w_spatial, b_spatial=m_b_spatial,
                w_attn_branch=m_w_attn_branch, w_sgu_branch=m_w_sgu_branch, w_out=m_w_out, ffn_norm=m_ffn_norm,
                w_gate=m_w_gate, w_up=m_w_up, w_down=m_w_down)
    mom2 = dict(mix_norm=v_mix_norm, w_in=v_w_in, q_norm=v_q_norm, k_norm=v_k_norm, sinks=v_sinks,
                sgu_ln_g=v_sgu_ln_g, sgu_ln_b=v_sgu_ln_b, w_spatial=v_w_spatial, b_spatial=v_b_spatial,
                w_attn_branch=v_w_attn_branch, w_sgu_branch=v_w_sgu_branch, w_out=v_w_out, ffn_norm=v_ffn_norm,
                w_gate=v_w_gate, w_up=v_w_up, w_down=v_w_down)
    xs, target = x[0], loss_target[0]
    S, D = xs.shape
    L = w_in.shape[0]
    AW, KW, SW = N_Q_HEADS * HEAD_DIM, N_KV_HEADS * HEAD_DIM, SGU_GROUPS * BLOCK
    dims = (AW, KW, AW + 2 * KW + 2 * SW, AW + 2 * KW)
    cos, sin = _rope_tables(S)
    reps = BLOCK // HEAD_DIM

    chip = (2 * lax.axis_index("x") + lax.axis_index("y")).astype(jnp.int32).reshape(1)
    core = lax.axis_index("c").astype(jnp.int32).reshape(1)
    started, token = {}, chip
    for l in range(L):
        for gi, names in enumerate(GATHER):
            bufs = [_cast_own("cast_%s_%d" % (n, l), chip, weights[n], l) for n in names]
            bufs, send, recv, token = _gather_start("gather_start_%d_%d" % (l, gi), bufs, token)
            started[(l, gi)] = (bufs, send, recv)
    stream = _WeightStream(started)
    sp = [dict(mix_norm=mix_norm[l][None], ffn_norm=ffn_norm[l][None], q_norm=jnp.tile(q_norm[l][None], (1, reps)),
               k_norm=jnp.tile(k_norm[l][None], (1, reps)), sinks=sinks[l][None], sgu_ln_g=sgu_ln_g[l][None],
               sgu_ln_b=sgu_ln_b[l][None], w_spatial=w_spatial[l], b_spatial=b_spatial[l][:, :, None])
          for l in range(L)]

    act, saved, wl = xs, [], []
    for l in range(L):
        act, sv, w_all = _layer_fwd(act, stream, l, token if l == 0 else act, sp[l], cos, sin, dims)
        saved.append(sv)
        wl.append(w_all)
    loss_part, dy, dyb = _loss_head(act, target)
    loss = lax.psum(loss_part[0, 0], ("x", "y", "c"))

    reducer = _GradReducer(chip, core)
    small_g = [None] * L
    for l in reversed(range(L)):
        dy, dyb, small_g[l] = _layer_bwd(dy, dyb, wl[l], sp[l], saved[l], cos, sin, dims, reducer, l)
    grad_x = dy[None]

    local = [_rows2d(jnp.stack([small_g[l][n].reshape(weights[n].shape[1:]) for l in range(L)])) for n in SMALL]
    small = list(_small_start(local, dy))
    updated = {}

    def update(layer, reduced):
        for n, g in reduced.items():
            updated[n] = _adamw_big("adamw_%s_%d" % (n, layer), layer, g, weights[n], mom1[n], mom2[n],
                                    updated.get(n))
        if len(small) == 4:
            arrays, lands, own_sems, _ = small
            small[:] = _small_forward(arrays, lands, own_sems, updated[n][0])
        return updated[n][0]

    reducer.finish(small[3], update)
    grads, deltas, new_m, new_v = {}, {}, {}, {}
    for n in BIG:
        grads[n], deltas[n], new_m[n], new_v[n] = updated[n]

    arrays, lands = _small_finish(*small, updated[BIG[0]][0])
    g_small = _small_sum(jnp.concatenate([chip, core]), arrays, lands)
    d_small, m_small, v_small = _adamw_small(g_small, [_rows2d(weights[n]) for n in SMALL],
                                             [_rows2d(mom1[n]) for n in SMALL], [_rows2d(mom2[n]) for n in SMALL])
    for n, g, d, m2, v2 in zip(SMALL, g_small, d_small, m_small, v_small):
        shape = weights[n].shape
        grads[n], deltas[n], new_m[n], new_v[n] = g.reshape(shape), d.reshape(shape), m2.reshape(shape), v2.reshape(shape)

    return (loss, grad_x, *[grads[n] for n in ORDER], *[deltas[n] for n in ORDER],
            *[new_m[n] for n in ORDER], *[new_v[n] for n in ORDER])
```
